```python
import math
import jax
import jax.numpy as jnp
from jax import lax
import numpy as np

D_MODEL = 1024
BATCH = 8
SEQ = 2048
DEPTH = 2

MEM_LEN = 256
EPS = 1e-6

HY_C = D_MODEL // 2
HY_ORDER = 2
HY_SHORT = 3
FILT_BANDS = 16
FILT_EMB = 2 * FILT_BANDS + 1
FILT_HIDDEN = 64
DECAY_TARGET = 1e-2
FAST_DECAY_PCT = 0.3
SLOW_DECAY_PCT = 1.5
MOD_SHIFT = 0.05

AT_GROUPS = ((128, 1), (512, 4), (2048, 16))
AT_HEADS = 4
AT_HD = D_MODEL // 8
AT_W = len(AT_GROUPS) * AT_HEADS * AT_HD
AT_OUT = AT_HEADS * AT_HD
NUM_BUCKETS = 32
REL_MAX_DIST = 1024
NEG_INF = -1e30

XA_HEADS = 4
XA_HD = D_MODEL // 8
XA_W = XA_HEADS * XA_HD

D_FF = 2816
N_EXPERTS = 8
TOP_K = 2
MOE_BLOCK = 256

kernel_name = 'hybrid_hyena_dilated_moe_encoder'


def rmsnorm(x, g):
    xf = x.astype(jnp.float32)
    y = xf * lax.rsqrt(jnp.mean(xf * xf, axis=-1, keepdims=True) + EPS)
    return (y * g.astype(jnp.float32)).astype(x.dtype)


def rel_bucket(rel):
    half = NUM_BUCKETS // 2
    exact = half // 2
    n = np.abs(rel)
    large = exact + (np.log(np.maximum(n, 1) / exact) / np.log(REL_MAX_DIST / exact) * (half - exact)).astype(np.int32)
    large = np.minimum(large, half - 1)
    return (np.where(rel > 0, half, 0) + np.where(n < exact, n, large)).astype(np.int32)


def implicit_filters(L, w1, b1, w2, b2, w3, freq):
    t = jnp.linspace(0.0, 1.0, L, dtype=jnp.float32)[:, None]
    f = jnp.linspace(1e-4, FILT_BANDS - 1, FILT_BANDS, dtype=jnp.float32)[None]
    ang = (2.0 * math.pi / L) * jnp.arange(L, dtype=jnp.float32)[:, None] * f
    feats = jnp.concatenate([t, jnp.cos(ang), -jnp.sin(ang)], axis=-1)
    fr = freq.astype(jnp.float32)
    h = jnp.sin(fr * (feats @ w1.astype(jnp.float32) + b1.astype(jnp.float32)))
    h = jnp.sin(fr * (h @ w2.astype(jnp.float32) + b2.astype(jnp.float32)))
    h = (h @ w3.astype(jnp.float32)).reshape(L, HY_ORDER, 2, HY_C)
    deltas = jnp.abs(jnp.linspace(math.log(DECAY_TARGET) / SLOW_DECAY_PCT,
                                  math.log(DECAY_TARGET) / FAST_DECAY_PCT, HY_C, dtype=jnp.float32))
    h = h * (jnp.exp(-t[:, :, None, None] * deltas) + MOD_SHIFT)
    fwd = h[:, :, 0]
    bwd = h[1:, :, 1][::-1]
    filt = jnp.concatenate([fwd, jnp.zeros((1, HY_ORDER, HY_C), jnp.float32), bwd], axis=0)
    filt = filt / (jnp.sum(jnp.abs(filt), axis=0, keepdims=True) + 1e-6)
    return jnp.fft.rfft(filt, axis=0)


def fftconv(u, filt_f, skip):
    L = u.shape[1]
    uf = jnp.fft.rfft(u.astype(jnp.float32), n=2 * L, axis=1)
    y = jnp.fft.irfft(uf * filt_f[None], n=2 * L, axis=1)[:, :L]
    return y + u.astype(jnp.float32) * skip.astype(jnp.float32)


def hyena(u, conv_w, conv_b, filt_f, skip):
    pad = HY_SHORT // 2
    up = jnp.pad(u, ((0, 0), (pad, pad), (0, 0)))
    S = u.shape[1]
    uc = conv_b.astype(jnp.float32) + sum(up[:, j:j + S].astype(jnp.float32) * conv_w[j].astype(jnp.float32)
                                          for j in range(HY_SHORT))
    chunks = jnp.split(uc, HY_ORDER + 1, axis=-1)
    z = chunks[0]
    for o in range(HY_ORDER):
        z = chunks[o + 1] * fftconv(z, filt_f[:, o], skip[o])
    return z


def head_rms(t, g):
    tf = t.astype(jnp.float32)
    return tf * lax.rsqrt(jnp.mean(tf * tf, axis=-1, keepdims=True) + EPS) * g.astype(jnp.float32)


def dilated_group(q, k, v, window, dil, bias_table):
    B, S, H, HD = q.shape
    band = window // (2 * dil)
    L = S // dil
    nb = -(-L // band)
    Lp = nb * band

    def to_sub(t):
        return t.reshape(B, L, dil, H, HD).transpose(0, 2, 3, 1, 4)

    def key_blocks(t):
        tp = jnp.pad(to_sub(t), ((0, 0), (0, 0), (0, 0), (band, Lp - L + band), (0, 0)))
        tp = tp.reshape(B, dil, H, nb + 2, band, HD)
        return jnp.concatenate([tp[:, :, :, :-2], tp[:, :, :, 1:-1], tp[:, :, :, 2:]], axis=4)

    qb = jnp.pad(to_sub(q), ((0, 0), (0, 0), (0, 0), (0, Lp - L), (0, 0))).reshape(B, dil, H, nb, band, HD)
    kb = key_blocks(k)
    vb = key_blocks(v)
    rel = np.arange(3 * band)[None, :] - band - np.arange(band)[:, None]
    key_idx = np.arange(nb)[:, None] * band + np.arange(3 * band)[None, :] - band
    mask = (np.abs(rel) <= band)[None] & ((key_idx >= 0) & (key_idx < L))[:, None, :]
    bias = jnp.moveaxis(bias_table[rel_bucket(rel * dil)], -1, 0).astype(jnp.float32)
    s = jnp.einsum('brhnqd,brhnkd->brhnqk', qb, kb, preferred_element_type=jnp.float32) * (HD ** -0.5)
    s = jnp.where(mask, s + bias[None, None, :, None], NEG_INF)
    m = jnp.max(s, axis=-1, keepdims=True)
    p = jnp.exp(s - m)
    z = jnp.sum(p, axis=-1, keepdims=True)
    o = jnp.einsum('brhnqk,brhnkd->brhnqd', p, vb.astype(jnp.float32)) / z
    lse = (m + jnp.log(z))[..., 0]
    o = o.reshape(B, dil, H, Lp, HD)[:, :, :, :L].transpose(0, 3, 1, 2, 4).reshape(B, S, H, HD)
    lse = lse.reshape(B, dil, H, Lp)[..., :L].transpose(0, 3, 1, 2).reshape(B, S, H)
    return o, lse


def dilated_attention(qkv, q_gain, k_gain, rel_bias):
    B, S, _ = qkv.shape
    t = qkv.reshape(B, S, 3, len(AT_GROUPS), AT_HEADS, AT_HD)
    q = head_rms(t[:, :, 0], q_gain).astype(qkv.dtype)
    k = head_rms(t[:, :, 1], k_gain).astype(qkv.dtype)
    v = t[:, :, 2]
    outs, lses = [], []
    for g, (window, dil) in enumerate(AT_GROUPS):
        o, lse = dilated_group(q[:, :, g], k[:, :, g], v[:, :, g], window, dil,
                               rel_bias[:, g * AT_HEADS:(g + 1) * AT_HEADS])
        outs.append(o)
        lses.append(lse)
    w = jax.nn.softmax(jnp.stack(lses), axis=0)
    o = jnp.sum(w[..., None] * jnp.stack(outs), axis=0)
    return o.reshape(B, S, AT_OUT)


def cross_attention(xq, m, w_kv, q_gain, k_gain):
    B, S, _ = xq.shape
    M = m.shape[1]
    q = head_rms(xq.reshape(B, S, XA_HEADS, XA_HD), q_gain)
    kv = (m @ w_kv).reshape(B, M, 2, XA_HEADS, XA_HD)
    k = head_rms(kv[:, :, 0], k_gain)
    v = kv[:, :, 1].astype(jnp.float32)
    s = jnp.einsum('bshd,bmhd->bhsm', q, k) * (XA_HD ** -0.5)
    p = jax.nn.softmax(s, axis=-1)
    return jnp.einsum('bhsm,bmhd->bshd', p, v).reshape(B, S, XA_W)


def swiglu(h, wg, wu, wd):
    return (jax.nn.silu(h @ wg) * (h @ wu)) @ wd


def moe_swiglu(h, router, wg, wu, wd):
    B, S, D = h.shape
    T = B * S
    A = T * TOP_K
    hf = h.reshape(T, D)
    logits = jnp.dot(hf, router, preferred_element_type=jnp.float32)
    top_v, top_i = lax.top_k(logits, TOP_K)
    gates = jax.nn.softmax(top_v, axis=-1).reshape(A)
    expert = top_i.reshape(A)
    token = jnp.repeat(jnp.arange(T, dtype=jnp.int32), TOP_K)
    order = jnp.argsort(expert)
    e_s, tok_s, g_s = expert[order], token[order], gates[order]
    counts = jnp.bincount(expert, length=N_EXPERTS)
    padded = (counts + MOE_BLOCK - 1) // MOE_BLOCK * MOE_BLOCK
    start = jnp.cumsum(counts) - counts
    pend = jnp.cumsum(padded)
    pstart = pend - padded
    dest = pstart[e_s] + jnp.arange(A, dtype=jnp.int32) - start[e_s]
    P = A + N_EXPERTS * MOE_BLOCK
    nblk = P // MOE_BLOCK
    buf_tok = jnp.zeros((P,), jnp.int32).at[dest].set(tok_s)
    buf_gate = jnp.zeros((P,), jnp.float32).at[dest].set(g_s)
    blk_expert = jnp.minimum(jnp.searchsorted(pend, jnp.arange(nblk, dtype=jnp.int32) * MOE_BLOCK, side='right'),
                             N_EXPERTS - 1)
    xb = hf[buf_tok].reshape(nblk, MOE_BLOCK, D)

    def expert_block(args):
        xe, e = args
        return (jax.nn.silu(xe @ wg[e]) * (xe @ wu[e])) @ wd[e]

    yb = lax.map(expert_block, (xb, blk_expert)).reshape(P, D)
    out = jnp.zeros((T, D), jnp.float32).at[buf_tok].add(yb.astype(jnp.float32) * buf_gate[:, None])
    return out.reshape(B, S, D).astype(h.dtype)


def setup_inputs(seed: int = 0) -> dict:
    key = jax.random.key(seed)
    ks = iter(jax.random.split(key, 32))
    NHY = (DEPTH + 1) // 2
    NAT = DEPTH // 2

    def nrm(shape, scale):
        return jax.random.normal(next(ks), shape, jnp.float32) * scale

    def gain(shape):
        return 1.0 + nrm(shape, 0.02)

    return {
        'x': nrm((BATCH, SEQ, D_MODEL), 1.0),
        'mem': nrm((BATCH, MEM_LEN, D_MODEL), 1.0),
        'rel_bias': nrm((NUM_BUCKETS, len(AT_GROUPS) * AT_HEADS), 0.5),
        'norm_mix': gain((DEPTH, D_MODEL)),
        'norm_mem': gain((DEPTH, D_MODEL)),
        'norm_ffn': gain((DEPTH, D_MODEL)),
        'w_mem_kv': nrm((DEPTH, D_MODEL, 2 * XA_W), D_MODEL ** -0.5),
        'xq_norm': gain((DEPTH, XA_HD)),
        'xk_norm': gain((DEPTH, XA_HD)),
        'w_out': nrm((DEPTH, D_MODEL, D_MODEL), D_MODEL ** -0.5),
        'hy_w_in': nrm((NHY, D_MODEL, (HY_ORDER + 1) * HY_C + XA_W), D_MODEL ** -0.5),
        'hy_conv_w': nrm((NHY, HY_SHORT, (HY_ORDER + 1) * HY_C), HY_SHORT ** -0.5),
        'hy_conv_b': nrm((NHY, (HY_ORDER + 1) * HY_C), 0.02),
        'hy_filt_w1': nrm((NHY, FILT_EMB, FILT_HIDDEN), FILT_EMB ** -0.5),
        'hy_filt_b1': nrm((NHY, FILT_HIDDEN), 0.1),
        'hy_filt_w2': nrm((NHY, FILT_HIDDEN, FILT_HIDDEN), FILT_HIDDEN ** -0.5),
        'hy_filt_b2': nrm((NHY, FILT_HIDDEN), 0.1),
        'hy_filt_w3': nrm((NHY, FILT_HIDDEN, HY_ORDER * 2 * HY_C), FILT_HIDDEN ** -0.5),
        'hy_sin_freq': 1.0 + nrm((NHY, FILT_HIDDEN), 0.1),
        'hy_skip': nrm((NHY, HY_ORDER, HY_C), 1.0),
        'at_w_in': nrm((NAT, D_MODEL, 3 * AT_W + XA_W), D_MODEL ** -0.5),
        'at_q_norm': gain((NAT, AT_HD)),
        'at_k_norm': gain((NAT, AT_HD)),
        'ffn_w_gate': nrm((NHY, D_MODEL, D_FF), D_MODEL ** -0.5),
        'ffn_w_up': nrm((NHY, D_MODEL, D_FF), D_MODEL ** -0.5),
        'ffn_w_down': nrm((NHY, D_FF, D_MODEL), D_FF ** -0.5),
        'moe_router': nrm((NAT, D_MODEL, N_EXPERTS), D_MODEL ** -0.5),
        'moe_w_gate': nrm((NAT, N_EXPERTS, D_MODEL, D_FF), D_MODEL ** -0.5),
        'moe_w_up': nrm((NAT, N_EXPERTS, D_MODEL, D_FF), D_MODEL ** -0.5),
        'moe_w_down': nrm((NAT, N_EXPERTS, D_FF, D_MODEL), D_FF ** -0.5),
    }


def reference(x, mem, rel_bias, norm_mix, norm_mem, norm_ffn, w_mem_kv, xq_norm, xk_norm, w_out,
              hy_w_in, hy_conv_w, hy_conv_b, hy_filt_w1, hy_filt_b1, hy_filt_w2, hy_filt_b2, hy_filt_w3,
              hy_sin_freq, hy_skip, at_w_in, at_q_norm, at_k_norm, ffn_w_gate, ffn_w_up, ffn_w_down,
              moe_router, moe_w_gate, moe_w_up, moe_w_down):
    S = x.shape[1]
    for i in range(DEPTH):
        j = i // 2
        h = rmsnorm(x, norm_mix[i])
        m = rmsnorm(mem, norm_mem[i])
        if i % 2 == 0:
            z = h @ hy_w_in[j]
            filt_f = implicit_filters(S, hy_filt_w1[j], hy_filt_b1[j], hy_filt_w2[j], hy_filt_b2[j],
                                      hy_filt_w3[j], hy_sin_freq[j])
            self_out = hyena(z[..., :(HY_ORDER + 1) * HY_C], hy_conv_w[j], hy_conv_b[j], filt_f, hy_skip[j])
            xq = z[..., (HY_ORDER + 1) * HY_C:]
        else:
            z = h @ at_w_in[j]
            self_out = dilated_attention(z[..., :3 * AT_W], at_q_norm[j], at_k_norm[j], rel_bias)
            xq = z[..., 3 * AT_W:]
        cross = cross_attention(xq, m, w_mem_kv[i], xq_norm[i], xk_norm[i])
        mixed = jnp.concatenate([self_out, cross], axis=-1).astype(x.dtype)
        x = x + mixed @ w_out[i]
        hf = rmsnorm(x, norm_ffn[i])
        if i % 2 == 0:
            x = x + swiglu(hf, ffn_w_gate[j], ffn_w_up[j], ffn_w_down[j])
        else:
            x = x + moe_swiglu(hf, moe_router[j], moe_w_gate[j], moe_w_up[j], moe_w_down[j])
    return x
```

```python
import functools
import math

import jax
import jax.numpy as jnp
import numpy as np
from jax import lax
from jax.experimental import pallas as pl
from jax.experimental.pallas import tpu as pltpu

F32 = jnp.float32
BF16 = jnp.bfloat16

D_MODEL = 1024
EPS = 1e-6
HY_C = 512
FILT_BANDS = 16
DECAY_TARGET = 1e-2
FAST_DECAY_PCT = 0.3
SLOW_DECAY_PCT = 1.5
MOD_SHIFT = 0.05
AT_GROUPS = ((128, 1), (512, 4), (2048, 16))
AT_HEADS = 4
HD = 128
AT_W = 1536
NUM_BUCKETS = 32
REL_MAX_DIST = 1024
NEG_INF = -1e30
XA_W = 512
D_FF = 2816
N_EXPERTS = 8
TOP_K = 2

VMEM_LIMIT_BYTES = 56 * 1024 * 1024
ROW_TILE = 512
FF_TILE = 1408
MOE_ROWS = 512
ATT_TQ = 128
ATT_KW = 256
HY_TC = 256
HY_KT = 512


def _params(*sem):
    return pltpu.CompilerParams(dimension_semantics=sem, vmem_limit_bytes=VMEM_LIMIT_BYTES)


def _rms_rows(x, gain):
    return x * lax.rsqrt(jnp.mean(x * x, axis=-1, keepdims=True) + EPS) * gain


def _resident(shape, index_map):
    return pl.BlockSpec(shape, index_map, pipeline_mode=pl.Buffered(1))


def _norm_matmul_kernel(x_ref, g_ref, w_ref, o_ref, h_ref):
    @pl.when(pl.program_id(1) == 0)
    def _():
        h_ref[...] = _rms_rows(x_ref[...], g_ref[...]).astype(BF16)

    o_ref[...] = jnp.dot(h_ref[...], w_ref[...], preferred_element_type=F32)


def _norm_matmul(x2, gain, w_bf, tn):
    rows, d = x2.shape
    n = w_bf.shape[1]
    tm = min(ROW_TILE, rows)
    return pl.pallas_call(
        _norm_matmul_kernel,
        out_shape=jax.ShapeDtypeStruct((rows, n), F32),
        grid=(rows // tm, n // tn),
        in_specs=[pl.BlockSpec((tm, d), lambda i, j: (i, 0)),
                  pl.BlockSpec((1, d), lambda i, j: (0, 0)),
                  pl.BlockSpec((d, tn), lambda i, j: (0, j))],
        out_specs=pl.BlockSpec((tm, tn), lambda i, j: (i, j)),
        scratch_shapes=[pltpu.VMEM((tm, d), BF16)],
        compiler_params=_params("parallel", "arbitrary"),
    )(x2, gain.reshape(1, d), w_bf)


def _outproj_kernel(x_ref, a_ref, c_ref, wa_ref, wc_ref, o_ref):
    o_ref[...] = (x_ref[...]
                  + jnp.dot(a_ref[...].astype(BF16), wa_ref[...], preferred_element_type=F32)
                  + jnp.dot(c_ref[...].astype(BF16), wc_ref[...], preferred_element_type=F32))


def _outproj(x2, self_out, cross, w_bf):
    rows, d = x2.shape
    half = self_out.shape[1]
    tm = ROW_TILE
    return pl.pallas_call(
        _outproj_kernel,
        out_shape=jax.ShapeDtypeStruct((rows, d), F32),
        grid=(rows // tm,),
        in_specs=[pl.BlockSpec((tm, d), lambda i: (i, 0)),
                  pl.BlockSpec((tm, half), lambda i: (i, 0)),
                  pl.BlockSpec((tm, half), lambda i: (i, 0)),
                  pl.BlockSpec((half, d), lambda i: (0, 0)),
                  pl.BlockSpec((half, d), lambda i: (1, 0))],
        out_specs=pl.BlockSpec((tm, d), lambda i: (i, 0)),
        compiler_params=_params("parallel"),
    )(x2, self_out, cross, w_bf, w_bf)


def _ffn_kernel(eid_ref, nused_ref, x_ref, g_ref, wg_ref, wu_ref, wd_ref, gate_ref, o_ref, h_ref, acc_ref,
                *, residual):
    i, j = pl.program_id(0), pl.program_id(1)
    active = i * x_ref.shape[0] < nused_ref[0]

    @pl.when(j == 0)
    def _():
        h_ref[...] = _rms_rows(x_ref[...], g_ref[...]).astype(BF16)
        acc_ref[...] = jnp.zeros_like(acc_ref)

    @pl.when(active)
    def _():
        h = h_ref[...]
        gg = jnp.dot(h, wg_ref[0], preferred_element_type=F32)
        uu = jnp.dot(h, wu_ref[0], preferred_element_type=F32)
        a = (gg * jax.nn.sigmoid(gg)) * uu
        acc_ref[...] += jnp.dot(a.astype(BF16), wd_ref[0], preferred_element_type=F32)

    @pl.when(j == pl.num_programs(1) - 1)
    def _():
        if residual:
            o_ref[...] = x_ref[...] + acc_ref[...]
        else:
            o_ref[...] = acc_ref[...] * gate_ref[...]


def _ffn(x2, gain, wg, wu, wd, eid, nused, gate, *, tm, residual):
    rows, d = x2.shape
    f = wg.shape[2]
    tf = FF_TILE
    grid_spec = pltpu.PrefetchScalarGridSpec(
        num_scalar_prefetch=2,
        grid=(rows // tm, f // tf),
        in_specs=[pl.BlockSpec((tm, d), lambda i, j, e, n: (i, 0)),
                  pl.BlockSpec((1, d), lambda i, j, e, n: (0, 0)),
                  pl.BlockSpec((1, d, tf), lambda i, j, e, n: (e[i], 0, j)),
                  pl.BlockSpec((1, d, tf), lambda i, j, e, n: (e[i], 0, j)),
                  pl.BlockSpec((1, tf, d), lambda i, j, e, n: (e[i], j, 0)),
                  pl.BlockSpec((tm, 1), lambda i, j, e, n: (i, 0))],
        out_specs=pl.BlockSpec((tm, d), lambda i, j, e, n: (i, 0)),
        scratch_shapes=[pltpu.VMEM((tm, d), BF16), pltpu.VMEM((tm, d), F32)],
    )
    return pl.pallas_call(
        functools.partial(_ffn_kernel, residual=residual),
        out_shape=jax.ShapeDtypeStruct((rows, d), F32),
        grid_spec=grid_spec,
        compiler_params=_params("parallel", "arbitrary"),
    )(eid, nused, x2, gain.reshape(1, d), wg, wu, wd, gate)


def _xattn_kernel(xq_ref, kv_ref, qg_ref, kg_ref, o_ref):
    scale = HD ** -0.5
    for h in range(XA_W // HD):
        cols = slice(h * HD, (h + 1) * HD)
        q = _rms_rows(xq_ref[0, :, cols], qg_ref[...]).astype(BF16)
        k = _rms_rows(kv_ref[0, :, cols], kg_ref[...]).astype(BF16)
        v = kv_ref[0, :, XA_W + h * HD:XA_W + (h + 1) * HD].astype(BF16)
        s = lax.dot_general(q, k, (((1,), (1,)), ((), ())), preferred_element_type=F32) * scale
        p = jnp.exp(s - jnp.max(s, axis=-1, keepdims=True))
        z = jnp.sum(p, axis=-1, keepdims=True)
        o_ref[0, :, cols] = jnp.dot(p.astype(BF16), v, preferred_element_type=F32) / z


def _xattn(z3, xq_block, kv3, q_gain, k_gain):
    b, s, _ = z3.shape
    m = kv3.shape[1]
    ts = ROW_TILE
    return pl.pallas_call(
        _xattn_kernel,
        out_shape=jax.ShapeDtypeStruct((b, s, XA_W), F32),
        grid=(b, s // ts),
        in_specs=[pl.BlockSpec((1, ts, XA_W), lambda i, j: (i, j, xq_block)),
                  pl.BlockSpec((1, m, 2 * XA_W), lambda i, j: (i, 0, 0)),
                  pl.BlockSpec((1, HD), lambda i, j: (0, 0)),
                  pl.BlockSpec((1, HD), lambda i, j: (0, 0))],
        out_specs=pl.BlockSpec((1, ts, XA_W), lambda i, j: (i, j, 0)),
        compiler_params=_params("parallel", "parallel"),
    )(z3, kv3, q_gain.reshape(1, HD), k_gain.reshape(1, HD))


def _dft_kernel(c_ref, s_ref, *, n_fft):
    rows, cols = c_ref.shape
    k = pl.program_id(0) * rows + lax.broadcasted_iota(jnp.int32, (rows, cols), 0)
    n = lax.broadcasted_iota(jnp.int32, (rows, cols), 1)
    ang = ((k * n) & (n_fft - 1)).astype(F32) * (2.0 * math.pi / n_fft)
    c_ref[...] = jnp.cos(ang).astype(BF16)
    s_ref[...] = jnp.sin(ang).astype(BF16)


def _dft_tables(length):
    rows = 256
    shape = jax.ShapeDtypeStruct((length, length), BF16)
    spec = pl.BlockSpec((rows, length), lambda i: (i, 0))
    return pl.pallas_call(
        functools.partial(_dft_kernel, n_fft=2 * length),
        out_shape=(shape, shape),
        grid=(length // rows,),
        out_specs=(spec, spec),
        compiler_params=_params("parallel"),
    )()


def _filter_time_kernel(feats_ref, w1_ref, b1_ref, w2_ref, b2_ref, fr_ref, w3_ref, t_ref, delta_ref,
                        gp_ref, gm_ref, hn_ref):
    hp = lax.Precision.HIGHEST
    length = feats_ref.shape[0]
    fr = fr_ref[...]
    h = jnp.sin(fr * (jnp.dot(feats_ref[...], w1_ref[...], preferred_element_type=F32, precision=hp) + b1_ref[...]))
    h = jnp.sin(fr * (jnp.dot(h, w2_ref[...], preferred_element_type=F32, precision=hp) + b2_ref[...]))
    mod = jnp.exp(-t_ref[...] * delta_ref[...]) + MOD_SHIFT
    row = lax.broadcasted_iota(jnp.int32, (length, 1), 0)
    alt = jnp.where(row % 2 == 0, 1.0, -1.0).astype(F32)
    for o in range(2):
        fwd = jnp.dot(h, w3_ref[2 * o], preferred_element_type=F32, precision=hp) * mod
        bwd = jnp.dot(h, w3_ref[2 * o + 1], preferred_element_type=F32, precision=hp) * mod
        bwd = jnp.where(row == 0, 0.0, bwd)
        norm = (jnp.sum(jnp.abs(fwd), axis=0, keepdims=True)
                + jnp.sum(jnp.abs(bwd), axis=0, keepdims=True) + 1e-6)
        fwd = fwd / norm
        bwd = bwd / norm
        gp = fwd + bwd
        gp_ref[o] = gp.astype(BF16)
        gm_ref[o] = (fwd - bwd).astype(BF16)
        hn_ref[o] = jnp.sum(gp * alt, axis=0, keepdims=True) * (1.0 / (2 * length))


def _filter_freq_kernel(c_ref, s_ref, gp_ref, gm_ref, h_ref):
    length = c_ref.shape[0]
    row = lax.broadcasted_iota(jnp.int32, (length, 1), 0)
    scale = jnp.where(row == 0, 1.0, 2.0).astype(F32) * (1.0 / (2 * length))
    h_ref[0, 0] = jnp.dot(c_ref[...], gp_ref[0], preferred_element_type=F32) * scale
    h_ref[0, 1] = -jnp.dot(s_ref[...], gm_ref[0], preferred_element_type=F32) * scale


def _hyena_filters(length, cmat, smat, w1, b1, w2, b2, w3, freq):
    t = jnp.linspace(0.0, 1.0, length, dtype=F32)[:, None]
    f = jnp.linspace(1e-4, FILT_BANDS - 1, FILT_BANDS, dtype=F32)[None]
    ang = (2.0 * math.pi / length) * jnp.arange(length, dtype=F32)[:, None] * f
    feats = jnp.concatenate([t, jnp.cos(ang), -jnp.sin(ang)], axis=-1)
    deltas = jnp.abs(jnp.linspace(math.log(DECAY_TARGET) / SLOW_DECAY_PCT,
                                  math.log(DECAY_TARGET) / FAST_DECAY_PCT, HY_C, dtype=F32))[None]
    hid = w1.shape[1]
    w3r = w3.reshape(hid, 4, HY_C).transpose(1, 0, 2)
    tc = HY_TC
    full = lambda shape: pl.BlockSpec(shape, lambda c: (0,) * len(shape))
    gp, gm, hn = pl.pallas_call(
        _filter_time_kernel,
        out_shape=(jax.ShapeDtypeStruct((2, length, HY_C), BF16),
                   jax.ShapeDtypeStruct((2, length, HY_C), BF16),
                   jax.ShapeDtypeStruct((2, 1, HY_C), F32)),
        grid=(HY_C // tc,),
        in_specs=[full(feats.shape), full(w1.shape), full((1, hid)), full(w2.shape), full((1, hid)),
                  full((1, hid)), pl.BlockSpec((4, hid, tc), lambda c: (0, 0, c)), full((length, 1)),
                  pl.BlockSpec((1, tc), lambda c: (0, c))],
        out_specs=(pl.BlockSpec((2, length, tc), lambda c: (0, 0, c)),
                   pl.BlockSpec((2, length, tc), lambda c: (0, 0, c)),
                   pl.BlockSpec((2, 1, tc), lambda c: (0, 0, c))),
        compiler_params=_params("parallel"),
    )(feats, w1, b1.reshape(1, hid), w2, b2.reshape(1, hid), freq.reshape(1, hid), w3r, t, deltas)
    spectra = pl.pallas_call(
        _filter_freq_kernel,
        out_shape=jax.ShapeDtypeStruct((2, 2, length, HY_C), F32),
        grid=(2, HY_C // tc),
        in_specs=[_resident((length, length), lambda o, c: (0, 0)),
                  _resident((length, length), lambda o, c: (0, 0)),
                  pl.BlockSpec((1, length, tc), lambda o, c: (o, 0, c)),
                  pl.BlockSpec((1, length, tc), lambda o, c: (o, 0, c))],
        out_specs=pl.BlockSpec((1, 2, length, tc), lambda o, c: (o, 0, 0, c)),
        compiler_params=_params("parallel", "parallel"),
    )(cmat, smat, gp, gm)
    return spectra, hn


def _hyena_kernel(z0_ref, z1_ref, z2_ref, cw_ref, cb_ref, c_ref, s_ref, h_ref, hn_ref, skip_ref, o_ref,
                  ubf_ref, y_ref):
    length = z0_ref.shape[1]
    row = lax.broadcasted_iota(jnp.int32, (length, 1), 0)
    alt = jnp.where(row % 2 == 0, 1.0, -1.0).astype(F32)

    def short_conv(z_ref, c):
        u = z_ref[0]
        prev = jnp.where(row == 0, 0.0, pltpu.roll(u, 1, axis=0))
        nxt = jnp.where(row == length - 1, 0.0, pltpu.roll(u, length - 1, axis=0))
        return cb_ref[c] + (prev * cw_ref[0, c] + u * cw_ref[1, c] + nxt * cw_ref[2, c])

    def long_conv(u, o):
        ubf_ref[...] = u.astype(BF16)
        nyq = jnp.sum(u * alt, axis=0, keepdims=True) * hn_ref[o]
        y_ref[...] = alt * nyq + u * skip_ref[o]
        for kt in range(length // HY_KT):
            ks = slice(kt * HY_KT, (kt + 1) * HY_KT)
            ub = ubf_ref[...]
            a = jnp.dot(c_ref[ks, :], ub, preferred_element_type=F32)
            b = jnp.dot(s_ref[ks, :], ub, preferred_element_type=F32)
            hr = h_ref[o, 0, ks, :]
            hi = h_ref[o, 1, ks, :]
            re = (a * hr + b * hi).astype(BF16)
            im = (a * hi - b * hr).astype(BF16)
            y_ref[...] += (jnp.dot(c_ref[:, ks], re, preferred_element_type=F32)
                           - jnp.dot(s_ref[:, ks], im, preferred_element_type=F32))
        return y_ref[...]

    z = short_conv(z1_ref, 1) * long_conv(short_conv(z0_ref, 0), 0)
    o_ref[0] = short_conv(z2_ref, 2) * long_conv(z, 1)


def _hyena(z3, conv_w, conv_b, cmat, smat, spectra, hn, skip):
    b, s, _ = z3.shape
    tc = HY_TC
    nct = HY_C // tc
    cw = conv_w.reshape(3, 3, 1, HY_C)
    cb = conv_b.reshape(3, 1, HY_C)
    zspec = lambda chunk: pl.BlockSpec((1, s, tc), lambda c, i: (i, 0, chunk * nct + c))
    return pl.pallas_call(
        _hyena_kernel,
        out_shape=jax.ShapeDtypeStruct((b, s, HY_C), F32),
        grid=(nct, b),
        in_specs=[zspec(0), zspec(1), zspec(2),
                  pl.BlockSpec((3, 3, 1, tc), lambda c, i: (0, 0, 0, c)),
                  pl.BlockSpec((3, 1, tc), lambda c, i: (0, 0, c)),
                  _resident((s, s), lambda c, i: (0, 0)),
                  _resident((s, s), lambda c, i: (0, 0)),
                  _resident((2, 2, s, tc), lambda c, i: (0, 0, 0, c)),
                  pl.BlockSpec((2, 1, tc), lambda c, i: (0, 0, c)),
                  pl.BlockSpec((2, 1, tc), lambda c, i: (0, 0, c))],
        out_specs=pl.BlockSpec((1, s, tc), lambda c, i: (i, 0, c)),
        scratch_shapes=[pltpu.VMEM((s, tc), BF16), pltpu.VMEM((s, tc), F32)],
        compiler_params=_params("parallel", "parallel"),
    )(z3, z3, z3, cw, cb, cmat, smat, spectra, hn, skip.reshape(2, 1, HY_C))


def _rel_bucket(rel):
    half = NUM_BUCKETS // 2
    exact = half // 2
    n = np.abs(rel)
    large = exact + (np.log(np.maximum(n, 1) / exact) / np.log(REL_MAX_DIST / exact) * (half - exact)).astype(np.int32)
    large = np.minimum(large, half - 1)
    return (np.where(rel > 0, half, 0) + np.where(n < exact, n, large)).astype(np.int32)


def _att_tiles(length):
    kw = min(ATT_KW, length)
    tiles = []
    for qs in range(0, length, ATT_TQ):
        ks = min(max(qs - (kw - ATT_TQ) // 2, 0), length - kw)
        tiles.append((qs, ks, {0: 0, -64: 1, -128: 2}[ks - qs]))
    return kw, tiles


def _att_bias(rel_bias, group, seq):
    window, dil = AT_GROUPS[group]
    band = window // (2 * dil)
    kw, tiles = _att_tiles(seq // dil)
    offsets = sorted({ks - qs for qs, ks, _ in tiles}, reverse=True)
    table = rel_bias[:, group * AT_HEADS:(group + 1) * AT_HEADS].astype(F32)
    out = []
    for off in offsets:
        rel = off + np.arange(kw)[None, :] - np.arange(ATT_TQ)[:, None]
        bias = jnp.moveaxis(table[_rel_bucket(rel * dil)], -1, 0)
        out.append(jnp.where(np.abs(rel) <= band, bias, NEG_INF))
    return jnp.stack(out)


def _dilated_kernel(q1, q2, q3, k1, k2, k3, v1, v2, v3, qg_ref, kg_ref, b1, b2, b3, o_ref,
                    qn_ref, kn_ref, og_ref, lg_ref):
    seq = o_ref.shape[1]
    scale = HD ** -0.5
    contract_last = (((1,), (1,)), ((), ()))
    for g, (q_ref, k_ref, v_ref, bias_ref) in enumerate(((q1, k1, v1, b1), (q2, k2, v2, b2), (q3, k3, v3, b3))):
        dil = AT_GROUPS[g][1]
        kw, tiles = _att_tiles(seq // dil)
        qn_ref[...] = _rms_rows(q_ref[0], qg_ref[...])
        kn_ref[...] = _rms_rows(k_ref[0], kg_ref[...])
        for r in range(dil):
            for qs, ks, var in tiles:
                rows = lambda start, size: (pl.ds(r + start * dil, size, stride=dil) if dil > 1
                                            else pl.ds(start, size))
                qt = qn_ref[rows(qs, ATT_TQ), :].astype(BF16)
                kt = kn_ref[rows(ks, kw), :].astype(BF16)
                vt = v_ref[0, rows(ks, kw), :].astype(BF16)
                s = lax.dot_general(qt, kt, contract_last, preferred_element_type=F32) * scale + bias_ref[var, 0]
                m = jnp.max(s, axis=-1, keepdims=True)
                p = jnp.exp(s - m)
                z = jnp.sum(p, axis=-1, keepdims=True)
                og_ref[g, rows(qs, ATT_TQ), :] = jnp.dot(p.astype(BF16), vt, preferred_element_type=F32) / z
                lg_ref[g, rows(qs, ATT_TQ), :] = jnp.broadcast_to(m + jnp.log(z), (ATT_TQ, HD))
    l0, l1, l2 = lg_ref[0], lg_ref[1], lg_ref[2]
    mx = jnp.maximum(jnp.maximum(l0, l1), l2)
    e0, e1, e2 = jnp.exp(l0 - mx), jnp.exp(l1 - mx), jnp.exp(l2 - mx)
    o_ref[0] = (e0 * og_ref[0] + e1 * og_ref[1] + e2 * og_ref[2]) / (e0 + e1 + e2)


def _dilated_attention(z3, q_gain, k_gain, rel_bias):
    b, s, _ = z3.shape
    ng = len(AT_GROUPS)
    col = lambda part, g: pl.BlockSpec((1, s, HD), lambda i, h: (i, 0, part * ng * AT_HEADS + g * AT_HEADS + h))
    biases = [_att_bias(rel_bias, g, s) for g in range(ng)]
    bias_spec = lambda a: pl.BlockSpec((a.shape[0], 1) + a.shape[2:], lambda i, h: (0, h, 0, 0))
    gain_spec = pl.BlockSpec((1, HD), lambda i, h: (0, 0))
    return pl.pallas_call(
        _dilated_kernel,
        out_shape=jax.ShapeDtypeStruct((b, s, AT_HEADS * HD), F32),
        grid=(b, AT_HEADS),
        in_specs=[col(p, g) for p in range(3) for g in range(ng)] + [gain_spec, gain_spec]
                 + [bias_spec(a) for a in biases],
        out_specs=pl.BlockSpec((1, s, HD), lambda i, h: (i, 0, h)),
        scratch_shapes=[pltpu.VMEM((s, HD), F32), pltpu.VMEM((s, HD), F32),
                        pltpu.VMEM((ng, s, HD), F32), pltpu.VMEM((ng, s, HD), F32)],
        compiler_params=_params("parallel", "parallel"),
    )(*([z3] * 9), q_gain.reshape(1, HD), k_gain.reshape(1, HD), *biases)


ROUTER_LANES = 128


def _router_kernel(x_ref, g_ref, w_ref, idx_ref, gate_ref):
    h = _rms_rows(x_ref[...], g_ref[...])
    logits = jnp.dot(h, w_ref[...], preferred_element_type=F32, precision=lax.Precision.HIGHEST)
    lane = lax.broadcasted_iota(jnp.int32, logits.shape, 1).astype(F32)
    logits = jnp.where(lane < N_EXPERTS, logits, -jnp.inf)
    m1 = jnp.max(logits, axis=-1, keepdims=True)
    i1 = jnp.min(jnp.where(logits == m1, lane, float(ROUTER_LANES)), axis=-1, keepdims=True)
    rest = jnp.where(lane == i1, -jnp.inf, logits)
    m2 = jnp.max(rest, axis=-1, keepdims=True)
    i2 = jnp.min(jnp.where(rest == m2, lane, float(ROUTER_LANES)), axis=-1, keepdims=True)
    e2 = jnp.exp(m2 - m1)
    den = 1.0 + e2
    idx_ref[...] = jnp.where(lane == 0, i1, i2).astype(jnp.int32)
    gate_ref[...] = jnp.where(lane == 0, 1.0 / den, e2 / den)


def _router(x2, gain, router):
    rows, d = x2.shape
    tm = ROW_TILE
    w = jnp.zeros((d, ROUTER_LANES), F32).at[:, :N_EXPERTS].set(router.astype(F32))
    idx, gate = pl.pallas_call(
        _router_kernel,
        out_shape=(jax.ShapeDtypeStruct((rows, ROUTER_LANES), jnp.int32),
                   jax.ShapeDtypeStruct((rows, ROUTER_LANES), F32)),
        grid=(rows // tm,),
        in_specs=[pl.BlockSpec((tm, d), lambda i: (i, 0)),
                  pl.BlockSpec((1, d), lambda i: (0, 0)),
                  pl.BlockSpec((d, ROUTER_LANES), lambda i: (0, 0))],
        out_specs=(pl.BlockSpec((tm, ROUTER_LANES), lambda i: (i, 0)),
                   pl.BlockSpec((tm, ROUTER_LANES), lambda i: (i, 0))),
        compiler_params=_params("parallel"),
    )(x2, gain.reshape(1, d), w)
    return idx[:, :TOP_K], gate[:, :TOP_K]


def _moe(x2, gain, router, wg, wu, wd):
    t, d = x2.shape
    a = t * TOP_K
    top_i, gates = _router(x2, gain, router)
    expert = top_i.reshape(a)
    onehot = (expert[:, None] == jnp.arange(N_EXPERTS, dtype=jnp.int32)[None]).astype(jnp.int32)
    csum = jnp.cumsum(onehot, axis=0)
    rank = jnp.sum(csum * onehot, axis=1) - 1
    counts = csum[-1]
    padded = (counts + MOE_ROWS - 1) // MOE_ROWS * MOE_ROWS
    pend = jnp.cumsum(padded)
    dest = (pend - padded)[expert] + rank
    p = a + N_EXPERTS * MOE_ROWS
    nblk = p // MOE_ROWS
    token = jnp.repeat(jnp.arange(t, dtype=jnp.int32), TOP_K)
    buf_tok = jnp.zeros((p,), jnp.int32).at[dest].set(token)
    buf_gate = jnp.zeros((p,), F32).at[dest].set(gates.reshape(a))
    blk_expert = jnp.minimum(
        jnp.searchsorted(pend, jnp.arange(nblk, dtype=jnp.int32) * MOE_ROWS, side='right'),
        N_EXPERTS - 1).astype(jnp.int32)
    xb = x2[buf_tok]
    yb = _ffn(xb, gain, wg, wu, wd, blk_expert, pend[-1:].astype(jnp.int32), buf_gate[:, None],
              tm=MOE_ROWS, residual=False)
    dest2 = dest.reshape(t, TOP_K)
    return x2 + yb[dest2[:, 0]] + yb[dest2[:, 1]]


def kernel(x, mem, rel_bias, norm_mix, norm_mem, norm_ffn, w_mem_kv, xq_norm, xk_norm, w_out, hy_w_in, hy_conv_w, hy_conv_b, hy_filt_w1, hy_filt_b1, hy_filt_w2, hy_filt_b2, hy_filt_w3, hy_sin_freq, hy_skip, at_w_in, at_q_norm, at_k_norm, ffn_w_gate, ffn_w_up, ffn_w_down, moe_router, moe_w_gate, moe_w_up, moe_w_down):
    b, s, d = x.shape
    t = b * s
    m_len = mem.shape[1]
    x2 = x.reshape(t, d)
    mem2 = mem.reshape(b * m_len, d)
    bf = lambda w: w.astype(BF16)
    zero_blocks = jnp.zeros((t // ROW_TILE,), jnp.int32)
    all_rows = jnp.full((1,), t, jnp.int32)
    ones = jnp.ones((t, 1), F32)

    cmat, smat = _dft_tables(s)
    spectra, hn = _hyena_filters(s, cmat, smat, hy_filt_w1[0], hy_filt_b1[0], hy_filt_w2[0], hy_filt_b2[0],
                                 hy_filt_w3[0], hy_sin_freq[0])
    z = _norm_matmul(x2, norm_mix[0], bf(hy_w_in[0]), 1024).reshape(b, s, -1)
    kv = _norm_matmul(mem2, norm_mem[0], bf(w_mem_kv[0]), 1024).reshape(b, m_len, -1)
    self_out = _hyena(z, hy_conv_w[0], hy_conv_b[0], cmat, smat, spectra, hn, hy_skip[0])
    cross = _xattn(z, 3 * HY_C // XA_W, kv, xq_norm[0], xk_norm[0])
    x2 = _outproj(x2, self_out.reshape(t, -1), cross.reshape(t, -1), bf(w_out[0]))
    x2 = _ffn(x2, norm_ffn[0], bf(ffn_w_gate), bf(ffn_w_up), bf(ffn_w_down), zero_blocks, all_rows, ones,
              tm=ROW_TILE, residual=True)

    z = _norm_matmul(x2, norm_mix[1], bf(at_w_in[0]), 1024).reshape(b, s, -1)
    kv = _norm_matmul(mem2, norm_mem[1], bf(w_mem_kv[1]), 1024).reshape(b, m_len, -1)
    self_out = _dilated_attention(z, at_q_norm[0], at_k_norm[0], rel_bias)
    cross = _xattn(z, 3 * AT_W // XA_W, kv, xq_norm[1], xk_norm[1])
    x2 = _outproj(x2, self_out.reshape(t, -1), cross.reshape(t, -1), bf(w_out[1]))
    x2 = _moe(x2, norm_ffn[1], moe_router[0], bf(moe_w_gate[0]), bf(moe_w_up[0]), bf(moe_w_down[0]))
    return x2.reshape(b, s, d)
```

```python
import functools
import math

import jax
import jax.numpy as jnp
import numpy as np
from jax import lax
from jax.experimental import pallas as pl
from jax.experimental.pallas import tpu as pltpu

F32 = jnp.float32
BF16 = jnp.bfloat16

D_MODEL = 1024
EPS = 1e-6
HY_C = 512
FILT_BANDS = 16
DECAY_TARGET = 1e-2
FAST_DECAY_PCT = 0.3
SLOW_DECAY_PCT = 1.5
MOD_SHIFT = 0.05
AT_GROUPS = ((128, 1), (512, 4), (2048, 16))
AT_HEADS = 4
HD = 128
AT_W = 1536
NUM_BUCKETS = 32
REL_MAX_DIST = 1024
NEG_INF = -1e30
XA_W = 512
D_FF = 2816
N_EXPERTS = 8
TOP_K = 2

VMEM_LIMIT_BYTES = 56 * 1024 * 1024
ROW_TILE = 512
FF_TILE = 1408
MOE_ROWS = 512
ATT_TQ = 128
ATT_KW = 256
HY_TC = 256
HY_KT = 512


def _params(*sem):
    return pltpu.CompilerParams(dimension_semantics=sem, vmem_limit_bytes=VMEM_LIMIT_BYTES)


def _rms_rows(x, gain):
    return x * lax.rsqrt(jnp.mean(x * x, axis=-1, keepdims=True) + EPS) * gain


def _resident(shape, index_map):
    return pl.BlockSpec(shape, index_map, pipeline_mode=pl.Buffered(1))


def _norm_matmul_kernel(x_ref, g_ref, w_ref, o_ref, *, tn):
    h = _rms_rows(x_ref[...], g_ref[...]).astype(BF16)
    for c in range(o_ref.shape[1] // tn):
        cols = slice(c * tn, (c + 1) * tn)
        o_ref[:, cols] = jnp.dot(h, w_ref[:, cols], preferred_element_type=F32).astype(o_ref.dtype)


def _norm_matmul(x2, gain, w_bf, tn):
    rows, d = x2.shape
    n = w_bf.shape[1]
    tm = min(ROW_TILE, rows)
    return pl.pallas_call(
        functools.partial(_norm_matmul_kernel, tn=tn),
        out_shape=jax.ShapeDtypeStruct((rows, n), BF16),
        grid=(rows // tm,),
        in_specs=[pl.BlockSpec((tm, d), lambda i: (i, 0)),
                  pl.BlockSpec((1, d), lambda i: (0, 0)),
                  _resident((d, n), lambda i: (0, 0))],
        out_specs=pl.BlockSpec((tm, n), lambda i: (i, 0)),
        compiler_params=_params("parallel"),
        name="norm_matmul",
    )(x2, gain.reshape(1, d), w_bf)


def _outproj_kernel(x_ref, a_ref, c_ref, wa_ref, wc_ref, o_ref):
    o_ref[...] = (x_ref[...]
                  + jnp.dot(a_ref[...].astype(BF16), wa_ref[...], preferred_element_type=F32)
                  + jnp.dot(c_ref[...].astype(BF16), wc_ref[...], preferred_element_type=F32))


def _outproj(x2, self_out, cross, w_bf):
    rows, d = x2.shape
    half = self_out.shape[1]
    tm = ROW_TILE
    return pl.pallas_call(
        _outproj_kernel,
        out_shape=jax.ShapeDtypeStruct((rows, d), F32),
        grid=(rows // tm,),
        in_specs=[pl.BlockSpec((tm, d), lambda i: (i, 0)),
                  pl.BlockSpec((tm, half), lambda i: (i, 0)),
                  pl.BlockSpec((tm, half), lambda i: (i, 0)),
                  pl.BlockSpec((half, d), lambda i: (0, 0)),
                  pl.BlockSpec((half, d), lambda i: (1, 0))],
        out_specs=pl.BlockSpec((tm, d), lambda i: (i, 0)),
        compiler_params=_params("parallel"),
        name="outproj",
    )(x2, self_out, cross, w_bf, w_bf)


def _ffn_kernel(eid_ref, nused_ref, x_ref, g_ref, wg_ref, wu_ref, wd_ref, gate_ref, o_ref, h_ref, acc_ref,
                *, residual):
    i, j = pl.program_id(0), pl.program_id(1)
    active = i * x_ref.shape[0] < nused_ref[0]

    @pl.when(j == 0)
    def _():
        h_ref[...] = _rms_rows(x_ref[...], g_ref[...]).astype(BF16)
        acc_ref[...] = jnp.zeros_like(acc_ref)

    @pl.when(active)
    def _():
        h = h_ref[...]
        gg = jnp.dot(h, wg_ref[0], preferred_element_type=F32)
        uu = jnp.dot(h, wu_ref[0], preferred_element_type=F32)
        a = (gg * jax.nn.sigmoid(gg)) * uu
        acc_ref[...] += jnp.dot(a.astype(BF16), wd_ref[0], preferred_element_type=F32)

    @pl.when(j == pl.num_programs(1) - 1)
    def _():
        if residual:
            o_ref[...] = x_ref[...] + acc_ref[...]
        else:
            o_ref[...] = acc_ref[...] * gate_ref[...]


def _ffn(x2, gain, wg, wu, wd, eid, nused, gate, *, tm, residual):
    rows, d = x2.shape
    f = wg.shape[2]
    tf = FF_TILE
    grid_spec = pltpu.PrefetchScalarGridSpec(
        num_scalar_prefetch=2,
        grid=(rows // tm, f // tf),
        in_specs=[pl.BlockSpec((tm, d), lambda i, j, e, n: (i, 0)),
                  pl.BlockSpec((1, d), lambda i, j, e, n: (0, 0)),
                  pl.BlockSpec((1, d, tf), lambda i, j, e, n: (e[i], 0, j)),
                  pl.BlockSpec((1, d, tf), lambda i, j, e, n: (e[i], 0, j)),
                  pl.BlockSpec((1, tf, d), lambda i, j, e, n: (e[i], j, 0)),
                  pl.BlockSpec((tm, 1), lambda i, j, e, n: (i, 0))],
        out_specs=pl.BlockSpec((tm, d), lambda i, j, e, n: (i, 0)),
        scratch_shapes=[pltpu.VMEM((tm, d), BF16), pltpu.VMEM((tm, d), F32)],
    )
    return pl.pallas_call(
        functools.partial(_ffn_kernel, residual=residual),
        out_shape=jax.ShapeDtypeStruct((rows, d), F32),
        grid_spec=grid_spec,
        compiler_params=_params("parallel", "arbitrary"),
        name="ffn_residual" if residual else "ffn_experts",
    )(eid, nused, x2, gain.reshape(1, d), wg, wu, wd, gate)


def _xattn_kernel(xq_ref, kv_ref, qg_ref, kg_ref, o_ref):
    scale = HD ** -0.5
    for h in range(XA_W // HD):
        cols = slice(h * HD, (h + 1) * HD)
        q = _rms_rows(xq_ref[0, :, cols].astype(F32), qg_ref[...]).astype(BF16)
        k = _rms_rows(kv_ref[0, :, cols].astype(F32), kg_ref[...]).astype(BF16)
        v = kv_ref[0, :, XA_W + h * HD:XA_W + (h + 1) * HD]
        s = lax.dot_general(q, k, (((1,), (1,)), ((), ())), preferred_element_type=F32) * scale
        p = jnp.exp(s - jnp.max(s, axis=-1, keepdims=True))
        z = jnp.sum(p, axis=-1, keepdims=True)
        o_ref[0, :, cols] = jnp.dot(p.astype(BF16), v, preferred_element_type=F32) / z


def _xattn(z3, xq_block, kv3, q_gain, k_gain):
    b, s, _ = z3.shape
    m = kv3.shape[1]
    ts = ROW_TILE
    return pl.pallas_call(
        _xattn_kernel,
        out_shape=jax.ShapeDtypeStruct((b, s, XA_W), F32),
        grid=(b, s // ts),
        in_specs=[pl.BlockSpec((1, ts, XA_W), lambda i, j: (i, j, xq_block)),
                  pl.BlockSpec((1, m, 2 * XA_W), lambda i, j: (i, 0, 0)),
                  pl.BlockSpec((1, HD), lambda i, j: (0, 0)),
                  pl.BlockSpec((1, HD), lambda i, j: (0, 0))],
        out_specs=pl.BlockSpec((1, ts, XA_W), lambda i, j: (i, j, 0)),
        compiler_params=_params("parallel", "parallel"),
        name="cross_attention",
    )(z3, kv3, q_gain.reshape(1, HD), k_gain.reshape(1, HD))


def _dft_kernel(c_ref, s_ref, c0_ref, s0_ref, *, n_fft):
    rows, cols = c_ref.shape
    i = pl.program_id(0)

    @pl.when(i == 0)
    def _():
        k = lax.broadcasted_iota(jnp.int32, (rows, cols), 0)
        n = lax.broadcasted_iota(jnp.int32, (rows, cols), 1)
        ang = ((k * n) & (n_fft - 1)).astype(F32) * (2.0 * math.pi / n_fft)
        c0_ref[...] = jnp.cos(ang)
        s0_ref[...] = jnp.sin(ang)

    period = n_fft // rows
    n = lax.broadcasted_iota(jnp.int32, (8, cols), 1)
    shift = ((i * n) & (period - 1)).astype(F32) * (2.0 * math.pi / period)
    ca = jnp.cos(shift)[0:1]
    sa = jnp.sin(shift)[0:1]
    c0 = c0_ref[...]
    s0 = s0_ref[...]
    c_ref[...] = (ca * c0 - sa * s0).astype(BF16)
    s_ref[...] = (sa * c0 + ca * s0).astype(BF16)


def _dft_tables(length):
    rows = 256
    shape = jax.ShapeDtypeStruct((length, length), BF16)
    spec = pl.BlockSpec((rows, length), lambda i: (i, 0))
    return pl.pallas_call(
        functools.partial(_dft_kernel, n_fft=2 * length),
        out_shape=(shape, shape),
        grid=(length // rows,),
        out_specs=(spec, spec),
        scratch_shapes=[pltpu.VMEM((rows, length), F32), pltpu.VMEM((rows, length), F32)],
        compiler_params=_params("arbitrary"),
        name="dft_tables",
    )()


def _filter_time_kernel(feats_ref, w1_ref, b1_ref, w2_ref, b2_ref, fr_ref, w3_ref, t_ref, delta_ref,
                        gp_ref, gm_ref, hn_ref):
    hp = lax.Precision.HIGHEST
    length = feats_ref.shape[0]
    fr = fr_ref[...]
    h = jnp.sin(fr * (jnp.dot(feats_ref[...], w1_ref[...], preferred_element_type=F32, precision=hp) + b1_ref[...]))
    h = jnp.sin(fr * (jnp.dot(h, w2_ref[...], preferred_element_type=F32, precision=hp) + b2_ref[...]))
    mod = jnp.exp(-t_ref[...] * delta_ref[...]) + MOD_SHIFT
    row = lax.broadcasted_iota(jnp.int32, (length, 1), 0)
    alt = jnp.where(row % 2 == 0, 1.0, -1.0).astype(F32)
    for o in range(2):
        fwd = jnp.dot(h, w3_ref[2 * o], preferred_element_type=F32, precision=hp) * mod
        bwd = jnp.dot(h, w3_ref[2 * o + 1], preferred_element_type=F32, precision=hp) * mod
        bwd = jnp.where(row == 0, 0.0, bwd)
        norm = (jnp.sum(jnp.abs(fwd), axis=0, keepdims=True)
                + jnp.sum(jnp.abs(bwd), axis=0, keepdims=True) + 1e-6)
        fwd = fwd / norm
        bwd = bwd / norm
        gp = fwd + bwd
        gp_ref[o] = gp.astype(BF16)
        gm_ref[o] = (fwd - bwd).astype(BF16)
        hn_ref[o] = jnp.sum(gp * alt, axis=0, keepdims=True) * (1.0 / (2 * length))


def _filter_freq_kernel(c_ref, s_ref, gp_ref, gm_ref, h_ref):
    length = c_ref.shape[0]
    row = lax.broadcasted_iota(jnp.int32, (length, 1), 0)
    scale = jnp.where(row == 0, 1.0, 2.0).astype(F32) * (1.0 / (2 * length))
    h_ref[0, 0] = jnp.dot(c_ref[...], gp_ref[0], preferred_element_type=F32) * scale
    h_ref[0, 1] = -jnp.dot(s_ref[...], gm_ref[0], preferred_element_type=F32) * scale


def _hyena_filters(length, cmat, smat, w1, b1, w2, b2, w3, freq):
    t = jnp.linspace(0.0, 1.0, length, dtype=F32)[:, None]
    f = jnp.linspace(1e-4, FILT_BANDS - 1, FILT_BANDS, dtype=F32)[None]
    ang = (2.0 * math.pi / length) * jnp.arange(length, dtype=F32)[:, None] * f
    feats = jnp.concatenate([t, jnp.cos(ang), -jnp.sin(ang)], axis=-1)
    deltas = jnp.abs(jnp.linspace(math.log(DECAY_TARGET) / SLOW_DECAY_PCT,
                                  math.log(DECAY_TARGET) / FAST_DECAY_PCT, HY_C, dtype=F32))[None]
    hid = w1.shape[1]
    w3r = w3.reshape(hid, 4, HY_C).transpose(1, 0, 2)
    tc = HY_TC
    full = lambda shape: pl.BlockSpec(shape, lambda c: (0,) * len(shape))
    gp, gm, hn = pl.pallas_call(
        _filter_time_kernel,
        out_shape=(jax.ShapeDtypeStruct((2, length, HY_C), BF16),
                   jax.ShapeDtypeStruct((2, length, HY_C), BF16),
                   jax.ShapeDtypeStruct((2, 1, HY_C), F32)),
        grid=(HY_C // tc,),
        in_specs=[full(feats.shape), full(w1.shape), full((1, hid)), full(w2.shape), full((1, hid)),
                  full((1, hid)), pl.BlockSpec((4, hid, tc), lambda c: (0, 0, c)), full((length, 1)),
                  pl.BlockSpec((1, tc), lambda c: (0, c))],
        out_specs=(pl.BlockSpec((2, length, tc), lambda c: (0, 0, c)),
                   pl.BlockSpec((2, length, tc), lambda c: (0, 0, c)),
                   pl.BlockSpec((2, 1, tc), lambda c: (0, 0, c))),
        compiler_params=_params("parallel"),
        name="filter_time",
    )(feats, w1, b1.reshape(1, hid), w2, b2.reshape(1, hid), freq.reshape(1, hid), w3r, t, deltas)
    spectra = pl.pallas_call(
        _filter_freq_kernel,
        out_shape=jax.ShapeDtypeStruct((2, 2, length, HY_C), F32),
        grid=(2, HY_C // tc),
        in_specs=[_resident((length, length), lambda o, c: (0, 0)),
                  _resident((length, length), lambda o, c: (0, 0)),
                  pl.BlockSpec((1, length, tc), lambda o, c: (o, 0, c)),
                  pl.BlockSpec((1, length, tc), lambda o, c: (o, 0, c))],
        out_specs=pl.BlockSpec((1, 2, length, tc), lambda o, c: (o, 0, 0, c)),
        compiler_params=_params("parallel", "parallel"),
        name="filter_freq",
    )(cmat, smat, gp, gm)
    return spectra, hn


def _hyena_kernel(z0_ref, z1_ref, z2_ref, cw_ref, cb_ref, c_ref, s_ref, h_ref, hn_ref, skip_ref, o_ref,
                  ubf_ref, y_ref):
    length = z0_ref.shape[1]
    row = lax.broadcasted_iota(jnp.int32, (length, 1), 0)
    alt = jnp.where(row % 2 == 0, 1.0, -1.0).astype(F32)

    def short_conv(z_ref, c):
        u = z_ref[0].astype(F32)
        prev = jnp.where(row == 0, 0.0, pltpu.roll(u, 1, axis=0))
        nxt = jnp.where(row == length - 1, 0.0, pltpu.roll(u, length - 1, axis=0))
        return cb_ref[c] + (prev * cw_ref[0, c] + u * cw_ref[1, c] + nxt * cw_ref[2, c])

    def long_conv(u, o):
        ubf_ref[...] = u.astype(BF16)
        nyq = jnp.sum(u * alt, axis=0, keepdims=True) * hn_ref[o]
        y_ref[...] = alt * nyq + u * skip_ref[o]
        for kt in range(length // HY_KT):
            ks = slice(kt * HY_KT, (kt + 1) * HY_KT)
            ub = ubf_ref[...]
            a = jnp.dot(c_ref[ks, :], ub, preferred_element_type=F32)
            b = jnp.dot(s_ref[ks, :], ub, preferred_element_type=F32)
            hr = h_ref[o, 0, ks, :]
            hi = h_ref[o, 1, ks, :]
            re = (a * hr + b * hi).astype(BF16)
            im = (a * hi - b * hr).astype(BF16)
            y_ref[...] += (jnp.dot(c_ref[:, ks], re, preferred_element_type=F32)
                           - jnp.dot(s_ref[:, ks], im, preferred_element_type=F32))
        return y_ref[...]

    z = short_conv(z1_ref, 1) * long_conv(short_conv(z0_ref, 0), 0)
    o_ref[0] = short_conv(z2_ref, 2) * long_conv(z, 1)


def _hyena(z3, conv_w, conv_b, cmat, smat, spectra, hn, skip):
    b, s, _ = z3.shape
    tc = HY_TC
    nct = HY_C // tc
    cw = conv_w.reshape(3, 3, 1, HY_C)
    cb = conv_b.reshape(3, 1, HY_C)
    zspec = lambda chunk: pl.BlockSpec((1, s, tc), lambda c, i: (i, 0, chunk * nct + c))
    return pl.pallas_call(
        _hyena_kernel,
        out_shape=jax.ShapeDtypeStruct((b, s, HY_C), F32),
        grid=(nct, b),
        in_specs=[zspec(0), zspec(1), zspec(2),
                  pl.BlockSpec((3, 3, 1, tc), lambda c, i: (0, 0, 0, c)),
                  pl.BlockSpec((3, 1, tc), lambda c, i: (0, 0, c)),
                  _resident((s, s), lambda c, i: (0, 0)),
                  _resident((s, s), lambda c, i: (0, 0)),
                  _resident((2, 2, s, tc), lambda c, i: (0, 0, 0, c)),
                  pl.BlockSpec((2, 1, tc), lambda c, i: (0, 0, c)),
                  pl.BlockSpec((2, 1, tc), lambda c, i: (0, 0, c))],
        out_specs=pl.BlockSpec((1, s, tc), lambda c, i: (i, 0, c)),
        scratch_shapes=[pltpu.VMEM((s, tc), BF16), pltpu.VMEM((s, tc), F32)],
        compiler_params=_params("parallel", "parallel"),
        name="hyena",
    )(z3, z3, z3, cw, cb, cmat, smat, spectra, hn, skip.reshape(2, 1, HY_C))


def _rel_bucket(rel):
    half = NUM_BUCKETS // 2
    exact = half // 2
    n = np.abs(rel)
    large = exact + (np.log(np.maximum(n, 1) / exact) / np.log(REL_MAX_DIST / exact) * (half - exact)).astype(np.int32)
    large = np.minimum(large, half - 1)
    return (np.where(rel > 0, half, 0) + np.where(n < exact, n, large)).astype(np.int32)


def _att_tiles(length):
    kw = min(ATT_KW, length)
    tiles = []
    for qs in range(0, length, ATT_TQ):
        ks = min(max(qs - (kw - ATT_TQ) // 2, 0), length - kw)
        tiles.append((qs, ks, {0: 0, -64: 1, -128: 2}[ks - qs]))
    return kw, tiles


def _att_bias(rel_bias, group, seq):
    window, dil = AT_GROUPS[group]
    band = window // (2 * dil)
    kw, tiles = _att_tiles(seq // dil)
    offsets = sorted({ks - qs for qs, ks, _ in tiles}, reverse=True)
    table = rel_bias[:, group * AT_HEADS:(group + 1) * AT_HEADS].astype(F32)
    out = []
    for off in offsets:
        rel = off + np.arange(kw)[None, :] - np.arange(ATT_TQ)[:, None]
        bias = jnp.moveaxis(table[_rel_bucket(rel * dil)], -1, 0)
        out.append(jnp.where(np.abs(rel) <= band, bias, NEG_INF))
    return jnp.stack(out)


def _dilated_kernel(q1, q2, q3, k1, k2, k3, v1, v2, v3, qg_ref, kg_ref, b1, b2, b3, o_ref,
                    qn_ref, kn_ref, vn_ref, og_ref, lg_ref):
    seq = o_ref.shape[1]
    scale = HD ** -0.5
    contract_last = (((1,), (1,)), ((), ()))
    for g, (q_ref, k_ref, v_ref, bias_ref) in enumerate(((q1, k1, v1, b1), (q2, k2, v2, b2), (q3, k3, v3, b3))):
        dil = AT_GROUPS[g][1]
        kw, tiles = _att_tiles(seq // dil)
        qn_ref[...] = _rms_rows(q_ref[0].astype(F32), qg_ref[...])
        kn_ref[...] = _rms_rows(k_ref[0].astype(F32), kg_ref[...])
        vn_ref[...] = v_ref[0].astype(F32)
        for r in range(dil):
            for qs, ks, var in tiles:
                rows = lambda start, size: (pl.ds(r + start * dil, size, stride=dil) if dil > 1
                                            else pl.ds(start, size))
                qt = qn_ref[rows(qs, ATT_TQ), :].astype(BF16)
                kt = kn_ref[rows(ks, kw), :].astype(BF16)
                vt = vn_ref[rows(ks, kw), :].astype(BF16)
                s = lax.dot_general(qt, kt, contract_last, preferred_element_type=F32) * scale + bias_ref[var, 0]
                m = jnp.max(s, axis=-1, keepdims=True)
                p = jnp.exp(s - m)
                z = jnp.sum(p, axis=-1, keepdims=True)
                og_ref[g, rows(qs, ATT_TQ), :] = jnp.dot(p.astype(BF16), vt, preferred_element_type=F32) / z
                lg_ref[g, rows(qs, ATT_TQ), :] = jnp.broadcast_to(m + jnp.log(z), (ATT_TQ, HD))
    l0, l1, l2 = lg_ref[0], lg_ref[1], lg_ref[2]
    mx = jnp.maximum(jnp.maximum(l0, l1), l2)
    e0, e1, e2 = jnp.exp(l0 - mx), jnp.exp(l1 - mx), jnp.exp(l2 - mx)
    o_ref[0] = (e0 * og_ref[0] + e1 * og_ref[1] + e2 * og_ref[2]) / (e0 + e1 + e2)


def _dilated_attention(z3, q_gain, k_gain, rel_bias):
    b, s, _ = z3.shape
    ng = len(AT_GROUPS)
    col = lambda part, g: pl.BlockSpec((1, s, HD), lambda i, h: (i, 0, part * ng * AT_HEADS + g * AT_HEADS + h))
    biases = [_att_bias(rel_bias, g, s) for g in range(ng)]
    bias_spec = lambda a: pl.BlockSpec((a.shape[0], 1) + a.shape[2:], lambda i, h: (0, h, 0, 0))
    gain_spec = pl.BlockSpec((1, HD), lambda i, h: (0, 0))
    return pl.pallas_call(
        _dilated_kernel,
        out_shape=jax.ShapeDtypeStruct((b, s, AT_HEADS * HD), F32),
        grid=(b, AT_HEADS),
        in_specs=[col(p, g) for p in range(3) for g in range(ng)] + [gain_spec, gain_spec]
                 + [bias_spec(a) for a in biases],
        out_specs=pl.BlockSpec((1, s, HD), lambda i, h: (i, 0, h)),
        scratch_shapes=[pltpu.VMEM((s, HD), F32), pltpu.VMEM((s, HD), F32), pltpu.VMEM((s, HD), F32),
                        pltpu.VMEM((ng, s, HD), F32), pltpu.VMEM((ng, s, HD), F32)],
        compiler_params=_params("parallel", "parallel"),
        name="dilated_attention",
    )(*([z3] * 9), q_gain.reshape(1, HD), k_gain.reshape(1, HD), *biases)


ROUTER_LANES = 128


def _router_kernel(x_ref, g_ref, w_ref, idx_ref, gate_ref):
    h = _rms_rows(x_ref[...], g_ref[...])
    logits = jnp.dot(h, w_ref[...], preferred_element_type=F32, precision=lax.Precision.HIGHEST)
    lane = lax.broadcasted_iota(jnp.int32, logits.shape, 1).astype(F32)
    logits = jnp.where(lane < N_EXPERTS, logits, -jnp.inf)
    m1 = jnp.max(logits, axis=-1, keepdims=True)
    i1 = jnp.min(jnp.where(logits == m1, lane, float(ROUTER_LANES)), axis=-1, keepdims=True)
    rest = jnp.where(lane == i1, -jnp.inf, logits)
    m2 = jnp.max(rest, axis=-1, keepdims=True)
    i2 = jnp.min(jnp.where(rest == m2, lane, float(ROUTER_LANES)), axis=-1, keepdims=True)
    e2 = jnp.exp(m2 - m1)
    den = 1.0 + e2
    idx_ref[...] = jnp.where(lane == 0, i1, i2).astype(jnp.int32)
    gate_ref[...] = jnp.where(lane == 0, 1.0 / den, e2 / den)


def _router(x2, gain, router):
    rows, d = x2.shape
    tm = ROW_TILE
    w = jnp.zeros((d, ROUTER_LANES), F32).at[:, :N_EXPERTS].set(router.astype(F32))
    idx, gate = pl.pallas_call(
        _router_kernel,
        out_shape=(jax.ShapeDtypeStruct((rows, ROUTER_LANES), jnp.int32),
                   jax.ShapeDtypeStruct((rows, ROUTER_LANES), F32)),
        grid=(rows // tm,),
        in_specs=[pl.BlockSpec((tm, d), lambda i: (i, 0)),
                  pl.BlockSpec((1, d), lambda i: (0, 0)),
                  pl.BlockSpec((d, ROUTER_LANES), lambda i: (0, 0))],
        out_specs=(pl.BlockSpec((tm, ROUTER_LANES), lambda i: (i, 0)),
                   pl.BlockSpec((tm, ROUTER_LANES), lambda i: (i, 0))),
        compiler_params=_params("parallel"),
        name="router",
    )(x2, gain.reshape(1, d), w)
    return idx[:, :TOP_K], gate[:, :TOP_K]


def _moe(x2, gain, router, wg, wu, wd):
    t, d = x2.shape
    a = t * TOP_K
    top_i, gates = _router(x2, gain, router)
    expert = top_i.reshape(a)
    onehot = (expert[:, None] == jnp.arange(N_EXPERTS, dtype=jnp.int32)[None]).astype(jnp.int32)
    csum = jnp.cumsum(onehot, axis=0)
    rank = jnp.sum(csum * onehot, axis=1) - 1
    counts = csum[-1]
    padded = (counts + MOE_ROWS - 1) // MOE_ROWS * MOE_ROWS
    pend = jnp.cumsum(padded)
    dest = (pend - padded)[expert] + rank
    p = a + N_EXPERTS * MOE_ROWS
    nblk = p // MOE_ROWS
    token = jnp.repeat(jnp.arange(t, dtype=jnp.int32), TOP_K)
    buf_tok = jnp.zeros((p,), jnp.int32).at[dest].set(token)
    buf_gate = jnp.zeros((p,), F32).at[dest].set(gates.reshape(a))
    blk_expert = jnp.minimum(
        jnp.searchsorted(pend, jnp.arange(nblk, dtype=jnp.int32) * MOE_ROWS, side='right'),
        N_EXPERTS - 1).astype(jnp.int32)
    xb = x2[buf_tok]
    yb = _ffn(xb, gain, wg, wu, wd, blk_expert, pend[-1:].astype(jnp.int32), buf_gate[:, None],
              tm=MOE_ROWS, residual=False)
    dest2 = dest.reshape(t, TOP_K)
    return x2 + yb[dest2[:, 0]] + yb[dest2[:, 1]]


def kernel(x, mem, rel_bias, norm_mix, norm_mem, norm_ffn, w_mem_kv, xq_norm, xk_norm, w_out, hy_w_in, hy_conv_w, hy_conv_b, hy_filt_w1, hy_filt_b1, hy_filt_w2, hy_filt_b2, hy_filt_w3, hy_sin_freq, hy_skip, at_w_in, at_q_norm, at_k_norm, ffn_w_gate, ffn_w_up, ffn_w_down, moe_router, moe_w_gate, moe_w_up, moe_w_down):
    b, s, d = x.shape
    t = b * s
    m_len = mem.shape[1]
    x2 = x.reshape(t, d)
    mem2 = mem.reshape(b * m_len, d)
    bf = lambda w: w.astype(BF16)
    zero_blocks = jnp.zeros((t // ROW_TILE,), jnp.int32)
    all_rows = jnp.full((1,), t, jnp.int32)
    ones = jnp.ones((t, 1), F32)

    cmat, smat = _dft_tables(s)
    spectra, hn = _hyena_filters(s, cmat, smat, hy_filt_w1[0], hy_filt_b1[0], hy_filt_w2[0], hy_filt_b2[0],
                                 hy_filt_w3[0], hy_sin_freq[0])
    z = _norm_matmul(x2, norm_mix[0], bf(hy_w_in[0]), 1024).reshape(b, s, -1)
    kv = _norm_matmul(mem2, norm_mem[0], bf(w_mem_kv[0]), 1024).reshape(b, m_len, -1)
    self_out = _hyena(z, hy_conv_w[0], hy_conv_b[0], cmat, smat, spectra, hn, hy_skip[0])
    cross = _xattn(z, 3 * HY_C // XA_W, kv, xq_norm[0], xk_norm[0])
    x2 = _outproj(x2, self_out.reshape(t, -1), cross.reshape(t, -1), bf(w_out[0]))
    x2 = _ffn(x2, norm_ffn[0], bf(ffn_w_gate), bf(ffn_w_up), bf(ffn_w_down), zero_blocks, all_rows, ones,
              tm=ROW_TILE, residual=True)

    z = _norm_matmul(x2, norm_mix[1], bf(at_w_in[0]), 1024).reshape(b, s, -1)
    kv = _norm_matmul(mem2, norm_mem[1], bf(w_mem_kv[1]), 1024).reshape(b, m_len, -1)
    self_out = _dilated_attention(z, at_q_norm[0], at_k_norm[0], rel_bias)
    cross = _xattn(z, 3 * AT_W // XA_W, kv, xq_norm[1], xk_norm[1])
    x2 = _outproj(x2, self_out.reshape(t, -1), cross.reshape(t, -1), bf(w_out[1]))
    x2 = _moe(x2, norm_ffn[1], moe_router[0], bf(moe_w_gate[0]), bf(moe_w_up[0]), bf(moe_w_down[0]))
    return x2.reshape(b, s, d)
```

```python
import functools
import math

import jax
import jax.numpy as jnp
import numpy as np
from jax import lax
from jax.experimental import pallas as pl
from jax.experimental.pallas import tpu as pltpu

F32 = jnp.float32
BF16 = jnp.bfloat16

D_MODEL = 1024
EPS = 1e-6
HY_C = 512
FILT_BANDS = 16
DECAY_TARGET = 1e-2
FAST_DECAY_PCT = 0.3
SLOW_DECAY_PCT = 1.5
MOD_SHIFT = 0.05
AT_GROUPS = ((128, 1), (512, 4), (2048, 16))
AT_HEADS = 4
HD = 128
AT_W = 1536
NUM_BUCKETS = 32
REL_MAX_DIST = 1024
NEG_INF = -1e30
XA_W = 512
D_FF = 2816
N_EXPERTS = 8
TOP_K = 2

VMEM_LIMIT_BYTES = 56 * 1024 * 1024
ROW_TILE = 512
FF_TILE = 1408
MOE_ROWS = 512
MOE_CHUNK = 512
ATT_TQ = 128
ATT_KW = 256
HY_TC = 256
HY_KT = 512


def _params(*sem):
    return pltpu.CompilerParams(dimension_semantics=sem, vmem_limit_bytes=VMEM_LIMIT_BYTES)


def _rms_rows(x, gain):
    return x * lax.rsqrt(jnp.mean(x * x, axis=-1, keepdims=True) + EPS) * gain


def _resident(shape, index_map):
    return pl.BlockSpec(shape, index_map, pipeline_mode=pl.Buffered(1))


def _norm_matmul_kernel(x_ref, g_ref, w_ref, o_ref, *, tn):
    h = _rms_rows(x_ref[...], g_ref[...]).astype(BF16)
    for c in range(o_ref.shape[1] // tn):
        cols = slice(c * tn, (c + 1) * tn)
        o_ref[:, cols] = jnp.dot(h, w_ref[:, cols], preferred_element_type=F32).astype(o_ref.dtype)


def _norm_matmul(x2, gain, w_bf, tn):
    rows, d = x2.shape
    n = w_bf.shape[1]
    tm = min(ROW_TILE, rows)
    return pl.pallas_call(
        functools.partial(_norm_matmul_kernel, tn=tn),
        out_shape=jax.ShapeDtypeStruct((rows, n), BF16),
        grid=(rows // tm,),
        in_specs=[pl.BlockSpec((tm, d), lambda i: (i, 0)),
                  pl.BlockSpec((1, d), lambda i: (0, 0)),
                  _resident((d, n), lambda i: (0, 0))],
        out_specs=pl.BlockSpec((tm, n), lambda i: (i, 0)),
        compiler_params=_params("parallel"),
        name="norm_matmul",
    )(x2, gain.reshape(1, d), w_bf)


def _outproj_kernel(x_ref, a_ref, c_ref, wa_ref, wc_ref, o_ref):
    o_ref[...] = (x_ref[...]
                  + jnp.dot(a_ref[...].astype(BF16), wa_ref[...], preferred_element_type=F32)
                  + jnp.dot(c_ref[...].astype(BF16), wc_ref[...], preferred_element_type=F32))


def _outproj(x2, self_out, cross, w_bf):
    rows, d = x2.shape
    half = self_out.shape[1]
    tm = ROW_TILE
    return pl.pallas_call(
        _outproj_kernel,
        out_shape=jax.ShapeDtypeStruct((rows, d), F32),
        grid=(rows // tm,),
        in_specs=[pl.BlockSpec((tm, d), lambda i: (i, 0)),
                  pl.BlockSpec((tm, half), lambda i: (i, 0)),
                  pl.BlockSpec((tm, half), lambda i: (i, 0)),
                  pl.BlockSpec((half, d), lambda i: (0, 0)),
                  pl.BlockSpec((half, d), lambda i: (1, 0))],
        out_specs=pl.BlockSpec((tm, d), lambda i: (i, 0)),
        compiler_params=_params("parallel"),
        name="outproj",
    )(x2, self_out, cross, w_bf, w_bf)


def _swiglu_accumulate(h, wg_ref, wu_ref, wd_ref, acc_ref):
    gg = jnp.dot(h, wg_ref[0], preferred_element_type=F32)
    uu = jnp.dot(h, wu_ref[0], preferred_element_type=F32)
    a = (gg * jax.nn.sigmoid(gg)) * uu
    acc_ref[...] += jnp.dot(a.astype(BF16), wd_ref[0], preferred_element_type=F32)


def _ffn_dense_kernel(x_ref, g_ref, wg_ref, wu_ref, wd_ref, o_ref, h_ref, acc_ref):
    j = pl.program_id(1)

    @pl.when(j == 0)
    def _():
        h_ref[...] = _rms_rows(x_ref[...], g_ref[...]).astype(BF16)
        acc_ref[...] = jnp.zeros_like(acc_ref)

    _swiglu_accumulate(h_ref[...], wg_ref, wu_ref, wd_ref, acc_ref)

    @pl.when(j == pl.num_programs(1) - 1)
    def _():
        o_ref[...] = x_ref[...] + acc_ref[...]


def _ffn_dense(x2, gain, wg, wu, wd):
    rows, d = x2.shape
    f = wg.shape[2]
    tm, tf = ROW_TILE, FF_TILE
    return pl.pallas_call(
        _ffn_dense_kernel,
        out_shape=jax.ShapeDtypeStruct((rows, d), F32),
        grid=(rows // tm, f // tf),
        in_specs=[pl.BlockSpec((tm, d), lambda i, j: (i, 0)),
                  pl.BlockSpec((1, d), lambda i, j: (0, 0)),
                  pl.BlockSpec((1, d, tf), lambda i, j: (0, 0, j)),
                  pl.BlockSpec((1, d, tf), lambda i, j: (0, 0, j)),
                  pl.BlockSpec((1, tf, d), lambda i, j: (0, j, 0))],
        out_specs=pl.BlockSpec((tm, d), lambda i, j: (i, 0)),
        scratch_shapes=[pltpu.VMEM((tm, d), BF16), pltpu.VMEM((tm, d), F32)],
        compiler_params=_params("parallel", "arbitrary"),
        name="ffn_dense",
    )(x2, gain.reshape(1, d), wg, wu, wd)


def _ffn_expert_kernel(eid_ref, nused_ref, h_ref, wg_ref, wu_ref, wd_ref, o_ref, acc_ref):
    i, j = pl.program_id(0), pl.program_id(1)

    @pl.when(j == 0)
    def _():
        acc_ref[...] = jnp.zeros_like(acc_ref)

    @pl.when(i * h_ref.shape[0] < nused_ref[0])
    def _():
        _swiglu_accumulate(h_ref[...], wg_ref, wu_ref, wd_ref, acc_ref)

    @pl.when(j == pl.num_programs(1) - 1)
    def _():
        o_ref[...] = acc_ref[...].astype(o_ref.dtype)


def _ffn_experts(hb, wg, wu, wd, eid, nused):
    rows, d = hb.shape
    f = wg.shape[2]
    tm, tf = MOE_ROWS, FF_TILE
    grid_spec = pltpu.PrefetchScalarGridSpec(
        num_scalar_prefetch=2,
        grid=(rows // tm, f // tf),
        in_specs=[pl.BlockSpec((tm, d), lambda i, j, e, n: (i, 0)),
                  pl.BlockSpec((1, d, tf), lambda i, j, e, n: (e[i], 0, j)),
                  pl.BlockSpec((1, d, tf), lambda i, j, e, n: (e[i], 0, j)),
                  pl.BlockSpec((1, tf, d), lambda i, j, e, n: (e[i], j, 0))],
        out_specs=pl.BlockSpec((tm, d), lambda i, j, e, n: (i, 0)),
        scratch_shapes=[pltpu.VMEM((tm, d), F32)],
    )
    return pl.pallas_call(
        _ffn_expert_kernel,
        out_shape=jax.ShapeDtypeStruct((rows, d), BF16),
        grid_spec=grid_spec,
        compiler_params=_params("parallel", "arbitrary"),
        name="ffn_experts",
    )(eid, nused, hb, wg, wu, wd)


def _xattn_kernel(xq_ref, kv_ref, qg_ref, kg_ref, o_ref):
    scale = HD ** -0.5
    for h in range(XA_W // HD):
        cols = slice(h * HD, (h + 1) * HD)
        q = _rms_rows(xq_ref[0, :, cols].astype(F32), qg_ref[...]).astype(BF16)
        k = _rms_rows(kv_ref[0, :, cols].astype(F32), kg_ref[...]).astype(BF16)
        v = kv_ref[0, :, XA_W + h * HD:XA_W + (h + 1) * HD]
        s = lax.dot_general(q, k, (((1,), (1,)), ((), ())), preferred_element_type=F32) * scale
        p = jnp.exp(s - jnp.max(s, axis=-1, keepdims=True))
        z = jnp.sum(p, axis=-1, keepdims=True)
        o_ref[0, :, cols] = jnp.dot(p.astype(BF16), v, preferred_element_type=F32) / z


def _xattn(z3, xq_block, kv3, q_gain, k_gain):
    b, s, _ = z3.shape
    m = kv3.shape[1]
    ts = ROW_TILE
    return pl.pallas_call(
        _xattn_kernel,
        out_shape=jax.ShapeDtypeStruct((b, s, XA_W), F32),
        grid=(b, s // ts),
        in_specs=[pl.BlockSpec((1, ts, XA_W), lambda i, j: (i, j, xq_block)),
                  pl.BlockSpec((1, m, 2 * XA_W), lambda i, j: (i, 0, 0)),
                  pl.BlockSpec((1, HD), lambda i, j: (0, 0)),
                  pl.BlockSpec((1, HD), lambda i, j: (0, 0))],
        out_specs=pl.BlockSpec((1, ts, XA_W), lambda i, j: (i, j, 0)),
        compiler_params=_params("parallel", "parallel"),
        name="cross_attention",
    )(z3, kv3, q_gain.reshape(1, HD), k_gain.reshape(1, HD))


def _dft_kernel(c_ref, s_ref, c0_ref, s0_ref, *, n_fft):
    rows, cols = c_ref.shape
    i = pl.program_id(0)

    @pl.when(i == 0)
    def _():
        k = lax.broadcasted_iota(jnp.int32, (rows, cols), 0)
        n = lax.broadcasted_iota(jnp.int32, (rows, cols), 1)
        ang = ((k * n) & (n_fft - 1)).astype(F32) * (2.0 * math.pi / n_fft)
        c0_ref[...] = jnp.cos(ang)
        s0_ref[...] = jnp.sin(ang)

    period = n_fft // rows
    n = lax.broadcasted_iota(jnp.int32, (8, cols), 1)
    shift = ((i * n) & (period - 1)).astype(F32) * (2.0 * math.pi / period)
    ca = jnp.cos(shift)[0:1]
    sa = jnp.sin(shift)[0:1]
    c0 = c0_ref[...]
    s0 = s0_ref[...]
    c_ref[...] = (ca * c0 - sa * s0).astype(BF16)
    s_ref[...] = (sa * c0 + ca * s0).astype(BF16)


def _dft_tables(length):
    rows = 256
    shape = jax.ShapeDtypeStruct((length, length), BF16)
    spec = pl.BlockSpec((rows, length), lambda i: (i, 0))
    return pl.pallas_call(
        functools.partial(_dft_kernel, n_fft=2 * length),
        out_shape=(shape, shape),
        grid=(length // rows,),
        out_specs=(spec, spec),
        scratch_shapes=[pltpu.VMEM((rows, length), F32), pltpu.VMEM((rows, length), F32)],
        compiler_params=_params("arbitrary"),
        name="dft_tables",
    )()


def _filter_time_kernel(feats_ref, w1_ref, b1_ref, w2_ref, b2_ref, fr_ref, w3_ref, t_ref, delta_ref,
                        gp_ref, gm_ref, hn_ref):
    hp = lax.Precision.HIGHEST
    length = feats_ref.shape[0]
    fr = fr_ref[...]
    h = jnp.sin(fr * (jnp.dot(feats_ref[...], w1_ref[...], preferred_element_type=F32, precision=hp) + b1_ref[...]))
    h = jnp.sin(fr * (jnp.dot(h, w2_ref[...], preferred_element_type=F32, precision=hp) + b2_ref[...]))
    mod = jnp.exp(-t_ref[...] * delta_ref[...]) + MOD_SHIFT
    row = lax.broadcasted_iota(jnp.int32, (length, 1), 0)
    alt = jnp.where(row % 2 == 0, 1.0, -1.0).astype(F32)
    for o in range(2):
        fwd = jnp.dot(h, w3_ref[2 * o], preferred_element_type=F32, precision=hp) * mod
        bwd = jnp.dot(h, w3_ref[2 * o + 1], preferred_element_type=F32, precision=hp) * mod
        bwd = jnp.where(row == 0, 0.0, bwd)
        norm = (jnp.sum(jnp.abs(fwd), axis=0, keepdims=True)
                + jnp.sum(jnp.abs(bwd), axis=0, keepdims=True) + 1e-6)
        fwd = fwd / norm
        bwd = bwd / norm
        gp = fwd + bwd
        gp_ref[o] = gp.astype(BF16)
        gm_ref[o] = (fwd - bwd).astype(BF16)
        hn_ref[o] = jnp.sum(gp * alt, axis=0, keepdims=True) * (1.0 / (2 * length))


def _filter_freq_kernel(c_ref, s_ref, gp_ref, gm_ref, h_ref):
    length = c_ref.shape[0]
    row = lax.broadcasted_iota(jnp.int32, (length, 1), 0)
    scale = jnp.where(row == 0, 1.0, 2.0).astype(F32) * (1.0 / (2 * length))
    h_ref[0, 0] = jnp.dot(c_ref[...], gp_ref[0], preferred_element_type=F32) * scale
    h_ref[0, 1] = -jnp.dot(s_ref[...], gm_ref[0], preferred_element_type=F32) * scale


def _hyena_filters(length, cmat, smat, w1, b1, w2, b2, w3, freq):
    t = jnp.linspace(0.0, 1.0, length, dtype=F32)[:, None]
    f = jnp.linspace(1e-4, FILT_BANDS - 1, FILT_BANDS, dtype=F32)[None]
    ang = (2.0 * math.pi / length) * jnp.arange(length, dtype=F32)[:, None] * f
    feats = jnp.concatenate([t, jnp.cos(ang), -jnp.sin(ang)], axis=-1)
    deltas = jnp.abs(jnp.linspace(math.log(DECAY_TARGET) / SLOW_DECAY_PCT,
                                  math.log(DECAY_TARGET) / FAST_DECAY_PCT, HY_C, dtype=F32))[None]
    hid = w1.shape[1]
    w3r = w3.reshape(hid, 4, HY_C).transpose(1, 0, 2)
    tc = HY_TC
    full = lambda shape: pl.BlockSpec(shape, lambda c: (0,) * len(shape))
    gp, gm, hn = pl.pallas_call(
        _filter_time_kernel,
        out_shape=(jax.ShapeDtypeStruct((2, length, HY_C), BF16),
                   jax.ShapeDtypeStruct((2, length, HY_C), BF16),
                   jax.ShapeDtypeStruct((2, 1, HY_C), F32)),
        grid=(HY_C // tc,),
        in_specs=[full(feats.shape), full(w1.shape), full((1, hid)), full(w2.shape), full((1, hid)),
                  full((1, hid)), pl.BlockSpec((4, hid, tc), lambda c: (0, 0, c)), full((length, 1)),
                  pl.BlockSpec((1, tc), lambda c: (0, c))],
        out_specs=(pl.BlockSpec((2, length, tc), lambda c: (0, 0, c)),
                   pl.BlockSpec((2, length, tc), lambda c: (0, 0, c)),
                   pl.BlockSpec((2, 1, tc), lambda c: (0, 0, c))),
        compiler_params=_params("parallel"),
        name="filter_time",
    )(feats, w1, b1.reshape(1, hid), w2, b2.reshape(1, hid), freq.reshape(1, hid), w3r, t, deltas)
    spectra = pl.pallas_call(
        _filter_freq_kernel,
        out_shape=jax.ShapeDtypeStruct((2, 2, length, HY_C), F32),
        grid=(2, HY_C // tc),
        in_specs=[_resident((length, length), lambda o, c: (0, 0)),
                  _resident((length, length), lambda o, c: (0, 0)),
                  pl.BlockSpec((1, length, tc), lambda o, c: (o, 0, c)),
                  pl.BlockSpec((1, length, tc), lambda o, c: (o, 0, c))],
        out_specs=pl.BlockSpec((1, 2, length, tc), lambda o, c: (o, 0, 0, c)),
        compiler_params=_params("parallel", "parallel"),
        name="filter_freq",
    )(cmat, smat, gp, gm)
    return spectra, hn


def _hyena_kernel(z0_ref, z1_ref, z2_ref, cw_ref, cb_ref, c_ref, s_ref, h_ref, hn_ref, skip_ref, o_ref,
                  ubf_ref, y_ref):
    length = z0_ref.shape[1]
    row = lax.broadcasted_iota(jnp.int32, (length, 1), 0)
    alt = jnp.where(row % 2 == 0, 1.0, -1.0).astype(F32)

    def short_conv(z_ref, c):
        u = z_ref[0].astype(F32)
        prev = jnp.where(row == 0, 0.0, pltpu.roll(u, 1, axis=0))
        nxt = jnp.where(row == length - 1, 0.0, pltpu.roll(u, length - 1, axis=0))
        return cb_ref[c] + (prev * cw_ref[0, c] + u * cw_ref[1, c] + nxt * cw_ref[2, c])

    def long_conv(u, o):
        ubf_ref[...] = u.astype(BF16)
        nyq = jnp.sum(u * alt, axis=0, keepdims=True) * hn_ref[o]
        y_ref[...] = alt * nyq + u * skip_ref[o]
        for kt in range(length // HY_KT):
            ks = slice(kt * HY_KT, (kt + 1) * HY_KT)
            ub = ubf_ref[...]
            a = jnp.dot(c_ref[ks, :], ub, preferred_element_type=F32)
            b = jnp.dot(s_ref[ks, :], ub, preferred_element_type=F32)
            hr = h_ref[o, 0, ks, :]
            hi = h_ref[o, 1, ks, :]
            re = (a * hr + b * hi).astype(BF16)
            im = (a * hi - b * hr).astype(BF16)
            y_ref[...] += (jnp.dot(c_ref[:, ks], re, preferred_element_type=F32)
                           - jnp.dot(s_ref[:, ks], im, preferred_element_type=F32))
        return y_ref[...]

    z = short_conv(z1_ref, 1) * long_conv(short_conv(z0_ref, 0), 0)
    o_ref[0] = short_conv(z2_ref, 2) * long_conv(z, 1)


def _hyena(z3, conv_w, conv_b, cmat, smat, spectra, hn, skip):
    b, s, _ = z3.shape
    tc = HY_TC
    nct = HY_C // tc
    cw = conv_w.reshape(3, 3, 1, HY_C)
    cb = conv_b.reshape(3, 1, HY_C)
    zspec = lambda chunk: pl.BlockSpec((1, s, tc), lambda c, i: (i, 0, chunk * nct + c))
    return pl.pallas_call(
        _hyena_kernel,
        out_shape=jax.ShapeDtypeStruct((b, s, HY_C), F32),
        grid=(nct, b),
        in_specs=[zspec(0), zspec(1), zspec(2),
                  pl.BlockSpec((3, 3, 1, tc), lambda c, i: (0, 0, 0, c)),
                  pl.BlockSpec((3, 1, tc), lambda c, i: (0, 0, c)),
                  _resident((s, s), lambda c, i: (0, 0)),
                  _resident((s, s), lambda c, i: (0, 0)),
                  _resident((2, 2, s, tc), lambda c, i: (0, 0, 0, c)),
                  pl.BlockSpec((2, 1, tc), lambda c, i: (0, 0, c)),
                  pl.BlockSpec((2, 1, tc), lambda c, i: (0, 0, c))],
        out_specs=pl.BlockSpec((1, s, tc), lambda c, i: (i, 0, c)),
        scratch_shapes=[pltpu.VMEM((s, tc), BF16), pltpu.VMEM((s, tc), F32)],
        compiler_params=_params("parallel", "parallel"),
        name="hyena",
    )(z3, z3, z3, cw, cb, cmat, smat, spectra, hn, skip.reshape(2, 1, HY_C))


def _rel_bucket(rel):
    half = NUM_BUCKETS // 2
    exact = half // 2
    n = np.abs(rel)
    large = exact + (np.log(np.maximum(n, 1) / exact) / np.log(REL_MAX_DIST / exact) * (half - exact)).astype(np.int32)
    large = np.minimum(large, half - 1)
    return (np.where(rel > 0, half, 0) + np.where(n < exact, n, large)).astype(np.int32)


def _att_tiles(length):
    kw = min(ATT_KW, length)
    tiles = []
    for qs in range(0, length, ATT_TQ):
        ks = min(max(qs - (kw - ATT_TQ) // 2, 0), length - kw)
        tiles.append((qs, ks, {0: 0, -64: 1, -128: 2}[ks - qs]))
    return kw, tiles


def _att_bias(rel_bias, group, seq):
    window, dil = AT_GROUPS[group]
    band = window // (2 * dil)
    kw, tiles = _att_tiles(seq // dil)
    offsets = sorted({ks - qs for qs, ks, _ in tiles}, reverse=True)
    table = rel_bias[:, group * AT_HEADS:(group + 1) * AT_HEADS].astype(F32)
    out = []
    for off in offsets:
        rel = off + np.arange(kw)[None, :] - np.arange(ATT_TQ)[:, None]
        bias = jnp.moveaxis(table[_rel_bucket(rel * dil)], -1, 0)
        out.append(jnp.where(np.abs(rel) <= band, bias, NEG_INF))
    return jnp.stack(out)


def _dilated_kernel(q1, q2, q3, k1, k2, k3, v1, v2, v3, qg_ref, kg_ref, b1, b2, b3, o_ref,
                    qn_ref, kn_ref, vn_ref, og_ref, lg_ref):
    seq = o_ref.shape[1]
    scale = HD ** -0.5
    contract_last = (((1,), (1,)), ((), ()))
    for g, (q_ref, k_ref, v_ref, bias_ref) in enumerate(((q1, k1, v1, b1), (q2, k2, v2, b2), (q3, k3, v3, b3))):
        dil = AT_GROUPS[g][1]
        kw, tiles = _att_tiles(seq // dil)
        qn_ref[...] = _rms_rows(q_ref[0].astype(F32), qg_ref[...])
        kn_ref[...] = _rms_rows(k_ref[0].astype(F32), kg_ref[...])
        vn_ref[...] = v_ref[0].astype(F32)
        for r in range(dil):
            for qs, ks, var in tiles:
                rows = lambda start, size: (pl.ds(r + start * dil, size, stride=dil) if dil > 1
                                            else pl.ds(start, size))
                qt = qn_ref[rows(qs, ATT_TQ), :].astype(BF16)
                kt = kn_ref[rows(ks, kw), :].astype(BF16)
                vt = vn_ref[rows(ks, kw), :].astype(BF16)
                s = lax.dot_general(qt, kt, contract_last, preferred_element_type=F32) * scale + bias_ref[var, 0]
                m = jnp.max(s, axis=-1, keepdims=True)
                p = jnp.exp(s - m)
                z = jnp.sum(p, axis=-1, keepdims=True)
                og_ref[g, rows(qs, ATT_TQ), :] = jnp.dot(p.astype(BF16), vt, preferred_element_type=F32) / z
                lg_ref[g, rows(qs, ATT_TQ), :] = jnp.broadcast_to(m + jnp.log(z), (ATT_TQ, HD))
    l0, l1, l2 = lg_ref[0], lg_ref[1], lg_ref[2]
    mx = jnp.maximum(jnp.maximum(l0, l1), l2)
    e0, e1, e2 = jnp.exp(l0 - mx), jnp.exp(l1 - mx), jnp.exp(l2 - mx)
    o_ref[0] = (e0 * og_ref[0] + e1 * og_ref[1] + e2 * og_ref[2]) / (e0 + e1 + e2)


def _dilated_attention(z3, q_gain, k_gain, rel_bias):
    b, s, _ = z3.shape
    ng = len(AT_GROUPS)
    col = lambda part, g: pl.BlockSpec((1, s, HD), lambda i, h: (i, 0, part * ng * AT_HEADS + g * AT_HEADS + h))
    biases = [_att_bias(rel_bias, g, s) for g in range(ng)]
    bias_spec = lambda a: pl.BlockSpec((a.shape[0], 1) + a.shape[2:], lambda i, h: (0, h, 0, 0))
    gain_spec = pl.BlockSpec((1, HD), lambda i, h: (0, 0))
    return pl.pallas_call(
        _dilated_kernel,
        out_shape=jax.ShapeDtypeStruct((b, s, AT_HEADS * HD), F32),
        grid=(b, AT_HEADS),
        in_specs=[col(p, g) for p in range(3) for g in range(ng)] + [gain_spec, gain_spec]
                 + [bias_spec(a) for a in biases],
        out_specs=pl.BlockSpec((1, s, HD), lambda i, h: (i, 0, h)),
        scratch_shapes=[pltpu.VMEM((s, HD), F32), pltpu.VMEM((s, HD), F32), pltpu.VMEM((s, HD), F32),
                        pltpu.VMEM((ng, s, HD), F32), pltpu.VMEM((ng, s, HD), F32)],
        compiler_params=_params("parallel", "parallel"),
        name="dilated_attention",
    )(*([z3] * 9), q_gain.reshape(1, HD), k_gain.reshape(1, HD), *biases)


ROUTER_LANES = 128


def _router_kernel(x_ref, g_ref, w_ref, h_ref, idx_ref, gate_ref, before_ref, count_ref, carry_ref):
    tm = x_ref.shape[0]

    @pl.when(pl.program_id(0) == 0)
    def _():
        carry_ref[...] = jnp.zeros_like(carry_ref)

    h = _rms_rows(x_ref[...], g_ref[...])
    h_ref[...] = h.astype(BF16)
    logits = jnp.dot(h, w_ref[...], preferred_element_type=F32, precision=lax.Precision.HIGHEST)
    lane = lax.broadcasted_iota(jnp.int32, logits.shape, 1).astype(F32)
    logits = jnp.where(lane < N_EXPERTS, logits, -jnp.inf)
    m1 = jnp.max(logits, axis=-1, keepdims=True)
    i1 = jnp.min(jnp.where(logits == m1, lane, float(ROUTER_LANES)), axis=-1, keepdims=True)
    rest = jnp.where(lane == i1, -jnp.inf, logits)
    m2 = jnp.max(rest, axis=-1, keepdims=True)
    i2 = jnp.min(jnp.where(rest == m2, lane, float(ROUTER_LANES)), axis=-1, keepdims=True)
    e2 = jnp.exp(m2 - m1)
    den = 1.0 + e2
    gate_ref[...] = jnp.where(lane == 0, 1.0 / den, e2 / den)
    chosen = jnp.where((lane == i1) | (lane == i2), 1.0, 0.0)
    earlier = lax.broadcasted_iota(jnp.int32, (tm, tm), 0) > lax.broadcasted_iota(jnp.int32, (tm, tm), 1)
    carry = carry_ref[...]
    before = jnp.dot(jnp.where(earlier, 1.0, 0.0).astype(BF16), chosen.astype(BF16),
                     preferred_element_type=F32) + carry
    r1 = jnp.sum(jnp.where(lane == i1, before, 0.0), axis=-1, keepdims=True)
    r2 = jnp.sum(jnp.where(lane == i2, before, 0.0), axis=-1, keepdims=True)
    idx_ref[...] = jnp.where(lane == 0, i1, jnp.where(lane == 1, i2, jnp.where(lane == 2, r1, r2))).astype(jnp.int32)
    before_ref[0] = carry.astype(jnp.int32)
    carry = carry + jnp.sum(chosen, axis=0, keepdims=True)
    carry_ref[...] = carry
    count_ref[...] = carry.astype(jnp.int32)


def _router(x2, gain, router):
    rows, d = x2.shape
    tm = MOE_CHUNK
    w = jnp.zeros((d, ROUTER_LANES), F32).at[:, :N_EXPERTS].set(router.astype(F32))
    row_spec = lambda width: pl.BlockSpec((tm, width), lambda i: (i, 0))
    return pl.pallas_call(
        _router_kernel,
        out_shape=(jax.ShapeDtypeStruct((rows, d), BF16),
                   jax.ShapeDtypeStruct((rows, ROUTER_LANES), jnp.int32),
                   jax.ShapeDtypeStruct((rows, ROUTER_LANES), F32),
                   jax.ShapeDtypeStruct((rows // tm, 1, ROUTER_LANES), jnp.int32),
                   jax.ShapeDtypeStruct((1, ROUTER_LANES), jnp.int32)),
        grid=(rows // tm,),
        in_specs=[row_spec(d),
                  pl.BlockSpec((1, d), lambda i: (0, 0)),
                  pl.BlockSpec((d, ROUTER_LANES), lambda i: (0, 0))],
        out_specs=(row_spec(d), row_spec(ROUTER_LANES), row_spec(ROUTER_LANES),
                   pl.BlockSpec((1, 1, ROUTER_LANES), lambda i: (i, 0, 0)),
                   pl.BlockSpec((1, ROUTER_LANES), lambda i: (0, 0))),
        scratch_shapes=[pltpu.VMEM((1, ROUTER_LANES), F32)],
        compiler_params=_params("arbitrary"),
        name="router",
    )(x2, gain.reshape(1, d), w)


def _dispatch_kernel(pb_ref, pc_ref, np_ref, h_ref, dest_ref, o_ref):
    s = pl.program_id(0)
    rows, chunk = o_ref.shape[0], h_ref.shape[0]
    active = s < np_ref[0]
    first = jnp.logical_or(s == 0, pb_ref[jnp.maximum(s - 1, 0)] != pb_ref[s])
    prow = pb_ref[s] * rows + lax.broadcasted_iota(jnp.int32, (rows, 1), 0)
    hit = (prow == dest_ref[0, 0:1, :]) | (prow == dest_ref[0, 1:2, :])
    sel = jnp.where(hit, 1.0, 0.0).astype(BF16)

    @pl.when(active & first)
    def _():
        o_ref[...] = jnp.dot(sel, h_ref[...], preferred_element_type=F32).astype(BF16)

    @pl.when(active & jnp.logical_not(first))
    def _():
        o_ref[...] += jnp.dot(sel, h_ref[...], preferred_element_type=F32).astype(BF16)


def _dispatch(h, dest_chunks, pair_blk, pair_chunk, npairs, p_rows):
    d = h.shape[1]
    grid_spec = pltpu.PrefetchScalarGridSpec(
        num_scalar_prefetch=3,
        grid=(pair_blk.shape[0],),
        in_specs=[pl.BlockSpec((MOE_CHUNK, d), lambda s, pb, pc, n: (pc[s], 0)),
                  pl.BlockSpec((1, TOP_K, MOE_CHUNK), lambda s, pb, pc, n: (pc[s], 0, 0))],
        out_specs=pl.BlockSpec((MOE_ROWS, d), lambda s, pb, pc, n: (pb[s], 0)),
    )
    return pl.pallas_call(
        _dispatch_kernel,
        out_shape=jax.ShapeDtypeStruct((p_rows, d), BF16),
        grid_spec=grid_spec,
        compiler_params=_params("arbitrary"),
        name="moe_dispatch",
    )(pair_blk, pair_chunk, npairs, h, dest_chunks)


def _combine_kernel(qc_ref, qb_ref, np_ref, x_ref, y_ref, dest_ref, gate_ref, o_ref):
    s = pl.program_id(0)
    rows = y_ref.shape[0]
    active = s < np_ref[0]
    first = jnp.logical_or(s == 0, qc_ref[jnp.maximum(s - 1, 0)] != qc_ref[s])
    base = qb_ref[s] * rows
    pcol = base + lax.broadcasted_iota(jnp.int32, (1, rows), 1)
    d0, d1 = dest_ref[:, 0:1], dest_ref[:, 1:2]
    sel = jnp.where((d0 == pcol) | (d1 == pcol), 1.0, 0.0).astype(BF16)
    in0 = (d0 >= base) & (d0 < base + rows)
    in1 = (d1 >= base) & (d1 < base + rows)
    gate = jnp.where(in0, gate_ref[:, 0:1], jnp.where(in1, gate_ref[:, 1:2], 0.0))

    @pl.when(active & first)
    def _():
        o_ref[...] = x_ref[...] + gate * jnp.dot(sel, y_ref[...], preferred_element_type=F32)

    @pl.when(active & jnp.logical_not(first))
    def _():
        o_ref[...] += gate * jnp.dot(sel, y_ref[...], preferred_element_type=F32)


def _combine(x2, yb, dest, gates, pair_chunk, pair_blk, npairs):
    t, d = x2.shape
    grid_spec = pltpu.PrefetchScalarGridSpec(
        num_scalar_prefetch=3,
        grid=(pair_blk.shape[0],),
        in_specs=[pl.BlockSpec((MOE_CHUNK, d), lambda s, qc, qb, n: (qc[s], 0)),
                  pl.BlockSpec((MOE_ROWS, d), lambda s, qc, qb, n: (qb[s], 0)),
                  pl.BlockSpec((MOE_CHUNK, TOP_K), lambda s, qc, qb, n: (qc[s], 0)),
                  pl.BlockSpec((MOE_CHUNK, TOP_K), lambda s, qc, qb, n: (qc[s], 0))],
        out_specs=pl.BlockSpec((MOE_CHUNK, d), lambda s, qc, qb, n: (qc[s], 0)),
    )
    return pl.pallas_call(
        _combine_kernel,
        out_shape=jax.ShapeDtypeStruct((t, d), F32),
        grid_spec=grid_spec,
        compiler_params=_params("arbitrary"),
        name="moe_combine",
    )(pair_chunk, pair_blk, npairs, x2, yb, dest, gates)


def _pairs(mask, npairs, size):
    rows, cols = jnp.nonzero(mask, size=size, fill_value=0)
    keep = jnp.arange(size) < npairs
    last = jnp.maximum(npairs - 1, 0)
    return (jnp.where(keep, rows, rows[last]).astype(jnp.int32),
            jnp.where(keep, cols, cols[last]).astype(jnp.int32))


def _moe(x2, gain, router, wg, wu, wd):
    t, d = x2.shape
    nch = t // MOE_CHUNK
    h, idx, gate, before, count = _router(x2, gain, router)
    experts = jnp.arange(N_EXPERTS, dtype=jnp.int32)
    counts = count[0, :N_EXPERTS]
    padded = (counts + MOE_ROWS - 1) // MOE_ROWS * MOE_ROWS
    pend = jnp.cumsum(padded)
    pstart = pend - padded
    expert, rank = idx[:, :TOP_K], idx[:, TOP_K:2 * TOP_K]
    dest = jnp.sum(jnp.where(expert[:, :, None] == experts, pstart, 0), axis=-1) + rank
    p_rows = t * TOP_K + N_EXPERTS * MOE_ROWS
    nblk = p_rows // MOE_ROWS
    blk_row = jnp.arange(nblk, dtype=jnp.int32) * MOE_ROWS
    blk_expert = jnp.minimum(jnp.searchsorted(pend, blk_row, side='right'), N_EXPERTS - 1).astype(jnp.int32)
    nused = pend[-1:].astype(jnp.int32)
    bounds = jnp.concatenate([before[:, 0, :N_EXPERTS], counts[None]], axis=0)
    lo = bounds[:-1].T[blk_expert]
    hi = bounds[1:].T[blk_expert]
    rank0 = (blk_row - pstart[blk_expert])[:, None]
    rank1 = jnp.minimum(rank0 + MOE_ROWS, counts[blk_expert][:, None])
    share = (hi > lo) & (lo < rank1) & (hi > rank0) & (blk_row < nused[0])[:, None]
    npairs = jnp.sum(share).astype(jnp.int32)
    max_pairs = nch * N_EXPERTS + nblk
    pair_blk, pair_chunk = _pairs(share, npairs, max_pairs)
    by_chunk, by_chunk_blk = _pairs(share.T, npairs, max_pairs)
    npairs = npairs.reshape(1)
    dest_chunks = dest.T.reshape(TOP_K, nch, MOE_CHUNK).transpose(1, 0, 2)
    hb = _dispatch(h, dest_chunks, pair_blk, pair_chunk, npairs, p_rows)
    yb = _ffn_experts(hb, wg, wu, wd, blk_expert, nused)
    return _combine(x2, yb, dest, gate[:, :TOP_K], by_chunk, by_chunk_blk, npairs)


def kernel(x, mem, rel_bias, norm_mix, norm_mem, norm_ffn, w_mem_kv, xq_norm, xk_norm, w_out, hy_w_in, hy_conv_w, hy_conv_b, hy_filt_w1, hy_filt_b1, hy_filt_w2, hy_filt_b2, hy_filt_w3, hy_sin_freq, hy_skip, at_w_in, at_q_norm, at_k_norm, ffn_w_gate, ffn_w_up, ffn_w_down, moe_router, moe_w_gate, moe_w_up, moe_w_down):
    b, s, d = x.shape
    t = b * s
    m_len = mem.shape[1]
    x2 = x.reshape(t, d)
    mem2 = mem.reshape(b * m_len, d)
    bf = lambda w: w.astype(BF16)

    cmat, smat = _dft_tables(s)
    spectra, hn = _hyena_filters(s, cmat, smat, hy_filt_w1[0], hy_filt_b1[0], hy_filt_w2[0], hy_filt_b2[0],
                                 hy_filt_w3[0], hy_sin_freq[0])
    z = _norm_matmul(x2, norm_mix[0], bf(hy_w_in[0]), 1024).reshape(b, s, -1)
    kv = _norm_matmul(mem2, norm_mem[0], bf(w_mem_kv[0]), 1024).reshape(b, m_len, -1)
    self_out = _hyena(z, hy_conv_w[0], hy_conv_b[0], cmat, smat, spectra, hn, hy_skip[0])
    cross = _xattn(z, 3 * HY_C // XA_W, kv, xq_norm[0], xk_norm[0])
    x2 = _outproj(x2, self_out.reshape(t, -1), cross.reshape(t, -1), bf(w_out[0]))
    x2 = _ffn_dense(x2, norm_ffn[0], bf(ffn_w_gate), bf(ffn_w_up), bf(ffn_w_down))

    z = _norm_matmul(x2, norm_mix[1], bf(at_w_in[0]), 1024).reshape(b, s, -1)
    kv = _norm_matmul(mem2, norm_mem[1], bf(w_mem_kv[1]), 1024).reshape(b, m_len, -1)
    self_out = _dilated_attention(z, at_q_norm[0], at_k_norm[0], rel_bias)
    cross = _xattn(z, 3 * AT_W // XA_W, kv, xq_norm[1], xk_norm[1])
    x2 = _outproj(x2, self_out.reshape(t, -1), cross.reshape(t, -1), bf(w_out[1]))
    x2 = _moe(x2, norm_ffn[1], moe_router[0], bf(moe_w_gate[0]), bf(moe_w_up[0]), bf(moe_w_down[0]))
    return x2.reshape(b, s, d)
```

```python
import functools
import math

import jax
import jax.numpy as jnp
import numpy as np
from jax import lax
from jax.experimental import pallas as pl
from jax.experimental.pallas import tpu as pltpu

F32 = jnp.float32
BF16 = jnp.bfloat16

D_MODEL = 1024
EPS = 1e-6
HY_C = 512
FILT_BANDS = 16
DECAY_TARGET = 1e-2
FAST_DECAY_PCT = 0.3
SLOW_DECAY_PCT = 1.5
MOD_SHIFT = 0.05
AT_GROUPS = ((128, 1), (512, 4), (2048, 16))
AT_HEADS = 4
HD = 128
AT_W = 1536
NUM_BUCKETS = 32
REL_MAX_DIST = 1024
NEG_INF = -1e30
XA_W = 512
D_FF = 2816
N_EXPERTS = 8
TOP_K = 2

VMEM_LIMIT_BYTES = 56 * 1024 * 1024
ROW_TILE = 512
FF_TILE = 1408
MOE_ROWS = 512
MOE_CHUNK = 512
ATT_TQ = 128
ATT_KW = 256
HY_TC = 256
HY_KT = 512


def _params(*sem):
    return pltpu.CompilerParams(dimension_semantics=sem, vmem_limit_bytes=VMEM_LIMIT_BYTES)


def _rms_rows(x, gain):
    return x * lax.rsqrt(jnp.mean(x * x, axis=-1, keepdims=True) + EPS) * gain


def _resident(shape, index_map):
    return pl.BlockSpec(shape, index_map, pipeline_mode=pl.Buffered(1))


def _norm_matmul_kernel(x_ref, g_ref, w_ref, o_ref, *, tn):
    h = _rms_rows(x_ref[...], g_ref[...]).astype(BF16)
    for c in range(o_ref.shape[1] // tn):
        cols = slice(c * tn, (c + 1) * tn)
        o_ref[:, cols] = jnp.dot(h, w_ref[:, cols], preferred_element_type=F32).astype(o_ref.dtype)


def _norm_matmul(x2, gain, w_bf, tn):
    rows, d = x2.shape
    n = w_bf.shape[1]
    tm = min(ROW_TILE, rows)
    return pl.pallas_call(
        functools.partial(_norm_matmul_kernel, tn=tn),
        out_shape=jax.ShapeDtypeStruct((rows, n), BF16),
        grid=(rows // tm,),
        in_specs=[pl.BlockSpec((tm, d), lambda i: (i, 0)),
                  pl.BlockSpec((1, d), lambda i: (0, 0)),
                  _resident((d, n), lambda i: (0, 0))],
        out_specs=pl.BlockSpec((tm, n), lambda i: (i, 0)),
        compiler_params=_params("parallel"),
        name="norm_matmul",
    )(x2, gain.reshape(1, d), w_bf)


def _outproj_kernel(x_ref, a_ref, c_ref, wa_ref, wc_ref, o_ref):
    o_ref[...] = (x_ref[...]
                  + jnp.dot(a_ref[...].astype(BF16), wa_ref[...], preferred_element_type=F32)
                  + jnp.dot(c_ref[...].astype(BF16), wc_ref[...], preferred_element_type=F32))


def _outproj(x2, self_out, cross, w_bf):
    rows, d = x2.shape
    half = self_out.shape[1]
    tm = ROW_TILE
    return pl.pallas_call(
        _outproj_kernel,
        out_shape=jax.ShapeDtypeStruct((rows, d), F32),
        grid=(rows // tm,),
        in_specs=[pl.BlockSpec((tm, d), lambda i: (i, 0)),
                  pl.BlockSpec((tm, half), lambda i: (i, 0)),
                  pl.BlockSpec((tm, half), lambda i: (i, 0)),
                  pl.BlockSpec((half, d), lambda i: (0, 0)),
                  pl.BlockSpec((half, d), lambda i: (1, 0))],
        out_specs=pl.BlockSpec((tm, d), lambda i: (i, 0)),
        compiler_params=_params("parallel"),
        name="outproj",
    )(x2, self_out, cross, w_bf, w_bf)


def _swiglu_accumulate(h, wg_ref, wu_ref, wd_ref, acc_ref):
    gg = jnp.dot(h, wg_ref[0], preferred_element_type=F32)
    uu = jnp.dot(h, wu_ref[0], preferred_element_type=F32)
    a = (gg * jax.nn.sigmoid(gg)) * uu
    acc_ref[...] += jnp.dot(a.astype(BF16), wd_ref[0], preferred_element_type=F32)


def _ffn_dense_kernel(x_ref, g_ref, wg_ref, wu_ref, wd_ref, o_ref, h_ref, acc_ref):
    j = pl.program_id(1)

    @pl.when(j == 0)
    def _():
        h_ref[...] = _rms_rows(x_ref[...], g_ref[...]).astype(BF16)
        acc_ref[...] = jnp.zeros_like(acc_ref)

    _swiglu_accumulate(h_ref[...], wg_ref, wu_ref, wd_ref, acc_ref)

    @pl.when(j == pl.num_programs(1) - 1)
    def _():
        o_ref[...] = x_ref[...] + acc_ref[...]


def _ffn_dense(x2, gain, wg, wu, wd):
    rows, d = x2.shape
    f = wg.shape[2]
    tm, tf = ROW_TILE, FF_TILE
    return pl.pallas_call(
        _ffn_dense_kernel,
        out_shape=jax.ShapeDtypeStruct((rows, d), F32),
        grid=(rows // tm, f // tf),
        in_specs=[pl.BlockSpec((tm, d), lambda i, j: (i, 0)),
                  pl.BlockSpec((1, d), lambda i, j: (0, 0)),
                  pl.BlockSpec((1, d, tf), lambda i, j: (0, 0, j)),
                  pl.BlockSpec((1, d, tf), lambda i, j: (0, 0, j)),
                  pl.BlockSpec((1, tf, d), lambda i, j: (0, j, 0))],
        out_specs=pl.BlockSpec((tm, d), lambda i, j: (i, 0)),
        scratch_shapes=[pltpu.VMEM((tm, d), BF16), pltpu.VMEM((tm, d), F32)],
        compiler_params=_params("parallel", "arbitrary"),
        name="ffn_dense",
    )(x2, gain.reshape(1, d), wg, wu, wd)


def _ffn_expert_kernel(eid_ref, nused_ref, h_ref, wg_ref, wu_ref, wd_ref, o_ref, acc_ref):
    i, j = pl.program_id(0), pl.program_id(1)

    @pl.when(j == 0)
    def _():
        acc_ref[...] = jnp.zeros_like(acc_ref)

    @pl.when(i * h_ref.shape[0] < nused_ref[0])
    def _():
        _swiglu_accumulate(h_ref[...], wg_ref, wu_ref, wd_ref, acc_ref)

    @pl.when(j == pl.num_programs(1) - 1)
    def _():
        o_ref[...] = acc_ref[...].astype(o_ref.dtype)


def _ffn_experts(hb, wg, wu, wd, eid, nused):
    rows, d = hb.shape
    f = wg.shape[2]
    tm, tf = MOE_ROWS, FF_TILE
    grid_spec = pltpu.PrefetchScalarGridSpec(
        num_scalar_prefetch=2,
        grid=(rows // tm, f // tf),
        in_specs=[pl.BlockSpec((tm, d), lambda i, j, e, n: (i, 0)),
                  pl.BlockSpec((1, d, tf), lambda i, j, e, n: (e[i], 0, j)),
                  pl.BlockSpec((1, d, tf), lambda i, j, e, n: (e[i], 0, j)),
                  pl.BlockSpec((1, tf, d), lambda i, j, e, n: (e[i], j, 0))],
        out_specs=pl.BlockSpec((tm, d), lambda i, j, e, n: (i, 0)),
        scratch_shapes=[pltpu.VMEM((tm, d), F32)],
    )
    return pl.pallas_call(
        _ffn_expert_kernel,
        out_shape=jax.ShapeDtypeStruct((rows, d), BF16),
        grid_spec=grid_spec,
        compiler_params=_params("parallel", "arbitrary"),
        name="ffn_experts",
    )(eid, nused, hb, wg, wu, wd)


def _xattn_kernel(xq_ref, kv_ref, qg_ref, kg_ref, o_ref):
    scale = HD ** -0.5
    for h in range(XA_W // HD):
        cols = slice(h * HD, (h + 1) * HD)
        q = _rms_rows(xq_ref[0, :, cols].astype(F32), qg_ref[...]).astype(BF16)
        k = _rms_rows(kv_ref[0, :, cols].astype(F32), kg_ref[...]).astype(BF16)
        v = kv_ref[0, :, XA_W + h * HD:XA_W + (h + 1) * HD]
        s = lax.dot_general(q, k, (((1,), (1,)), ((), ())), preferred_element_type=F32) * scale
        p = jnp.exp(s - jnp.max(s, axis=-1, keepdims=True))
        z = jnp.sum(p, axis=-1, keepdims=True)
        o_ref[0, :, cols] = jnp.dot(p.astype(BF16), v, preferred_element_type=F32) / z


def _xattn(z3, xq_block, kv3, q_gain, k_gain):
    b, s, _ = z3.shape
    m = kv3.shape[1]
    ts = ROW_TILE
    return pl.pallas_call(
        _xattn_kernel,
        out_shape=jax.ShapeDtypeStruct((b, s, XA_W), F32),
        grid=(b, s // ts),
        in_specs=[pl.BlockSpec((1, ts, XA_W), lambda i, j: (i, j, xq_block)),
                  pl.BlockSpec((1, m, 2 * XA_W), lambda i, j: (i, 0, 0)),
                  pl.BlockSpec((1, HD), lambda i, j: (0, 0)),
                  pl.BlockSpec((1, HD), lambda i, j: (0, 0))],
        out_specs=pl.BlockSpec((1, ts, XA_W), lambda i, j: (i, j, 0)),
        compiler_params=_params("parallel", "parallel"),
        name="cross_attention",
    )(z3, kv3, q_gain.reshape(1, HD), k_gain.reshape(1, HD))


def _dft_kernel(c_ref, s_ref, c0_ref, s0_ref, *, n_fft):
    rows, cols = c_ref.shape
    i = pl.program_id(0)

    @pl.when(i == 0)
    def _():
        k = lax.broadcasted_iota(jnp.int32, (rows, cols), 0)
        n = lax.broadcasted_iota(jnp.int32, (rows, cols), 1)
        ang = ((k * n) & (n_fft - 1)).astype(F32) * (2.0 * math.pi / n_fft)
        c0_ref[...] = jnp.cos(ang)
        s0_ref[...] = jnp.sin(ang)

    period = n_fft // rows
    n = lax.broadcasted_iota(jnp.int32, (8, cols), 1)
    shift = ((i * n) & (period - 1)).astype(F32) * (2.0 * math.pi / period)
    ca = jnp.cos(shift)[0:1]
    sa = jnp.sin(shift)[0:1]
    c0 = c0_ref[...]
    s0 = s0_ref[...]
    c_ref[...] = (ca * c0 - sa * s0).astype(BF16)
    s_ref[...] = (sa * c0 + ca * s0).astype(BF16)


def _dft_tables(length):
    rows = 256
    shape = jax.ShapeDtypeStruct((length, length), BF16)
    spec = pl.BlockSpec((rows, length), lambda i: (i, 0))
    return pl.pallas_call(
        functools.partial(_dft_kernel, n_fft=2 * length),
        out_shape=(shape, shape),
        grid=(length // rows,),
        out_specs=(spec, spec),
        scratch_shapes=[pltpu.VMEM((rows, length), F32), pltpu.VMEM((rows, length), F32)],
        compiler_params=_params("arbitrary"),
        name="dft_tables",
    )()


def _filter_time_kernel(feats_ref, w1_ref, b1_ref, w2_ref, b2_ref, fr_ref, w3_ref, t_ref, delta_ref,
                        gp_ref, gm_ref, hn_ref):
    hp = lax.Precision.HIGHEST
    length = feats_ref.shape[0]
    fr = fr_ref[...]
    h = jnp.sin(fr * (jnp.dot(feats_ref[...], w1_ref[...], preferred_element_type=F32, precision=hp) + b1_ref[...]))
    h = jnp.sin(fr * (jnp.dot(h, w2_ref[...], preferred_element_type=F32, precision=hp) + b2_ref[...]))
    mod = jnp.exp(-t_ref[...] * delta_ref[...]) + MOD_SHIFT
    row = lax.broadcasted_iota(jnp.int32, (length, 1), 0)
    alt = jnp.where(row % 2 == 0, 1.0, -1.0).astype(F32)
    for o in range(2):
        fwd = jnp.dot(h, w3_ref[2 * o], preferred_element_type=F32, precision=hp) * mod
        bwd = jnp.dot(h, w3_ref[2 * o + 1], preferred_element_type=F32, precision=hp) * mod
        bwd = jnp.where(row == 0, 0.0, bwd)
        norm = (jnp.sum(jnp.abs(fwd), axis=0, keepdims=True)
                + jnp.sum(jnp.abs(bwd), axis=0, keepdims=True) + 1e-6)
        fwd = fwd / norm
        bwd = bwd / norm
        gp = fwd + bwd
        gp_ref[o] = gp.astype(BF16)
        gm_ref[o] = (fwd - bwd).astype(BF16)
        hn_ref[o] = jnp.sum(gp * alt, axis=0, keepdims=True) * (1.0 / (2 * length))


def _filter_freq_kernel(c_ref, s_ref, gp_ref, gm_ref, h_ref):
    length = c_ref.shape[0]
    row = lax.broadcasted_iota(jnp.int32, (length, 1), 0)
    scale = jnp.where(row == 0, 1.0, 2.0).astype(F32) * (1.0 / (2 * length))
    h_ref[0, 0] = jnp.dot(c_ref[...], gp_ref[0], preferred_element_type=F32) * scale
    h_ref[0, 1] = -jnp.dot(s_ref[...], gm_ref[0], preferred_element_type=F32) * scale


def _hyena_filters(length, cmat, smat, w1, b1, w2, b2, w3, freq):
    t = jnp.linspace(0.0, 1.0, length, dtype=F32)[:, None]
    f = jnp.linspace(1e-4, FILT_BANDS - 1, FILT_BANDS, dtype=F32)[None]
    ang = (2.0 * math.pi / length) * jnp.arange(length, dtype=F32)[:, None] * f
    feats = jnp.concatenate([t, jnp.cos(ang), -jnp.sin(ang)], axis=-1)
    deltas = jnp.abs(jnp.linspace(math.log(DECAY_TARGET) / SLOW_DECAY_PCT,
                                  math.log(DECAY_TARGET) / FAST_DECAY_PCT, HY_C, dtype=F32))[None]
    hid = w1.shape[1]
    w3r = w3.reshape(hid, 4, HY_C).transpose(1, 0, 2)
    tc = HY_TC
    full = lambda shape: pl.BlockSpec(shape, lambda c: (0,) * len(shape))
    gp, gm, hn = pl.pallas_call(
        _filter_time_kernel,
        out_shape=(jax.ShapeDtypeStruct((2, length, HY_C), BF16),
                   jax.ShapeDtypeStruct((2, length, HY_C), BF16),
                   jax.ShapeDtypeStruct((2, 1, HY_C), F32)),
        grid=(HY_C // tc,),
        in_specs=[full(feats.shape), full(w1.shape), full((1, hid)), full(w2.shape), full((1, hid)),
                  full((1, hid)), pl.BlockSpec((4, hid, tc), lambda c: (0, 0, c)), full((length, 1)),
                  pl.BlockSpec((1, tc), lambda c: (0, c))],
        out_specs=(pl.BlockSpec((2, length, tc), lambda c: (0, 0, c)),
                   pl.BlockSpec((2, length, tc), lambda c: (0, 0, c)),
                   pl.BlockSpec((2, 1, tc), lambda c: (0, 0, c))),
        compiler_params=_params("parallel"),
        name="filter_time",
    )(feats, w1, b1.reshape(1, hid), w2, b2.reshape(1, hid), freq.reshape(1, hid), w3r, t, deltas)
    spectra = pl.pallas_call(
        _filter_freq_kernel,
        out_shape=jax.ShapeDtypeStruct((2, 2, length, HY_C), F32),
        grid=(2, HY_C // tc),
        in_specs=[_resident((length, length), lambda o, c: (0, 0)),
                  _resident((length, length), lambda o, c: (0, 0)),
                  pl.BlockSpec((1, length, tc), lambda o, c: (o, 0, c)),
                  pl.BlockSpec((1, length, tc), lambda o, c: (o, 0, c))],
        out_specs=pl.BlockSpec((1, 2, length, tc), lambda o, c: (o, 0, 0, c)),
        compiler_params=_params("parallel", "parallel"),
        name="filter_freq",
    )(cmat, smat, gp, gm)
    return spectra, hn


def _hyena_kernel(z0_ref, z1_ref, z2_ref, cw_ref, cb_ref, c_ref, s_ref, h_ref, hn_ref, skip_ref, o_ref,
                  ubf_ref, y_ref):
    length = z0_ref.shape[1]
    row = lax.broadcasted_iota(jnp.int32, (length, 1), 0)
    alt = jnp.where(row % 2 == 0, 1.0, -1.0).astype(F32)

    def short_conv(z_ref, c):
        u = z_ref[0].astype(F32)
        prev = jnp.where(row == 0, 0.0, pltpu.roll(u, 1, axis=0))
        nxt = jnp.where(row == length - 1, 0.0, pltpu.roll(u, length - 1, axis=0))
        return cb_ref[c] + (prev * cw_ref[0, c] + u * cw_ref[1, c] + nxt * cw_ref[2, c])

    def long_conv(u, o):
        ubf_ref[...] = u.astype(BF16)
        nyq = jnp.sum(u * alt, axis=0, keepdims=True) * hn_ref[o]
        y_ref[...] = alt * nyq + u * skip_ref[o]
        for kt in range(length // HY_KT):
            ks = slice(kt * HY_KT, (kt + 1) * HY_KT)
            ub = ubf_ref[...]
            a = jnp.dot(c_ref[ks, :], ub, preferred_element_type=F32)
            b = jnp.dot(s_ref[ks, :], ub, preferred_element_type=F32)
            hr = h_ref[o, 0, ks, :]
            hi = h_ref[o, 1, ks, :]
            re = (a * hr + b * hi).astype(BF16)
            im = (a * hi - b * hr).astype(BF16)
            y_ref[...] += (jnp.dot(c_ref[:, ks], re, preferred_element_type=F32)
                           - jnp.dot(s_ref[:, ks], im, preferred_element_type=F32))
        return y_ref[...]

    z = short_conv(z1_ref, 1) * long_conv(short_conv(z0_ref, 0), 0)
    o_ref[0] = short_conv(z2_ref, 2) * long_conv(z, 1)


def _hyena(z3, conv_w, conv_b, cmat, smat, spectra, hn, skip):
    b, s, _ = z3.shape
    tc = HY_TC
    nct = HY_C // tc
    cw = conv_w.reshape(3, 3, 1, HY_C)
    cb = conv_b.reshape(3, 1, HY_C)
    zspec = lambda chunk: pl.BlockSpec((1, s, tc), lambda c, i: (i, 0, chunk * nct + c))
    return pl.pallas_call(
        _hyena_kernel,
        out_shape=jax.ShapeDtypeStruct((b, s, HY_C), F32),
        grid=(nct, b),
        in_specs=[zspec(0), zspec(1), zspec(2),
                  pl.BlockSpec((3, 3, 1, tc), lambda c, i: (0, 0, 0, c)),
                  pl.BlockSpec((3, 1, tc), lambda c, i: (0, 0, c)),
                  _resident((s, s), lambda c, i: (0, 0)),
                  _resident((s, s), lambda c, i: (0, 0)),
                  _resident((2, 2, s, tc), lambda c, i: (0, 0, 0, c)),
                  pl.BlockSpec((2, 1, tc), lambda c, i: (0, 0, c)),
                  pl.BlockSpec((2, 1, tc), lambda c, i: (0, 0, c))],
        out_specs=pl.BlockSpec((1, s, tc), lambda c, i: (i, 0, c)),
        scratch_shapes=[pltpu.VMEM((s, tc), BF16), pltpu.VMEM((s, tc), F32)],
        compiler_params=_params("parallel", "parallel"),
        name="hyena",
    )(z3, z3, z3, cw, cb, cmat, smat, spectra, hn, skip.reshape(2, 1, HY_C))


def _rel_bucket(rel):
    half = NUM_BUCKETS // 2
    exact = half // 2
    n = np.abs(rel)
    large = exact + (np.log(np.maximum(n, 1) / exact) / np.log(REL_MAX_DIST / exact) * (half - exact)).astype(np.int32)
    large = np.minimum(large, half - 1)
    return (np.where(rel > 0, half, 0) + np.where(n < exact, n, large)).astype(np.int32)


def _att_tiles(length):
    kw = min(ATT_KW, length)
    tiles = []
    for qs in range(0, length, ATT_TQ):
        ks = min(max(qs - (kw - ATT_TQ) // 2, 0), length - kw)
        tiles.append((qs, ks, {0: 0, -64: 1, -128: 2}[ks - qs]))
    return kw, tiles


def _att_bias(rel_bias, group, seq):
    window, dil = AT_GROUPS[group]
    band = window // (2 * dil)
    kw, tiles = _att_tiles(seq // dil)
    offsets = sorted({ks - qs for qs, ks, _ in tiles}, reverse=True)
    table = rel_bias[:, group * AT_HEADS:(group + 1) * AT_HEADS].astype(F32)
    span = kw + ATT_TQ
    out = []
    for off in offsets:
        k = np.arange(span)
        rel = off + np.where(k < kw, k, k - span)
        diag = jnp.where(np.abs(rel) <= band, table[_rel_bucket(rel * dil)].T, NEG_INF)
        flat = jnp.tile(diag, (1, ATT_TQ))[:, :ATT_TQ * (span - 1)]
        out.append(flat.reshape(AT_HEADS, ATT_TQ, span - 1)[:, :, :kw])
    return jnp.stack(out)


def _dilated_kernel(q1, q2, q3, k1, k2, k3, v1, v2, v3, qg_ref, kg_ref, b1, b2, b3, o_ref,
                    qn_ref, kn_ref, vn_ref, og_ref, lg_ref):
    seq = o_ref.shape[1]
    scale = HD ** -0.5
    contract_last = (((1,), (1,)), ((), ()))
    for g, (q_ref, k_ref, v_ref, bias_ref) in enumerate(((q1, k1, v1, b1), (q2, k2, v2, b2), (q3, k3, v3, b3))):
        dil = AT_GROUPS[g][1]
        kw, tiles = _att_tiles(seq // dil)
        qn_ref[...] = _rms_rows(q_ref[0].astype(F32), qg_ref[...])
        kn_ref[...] = _rms_rows(k_ref[0].astype(F32), kg_ref[...])
        vn_ref[...] = v_ref[0].astype(F32)
        for r in range(dil):
            for qs, ks, var in tiles:
                rows = lambda start, size: (pl.ds(r + start * dil, size, stride=dil) if dil > 1
                                            else pl.ds(start, size))
                qt = qn_ref[rows(qs, ATT_TQ), :].astype(BF16)
                kt = kn_ref[rows(ks, kw), :].astype(BF16)
                vt = vn_ref[rows(ks, kw), :].astype(BF16)
                s = lax.dot_general(qt, kt, contract_last, preferred_element_type=F32) * scale + bias_ref[var, 0]
                m = jnp.max(s, axis=-1, keepdims=True)
                p = jnp.exp(s - m)
                z = jnp.sum(p, axis=-1, keepdims=True)
                og_ref[g, rows(qs, ATT_TQ), :] = jnp.dot(p.astype(BF16), vt, preferred_element_type=F32) / z
                lg_ref[g, rows(qs, ATT_TQ), :] = jnp.broadcast_to(m + jnp.log(z), (ATT_TQ, HD))
    l0, l1, l2 = lg_ref[0], lg_ref[1], lg_ref[2]
    mx = jnp.maximum(jnp.maximum(l0, l1), l2)
    e0, e1, e2 = jnp.exp(l0 - mx), jnp.exp(l1 - mx), jnp.exp(l2 - mx)
    o_ref[0] = (e0 * og_ref[0] + e1 * og_ref[1] + e2 * og_ref[2]) / (e0 + e1 + e2)


def _dilated_attention(z3, q_gain, k_gain, rel_bias):
    b, s, _ = z3.shape
    ng = len(AT_GROUPS)
    col = lambda part, g: pl.BlockSpec((1, s, HD), lambda i, h: (i, 0, part * ng * AT_HEADS + g * AT_HEADS + h))
    biases = [_att_bias(rel_bias, g, s) for g in range(ng)]
    bias_spec = lambda a: pl.BlockSpec((a.shape[0], 1) + a.shape[2:], lambda i, h: (0, h, 0, 0))
    gain_spec = pl.BlockSpec((1, HD), lambda i, h: (0, 0))
    return pl.pallas_call(
        _dilated_kernel,
        out_shape=jax.ShapeDtypeStruct((b, s, AT_HEADS * HD), F32),
        grid=(b, AT_HEADS),
        in_specs=[col(p, g) for p in range(3) for g in range(ng)] + [gain_spec, gain_spec]
                 + [bias_spec(a) for a in biases],
        out_specs=pl.BlockSpec((1, s, HD), lambda i, h: (i, 0, h)),
        scratch_shapes=[pltpu.VMEM((s, HD), F32), pltpu.VMEM((s, HD), F32), pltpu.VMEM((s, HD), F32),
                        pltpu.VMEM((ng, s, HD), F32), pltpu.VMEM((ng, s, HD), F32)],
        compiler_params=_params("parallel", "parallel"),
        name="dilated_attention",
    )(*([z3] * 9), q_gain.reshape(1, HD), k_gain.reshape(1, HD), *biases)


ROUTER_LANES = 128


def _router_kernel(x_ref, g_ref, w_ref, h_ref, idx_ref, gate_ref, before_ref, count_ref, carry_ref):
    tm = x_ref.shape[0]

    @pl.when(pl.program_id(0) == 0)
    def _():
        carry_ref[...] = jnp.zeros_like(carry_ref)

    h = _rms_rows(x_ref[...], g_ref[...])
    h_ref[...] = h.astype(BF16)
    logits = jnp.dot(h, w_ref[...], preferred_element_type=F32, precision=lax.Precision.HIGHEST)
    lane = lax.broadcasted_iota(jnp.int32, logits.shape, 1).astype(F32)
    logits = jnp.where(lane < N_EXPERTS, logits, -jnp.inf)
    m1 = jnp.max(logits, axis=-1, keepdims=True)
    i1 = jnp.min(jnp.where(logits == m1, lane, float(ROUTER_LANES)), axis=-1, keepdims=True)
    rest = jnp.where(lane == i1, -jnp.inf, logits)
    m2 = jnp.max(rest, axis=-1, keepdims=True)
    i2 = jnp.min(jnp.where(rest == m2, lane, float(ROUTER_LANES)), axis=-1, keepdims=True)
    e2 = jnp.exp(m2 - m1)
    den = 1.0 + e2
    gate_ref[...] = jnp.where(lane == 0, 1.0 / den, e2 / den)
    chosen = jnp.where((lane == i1) | (lane == i2), 1.0, 0.0)
    earlier = lax.broadcasted_iota(jnp.int32, (tm, tm), 0) > lax.broadcasted_iota(jnp.int32, (tm, tm), 1)
    carry = carry_ref[...]
    before = jnp.dot(jnp.where(earlier, 1.0, 0.0).astype(BF16), chosen.astype(BF16),
                     preferred_element_type=F32) + carry
    r1 = jnp.sum(jnp.where(lane == i1, before, 0.0), axis=-1, keepdims=True)
    r2 = jnp.sum(jnp.where(lane == i2, before, 0.0), axis=-1, keepdims=True)
    idx_ref[...] = jnp.where(lane == 0, i1, jnp.where(lane == 1, i2, jnp.where(lane == 2, r1, r2))).astype(jnp.int32)
    before_ref[0] = carry.astype(jnp.int32)
    carry = carry + jnp.sum(chosen, axis=0, keepdims=True)
    carry_ref[...] = carry
    count_ref[...] = carry.astype(jnp.int32)


def _router(x2, gain, router):
    rows, d = x2.shape
    tm = MOE_CHUNK
    w = jnp.zeros((d, ROUTER_LANES), F32).at[:, :N_EXPERTS].set(router.astype(F32))
    row_spec = lambda width: pl.BlockSpec((tm, width), lambda i: (i, 0))
    return pl.pallas_call(
        _router_kernel,
        out_shape=(jax.ShapeDtypeStruct((rows, d), BF16),
                   jax.ShapeDtypeStruct((rows, ROUTER_LANES), jnp.int32),
                   jax.ShapeDtypeStruct((rows, ROUTER_LANES), F32),
                   jax.ShapeDtypeStruct((rows // tm, 1, ROUTER_LANES), jnp.int32),
                   jax.ShapeDtypeStruct((1, ROUTER_LANES), jnp.int32)),
        grid=(rows // tm,),
        in_specs=[row_spec(d),
                  pl.BlockSpec((1, d), lambda i: (0, 0)),
                  pl.BlockSpec((d, ROUTER_LANES), lambda i: (0, 0))],
        out_specs=(row_spec(d), row_spec(ROUTER_LANES), row_spec(ROUTER_LANES),
                   pl.BlockSpec((1, 1, ROUTER_LANES), lambda i: (i, 0, 0)),
                   pl.BlockSpec((1, ROUTER_LANES), lambda i: (0, 0))),
        scratch_shapes=[pltpu.VMEM((1, ROUTER_LANES), F32)],
        compiler_params=_params("arbitrary"),
        name="router",
    )(x2, gain.reshape(1, d), w)


def _dispatch_kernel(pb_ref, pc_ref, np_ref, h_ref, dest_ref, o_ref):
    s = pl.program_id(0)
    rows, chunk = o_ref.shape[0], h_ref.shape[0]
    active = s < np_ref[0]
    first = jnp.logical_or(s == 0, pb_ref[jnp.maximum(s - 1, 0)] != pb_ref[s])
    prow = pb_ref[s] * rows + lax.broadcasted_iota(jnp.int32, (rows, 1), 0)
    hit = (prow == dest_ref[0, 0:1, :]) | (prow == dest_ref[0, 1:2, :])
    sel = jnp.where(hit, 1.0, 0.0).astype(BF16)

    @pl.when(active & first)
    def _():
        o_ref[...] = jnp.dot(sel, h_ref[...], preferred_element_type=F32).astype(BF16)

    @pl.when(active & jnp.logical_not(first))
    def _():
        o_ref[...] += jnp.dot(sel, h_ref[...], preferred_element_type=F32).astype(BF16)


def _dispatch(h, dest_chunks, pair_blk, pair_chunk, npairs, p_rows):
    d = h.shape[1]
    grid_spec = pltpu.PrefetchScalarGridSpec(
        num_scalar_prefetch=3,
        grid=(pair_blk.shape[0],),
        in_specs=[pl.BlockSpec((MOE_CHUNK, d), lambda s, pb, pc, n: (pc[s], 0)),
                  pl.BlockSpec((1, TOP_K, MOE_CHUNK), lambda s, pb, pc, n: (pc[s], 0, 0))],
        out_specs=pl.BlockSpec((MOE_ROWS, d), lambda s, pb, pc, n: (pb[s], 0)),
    )
    return pl.pallas_call(
        _dispatch_kernel,
        out_shape=jax.ShapeDtypeStruct((p_rows, d), BF16),
        grid_spec=grid_spec,
        compiler_params=_params("arbitrary"),
        name="moe_dispatch",
    )(pair_blk, pair_chunk, npairs, h, dest_chunks)


def _combine_kernel(qc_ref, qb_ref, np_ref, x_ref, y_ref, dest_ref, gate_ref, o_ref):
    s = pl.program_id(0)
    rows = y_ref.shape[0]
    active = s < np_ref[0]
    first = jnp.logical_or(s == 0, qc_ref[jnp.maximum(s - 1, 0)] != qc_ref[s])
    base = qb_ref[s] * rows
    pcol = base + lax.broadcasted_iota(jnp.int32, (1, rows), 1)
    d0, d1 = dest_ref[:, 0:1], dest_ref[:, 1:2]
    sel = jnp.where((d0 == pcol) | (d1 == pcol), 1.0, 0.0).astype(BF16)
    in0 = (d0 >= base) & (d0 < base + rows)
    in1 = (d1 >= base) & (d1 < base + rows)
    gate = jnp.where(in0, gate_ref[:, 0:1], jnp.where(in1, gate_ref[:, 1:2], 0.0))

    @pl.when(active & first)
    def _():
        o_ref[...] = x_ref[...] + gate * jnp.dot(sel, y_ref[...], preferred_element_type=F32)

    @pl.when(active & jnp.logical_not(first))
    def _():
        o_ref[...] += gate * jnp.dot(sel, y_ref[...], preferred_element_type=F32)


def _combine(x2, yb, dest, gates, pair_chunk, pair_blk, npairs):
    t, d = x2.shape
    grid_spec = pltpu.PrefetchScalarGridSpec(
        num_scalar_prefetch=3,
        grid=(pair_blk.shape[0],),
        in_specs=[pl.BlockSpec((MOE_CHUNK, d), lambda s, qc, qb, n: (qc[s], 0)),
                  pl.BlockSpec((MOE_ROWS, d), lambda s, qc, qb, n: (qb[s], 0)),
                  pl.BlockSpec((MOE_CHUNK, TOP_K), lambda s, qc, qb, n: (qc[s], 0)),
                  pl.BlockSpec((MOE_CHUNK, TOP_K), lambda s, qc, qb, n: (qc[s], 0))],
        out_specs=pl.BlockSpec((MOE_CHUNK, d), lambda s, qc, qb, n: (qc[s], 0)),
    )
    return pl.pallas_call(
        _combine_kernel,
        out_shape=jax.ShapeDtypeStruct((t, d), F32),
        grid_spec=grid_spec,
        compiler_params=_params("arbitrary"),
        name="moe_combine",
    )(pair_chunk, pair_blk, npairs, x2, yb, dest, gates)


def _pairs(mask, npairs, size):
    rows, cols = jnp.nonzero(mask, size=size, fill_value=0)
    keep = jnp.arange(size) < npairs
    last = jnp.maximum(npairs - 1, 0)
    return (jnp.where(keep, rows, rows[last]).astype(jnp.int32),
            jnp.where(keep, cols, cols[last]).astype(jnp.int32))


def _moe(x2, gain, router, wg, wu, wd):
    t, d = x2.shape
    nch = t // MOE_CHUNK
    h, idx, gate, before, count = _router(x2, gain, router)
    experts = jnp.arange(N_EXPERTS, dtype=jnp.int32)
    counts = count[0, :N_EXPERTS]
    padded = (counts + MOE_ROWS - 1) // MOE_ROWS * MOE_ROWS
    pend = jnp.cumsum(padded)
    pstart = pend - padded
    expert, rank = idx[:, :TOP_K], idx[:, TOP_K:2 * TOP_K]
    dest = jnp.sum(jnp.where(expert[:, :, None] == experts, pstart, 0), axis=-1) + rank
    p_rows = t * TOP_K + N_EXPERTS * MOE_ROWS
    nblk = p_rows // MOE_ROWS
    blk_row = jnp.arange(nblk, dtype=jnp.int32) * MOE_ROWS
    blk_expert = jnp.minimum(jnp.searchsorted(pend, blk_row, side='right'), N_EXPERTS - 1).astype(jnp.int32)
    nused = pend[-1:].astype(jnp.int32)
    bounds = jnp.concatenate([before[:, 0, :N_EXPERTS], counts[None]], axis=0)
    lo = bounds[:-1].T[blk_expert]
    hi = bounds[1:].T[blk_expert]
    rank0 = (blk_row - pstart[blk_expert])[:, None]
    rank1 = jnp.minimum(rank0 + MOE_ROWS, counts[blk_expert][:, None])
    share = (hi > lo) & (lo < rank1) & (hi > rank0) & (blk_row < nused[0])[:, None]
    npairs = jnp.sum(share).astype(jnp.int32)
    max_pairs = nch * N_EXPERTS + nblk
    pair_blk, pair_chunk = _pairs(share, npairs, max_pairs)
    by_chunk, by_chunk_blk = _pairs(share.T, npairs, max_pairs)
    npairs = npairs.reshape(1)
    dest_chunks = dest.T.reshape(TOP_K, nch, MOE_CHUNK).transpose(1, 0, 2)
    hb = _dispatch(h, dest_chunks, pair_blk, pair_chunk, npairs, p_rows)
    yb = _ffn_experts(hb, wg, wu, wd, blk_expert, nused)
    return _combine(x2, yb, dest, gate[:, :TOP_K], by_chunk, by_chunk_blk, npairs)


def kernel(x, mem, rel_bias, norm_mix, norm_mem, norm_ffn, w_mem_kv, xq_norm, xk_norm, w_out, hy_w_in, hy_conv_w, hy_conv_b, hy_filt_w1, hy_filt_b1, hy_filt_w2, hy_filt_b2, hy_filt_w3, hy_sin_freq, hy_skip, at_w_in, at_q_norm, at_k_norm, ffn_w_gate, ffn_w_up, ffn_w_down, moe_router, moe_w_gate, moe_w_up, moe_w_down):
    b, s, d = x.shape
    t = b * s
    m_len = mem.shape[1]
    x2 = x.reshape(t, d)
    mem2 = mem.reshape(b * m_len, d)
    bf = lambda w: w.astype(BF16)

    cmat, smat = _dft_tables(s)
    spectra, hn = _hyena_filters(s, cmat, smat, hy_filt_w1[0], hy_filt_b1[0], hy_filt_w2[0], hy_filt_b2[0],
                                 hy_filt_w3[0], hy_sin_freq[0])
    z = _norm_matmul(x2, norm_mix[0], bf(hy_w_in[0]), 1024).reshape(b, s, -1)
    kv = _norm_matmul(mem2, norm_mem[0], bf(w_mem_kv[0]), 1024).reshape(b, m_len, -1)
    self_out = _hyena(z, hy_conv_w[0], hy_conv_b[0], cmat, smat, spectra, hn, hy_skip[0])
    cross = _xattn(z, 3 * HY_C // XA_W, kv, xq_norm[0], xk_norm[0])
    x2 = _outproj(x2, self_out.reshape(t, -1), cross.reshape(t, -1), bf(w_out[0]))
    x2 = _ffn_dense(x2, norm_ffn[0], bf(ffn_w_gate), bf(ffn_w_up), bf(ffn_w_down))

    z = _norm_matmul(x2, norm_mix[1], bf(at_w_in[0]), 1024).reshape(b, s, -1)
    kv = _norm_matmul(mem2, norm_mem[1], bf(w_mem_kv[1]), 1024).reshape(b, m_len, -1)
    self_out = _dilated_attention(z, at_q_norm[0], at_k_norm[0], rel_bias)
    cross = _xattn(z, 3 * AT_W // XA_W, kv, xq_norm[1], xk_norm[1])
    x2 = _outproj(x2, self_out.reshape(t, -1), cross.reshape(t, -1), bf(w_out[1]))
    x2 = _moe(x2, norm_ffn[1], moe_router[0], bf(moe_w_gate[0]), bf(moe_w_up[0]), bf(moe_w_down[0]))
    return x2.reshape(b, s, d)
```

```python
import functools
import math

import jax
import jax.numpy as jnp
import numpy as np
from jax import lax
from jax.experimental import pallas as pl
from jax.experimental.pallas import tpu as pltpu
from jax.experimental.pallas import tpu_sc as plsc

F32 = jnp.float32
BF16 = jnp.bfloat16

D_MODEL = 1024
EPS = 1e-6
HY_C = 512
FILT_BANDS = 16
DECAY_TARGET = 1e-2
FAST_DECAY_PCT = 0.3
SLOW_DECAY_PCT = 1.5
MOD_SHIFT = 0.05
AT_GROUPS = ((128, 1), (512, 4), (2048, 16))
AT_HEADS = 4
HD = 128
AT_W = 1536
NUM_BUCKETS = 32
REL_MAX_DIST = 1024
NEG_INF = -1e30
XA_W = 512
D_FF = 2816
N_EXPERTS = 8
TOP_K = 2

VMEM_LIMIT_BYTES = 56 * 1024 * 1024
ROW_TILE = 512
FF_TILE = 1408
MOE_ROWS = 512
MOE_CHUNK = 512
SC_GATHER_ROWS = 32
ATT_TQ = 128
ATT_KW = 256
HY_TC = 256
HY_KT = 512


def _params(*sem):
    return pltpu.CompilerParams(dimension_semantics=sem, vmem_limit_bytes=VMEM_LIMIT_BYTES)


def _rms_rows(x, gain):
    return x * lax.rsqrt(jnp.mean(x * x, axis=-1, keepdims=True) + EPS) * gain


def _bf16_bits(v):
    u = lax.bitcast_convert_type(v, jnp.uint32)
    return (u + jnp.uint32(0x7FFF) + ((u >> 16) & jnp.uint32(1))) >> 16


def _resident(shape, index_map):
    return pl.BlockSpec(shape, index_map, pipeline_mode=pl.Buffered(1))


def _norm_matmul_kernel(x_ref, g_ref, w_ref, o_ref, *, tn):
    h = _rms_rows(x_ref[...], g_ref[...]).astype(BF16)
    for c in range(o_ref.shape[1] // tn):
        cols = slice(c * tn, (c + 1) * tn)
        o_ref[:, cols] = jnp.dot(h, w_ref[:, cols], preferred_element_type=F32).astype(o_ref.dtype)


def _norm_matmul(x2, gain, w_bf, tn):
    rows, d = x2.shape
    n = w_bf.shape[1]
    tm = min(ROW_TILE, rows)
    return pl.pallas_call(
        functools.partial(_norm_matmul_kernel, tn=tn),
        out_shape=jax.ShapeDtypeStruct((rows, n), BF16),
        grid=(rows // tm,),
        in_specs=[pl.BlockSpec((tm, d), lambda i: (i, 0)),
                  pl.BlockSpec((1, d), lambda i: (0, 0)),
                  _resident((d, n), lambda i: (0, 0))],
        out_specs=pl.BlockSpec((tm, n), lambda i: (i, 0)),
        compiler_params=_params("parallel"),
        name="norm_matmul",
    )(x2, gain.reshape(1, d), w_bf)


def _outproj_kernel(x_ref, a_ref, c_ref, wa_ref, wc_ref, o_ref):
    o_ref[...] = (x_ref[...]
                  + jnp.dot(a_ref[...].astype(BF16), wa_ref[...], preferred_element_type=F32)
                  + jnp.dot(c_ref[...].astype(BF16), wc_ref[...], preferred_element_type=F32))


def _outproj(x2, self_out, cross, w_bf):
    rows, d = x2.shape
    half = self_out.shape[1]
    tm = ROW_TILE
    return pl.pallas_call(
        _outproj_kernel,
        out_shape=jax.ShapeDtypeStruct((rows, d), F32),
        grid=(rows // tm,),
        in_specs=[pl.BlockSpec((tm, d), lambda i: (i, 0)),
                  pl.BlockSpec((tm, half), lambda i: (i, 0)),
                  pl.BlockSpec((tm, half), lambda i: (i, 0)),
                  pl.BlockSpec((half, d), lambda i: (0, 0)),
                  pl.BlockSpec((half, d), lambda i: (1, 0))],
        out_specs=pl.BlockSpec((tm, d), lambda i: (i, 0)),
        compiler_params=_params("parallel"),
        name="outproj",
    )(x2, self_out, cross, w_bf, w_bf)


def _swiglu_accumulate(h, wg_ref, wu_ref, wd_ref, acc_ref):
    gg = jnp.dot(h, wg_ref[0], preferred_element_type=F32)
    uu = jnp.dot(h, wu_ref[0], preferred_element_type=F32)
    a = (gg * jax.nn.sigmoid(gg)) * uu
    acc_ref[...] += jnp.dot(a.astype(BF16), wd_ref[0], preferred_element_type=F32)


def _ffn_dense_kernel(x_ref, g_ref, wg_ref, wu_ref, wd_ref, o_ref, h_ref, acc_ref):
    j = pl.program_id(1)

    @pl.when(j == 0)
    def _():
        h_ref[...] = _rms_rows(x_ref[...], g_ref[...]).astype(BF16)
        acc_ref[...] = jnp.zeros_like(acc_ref)

    _swiglu_accumulate(h_ref[...], wg_ref, wu_ref, wd_ref, acc_ref)

    @pl.when(j == pl.num_programs(1) - 1)
    def _():
        o_ref[...] = x_ref[...] + acc_ref[...]


def _ffn_dense(x2, gain, wg, wu, wd):
    rows, d = x2.shape
    f = wg.shape[2]
    tm, tf = ROW_TILE, FF_TILE
    return pl.pallas_call(
        _ffn_dense_kernel,
        out_shape=jax.ShapeDtypeStruct((rows, d), F32),
        grid=(rows // tm, f // tf),
        in_specs=[pl.BlockSpec((tm, d), lambda i, j: (i, 0)),
                  pl.BlockSpec((1, d), lambda i, j: (0, 0)),
                  pl.BlockSpec((1, d, tf), lambda i, j: (0, 0, j)),
                  pl.BlockSpec((1, d, tf), lambda i, j: (0, 0, j)),
                  pl.BlockSpec((1, tf, d), lambda i, j: (0, j, 0))],
        out_specs=pl.BlockSpec((tm, d), lambda i, j: (i, 0)),
        scratch_shapes=[pltpu.VMEM((tm, d), BF16), pltpu.VMEM((tm, d), F32)],
        compiler_params=_params("parallel", "arbitrary"),
        name="ffn_dense",
    )(x2, gain.reshape(1, d), wg, wu, wd)


def _ffn_expert_kernel(eid_ref, nused_ref, h_ref, wg_ref, wu_ref, wd_ref, o_ref, acc_ref):
    i, j = pl.program_id(0), pl.program_id(1)

    @pl.when(j == 0)
    def _():
        acc_ref[...] = jnp.zeros_like(acc_ref)

    @pl.when(i * h_ref.shape[0] < nused_ref[0])
    def _():
        _swiglu_accumulate(h_ref[...], wg_ref, wu_ref, wd_ref, acc_ref)

    @pl.when(j == pl.num_programs(1) - 1)
    def _():
        half = o_ref.shape[1]
        o_ref[...] = _bf16_bits(acc_ref[:, :half]) | (_bf16_bits(acc_ref[:, half:]) << 16)


def _ffn_experts(hb, wg, wu, wd, eid, nused):
    rows, d = hb.shape
    f = wg.shape[2]
    tm, tf = MOE_ROWS, FF_TILE
    grid_spec = pltpu.PrefetchScalarGridSpec(
        num_scalar_prefetch=2,
        grid=(rows // tm, f // tf),
        in_specs=[pl.BlockSpec((tm, d), lambda i, j, e, n: (i, 0)),
                  pl.BlockSpec((1, d, tf), lambda i, j, e, n: (e[i], 0, j)),
                  pl.BlockSpec((1, d, tf), lambda i, j, e, n: (e[i], 0, j)),
                  pl.BlockSpec((1, tf, d), lambda i, j, e, n: (e[i], j, 0))],
        out_specs=pl.BlockSpec((tm, d // 2), lambda i, j, e, n: (i, 0)),
        scratch_shapes=[pltpu.VMEM((tm, d), F32)],
    )
    return pl.pallas_call(
        _ffn_expert_kernel,
        out_shape=jax.ShapeDtypeStruct((rows, d // 2), jnp.uint32),
        grid_spec=grid_spec,
        compiler_params=_params("parallel", "arbitrary"),
        name="ffn_experts",
    )(eid, nused, hb, wg, wu, wd)


def _xattn_kernel(xq_ref, kv_ref, qg_ref, kg_ref, o_ref):
    scale = HD ** -0.5
    for h in range(XA_W // HD):
        cols = slice(h * HD, (h + 1) * HD)
        q = _rms_rows(xq_ref[0, :, cols].astype(F32), qg_ref[...]).astype(BF16)
        k = _rms_rows(kv_ref[0, :, cols].astype(F32), kg_ref[...]).astype(BF16)
        v = kv_ref[0, :, XA_W + h * HD:XA_W + (h + 1) * HD]
        s = lax.dot_general(q, k, (((1,), (1,)), ((), ())), preferred_element_type=F32) * scale
        p = jnp.exp(s - jnp.max(s, axis=-1, keepdims=True))
        z = jnp.sum(p, axis=-1, keepdims=True)
        o_ref[0, :, cols] = jnp.dot(p.astype(BF16), v, preferred_element_type=F32) / z


def _xattn(z3, xq_block, kv3, q_gain, k_gain):
    b, s, _ = z3.shape
    m = kv3.shape[1]
    ts = ROW_TILE
    return pl.pallas_call(
        _xattn_kernel,
        out_shape=jax.ShapeDtypeStruct((b, s, XA_W), F32),
        grid=(b, s // ts),
        in_specs=[pl.BlockSpec((1, ts, XA_W), lambda i, j: (i, j, xq_block)),
                  pl.BlockSpec((1, m, 2 * XA_W), lambda i, j: (i, 0, 0)),
                  pl.BlockSpec((1, HD), lambda i, j: (0, 0)),
                  pl.BlockSpec((1, HD), lambda i, j: (0, 0))],
        out_specs=pl.BlockSpec((1, ts, XA_W), lambda i, j: (i, j, 0)),
        compiler_params=_params("parallel", "parallel"),
        name="cross_attention",
    )(z3, kv3, q_gain.reshape(1, HD), k_gain.reshape(1, HD))


def _dft_kernel(c_ref, s_ref, c0_ref, s0_ref, *, n_fft):
    rows, cols = c_ref.shape
    i = pl.program_id(0)

    @pl.when(i == 0)
    def _():
        k = lax.broadcasted_iota(jnp.int32, (rows, cols), 0)
        n = lax.broadcasted_iota(jnp.int32, (rows, cols), 1)
        ang = ((k * n) & (n_fft - 1)).astype(F32) * (2.0 * math.pi / n_fft)
        c0_ref[...] = jnp.cos(ang)
        s0_ref[...] = jnp.sin(ang)

    period = n_fft // rows
    n = lax.broadcasted_iota(jnp.int32, (8, cols), 1)
    shift = ((i * n) & (period - 1)).astype(F32) * (2.0 * math.pi / period)
    ca = jnp.cos(shift)[0:1]
    sa = jnp.sin(shift)[0:1]
    c0 = c0_ref[...]
    s0 = s0_ref[...]
    c_ref[...] = (ca * c0 - sa * s0).astype(BF16)
    s_ref[...] = (sa * c0 + ca * s0).astype(BF16)


def _dft_tables(length):
    rows = 256
    shape = jax.ShapeDtypeStruct((length, length), BF16)
    spec = pl.BlockSpec((rows, length), lambda i: (i, 0))
    return pl.pallas_call(
        functools.partial(_dft_kernel, n_fft=2 * length),
        out_shape=(shape, shape),
        grid=(length // rows,),
        out_specs=(spec, spec),
        scratch_shapes=[pltpu.VMEM((rows, length), F32), pltpu.VMEM((rows, length), F32)],
        compiler_params=_params("arbitrary"),
        name="dft_tables",
    )()


def _filter_time_kernel(feats_ref, w1_ref, b1_ref, w2_ref, b2_ref, fr_ref, w3_ref, t_ref, delta_ref,
                        gp_ref, gm_ref, hn_ref):
    hp = lax.Precision.HIGHEST
    length = feats_ref.shape[0]
    fr = fr_ref[...]
    h = jnp.sin(fr * (jnp.dot(feats_ref[...], w1_ref[...], preferred_element_type=F32, precision=hp) + b1_ref[...]))
    h = jnp.sin(fr * (jnp.dot(h, w2_ref[...], preferred_element_type=F32, precision=hp) + b2_ref[...]))
    mod = jnp.exp(-t_ref[...] * delta_ref[...]) + MOD_SHIFT
    row = lax.broadcasted_iota(jnp.int32, (length, 1), 0)
    alt = jnp.where(row % 2 == 0, 1.0, -1.0).astype(F32)
    for o in range(2):
        fwd = jnp.dot(h, w3_ref[2 * o], preferred_element_type=F32, precision=hp) * mod
        bwd = jnp.dot(h, w3_ref[2 * o + 1], preferred_element_type=F32, precision=hp) * mod
        bwd = jnp.where(row == 0, 0.0, bwd)
        norm = (jnp.sum(jnp.abs(fwd), axis=0, keepdims=True)
                + jnp.sum(jnp.abs(bwd), axis=0, keepdims=True) + 1e-6)
        fwd = fwd / norm
        bwd = bwd / norm
        gp = fwd + bwd
        gp_ref[o] = gp.astype(BF16)
        gm_ref[o] = (fwd - bwd).astype(BF16)
        hn_ref[o] = jnp.sum(gp * alt, axis=0, keepdims=True) * (1.0 / (2 * length))


def _filter_freq_kernel(c_ref, s_ref, gp_ref, gm_ref, h_ref):
    length = c_ref.shape[0]
    row = lax.broadcasted_iota(jnp.int32, (length, 1), 0)
    scale = jnp.where(row == 0, 1.0, 2.0).astype(F32) * (1.0 / (2 * length))
    h_ref[0, 0] = jnp.dot(c_ref[...], gp_ref[0], preferred_element_type=F32) * scale
    h_ref[0, 1] = -jnp.dot(s_ref[...], gm_ref[0], preferred_element_type=F32) * scale


def _hyena_filters(length, cmat, smat, w1, b1, w2, b2, w3, freq):
    t = jnp.linspace(0.0, 1.0, length, dtype=F32)[:, None]
    f = jnp.linspace(1e-4, FILT_BANDS - 1, FILT_BANDS, dtype=F32)[None]
    ang = (2.0 * math.pi / length) * jnp.arange(length, dtype=F32)[:, None] * f
    feats = jnp.concatenate([t, jnp.cos(ang), -jnp.sin(ang)], axis=-1)
    deltas = jnp.abs(jnp.linspace(math.log(DECAY_TARGET) / SLOW_DECAY_PCT,
                                  math.log(DECAY_TARGET) / FAST_DECAY_PCT, HY_C, dtype=F32))[None]
    hid = w1.shape[1]
    w3r = w3.reshape(hid, 4, HY_C).transpose(1, 0, 2)
    tc = HY_TC
    full = lambda shape: pl.BlockSpec(shape, lambda c: (0,) * len(shape))
    gp, gm, hn = pl.pallas_call(
        _filter_time_kernel,
        out_shape=(jax.ShapeDtypeStruct((2, length, HY_C), BF16),
                   jax.ShapeDtypeStruct((2, length, HY_C), BF16),
                   jax.ShapeDtypeStruct((2, 1, HY_C), F32)),
        grid=(HY_C // tc,),
        in_specs=[full(feats.shape), full(w1.shape), full((1, hid)), full(w2.shape), full((1, hid)),
                  full((1, hid)), pl.BlockSpec((4, hid, tc), lambda c: (0, 0, c)), full((length, 1)),
                  pl.BlockSpec((1, tc), lambda c: (0, c))],
        out_specs=(pl.BlockSpec((2, length, tc), lambda c: (0, 0, c)),
                   pl.BlockSpec((2, length, tc), lambda c: (0, 0, c)),
                   pl.BlockSpec((2, 1, tc), lambda c: (0, 0, c))),
        compiler_params=_params("parallel"),
        name="filter_time",
    )(feats, w1, b1.reshape(1, hid), w2, b2.reshape(1, hid), freq.reshape(1, hid), w3r, t, deltas)
    spectra = pl.pallas_call(
        _filter_freq_kernel,
        out_shape=jax.ShapeDtypeStruct((2, 2, length, HY_C), F32),
        grid=(2, HY_C // tc),
        in_specs=[_resident((length, length), lambda o, c: (0, 0)),
                  _resident((length, length), lambda o, c: (0, 0)),
                  pl.BlockSpec((1, length, tc), lambda o, c: (o, 0, c)),
                  pl.BlockSpec((1, length, tc), lambda o, c: (o, 0, c))],
        out_specs=pl.BlockSpec((1, 2, length, tc), lambda o, c: (o, 0, 0, c)),
        compiler_params=_params("parallel", "parallel"),
        name="filter_freq",
    )(cmat, smat, gp, gm)
    return spectra, hn


def _hyena_kernel(z0_ref, z1_ref, z2_ref, cw_ref, cb_ref, c_ref, s_ref, h_ref, hn_ref, skip_ref, o_ref,
                  ubf_ref, y_ref):
    length = z0_ref.shape[1]
    row = lax.broadcasted_iota(jnp.int32, (length, 1), 0)
    alt = jnp.where(row % 2 == 0, 1.0, -1.0).astype(F32)

    def short_conv(z_ref, c):
        u = z_ref[0].astype(F32)
        prev = jnp.where(row == 0, 0.0, pltpu.roll(u, 1, axis=0))
        nxt = jnp.where(row == length - 1, 0.0, pltpu.roll(u, length - 1, axis=0))
        return cb_ref[c] + (prev * cw_ref[0, c] + u * cw_ref[1, c] + nxt * cw_ref[2, c])

    def long_conv(u, o):
        ubf_ref[...] = u.astype(BF16)
        nyq = jnp.sum(u * alt, axis=0, keepdims=True) * hn_ref[o]
        y_ref[...] = alt * nyq + u * skip_ref[o]
        for kt in range(length // HY_KT):
            ks = slice(kt * HY_KT, (kt + 1) * HY_KT)
            ub = ubf_ref[...]
            a = jnp.dot(c_ref[ks, :], ub, preferred_element_type=F32)
            b = jnp.dot(s_ref[ks, :], ub, preferred_element_type=F32)
            hr = h_ref[o, 0, ks, :]
            hi = h_ref[o, 1, ks, :]
            re = (a * hr + b * hi).astype(BF16)
            im = (a * hi - b * hr).astype(BF16)
            y_ref[...] += (jnp.dot(c_ref[:, ks], re, preferred_element_type=F32)
                           - jnp.dot(s_ref[:, ks], im, preferred_element_type=F32))
        return y_ref[...]

    z = short_conv(z1_ref, 1) * long_conv(short_conv(z0_ref, 0), 0)
    o_ref[0] = short_conv(z2_ref, 2) * long_conv(z, 1)


def _hyena(z3, conv_w, conv_b, cmat, smat, spectra, hn, skip):
    b, s, _ = z3.shape
    tc = HY_TC
    nct = HY_C // tc
    cw = conv_w.reshape(3, 3, 1, HY_C)
    cb = conv_b.reshape(3, 1, HY_C)
    zspec = lambda chunk: pl.BlockSpec((1, s, tc), lambda c, i: (i, 0, chunk * nct + c))
    return pl.pallas_call(
        _hyena_kernel,
        out_shape=jax.ShapeDtypeStruct((b, s, HY_C), F32),
        grid=(nct, b),
        in_specs=[zspec(0), zspec(1), zspec(2),
                  pl.BlockSpec((3, 3, 1, tc), lambda c, i: (0, 0, 0, c)),
                  pl.BlockSpec((3, 1, tc), lambda c, i: (0, 0, c)),
                  _resident((s, s), lambda c, i: (0, 0)),
                  _resident((s, s), lambda c, i: (0, 0)),
                  _resident((2, 2, s, tc), lambda c, i: (0, 0, 0, c)),
                  pl.BlockSpec((2, 1, tc), lambda c, i: (0, 0, c)),
                  pl.BlockSpec((2, 1, tc), lambda c, i: (0, 0, c))],
        out_specs=pl.BlockSpec((1, s, tc), lambda c, i: (i, 0, c)),
        scratch_shapes=[pltpu.VMEM((s, tc), BF16), pltpu.VMEM((s, tc), F32)],
        compiler_params=_params("parallel", "parallel"),
        name="hyena",
    )(z3, z3, z3, cw, cb, cmat, smat, spectra, hn, skip.reshape(2, 1, HY_C))


def _rel_bucket(rel):
    half = NUM_BUCKETS // 2
    exact = half // 2
    n = np.abs(rel)
    large = exact + (np.log(np.maximum(n, 1) / exact) / np.log(REL_MAX_DIST / exact) * (half - exact)).astype(np.int32)
    large = np.minimum(large, half - 1)
    return (np.where(rel > 0, half, 0) + np.where(n < exact, n, large)).astype(np.int32)


def _att_tiles(length):
    kw = min(ATT_KW, length)
    tiles = []
    for qs in range(0, length, ATT_TQ):
        ks = min(max(qs - (kw - ATT_TQ) // 2, 0), length - kw)
        tiles.append((qs, ks, {0: 0, -64: 1, -128: 2}[ks - qs]))
    return kw, tiles


def _att_bias(rel_bias, group, seq):
    window, dil = AT_GROUPS[group]
    band = window // (2 * dil)
    kw, tiles = _att_tiles(seq // dil)
    offsets = sorted({ks - qs for qs, ks, _ in tiles}, reverse=True)
    table = rel_bias[:, group * AT_HEADS:(group + 1) * AT_HEADS].astype(F32)
    span = kw + ATT_TQ
    out = []
    for off in offsets:
        k = np.arange(span)
        rel = off + np.where(k < kw, k, k - span)
        diag = jnp.where(np.abs(rel) <= band, table[_rel_bucket(rel * dil)].T, NEG_INF)
        flat = jnp.tile(diag, (1, ATT_TQ))[:, :ATT_TQ * (span - 1)]
        out.append(flat.reshape(AT_HEADS, ATT_TQ, span - 1)[:, :, :kw])
    return jnp.stack(out)


def _dilated_kernel(q1, q2, q3, k1, k2, k3, v1, v2, v3, qg_ref, kg_ref, b1, b2, b3, o_ref,
                    qn_ref, kn_ref, vn_ref, og_ref, lg_ref):
    seq = o_ref.shape[1]
    scale = HD ** -0.5
    contract_last = (((1,), (1,)), ((), ()))
    for g, (q_ref, k_ref, v_ref, bias_ref) in enumerate(((q1, k1, v1, b1), (q2, k2, v2, b2), (q3, k3, v3, b3))):
        dil = AT_GROUPS[g][1]
        kw, tiles = _att_tiles(seq // dil)
        qn_ref[...] = _rms_rows(q_ref[0].astype(F32), qg_ref[...])
        kn_ref[...] = _rms_rows(k_ref[0].astype(F32), kg_ref[...])
        vn_ref[...] = v_ref[0].astype(F32)
        for r in range(dil):
            for qs, ks, var in tiles:
                rows = lambda start, size: (pl.ds(r + start * dil, size, stride=dil) if dil > 1
                                            else pl.ds(start, size))
                qt = qn_ref[rows(qs, ATT_TQ), :].astype(BF16)
                kt = kn_ref[rows(ks, kw), :].astype(BF16)
                vt = vn_ref[rows(ks, kw), :].astype(BF16)
                s = lax.dot_general(qt, kt, contract_last, preferred_element_type=F32) * scale + bias_ref[var, 0]
                m = jnp.max(s, axis=-1, keepdims=True)
                p = jnp.exp(s - m)
                z = jnp.sum(p, axis=-1, keepdims=True)
                og_ref[g, rows(qs, ATT_TQ), :] = jnp.dot(p.astype(BF16), vt, preferred_element_type=F32) / z
                lg_ref[g, rows(qs, ATT_TQ), :] = jnp.broadcast_to(m + jnp.log(z), (ATT_TQ, HD))
    l0, l1, l2 = lg_ref[0], lg_ref[1], lg_ref[2]
    mx = jnp.maximum(jnp.maximum(l0, l1), l2)
    e0, e1, e2 = jnp.exp(l0 - mx), jnp.exp(l1 - mx), jnp.exp(l2 - mx)
    o_ref[0] = (e0 * og_ref[0] + e1 * og_ref[1] + e2 * og_ref[2]) / (e0 + e1 + e2)


def _dilated_attention(z3, q_gain, k_gain, rel_bias):
    b, s, _ = z3.shape
    ng = len(AT_GROUPS)
    col = lambda part, g: pl.BlockSpec((1, s, HD), lambda i, h: (i, 0, part * ng * AT_HEADS + g * AT_HEADS + h))
    biases = [_att_bias(rel_bias, g, s) for g in range(ng)]
    bias_spec = lambda a: pl.BlockSpec((a.shape[0], 1) + a.shape[2:], lambda i, h: (0, h, 0, 0))
    gain_spec = pl.BlockSpec((1, HD), lambda i, h: (0, 0))
    return pl.pallas_call(
        _dilated_kernel,
        out_shape=jax.ShapeDtypeStruct((b, s, AT_HEADS * HD), F32),
        grid=(b, AT_HEADS),
        in_specs=[col(p, g) for p in range(3) for g in range(ng)] + [gain_spec, gain_spec]
                 + [bias_spec(a) for a in biases],
        out_specs=pl.BlockSpec((1, s, HD), lambda i, h: (i, 0, h)),
        scratch_shapes=[pltpu.VMEM((s, HD), F32), pltpu.VMEM((s, HD), F32), pltpu.VMEM((s, HD), F32),
                        pltpu.VMEM((ng, s, HD), F32), pltpu.VMEM((ng, s, HD), F32)],
        compiler_params=_params("parallel", "parallel"),
        name="dilated_attention",
    )(*([z3] * 9), q_gain.reshape(1, HD), k_gain.reshape(1, HD), *biases)


ROUTER_LANES = 128


def _router_kernel(x_ref, g_ref, w_ref, h_ref, idx_ref, gate_ref, before_ref, count_ref, carry_ref):
    tm = x_ref.shape[0]

    @pl.when(pl.program_id(0) == 0)
    def _():
        carry_ref[...] = jnp.zeros_like(carry_ref)

    h = _rms_rows(x_ref[...], g_ref[...])
    h_ref[...] = h.astype(BF16)
    logits = jnp.dot(h, w_ref[...], preferred_element_type=F32, precision=lax.Precision.HIGHEST)
    lane = lax.broadcasted_iota(jnp.int32, logits.shape, 1).astype(F32)
    logits = jnp.where(lane < N_EXPERTS, logits, -jnp.inf)
    m1 = jnp.max(logits, axis=-1, keepdims=True)
    i1 = jnp.min(jnp.where(logits == m1, lane, float(ROUTER_LANES)), axis=-1, keepdims=True)
    rest = jnp.where(lane == i1, -jnp.inf, logits)
    m2 = jnp.max(rest, axis=-1, keepdims=True)
    i2 = jnp.min(jnp.where(rest == m2, lane, float(ROUTER_LANES)), axis=-1, keepdims=True)
    e2 = jnp.exp(m2 - m1)
    den = 1.0 + e2
    gate_ref[...] = jnp.where(lane == 0, 1.0 / den, e2 / den)
    chosen = jnp.where((lane == i1) | (lane == i2), 1.0, 0.0)
    earlier = lax.broadcasted_iota(jnp.int32, (tm, tm), 0) > lax.broadcasted_iota(jnp.int32, (tm, tm), 1)
    carry = carry_ref[...]
    before = jnp.dot(jnp.where(earlier, 1.0, 0.0).astype(BF16), chosen.astype(BF16),
                     preferred_element_type=F32) + carry
    r1 = jnp.sum(jnp.where(lane == i1, before, 0.0), axis=-1, keepdims=True)
    r2 = jnp.sum(jnp.where(lane == i2, before, 0.0), axis=-1, keepdims=True)
    idx_ref[...] = jnp.where(lane == 0, i1, jnp.where(lane == 1, i2, jnp.where(lane == 2, r1, r2))).astype(jnp.int32)
    before_ref[0] = carry.astype(jnp.int32)
    carry = carry + jnp.sum(chosen, axis=0, keepdims=True)
    carry_ref[...] = carry
    count_ref[...] = carry.astype(jnp.int32)


def _router(x2, gain, router):
    rows, d = x2.shape
    tm = MOE_CHUNK
    w = jnp.zeros((d, ROUTER_LANES), F32).at[:, :N_EXPERTS].set(router.astype(F32))
    row_spec = lambda width: pl.BlockSpec((tm, width), lambda i: (i, 0))
    return pl.pallas_call(
        _router_kernel,
        out_shape=(jax.ShapeDtypeStruct((rows, d), BF16),
                   jax.ShapeDtypeStruct((rows, ROUTER_LANES), jnp.int32),
                   jax.ShapeDtypeStruct((rows, ROUTER_LANES), F32),
                   jax.ShapeDtypeStruct((rows // tm, 1, ROUTER_LANES), jnp.int32),
                   jax.ShapeDtypeStruct((1, ROUTER_LANES), jnp.int32)),
        grid=(rows // tm,),
        in_specs=[row_spec(d),
                  pl.BlockSpec((1, d), lambda i: (0, 0)),
                  pl.BlockSpec((d, ROUTER_LANES), lambda i: (0, 0))],
        out_specs=(row_spec(d), row_spec(ROUTER_LANES), row_spec(ROUTER_LANES),
                   pl.BlockSpec((1, 1, ROUTER_LANES), lambda i: (i, 0, 0)),
                   pl.BlockSpec((1, ROUTER_LANES), lambda i: (0, 0))),
        scratch_shapes=[pltpu.VMEM((1, ROUTER_LANES), F32)],
        compiler_params=_params("arbitrary"),
        name="router",
    )(x2, gain.reshape(1, d), w)


def _dispatch_kernel(pb_ref, pc_ref, np_ref, h_ref, dest_ref, o_ref):
    s = pl.program_id(0)
    rows, chunk = o_ref.shape[0], h_ref.shape[0]
    active = s < np_ref[0]
    first = jnp.logical_or(s == 0, pb_ref[jnp.maximum(s - 1, 0)] != pb_ref[s])
    prow = pb_ref[s] * rows + lax.broadcasted_iota(jnp.int32, (rows, 1), 0)
    hit = (prow == dest_ref[0, 0:1, :]) | (prow == dest_ref[0, 1:2, :])
    sel = jnp.where(hit, 1.0, 0.0).astype(BF16)

    @pl.when(active & first)
    def _():
        o_ref[...] = jnp.dot(sel, h_ref[...], preferred_element_type=F32).astype(BF16)

    @pl.when(active & jnp.logical_not(first))
    def _():
        o_ref[...] += jnp.dot(sel, h_ref[...], preferred_element_type=F32).astype(BF16)


def _dispatch(h, dest_chunks, pair_blk, pair_chunk, npairs, p_rows):
    d = h.shape[1]
    grid_spec = pltpu.PrefetchScalarGridSpec(
        num_scalar_prefetch=3,
        grid=(pair_blk.shape[0],),
        in_specs=[pl.BlockSpec((MOE_CHUNK, d), lambda s, pb, pc, n: (pc[s], 0)),
                  pl.BlockSpec((1, TOP_K, MOE_CHUNK), lambda s, pb, pc, n: (pc[s], 0, 0))],
        out_specs=pl.BlockSpec((MOE_ROWS, d), lambda s, pb, pc, n: (pb[s], 0)),
    )
    return pl.pallas_call(
        _dispatch_kernel,
        out_shape=jax.ShapeDtypeStruct((p_rows, d), BF16),
        grid_spec=grid_spec,
        compiler_params=_params("arbitrary"),
        name="moe_dispatch",
    )(pair_blk, pair_chunk, npairs, h, dest_chunks)


def _row_gather(table, idx):
    n = idx.shape[0]
    w = table.shape[1]
    info = plsc.get_sparse_core_info()
    nc, ns = info.num_cores, info.num_subcores
    per_w = n // (nc * ns)
    ch = SC_GATHER_ROWS
    assert per_w * nc * ns == n and per_w % (2 * ch) == 0
    mesh = plsc.VectorSubcoreMesh(core_axis_name="core", subcore_axis_name="subcore")

    @functools.partial(
        pl.kernel, mesh=mesh, out_type=jax.ShapeDtypeStruct((n, w), table.dtype),
        scratch_types=[pltpu.VMEM((per_w,), jnp.int32), pltpu.VMEM((2, ch, w), table.dtype),
                       pltpu.SemaphoreType.DMA((2,)), pltpu.SemaphoreType.DMA((2,))])
    def gather(table_hbm, idx_hbm, out_hbm, idx_v, rows_v, gsem, wsem):
        base = (lax.axis_index("subcore") * nc + lax.axis_index("core")) * per_w
        pltpu.sync_copy(idx_hbm.at[pl.ds(base, per_w)], idx_v)

        @pl.loop(0, per_w // (2 * ch))
        def _(it):
            c0 = it * 2 * ch
            reads = [pltpu.make_async_copy(table_hbm.at[idx_v.at[pl.ds(c0 + b * ch, ch)]], rows_v.at[b], gsem.at[b])
                     for b in range(2)]
            writes = [pltpu.make_async_copy(rows_v.at[b], out_hbm.at[pl.ds(base + c0 + b * ch, ch)], wsem.at[b])
                      for b in range(2)]
            reads[0].start()
            reads[1].start()
            reads[0].wait()
            writes[0].start()
            reads[1].wait()
            writes[1].start()
            writes[0].wait()
            writes[1].wait()

    return gather(table, idx)


def _combine_kernel(x_ref, y_ref, gate_ref, o_ref):
    half = x_ref.shape[1] // 2
    for part in range(2):
        cols = slice(part * half, (part + 1) * half)
        acc = x_ref[:, cols]
        for k in range(TOP_K):
            word = y_ref[:, k * half:(k + 1) * half]
            bits = (word << 16) if part == 0 else (word & jnp.uint32(0xFFFF0000))
            acc = acc + gate_ref[:, k:k + 1] * lax.bitcast_convert_type(bits, F32)
        o_ref[:, cols] = acc


def _combine(x2, y_pairs, gates):
    t, d = x2.shape
    tm = ROW_TILE
    return pl.pallas_call(
        _combine_kernel,
        out_shape=jax.ShapeDtypeStruct((t, d), F32),
        grid=(t // tm,),
        in_specs=[pl.BlockSpec((tm, d), lambda i: (i, 0)),
                  pl.BlockSpec((tm, TOP_K * d // 2), lambda i: (i, 0)),
                  pl.BlockSpec((tm, TOP_K), lambda i: (i, 0))],
        out_specs=pl.BlockSpec((tm, d), lambda i: (i, 0)),
        compiler_params=_params("parallel"),
        name="moe_combine",
    )(x2, y_pairs, gates)


def _pairs(mask, npairs, size):
    rows, cols = jnp.nonzero(mask, size=size, fill_value=0)
    keep = jnp.arange(size) < npairs
    last = jnp.maximum(npairs - 1, 0)
    return (jnp.where(keep, rows, rows[last]).astype(jnp.int32),
            jnp.where(keep, cols, cols[last]).astype(jnp.int32))


def _moe(x2, gain, router, wg, wu, wd):
    t, d = x2.shape
    nch = t // MOE_CHUNK
    h, idx, gate, before, count = _router(x2, gain, router)
    experts = jnp.arange(N_EXPERTS, dtype=jnp.int32)
    counts = count[0, :N_EXPERTS]
    padded = (counts + MOE_ROWS - 1) // MOE_ROWS * MOE_ROWS
    pend = jnp.cumsum(padded)
    pstart = pend - padded
    expert, rank = idx[:, :TOP_K], idx[:, TOP_K:2 * TOP_K]
    dest = jnp.sum(jnp.where(expert[:, :, None] == experts, pstart, 0), axis=-1) + rank
    p_rows = t * TOP_K + N_EXPERTS * MOE_ROWS
    nblk = p_rows // MOE_ROWS
    blk_row = jnp.arange(nblk, dtype=jnp.int32) * MOE_ROWS
    blk_expert = jnp.minimum(jnp.searchsorted(pend, blk_row, side='right'), N_EXPERTS - 1).astype(jnp.int32)
    nused = pend[-1:].astype(jnp.int32)
    bounds = jnp.concatenate([before[:, 0, :N_EXPERTS], counts[None]], axis=0)
    lo = bounds[:-1].T[blk_expert]
    hi = bounds[1:].T[blk_expert]
    rank0 = (blk_row - pstart[blk_expert])[:, None]
    rank1 = jnp.minimum(rank0 + MOE_ROWS, counts[blk_expert][:, None])
    used = (blk_row < nused[0])[:, None]
    share = (hi > lo) & (lo < rank1) & (hi > rank0) & used
    share = share | (jnp.logical_not(used) & (jnp.arange(nch) == 0)[None])
    npairs = jnp.sum(share).astype(jnp.int32)
    pair_blk, pair_chunk = _pairs(share, npairs, nch * N_EXPERTS + nblk)
    dest_chunks = dest.T.reshape(TOP_K, nch, MOE_CHUNK).transpose(1, 0, 2)
    hb = _dispatch(h, dest_chunks, pair_blk, pair_chunk, npairs.reshape(1), p_rows)
    yb = _ffn_experts(hb, wg, wu, wd, blk_expert, nused)
    y_pairs = _row_gather(yb, dest.reshape(t * TOP_K)).reshape(t, TOP_K * yb.shape[1])
    return _combine(x2, y_pairs, gate[:, :TOP_K])


def kernel(x, mem, rel_bias, norm_mix, norm_mem, norm_ffn, w_mem_kv, xq_norm, xk_norm, w_out, hy_w_in, hy_conv_w, hy_conv_b, hy_filt_w1, hy_filt_b1, hy_filt_w2, hy_filt_b2, hy_filt_w3, hy_sin_freq, hy_skip, at_w_in, at_q_norm, at_k_norm, ffn_w_gate, ffn_w_up, ffn_w_down, moe_router, moe_w_gate, moe_w_up, moe_w_down):
    b, s, d = x.shape
    t = b * s
    m_len = mem.shape[1]
    x2 = x.reshape(t, d)
    mem2 = mem.reshape(b * m_len, d)
    bf = lambda w: w.astype(BF16)

    cmat, smat = _dft_tables(s)
    spectra, hn = _hyena_filters(s, cmat, smat, hy_filt_w1[0], hy_filt_b1[0], hy_filt_w2[0], hy_filt_b2[0],
                                 hy_filt_w3[0], hy_sin_freq[0])
    z = _norm_matmul(x2, norm_mix[0], bf(hy_w_in[0]), 1024).reshape(b, s, -1)
    kv = _norm_matmul(mem2, norm_mem[0], bf(w_mem_kv[0]), 1024).reshape(b, m_len, -1)
    self_out = _hyena(z, hy_conv_w[0], hy_conv_b[0], cmat, smat, spectra, hn, hy_skip[0])
    cross = _xattn(z, 3 * HY_C // XA_W, kv, xq_norm[0], xk_norm[0])
    x2 = _outproj(x2, self_out.reshape(t, -1), cross.reshape(t, -1), bf(w_out[0]))
    x2 = _ffn_dense(x2, norm_ffn[0], bf(ffn_w_gate), bf(ffn_w_up), bf(ffn_w_down))

    z = _norm_matmul(x2, norm_mix[1], bf(at_w_in[0]), 1024).reshape(b, s, -1)
    kv = _norm_matmul(mem2, norm_mem[1], bf(w_mem_kv[1]), 1024).reshape(b, m_len, -1)
    self_out = _dilated_attention(z, at_q_norm[0], at_k_norm[0], rel_bias)
    cross = _xattn(z, 3 * AT_W // XA_W, kv, xq_norm[1], xk_norm[1])
    x2 = _outproj(x2, self_out.reshape(t, -1), cross.reshape(t, -1), bf(w_out[1]))
    x2 = _moe(x2, norm_ffn[1], moe_router[0], bf(moe_w_gate[0]), bf(moe_w_up[0]), bf(moe_w_down[0]))
    return x2.reshape(b, s, d)
```

```python
import functools
import math

import jax
import jax.numpy as jnp
import numpy as np
from jax import lax
from jax.experimental import pallas as pl
from jax.experimental.pallas import tpu as pltpu
from jax.experimental.pallas import tpu_sc as plsc

F32 = jnp.float32
BF16 = jnp.bfloat16

D_MODEL = 1024
EPS = 1e-6
HY_C = 512
FILT_BANDS = 16
DECAY_TARGET = 1e-2
FAST_DECAY_PCT = 0.3
SLOW_DECAY_PCT = 1.5
MOD_SHIFT = 0.05
AT_GROUPS = ((128, 1), (512, 4), (2048, 16))
AT_HEADS = 4
HD = 128
AT_W = 1536
NUM_BUCKETS = 32
REL_MAX_DIST = 1024
NEG_INF = -1e30
XA_W = 512
D_FF = 2816
N_EXPERTS = 8
TOP_K = 2

VMEM_LIMIT_BYTES = 56 * 1024 * 1024
ROW_TILE = 512
FF_TILE = 1408
MOE_ROWS = 512
MOE_CHUNK = 512
SC_ROWS = 32
ATT_TQ = 128
ATT_KW = 256
HY_TC = 256
HY_KT = 512


def _params(*sem):
    return pltpu.CompilerParams(dimension_semantics=sem, vmem_limit_bytes=VMEM_LIMIT_BYTES)


def _rms_rows(x, gain):
    return x * lax.rsqrt(jnp.mean(x * x, axis=-1, keepdims=True) + EPS) * gain


def _bf16_bits(v):
    u = lax.bitcast_convert_type(v, jnp.uint32)
    return (u + jnp.uint32(0x7FFF) + ((u >> 16) & jnp.uint32(1))) >> 16


def _resident(shape, index_map):
    return pl.BlockSpec(shape, index_map, pipeline_mode=pl.Buffered(1))


def _norm_matmul_kernel(x_ref, g_ref, w_ref, o_ref, *, tn):
    h = _rms_rows(x_ref[...], g_ref[...]).astype(BF16)
    for c in range(o_ref.shape[1] // tn):
        cols = slice(c * tn, (c + 1) * tn)
        o_ref[:, cols] = jnp.dot(h, w_ref[:, cols], preferred_element_type=F32).astype(o_ref.dtype)


def _norm_matmul(x2, gain, w_bf, tn):
    rows, d = x2.shape
    n = w_bf.shape[1]
    tm = min(ROW_TILE, rows)
    return pl.pallas_call(
        functools.partial(_norm_matmul_kernel, tn=tn),
        out_shape=jax.ShapeDtypeStruct((rows, n), BF16),
        grid=(rows // tm,),
        in_specs=[pl.BlockSpec((tm, d), lambda i: (i, 0)),
                  pl.BlockSpec((1, d), lambda i: (0, 0)),
                  _resident((d, n), lambda i: (0, 0))],
        out_specs=pl.BlockSpec((tm, n), lambda i: (i, 0)),
        compiler_params=_params("parallel"),
        name="norm_matmul",
    )(x2, gain.reshape(1, d), w_bf)


def _outproj_kernel(x_ref, a_ref, c_ref, wa_ref, wc_ref, o_ref):
    o_ref[...] = (x_ref[...]
                  + jnp.dot(a_ref[...].astype(BF16), wa_ref[...], preferred_element_type=F32)
                  + jnp.dot(c_ref[...].astype(BF16), wc_ref[...], preferred_element_type=F32))


def _outproj(x2, self_out, cross, w_bf):
    rows, d = x2.shape
    half = self_out.shape[1]
    tm = ROW_TILE
    return pl.pallas_call(
        _outproj_kernel,
        out_shape=jax.ShapeDtypeStruct((rows, d), F32),
        grid=(rows // tm,),
        in_specs=[pl.BlockSpec((tm, d), lambda i: (i, 0)),
                  pl.BlockSpec((tm, half), lambda i: (i, 0)),
                  pl.BlockSpec((tm, half), lambda i: (i, 0)),
                  pl.BlockSpec((half, d), lambda i: (0, 0)),
                  pl.BlockSpec((half, d), lambda i: (1, 0))],
        out_specs=pl.BlockSpec((tm, d), lambda i: (i, 0)),
        compiler_params=_params("parallel"),
        name="outproj",
    )(x2, self_out, cross, w_bf, w_bf)


def _swiglu_accumulate(h, wg_ref, wu_ref, wd_ref, acc_ref):
    gg = jnp.dot(h, wg_ref[0], preferred_element_type=F32)
    uu = jnp.dot(h, wu_ref[0], preferred_element_type=F32)
    a = (gg * jax.nn.sigmoid(gg)) * uu
    acc_ref[...] += jnp.dot(a.astype(BF16), wd_ref[0], preferred_element_type=F32)


def _ffn_dense_kernel(x_ref, g_ref, wg_ref, wu_ref, wd_ref, o_ref, h_ref, acc_ref):
    j = pl.program_id(1)

    @pl.when(j == 0)
    def _():
        h_ref[...] = _rms_rows(x_ref[...], g_ref[...]).astype(BF16)
        acc_ref[...] = jnp.zeros_like(acc_ref)

    _swiglu_accumulate(h_ref[...], wg_ref, wu_ref, wd_ref, acc_ref)

    @pl.when(j == pl.num_programs(1) - 1)
    def _():
        o_ref[...] = x_ref[...] + acc_ref[...]


def _ffn_dense(x2, gain, wg, wu, wd):
    rows, d = x2.shape
    f = wg.shape[2]
    tm, tf = ROW_TILE, FF_TILE
    return pl.pallas_call(
        _ffn_dense_kernel,
        out_shape=jax.ShapeDtypeStruct((rows, d), F32),
        grid=(rows // tm, f // tf),
        in_specs=[pl.BlockSpec((tm, d), lambda i, j: (i, 0)),
                  pl.BlockSpec((1, d), lambda i, j: (0, 0)),
                  pl.BlockSpec((1, d, tf), lambda i, j: (0, 0, j)),
                  pl.BlockSpec((1, d, tf), lambda i, j: (0, 0, j)),
                  pl.BlockSpec((1, tf, d), lambda i, j: (0, j, 0))],
        out_specs=pl.BlockSpec((tm, d), lambda i, j: (i, 0)),
        scratch_shapes=[pltpu.VMEM((tm, d), BF16), pltpu.VMEM((tm, d), F32)],
        compiler_params=_params("parallel", "arbitrary"),
        name="ffn_dense",
    )(x2, gain.reshape(1, d), wg, wu, wd)


def _ffn_expert_kernel(eid_ref, valid_ref, hp_ref, wg_ref, wu_ref, wd_ref, o_ref, h_ref, acc_ref):
    i, j = pl.program_id(0), pl.program_id(1)
    half = hp_ref.shape[1]
    valid = valid_ref[i]

    @pl.when(j == 0)
    def _():
        keep = lax.broadcasted_iota(jnp.int32, (hp_ref.shape[0], 1), 0) < valid
        word = hp_ref[...]
        h_ref[:, :half] = jnp.where(keep, lax.bitcast_convert_type(word << 16, F32), 0.0).astype(BF16)
        h_ref[:, half:] = jnp.where(keep, lax.bitcast_convert_type(word & jnp.uint32(0xFFFF0000), F32),
                                    0.0).astype(BF16)
        acc_ref[...] = jnp.zeros_like(acc_ref)

    @pl.when(valid > 0)
    def _():
        _swiglu_accumulate(h_ref[...], wg_ref, wu_ref, wd_ref, acc_ref)

    @pl.when(j == pl.num_programs(1) - 1)
    def _():
        o_ref[...] = _bf16_bits(acc_ref[:, :half]) | (_bf16_bits(acc_ref[:, half:]) << 16)


def _ffn_experts(hp, wg, wu, wd, eid, valid):
    rows, half = hp.shape
    d = 2 * half
    f = wg.shape[2]
    tm, tf = MOE_ROWS, FF_TILE
    grid_spec = pltpu.PrefetchScalarGridSpec(
        num_scalar_prefetch=2,
        grid=(rows // tm, f // tf),
        in_specs=[pl.BlockSpec((tm, half), lambda i, j, e, n: (i, 0)),
                  pl.BlockSpec((1, d, tf), lambda i, j, e, n: (e[i], 0, j)),
                  pl.BlockSpec((1, d, tf), lambda i, j, e, n: (e[i], 0, j)),
                  pl.BlockSpec((1, tf, d), lambda i, j, e, n: (e[i], j, 0))],
        out_specs=pl.BlockSpec((tm, half), lambda i, j, e, n: (i, 0)),
        scratch_shapes=[pltpu.VMEM((tm, d), BF16), pltpu.VMEM((tm, d), F32)],
    )
    return pl.pallas_call(
        _ffn_expert_kernel,
        out_shape=jax.ShapeDtypeStruct((rows, half), jnp.uint32),
        grid_spec=grid_spec,
        compiler_params=_params("parallel", "arbitrary"),
        name="ffn_experts",
    )(eid, valid, hp, wg, wu, wd)


def _xattn_kernel(xq_ref, kv_ref, qg_ref, kg_ref, o_ref):
    scale = HD ** -0.5
    for h in range(XA_W // HD):
        cols = slice(h * HD, (h + 1) * HD)
        q = _rms_rows(xq_ref[0, :, cols].astype(F32), qg_ref[...]).astype(BF16)
        k = _rms_rows(kv_ref[0, :, cols].astype(F32), kg_ref[...]).astype(BF16)
        v = kv_ref[0, :, XA_W + h * HD:XA_W + (h + 1) * HD]
        s = lax.dot_general(q, k, (((1,), (1,)), ((), ())), preferred_element_type=F32) * scale
        p = jnp.exp(s - jnp.max(s, axis=-1, keepdims=True))
        z = jnp.sum(p, axis=-1, keepdims=True)
        o_ref[0, :, cols] = jnp.dot(p.astype(BF16), v, preferred_element_type=F32) / z


def _xattn(z3, xq_block, kv3, q_gain, k_gain):
    b, s, _ = z3.shape
    m = kv3.shape[1]
    ts = ROW_TILE
    return pl.pallas_call(
        _xattn_kernel,
        out_shape=jax.ShapeDtypeStruct((b, s, XA_W), F32),
        grid=(b, s // ts),
        in_specs=[pl.BlockSpec((1, ts, XA_W), lambda i, j: (i, j, xq_block)),
                  pl.BlockSpec((1, m, 2 * XA_W), lambda i, j: (i, 0, 0)),
                  pl.BlockSpec((1, HD), lambda i, j: (0, 0)),
                  pl.BlockSpec((1, HD), lambda i, j: (0, 0))],
        out_specs=pl.BlockSpec((1, ts, XA_W), lambda i, j: (i, j, 0)),
        compiler_params=_params("parallel", "parallel"),
        name="cross_attention",
    )(z3, kv3, q_gain.reshape(1, HD), k_gain.reshape(1, HD))


def _dft_kernel(c_ref, s_ref, c0_ref, s0_ref, *, n_fft):
    rows, cols = c_ref.shape
    i = pl.program_id(0)

    @pl.when(i == 0)
    def _():
        k = lax.broadcasted_iota(jnp.int32, (rows, cols), 0)
        n = lax.broadcasted_iota(jnp.int32, (rows, cols), 1)
        ang = ((k * n) & (n_fft - 1)).astype(F32) * (2.0 * math.pi / n_fft)
        c0_ref[...] = jnp.cos(ang)
        s0_ref[...] = jnp.sin(ang)

    period = n_fft // rows
    n = lax.broadcasted_iota(jnp.int32, (8, cols), 1)
    shift = ((i * n) & (period - 1)).astype(F32) * (2.0 * math.pi / period)
    ca = jnp.cos(shift)[0:1]
    sa = jnp.sin(shift)[0:1]
    c0 = c0_ref[...]
    s0 = s0_ref[...]
    c_ref[...] = (ca * c0 - sa * s0).astype(BF16)
    s_ref[...] = (sa * c0 + ca * s0).astype(BF16)


def _dft_tables(length):
    rows = 256
    shape = jax.ShapeDtypeStruct((length, length), BF16)
    spec = pl.BlockSpec((rows, length), lambda i: (i, 0))
    return pl.pallas_call(
        functools.partial(_dft_kernel, n_fft=2 * length),
        out_shape=(shape, shape),
        grid=(length // rows,),
        out_specs=(spec, spec),
        scratch_shapes=[pltpu.VMEM((rows, length), F32), pltpu.VMEM((rows, length), F32)],
        compiler_params=_params("arbitrary"),
        name="dft_tables",
    )()


def _filter_time_kernel(feats_ref, w1_ref, b1_ref, w2_ref, b2_ref, fr_ref, w3_ref, t_ref, delta_ref,
                        gp_ref, gm_ref, hn_ref):
    hp = lax.Precision.HIGHEST
    length = feats_ref.shape[0]
    fr = fr_ref[...]
    h = jnp.sin(fr * (jnp.dot(feats_ref[...], w1_ref[...], preferred_element_type=F32, precision=hp) + b1_ref[...]))
    h = jnp.sin(fr * (jnp.dot(h, w2_ref[...], preferred_element_type=F32, precision=hp) + b2_ref[...]))
    mod = jnp.exp(-t_ref[...] * delta_ref[...]) + MOD_SHIFT
    row = lax.broadcasted_iota(jnp.int32, (length, 1), 0)
    alt = jnp.where(row % 2 == 0, 1.0, -1.0).astype(F32)
    for o in range(2):
        fwd = jnp.dot(h, w3_ref[2 * o], preferred_element_type=F32, precision=hp) * mod
        bwd = jnp.dot(h, w3_ref[2 * o + 1], preferred_element_type=F32, precision=hp) * mod
        bwd = jnp.where(row == 0, 0.0, bwd)
        norm = (jnp.sum(jnp.abs(fwd), axis=0, keepdims=True)
                + jnp.sum(jnp.abs(bwd), axis=0, keepdims=True) + 1e-6)
        fwd = fwd / norm
        bwd = bwd / norm
        gp = fwd + bwd
        gp_ref[o] = gp.astype(BF16)
        gm_ref[o] = (fwd - bwd).astype(BF16)
        hn_ref[o] = jnp.sum(gp * alt, axis=0, keepdims=True) * (1.0 / (2 * length))


def _filter_freq_kernel(c_ref, s_ref, gp_ref, gm_ref, h_ref):
    length = c_ref.shape[0]
    row = lax.broadcasted_iota(jnp.int32, (length, 1), 0)
    scale = jnp.where(row == 0, 1.0, 2.0).astype(F32) * (1.0 / (2 * length))
    h_ref[0, 0] = jnp.dot(c_ref[...], gp_ref[0], preferred_element_type=F32) * scale
    h_ref[0, 1] = -jnp.dot(s_ref[...], gm_ref[0], preferred_element_type=F32) * scale


def _hyena_filters(length, cmat, smat, w1, b1, w2, b2, w3, freq):
    t = jnp.linspace(0.0, 1.0, length, dtype=F32)[:, None]
    f = jnp.linspace(1e-4, FILT_BANDS - 1, FILT_BANDS, dtype=F32)[None]
    ang = (2.0 * math.pi / length) * jnp.arange(length, dtype=F32)[:, None] * f
    feats = jnp.concatenate([t, jnp.cos(ang), -jnp.sin(ang)], axis=-1)
    deltas = jnp.abs(jnp.linspace(math.log(DECAY_TARGET) / SLOW_DECAY_PCT,
                                  math.log(DECAY_TARGET) / FAST_DECAY_PCT, HY_C, dtype=F32))[None]
    hid = w1.shape[1]
    w3r = w3.reshape(hid, 4, HY_C).transpose(1, 0, 2)
    tc = HY_TC
    full = lambda shape: pl.BlockSpec(shape, lambda c: (0,) * len(shape))
    gp, gm, hn = pl.pallas_call(
        _filter_time_kernel,
        out_shape=(jax.ShapeDtypeStruct((2, length, HY_C), BF16),
                   jax.ShapeDtypeStruct((2, length, HY_C), BF16),
                   jax.ShapeDtypeStruct((2, 1, HY_C), F32)),
        grid=(HY_C // tc,),
        in_specs=[full(feats.shape), full(w1.shape), full((1, hid)), full(w2.shape), full((1, hid)),
                  full((1, hid)), pl.BlockSpec((4, hid, tc), lambda c: (0, 0, c)), full((length, 1)),
                  pl.BlockSpec((1, tc), lambda c: (0, c))],
        out_specs=(pl.BlockSpec((2, length, tc), lambda c: (0, 0, c)),
                   pl.BlockSpec((2, length, tc), lambda c: (0, 0, c)),
                   pl.BlockSpec((2, 1, tc), lambda c: (0, 0, c))),
        compiler_params=_params("parallel"),
        name="filter_time",
    )(feats, w1, b1.reshape(1, hid), w2, b2.reshape(1, hid), freq.reshape(1, hid), w3r, t, deltas)
    spectra = pl.pallas_call(
        _filter_freq_kernel,
        out_shape=jax.ShapeDtypeStruct((2, 2, length, HY_C), F32),
        grid=(2, HY_C // tc),
        in_specs=[_resident((length, length), lambda o, c: (0, 0)),
                  _resident((length, length), lambda o, c: (0, 0)),
                  pl.BlockSpec((1, length, tc), lambda o, c: (o, 0, c)),
                  pl.BlockSpec((1, length, tc), lambda o, c: (o, 0, c))],
        out_specs=pl.BlockSpec((1, 2, length, tc), lambda o, c: (o, 0, 0, c)),
        compiler_params=_params("parallel", "parallel"),
        name="filter_freq",
    )(cmat, smat, gp, gm)
    return spectra, hn


def _hyena_kernel(z0_ref, z1_ref, z2_ref, cw_ref, cb_ref, c_ref, s_ref, h_ref, hn_ref, skip_ref, o_ref,
                  ubf_ref, y_ref):
    length = z0_ref.shape[1]
    row = lax.broadcasted_iota(jnp.int32, (length, 1), 0)
    alt = jnp.where(row % 2 == 0, 1.0, -1.0).astype(F32)

    def short_conv(z_ref, c):
        u = z_ref[0].astype(F32)
        prev = jnp.where(row == 0, 0.0, pltpu.roll(u, 1, axis=0))
        nxt = jnp.where(row == length - 1, 0.0, pltpu.roll(u, length - 1, axis=0))
        return cb_ref[c] + (prev * cw_ref[0, c] + u * cw_ref[1, c] + nxt * cw_ref[2, c])

    def long_conv(u, o):
        ubf_ref[...] = u.astype(BF16)
        nyq = jnp.sum(u * alt, axis=0, keepdims=True) * hn_ref[o]
        y_ref[...] = alt * nyq + u * skip_ref[o]
        for kt in range(length // HY_KT):
            ks = slice(kt * HY_KT, (kt + 1) * HY_KT)
            ub = ubf_ref[...]
            a = jnp.dot(c_ref[ks, :], ub, preferred_element_type=F32)
            b = jnp.dot(s_ref[ks, :], ub, preferred_element_type=F32)
            hr = h_ref[o, 0, ks, :]
            hi = h_ref[o, 1, ks, :]
            re = (a * hr + b * hi).astype(BF16)
            im = (a * hi - b * hr).astype(BF16)
            y_ref[...] += (jnp.dot(c_ref[:, ks], re, preferred_element_type=F32)
                           - jnp.dot(s_ref[:, ks], im, preferred_element_type=F32))
        return y_ref[...]

    z = short_conv(z1_ref, 1) * long_conv(short_conv(z0_ref, 0), 0)
    o_ref[0] = short_conv(z2_ref, 2) * long_conv(z, 1)


def _hyena(z3, conv_w, conv_b, cmat, smat, spectra, hn, skip):
    b, s, _ = z3.shape
    tc = HY_TC
    nct = HY_C // tc
    cw = conv_w.reshape(3, 3, 1, HY_C)
    cb = conv_b.reshape(3, 1, HY_C)
    zspec = lambda chunk: pl.BlockSpec((1, s, tc), lambda c, i: (i, 0, chunk * nct + c))
    return pl.pallas_call(
        _hyena_kernel,
        out_shape=jax.ShapeDtypeStruct((b, s, HY_C), F32),
        grid=(nct, b),
        in_specs=[zspec(0), zspec(1), zspec(2),
                  pl.BlockSpec((3, 3, 1, tc), lambda c, i: (0, 0, 0, c)),
                  pl.BlockSpec((3, 1, tc), lambda c, i: (0, 0, c)),
                  _resident((s, s), lambda c, i: (0, 0)),
                  _resident((s, s), lambda c, i: (0, 0)),
                  _resident((2, 2, s, tc), lambda c, i: (0, 0, 0, c)),
                  pl.BlockSpec((2, 1, tc), lambda c, i: (0, 0, c)),
                  pl.BlockSpec((2, 1, tc), lambda c, i: (0, 0, c))],
        out_specs=pl.BlockSpec((1, s, tc), lambda c, i: (i, 0, c)),
        scratch_shapes=[pltpu.VMEM((s, tc), BF16), pltpu.VMEM((s, tc), F32)],
        compiler_params=_params("parallel", "parallel"),
        name="hyena",
    )(z3, z3, z3, cw, cb, cmat, smat, spectra, hn, skip.reshape(2, 1, HY_C))


def _rel_bucket(rel):
    half = NUM_BUCKETS // 2
    exact = half // 2
    n = np.abs(rel)
    large = exact + (np.log(np.maximum(n, 1) / exact) / np.log(REL_MAX_DIST / exact) * (half - exact)).astype(np.int32)
    large = np.minimum(large, half - 1)
    return (np.where(rel > 0, half, 0) + np.where(n < exact, n, large)).astype(np.int32)


def _att_tiles(length):
    kw = min(ATT_KW, length)
    tiles = []
    for qs in range(0, length, ATT_TQ):
        ks = min(max(qs - (kw - ATT_TQ) // 2, 0), length - kw)
        tiles.append((qs, ks, {0: 0, -64: 1, -128: 2}[ks - qs]))
    return kw, tiles


def _att_bias(rel_bias, group, seq):
    window, dil = AT_GROUPS[group]
    band = window // (2 * dil)
    kw, tiles = _att_tiles(seq // dil)
    offsets = sorted({ks - qs for qs, ks, _ in tiles}, reverse=True)
    table = rel_bias[:, group * AT_HEADS:(group + 1) * AT_HEADS].astype(F32)
    span = kw + ATT_TQ
    out = []
    for off in offsets:
        k = np.arange(span)
        rel = off + np.where(k < kw, k, k - span)
        diag = jnp.where(np.abs(rel) <= band, table[_rel_bucket(rel * dil)].T, NEG_INF)
        flat = jnp.tile(diag, (1, ATT_TQ))[:, :ATT_TQ * (span - 1)]
        out.append(flat.reshape(AT_HEADS, ATT_TQ, span - 1)[:, :, :kw])
    return jnp.stack(out)


def _dilated_kernel(q1, q2, q3, k1, k2, k3, v1, v2, v3, qg_ref, kg_ref, b1, b2, b3, o_ref,
                    qn_ref, kn_ref, vn_ref, og_ref, lg_ref):
    seq = o_ref.shape[1]
    scale = HD ** -0.5
    contract_last = (((1,), (1,)), ((), ()))
    for g, (q_ref, k_ref, v_ref, bias_ref) in enumerate(((q1, k1, v1, b1), (q2, k2, v2, b2), (q3, k3, v3, b3))):
        dil = AT_GROUPS[g][1]
        kw, tiles = _att_tiles(seq // dil)
        qn_ref[...] = _rms_rows(q_ref[0].astype(F32), qg_ref[...])
        kn_ref[...] = _rms_rows(k_ref[0].astype(F32), kg_ref[...])
        vn_ref[...] = v_ref[0].astype(F32)
        for r in range(dil):
            for qs, ks, var in tiles:
                rows = lambda start, size: (pl.ds(r + start * dil, size, stride=dil) if dil > 1
                                            else pl.ds(start, size))
                qt = qn_ref[rows(qs, ATT_TQ), :].astype(BF16)
                kt = kn_ref[rows(ks, kw), :].astype(BF16)
                vt = vn_ref[rows(ks, kw), :].astype(BF16)
                s = lax.dot_general(qt, kt, contract_last, preferred_element_type=F32) * scale + bias_ref[var, 0]
                m = jnp.max(s, axis=-1, keepdims=True)
                p = jnp.exp(s - m)
                z = jnp.sum(p, axis=-1, keepdims=True)
                og_ref[g, rows(qs, ATT_TQ), :] = jnp.dot(p.astype(BF16), vt, preferred_element_type=F32) / z
                lg_ref[g, rows(qs, ATT_TQ), :] = jnp.broadcast_to(m + jnp.log(z), (ATT_TQ, HD))
    l0, l1, l2 = lg_ref[0], lg_ref[1], lg_ref[2]
    mx = jnp.maximum(jnp.maximum(l0, l1), l2)
    e0, e1, e2 = jnp.exp(l0 - mx), jnp.exp(l1 - mx), jnp.exp(l2 - mx)
    o_ref[0] = (e0 * og_ref[0] + e1 * og_ref[1] + e2 * og_ref[2]) / (e0 + e1 + e2)


def _dilated_attention(z3, q_gain, k_gain, rel_bias):
    b, s, _ = z3.shape
    ng = len(AT_GROUPS)
    col = lambda part, g: pl.BlockSpec((1, s, HD), lambda i, h: (i, 0, part * ng * AT_HEADS + g * AT_HEADS + h))
    biases = [_att_bias(rel_bias, g, s) for g in range(ng)]
    bias_spec = lambda a: pl.BlockSpec((a.shape[0], 1) + a.shape[2:], lambda i, h: (0, h, 0, 0))
    gain_spec = pl.BlockSpec((1, HD), lambda i, h: (0, 0))
    return pl.pallas_call(
        _dilated_kernel,
        out_shape=jax.ShapeDtypeStruct((b, s, AT_HEADS * HD), F32),
        grid=(b, AT_HEADS),
        in_specs=[col(p, g) for p in range(3) for g in range(ng)] + [gain_spec, gain_spec]
                 + [bias_spec(a) for a in biases],
        out_specs=pl.BlockSpec((1, s, HD), lambda i, h: (i, 0, h)),
        scratch_shapes=[pltpu.VMEM((s, HD), F32), pltpu.VMEM((s, HD), F32), pltpu.VMEM((s, HD), F32),
                        pltpu.VMEM((ng, s, HD), F32), pltpu.VMEM((ng, s, HD), F32)],
        compiler_params=_params("parallel", "parallel"),
        name="dilated_attention",
    )(*([z3] * 9), q_gain.reshape(1, HD), k_gain.reshape(1, HD), *biases)


ROUTER_LANES = 128


def _router_kernel(x_ref, g_ref, w_ref, hp_ref, idx_ref, gate_ref, count_ref, carry_ref):
    tm = x_ref.shape[0]
    half = hp_ref.shape[1]

    @pl.when(pl.program_id(0) == 0)
    def _():
        carry_ref[...] = jnp.zeros_like(carry_ref)

    h = _rms_rows(x_ref[...], g_ref[...])
    hp_ref[...] = _bf16_bits(h[:, :half]) | (_bf16_bits(h[:, half:]) << 16)
    logits = jnp.dot(h, w_ref[...], preferred_element_type=F32, precision=lax.Precision.HIGHEST)
    lane = lax.broadcasted_iota(jnp.int32, logits.shape, 1).astype(F32)
    logits = jnp.where(lane < N_EXPERTS, logits, -jnp.inf)
    m1 = jnp.max(logits, axis=-1, keepdims=True)
    i1 = jnp.min(jnp.where(logits == m1, lane, float(ROUTER_LANES)), axis=-1, keepdims=True)
    rest = jnp.where(lane == i1, -jnp.inf, logits)
    m2 = jnp.max(rest, axis=-1, keepdims=True)
    i2 = jnp.min(jnp.where(rest == m2, lane, float(ROUTER_LANES)), axis=-1, keepdims=True)
    e2 = jnp.exp(m2 - m1)
    den = 1.0 + e2
    gate_ref[...] = jnp.where(lane == 0, 1.0 / den, e2 / den)
    chosen = jnp.where((lane == i1) | (lane == i2), 1.0, 0.0)
    earlier = lax.broadcasted_iota(jnp.int32, (tm, tm), 0) > lax.broadcasted_iota(jnp.int32, (tm, tm), 1)
    carry = carry_ref[...]
    before = jnp.dot(jnp.where(earlier, 1.0, 0.0).astype(BF16), chosen.astype(BF16),
                     preferred_element_type=F32) + carry
    r1 = jnp.sum(jnp.where(lane == i1, before, 0.0), axis=-1, keepdims=True)
    r2 = jnp.sum(jnp.where(lane == i2, before, 0.0), axis=-1, keepdims=True)
    idx_ref[...] = jnp.where(lane == 0, i1, jnp.where(lane == 1, i2, jnp.where(lane == 2, r1, r2))).astype(jnp.int32)
    carry = carry + jnp.sum(chosen, axis=0, keepdims=True)
    carry_ref[...] = carry
    count_ref[...] = carry.astype(jnp.int32)


def _router(x2, gain, router):
    rows, d = x2.shape
    tm = MOE_CHUNK
    w = jnp.zeros((d, ROUTER_LANES), F32).at[:, :N_EXPERTS].set(router.astype(F32))
    row_spec = lambda width: pl.BlockSpec((tm, width), lambda i: (i, 0))
    return pl.pallas_call(
        _router_kernel,
        out_shape=(jax.ShapeDtypeStruct((rows, d // 2), jnp.uint32),
                   jax.ShapeDtypeStruct((rows, ROUTER_LANES), jnp.int32),
                   jax.ShapeDtypeStruct((rows, ROUTER_LANES), F32),
                   jax.ShapeDtypeStruct((1, ROUTER_LANES), jnp.int32)),
        grid=(rows // tm,),
        in_specs=[row_spec(d),
                  pl.BlockSpec((1, d), lambda i: (0, 0)),
                  pl.BlockSpec((d, ROUTER_LANES), lambda i: (0, 0))],
        out_specs=(row_spec(d // 2), row_spec(ROUTER_LANES), row_spec(ROUTER_LANES),
                   pl.BlockSpec((1, ROUTER_LANES), lambda i: (0, 0))),
        scratch_shapes=[pltpu.VMEM((1, ROUTER_LANES), F32)],
        compiler_params=_params("arbitrary"),
        name="router",
    )(x2, gain.reshape(1, d), w)


def _sc_workers():
    info = plsc.get_sparse_core_info()
    mesh = plsc.VectorSubcoreMesh(core_axis_name="core", subcore_axis_name="subcore")
    return mesh, info.num_cores, info.num_subcores


def _sc_token_rows(t, nc, ns):
    per_w = t // (nc * ns)
    assert per_w * nc * ns == t and per_w % (2 * SC_ROWS) == 0
    return per_w, per_w // SC_ROWS


def _scatter_rows(table, dests, p_rows):
    t, w = table.shape
    nk = len(dests)
    mesh, nc, ns = _sc_workers()
    per_w, nit = _sc_token_rows(t, nc, ns)
    ch = SC_ROWS

    @functools.partial(
        pl.kernel, mesh=mesh, out_type=jax.ShapeDtypeStruct((p_rows, w), table.dtype),
        scratch_types=[pltpu.VMEM((nk, nit, ch), jnp.int32), pltpu.VMEM((2, ch, w), table.dtype),
                       pltpu.SemaphoreType.DMA((2,)), pltpu.SemaphoreType.DMA((2, nk))])
    def scatter(table_hbm, *refs):
        dest_hbm, out_hbm = refs[:nk], refs[nk]
        idx_v, rows_v, rsem, wsem = refs[nk + 1:]
        wid = lax.axis_index("subcore") * nc + lax.axis_index("core")
        for k in range(nk):
            pltpu.sync_copy(dest_hbm[k].at[pl.ds(wid * nit, nit)], idx_v.at[k])

        @pl.loop(0, nit // 2)
        def _(it):
            reads = [pltpu.make_async_copy(table_hbm.at[pl.ds(wid * per_w + (2 * it + b) * ch, ch)],
                                           rows_v.at[b], rsem.at[b]) for b in range(2)]
            writes = [[pltpu.make_async_copy(rows_v.at[b], out_hbm.at[idx_v.at[k].at[2 * it + b]], wsem.at[b, k])
                       for k in range(nk)] for b in range(2)]
            reads[0].start()
            reads[1].start()
            for b in range(2):
                reads[b].wait()
                for k in range(nk):
                    writes[b][k].start()
            for b in range(2):
                for k in range(nk):
                    writes[b][k].wait()

    return scatter(table, *[d.reshape(t // ch, ch) for d in dests])


def _gather_rows(table, idxs):
    t = idxs[0].shape[0]
    w = table.shape[1]
    nk = len(idxs)
    mesh, nc, ns = _sc_workers()
    per_w, nit = _sc_token_rows(t, nc, ns)
    ch = SC_ROWS

    @functools.partial(
        pl.kernel, mesh=mesh, out_type=[jax.ShapeDtypeStruct((t, w), table.dtype)] * nk,
        scratch_types=[pltpu.VMEM((nk, nit, ch), jnp.int32), pltpu.VMEM((nk, 2, ch, w), table.dtype),
                       pltpu.SemaphoreType.DMA((nk, 2)), pltpu.SemaphoreType.DMA((nk, 2))])
    def gather(table_hbm, *refs):
        idx_hbm, out_hbm = refs[:nk], refs[nk:2 * nk]
        idx_v, rows_v, rsem, wsem = refs[2 * nk:]
        wid = lax.axis_index("subcore") * nc + lax.axis_index("core")
        for k in range(nk):
            pltpu.sync_copy(idx_hbm[k].at[pl.ds(wid * nit, nit)], idx_v.at[k])

        @pl.loop(0, nit // 2)
        def _(it):
            slots = [(k, b) for k in range(nk) for b in range(2)]
            reads = {(k, b): pltpu.make_async_copy(table_hbm.at[idx_v.at[k].at[2 * it + b]], rows_v.at[k, b],
                                                   rsem.at[k, b]) for k, b in slots}
            writes = {(k, b): pltpu.make_async_copy(rows_v.at[k, b],
                                                    out_hbm[k].at[pl.ds(wid * per_w + (2 * it + b) * ch, ch)],
                                                    wsem.at[k, b]) for k, b in slots}
            for s in slots:
                reads[s].start()
            for s in slots:
                reads[s].wait()
                writes[s].start()
            for s in slots:
                writes[s].wait()

    return gather(table, *[i.reshape(t // ch, ch) for i in idxs])


def _combine_kernel(x_ref, y0_ref, y1_ref, gate_ref, o_ref):
    half = x_ref.shape[1] // 2
    for part in range(2):
        cols = slice(part * half, (part + 1) * half)
        acc = x_ref[:, cols]
        for k, y_ref in enumerate((y0_ref, y1_ref)):
            word = y_ref[...]
            bits = (word << 16) if part == 0 else (word & jnp.uint32(0xFFFF0000))
            acc = acc + gate_ref[:, k:k + 1] * lax.bitcast_convert_type(bits, F32)
        o_ref[:, cols] = acc


def _combine(x2, y0, y1, gates):
    t, d = x2.shape
    tm = ROW_TILE
    return pl.pallas_call(
        _combine_kernel,
        out_shape=jax.ShapeDtypeStruct((t, d), F32),
        grid=(t // tm,),
        in_specs=[pl.BlockSpec((tm, d), lambda i: (i, 0)),
                  pl.BlockSpec((tm, d // 2), lambda i: (i, 0)),
                  pl.BlockSpec((tm, d // 2), lambda i: (i, 0)),
                  pl.BlockSpec((tm, TOP_K), lambda i: (i, 0))],
        out_specs=pl.BlockSpec((tm, d), lambda i: (i, 0)),
        compiler_params=_params("parallel"),
        name="moe_combine",
    )(x2, y0, y1, gates)


def _moe(x2, gain, router, wg, wu, wd):
    t, d = x2.shape
    hp, idx, gate, count = _router(x2, gain, router)
    experts = jnp.arange(N_EXPERTS, dtype=jnp.int32)
    counts = count[0, :N_EXPERTS]
    padded = (counts + MOE_ROWS - 1) // MOE_ROWS * MOE_ROWS
    pend = jnp.cumsum(padded)
    pstart = pend - padded
    expert, rank = idx[:, :TOP_K], idx[:, TOP_K:2 * TOP_K]
    dest = jnp.sum(jnp.where(expert[:, :, None] == experts, pstart, 0), axis=-1) + rank
    dests = [dest[:, k] for k in range(TOP_K)]
    p_rows = t * TOP_K + N_EXPERTS * MOE_ROWS
    blk_row = jnp.arange(p_rows // MOE_ROWS, dtype=jnp.int32) * MOE_ROWS
    blk_expert = jnp.minimum(jnp.searchsorted(pend, blk_row, side='right'), N_EXPERTS - 1).astype(jnp.int32)
    valid = jnp.clip(counts[blk_expert] - (blk_row - pstart[blk_expert]), 0, MOE_ROWS)
    valid = jnp.where(blk_row < pend[-1], valid, 0).astype(jnp.int32)
    hb = _scatter_rows(hp, dests, p_rows)
    yb = _ffn_experts(hb, wg, wu, wd, blk_expert, valid)
    y0, y1 = _gather_rows(yb, dests)
    return _combine(x2, y0, y1, gate[:, :TOP_K])


def kernel(x, mem, rel_bias, norm_mix, norm_mem, norm_ffn, w_mem_kv, xq_norm, xk_norm, w_out, hy_w_in, hy_conv_w, hy_conv_b, hy_filt_w1, hy_filt_b1, hy_filt_w2, hy_filt_b2, hy_filt_w3, hy_sin_freq, hy_skip, at_w_in, at_q_norm, at_k_norm, ffn_w_gate, ffn_w_up, ffn_w_down, moe_router, moe_w_gate, moe_w_up, moe_w_down):
    b, s, d = x.shape
    t = b * s
    m_len = mem.shape[1]
    x2 = x.reshape(t, d)
    mem2 = mem.reshape(b * m_len, d)
    bf = lambda w: w.astype(BF16)

    cmat, smat = _dft_tables(s)
    spectra, hn = _hyena_filters(s, cmat, smat, hy_filt_w1[0], hy_filt_b1[0], hy_filt_w2[0], hy_filt_b2[0],
                                 hy_filt_w3[0], hy_sin_freq[0])
    z = _norm_matmul(x2, norm_mix[0], bf(hy_w_in[0]), 1024).reshape(b, s, -1)
    kv = _norm_matmul(mem2, norm_mem[0], bf(w_mem_kv[0]), 1024).reshape(b, m_len, -1)
    self_out = _hyena(z, hy_conv_w[0], hy_conv_b[0], cmat, smat, spectra, hn, hy_skip[0])
    cross = _xattn(z, 3 * HY_C // XA_W, kv, xq_norm[0], xk_norm[0])
    x2 = _outproj(x2, self_out.reshape(t, -1), cross.reshape(t, -1), bf(w_out[0]))
    x2 = _ffn_dense(x2, norm_ffn[0], bf(ffn_w_gate), bf(ffn_w_up), bf(ffn_w_down))

    z = _norm_matmul(x2, norm_mix[1], bf(at_w_in[0]), 1024).reshape(b, s, -1)
    kv = _norm_matmul(mem2, norm_mem[1], bf(w_mem_kv[1]), 1024).reshape(b, m_len, -1)
    self_out = _dilated_attention(z, at_q_norm[0], at_k_norm[0], rel_bias)
    cross = _xattn(z, 3 * AT_W // XA_W, kv, xq_norm[1], xk_norm[1])
    x2 = _outproj(x2, self_out.reshape(t, -1), cross.reshape(t, -1), bf(w_out[1]))
    x2 = _moe(x2, norm_ffn[1], moe_router[0], bf(moe_w_gate[0]), bf(moe_w_up[0]), bf(moe_w_down[0]))
    return x2.reshape(b, s, d)
```

```python
import functools
import math

import jax
import jax.numpy as jnp
import numpy as np
from jax import lax
from jax.experimental import pallas as pl
from jax.experimental.pallas import tpu as pltpu
from jax.experimental.pallas import tpu_sc as plsc

F32 = jnp.float32
BF16 = jnp.bfloat16

D_MODEL = 1024
EPS = 1e-6
HY_C = 512
FILT_BANDS = 16
DECAY_TARGET = 1e-2
FAST_DECAY_PCT = 0.3
SLOW_DECAY_PCT = 1.5
MOD_SHIFT = 0.05
AT_GROUPS = ((128, 1), (512, 4), (2048, 16))
AT_HEADS = 4
HD = 128
AT_W = 1536
NUM_BUCKETS = 32
REL_MAX_DIST = 1024
NEG_INF = -1e30
XA_W = 512
D_FF = 2816
N_EXPERTS = 8
TOP_K = 2

VMEM_LIMIT_BYTES = 56 * 1024 * 1024
ROW_TILE = 512
FF_TILE = 1408
MOE_ROWS = 512
MOE_CHUNK = 512
SC_ROWS = 32
ATT_TQ = 128
ATT_KW = 256
HY_TC = 256
HY_KT = 512


def _params(*sem):
    return pltpu.CompilerParams(dimension_semantics=sem, vmem_limit_bytes=VMEM_LIMIT_BYTES)


def _rms_rows(x, gain):
    return x * lax.rsqrt(jnp.mean(x * x, axis=-1, keepdims=True) + EPS) * gain


def _bf16_bits(v):
    u = lax.bitcast_convert_type(v, jnp.uint32)
    return (u + jnp.uint32(0x7FFF) + ((u >> 16) & jnp.uint32(1))) >> 16


def _resident(shape, index_map):
    return pl.BlockSpec(shape, index_map, pipeline_mode=pl.Buffered(1))


def _norm_matmul_kernel(x_ref, g_ref, w_ref, o_ref, *, tn):
    h = _rms_rows(x_ref[...], g_ref[...]).astype(BF16)
    for c in range(o_ref.shape[1] // tn):
        cols = slice(c * tn, (c + 1) * tn)
        o_ref[:, cols] = jnp.dot(h, w_ref[:, cols], preferred_element_type=F32).astype(o_ref.dtype)


def _norm_matmul(x2, gain, w_bf, tn):
    rows, d = x2.shape
    n = w_bf.shape[1]
    tm = min(ROW_TILE, rows)
    return pl.pallas_call(
        functools.partial(_norm_matmul_kernel, tn=tn),
        out_shape=jax.ShapeDtypeStruct((rows, n), BF16),
        grid=(rows // tm,),
        in_specs=[pl.BlockSpec((tm, d), lambda i: (i, 0)),
                  pl.BlockSpec((1, d), lambda i: (0, 0)),
                  _resident((d, n), lambda i: (0, 0))],
        out_specs=pl.BlockSpec((tm, n), lambda i: (i, 0)),
        compiler_params=_params("parallel"),
        name="norm_matmul",
    )(x2, gain.reshape(1, d), w_bf)


def _outproj_kernel(x_ref, a_ref, c_ref, wa_ref, wc_ref, o_ref):
    o_ref[...] = (x_ref[...]
                  + jnp.dot(a_ref[...].astype(BF16), wa_ref[...], preferred_element_type=F32)
                  + jnp.dot(c_ref[...].astype(BF16), wc_ref[...], preferred_element_type=F32))


def _outproj(x2, self_out, cross, w_bf):
    rows, d = x2.shape
    half = self_out.shape[1]
    tm = ROW_TILE
    return pl.pallas_call(
        _outproj_kernel,
        out_shape=jax.ShapeDtypeStruct((rows, d), F32),
        grid=(rows // tm,),
        in_specs=[pl.BlockSpec((tm, d), lambda i: (i, 0)),
                  pl.BlockSpec((tm, half), lambda i: (i, 0)),
                  pl.BlockSpec((tm, half), lambda i: (i, 0)),
                  pl.BlockSpec((half, d), lambda i: (0, 0)),
                  pl.BlockSpec((half, d), lambda i: (1, 0))],
        out_specs=pl.BlockSpec((tm, d), lambda i: (i, 0)),
        compiler_params=_params("parallel"),
        name="outproj",
    )(x2, self_out, cross, w_bf, w_bf)


def _swiglu_accumulate(h, wg_ref, wu_ref, wd_ref, acc_ref):
    gg = jnp.dot(h, wg_ref[0], preferred_element_type=F32)
    uu = jnp.dot(h, wu_ref[0], preferred_element_type=F32)
    a = (gg * jax.nn.sigmoid(gg)) * uu
    acc_ref[...] += jnp.dot(a.astype(BF16), wd_ref[0], preferred_element_type=F32)


def _ffn_dense_kernel(x_ref, g_ref, wg_ref, wu_ref, wd_ref, o_ref, *, tf):
    x = x_ref[...]
    h = _rms_rows(x, g_ref[...]).astype(BF16)
    y = x
    for j in range(wg_ref.shape[1] // tf):
        cols = slice(j * tf, (j + 1) * tf)
        gg = jnp.dot(h, wg_ref[:, cols], preferred_element_type=F32)
        uu = jnp.dot(h, wu_ref[:, cols], preferred_element_type=F32)
        a = (gg * jax.nn.sigmoid(gg)) * uu
        y = y + jnp.dot(a.astype(BF16), wd_ref[cols, :], preferred_element_type=F32)
    o_ref[...] = y


def _ffn_dense(x2, gain, wg, wu, wd):
    rows, d = x2.shape
    f = wg.shape[1]
    tm = ROW_TILE
    return pl.pallas_call(
        functools.partial(_ffn_dense_kernel, tf=FF_TILE),
        out_shape=jax.ShapeDtypeStruct((rows, d), F32),
        grid=(rows // tm,),
        in_specs=[pl.BlockSpec((tm, d), lambda i: (i, 0)),
                  pl.BlockSpec((1, d), lambda i: (0, 0)),
                  _resident((d, f), lambda i: (0, 0)),
                  _resident((d, f), lambda i: (0, 0)),
                  _resident((f, d), lambda i: (0, 0))],
        out_specs=pl.BlockSpec((tm, d), lambda i: (i, 0)),
        compiler_params=_params("parallel"),
        name="ffn_dense",
    )(x2, gain.reshape(1, d), wg, wu, wd)


def _ffn_expert_kernel(eid_ref, valid_ref, hp_ref, wg_ref, wu_ref, wd_ref, o_ref, h_ref, acc_ref):
    i, j = pl.program_id(0), pl.program_id(1)
    half = hp_ref.shape[1]
    valid = valid_ref[i]

    @pl.when(j == 0)
    def _():
        keep = lax.broadcasted_iota(jnp.int32, (hp_ref.shape[0], 1), 0) < valid
        word = hp_ref[...]
        h_ref[:, :half] = jnp.where(keep, lax.bitcast_convert_type(word << 16, F32), 0.0).astype(BF16)
        h_ref[:, half:] = jnp.where(keep, lax.bitcast_convert_type(word & jnp.uint32(0xFFFF0000), F32),
                                    0.0).astype(BF16)
        acc_ref[...] = jnp.zeros_like(acc_ref)

    @pl.when(valid > 0)
    def _():
        _swiglu_accumulate(h_ref[...], wg_ref, wu_ref, wd_ref, acc_ref)

    @pl.when(j == pl.num_programs(1) - 1)
    def _():
        o_ref[...] = _bf16_bits(acc_ref[:, :half]) | (_bf16_bits(acc_ref[:, half:]) << 16)


def _ffn_experts(hp, wg, wu, wd, eid, valid):
    rows, half = hp.shape
    d = 2 * half
    f = wg.shape[2]
    tm, tf = MOE_ROWS, FF_TILE
    grid_spec = pltpu.PrefetchScalarGridSpec(
        num_scalar_prefetch=2,
        grid=(rows // tm, f // tf),
        in_specs=[pl.BlockSpec((tm, half), lambda i, j, e, n: (i, 0)),
                  pl.BlockSpec((1, d, tf), lambda i, j, e, n: (e[i], 0, j)),
                  pl.BlockSpec((1, d, tf), lambda i, j, e, n: (e[i], 0, j)),
                  pl.BlockSpec((1, tf, d), lambda i, j, e, n: (e[i], j, 0))],
        out_specs=pl.BlockSpec((tm, half), lambda i, j, e, n: (i, 0)),
        scratch_shapes=[pltpu.VMEM((tm, d), BF16), pltpu.VMEM((tm, d), F32)],
    )
    return pl.pallas_call(
        _ffn_expert_kernel,
        out_shape=jax.ShapeDtypeStruct((rows, half), jnp.uint32),
        grid_spec=grid_spec,
        compiler_params=_params("parallel", "arbitrary"),
        name="ffn_experts",
    )(eid, valid, hp, wg, wu, wd)


def _xattn_kernel(xq_ref, kv_ref, qg_ref, kg_ref, o_ref):
    scale = HD ** -0.5
    for h in range(XA_W // HD):
        cols = slice(h * HD, (h + 1) * HD)
        q = _rms_rows(xq_ref[0, :, cols].astype(F32), qg_ref[...]).astype(BF16)
        k = _rms_rows(kv_ref[0, :, cols].astype(F32), kg_ref[...]).astype(BF16)
        v = kv_ref[0, :, XA_W + h * HD:XA_W + (h + 1) * HD]
        s = lax.dot_general(q, k, (((1,), (1,)), ((), ())), preferred_element_type=F32) * scale
        p = jnp.exp(s - jnp.max(s, axis=-1, keepdims=True))
        z = jnp.sum(p, axis=-1, keepdims=True)
        o_ref[0, :, cols] = jnp.dot(p.astype(BF16), v, preferred_element_type=F32) / z


def _xattn(z3, xq_block, kv3, q_gain, k_gain):
    b, s, _ = z3.shape
    m = kv3.shape[1]
    ts = ROW_TILE
    return pl.pallas_call(
        _xattn_kernel,
        out_shape=jax.ShapeDtypeStruct((b, s, XA_W), F32),
        grid=(b, s // ts),
        in_specs=[pl.BlockSpec((1, ts, XA_W), lambda i, j: (i, j, xq_block)),
                  pl.BlockSpec((1, m, 2 * XA_W), lambda i, j: (i, 0, 0)),
                  pl.BlockSpec((1, HD), lambda i, j: (0, 0)),
                  pl.BlockSpec((1, HD), lambda i, j: (0, 0))],
        out_specs=pl.BlockSpec((1, ts, XA_W), lambda i, j: (i, j, 0)),
        compiler_params=_params("parallel", "parallel"),
        name="cross_attention",
    )(z3, kv3, q_gain.reshape(1, HD), k_gain.reshape(1, HD))


def _dft_kernel(c_ref, s_ref, c0_ref, s0_ref, *, n_fft):
    rows, cols = c_ref.shape
    i = pl.program_id(0)

    @pl.when(i == 0)
    def _():
        k = lax.broadcasted_iota(jnp.int32, (rows, cols), 0)
        n = lax.broadcasted_iota(jnp.int32, (rows, cols), 1)
        ang = ((k * n) & (n_fft - 1)).astype(F32) * (2.0 * math.pi / n_fft)
        c0_ref[...] = jnp.cos(ang)
        s0_ref[...] = jnp.sin(ang)

    period = n_fft // rows
    n = lax.broadcasted_iota(jnp.int32, (8, cols), 1)
    shift = ((i * n) & (period - 1)).astype(F32) * (2.0 * math.pi / period)
    ca = jnp.cos(shift)[0:1]
    sa = jnp.sin(shift)[0:1]
    c0 = c0_ref[...]
    s0 = s0_ref[...]
    c_ref[...] = (ca * c0 - sa * s0).astype(BF16)
    s_ref[...] = (sa * c0 + ca * s0).astype(BF16)


def _dft_tables(length):
    rows = 256
    shape = jax.ShapeDtypeStruct((length, length), BF16)
    spec = pl.BlockSpec((rows, length), lambda i: (i, 0))
    return pl.pallas_call(
        functools.partial(_dft_kernel, n_fft=2 * length),
        out_shape=(shape, shape),
        grid=(length // rows,),
        out_specs=(spec, spec),
        scratch_shapes=[pltpu.VMEM((rows, length), F32), pltpu.VMEM((rows, length), F32)],
        compiler_params=_params("arbitrary"),
        name="dft_tables",
    )()


def _filter_time_kernel(feats_ref, w1_ref, b1_ref, w2_ref, b2_ref, fr_ref, w3_ref, t_ref, delta_ref,
                        gp_ref, gm_ref, hn_ref):
    hp = lax.Precision.HIGHEST
    length = feats_ref.shape[0]
    fr = fr_ref[...]
    h = jnp.sin(fr * (jnp.dot(feats_ref[...], w1_ref[...], preferred_element_type=F32, precision=hp) + b1_ref[...]))
    h = jnp.sin(fr * (jnp.dot(h, w2_ref[...], preferred_element_type=F32, precision=hp) + b2_ref[...]))
    mod = jnp.exp(-t_ref[...] * delta_ref[...]) + MOD_SHIFT
    row = lax.broadcasted_iota(jnp.int32, (length, 1), 0)
    alt = jnp.where(row % 2 == 0, 1.0, -1.0).astype(F32)
    for o in range(2):
        fwd = jnp.dot(h, w3_ref[2 * o], preferred_element_type=F32, precision=hp) * mod
        bwd = jnp.dot(h, w3_ref[2 * o + 1], preferred_element_type=F32, precision=hp) * mod
        bwd = jnp.where(row == 0, 0.0, bwd)
        norm = (jnp.sum(jnp.abs(fwd), axis=0, keepdims=True)
                + jnp.sum(jnp.abs(bwd), axis=0, keepdims=True) + 1e-6)
        fwd = fwd / norm
        bwd = bwd / norm
        gp = fwd + bwd
        gp_ref[o] = gp.astype(BF16)
        gm_ref[o] = (fwd - bwd).astype(BF16)
        hn_ref[o] = jnp.sum(gp * alt, axis=0, keepdims=True) * (1.0 / (2 * length))


def _filter_freq_kernel(c_ref, s_ref, gp_ref, gm_ref, h_ref):
    length = c_ref.shape[0]
    row = lax.broadcasted_iota(jnp.int32, (length, 1), 0)
    scale = jnp.where(row == 0, 1.0, 2.0).astype(F32) * (1.0 / (2 * length))
    h_ref[0, 0] = jnp.dot(c_ref[...], gp_ref[0], preferred_element_type=F32) * scale
    h_ref[0, 1] = -jnp.dot(s_ref[...], gm_ref[0], preferred_element_type=F32) * scale


def _hyena_filters(length, cmat, smat, w1, b1, w2, b2, w3, freq):
    t = jnp.linspace(0.0, 1.0, length, dtype=F32)[:, None]
    f = jnp.linspace(1e-4, FILT_BANDS - 1, FILT_BANDS, dtype=F32)[None]
    ang = (2.0 * math.pi / length) * jnp.arange(length, dtype=F32)[:, None] * f
    feats = jnp.concatenate([t, jnp.cos(ang), -jnp.sin(ang)], axis=-1)
    deltas = jnp.abs(jnp.linspace(math.log(DECAY_TARGET) / SLOW_DECAY_PCT,
                                  math.log(DECAY_TARGET) / FAST_DECAY_PCT, HY_C, dtype=F32))[None]
    hid = w1.shape[1]
    w3r = w3.reshape(hid, 4, HY_C).transpose(1, 0, 2)
    tc = HY_TC
    full = lambda shape: pl.BlockSpec(shape, lambda c: (0,) * len(shape))
    gp, gm, hn = pl.pallas_call(
        _filter_time_kernel,
        out_shape=(jax.ShapeDtypeStruct((2, length, HY_C), BF16),
                   jax.ShapeDtypeStruct((2, length, HY_C), BF16),
                   jax.ShapeDtypeStruct((2, 1, HY_C), F32)),
        grid=(HY_C // tc,),
        in_specs=[full(feats.shape), full(w1.shape), full((1, hid)), full(w2.shape), full((1, hid)),
                  full((1, hid)), pl.BlockSpec((4, hid, tc), lambda c: (0, 0, c)), full((length, 1)),
                  pl.BlockSpec((1, tc), lambda c: (0, c))],
        out_specs=(pl.BlockSpec((2, length, tc), lambda c: (0, 0, c)),
                   pl.BlockSpec((2, length, tc), lambda c: (0, 0, c)),
                   pl.BlockSpec((2, 1, tc), lambda c: (0, 0, c))),
        compiler_params=_params("parallel"),
        name="filter_time",
    )(feats, w1, b1.reshape(1, hid), w2, b2.reshape(1, hid), freq.reshape(1, hid), w3r, t, deltas)
    spectra = pl.pallas_call(
        _filter_freq_kernel,
        out_shape=jax.ShapeDtypeStruct((2, 2, length, HY_C), F32),
        grid=(2, HY_C // tc),
        in_specs=[_resident((length, length), lambda o, c: (0, 0)),
                  _resident((length, length), lambda o, c: (0, 0)),
                  pl.BlockSpec((1, length, tc), lambda o, c: (o, 0, c)),
                  pl.BlockSpec((1, length, tc), lambda o, c: (o, 0, c))],
        out_specs=pl.BlockSpec((1, 2, length, tc), lambda o, c: (o, 0, 0, c)),
        compiler_params=_params("parallel", "parallel"),
        name="filter_freq",
    )(cmat, smat, gp, gm)
    return spectra, hn


def _hyena_kernel(z0_ref, z1_ref, z2_ref, cw_ref, cb_ref, c_ref, s_ref, h_ref, hn_ref, skip_ref, o_ref,
                  ubf_ref, y_ref):
    length = z0_ref.shape[1]
    row = lax.broadcasted_iota(jnp.int32, (length, 1), 0)
    alt = jnp.where(row % 2 == 0, 1.0, -1.0).astype(F32)

    def short_conv(z_ref, c):
        u = z_ref[0].astype(F32)
        prev = jnp.where(row == 0, 0.0, pltpu.roll(u, 1, axis=0))
        nxt = jnp.where(row == length - 1, 0.0, pltpu.roll(u, length - 1, axis=0))
        return cb_ref[c] + (prev * cw_ref[0, c] + u * cw_ref[1, c] + nxt * cw_ref[2, c])

    def long_conv(u, o):
        ubf_ref[...] = u.astype(BF16)
        nyq = jnp.sum(u * alt, axis=0, keepdims=True) * hn_ref[o]
        y_ref[...] = alt * nyq + u * skip_ref[o]
        for kt in range(length // HY_KT):
            ks = slice(kt * HY_KT, (kt + 1) * HY_KT)
            ub = ubf_ref[...]
            a = jnp.dot(c_ref[ks, :], ub, preferred_element_type=F32)
            b = jnp.dot(s_ref[ks, :], ub, preferred_element_type=F32)
            hr = h_ref[o, 0, ks, :]
            hi = h_ref[o, 1, ks, :]
            re = (a * hr + b * hi).astype(BF16)
            im = (a * hi - b * hr).astype(BF16)
            y_ref[...] += (jnp.dot(c_ref[:, ks], re, preferred_element_type=F32)
                           - jnp.dot(s_ref[:, ks], im, preferred_element_type=F32))
        return y_ref[...]

    z = short_conv(z1_ref, 1) * long_conv(short_conv(z0_ref, 0), 0)
    o_ref[0] = short_conv(z2_ref, 2) * long_conv(z, 1)


def _hyena(z3, conv_w, conv_b, cmat, smat, spectra, hn, skip):
    b, s, _ = z3.shape
    tc = HY_TC
    nct = HY_C // tc
    cw = conv_w.reshape(3, 3, 1, HY_C)
    cb = conv_b.reshape(3, 1, HY_C)
    zspec = lambda chunk: pl.BlockSpec((1, s, tc), lambda c, i: (i, 0, chunk * nct + c))
    return pl.pallas_call(
        _hyena_kernel,
        out_shape=jax.ShapeDtypeStruct((b, s, HY_C), F32),
        grid=(nct, b),
        in_specs=[zspec(0), zspec(1), zspec(2),
                  pl.BlockSpec((3, 3, 1, tc), lambda c, i: (0, 0, 0, c)),
                  pl.BlockSpec((3, 1, tc), lambda c, i: (0, 0, c)),
                  _resident((s, s), lambda c, i: (0, 0)),
                  _resident((s, s), lambda c, i: (0, 0)),
                  _resident((2, 2, s, tc), lambda c, i: (0, 0, 0, c)),
                  pl.BlockSpec((2, 1, tc), lambda c, i: (0, 0, c)),
                  pl.BlockSpec((2, 1, tc), lambda c, i: (0, 0, c))],
        out_specs=pl.BlockSpec((1, s, tc), lambda c, i: (i, 0, c)),
        scratch_shapes=[pltpu.VMEM((s, tc), BF16), pltpu.VMEM((s, tc), F32)],
        compiler_params=_params("parallel", "parallel"),
        name="hyena",
    )(z3, z3, z3, cw, cb, cmat, smat, spectra, hn, skip.reshape(2, 1, HY_C))


def _rel_bucket(rel):
    half = NUM_BUCKETS // 2
    exact = half // 2
    n = np.abs(rel)
    large = exact + (np.log(np.maximum(n, 1) / exact) / np.log(REL_MAX_DIST / exact) * (half - exact)).astype(np.int32)
    large = np.minimum(large, half - 1)
    return (np.where(rel > 0, half, 0) + np.where(n < exact, n, large)).astype(np.int32)


def _att_tiles(length):
    kw = min(ATT_KW, length)
    tiles = []
    for qs in range(0, length, ATT_TQ):
        ks = min(max(qs - (kw - ATT_TQ) // 2, 0), length - kw)
        tiles.append((qs, ks, {0: 0, -64: 1, -128: 2}[ks - qs]))
    return kw, tiles


def _att_bias(rel_bias, group, seq):
    window, dil = AT_GROUPS[group]
    band = window // (2 * dil)
    kw, tiles = _att_tiles(seq // dil)
    offsets = sorted({ks - qs for qs, ks, _ in tiles}, reverse=True)
    table = rel_bias[:, group * AT_HEADS:(group + 1) * AT_HEADS].astype(F32)
    span = kw + ATT_TQ
    out = []
    for off in offsets:
        k = np.arange(span)
        rel = off + np.where(k < kw, k, k - span)
        diag = jnp.where(np.abs(rel) <= band, table[_rel_bucket(rel * dil)].T, NEG_INF)
        flat = jnp.tile(diag, (1, ATT_TQ))[:, :ATT_TQ * (span - 1)]
        out.append(flat.reshape(AT_HEADS, ATT_TQ, span - 1)[:, :, :kw])
    return jnp.stack(out)


def _dilated_kernel(q1, q2, q3, k1, k2, k3, v1, v2, v3, qg_ref, kg_ref, b1, b2, b3, o_ref,
                    qn_ref, kn_ref, vn_ref, og_ref, lg_ref):
    seq = o_ref.shape[1]
    scale = HD ** -0.5
    contract_last = (((1,), (1,)), ((), ()))
    for g, (q_ref, k_ref, v_ref, bias_ref) in enumerate(((q1, k1, v1, b1), (q2, k2, v2, b2), (q3, k3, v3, b3))):
        dil = AT_GROUPS[g][1]
        kw, tiles = _att_tiles(seq // dil)
        qn_ref[...] = _rms_rows(q_ref[0].astype(F32), qg_ref[...])
        kn_ref[...] = _rms_rows(k_ref[0].astype(F32), kg_ref[...])
        vn_ref[...] = v_ref[0].astype(F32)
        for r in range(dil):
            for qs, ks, var in tiles:
                rows = lambda start, size: (pl.ds(r + start * dil, size, stride=dil) if dil > 1
                                            else pl.ds(start, size))
                qt = qn_ref[rows(qs, ATT_TQ), :].astype(BF16)
                kt = kn_ref[rows(ks, kw), :].astype(BF16)
                vt = vn_ref[rows(ks, kw), :].astype(BF16)
                s = lax.dot_general(qt, kt, contract_last, preferred_element_type=F32) * scale + bias_ref[var, 0]
                m = jnp.max(s, axis=-1, keepdims=True)
                p = jnp.exp(s - m)
                z = jnp.sum(p, axis=-1, keepdims=True)
                og_ref[g, rows(qs, ATT_TQ), :] = jnp.dot(p.astype(BF16), vt, preferred_element_type=F32) / z
                lg_ref[g, rows(qs, ATT_TQ), :] = jnp.broadcast_to(m + jnp.log(z), (ATT_TQ, HD))
    l0, l1, l2 = lg_ref[0], lg_ref[1], lg_ref[2]
    mx = jnp.maximum(jnp.maximum(l0, l1), l2)
    e0, e1, e2 = jnp.exp(l0 - mx), jnp.exp(l1 - mx), jnp.exp(l2 - mx)
    o_ref[0] = (e0 * og_ref[0] + e1 * og_ref[1] + e2 * og_ref[2]) / (e0 + e1 + e2)


def _dilated_attention(z3, q_gain, k_gain, rel_bias):
    b, s, _ = z3.shape
    ng = len(AT_GROUPS)
    col = lambda part, g: pl.BlockSpec((1, s, HD), lambda i, h: (i, 0, part * ng * AT_HEADS + g * AT_HEADS + h))
    biases = [_att_bias(rel_bias, g, s) for g in range(ng)]
    bias_spec = lambda a: pl.BlockSpec((a.shape[0], 1) + a.shape[2:], lambda i, h: (0, h, 0, 0))
    gain_spec = pl.BlockSpec((1, HD), lambda i, h: (0, 0))
    return pl.pallas_call(
        _dilated_kernel,
        out_shape=jax.ShapeDtypeStruct((b, s, AT_HEADS * HD), F32),
        grid=(b, AT_HEADS),
        in_specs=[col(p, g) for p in range(3) for g in range(ng)] + [gain_spec, gain_spec]
                 + [bias_spec(a) for a in biases],
        out_specs=pl.BlockSpec((1, s, HD), lambda i, h: (i, 0, h)),
        scratch_shapes=[pltpu.VMEM((s, HD), F32), pltpu.VMEM((s, HD), F32), pltpu.VMEM((s, HD), F32),
                        pltpu.VMEM((ng, s, HD), F32), pltpu.VMEM((ng, s, HD), F32)],
        compiler_params=_params("parallel", "parallel"),
        name="dilated_attention",
    )(*([z3] * 9), q_gain.reshape(1, HD), k_gain.reshape(1, HD), *biases)


ROUTER_LANES = 128


def _router_kernel(x_ref, g_ref, w_ref, hp_ref, idx_ref, gate_ref, count_ref, carry_ref):
    tm = x_ref.shape[0]
    half = hp_ref.shape[1]

    @pl.when(pl.program_id(0) == 0)
    def _():
        carry_ref[...] = jnp.zeros_like(carry_ref)

    h = _rms_rows(x_ref[...], g_ref[...])
    hp_ref[...] = _bf16_bits(h[:, :half]) | (_bf16_bits(h[:, half:]) << 16)
    logits = jnp.dot(h, w_ref[...], preferred_element_type=F32, precision=lax.Precision.HIGHEST)
    lane = lax.broadcasted_iota(jnp.int32, logits.shape, 1).astype(F32)
    logits = jnp.where(lane < N_EXPERTS, logits, -jnp.inf)
    m1 = jnp.max(logits, axis=-1, keepdims=True)
    i1 = jnp.min(jnp.where(logits == m1, lane, float(ROUTER_LANES)), axis=-1, keepdims=True)
    rest = jnp.where(lane == i1, -jnp.inf, logits)
    m2 = jnp.max(rest, axis=-1, keepdims=True)
    i2 = jnp.min(jnp.where(rest == m2, lane, float(ROUTER_LANES)), axis=-1, keepdims=True)
    e2 = jnp.exp(m2 - m1)
    den = 1.0 + e2
    gate_ref[...] = jnp.where(lane == 0, 1.0 / den, e2 / den)
    chosen = jnp.where((lane == i1) | (lane == i2), 1.0, 0.0)
    earlier = lax.broadcasted_iota(jnp.int32, (tm, tm), 0) > lax.broadcasted_iota(jnp.int32, (tm, tm), 1)
    carry = carry_ref[...]
    before = jnp.dot(jnp.where(earlier, 1.0, 0.0).astype(BF16), chosen.astype(BF16),
                     preferred_element_type=F32) + carry
    r1 = jnp.sum(jnp.where(lane == i1, before, 0.0), axis=-1, keepdims=True)
    r2 = jnp.sum(jnp.where(lane == i2, before, 0.0), axis=-1, keepdims=True)
    idx_ref[...] = jnp.where(lane == 0, i1, jnp.where(lane == 1, i2, jnp.where(lane == 2, r1, r2))).astype(jnp.int32)
    carry = carry + jnp.sum(chosen, axis=0, keepdims=True)
    carry_ref[...] = carry
    count_ref[...] = carry.astype(jnp.int32)


def _router(x2, gain, router):
    rows, d = x2.shape
    tm = MOE_CHUNK
    w = jnp.zeros((d, ROUTER_LANES), F32).at[:, :N_EXPERTS].set(router.astype(F32))
    row_spec = lambda width: pl.BlockSpec((tm, width), lambda i: (i, 0))
    return pl.pallas_call(
        _router_kernel,
        out_shape=(jax.ShapeDtypeStruct((rows, d // 2), jnp.uint32),
                   jax.ShapeDtypeStruct((rows, ROUTER_LANES), jnp.int32),
                   jax.ShapeDtypeStruct((rows, ROUTER_LANES), F32),
                   jax.ShapeDtypeStruct((1, ROUTER_LANES), jnp.int32)),
        grid=(rows // tm,),
        in_specs=[row_spec(d),
                  pl.BlockSpec((1, d), lambda i: (0, 0)),
                  pl.BlockSpec((d, ROUTER_LANES), lambda i: (0, 0))],
        out_specs=(row_spec(d // 2), row_spec(ROUTER_LANES), row_spec(ROUTER_LANES),
                   pl.BlockSpec((1, ROUTER_LANES), lambda i: (0, 0))),
        scratch_shapes=[pltpu.VMEM((1, ROUTER_LANES), F32)],
        compiler_params=_params("arbitrary"),
        name="router",
    )(x2, gain.reshape(1, d), w)


def _sc_workers():
    info = plsc.get_sparse_core_info()
    mesh = plsc.VectorSubcoreMesh(core_axis_name="core", subcore_axis_name="subcore")
    return mesh, info.num_cores, info.num_subcores


def _sc_token_rows(t, nc, ns):
    per_w = t // (nc * ns)
    assert per_w * nc * ns == t and per_w % (2 * SC_ROWS) == 0
    return per_w, per_w // SC_ROWS


def _scatter_rows(table, dests, p_rows):
    t, w = table.shape
    nk = len(dests)
    mesh, nc, ns = _sc_workers()
    per_w, nit = _sc_token_rows(t, nc, ns)
    ch = SC_ROWS

    @functools.partial(
        pl.kernel, mesh=mesh, out_type=jax.ShapeDtypeStruct((p_rows, w), table.dtype),
        scratch_types=[pltpu.VMEM((nk, nit, ch), jnp.int32), pltpu.VMEM((2, ch, w), table.dtype),
                       pltpu.SemaphoreType.DMA((2,)), pltpu.SemaphoreType.DMA((2, nk))])
    def scatter(table_hbm, *refs):
        dest_hbm, out_hbm = refs[:nk], refs[nk]
        idx_v, rows_v, rsem, wsem = refs[nk + 1:]
        wid = lax.axis_index("subcore") * nc + lax.axis_index("core")
        for k in range(nk):
            pltpu.sync_copy(dest_hbm[k].at[pl.ds(wid * nit, nit)], idx_v.at[k])

        @pl.loop(0, nit // 2)
        def _(it):
            reads = [pltpu.make_async_copy(table_hbm.at[pl.ds(wid * per_w + (2 * it + b) * ch, ch)],
                                           rows_v.at[b], rsem.at[b]) for b in range(2)]
            writes = [[pltpu.make_async_copy(rows_v.at[b], out_hbm.at[idx_v.at[k].at[2 * it + b]], wsem.at[b, k])
                       for k in range(nk)] for b in range(2)]
            reads[0].start()
            reads[1].start()
            for b in range(2):
                reads[b].wait()
                for k in range(nk):
                    writes[b][k].start()
            for b in range(2):
                for k in range(nk):
                    writes[b][k].wait()

    return scatter(table, *[d.reshape(t // ch, ch) for d in dests])


def _gather_rows(table, idxs):
    t = idxs[0].shape[0]
    w = table.shape[1]
    nk = len(idxs)
    mesh, nc, ns = _sc_workers()
    per_w, nit = _sc_token_rows(t, nc, ns)
    ch = SC_ROWS

    @functools.partial(
        pl.kernel, mesh=mesh, out_type=[jax.ShapeDtypeStruct((t, w), table.dtype)] * nk,
        scratch_types=[pltpu.VMEM((nk, nit, ch), jnp.int32), pltpu.VMEM((nk, 2, ch, w), table.dtype),
                       pltpu.SemaphoreType.DMA((nk, 2)), pltpu.SemaphoreType.DMA((nk, 2))])
    def gather(table_hbm, *refs):
        idx_hbm, out_hbm = refs[:nk], refs[nk:2 * nk]
        idx_v, rows_v, rsem, wsem = refs[2 * nk:]
        wid = lax.axis_index("subcore") * nc + lax.axis_index("core")
        for k in range(nk):
            pltpu.sync_copy(idx_hbm[k].at[pl.ds(wid * nit, nit)], idx_v.at[k])

        @pl.loop(0, nit // 2)
        def _(it):
            slots = [(k, b) for k in range(nk) for b in range(2)]
            reads = {(k, b): pltpu.make_async_copy(table_hbm.at[idx_v.at[k].at[2 * it + b]], rows_v.at[k, b],
                                                   rsem.at[k, b]) for k, b in slots}
            writes = {(k, b): pltpu.make_async_copy(rows_v.at[k, b],
                                                    out_hbm[k].at[pl.ds(wid * per_w + (2 * it + b) * ch, ch)],
                                                    wsem.at[k, b]) for k, b in slots}
            for s in slots:
                reads[s].start()
            for s in slots:
                reads[s].wait()
                writes[s].start()
            for s in slots:
                writes[s].wait()

    return gather(table, *[i.reshape(t // ch, ch) for i in idxs])


def _combine_kernel(x_ref, y0_ref, y1_ref, gate_ref, o_ref):
    half = x_ref.shape[1] // 2
    for part in range(2):
        cols = slice(part * half, (part + 1) * half)
        acc = x_ref[:, cols]
        for k, y_ref in enumerate((y0_ref, y1_ref)):
            word = y_ref[...]
            bits = (word << 16) if part == 0 else (word & jnp.uint32(0xFFFF0000))
            acc = acc + gate_ref[:, k:k + 1] * lax.bitcast_convert_type(bits, F32)
        o_ref[:, cols] = acc


def _combine(x2, y0, y1, gates):
    t, d = x2.shape
    tm = ROW_TILE
    return pl.pallas_call(
        _combine_kernel,
        out_shape=jax.ShapeDtypeStruct((t, d), F32),
        grid=(t // tm,),
        in_specs=[pl.BlockSpec((tm, d), lambda i: (i, 0)),
                  pl.BlockSpec((tm, d // 2), lambda i: (i, 0)),
                  pl.BlockSpec((tm, d // 2), lambda i: (i, 0)),
                  pl.BlockSpec((tm, TOP_K), lambda i: (i, 0))],
        out_specs=pl.BlockSpec((tm, d), lambda i: (i, 0)),
        compiler_params=_params("parallel"),
        name="moe_combine",
    )(x2, y0, y1, gates)


def _moe(x2, gain, router, wg, wu, wd):
    t, d = x2.shape
    hp, idx, gate, count = _router(x2, gain, router)
    experts = jnp.arange(N_EXPERTS, dtype=jnp.int32)
    counts = count[0, :N_EXPERTS]
    padded = (counts + MOE_ROWS - 1) // MOE_ROWS * MOE_ROWS
    pend = jnp.cumsum(padded)
    pstart = pend - padded
    expert, rank = idx[:, :TOP_K], idx[:, TOP_K:2 * TOP_K]
    dest = jnp.sum(jnp.where(expert[:, :, None] == experts, pstart, 0), axis=-1) + rank
    dests = [dest[:, k] for k in range(TOP_K)]
    p_rows = t * TOP_K + N_EXPERTS * MOE_ROWS
    blk_row = jnp.arange(p_rows // MOE_ROWS, dtype=jnp.int32) * MOE_ROWS
    blk_expert = jnp.minimum(jnp.searchsorted(pend, blk_row, side='right'), N_EXPERTS - 1).astype(jnp.int32)
    valid = jnp.clip(counts[blk_expert] - (blk_row - pstart[blk_expert]), 0, MOE_ROWS)
    valid = jnp.where(blk_row < pend[-1], valid, 0).astype(jnp.int32)
    hb = _scatter_rows(hp, dests, p_rows)
    yb = _ffn_experts(hb, wg, wu, wd, blk_expert, valid)
    y0, y1 = _gather_rows(yb, dests)
    return _combine(x2, y0, y1, gate[:, :TOP_K])


def kernel(x, mem, rel_bias, norm_mix, norm_mem, norm_ffn, w_mem_kv, xq_norm, xk_norm, w_out, hy_w_in, hy_conv_w, hy_conv_b, hy_filt_w1, hy_filt_b1, hy_filt_w2, hy_filt_b2, hy_filt_w3, hy_sin_freq, hy_skip, at_w_in, at_q_norm, at_k_norm, ffn_w_gate, ffn_w_up, ffn_w_down, moe_router, moe_w_gate, moe_w_up, moe_w_down):
    b, s, d = x.shape
    t = b * s
    m_len = mem.shape[1]
    x2 = x.reshape(t, d)
    mem2 = mem.reshape(b * m_len, d)
    bf = lambda w: w.astype(BF16)

    cmat, smat = _dft_tables(s)
    spectra, hn = _hyena_filters(s, cmat, smat, hy_filt_w1[0], hy_filt_b1[0], hy_filt_w2[0], hy_filt_b2[0],
                                 hy_filt_w3[0], hy_sin_freq[0])
    z = _norm_matmul(x2, norm_mix[0], bf(hy_w_in[0]), 1024).reshape(b, s, -1)
    kv = _norm_matmul(mem2, norm_mem[0], bf(w_mem_kv[0]), 1024).reshape(b, m_len, -1)
    self_out = _hyena(z, hy_conv_w[0], hy_conv_b[0], cmat, smat, spectra, hn, hy_skip[0])
    cross = _xattn(z, 3 * HY_C // XA_W, kv, xq_norm[0], xk_norm[0])
    x2 = _outproj(x2, self_out.reshape(t, -1), cross.reshape(t, -1), bf(w_out[0]))
    x2 = _ffn_dense(x2, norm_ffn[0], bf(ffn_w_gate[0]), bf(ffn_w_up[0]), bf(ffn_w_down[0]))

    z = _norm_matmul(x2, norm_mix[1], bf(at_w_in[0]), 1024).reshape(b, s, -1)
    kv = _norm_matmul(mem2, norm_mem[1], bf(w_mem_kv[1]), 1024).reshape(b, m_len, -1)
    self_out = _dilated_attention(z, at_q_norm[0], at_k_norm[0], rel_bias)
    cross = _xattn(z, 3 * AT_W // XA_W, kv, xq_norm[1], xk_norm[1])
    x2 = _outproj(x2, self_out.reshape(t, -1), cross.reshape(t, -1), bf(w_out[1]))
    x2 = _moe(x2, norm_ffn[1], moe_router[0], bf(moe_w_gate[0]), bf(moe_w_up[0]), bf(moe_w_down[0]))
    return x2.reshape(b, s, d)
```

```python
import functools
import math

import jax
import jax.numpy as jnp
import numpy as np
from jax import lax
from jax.experimental import pallas as pl
from jax.experimental.pallas import tpu as pltpu
from jax.experimental.pallas import tpu_sc as plsc

F32 = jnp.float32
BF16 = jnp.bfloat16

D_MODEL = 1024
EPS = 1e-6
HY_C = 512
FILT_BANDS = 16
DECAY_TARGET = 1e-2
FAST_DECAY_PCT = 0.3
SLOW_DECAY_PCT = 1.5
MOD_SHIFT = 0.05
AT_GROUPS = ((128, 1), (512, 4), (2048, 16))
AT_HEADS = 4
HD = 128
AT_W = 1536
NUM_BUCKETS = 32
REL_MAX_DIST = 1024
NEG_INF = -1e30
XA_W = 512
D_FF = 2816
N_EXPERTS = 8
TOP_K = 2

VMEM_LIMIT_BYTES = 56 * 1024 * 1024
ROW_TILE = 512
FF_TILE = 1408
MOE_ROWS = 512
MOE_CHUNK = 512
SC_ROWS = 32
ATT_TQ = 128
ATT_KW = 256
HY_TC = 256
HY_KT = 512


def _params(*sem):
    return pltpu.CompilerParams(dimension_semantics=sem, vmem_limit_bytes=VMEM_LIMIT_BYTES)


def _rms_rows(x, gain):
    return x * lax.rsqrt(jnp.mean(x * x, axis=-1, keepdims=True) + EPS) * gain


def _bf16_bits(v):
    u = lax.bitcast_convert_type(v, jnp.uint32)
    return (u + jnp.uint32(0x7FFF) + ((u >> 16) & jnp.uint32(1))) >> 16


def _resident(shape, index_map):
    return pl.BlockSpec(shape, index_map, pipeline_mode=pl.Buffered(1))


def _norm_matmul_kernel(x_ref, g_ref, w_ref, o_ref, *, tn):
    h = _rms_rows(x_ref[...], g_ref[...]).astype(BF16)
    for c in range(o_ref.shape[1] // tn):
        cols = slice(c * tn, (c + 1) * tn)
        o_ref[:, cols] = jnp.dot(h, w_ref[:, cols], preferred_element_type=F32).astype(o_ref.dtype)


def _norm_matmul(x2, gain, w_bf, tn):
    rows, d = x2.shape
    n = w_bf.shape[1]
    tm = min(ROW_TILE, rows)
    return pl.pallas_call(
        functools.partial(_norm_matmul_kernel, tn=tn),
        out_shape=jax.ShapeDtypeStruct((rows, n), BF16),
        grid=(rows // tm,),
        in_specs=[pl.BlockSpec((tm, d), lambda i: (i, 0)),
                  pl.BlockSpec((1, d), lambda i: (0, 0)),
                  _resident((d, n), lambda i: (0, 0))],
        out_specs=pl.BlockSpec((tm, n), lambda i: (i, 0)),
        compiler_params=_params("parallel"),
        name="norm_matmul",
    )(x2, gain.reshape(1, d), w_bf)


def _outproj_kernel(x_ref, a_ref, c_ref, wa_ref, wc_ref, o_ref):
    o_ref[...] = (x_ref[...]
                  + jnp.dot(a_ref[...].astype(BF16), wa_ref[...], preferred_element_type=F32)
                  + jnp.dot(c_ref[...].astype(BF16), wc_ref[...], preferred_element_type=F32))


def _outproj(x2, self_out, cross, w_bf):
    rows, d = x2.shape
    half = self_out.shape[1]
    tm = ROW_TILE
    return pl.pallas_call(
        _outproj_kernel,
        out_shape=jax.ShapeDtypeStruct((rows, d), F32),
        grid=(rows // tm,),
        in_specs=[pl.BlockSpec((tm, d), lambda i: (i, 0)),
                  pl.BlockSpec((tm, half), lambda i: (i, 0)),
                  pl.BlockSpec((tm, half), lambda i: (i, 0)),
                  pl.BlockSpec((half, d), lambda i: (0, 0)),
                  pl.BlockSpec((half, d), lambda i: (1, 0))],
        out_specs=pl.BlockSpec((tm, d), lambda i: (i, 0)),
        compiler_params=_params("parallel"),
        name="outproj",
    )(x2, self_out, cross, w_bf, w_bf)


def _swiglu(h, wg_ref, wu_ref, wd_ref, tf):
    y = None
    for j in range(wg_ref.shape[1] // tf):
        cols = slice(j * tf, (j + 1) * tf)
        gg = jnp.dot(h, wg_ref[:, cols], preferred_element_type=F32)
        uu = jnp.dot(h, wu_ref[:, cols], preferred_element_type=F32)
        a = ((gg * jax.nn.sigmoid(gg)) * uu).astype(BF16)
        part = jnp.dot(a, wd_ref[cols, :], preferred_element_type=F32)
        y = part if y is None else y + part
    return y


def _ffn_dense_kernel(x_ref, g_ref, wg_ref, wu_ref, wd_ref, o_ref, *, tf):
    x = x_ref[...]
    h = _rms_rows(x, g_ref[...]).astype(BF16)
    o_ref[...] = x + _swiglu(h, wg_ref, wu_ref, wd_ref, tf)


def _ffn_dense(x2, gain, wg, wu, wd):
    rows, d = x2.shape
    f = wg.shape[1]
    tm = ROW_TILE
    return pl.pallas_call(
        functools.partial(_ffn_dense_kernel, tf=FF_TILE),
        out_shape=jax.ShapeDtypeStruct((rows, d), F32),
        grid=(rows // tm,),
        in_specs=[pl.BlockSpec((tm, d), lambda i: (i, 0)),
                  pl.BlockSpec((1, d), lambda i: (0, 0)),
                  _resident((d, f), lambda i: (0, 0)),
                  _resident((d, f), lambda i: (0, 0)),
                  _resident((f, d), lambda i: (0, 0))],
        out_specs=pl.BlockSpec((tm, d), lambda i: (i, 0)),
        compiler_params=_params("parallel"),
        name="ffn_dense",
    )(x2, gain.reshape(1, d), wg, wu, wd)


def _ffn_expert_kernel(eid_ref, valid_ref, hp_ref, wg_ref, wu_ref, wd_ref, o_ref, h_ref, *, tf):
    half = hp_ref.shape[1]
    valid = valid_ref[pl.program_id(0)]

    @pl.when(valid > 0)
    def _():
        keep = lax.broadcasted_iota(jnp.int32, (hp_ref.shape[0], 1), 0) < valid
        word = hp_ref[...]
        h_ref[:, :half] = jnp.where(keep, lax.bitcast_convert_type(word << 16, F32), 0.0).astype(BF16)
        h_ref[:, half:] = jnp.where(keep, lax.bitcast_convert_type(word & jnp.uint32(0xFFFF0000), F32),
                                    0.0).astype(BF16)
        y = _swiglu(h_ref[...], wg_ref.at[0], wu_ref.at[0], wd_ref.at[0], tf)
        o_ref[...] = _bf16_bits(y[:, :half]) | (_bf16_bits(y[:, half:]) << 16)

    @pl.when(valid <= 0)
    def _():
        o_ref[...] = jnp.zeros_like(o_ref)


def _ffn_experts(hp, wg, wu, wd, eid, valid):
    rows, half = hp.shape
    d = 2 * half
    f = wg.shape[2]
    tm = MOE_ROWS
    grid_spec = pltpu.PrefetchScalarGridSpec(
        num_scalar_prefetch=2,
        grid=(rows // tm,),
        in_specs=[pl.BlockSpec((tm, half), lambda i, e, n: (i, 0)),
                  pl.BlockSpec((1, d, f), lambda i, e, n: (e[i], 0, 0)),
                  pl.BlockSpec((1, d, f), lambda i, e, n: (e[i], 0, 0)),
                  pl.BlockSpec((1, f, d), lambda i, e, n: (e[i], 0, 0))],
        out_specs=pl.BlockSpec((tm, half), lambda i, e, n: (i, 0)),
        scratch_shapes=[pltpu.VMEM((tm, d), BF16)],
    )
    return pl.pallas_call(
        functools.partial(_ffn_expert_kernel, tf=FF_TILE),
        out_shape=jax.ShapeDtypeStruct((rows, half), jnp.uint32),
        grid_spec=grid_spec,
        compiler_params=_params("arbitrary"),
        name="ffn_experts",
    )(eid, valid, hp, wg, wu, wd)


def _xattn_kernel(xq_ref, kv_ref, qg_ref, kg_ref, o_ref):
    scale = HD ** -0.5
    for h in range(XA_W // HD):
        cols = slice(h * HD, (h + 1) * HD)
        q = _rms_rows(xq_ref[0, :, cols].astype(F32), qg_ref[...]).astype(BF16)
        k = _rms_rows(kv_ref[0, :, cols].astype(F32), kg_ref[...]).astype(BF16)
        v = kv_ref[0, :, XA_W + h * HD:XA_W + (h + 1) * HD]
        s = lax.dot_general(q, k, (((1,), (1,)), ((), ())), preferred_element_type=F32) * scale
        p = jnp.exp(s - jnp.max(s, axis=-1, keepdims=True))
        z = jnp.sum(p, axis=-1, keepdims=True)
        o_ref[0, :, cols] = jnp.dot(p.astype(BF16), v, preferred_element_type=F32) / z


def _xattn(z3, xq_block, kv3, q_gain, k_gain):
    b, s, _ = z3.shape
    m = kv3.shape[1]
    ts = ROW_TILE
    return pl.pallas_call(
        _xattn_kernel,
        out_shape=jax.ShapeDtypeStruct((b, s, XA_W), F32),
        grid=(b, s // ts),
        in_specs=[pl.BlockSpec((1, ts, XA_W), lambda i, j: (i, j, xq_block)),
                  pl.BlockSpec((1, m, 2 * XA_W), lambda i, j: (i, 0, 0)),
                  pl.BlockSpec((1, HD), lambda i, j: (0, 0)),
                  pl.BlockSpec((1, HD), lambda i, j: (0, 0))],
        out_specs=pl.BlockSpec((1, ts, XA_W), lambda i, j: (i, j, 0)),
        compiler_params=_params("parallel", "parallel"),
        name="cross_attention",
    )(z3, kv3, q_gain.reshape(1, HD), k_gain.reshape(1, HD))


def _dft_kernel(c_ref, s_ref, c0_ref, s0_ref, *, n_fft):
    rows, cols = c_ref.shape
    i = pl.program_id(0)

    @pl.when(i == 0)
    def _():
        k = lax.broadcasted_iota(jnp.int32, (rows, cols), 0)
        n = lax.broadcasted_iota(jnp.int32, (rows, cols), 1)
        ang = ((k * n) & (n_fft - 1)).astype(F32) * (2.0 * math.pi / n_fft)
        c0_ref[...] = jnp.cos(ang)
        s0_ref[...] = jnp.sin(ang)

    period = n_fft // rows
    n = lax.broadcasted_iota(jnp.int32, (8, cols), 1)
    shift = ((i * n) & (period - 1)).astype(F32) * (2.0 * math.pi / period)
    ca = jnp.cos(shift)[0:1]
    sa = jnp.sin(shift)[0:1]
    c0 = c0_ref[...]
    s0 = s0_ref[...]
    c_ref[...] = (ca * c0 - sa * s0).astype(BF16)
    s_ref[...] = (sa * c0 + ca * s0).astype(BF16)


def _dft_tables(length):
    rows = 256
    shape = jax.ShapeDtypeStruct((length, length), BF16)
    spec = pl.BlockSpec((rows, length), lambda i: (i, 0))
    return pl.pallas_call(
        functools.partial(_dft_kernel, n_fft=2 * length),
        out_shape=(shape, shape),
        grid=(length // rows,),
        out_specs=(spec, spec),
        scratch_shapes=[pltpu.VMEM((rows, length), F32), pltpu.VMEM((rows, length), F32)],
        compiler_params=_params("arbitrary"),
        name="dft_tables",
    )()


def _filter_time_kernel(feats_ref, w1_ref, b1_ref, w2_ref, b2_ref, fr_ref, w3_ref, t_ref, delta_ref,
                        gp_ref, gm_ref, hn_ref):
    hp = lax.Precision.HIGHEST
    length = feats_ref.shape[0]
    fr = fr_ref[...]
    h = jnp.sin(fr * (jnp.dot(feats_ref[...], w1_ref[...], preferred_element_type=F32, precision=hp) + b1_ref[...]))
    h = jnp.sin(fr * (jnp.dot(h, w2_ref[...], preferred_element_type=F32, precision=hp) + b2_ref[...]))
    mod = jnp.exp(-t_ref[...] * delta_ref[...]) + MOD_SHIFT
    row = lax.broadcasted_iota(jnp.int32, (length, 1), 0)
    alt = jnp.where(row % 2 == 0, 1.0, -1.0).astype(F32)
    for o in range(2):
        fwd = jnp.dot(h, w3_ref[2 * o], preferred_element_type=F32, precision=hp) * mod
        bwd = jnp.dot(h, w3_ref[2 * o + 1], preferred_element_type=F32, precision=hp) * mod
        bwd = jnp.where(row == 0, 0.0, bwd)
        norm = (jnp.sum(jnp.abs(fwd), axis=0, keepdims=True)
                + jnp.sum(jnp.abs(bwd), axis=0, keepdims=True) + 1e-6)
        fwd = fwd / norm
        bwd = bwd / norm
        gp = fwd + bwd
        gp_ref[o] = gp.astype(BF16)
        gm_ref[o] = (fwd - bwd).astype(BF16)
        hn_ref[o] = jnp.sum(gp * alt, axis=0, keepdims=True) * (1.0 / (2 * length))


def _filter_freq_kernel(c_ref, s_ref, gp_ref, gm_ref, h_ref):
    length = c_ref.shape[0]
    row = lax.broadcasted_iota(jnp.int32, (length, 1), 0)
    scale = jnp.where(row == 0, 1.0, 2.0).astype(F32) * (1.0 / (2 * length))
    h_ref[0, 0] = jnp.dot(c_ref[...], gp_ref[0], preferred_element_type=F32) * scale
    h_ref[0, 1] = -jnp.dot(s_ref[...], gm_ref[0], preferred_element_type=F32) * scale


def _hyena_filters(length, cmat, smat, w1, b1, w2, b2, w3, freq):
    t = jnp.linspace(0.0, 1.0, length, dtype=F32)[:, None]
    f = jnp.linspace(1e-4, FILT_BANDS - 1, FILT_BANDS, dtype=F32)[None]
    ang = (2.0 * math.pi / length) * jnp.arange(length, dtype=F32)[:, None] * f
    feats = jnp.concatenate([t, jnp.cos(ang), -jnp.sin(ang)], axis=-1)
    deltas = jnp.abs(jnp.linspace(math.log(DECAY_TARGET) / SLOW_DECAY_PCT,
                                  math.log(DECAY_TARGET) / FAST_DECAY_PCT, HY_C, dtype=F32))[None]
    hid = w1.shape[1]
    w3r = w3.reshape(hid, 4, HY_C).transpose(1, 0, 2)
    tc = HY_TC
    full = lambda shape: pl.BlockSpec(shape, lambda c: (0,) * len(shape))
    gp, gm, hn = pl.pallas_call(
        _filter_time_kernel,
        out_shape=(jax.ShapeDtypeStruct((2, length, HY_C), BF16),
                   jax.ShapeDtypeStruct((2, length, HY_C), BF16),
                   jax.ShapeDtypeStruct((2, 1, HY_C), F32)),
        grid=(HY_C // tc,),
        in_specs=[full(feats.shape), full(w1.shape), full((1, hid)), full(w2.shape), full((1, hid)),
                  full((1, hid)), pl.BlockSpec((4, hid, tc), lambda c: (0, 0, c)), full((length, 1)),
                  pl.BlockSpec((1, tc), lambda c: (0, c))],
        out_specs=(pl.BlockSpec((2, length, tc), lambda c: (0, 0, c)),
                   pl.BlockSpec((2, length, tc), lambda c: (0, 0, c)),
                   pl.BlockSpec((2, 1, tc), lambda c: (0, 0, c))),
        compiler_params=_params("parallel"),
        name="filter_time",
    )(feats, w1, b1.reshape(1, hid), w2, b2.reshape(1, hid), freq.reshape(1, hid), w3r, t, deltas)
    spectra = pl.pallas_call(
        _filter_freq_kernel,
        out_shape=jax.ShapeDtypeStruct((2, 2, length, HY_C), F32),
        grid=(2, HY_C // tc),
        in_specs=[_resident((length, length), lambda o, c: (0, 0)),
                  _resident((length, length), lambda o, c: (0, 0)),
                  pl.BlockSpec((1, length, tc), lambda o, c: (o, 0, c)),
                  pl.BlockSpec((1, length, tc), lambda o, c: (o, 0, c))],
        out_specs=pl.BlockSpec((1, 2, length, tc), lambda o, c: (o, 0, 0, c)),
        compiler_params=_params("parallel", "parallel"),
        name="filter_freq",
    )(cmat, smat, gp, gm)
    return spectra, hn


def _hyena_kernel(z0_ref, z1_ref, z2_ref, cw_ref, cb_ref, c_ref, s_ref, h_ref, hn_ref, skip_ref, o_ref,
                  ubf_ref, y_ref):
    length = z0_ref.shape[1]
    row = lax.broadcasted_iota(jnp.int32, (length, 1), 0)
    alt = jnp.where(row % 2 == 0, 1.0, -1.0).astype(F32)

    def short_conv(z_ref, c):
        u = z_ref[0].astype(F32)
        prev = jnp.where(row == 0, 0.0, pltpu.roll(u, 1, axis=0))
        nxt = jnp.where(row == length - 1, 0.0, pltpu.roll(u, length - 1, axis=0))
        return cb_ref[c] + (prev * cw_ref[0, c] + u * cw_ref[1, c] + nxt * cw_ref[2, c])

    def long_conv(u, o):
        ubf_ref[...] = u.astype(BF16)
        nyq = jnp.sum(u * alt, axis=0, keepdims=True) * hn_ref[o]
        y_ref[...] = alt * nyq + u * skip_ref[o]
        for kt in range(length // HY_KT):
            ks = slice(kt * HY_KT, (kt + 1) * HY_KT)
            ub = ubf_ref[...]
            a = jnp.dot(c_ref[ks, :], ub, preferred_element_type=F32)
            b = jnp.dot(s_ref[ks, :], ub, preferred_element_type=F32)
            hr = h_ref[o, 0, ks, :]
            hi = h_ref[o, 1, ks, :]
            re = (a * hr + b * hi).astype(BF16)
            im = (a * hi - b * hr).astype(BF16)
            y_ref[...] += (jnp.dot(c_ref[:, ks], re, preferred_element_type=F32)
                           - jnp.dot(s_ref[:, ks], im, preferred_element_type=F32))
        return y_ref[...]

    z = short_conv(z1_ref, 1) * long_conv(short_conv(z0_ref, 0), 0)
    o_ref[0] = short_conv(z2_ref, 2) * long_conv(z, 1)


def _hyena(z3, conv_w, conv_b, cmat, smat, spectra, hn, skip):
    b, s, _ = z3.shape
    tc = HY_TC
    nct = HY_C // tc
    cw = conv_w.reshape(3, 3, 1, HY_C)
    cb = conv_b.reshape(3, 1, HY_C)
    zspec = lambda chunk: pl.BlockSpec((1, s, tc), lambda c, i: (i, 0, chunk * nct + c))
    return pl.pallas_call(
        _hyena_kernel,
        out_shape=jax.ShapeDtypeStruct((b, s, HY_C), F32),
        grid=(nct, b),
        in_specs=[zspec(0), zspec(1), zspec(2),
                  pl.BlockSpec((3, 3, 1, tc), lambda c, i: (0, 0, 0, c)),
                  pl.BlockSpec((3, 1, tc), lambda c, i: (0, 0, c)),
                  _resident((s, s), lambda c, i: (0, 0)),
                  _resident((s, s), lambda c, i: (0, 0)),
                  _resident((2, 2, s, tc), lambda c, i: (0, 0, 0, c)),
                  pl.BlockSpec((2, 1, tc), lambda c, i: (0, 0, c)),
                  pl.BlockSpec((2, 1, tc), lambda c, i: (0, 0, c))],
        out_specs=pl.BlockSpec((1, s, tc), lambda c, i: (i, 0, c)),
        scratch_shapes=[pltpu.VMEM((s, tc), BF16), pltpu.VMEM((s, tc), F32)],
        compiler_params=_params("parallel", "parallel"),
        name="hyena",
    )(z3, z3, z3, cw, cb, cmat, smat, spectra, hn, skip.reshape(2, 1, HY_C))


def _rel_bucket(rel):
    half = NUM_BUCKETS // 2
    exact = half // 2
    n = np.abs(rel)
    large = exact + (np.log(np.maximum(n, 1) / exact) / np.log(REL_MAX_DIST / exact) * (half - exact)).astype(np.int32)
    large = np.minimum(large, half - 1)
    return (np.where(rel > 0, half, 0) + np.where(n < exact, n, large)).astype(np.int32)


def _att_tiles(length):
    kw = min(ATT_KW, length)
    tiles = []
    for qs in range(0, length, ATT_TQ):
        ks = min(max(qs - (kw - ATT_TQ) // 2, 0), length - kw)
        tiles.append((qs, ks, {0: 0, -64: 1, -128: 2}[ks - qs]))
    return kw, tiles


def _att_bias(rel_bias, group, seq):
    window, dil = AT_GROUPS[group]
    band = window // (2 * dil)
    kw, tiles = _att_tiles(seq // dil)
    offsets = sorted({ks - qs for qs, ks, _ in tiles}, reverse=True)
    table = rel_bias[:, group * AT_HEADS:(group + 1) * AT_HEADS].astype(F32)
    span = kw + ATT_TQ
    out = []
    for off in offsets:
        k = np.arange(span)
        rel = off + np.where(k < kw, k, k - span)
        diag = jnp.where(np.abs(rel) <= band, table[_rel_bucket(rel * dil)].T, NEG_INF)
        flat = jnp.tile(diag, (1, ATT_TQ))[:, :ATT_TQ * (span - 1)]
        out.append(flat.reshape(AT_HEADS, ATT_TQ, span - 1)[:, :, :kw])
    return jnp.stack(out)


def _dilated_kernel(q1, q2, q3, k1, k2, k3, v1, v2, v3, qg_ref, kg_ref, b1, b2, b3, o_ref,
                    qn_ref, kn_ref, vn_ref, og_ref, lg_ref):
    seq = o_ref.shape[1]
    scale = HD ** -0.5
    contract_last = (((1,), (1,)), ((), ()))
    for g, (q_ref, k_ref, v_ref, bias_ref) in enumerate(((q1, k1, v1, b1), (q2, k2, v2, b2), (q3, k3, v3, b3))):
        dil = AT_GROUPS[g][1]
        kw, tiles = _att_tiles(seq // dil)
        qn_ref[...] = _rms_rows(q_ref[0].astype(F32), qg_ref[...])
        kn_ref[...] = _rms_rows(k_ref[0].astype(F32), kg_ref[...])
        vn_ref[...] = v_ref[0].astype(F32)
        for r in range(dil):
            for qs, ks, var in tiles:
                rows = lambda start, size: (pl.ds(r + start * dil, size, stride=dil) if dil > 1
                                            else pl.ds(start, size))
                qt = qn_ref[rows(qs, ATT_TQ), :].astype(BF16)
                kt = kn_ref[rows(ks, kw), :].astype(BF16)
                vt = vn_ref[rows(ks, kw), :].astype(BF16)
                s = lax.dot_general(qt, kt, contract_last, preferred_element_type=F32) * scale + bias_ref[var, 0]
                m = jnp.max(s, axis=-1, keepdims=True)
                p = jnp.exp(s - m)
                z = jnp.sum(p, axis=-1, keepdims=True)
                og_ref[g, rows(qs, ATT_TQ), :] = jnp.dot(p.astype(BF16), vt, preferred_element_type=F32) / z
                lg_ref[g, rows(qs, ATT_TQ), :] = jnp.broadcast_to(m + jnp.log(z), (ATT_TQ, HD))
    l0, l1, l2 = lg_ref[0], lg_ref[1], lg_ref[2]
    mx = jnp.maximum(jnp.maximum(l0, l1), l2)
    e0, e1, e2 = jnp.exp(l0 - mx), jnp.exp(l1 - mx), jnp.exp(l2 - mx)
    o_ref[0] = (e0 * og_ref[0] + e1 * og_ref[1] + e2 * og_ref[2]) / (e0 + e1 + e2)


def _dilated_attention(z3, q_gain, k_gain, rel_bias):
    b, s, _ = z3.shape
    ng = len(AT_GROUPS)
    col = lambda part, g: pl.BlockSpec((1, s, HD), lambda i, h: (i, 0, part * ng * AT_HEADS + g * AT_HEADS + h))
    biases = [_att_bias(rel_bias, g, s) for g in range(ng)]
    bias_spec = lambda a: pl.BlockSpec((a.shape[0], 1) + a.shape[2:], lambda i, h: (0, h, 0, 0))
    gain_spec = pl.BlockSpec((1, HD), lambda i, h: (0, 0))
    return pl.pallas_call(
        _dilated_kernel,
        out_shape=jax.ShapeDtypeStruct((b, s, AT_HEADS * HD), F32),
        grid=(b, AT_HEADS),
        in_specs=[col(p, g) for p in range(3) for g in range(ng)] + [gain_spec, gain_spec]
                 + [bias_spec(a) for a in biases],
        out_specs=pl.BlockSpec((1, s, HD), lambda i, h: (i, 0, h)),
        scratch_shapes=[pltpu.VMEM((s, HD), F32), pltpu.VMEM((s, HD), F32), pltpu.VMEM((s, HD), F32),
                        pltpu.VMEM((ng, s, HD), F32), pltpu.VMEM((ng, s, HD), F32)],
        compiler_params=_params("parallel", "parallel"),
        name="dilated_attention",
    )(*([z3] * 9), q_gain.reshape(1, HD), k_gain.reshape(1, HD), *biases)


ROUTER_LANES = 128


def _router_kernel(x_ref, g_ref, w_ref, hp_ref, idx_ref, gate_ref, count_ref, carry_ref):
    tm = x_ref.shape[0]
    half = hp_ref.shape[1]

    @pl.when(pl.program_id(0) == 0)
    def _():
        carry_ref[...] = jnp.zeros_like(carry_ref)

    h = _rms_rows(x_ref[...], g_ref[...])
    hp_ref[...] = _bf16_bits(h[:, :half]) | (_bf16_bits(h[:, half:]) << 16)
    h_hi = h.astype(BF16)
    h_lo = (h - h_hi.astype(F32)).astype(BF16)
    logits = (jnp.dot(h_hi, w_ref[0], preferred_element_type=F32)
              + (jnp.dot(h_hi, w_ref[1], preferred_element_type=F32)
                 + jnp.dot(h_lo, w_ref[0], preferred_element_type=F32)))
    lane =lax.broadcasted_iota(jnp.int32, logits.shape, 1).astype(F32)
    logits = jnp.where(lane < N_EXPERTS, logits, -jnp.inf)
    m1 = jnp.max(logits, axis=-1, keepdims=True)
    i1 = jnp.min(jnp.where(logits == m1, lane, float(ROUTER_LANES)), axis=-1, keepdims=True)
    rest = jnp.where(lane == i1, -jnp.inf, logits)
    m2 = jnp.max(rest, axis=-1, keepdims=True)
    i2 = jnp.min(jnp.where(rest == m2, lane, float(ROUTER_LANES)), axis=-1, keepdims=True)
    e2 = jnp.exp(m2 - m1)
    den = 1.0 + e2
    gate_ref[...] = jnp.where(lane == 0, 1.0 / den, e2 / den)
    chosen = jnp.where((lane == i1) | (lane == i2), 1.0, 0.0)
    earlier = lax.broadcasted_iota(jnp.int32, (tm, tm), 0) > lax.broadcasted_iota(jnp.int32, (tm, tm), 1)
    carry = carry_ref[...]
    before = jnp.dot(jnp.where(earlier, 1.0, 0.0).astype(BF16), chosen.astype(BF16),
                     preferred_element_type=F32) + carry
    r1 = jnp.sum(jnp.where(lane == i1, before, 0.0), axis=-1, keepdims=True)
    r2 = jnp.sum(jnp.where(lane == i2, before, 0.0), axis=-1, keepdims=True)
    idx_ref[...] = jnp.where(lane == 0, i1, jnp.where(lane == 1, i2, jnp.where(lane == 2, r1, r2))).astype(jnp.int32)
    carry = carry + jnp.sum(chosen, axis=0, keepdims=True)
    carry_ref[...] = carry
    count_ref[...] = carry.astype(jnp.int32)


def _router(x2, gain, router):
    rows, d = x2.shape
    tm = MOE_CHUNK
    w = jnp.zeros((d, ROUTER_LANES), F32).at[:, :N_EXPERTS].set(router.astype(F32))
    w_hi = w.astype(BF16)
    w = jnp.stack([w_hi, (w - w_hi.astype(F32)).astype(BF16)])
    row_spec = lambda width: pl.BlockSpec((tm, width), lambda i: (i, 0))
    return pl.pallas_call(
        _router_kernel,
        out_shape=(jax.ShapeDtypeStruct((rows, d // 2), jnp.uint32),
                   jax.ShapeDtypeStruct((rows, ROUTER_LANES), jnp.int32),
                   jax.ShapeDtypeStruct((rows, ROUTER_LANES), F32),
                   jax.ShapeDtypeStruct((1, ROUTER_LANES), jnp.int32)),
        grid=(rows // tm,),
        in_specs=[row_spec(d),
                  pl.BlockSpec((1, d), lambda i: (0, 0)),
                  pl.BlockSpec((2, d, ROUTER_LANES), lambda i: (0, 0, 0))],
        out_specs=(row_spec(d // 2), row_spec(ROUTER_LANES), row_spec(ROUTER_LANES),
                   pl.BlockSpec((1, ROUTER_LANES), lambda i: (0, 0))),
        scratch_shapes=[pltpu.VMEM((1, ROUTER_LANES), F32)],
        compiler_params=_params("arbitrary"),
        name="router",
    )(x2, gain.reshape(1, d), w)


def _sc_workers():
    info = plsc.get_sparse_core_info()
    mesh = plsc.VectorSubcoreMesh(core_axis_name="core", subcore_axis_name="subcore")
    return mesh, info.num_cores, info.num_subcores


def _sc_token_rows(t, nc, ns):
    per_w = t // (nc * ns)
    assert per_w * nc * ns == t and per_w % (2 * SC_ROWS) == 0
    return per_w, per_w // SC_ROWS


def _scatter_rows(table, dests, p_rows):
    t, w = table.shape
    nk = len(dests)
    mesh, nc, ns = _sc_workers()
    per_w, nit = _sc_token_rows(t, nc, ns)
    ch = SC_ROWS

    @functools.partial(
        pl.kernel, mesh=mesh, out_type=jax.ShapeDtypeStruct((p_rows, w), table.dtype),
        scratch_types=[pltpu.VMEM((nk, nit, ch), jnp.int32), pltpu.VMEM((2, ch, w), table.dtype),
                       pltpu.SemaphoreType.DMA((2,)), pltpu.SemaphoreType.DMA((2, nk))])
    def scatter(table_hbm, *refs):
        dest_hbm, out_hbm = refs[:nk], refs[nk]
        idx_v, rows_v, rsem, wsem = refs[nk + 1:]
        wid = lax.axis_index("subcore") * nc + lax.axis_index("core")
        for k in range(nk):
            pltpu.sync_copy(dest_hbm[k].at[pl.ds(wid * nit, nit)], idx_v.at[k])

        @pl.loop(0, nit // 2)
        def _(it):
            reads = [pltpu.make_async_copy(table_hbm.at[pl.ds(wid * per_w + (2 * it + b) * ch, ch)],
                                           rows_v.at[b], rsem.at[b]) for b in range(2)]
            writes = [[pltpu.make_async_copy(rows_v.at[b], out_hbm.at[idx_v.at[k].at[2 * it + b]], wsem.at[b, k])
                       for k in range(nk)] for b in range(2)]
            reads[0].start()
            reads[1].start()
            for b in range(2):
                reads[b].wait()
                for k in range(nk):
                    writes[b][k].start()
            for b in range(2):
                for k in range(nk):
                    writes[b][k].wait()

    return scatter(table, *[d.reshape(t // ch, ch) for d in dests])


def _gather_rows(table, idxs):
    t = idxs[0].shape[0]
    w = table.shape[1]
    nk = len(idxs)
    mesh, nc, ns = _sc_workers()
    per_w, nit = _sc_token_rows(t, nc, ns)
    ch = SC_ROWS

    @functools.partial(
        pl.kernel, mesh=mesh, out_type=[jax.ShapeDtypeStruct((t, w), table.dtype)] * nk,
        scratch_types=[pltpu.VMEM((nk, nit, ch), jnp.int32), pltpu.VMEM((nk, 2, ch, w), table.dtype),
                       pltpu.SemaphoreType.DMA((nk, 2)), pltpu.SemaphoreType.DMA((nk, 2))])
    def gather(table_hbm, *refs):
        idx_hbm, out_hbm = refs[:nk], refs[nk:2 * nk]
        idx_v, rows_v, rsem, wsem = refs[2 * nk:]
        wid = lax.axis_index("subcore") * nc + lax.axis_index("core")
        for k in range(nk):
            pltpu.sync_copy(idx_hbm[k].at[pl.ds(wid * nit, nit)], idx_v.at[k])

        @pl.loop(0, nit // 2)
        def _(it):
            slots = [(k, b) for k in range(nk) for b in range(2)]
            reads = {(k, b): pltpu.make_async_copy(table_hbm.at[idx_v.at[k].at[2 * it + b]], rows_v.at[k, b],
                                                   rsem.at[k, b]) for k, b in slots}
            writes = {(k, b): pltpu.make_async_copy(rows_v.at[k, b],
                                                    out_hbm[k].at[pl.ds(wid * per_w + (2 * it + b) * ch, ch)],
                                                    wsem.at[k, b]) for k, b in slots}
            for s in slots:
                reads[s].start()
            for s in slots:
                reads[s].wait()
                writes[s].start()
            for s in slots:
                writes[s].wait()

    return gather(table, *[i.reshape(t // ch, ch) for i in idxs])


def _combine_kernel(x_ref, y0_ref, y1_ref, gate_ref, o_ref):
    half = x_ref.shape[1] // 2
    for part in range(2):
        cols = slice(part * half, (part + 1) * half)
        acc = x_ref[:, cols]
        for k, y_ref in enumerate((y0_ref, y1_ref)):
            word = y_ref[...]
            bits = (word << 16) if part == 0 else (word & jnp.uint32(0xFFFF0000))
            acc = acc + gate_ref[:, k:k + 1] * lax.bitcast_convert_type(bits, F32)
        o_ref[:, cols] = acc


def _combine(x2, y0, y1, gates):
    t, d = x2.shape
    tm = ROW_TILE
    return pl.pallas_call(
        _combine_kernel,
        out_shape=jax.ShapeDtypeStruct((t, d), F32),
        grid=(t // tm,),
        in_specs=[pl.BlockSpec((tm, d), lambda i: (i, 0)),
                  pl.BlockSpec((tm, d // 2), lambda i: (i, 0)),
                  pl.BlockSpec((tm, d // 2), lambda i: (i, 0)),
                  pl.BlockSpec((tm, TOP_K), lambda i: (i, 0))],
        out_specs=pl.BlockSpec((tm, d), lambda i: (i, 0)),
        compiler_params=_params("parallel"),
        name="moe_combine",
    )(x2, y0, y1, gates)


def _moe(x2, gain, router, wg, wu, wd):
    t, d = x2.shape
    hp, idx, gate, count = _router(x2, gain, router)
    experts = jnp.arange(N_EXPERTS, dtype=jnp.int32)
    counts = count[0, :N_EXPERTS]
    padded = (counts + MOE_ROWS - 1) // MOE_ROWS * MOE_ROWS
    pend = jnp.cumsum(padded)
    pstart = pend - padded
    expert, rank = idx[:, :TOP_K], idx[:, TOP_K:2 * TOP_K]
    dest = jnp.sum(jnp.where(expert[:, :, None] == experts, pstart, 0), axis=-1) + rank
    dests = [dest[:, k] for k in range(TOP_K)]
    p_rows = t * TOP_K + N_EXPERTS * MOE_ROWS
    blk_row = jnp.arange(p_rows // MOE_ROWS, dtype=jnp.int32) * MOE_ROWS
    blk_expert = jnp.minimum(jnp.sum(pend[None, :] <= blk_row[:, None], axis=1), N_EXPERTS - 1).astype(jnp.int32)
    valid = jnp.clip(counts[blk_expert] - (blk_row - pstart[blk_expert]), 0, MOE_ROWS)
    valid = jnp.where(blk_row < pend[-1], valid, 0).astype(jnp.int32)
    hb = _scatter_rows(hp, dests, p_rows)
    yb = _ffn_experts(hb, wg, wu, wd, blk_expert, valid)
    y0, y1 = _gather_rows(yb, dests)
    return _combine(x2, y0, y1, gate[:, :TOP_K])


def kernel(x, mem, rel_bias, norm_mix, norm_mem, norm_ffn, w_mem_kv, xq_norm, xk_norm, w_out, hy_w_in, hy_conv_w, hy_conv_b, hy_filt_w1, hy_filt_b1, hy_filt_w2, hy_filt_b2, hy_filt_w3, hy_sin_freq, hy_skip, at_w_in, at_q_norm, at_k_norm, ffn_w_gate, ffn_w_up, ffn_w_down, moe_router, moe_w_gate, moe_w_up, moe_w_down):
    b, s, d = x.shape
    t = b * s
    m_len = mem.shape[1]
    x2 = x.reshape(t, d)
    mem2 = mem.reshape(b * m_len, d)
    bf = lambda w: w.astype(BF16)

    cmat, smat = _dft_tables(s)
    spectra, hn = _hyena_filters(s, cmat, smat, hy_filt_w1[0], hy_filt_b1[0], hy_filt_w2[0], hy_filt_b2[0],
                                 hy_filt_w3[0], hy_sin_freq[0])
    z = _norm_matmul(x2, norm_mix[0], bf(hy_w_in[0]), 1024).reshape(b, s, -1)
    kv = _norm_matmul(mem2, norm_mem[0], bf(w_mem_kv[0]), 1024).reshape(b, m_len, -1)
    self_out = _hyena(z, hy_conv_w[0], hy_conv_b[0], cmat, smat, spectra, hn, hy_skip[0])
    cross = _xattn(z, 3 * HY_C // XA_W, kv, xq_norm[0], xk_norm[0])
    x2 = _outproj(x2, self_out.reshape(t, -1), cross.reshape(t, -1), bf(w_out[0]))
    x2 = _ffn_dense(x2, norm_ffn[0], bf(ffn_w_gate[0]), bf(ffn_w_up[0]), bf(ffn_w_down[0]))

    z = _norm_matmul(x2, norm_mix[1], bf(at_w_in[0]), 1024).reshape(b, s, -1)
    kv = _norm_matmul(mem2, norm_mem[1], bf(w_mem_kv[1]), 1024).reshape(b, m_len, -1)
    self_out = _dilated_attention(z, at_q_norm[0], at_k_norm[0], rel_bias)
    cross = _xattn(z, 3 * AT_W // XA_W, kv, xq_norm[1], xk_norm[1])
    x2 = _outproj(x2, self_out.reshape(t, -1), cross.reshape(t, -1), bf(w_out[1]))
    x2 = _moe(x2, norm_ffn[1], moe_router[0], bf(moe_w_gate[0]), bf(moe_w_up[0]), bf(moe_w_down[0]))
    return x2.reshape(b, s, d)
```

```python
import dataclasses
import functools
import math

import jax
import jax.numpy as jnp
import numpy as np
from jax import lax
from jax.experimental import pallas as pl
from jax.experimental.pallas import tpu as pltpu
from jax.experimental.pallas import tpu_sc as plsc

F32 = jnp.float32
BF16 = jnp.bfloat16

D_MODEL = 1024
EPS = 1e-6
HY_C = 512
FILT_BANDS = 16
DECAY_TARGET = 1e-2
FAST_DECAY_PCT = 0.3
SLOW_DECAY_PCT = 1.5
MOD_SHIFT = 0.05
AT_GROUPS = ((128, 1), (512, 4), (2048, 16))
AT_HEADS = 4
HD = 128
AT_W = 1536
NUM_BUCKETS = 32
REL_MAX_DIST = 1024
NEG_INF = -1e30
XA_W = 512
D_FF = 2816
N_EXPERTS = 8
TOP_K = 2

VMEM_LIMIT_BYTES = 56 * 1024 * 1024
ROW_TILE = 512
FF_TILE = 1408
MOE_ROWS = 512
MOE_CHUNK = 512
SC_ROWS = 32
SC_LANES = 16
SC_PACK_ROWS = 16
ATT_TQ = 128
ATT_KW = 256
HY_TC = 256
HY_KT = 512


def _params(*sem):
    return pltpu.CompilerParams(dimension_semantics=sem, vmem_limit_bytes=VMEM_LIMIT_BYTES)


def _rms_rows(x, gain):
    return x * lax.rsqrt(jnp.mean(x * x, axis=-1, keepdims=True) + EPS) * gain


def _bf16_bits(v):
    u = lax.bitcast_convert_type(v, jnp.uint32)
    return (u + jnp.uint32(0x7FFF) + ((u >> 16) & jnp.uint32(1))) >> 16


def _resident(shape, index_map):
    return pl.BlockSpec(shape, index_map, pipeline_mode=pl.Buffered(1))


def _norm_matmul_kernel(x_ref, g_ref, w_ref, o_ref, *, tn):
    h = _rms_rows(x_ref[...], g_ref[...]).astype(BF16)
    for c in range(o_ref.shape[1] // tn):
        cols = slice(c * tn, (c + 1) * tn)
        o_ref[:, cols] = jnp.dot(h, w_ref[:, cols], preferred_element_type=F32).astype(o_ref.dtype)


def _norm_matmul(x2, gain, w_bf, tn):
    rows, d = x2.shape
    n = w_bf.shape[1]
    tm = min(ROW_TILE, rows)
    return pl.pallas_call(
        functools.partial(_norm_matmul_kernel, tn=tn),
        out_shape=jax.ShapeDtypeStruct((rows, n), BF16),
        grid=(rows // tm,),
        in_specs=[pl.BlockSpec((tm, d), lambda i: (i, 0)),
                  pl.BlockSpec((1, d), lambda i: (0, 0)),
                  _resident((d, n), lambda i: (0, 0))],
        out_specs=pl.BlockSpec((tm, n), lambda i: (i, 0)),
        compiler_params=_params("parallel"),
        name="norm_matmul",
    )(x2, gain.reshape(1, d), w_bf)


def _outproj_kernel(x_ref, a_ref, c_ref, wa_ref, wc_ref, o_ref):
    o_ref[...] = (x_ref[...]
                  + jnp.dot(a_ref[...].astype(BF16), wa_ref[...], preferred_element_type=F32)
                  + jnp.dot(c_ref[...].astype(BF16), wc_ref[...], preferred_element_type=F32))


def _outproj(x2, self_out, cross, w_bf):
    rows, d = x2.shape
    half = self_out.shape[1]
    tm = ROW_TILE
    return pl.pallas_call(
        _outproj_kernel,
        out_shape=jax.ShapeDtypeStruct((rows, d), F32),
        grid=(rows // tm,),
        in_specs=[pl.BlockSpec((tm, d), lambda i: (i, 0)),
                  pl.BlockSpec((tm, half), lambda i: (i, 0)),
                  pl.BlockSpec((tm, half), lambda i: (i, 0)),
                  pl.BlockSpec((half, d), lambda i: (0, 0)),
                  pl.BlockSpec((half, d), lambda i: (1, 0))],
        out_specs=pl.BlockSpec((tm, d), lambda i: (i, 0)),
        compiler_params=_params("parallel"),
        name="outproj",
    )(x2, self_out, cross, w_bf, w_bf)


def _weight_rows(ref, start, size):
    if ref.dtype == jnp.uint32:
        return lambda cols: pltpu.bitcast(ref[start // 2:(start + size) // 2, cols], BF16)
    return lambda cols: ref[start:start + size, cols]


def _weight_shape(ref):
    return (ref.shape[0] * (2 if ref.dtype == jnp.uint32 else 1), ref.shape[1])


def _swiglu(h, wg_ref, wu_ref, wd_ref, tf):
    d, f = _weight_shape(wg_ref)
    y = None
    for j in range(f // tf):
        cols = slice(j * tf, (j + 1) * tf)
        gg = jnp.dot(h, _weight_rows(wg_ref, 0, d)(cols), preferred_element_type=F32)
        uu = jnp.dot(h, _weight_rows(wu_ref, 0, d)(cols), preferred_element_type=F32)
        a = ((gg * jax.nn.sigmoid(gg)) * uu).astype(BF16)
        part = jnp.dot(a, _weight_rows(wd_ref, j * tf, tf)(slice(None)), preferred_element_type=F32)
        y = part if y is None else y + part
    return y


def _ffn_dense_kernel(x_ref, g_ref, wg_ref, wu_ref, wd_ref, o_ref, *, tf):
    x = x_ref[...]
    h = _rms_rows(x, g_ref[...]).astype(BF16)
    o_ref[...] = x + _swiglu(h, wg_ref, wu_ref, wd_ref, tf)


def _ffn_dense(x2, gain, wg, wu, wd):
    rows, d = x2.shape
    f = wg.shape[1]
    tm = ROW_TILE
    return pl.pallas_call(
        functools.partial(_ffn_dense_kernel, tf=FF_TILE),
        out_shape=jax.ShapeDtypeStruct((rows, d), F32),
        grid=(rows // tm,),
        in_specs=[pl.BlockSpec((tm, d), lambda i: (i, 0)),
                  pl.BlockSpec((1, d), lambda i: (0, 0)),
                  _resident((d, f), lambda i: (0, 0)),
                  _resident((d, f), lambda i: (0, 0)),
                  _resident((f, d), lambda i: (0, 0))],
        out_specs=pl.BlockSpec((tm, d), lambda i: (i, 0)),
        compiler_params=_params("parallel"),
        name="ffn_dense",
    )(x2, gain.reshape(1, d), wg, wu, wd)


def _ffn_expert_kernel(eid_ref, valid_ref, hp_ref, wg_ref, wu_ref, wd_ref, o_ref, h_ref, *, tf):
    half = hp_ref.shape[1]
    valid = valid_ref[pl.program_id(0)]

    @pl.when(valid > 0)
    def _():
        keep = lax.broadcasted_iota(jnp.int32, (hp_ref.shape[0], 1), 0) < valid
        word = hp_ref[...]
        h_ref[:, :half] = jnp.where(keep, lax.bitcast_convert_type(word << 16, F32), 0.0).astype(BF16)
        h_ref[:, half:] = jnp.where(keep, lax.bitcast_convert_type(word & jnp.uint32(0xFFFF0000), F32),
                                    0.0).astype(BF16)
        y = _swiglu(h_ref[...], wg_ref.at[0], wu_ref.at[0], wd_ref.at[0], tf)
        o_ref[...] = _bf16_bits(y[:, :half]) | (_bf16_bits(y[:, half:]) << 16)

    @pl.when(valid <= 0)
    def _():
        o_ref[...] = jnp.zeros_like(o_ref)


def _ffn_experts(hp, wg, wu, wd, eid, valid):
    rows, half = hp.shape
    d = 2 * half
    tm = MOE_ROWS
    expert_spec = lambda w: pl.BlockSpec((1,) + w.shape[1:], lambda i, e, n: (e[i], 0, 0))
    grid_spec = pltpu.PrefetchScalarGridSpec(
        num_scalar_prefetch=2,
        grid=(rows // tm,),
        in_specs=[pl.BlockSpec((tm, half), lambda i, e, n: (i, 0)),
                  expert_spec(wg), expert_spec(wu), expert_spec(wd)],
        out_specs=pl.BlockSpec((tm, half), lambda i, e, n: (i, 0)),
        scratch_shapes=[pltpu.VMEM((tm, d), BF16)],
    )
    return pl.pallas_call(
        functools.partial(_ffn_expert_kernel, tf=FF_TILE),
        out_shape=jax.ShapeDtypeStruct((rows, half), jnp.uint32),
        grid_spec=grid_spec,
        compiler_params=_params("arbitrary"),
        name="ffn_experts",
    )(eid, valid, hp, wg, wu, wd)


def _xattn_kernel(xq_ref, kv_ref, qg_ref, kg_ref, o_ref):
    scale = HD ** -0.5
    for h in range(XA_W // HD):
        cols = slice(h * HD, (h + 1) * HD)
        q = _rms_rows(xq_ref[0, :, cols].astype(F32), qg_ref[...]).astype(BF16)
        k = _rms_rows(kv_ref[0, :, cols].astype(F32), kg_ref[...]).astype(BF16)
        v = kv_ref[0, :, XA_W + h * HD:XA_W + (h + 1) * HD]
        s = lax.dot_general(q, k, (((1,), (1,)), ((), ())), preferred_element_type=F32) * scale
        p = jnp.exp(s - jnp.max(s, axis=-1, keepdims=True))
        z = jnp.sum(p, axis=-1, keepdims=True)
        o_ref[0, :, cols] = jnp.dot(p.astype(BF16), v, preferred_element_type=F32) / z


def _xattn(z3, xq_block, kv3, q_gain, k_gain):
    b, s, _ = z3.shape
    m = kv3.shape[1]
    ts = ROW_TILE
    return pl.pallas_call(
        _xattn_kernel,
        out_shape=jax.ShapeDtypeStruct((b, s, XA_W), F32),
        grid=(b, s // ts),
        in_specs=[pl.BlockSpec((1, ts, XA_W), lambda i, j: (i, j, xq_block)),
                  pl.BlockSpec((1, m, 2 * XA_W), lambda i, j: (i, 0, 0)),
                  pl.BlockSpec((1, HD), lambda i, j: (0, 0)),
                  pl.BlockSpec((1, HD), lambda i, j: (0, 0))],
        out_specs=pl.BlockSpec((1, ts, XA_W), lambda i, j: (i, j, 0)),
        compiler_params=_params("parallel", "parallel"),
        name="cross_attention",
    )(z3, kv3, q_gain.reshape(1, HD), k_gain.reshape(1, HD))


def _dft_kernel(c_ref, s_ref, c0_ref, s0_ref, *, n_fft):
    rows, cols = c_ref.shape
    i = pl.program_id(0)

    @pl.when(i == 0)
    def _():
        k = lax.broadcasted_iota(jnp.int32, (rows, cols), 0)
        n = lax.broadcasted_iota(jnp.int32, (rows, cols), 1)
        ang = ((k * n) & (n_fft - 1)).astype(F32) * (2.0 * math.pi / n_fft)
        c0_ref[...] = jnp.cos(ang)
        s0_ref[...] = jnp.sin(ang)

    period = n_fft // rows
    n = lax.broadcasted_iota(jnp.int32, (8, cols), 1)
    shift = ((i * n) & (period - 1)).astype(F32) * (2.0 * math.pi / period)
    ca = jnp.cos(shift)[0:1]
    sa = jnp.sin(shift)[0:1]
    c0 = c0_ref[...]
    s0 = s0_ref[...]
    c_ref[...] = (ca * c0 - sa * s0).astype(BF16)
    s_ref[...] = (sa * c0 + ca * s0).astype(BF16)


def _dft_tables(length):
    rows = 256
    shape = jax.ShapeDtypeStruct((length, length), BF16)
    spec = pl.BlockSpec((rows, length), lambda i: (i, 0))
    return pl.pallas_call(
        functools.partial(_dft_kernel, n_fft=2 * length),
        out_shape=(shape, shape),
        grid=(length // rows,),
        out_specs=(spec, spec),
        scratch_shapes=[pltpu.VMEM((rows, length), F32), pltpu.VMEM((rows, length), F32)],
        compiler_params=_params("arbitrary"),
        name="dft_tables",
    )()


def _filter_time_kernel(feats_ref, w1_ref, b1_ref, w2_ref, b2_ref, fr_ref, w3_ref, t_ref, delta_ref,
                        gp_ref, gm_ref, hn_ref):
    hp = lax.Precision.HIGHEST
    length = feats_ref.shape[0]
    fr = fr_ref[...]
    h = jnp.sin(fr * (jnp.dot(feats_ref[...], w1_ref[...], preferred_element_type=F32, precision=hp) + b1_ref[...]))
    h = jnp.sin(fr * (jnp.dot(h, w2_ref[...], preferred_element_type=F32, precision=hp) + b2_ref[...]))
    mod = jnp.exp(-t_ref[...] * delta_ref[...]) + MOD_SHIFT
    row = lax.broadcasted_iota(jnp.int32, (length, 1), 0)
    alt = jnp.where(row % 2 == 0, 1.0, -1.0).astype(F32)
    for o in range(2):
        fwd = jnp.dot(h, w3_ref[2 * o], preferred_element_type=F32, precision=hp) * mod
        bwd = jnp.dot(h, w3_ref[2 * o + 1], preferred_element_type=F32, precision=hp) * mod
        bwd = jnp.where(row == 0, 0.0, bwd)
        norm = (jnp.sum(jnp.abs(fwd), axis=0, keepdims=True)
                + jnp.sum(jnp.abs(bwd), axis=0, keepdims=True) + 1e-6)
        fwd = fwd / norm
        bwd = bwd / norm
        gp = fwd + bwd
        gp_ref[o] = gp.astype(BF16)
        gm_ref[o] = (fwd - bwd).astype(BF16)
        hn_ref[o] = jnp.sum(gp * alt, axis=0, keepdims=True) * (1.0 / (2 * length))


def _filter_freq_kernel(c_ref, s_ref, gp_ref, gm_ref, h_ref):
    length = c_ref.shape[0]
    row = lax.broadcasted_iota(jnp.int32, (length, 1), 0)
    scale = jnp.where(row == 0, 1.0, 2.0).astype(F32) * (1.0 / (2 * length))
    h_ref[0, 0] = jnp.dot(c_ref[...], gp_ref[0], preferred_element_type=F32) * scale
    h_ref[0, 1] = -jnp.dot(s_ref[...], gm_ref[0], preferred_element_type=F32) * scale


def _hyena_filters(length, cmat, smat, w1, b1, w2, b2, w3, freq):
    t = jnp.linspace(0.0, 1.0, length, dtype=F32)[:, None]
    f = jnp.linspace(1e-4, FILT_BANDS - 1, FILT_BANDS, dtype=F32)[None]
    ang = (2.0 * math.pi / length) * jnp.arange(length, dtype=F32)[:, None] * f
    feats = jnp.concatenate([t, jnp.cos(ang), -jnp.sin(ang)], axis=-1)
    deltas = jnp.abs(jnp.linspace(math.log(DECAY_TARGET) / SLOW_DECAY_PCT,
                                  math.log(DECAY_TARGET) / FAST_DECAY_PCT, HY_C, dtype=F32))[None]
    hid = w1.shape[1]
    w3r = w3.reshape(hid, 4, HY_C).transpose(1, 0, 2)
    tc = HY_TC
    full = lambda shape: pl.BlockSpec(shape, lambda c: (0,) * len(shape))
    gp, gm, hn = pl.pallas_call(
        _filter_time_kernel,
        out_shape=(jax.ShapeDtypeStruct((2, length, HY_C), BF16),
                   jax.ShapeDtypeStruct((2, length, HY_C), BF16),
                   jax.ShapeDtypeStruct((2, 1, HY_C), F32)),
        grid=(HY_C // tc,),
        in_specs=[full(feats.shape), full(w1.shape), full((1, hid)), full(w2.shape), full((1, hid)),
                  full((1, hid)), pl.BlockSpec((4, hid, tc), lambda c: (0, 0, c)), full((length, 1)),
                  pl.BlockSpec((1, tc), lambda c: (0, c))],
        out_specs=(pl.BlockSpec((2, length, tc), lambda c: (0, 0, c)),
                   pl.BlockSpec((2, length, tc), lambda c: (0, 0, c)),
                   pl.BlockSpec((2, 1, tc), lambda c: (0, 0, c))),
        compiler_params=_params("parallel"),
        name="filter_time",
    )(feats, w1, b1.reshape(1, hid), w2, b2.reshape(1, hid), freq.reshape(1, hid), w3r, t, deltas)
    spectra = pl.pallas_call(
        _filter_freq_kernel,
        out_shape=jax.ShapeDtypeStruct((2, 2, length, HY_C), F32),
        grid=(2, HY_C // tc),
        in_specs=[_resident((length, length), lambda o, c: (0, 0)),
                  _resident((length, length), lambda o, c: (0, 0)),
                  pl.BlockSpec((1, length, tc), lambda o, c: (o, 0, c)),
                  pl.BlockSpec((1, length, tc), lambda o, c: (o, 0, c))],
        out_specs=pl.BlockSpec((1, 2, length, tc), lambda o, c: (o, 0, 0, c)),
        compiler_params=_params("parallel", "parallel"),
        name="filter_freq",
    )(cmat, smat, gp, gm)
    return spectra, hn


def _hyena_kernel(z0_ref, z1_ref, z2_ref, cw_ref, cb_ref, c_ref, s_ref, h_ref, hn_ref, skip_ref, o_ref,
                  ubf_ref, y_ref):
    length = z0_ref.shape[1]
    row = lax.broadcasted_iota(jnp.int32, (length, 1), 0)
    alt = jnp.where(row % 2 == 0, 1.0, -1.0).astype(F32)

    def short_conv(z_ref, c):
        u = z_ref[0].astype(F32)
        prev = jnp.where(row == 0, 0.0, pltpu.roll(u, 1, axis=0))
        nxt = jnp.where(row == length - 1, 0.0, pltpu.roll(u, length - 1, axis=0))
        return cb_ref[c] + (prev * cw_ref[0, c] + u * cw_ref[1, c] + nxt * cw_ref[2, c])

    def long_conv(u, o):
        ubf_ref[...] = u.astype(BF16)
        nyq = jnp.sum(u * alt, axis=0, keepdims=True) * hn_ref[o]
        y_ref[...] = alt * nyq + u * skip_ref[o]
        for kt in range(length // HY_KT):
            ks = slice(kt * HY_KT, (kt + 1) * HY_KT)
            ub = ubf_ref[...]
            a = jnp.dot(c_ref[ks, :], ub, preferred_element_type=F32)
            b = jnp.dot(s_ref[ks, :], ub, preferred_element_type=F32)
            hr = h_ref[o, 0, ks, :]
            hi = h_ref[o, 1, ks, :]
            re = (a * hr + b * hi).astype(BF16)
            im = (a * hi - b * hr).astype(BF16)
            y_ref[...] += (jnp.dot(c_ref[:, ks], re, preferred_element_type=F32)
                           - jnp.dot(s_ref[:, ks], im, preferred_element_type=F32))
        return y_ref[...]

    z = short_conv(z1_ref, 1) * long_conv(short_conv(z0_ref, 0), 0)
    o_ref[0] = short_conv(z2_ref, 2) * long_conv(z, 1)


def _hyena(z3, conv_w, conv_b, cmat, smat, spectra, hn, skip):
    b, s, _ = z3.shape
    tc = HY_TC
    nct = HY_C // tc
    cw = conv_w.reshape(3, 3, 1, HY_C)
    cb = conv_b.reshape(3, 1, HY_C)
    zspec = lambda chunk: pl.BlockSpec((1, s, tc), lambda c, i: (i, 0, chunk * nct + c))
    return pl.pallas_call(
        _hyena_kernel,
        out_shape=jax.ShapeDtypeStruct((b, s, HY_C), F32),
        grid=(nct, b),
        in_specs=[zspec(0), zspec(1), zspec(2),
                  pl.BlockSpec((3, 3, 1, tc), lambda c, i: (0, 0, 0, c)),
                  pl.BlockSpec((3, 1, tc), lambda c, i: (0, 0, c)),
                  _resident((s, s), lambda c, i: (0, 0)),
                  _resident((s, s), lambda c, i: (0, 0)),
                  _resident((2, 2, s, tc), lambda c, i: (0, 0, 0, c)),
                  pl.BlockSpec((2, 1, tc), lambda c, i: (0, 0, c)),
                  pl.BlockSpec((2, 1, tc), lambda c, i: (0, 0, c))],
        out_specs=pl.BlockSpec((1, s, tc), lambda c, i: (i, 0, c)),
        scratch_shapes=[pltpu.VMEM((s, tc), BF16), pltpu.VMEM((s, tc), F32)],
        compiler_params=_params("parallel", "parallel"),
        name="hyena",
    )(z3, z3, z3, cw, cb, cmat, smat, spectra, hn, skip.reshape(2, 1, HY_C))


def _rel_bucket(rel):
    half = NUM_BUCKETS // 2
    exact = half // 2
    n = np.abs(rel)
    large = exact + (np.log(np.maximum(n, 1) / exact) / np.log(REL_MAX_DIST / exact) * (half - exact)).astype(np.int32)
    large = np.minimum(large, half - 1)
    return (np.where(rel > 0, half, 0) + np.where(n < exact, n, large)).astype(np.int32)


def _att_tiles(length):
    kw = min(ATT_KW, length)
    tiles = []
    for qs in range(0, length, ATT_TQ):
        ks = min(max(qs - (kw - ATT_TQ) // 2, 0), length - kw)
        tiles.append((qs, ks, {0: 0, -64: 1, -128: 2}[ks - qs]))
    return kw, tiles


def _att_bias(rel_bias, group, seq):
    window, dil = AT_GROUPS[group]
    band = window // (2 * dil)
    kw, tiles = _att_tiles(seq // dil)
    offsets = sorted({ks - qs for qs, ks, _ in tiles}, reverse=True)
    table = rel_bias[:, group * AT_HEADS:(group + 1) * AT_HEADS].astype(F32)
    span = kw + ATT_TQ
    out = []
    for off in offsets:
        k = np.arange(span)
        rel = off + np.where(k < kw, k, k - span)
        diag = jnp.where(np.abs(rel) <= band, table[_rel_bucket(rel * dil)].T, NEG_INF)
        flat = jnp.tile(diag, (1, ATT_TQ))[:, :ATT_TQ * (span - 1)]
        out.append(flat.reshape(AT_HEADS, ATT_TQ, span - 1)[:, :, :kw])
    return jnp.stack(out)


def _dilated_kernel(q1, q2, q3, k1, k2, k3, v1, v2, v3, qg_ref, kg_ref, b1, b2, b3, o_ref,
                    qn_ref, kn_ref, vn_ref, og_ref, lg_ref):
    seq = o_ref.shape[1]
    scale = HD ** -0.5
    contract_last = (((1,), (1,)), ((), ()))
    for g, (q_ref, k_ref, v_ref, bias_ref) in enumerate(((q1, k1, v1, b1), (q2, k2, v2, b2), (q3, k3, v3, b3))):
        dil = AT_GROUPS[g][1]
        kw, tiles = _att_tiles(seq // dil)
        qn_ref[...] = _rms_rows(q_ref[0].astype(F32), qg_ref[...])
        kn_ref[...] = _rms_rows(k_ref[0].astype(F32), kg_ref[...])
        vn_ref[...] = v_ref[0].astype(F32)
        for r in range(dil):
            for qs, ks, var in tiles:
                rows = lambda start, size: (pl.ds(r + start * dil, size, stride=dil) if dil > 1
                                            else pl.ds(start, size))
                qt = qn_ref[rows(qs, ATT_TQ), :].astype(BF16)
                kt = kn_ref[rows(ks, kw), :].astype(BF16)
                vt = vn_ref[rows(ks, kw), :].astype(BF16)
                s = lax.dot_general(qt, kt, contract_last, preferred_element_type=F32) * scale + bias_ref[var, 0]
                m = jnp.max(s, axis=-1, keepdims=True)
                p = jnp.exp(s - m)
                z = jnp.sum(p, axis=-1, keepdims=True)
                og_ref[g, rows(qs, ATT_TQ), :] = jnp.dot(p.astype(BF16), vt, preferred_element_type=F32) / z
                lg_ref[g, rows(qs, ATT_TQ), :] = jnp.broadcast_to(m + jnp.log(z), (ATT_TQ, HD))
    l0, l1, l2 = lg_ref[0], lg_ref[1], lg_ref[2]
    mx = jnp.maximum(jnp.maximum(l0, l1), l2)
    e0, e1, e2 = jnp.exp(l0 - mx), jnp.exp(l1 - mx), jnp.exp(l2 - mx)
    o_ref[0] = (e0 * og_ref[0] + e1 * og_ref[1] + e2 * og_ref[2]) / (e0 + e1 + e2)


def _dilated_attention(z3, q_gain, k_gain, rel_bias):
    b, s, _ = z3.shape
    ng = len(AT_GROUPS)
    col = lambda part, g: pl.BlockSpec((1, s, HD), lambda i, h: (i, 0, part * ng * AT_HEADS + g * AT_HEADS + h))
    biases = [_att_bias(rel_bias, g, s) for g in range(ng)]
    bias_spec = lambda a: pl.BlockSpec((a.shape[0], 1) + a.shape[2:], lambda i, h: (0, h, 0, 0))
    gain_spec = pl.BlockSpec((1, HD), lambda i, h: (0, 0))
    return pl.pallas_call(
        _dilated_kernel,
        out_shape=jax.ShapeDtypeStruct((b, s, AT_HEADS * HD), F32),
        grid=(b, AT_HEADS),
        in_specs=[col(p, g) for p in range(3) for g in range(ng)] + [gain_spec, gain_spec]
                 + [bias_spec(a) for a in biases],
        out_specs=pl.BlockSpec((1, s, HD), lambda i, h: (i, 0, h)),
        scratch_shapes=[pltpu.VMEM((s, HD), F32), pltpu.VMEM((s, HD), F32), pltpu.VMEM((s, HD), F32),
                        pltpu.VMEM((ng, s, HD), F32), pltpu.VMEM((ng, s, HD), F32)],
        compiler_params=_params("parallel", "parallel"),
        name="dilated_attention",
    )(*([z3] * 9), q_gain.reshape(1, HD), k_gain.reshape(1, HD), *biases)


ROUTER_LANES = 128


def _router_kernel(x_ref, g_ref, w_ref, hp_ref, idx_ref, gate_ref, count_ref, carry_ref):
    tm = x_ref.shape[0]
    half = hp_ref.shape[1]

    @pl.when(pl.program_id(0) == 0)
    def _():
        carry_ref[...] = jnp.zeros_like(carry_ref)

    h = _rms_rows(x_ref[...], g_ref[...])
    hp_ref[...] = _bf16_bits(h[:, :half]) | (_bf16_bits(h[:, half:]) << 16)
    h_hi = h.astype(BF16)
    h_lo = (h - h_hi.astype(F32)).astype(BF16)
    logits = (jnp.dot(h_hi, w_ref[0], preferred_element_type=F32)
              + (jnp.dot(h_hi, w_ref[1], preferred_element_type=F32)
                 + jnp.dot(h_lo, w_ref[0], preferred_element_type=F32)))
    lane =lax.broadcasted_iota(jnp.int32, logits.shape, 1).astype(F32)
    logits = jnp.where(lane < N_EXPERTS, logits, -jnp.inf)
    m1 = jnp.max(logits, axis=-1, keepdims=True)
    i1 = jnp.min(jnp.where(logits == m1, lane, float(ROUTER_LANES)), axis=-1, keepdims=True)
    rest = jnp.where(lane == i1, -jnp.inf, logits)
    m2 = jnp.max(rest, axis=-1, keepdims=True)
    i2 = jnp.min(jnp.where(rest == m2, lane, float(ROUTER_LANES)), axis=-1, keepdims=True)
    e2 = jnp.exp(m2 - m1)
    den = 1.0 + e2
    gate_ref[...] = jnp.where(lane == 0, 1.0 / den, e2 / den)
    chosen = jnp.where((lane == i1) | (lane == i2), 1.0, 0.0)
    earlier = lax.broadcasted_iota(jnp.int32, (tm, tm), 0) > lax.broadcasted_iota(jnp.int32, (tm, tm), 1)
    carry = carry_ref[...]
    before = jnp.dot(jnp.where(earlier, 1.0, 0.0).astype(BF16), chosen.astype(BF16),
                     preferred_element_type=F32) + carry
    r1 = jnp.sum(jnp.where(lane == i1, before, 0.0), axis=-1, keepdims=True)
    r2 = jnp.sum(jnp.where(lane == i2, before, 0.0), axis=-1, keepdims=True)
    idx_ref[...] = jnp.where(lane == 0, i1, jnp.where(lane == 1, i2, jnp.where(lane == 2, r1, r2))).astype(jnp.int32)
    carry = carry + jnp.sum(chosen, axis=0, keepdims=True)
    carry_ref[...] = carry
    count_ref[...] = carry.astype(jnp.int32)


def _router(x2, gain, router):
    rows, d = x2.shape
    tm = MOE_CHUNK
    w = jnp.zeros((d, ROUTER_LANES), F32).at[:, :N_EXPERTS].set(router.astype(F32))
    w_hi = w.astype(BF16)
    w = jnp.stack([w_hi, (w - w_hi.astype(F32)).astype(BF16)])
    row_spec = lambda width: pl.BlockSpec((tm, width), lambda i: (i, 0))
    return pl.pallas_call(
        _router_kernel,
        out_shape=(jax.ShapeDtypeStruct((rows, d // 2), jnp.uint32),
                   jax.ShapeDtypeStruct((rows, ROUTER_LANES), jnp.int32),
                   jax.ShapeDtypeStruct((rows, ROUTER_LANES), F32),
                   jax.ShapeDtypeStruct((1, ROUTER_LANES), jnp.int32)),
        grid=(rows // tm,),
        in_specs=[row_spec(d),
                  pl.BlockSpec((1, d), lambda i: (0, 0)),
                  pl.BlockSpec((2, d, ROUTER_LANES), lambda i: (0, 0, 0))],
        out_specs=(row_spec(d // 2), row_spec(ROUTER_LANES), row_spec(ROUTER_LANES),
                   pl.BlockSpec((1, ROUTER_LANES), lambda i: (0, 0))),
        scratch_shapes=[pltpu.VMEM((1, ROUTER_LANES), F32)],
        compiler_params=_params("arbitrary"),
        name="router",
    )(x2, gain.reshape(1, d), w)


def _sc_workers():
    info = plsc.get_sparse_core_info()
    mesh = plsc.VectorSubcoreMesh(core_axis_name="core", subcore_axis_name="subcore")
    return mesh, info.num_cores, info.num_subcores


def _pack_weight_rows(w, col_tile):
    r, c = w.shape
    mesh, nc, ns = _sc_workers()
    rs, ncol = SC_PACK_ROWS, c // col_tile
    per_w = (r // rs) * ncol // (nc * ns)
    assert per_w * nc * ns * rs * col_tile == r * c and per_w % 2 == 0 and col_tile % SC_LANES == 0
    params = pltpu.CompilerParams()
    if "needs_layout_passes" in pltpu.CompilerParams.__dataclass_fields__:
        params = dataclasses.replace(params, needs_layout_passes=False)

    @functools.partial(
        pl.kernel, mesh=mesh, out_type=jax.ShapeDtypeStruct((r // 2, c), jnp.uint32), compiler_params=params,
        scratch_types=[pltpu.VMEM((2, rs, col_tile), F32), pltpu.VMEM((2, rs // 2, col_tile), jnp.uint32),
                       pltpu.SemaphoreType.DMA((2,)), pltpu.SemaphoreType.DMA((2,))])
    def pack(w_hbm, o_hbm, in_v, out_v, rsem, wsem):
        wid = lax.axis_index("subcore") * nc + lax.axis_index("core")

        @pl.loop(0, per_w // 2)
        def _(it):
            reads, writes = [], []
            for b in range(2):
                tile = wid * per_w + 2 * it + b
                r0 = pl.multiple_of((tile // ncol) * rs, rs)
                c0 = pl.multiple_of((tile % ncol) * col_tile, col_tile)
                reads.append(pltpu.make_async_copy(w_hbm.at[pl.ds(r0, rs), pl.ds(c0, col_tile)], in_v.at[b],
                                                   rsem.at[b]))
                writes.append(pltpu.make_async_copy(
                    out_v.at[b], o_hbm.at[pl.ds(pl.multiple_of(r0 // 2, rs // 2), rs // 2), pl.ds(c0, col_tile)],
                    wsem.at[b]))
            reads[0].start()
            reads[1].start()
            for b in range(2):
                reads[b].wait()
                for pair in range(rs // 2):
                    @plsc.parallel_loop(0, col_tile, step=SC_LANES, unroll=8)
                    def _(j):
                        packed = plsc.pack(in_v[b, 2 * pair, pl.ds(j, SC_LANES)],
                                           in_v[b, 2 * pair + 1, pl.ds(j, SC_LANES)],
                                           format=plsc.PackFormat.INTERLEAVED)
                        out_v[b, pair, pl.ds(j, SC_LANES)] = plsc.bitcast(packed, jnp.uint32)
                writes[b].start()
            writes[0].wait()
            writes[1].wait()

    return pack(w)


def _sc_token_rows(t, nc, ns):
    per_w = t // (nc * ns)
    assert per_w * nc * ns == t and per_w % (2 * SC_ROWS) == 0
    return per_w, per_w // SC_ROWS


def _scatter_rows(table, dests, p_rows):
    t, w = table.shape
    nk = len(dests)
    mesh, nc, ns = _sc_workers()
    per_w, nit = _sc_token_rows(t, nc, ns)
    ch = SC_ROWS

    @functools.partial(
        pl.kernel, mesh=mesh, out_type=jax.ShapeDtypeStruct((p_rows, w), table.dtype),
        scratch_types=[pltpu.VMEM((nk, nit, ch), jnp.int32), pltpu.VMEM((2, ch, w), table.dtype),
                       pltpu.SemaphoreType.DMA((2,)), pltpu.SemaphoreType.DMA((2, nk))])
    def scatter(table_hbm, *refs):
        dest_hbm, out_hbm = refs[:nk], refs[nk]
        idx_v, rows_v, rsem, wsem = refs[nk + 1:]
        wid = lax.axis_index("subcore") * nc + lax.axis_index("core")
        for k in range(nk):
            pltpu.sync_copy(dest_hbm[k].at[pl.ds(wid * nit, nit)], idx_v.at[k])

        @pl.loop(0, nit // 2)
        def _(it):
            reads = [pltpu.make_async_copy(table_hbm.at[pl.ds(wid * per_w + (2 * it + b) * ch, ch)],
                                           rows_v.at[b], rsem.at[b]) for b in range(2)]
            writes = [[pltpu.make_async_copy(rows_v.at[b], out_hbm.at[idx_v.at[k].at[2 * it + b]], wsem.at[b, k])
                       for k in range(nk)] for b in range(2)]
            reads[0].start()
            reads[1].start()
            for b in range(2):
                reads[b].wait()
                for k in range(nk):
                    writes[b][k].start()
            for b in range(2):
                for k in range(nk):
                    writes[b][k].wait()

    return scatter(table, *[d.reshape(t // ch, ch) for d in dests])


def _gather_rows(table, idxs):
    t = idxs[0].shape[0]
    w = table.shape[1]
    nk = len(idxs)
    mesh, nc, ns = _sc_workers()
    per_w, nit = _sc_token_rows(t, nc, ns)
    ch = SC_ROWS

    @functools.partial(
        pl.kernel, mesh=mesh, out_type=[jax.ShapeDtypeStruct((t, w), table.dtype)] * nk,
        scratch_types=[pltpu.VMEM((nk, nit, ch), jnp.int32), pltpu.VMEM((nk, 2, ch, w), table.dtype),
                       pltpu.SemaphoreType.DMA((nk, 2)), pltpu.SemaphoreType.DMA((nk, 2))])
    def gather(table_hbm, *refs):
        idx_hbm, out_hbm = refs[:nk], refs[nk:2 * nk]
        idx_v, rows_v, rsem, wsem = refs[2 * nk:]
        wid = lax.axis_index("subcore") * nc + lax.axis_index("core")
        for k in range(nk):
            pltpu.sync_copy(idx_hbm[k].at[pl.ds(wid * nit, nit)], idx_v.at[k])

        @pl.loop(0, nit // 2)
        def _(it):
            slots = [(k, b) for k in range(nk) for b in range(2)]
            reads = {(k, b): pltpu.make_async_copy(table_hbm.at[idx_v.at[k].at[2 * it + b]], rows_v.at[k, b],
                                                   rsem.at[k, b]) for k, b in slots}
            writes = {(k, b): pltpu.make_async_copy(rows_v.at[k, b],
                                                    out_hbm[k].at[pl.ds(wid * per_w + (2 * it + b) * ch, ch)],
                                                    wsem.at[k, b]) for k, b in slots}
            for s in slots:
                reads[s].start()
            for s in slots:
                reads[s].wait()
                writes[s].start()
            for s in slots:
                writes[s].wait()

    return gather(table, *[i.reshape(t // ch, ch) for i in idxs])


def _combine_kernel(x_ref, y0_ref, y1_ref, gate_ref, o_ref):
    half = x_ref.shape[1] // 2
    for part in range(2):
        cols = slice(part * half, (part + 1) * half)
        acc = x_ref[:, cols]
        for k, y_ref in enumerate((y0_ref, y1_ref)):
            word = y_ref[...]
            bits = (word << 16) if part == 0 else (word & jnp.uint32(0xFFFF0000))
            acc = acc + gate_ref[:, k:k + 1] * lax.bitcast_convert_type(bits, F32)
        o_ref[:, cols] = acc


def _combine(x2, y0, y1, gates):
    t, d = x2.shape
    tm = ROW_TILE
    return pl.pallas_call(
        _combine_kernel,
        out_shape=jax.ShapeDtypeStruct((t, d), F32),
        grid=(t // tm,),
        in_specs=[pl.BlockSpec((tm, d), lambda i: (i, 0)),
                  pl.BlockSpec((tm, d // 2), lambda i: (i, 0)),
                  pl.BlockSpec((tm, d // 2), lambda i: (i, 0)),
                  pl.BlockSpec((tm, TOP_K), lambda i: (i, 0))],
        out_specs=pl.BlockSpec((tm, d), lambda i: (i, 0)),
        compiler_params=_params("parallel"),
        name="moe_combine",
    )(x2, y0, y1, gates)


def _moe(x2, gain, router, wg, wu, wd):
    t, d = x2.shape
    hp, idx, gate, count = _router(x2, gain, router)
    experts = jnp.arange(N_EXPERTS, dtype=jnp.int32)
    counts = count[0, :N_EXPERTS]
    padded = (counts + MOE_ROWS - 1) // MOE_ROWS * MOE_ROWS
    pend = jnp.cumsum(padded)
    pstart = pend - padded
    expert, rank = idx[:, :TOP_K], idx[:, TOP_K:2 * TOP_K]
    dest = jnp.sum(jnp.where(expert[:, :, None] == experts, pstart, 0), axis=-1) + rank
    dests = [dest[:, k] for k in range(TOP_K)]
    p_rows = t * TOP_K + N_EXPERTS * MOE_ROWS
    blk_row = jnp.arange(p_rows // MOE_ROWS, dtype=jnp.int32) * MOE_ROWS
    blk_expert = jnp.minimum(jnp.sum(pend[None, :] <= blk_row[:, None], axis=1), N_EXPERTS - 1).astype(jnp.int32)
    valid = jnp.clip(counts[blk_expert] - (blk_row - pstart[blk_expert]), 0, MOE_ROWS)
    valid = jnp.where(blk_row < pend[-1], valid, 0).astype(jnp.int32)
    n_exp, _, f = wg.shape
    pack = lambda w, col_tile: _pack_weight_rows(w.reshape(-1, w.shape[2]), col_tile).reshape(n_exp, -1, w.shape[2])
    wg, wu, wd = pack(wg, f // 2), pack(wu, f // 2), pack(wd, d)
    hb = _scatter_rows(hp, dests, p_rows)
    yb = _ffn_experts(hb, wg, wu, wd, blk_expert, valid)
    y0, y1 = _gather_rows(yb, dests)
    return _combine(x2, y0, y1, gate[:, :TOP_K])


def kernel(x, mem, rel_bias, norm_mix, norm_mem, norm_ffn, w_mem_kv, xq_norm, xk_norm, w_out, hy_w_in, hy_conv_w, hy_conv_b, hy_filt_w1, hy_filt_b1, hy_filt_w2, hy_filt_b2, hy_filt_w3, hy_sin_freq, hy_skip, at_w_in, at_q_norm, at_k_norm, ffn_w_gate, ffn_w_up, ffn_w_down, moe_router, moe_w_gate, moe_w_up, moe_w_down):
    b, s, d = x.shape
    t = b * s
    m_len = mem.shape[1]
    x2 = x.reshape(t, d)
    mem2 = mem.reshape(b * m_len, d)
    bf = lambda w: w.astype(BF16)

    cmat, smat = _dft_tables(s)
    spectra, hn = _hyena_filters(s, cmat, smat, hy_filt_w1[0], hy_filt_b1[0], hy_filt_w2[0], hy_filt_b2[0],
                                 hy_filt_w3[0], hy_sin_freq[0])
    z = _norm_matmul(x2, norm_mix[0], bf(hy_w_in[0]), 1024).reshape(b, s, -1)
    kv = _norm_matmul(mem2, norm_mem[0], bf(w_mem_kv[0]), 1024).reshape(b, m_len, -1)
    self_out = _hyena(z, hy_conv_w[0], hy_conv_b[0], cmat, smat, spectra, hn, hy_skip[0])
    cross = _xattn(z, 3 * HY_C // XA_W, kv, xq_norm[0], xk_norm[0])
    x2 = _outproj(x2, self_out.reshape(t, -1), cross.reshape(t, -1), bf(w_out[0]))
    x2 = _ffn_dense(x2, norm_ffn[0], bf(ffn_w_gate[0]), bf(ffn_w_up[0]), bf(ffn_w_down[0]))

    z = _norm_matmul(x2, norm_mix[1], bf(at_w_in[0]), 1024).reshape(b, s, -1)
    kv = _norm_matmul(mem2, norm_mem[1], bf(w_mem_kv[1]), 1024).reshape(b, m_len, -1)
    self_out = _dilated_attention(z, at_q_norm[0], at_k_norm[0], rel_bias)
    cross = _xattn(z, 3 * AT_W // XA_W, kv, xq_norm[1], xk_norm[1])
    x2 = _outproj(x2, self_out.reshape(t, -1), cross.reshape(t, -1), bf(w_out[1]))
    x2 = _moe(x2, norm_ffn[1], moe_router[0], moe_w_gate[0], moe_w_up[0], moe_w_down[0])
    return x2.reshape(b, s, d)
```

```python
import dataclasses
import functools
import math

import jax
import jax.numpy as jnp
import numpy as np
from jax import lax
from jax.experimental import pallas as pl
from jax.experimental.pallas import tpu as pltpu
from jax.experimental.pallas import tpu_sc as plsc

F32 = jnp.float32
BF16 = jnp.bfloat16

D_MODEL = 1024
EPS = 1e-6
HY_C = 512
FILT_BANDS = 16
DECAY_TARGET = 1e-2
FAST_DECAY_PCT = 0.3
SLOW_DECAY_PCT = 1.5
MOD_SHIFT = 0.05
AT_GROUPS = ((128, 1), (512, 4), (2048, 16))
AT_HEADS = 4
HD = 128
AT_W = 1536
NUM_BUCKETS = 32
REL_MAX_DIST = 1024
NEG_INF = -1e30
XA_W = 512
D_FF = 2816
N_EXPERTS = 8
TOP_K = 2

VMEM_LIMIT_BYTES = 56 * 1024 * 1024
ROW_TILE = 512
FF_TILE = 1408
MOE_ROWS = 512
MOE_CHUNK = 512
SC_ROWS = 32
SC_LANES = 16
SC_PACK_ROWS = 16
ATT_TQ = 128
ATT_KW = 256
HY_TC = 256
HY_KT = 512


def _params(*sem):
    return pltpu.CompilerParams(dimension_semantics=sem, vmem_limit_bytes=VMEM_LIMIT_BYTES)


def _rms_rows(x, gain):
    return x * lax.rsqrt(jnp.mean(x * x, axis=-1, keepdims=True) + EPS) * gain


def _bf16_bits(v):
    u = lax.bitcast_convert_type(v, jnp.uint32)
    return (u + jnp.uint32(0x7FFF) + ((u >> 16) & jnp.uint32(1))) >> 16


def _resident(shape, index_map):
    return pl.BlockSpec(shape, index_map, pipeline_mode=pl.Buffered(1))


def _norm_matmul_kernel(x_ref, g_ref, w_ref, o_ref, *, tn):
    h = _rms_rows(x_ref[...], g_ref[...]).astype(BF16)
    for c in range(o_ref.shape[1] // tn):
        cols = slice(c * tn, (c + 1) * tn)
        o_ref[:, cols] = jnp.dot(h, w_ref[:, cols], preferred_element_type=F32).astype(o_ref.dtype)


def _norm_matmul(x2, gain, w_bf, tn):
    rows, d = x2.shape
    n = w_bf.shape[1]
    tm = min(ROW_TILE, rows)
    return pl.pallas_call(
        functools.partial(_norm_matmul_kernel, tn=tn),
        out_shape=jax.ShapeDtypeStruct((rows, n), BF16),
        grid=(rows // tm,),
        in_specs=[pl.BlockSpec((tm, d), lambda i: (i, 0)),
                  pl.BlockSpec((1, d), lambda i: (0, 0)),
                  _resident((d, n), lambda i: (0, 0))],
        out_specs=pl.BlockSpec((tm, n), lambda i: (i, 0)),
        compiler_params=_params("parallel"),
        cost_estimate=pl.CostEstimate(flops=2 * rows * d * n, transcendentals=rows,
                                      bytes_accessed=4 * rows * d + 2 * d * n + 2 * rows * n),
        name="norm_matmul",
    )(x2, gain.reshape(1, d), w_bf)


def _outproj_kernel(x_ref, a_ref, c_ref, wa_ref, wc_ref, o_ref):
    o_ref[...] = (x_ref[...]
                  + jnp.dot(a_ref[...].astype(BF16), wa_ref[...], preferred_element_type=F32)
                  + jnp.dot(c_ref[...].astype(BF16), wc_ref[...], preferred_element_type=F32))


def _outproj(x2, self_out, cross, w_bf):
    rows, d = x2.shape
    half = self_out.shape[1]
    tm = ROW_TILE
    return pl.pallas_call(
        _outproj_kernel,
        out_shape=jax.ShapeDtypeStruct((rows, d), F32),
        grid=(rows // tm,),
        in_specs=[pl.BlockSpec((tm, d), lambda i: (i, 0)),
                  pl.BlockSpec((tm, half), lambda i: (i, 0)),
                  pl.BlockSpec((tm, half), lambda i: (i, 0)),
                  pl.BlockSpec((half, d), lambda i: (0, 0)),
                  pl.BlockSpec((half, d), lambda i: (1, 0))],
        out_specs=pl.BlockSpec((tm, d), lambda i: (i, 0)),
        compiler_params=_params("parallel"),
        name="outproj",
    )(x2, self_out, cross, w_bf, w_bf)


def _weight_rows(ref, start, size):
    if ref.dtype == jnp.uint32:
        return lambda cols: pltpu.bitcast(ref[start // 2:(start + size) // 2, cols], BF16)
    return lambda cols: ref[start:start + size, cols]


def _weight_shape(ref):
    return (ref.shape[0] * (2 if ref.dtype == jnp.uint32 else 1), ref.shape[1])


def _swiglu(h, wg_ref, wu_ref, wd_ref, tf):
    d, f = _weight_shape(wg_ref)
    y = None
    for j in range(f // tf):
        cols = slice(j * tf, (j + 1) * tf)
        gg = jnp.dot(h, _weight_rows(wg_ref, 0, d)(cols), preferred_element_type=F32)
        uu = jnp.dot(h, _weight_rows(wu_ref, 0, d)(cols), preferred_element_type=F32)
        a = ((gg * jax.nn.sigmoid(gg)) * uu).astype(BF16)
        part = jnp.dot(a, _weight_rows(wd_ref, j * tf, tf)(slice(None)), preferred_element_type=F32)
        y = part if y is None else y + part
    return y


def _ffn_dense_kernel(x_ref, g_ref, wg_ref, wu_ref, wd_ref, o_ref, *, tf):
    x = x_ref[...]
    h = _rms_rows(x, g_ref[...]).astype(BF16)
    o_ref[...] = x + _swiglu(h, wg_ref, wu_ref, wd_ref, tf)


def _ffn_dense(x2, gain, wg, wu, wd):
    rows, d = x2.shape
    f = wg.shape[1]
    tm = ROW_TILE
    return pl.pallas_call(
        functools.partial(_ffn_dense_kernel, tf=FF_TILE),
        out_shape=jax.ShapeDtypeStruct((rows, d), F32),
        grid=(rows // tm,),
        in_specs=[pl.BlockSpec((tm, d), lambda i: (i, 0)),
                  pl.BlockSpec((1, d), lambda i: (0, 0)),
                  _resident((d, f), lambda i: (0, 0)),
                  _resident((d, f), lambda i: (0, 0)),
                  _resident((f, d), lambda i: (0, 0))],
        out_specs=pl.BlockSpec((tm, d), lambda i: (i, 0)),
        compiler_params=_params("parallel"),
        cost_estimate=pl.CostEstimate(flops=6 * rows * d * f, transcendentals=rows * f,
                                      bytes_accessed=8 * rows * d + 6 * d * f),
        name="ffn_dense",
    )(x2, gain.reshape(1, d), wg, wu, wd)


def _ffn_expert_kernel(eid_ref, valid_ref, hp_ref, wg_ref, wu_ref, wd_ref, o_ref, h_ref, *, tf):
    half = hp_ref.shape[1]
    valid = valid_ref[pl.program_id(0)]

    @pl.when(valid > 0)
    def _():
        keep = lax.broadcasted_iota(jnp.int32, (hp_ref.shape[0], 1), 0) < valid
        word = hp_ref[...]
        h_ref[:, :half] = jnp.where(keep, lax.bitcast_convert_type(word << 16, F32), 0.0).astype(BF16)
        h_ref[:, half:] = jnp.where(keep, lax.bitcast_convert_type(word & jnp.uint32(0xFFFF0000), F32),
                                    0.0).astype(BF16)
        y = _swiglu(h_ref[...], wg_ref.at[0], wu_ref.at[0], wd_ref.at[0], tf)
        o_ref[...] = _bf16_bits(y[:, :half]) | (_bf16_bits(y[:, half:]) << 16)

    @pl.when(valid <= 0)
    def _():
        o_ref[...] = jnp.zeros_like(o_ref)


def _ffn_experts(hp, wg, wu, wd, eid, valid):
    rows, half = hp.shape
    d = 2 * half
    tm = MOE_ROWS
    expert_spec = lambda w: pl.BlockSpec((1,) + w.shape[1:], lambda i, e, n: (e[i], 0, 0))
    grid_spec = pltpu.PrefetchScalarGridSpec(
        num_scalar_prefetch=2,
        grid=(rows // tm,),
        in_specs=[pl.BlockSpec((tm, half), lambda i, e, n: (i, 0)),
                  expert_spec(wg), expert_spec(wu), expert_spec(wd)],
        out_specs=pl.BlockSpec((tm, half), lambda i, e, n: (i, 0)),
        scratch_shapes=[pltpu.VMEM((tm, d), BF16)],
    )
    return pl.pallas_call(
        functools.partial(_ffn_expert_kernel, tf=FF_TILE),
        out_shape=jax.ShapeDtypeStruct((rows, half), jnp.uint32),
        grid_spec=grid_spec,
        compiler_params=_params("arbitrary"),
        name="ffn_experts",
    )(eid, valid, hp, wg, wu, wd)


def _xattn_kernel(xq_ref, kv_ref, qg_ref, kg_ref, o_ref):
    scale = HD ** -0.5
    for h in range(XA_W // HD):
        cols = slice(h * HD, (h + 1) * HD)
        q = _rms_rows(xq_ref[0, :, cols].astype(F32), qg_ref[...]).astype(BF16)
        k = _rms_rows(kv_ref[0, :, cols].astype(F32), kg_ref[...]).astype(BF16)
        v = kv_ref[0, :, XA_W + h * HD:XA_W + (h + 1) * HD]
        s = lax.dot_general(q, k, (((1,), (1,)), ((), ())), preferred_element_type=F32) * scale
        p = jnp.exp(s - jnp.max(s, axis=-1, keepdims=True))
        z = jnp.sum(p, axis=-1, keepdims=True)
        o_ref[0, :, cols] = jnp.dot(p.astype(BF16), v, preferred_element_type=F32) / z


def _xattn(z3, xq_block, kv3, q_gain, k_gain):
    b, s, _ = z3.shape
    m = kv3.shape[1]
    ts = ROW_TILE
    return pl.pallas_call(
        _xattn_kernel,
        out_shape=jax.ShapeDtypeStruct((b, s, XA_W), F32),
        grid=(b, s // ts),
        in_specs=[pl.BlockSpec((1, ts, XA_W), lambda i, j: (i, j, xq_block)),
                  pl.BlockSpec((1, m, 2 * XA_W), lambda i, j: (i, 0, 0)),
                  pl.BlockSpec((1, HD), lambda i, j: (0, 0)),
                  pl.BlockSpec((1, HD), lambda i, j: (0, 0))],
        out_specs=pl.BlockSpec((1, ts, XA_W), lambda i, j: (i, j, 0)),
        compiler_params=_params("parallel", "parallel"),
        name="cross_attention",
    )(z3, kv3, q_gain.reshape(1, HD), k_gain.reshape(1, HD))


def _dft_kernel(c_ref, s_ref, c0_ref, s0_ref, *, n_fft):
    rows, cols = c_ref.shape
    i = pl.program_id(0)

    @pl.when(i == 0)
    def _():
        k = lax.broadcasted_iota(jnp.int32, (rows, cols), 0)
        n = lax.broadcasted_iota(jnp.int32, (rows, cols), 1)
        ang = ((k * n) & (n_fft - 1)).astype(F32) * (2.0 * math.pi / n_fft)
        c0_ref[...] = jnp.cos(ang)
        s0_ref[...] = jnp.sin(ang)

    period = n_fft // rows
    n = lax.broadcasted_iota(jnp.int32, (8, cols), 1)
    shift = ((i * n) & (period - 1)).astype(F32) * (2.0 * math.pi / period)
    ca = jnp.cos(shift)[0:1]
    sa = jnp.sin(shift)[0:1]
    c0 = c0_ref[...]
    s0 = s0_ref[...]
    c_ref[...] = (ca * c0 - sa * s0).astype(BF16)
    s_ref[...] = (sa * c0 + ca * s0).astype(BF16)


def _dft_tables(length):
    rows = 256
    shape = jax.ShapeDtypeStruct((length, length), BF16)
    spec = pl.BlockSpec((rows, length), lambda i: (i, 0))
    return pl.pallas_call(
        functools.partial(_dft_kernel, n_fft=2 * length),
        out_shape=(shape, shape),
        grid=(length // rows,),
        out_specs=(spec, spec),
        scratch_shapes=[pltpu.VMEM((rows, length), F32), pltpu.VMEM((rows, length), F32)],
        compiler_params=_params("arbitrary"),
        name="dft_tables",
    )()


def _filter_time_kernel(feats_ref, w1_ref, b1_ref, w2_ref, b2_ref, fr_ref, w3_ref, t_ref, delta_ref,
                        gp_ref, gm_ref, hn_ref):
    hp = lax.Precision.HIGHEST
    length = feats_ref.shape[0]
    fr = fr_ref[...]
    h = jnp.sin(fr * (jnp.dot(feats_ref[...], w1_ref[...], preferred_element_type=F32, precision=hp) + b1_ref[...]))
    h = jnp.sin(fr * (jnp.dot(h, w2_ref[...], preferred_element_type=F32, precision=hp) + b2_ref[...]))
    mod = jnp.exp(-t_ref[...] * delta_ref[...]) + MOD_SHIFT
    row = lax.broadcasted_iota(jnp.int32, (length, 1), 0)
    alt = jnp.where(row % 2 == 0, 1.0, -1.0).astype(F32)
    for o in range(2):
        fwd = jnp.dot(h, w3_ref[2 * o], preferred_element_type=F32, precision=hp) * mod
        bwd = jnp.dot(h, w3_ref[2 * o + 1], preferred_element_type=F32, precision=hp) * mod
        bwd = jnp.where(row == 0, 0.0, bwd)
        norm = (jnp.sum(jnp.abs(fwd), axis=0, keepdims=True)
                + jnp.sum(jnp.abs(bwd), axis=0, keepdims=True) + 1e-6)
        fwd = fwd / norm
        bwd = bwd / norm
        gp = fwd + bwd
        gp_ref[o] = gp.astype(BF16)
        gm_ref[o] = (fwd - bwd).astype(BF16)
        hn_ref[o] = jnp.sum(gp * alt, axis=0, keepdims=True) * (1.0 / (2 * length))


def _filter_freq_kernel(c_ref, s_ref, gp_ref, gm_ref, h_ref):
    length = c_ref.shape[0]
    row = lax.broadcasted_iota(jnp.int32, (length, 1), 0)
    scale = jnp.where(row == 0, 1.0, 2.0).astype(F32) * (1.0 / (2 * length))
    h_ref[0, 0] = jnp.dot(c_ref[...], gp_ref[0], preferred_element_type=F32) * scale
    h_ref[0, 1] = -jnp.dot(s_ref[...], gm_ref[0], preferred_element_type=F32) * scale


def _hyena_filters(length, cmat, smat, w1, b1, w2, b2, w3, freq):
    t = jnp.linspace(0.0, 1.0, length, dtype=F32)[:, None]
    f = jnp.linspace(1e-4, FILT_BANDS - 1, FILT_BANDS, dtype=F32)[None]
    ang = (2.0 * math.pi / length) * jnp.arange(length, dtype=F32)[:, None] * f
    feats = jnp.concatenate([t, jnp.cos(ang), -jnp.sin(ang)], axis=-1)
    deltas = jnp.abs(jnp.linspace(math.log(DECAY_TARGET) / SLOW_DECAY_PCT,
                                  math.log(DECAY_TARGET) / FAST_DECAY_PCT, HY_C, dtype=F32))[None]
    hid = w1.shape[1]
    w3r = w3.reshape(hid, 4, HY_C).transpose(1, 0, 2)
    tc = HY_TC
    full = lambda shape: pl.BlockSpec(shape, lambda c: (0,) * len(shape))
    gp, gm, hn = pl.pallas_call(
        _filter_time_kernel,
        out_shape=(jax.ShapeDtypeStruct((2, length, HY_C), BF16),
                   jax.ShapeDtypeStruct((2, length, HY_C), BF16),
                   jax.ShapeDtypeStruct((2, 1, HY_C), F32)),
        grid=(HY_C // tc,),
        in_specs=[full(feats.shape), full(w1.shape), full((1, hid)), full(w2.shape), full((1, hid)),
                  full((1, hid)), pl.BlockSpec((4, hid, tc), lambda c: (0, 0, c)), full((length, 1)),
                  pl.BlockSpec((1, tc), lambda c: (0, c))],
        out_specs=(pl.BlockSpec((2, length, tc), lambda c: (0, 0, c)),
                   pl.BlockSpec((2, length, tc), lambda c: (0, 0, c)),
                   pl.BlockSpec((2, 1, tc), lambda c: (0, 0, c))),
        compiler_params=_params("parallel"),
        name="filter_time",
    )(feats, w1, b1.reshape(1, hid), w2, b2.reshape(1, hid), freq.reshape(1, hid), w3r, t, deltas)
    spectra = pl.pallas_call(
        _filter_freq_kernel,
        out_shape=jax.ShapeDtypeStruct((2, 2, length, HY_C), F32),
        grid=(2, HY_C // tc),
        in_specs=[_resident((length, length), lambda o, c: (0, 0)),
                  _resident((length, length), lambda o, c: (0, 0)),
                  pl.BlockSpec((1, length, tc), lambda o, c: (o, 0, c)),
                  pl.BlockSpec((1, length, tc), lambda o, c: (o, 0, c))],
        out_specs=pl.BlockSpec((1, 2, length, tc), lambda o, c: (o, 0, 0, c)),
        compiler_params=_params("parallel", "parallel"),
        name="filter_freq",
    )(cmat, smat, gp, gm)
    return spectra, hn


def _hyena_kernel(z0_ref, z1_ref, z2_ref, cw_ref, cb_ref, c_ref, s_ref, h_ref, hn_ref, skip_ref, o_ref,
                  ubf_ref, y_ref):
    length = z0_ref.shape[1]
    row = lax.broadcasted_iota(jnp.int32, (length, 1), 0)
    alt = jnp.where(row % 2 == 0, 1.0, -1.0).astype(F32)

    def short_conv(z_ref, c):
        u = z_ref[0].astype(F32)
        prev = jnp.where(row == 0, 0.0, pltpu.roll(u, 1, axis=0))
        nxt = jnp.where(row == length - 1, 0.0, pltpu.roll(u, length - 1, axis=0))
        return cb_ref[c] + (prev * cw_ref[0, c] + u * cw_ref[1, c] + nxt * cw_ref[2, c])

    def long_conv(u, o):
        ubf_ref[...] = u.astype(BF16)
        nyq = jnp.sum(u * alt, axis=0, keepdims=True) * hn_ref[o]
        y_ref[...] = alt * nyq + u * skip_ref[o]
        for kt in range(length // HY_KT):
            ks = slice(kt * HY_KT, (kt + 1) * HY_KT)
            ub = ubf_ref[...]
            a = jnp.dot(c_ref[ks, :], ub, preferred_element_type=F32)
            b = jnp.dot(s_ref[ks, :], ub, preferred_element_type=F32)
            hr = h_ref[o, 0, ks, :]
            hi = h_ref[o, 1, ks, :]
            re = (a * hr + b * hi).astype(BF16)
            im = (a * hi - b * hr).astype(BF16)
            y_ref[...] += (jnp.dot(c_ref[:, ks], re, preferred_element_type=F32)
                           - jnp.dot(s_ref[:, ks], im, preferred_element_type=F32))
        return y_ref[...]

    z = short_conv(z1_ref, 1) * long_conv(short_conv(z0_ref, 0), 0)
    o_ref[0] = short_conv(z2_ref, 2) * long_conv(z, 1)


def _hyena(z3, conv_w, conv_b, cmat, smat, spectra, hn, skip):
    b, s, _ = z3.shape
    tc = HY_TC
    nct = HY_C // tc
    cw = conv_w.reshape(3, 3, 1, HY_C)
    cb = conv_b.reshape(3, 1, HY_C)
    zspec = lambda chunk: pl.BlockSpec((1, s, tc), lambda c, i: (i, 0, chunk * nct + c))
    return pl.pallas_call(
        _hyena_kernel,
        out_shape=jax.ShapeDtypeStruct((b, s, HY_C), F32),
        grid=(nct, b),
        in_specs=[zspec(0), zspec(1), zspec(2),
                  pl.BlockSpec((3, 3, 1, tc), lambda c, i: (0, 0, 0, c)),
                  pl.BlockSpec((3, 1, tc), lambda c, i: (0, 0, c)),
                  _resident((s, s), lambda c, i: (0, 0)),
                  _resident((s, s), lambda c, i: (0, 0)),
                  _resident((2, 2, s, tc), lambda c, i: (0, 0, 0, c)),
                  pl.BlockSpec((2, 1, tc), lambda c, i: (0, 0, c)),
                  pl.BlockSpec((2, 1, tc), lambda c, i: (0, 0, c))],
        out_specs=pl.BlockSpec((1, s, tc), lambda c, i: (i, 0, c)),
        scratch_shapes=[pltpu.VMEM((s, tc), BF16), pltpu.VMEM((s, tc), F32)],
        compiler_params=_params("parallel", "parallel"),
        cost_estimate=pl.CostEstimate(flops=2 * 4 * 2 * b * s * s * HY_C, transcendentals=0,
                                      bytes_accessed=2 * 3 * b * s * HY_C + 4 * b * s * HY_C + 4 * s * s),
        name="hyena",
    )(z3, z3, z3, cw, cb, cmat, smat, spectra, hn, skip.reshape(2, 1, HY_C))


def _rel_bucket(rel):
    half = NUM_BUCKETS // 2
    exact = half // 2
    n = np.abs(rel)
    large = exact + (np.log(np.maximum(n, 1) / exact) / np.log(REL_MAX_DIST / exact) * (half - exact)).astype(np.int32)
    large = np.minimum(large, half - 1)
    return (np.where(rel > 0, half, 0) + np.where(n < exact, n, large)).astype(np.int32)


def _att_tiles(length):
    kw = min(ATT_KW, length)
    tiles = []
    for qs in range(0, length, ATT_TQ):
        ks = min(max(qs - (kw - ATT_TQ) // 2, 0), length - kw)
        tiles.append((qs, ks, {0: 0, -64: 1, -128: 2}[ks - qs]))
    return kw, tiles


def _att_bias(rel_bias, group, seq):
    window, dil = AT_GROUPS[group]
    band = window // (2 * dil)
    kw, tiles = _att_tiles(seq // dil)
    offsets = sorted({ks - qs for qs, ks, _ in tiles}, reverse=True)
    table = rel_bias[:, group * AT_HEADS:(group + 1) * AT_HEADS].astype(F32)
    span = kw + ATT_TQ
    out = []
    for off in offsets:
        k = np.arange(span)
        rel = off + np.where(k < kw, k, k - span)
        diag = jnp.where(np.abs(rel) <= band, table[_rel_bucket(rel * dil)].T, NEG_INF)
        flat = jnp.tile(diag, (1, ATT_TQ))[:, :ATT_TQ * (span - 1)]
        out.append(flat.reshape(AT_HEADS, ATT_TQ, span - 1)[:, :, :kw])
    return jnp.stack(out)


def _dilated_kernel(q1, q2, q3, k1, k2, k3, v1, v2, v3, qg_ref, kg_ref, b1, b2, b3, o_ref,
                    qn_ref, kn_ref, vn_ref, og_ref, lg_ref):
    seq = o_ref.shape[1]
    scale = HD ** -0.5
    contract_last = (((1,), (1,)), ((), ()))
    for g, (q_ref, k_ref, v_ref, bias_ref) in enumerate(((q1, k1, v1, b1), (q2, k2, v2, b2), (q3, k3, v3, b3))):
        dil = AT_GROUPS[g][1]
        kw, tiles = _att_tiles(seq // dil)
        qn_ref[...] = _rms_rows(q_ref[0].astype(F32), qg_ref[...])
        kn_ref[...] = _rms_rows(k_ref[0].astype(F32), kg_ref[...])
        vn_ref[...] = v_ref[0].astype(F32)
        for r in range(dil):
            for qs, ks, var in tiles:
                rows = lambda start, size: (pl.ds(r + start * dil, size, stride=dil) if dil > 1
                                            else pl.ds(start, size))
                qt = qn_ref[rows(qs, ATT_TQ), :].astype(BF16)
                kt = kn_ref[rows(ks, kw), :].astype(BF16)
                vt = vn_ref[rows(ks, kw), :].astype(BF16)
                s = lax.dot_general(qt, kt, contract_last, preferred_element_type=F32) * scale + bias_ref[var, 0]
                m = jnp.max(s, axis=-1, keepdims=True)
                p = jnp.exp(s - m)
                z = jnp.sum(p, axis=-1, keepdims=True)
                og_ref[g, rows(qs, ATT_TQ), :] = jnp.dot(p.astype(BF16), vt, preferred_element_type=F32) / z
                lg_ref[g, rows(qs, ATT_TQ), :] = jnp.broadcast_to(m + jnp.log(z), (ATT_TQ, HD))
    l0, l1, l2 = lg_ref[0], lg_ref[1], lg_ref[2]
    mx = jnp.maximum(jnp.maximum(l0, l1), l2)
    e0, e1, e2 = jnp.exp(l0 - mx), jnp.exp(l1 - mx), jnp.exp(l2 - mx)
    o_ref[0] = (e0 * og_ref[0] + e1 * og_ref[1] + e2 * og_ref[2]) / (e0 + e1 + e2)


def _dilated_attention(z3, q_gain, k_gain, rel_bias):
    b, s, _ = z3.shape
    ng = len(AT_GROUPS)
    col = lambda part, g: pl.BlockSpec((1, s, HD), lambda i, h: (i, 0, part * ng * AT_HEADS + g * AT_HEADS + h))
    biases = [_att_bias(rel_bias, g, s) for g in range(ng)]
    bias_spec = lambda a: pl.BlockSpec((a.shape[0], 1) + a.shape[2:], lambda i, h: (0, h, 0, 0))
    gain_spec = pl.BlockSpec((1, HD), lambda i, h: (0, 0))
    return pl.pallas_call(
        _dilated_kernel,
        out_shape=jax.ShapeDtypeStruct((b, s, AT_HEADS * HD), F32),
        grid=(b, AT_HEADS),
        in_specs=[col(p, g) for p in range(3) for g in range(ng)] + [gain_spec, gain_spec]
                 + [bias_spec(a) for a in biases],
        out_specs=pl.BlockSpec((1, s, HD), lambda i, h: (i, 0, h)),
        scratch_shapes=[pltpu.VMEM((s, HD), F32), pltpu.VMEM((s, HD), F32), pltpu.VMEM((s, HD), F32),
                        pltpu.VMEM((ng, s, HD), F32), pltpu.VMEM((ng, s, HD), F32)],
        compiler_params=_params("parallel", "parallel"),
        name="dilated_attention",
    )(*([z3] * 9), q_gain.reshape(1, HD), k_gain.reshape(1, HD), *biases)


ROUTER_LANES = 128


def _router_kernel(x_ref, g_ref, w_ref, hp_ref, idx_ref, gate_ref, count_ref, carry_ref):
    tm = x_ref.shape[0]
    half = hp_ref.shape[1]

    @pl.when(pl.program_id(0) == 0)
    def _():
        carry_ref[...] = jnp.zeros_like(carry_ref)

    h = _rms_rows(x_ref[...], g_ref[...])
    hp_ref[...] = _bf16_bits(h[:, :half]) | (_bf16_bits(h[:, half:]) << 16)
    h_hi = h.astype(BF16)
    h_lo = (h - h_hi.astype(F32)).astype(BF16)
    logits = (jnp.dot(h_hi, w_ref[0], preferred_element_type=F32)
              + (jnp.dot(h_hi, w_ref[1], preferred_element_type=F32)
                 + jnp.dot(h_lo, w_ref[0], preferred_element_type=F32)))
    lane =lax.broadcasted_iota(jnp.int32, logits.shape, 1).astype(F32)
    logits = jnp.where(lane < N_EXPERTS, logits, -jnp.inf)
    m1 = jnp.max(logits, axis=-1, keepdims=True)
    i1 = jnp.min(jnp.where(logits == m1, lane, float(ROUTER_LANES)), axis=-1, keepdims=True)
    rest = jnp.where(lane == i1, -jnp.inf, logits)
    m2 = jnp.max(rest, axis=-1, keepdims=True)
    i2 = jnp.min(jnp.where(rest == m2, lane, float(ROUTER_LANES)), axis=-1, keepdims=True)
    e2 = jnp.exp(m2 - m1)
    den = 1.0 + e2
    gate_ref[...] = jnp.where(lane == 0, 1.0 / den, e2 / den)
    chosen = jnp.where((lane == i1) | (lane == i2), 1.0, 0.0)
    earlier = lax.broadcasted_iota(jnp.int32, (tm, tm), 0) > lax.broadcasted_iota(jnp.int32, (tm, tm), 1)
    carry = carry_ref[...]
    before = jnp.dot(jnp.where(earlier, 1.0, 0.0).astype(BF16), chosen.astype(BF16),
                     preferred_element_type=F32) + carry
    r1 = jnp.sum(jnp.where(lane == i1, before, 0.0), axis=-1, keepdims=True)
    r2 = jnp.sum(jnp.where(lane == i2, before, 0.0), axis=-1, keepdims=True)
    idx_ref[...] = jnp.where(lane == 0, i1, jnp.where(lane == 1, i2, jnp.where(lane == 2, r1, r2))).astype(jnp.int32)
    carry = carry + jnp.sum(chosen, axis=0, keepdims=True)
    carry_ref[...] = carry
    count_ref[...] = carry.astype(jnp.int32)


def _router(x2, gain, router):
    rows, d = x2.shape
    tm = MOE_CHUNK
    w = jnp.zeros((d, ROUTER_LANES), F32).at[:, :N_EXPERTS].set(router.astype(F32))
    w_hi = w.astype(BF16)
    w = jnp.stack([w_hi, (w - w_hi.astype(F32)).astype(BF16)])
    row_spec = lambda width: pl.BlockSpec((tm, width), lambda i: (i, 0))
    return pl.pallas_call(
        _router_kernel,
        out_shape=(jax.ShapeDtypeStruct((rows, d // 2), jnp.uint32),
                   jax.ShapeDtypeStruct((rows, ROUTER_LANES), jnp.int32),
                   jax.ShapeDtypeStruct((rows, ROUTER_LANES), F32),
                   jax.ShapeDtypeStruct((1, ROUTER_LANES), jnp.int32)),
        grid=(rows // tm,),
        in_specs=[row_spec(d),
                  pl.BlockSpec((1, d), lambda i: (0, 0)),
                  pl.BlockSpec((2, d, ROUTER_LANES), lambda i: (0, 0, 0))],
        out_specs=(row_spec(d // 2), row_spec(ROUTER_LANES), row_spec(ROUTER_LANES),
                   pl.BlockSpec((1, ROUTER_LANES), lambda i: (0, 0))),
        scratch_shapes=[pltpu.VMEM((1, ROUTER_LANES), F32)],
        compiler_params=_params("arbitrary"),
        name="router",
    )(x2, gain.reshape(1, d), w)


def _sc_workers():
    info = plsc.get_sparse_core_info()
    mesh = plsc.VectorSubcoreMesh(core_axis_name="core", subcore_axis_name="subcore")
    return mesh, info.num_cores, info.num_subcores


def _pack_weight_rows(w, col_tile):
    r, c = w.shape
    mesh, nc, ns = _sc_workers()
    rs, ncol = SC_PACK_ROWS, c // col_tile
    per_w = (r // rs) * ncol // (nc * ns)
    assert per_w * nc * ns * rs * col_tile == r * c and per_w % 2 == 0 and col_tile % SC_LANES == 0
    params = pltpu.CompilerParams()
    if "needs_layout_passes" in pltpu.CompilerParams.__dataclass_fields__:
        params = dataclasses.replace(params, needs_layout_passes=False)

    @functools.partial(
        pl.kernel, mesh=mesh, out_type=jax.ShapeDtypeStruct((r // 2, c), jnp.uint32), compiler_params=params,
        cost_estimate=pl.CostEstimate(flops=r * c, transcendentals=0, bytes_accessed=6 * r * c),
        scratch_types=[pltpu.VMEM((2, rs, col_tile), F32), pltpu.VMEM((2, rs // 2, col_tile), jnp.uint32),
                       pltpu.SemaphoreType.DMA((2,)), pltpu.SemaphoreType.DMA((2,))])
    def pack(w_hbm, o_hbm, in_v, out_v, rsem, wsem):
        wid = lax.axis_index("subcore") * nc + lax.axis_index("core")

        @pl.loop(0, per_w // 2)
        def _(it):
            reads, writes = [], []
            for b in range(2):
                tile = wid * per_w + 2 * it + b
                r0 = pl.multiple_of((tile // ncol) * rs, rs)
                c0 = pl.multiple_of((tile % ncol) * col_tile, col_tile)
                reads.append(pltpu.make_async_copy(w_hbm.at[pl.ds(r0, rs), pl.ds(c0, col_tile)], in_v.at[b],
                                                   rsem.at[b]))
                writes.append(pltpu.make_async_copy(
                    out_v.at[b], o_hbm.at[pl.ds(pl.multiple_of(r0 // 2, rs // 2), rs // 2), pl.ds(c0, col_tile)],
                    wsem.at[b]))
            reads[0].start()
            reads[1].start()
            for b in range(2):
                reads[b].wait()
                for pair in range(rs // 2):
                    @plsc.parallel_loop(0, col_tile, step=SC_LANES, unroll=8)
                    def _(j):
                        packed = plsc.pack(in_v[b, 2 * pair, pl.ds(j, SC_LANES)],
                                           in_v[b, 2 * pair + 1, pl.ds(j, SC_LANES)],
                                           format=plsc.PackFormat.INTERLEAVED)
                        out_v[b, pair, pl.ds(j, SC_LANES)] = plsc.bitcast(packed, jnp.uint32)
                writes[b].start()
            writes[0].wait()
            writes[1].wait()

    return pack(w)


def _sc_token_rows(t, nc, ns):
    per_w = t // (nc * ns)
    assert per_w * nc * ns == t and per_w % (2 * SC_ROWS) == 0
    return per_w, per_w // SC_ROWS


def _scatter_rows(table, dests, p_rows, after=()):
    t, w = table.shape
    nk = len(dests)
    mesh, nc, ns = _sc_workers()
    per_w, nit = _sc_token_rows(t, nc, ns)
    ch = SC_ROWS

    @functools.partial(
        pl.kernel, mesh=mesh, out_type=jax.ShapeDtypeStruct((p_rows, w), table.dtype),
        scratch_types=[pltpu.VMEM((nk, nit, ch), jnp.int32), pltpu.VMEM((2, ch, w), table.dtype),
                       pltpu.SemaphoreType.DMA((2,)), pltpu.SemaphoreType.DMA((2, nk))])
    def scatter(table_hbm, *refs):
        dest_hbm, out_hbm = refs[:nk], refs[nk + len(after)]
        idx_v, rows_v, rsem, wsem = refs[nk + len(after) + 1:]
        wid = lax.axis_index("subcore") * nc + lax.axis_index("core")
        for k in range(nk):
            pltpu.sync_copy(dest_hbm[k].at[pl.ds(wid * nit, nit)], idx_v.at[k])

        @pl.loop(0, nit // 2)
        def _(it):
            reads = [pltpu.make_async_copy(table_hbm.at[pl.ds(wid * per_w + (2 * it + b) * ch, ch)],
                                           rows_v.at[b], rsem.at[b]) for b in range(2)]
            writes = [[pltpu.make_async_copy(rows_v.at[b], out_hbm.at[idx_v.at[k].at[2 * it + b]], wsem.at[b, k])
                       for k in range(nk)] for b in range(2)]
            reads[0].start()
            reads[1].start()
            for b in range(2):
                reads[b].wait()
                for k in range(nk):
                    writes[b][k].start()
            for b in range(2):
                for k in range(nk):
                    writes[b][k].wait()

    return scatter(table, *[d.reshape(t // ch, ch) for d in dests], *after)


def _gather_rows(table, idxs):
    t = idxs[0].shape[0]
    w = table.shape[1]
    nk = len(idxs)
    mesh, nc, ns = _sc_workers()
    per_w, nit = _sc_token_rows(t, nc, ns)
    ch = SC_ROWS

    @functools.partial(
        pl.kernel, mesh=mesh, out_type=[jax.ShapeDtypeStruct((t, w), table.dtype)] * nk,
        scratch_types=[pltpu.VMEM((nk, nit, ch), jnp.int32), pltpu.VMEM((nk, 2, ch, w), table.dtype),
                       pltpu.SemaphoreType.DMA((nk, 2)), pltpu.SemaphoreType.DMA((nk, 2))])
    def gather(table_hbm, *refs):
        idx_hbm, out_hbm = refs[:nk], refs[nk:2 * nk]
        idx_v, rows_v, rsem, wsem = refs[2 * nk:]
        wid = lax.axis_index("subcore") * nc + lax.axis_index("core")
        for k in range(nk):
            pltpu.sync_copy(idx_hbm[k].at[pl.ds(wid * nit, nit)], idx_v.at[k])

        @pl.loop(0, nit // 2)
        def _(it):
            slots = [(k, b) for k in range(nk) for b in range(2)]
            reads = {(k, b): pltpu.make_async_copy(table_hbm.at[idx_v.at[k].at[2 * it + b]], rows_v.at[k, b],
                                                   rsem.at[k, b]) for k, b in slots}
            writes = {(k, b): pltpu.make_async_copy(rows_v.at[k, b],
                                                    out_hbm[k].at[pl.ds(wid * per_w + (2 * it + b) * ch, ch)],
                                                    wsem.at[k, b]) for k, b in slots}
            for s in slots:
                reads[s].start()
            for s in slots:
                reads[s].wait()
                writes[s].start()
            for s in slots:
                writes[s].wait()

    return gather(table, *[i.reshape(t // ch, ch) for i in idxs])


def _combine_kernel(x_ref, y0_ref, y1_ref, gate_ref, o_ref):
    half = x_ref.shape[1] // 2
    for part in range(2):
        cols = slice(part * half, (part + 1) * half)
        acc = x_ref[:, cols]
        for k, y_ref in enumerate((y0_ref, y1_ref)):
            word = y_ref[...]
            bits = (word << 16) if part == 0 else (word & jnp.uint32(0xFFFF0000))
            acc = acc + gate_ref[:, k:k + 1] * lax.bitcast_convert_type(bits, F32)
        o_ref[:, cols] = acc


def _combine(x2, y0, y1, gates):
    t, d = x2.shape
    tm = ROW_TILE
    return pl.pallas_call(
        _combine_kernel,
        out_shape=jax.ShapeDtypeStruct((t, d), F32),
        grid=(t // tm,),
        in_specs=[pl.BlockSpec((tm, d), lambda i: (i, 0)),
                  pl.BlockSpec((tm, d // 2), lambda i: (i, 0)),
                  pl.BlockSpec((tm, d // 2), lambda i: (i, 0)),
                  pl.BlockSpec((tm, TOP_K), lambda i: (i, 0))],
        out_specs=pl.BlockSpec((tm, d), lambda i: (i, 0)),
        compiler_params=_params("parallel"),
        name="moe_combine",
    )(x2, y0, y1, gates)


def _moe(x2, gain, router, wg, wu, wd):
    t, d = x2.shape
    hp, idx, gate, count = _router(x2, gain, router)
    experts = jnp.arange(N_EXPERTS, dtype=jnp.int32)
    counts = count[0, :N_EXPERTS]
    padded = (counts + MOE_ROWS - 1) // MOE_ROWS * MOE_ROWS
    pend = jnp.cumsum(padded)
    pstart = pend - padded
    expert, rank = idx[:, :TOP_K], idx[:, TOP_K:2 * TOP_K]
    dest = jnp.sum(jnp.where(expert[:, :, None] == experts, pstart, 0), axis=-1) + rank
    dests = [dest[:, k] for k in range(TOP_K)]
    p_rows = t * TOP_K + N_EXPERTS * MOE_ROWS
    blk_row = jnp.arange(p_rows // MOE_ROWS, dtype=jnp.int32) * MOE_ROWS
    blk_expert = jnp.minimum(jnp.sum(pend[None, :] <= blk_row[:, None], axis=1), N_EXPERTS - 1).astype(jnp.int32)
    valid = jnp.clip(counts[blk_expert] - (blk_row - pstart[blk_expert]), 0, MOE_ROWS)
    valid = jnp.where(blk_row < pend[-1], valid, 0).astype(jnp.int32)
    n_exp, _, f = wg.shape
    pack = lambda w, col_tile: _pack_weight_rows(w.reshape(-1, w.shape[2]), col_tile).reshape(n_exp, -1, w.shape[2])
    wg, wu, wd = pack(wg, f // 2), pack(wu, f // 2), pack(wd, d)
    hb = _scatter_rows(hp, dests, p_rows, after=(wg, wu, wd))
    yb = _ffn_experts(hb, wg, wu, wd, blk_expert, valid)
    y0, y1 = _gather_rows(yb, dests)
    return _combine(x2, y0, y1, gate[:, :TOP_K])


def kernel(x, mem, rel_bias, norm_mix, norm_mem, norm_ffn, w_mem_kv, xq_norm, xk_norm, w_out, hy_w_in, hy_conv_w, hy_conv_b, hy_filt_w1, hy_filt_b1, hy_filt_w2, hy_filt_b2, hy_filt_w3, hy_sin_freq, hy_skip, at_w_in, at_q_norm, at_k_norm, ffn_w_gate, ffn_w_up, ffn_w_down, moe_router, moe_w_gate, moe_w_up, moe_w_down):
    b, s, d = x.shape
    t = b * s
    m_len = mem.shape[1]
    x2 = x.reshape(t, d)
    mem2 = mem.reshape(b * m_len, d)
    bf = lambda w: w.astype(BF16)

    cmat, smat = _dft_tables(s)
    spectra, hn = _hyena_filters(s, cmat, smat, hy_filt_w1[0], hy_filt_b1[0], hy_filt_w2[0], hy_filt_b2[0],
                                 hy_filt_w3[0], hy_sin_freq[0])
    z = _norm_matmul(x2, norm_mix[0], bf(hy_w_in[0]), 1024).reshape(b, s, -1)
    kv = _norm_matmul(mem2, norm_mem[0], bf(w_mem_kv[0]), 1024).reshape(b, m_len, -1)
    self_out = _hyena(z, hy_conv_w[0], hy_conv_b[0], cmat, smat, spectra, hn, hy_skip[0])
    cross = _xattn(z, 3 * HY_C // XA_W, kv, xq_norm[0], xk_norm[0])
    x2 = _outproj(x2, self_out.reshape(t, -1), cross.reshape(t, -1), bf(w_out[0]))
    x2 = _ffn_dense(x2, norm_ffn[0], bf(ffn_w_gate[0]), bf(ffn_w_up[0]), bf(ffn_w_down[0]))

    z = _norm_matmul(x2, norm_mix[1], bf(at_w_in[0]), 1024).reshape(b, s, -1)
    kv = _norm_matmul(mem2, norm_mem[1], bf(w_mem_kv[1]), 1024).reshape(b, m_len, -1)
    self_out = _dilated_attention(z, at_q_norm[0], at_k_norm[0], rel_bias)
    cross = _xattn(z, 3 * AT_W // XA_W, kv, xq_norm[1], xk_norm[1])
    x2 = _outproj(x2, self_out.reshape(t, -1), cross.reshape(t, -1), bf(w_out[1]))
    x2 = _moe(x2, norm_ffn[1], moe_router[0], moe_w_gate[0], moe_w_up[0], moe_w_down[0])
    return x2.reshape(b, s, d)
```

```python
import dataclasses
import functools
import math

import jax
import jax.numpy as jnp
import numpy as np
from jax import lax
from jax.experimental import pallas as pl
from jax.experimental.pallas import tpu as pltpu
from jax.experimental.pallas import tpu_sc as plsc

F32 = jnp.float32
BF16 = jnp.bfloat16

D_MODEL = 1024
EPS = 1e-6
HY_C = 512
FILT_BANDS = 16
DECAY_TARGET = 1e-2
FAST_DECAY_PCT = 0.3
SLOW_DECAY_PCT = 1.5
MOD_SHIFT = 0.05
AT_GROUPS = ((128, 1), (512, 4), (2048, 16))
AT_HEADS = 4
HD = 128
AT_W = 1536
NUM_BUCKETS = 32
REL_MAX_DIST = 1024
NEG_INF = -1e30
XA_W = 512
D_FF = 2816
N_EXPERTS = 8
TOP_K = 2

VMEM_LIMIT_BYTES = 56 * 1024 * 1024
ROW_TILE = 512
FF_TILE = 1408
MOE_ROWS = 512
MOE_CHUNK = 512
SC_ROWS = 32
SC_LANES = 16
SC_PACK_ROWS = 16
ATT_TQ = 128
ATT_KW = 256
HY_TC = 256
HY_KT = 512


def _params(*sem):
    return pltpu.CompilerParams(dimension_semantics=sem, vmem_limit_bytes=VMEM_LIMIT_BYTES)


def _rms_rows(x, gain):
    return x * lax.rsqrt(jnp.mean(x * x, axis=-1, keepdims=True) + EPS) * gain


def _bf16_bits(v):
    u = lax.bitcast_convert_type(v, jnp.uint32)
    return (u + jnp.uint32(0x7FFF) + ((u >> 16) & jnp.uint32(1))) >> 16


def _resident(shape, index_map):
    return pl.BlockSpec(shape, index_map, pipeline_mode=pl.Buffered(1))


def _norm_matmul_kernel(x_ref, g_ref, w_ref, o_ref, *, tn):
    h = _rms_rows(x_ref[...], g_ref[...]).astype(BF16)
    for c in range(o_ref.shape[1] // tn):
        cols = slice(c * tn, (c + 1) * tn)
        o_ref[:, cols] = jnp.dot(h, w_ref[:, cols], preferred_element_type=F32).astype(o_ref.dtype)


def _norm_matmul(x2, gain, w_bf, tn):
    rows, d = x2.shape
    n = w_bf.shape[1]
    tm = min(ROW_TILE, rows)
    return pl.pallas_call(
        functools.partial(_norm_matmul_kernel, tn=tn),
        out_shape=jax.ShapeDtypeStruct((rows, n), BF16),
        grid=(rows // tm,),
        in_specs=[pl.BlockSpec((tm, d), lambda i: (i, 0)),
                  pl.BlockSpec((1, d), lambda i: (0, 0)),
                  _resident((d, n), lambda i: (0, 0))],
        out_specs=pl.BlockSpec((tm, n), lambda i: (i, 0)),
        compiler_params=_params("parallel"),
        cost_estimate=pl.CostEstimate(flops=2 * rows * d * n, transcendentals=rows,
                                      bytes_accessed=4 * rows * d + 2 * d * n + 2 * rows * n),
        name="norm_matmul",
    )(x2, gain.reshape(1, d), w_bf)


def _outproj_kernel(x_ref, a_ref, c_ref, wa_ref, wc_ref, o_ref):
    o_ref[...] = (x_ref[...]
                  + jnp.dot(a_ref[...].astype(BF16), wa_ref[...], preferred_element_type=F32)
                  + jnp.dot(c_ref[...].astype(BF16), wc_ref[...], preferred_element_type=F32))


def _outproj(x2, self_out, cross, w_bf):
    rows, d = x2.shape
    half = self_out.shape[1]
    tm = ROW_TILE
    return pl.pallas_call(
        _outproj_kernel,
        out_shape=jax.ShapeDtypeStruct((rows, d), F32),
        grid=(rows // tm,),
        in_specs=[pl.BlockSpec((tm, d), lambda i: (i, 0)),
                  pl.BlockSpec((tm, half), lambda i: (i, 0)),
                  pl.BlockSpec((tm, half), lambda i: (i, 0)),
                  pl.BlockSpec((half, d), lambda i: (0, 0)),
                  pl.BlockSpec((half, d), lambda i: (1, 0))],
        out_specs=pl.BlockSpec((tm, d), lambda i: (i, 0)),
        compiler_params=_params("parallel"),
        name="outproj",
    )(x2, self_out, cross, w_bf, w_bf)


def _weight_rows(ref, start, size):
    if ref.dtype == jnp.uint32:
        return lambda cols: pltpu.bitcast(ref[start // 2:(start + size) // 2, cols], BF16)
    return lambda cols: ref[start:start + size, cols]


def _weight_shape(ref):
    return (ref.shape[0] * (2 if ref.dtype == jnp.uint32 else 1), ref.shape[1])


def _swiglu(h, wg_ref, wu_ref, wd_ref, tf):
    d, f = _weight_shape(wg_ref)
    y = None
    for j in range(f // tf):
        cols = slice(j * tf, (j + 1) * tf)
        gg = jnp.dot(h, _weight_rows(wg_ref, 0, d)(cols), preferred_element_type=F32)
        uu = jnp.dot(h, _weight_rows(wu_ref, 0, d)(cols), preferred_element_type=F32)
        a = ((gg * jax.nn.sigmoid(gg)) * uu).astype(BF16)
        part = jnp.dot(a, _weight_rows(wd_ref, j * tf, tf)(slice(None)), preferred_element_type=F32)
        y = part if y is None else y + part
    return y


def _ffn_dense_kernel(x_ref, g_ref, wg_ref, wu_ref, wd_ref, o_ref, *, tf):
    x = x_ref[...]
    h = _rms_rows(x, g_ref[...]).astype(BF16)
    o_ref[...] = x + _swiglu(h, wg_ref, wu_ref, wd_ref, tf)


def _ffn_dense(x2, gain, wg, wu, wd):
    rows, d = x2.shape
    f = wg.shape[1]
    tm = ROW_TILE
    return pl.pallas_call(
        functools.partial(_ffn_dense_kernel, tf=FF_TILE),
        out_shape=jax.ShapeDtypeStruct((rows, d), F32),
        grid=(rows // tm,),
        in_specs=[pl.BlockSpec((tm, d), lambda i: (i, 0)),
                  pl.BlockSpec((1, d), lambda i: (0, 0)),
                  _resident((d, f), lambda i: (0, 0)),
                  _resident((d, f), lambda i: (0, 0)),
                  _resident((f, d), lambda i: (0, 0))],
        out_specs=pl.BlockSpec((tm, d), lambda i: (i, 0)),
        compiler_params=_params("parallel"),
        cost_estimate=pl.CostEstimate(flops=6 * rows * d * f, transcendentals=rows * f,
                                      bytes_accessed=8 * rows * d + 6 * d * f),
        name="ffn_dense",
    )(x2, gain.reshape(1, d), wg, wu, wd)


def _ffn_expert_kernel(eid_ref, valid_ref, hp_ref, wg_ref, wu_ref, wd_ref, o_ref, h_ref, *, tf):
    half = hp_ref.shape[1]
    valid = valid_ref[pl.program_id(0)]

    @pl.when(valid > 0)
    def _():
        keep = lax.broadcasted_iota(jnp.int32, (hp_ref.shape[0], 1), 0) < valid
        word = hp_ref[...]
        h_ref[:, :half] = jnp.where(keep, lax.bitcast_convert_type(word << 16, F32), 0.0).astype(BF16)
        h_ref[:, half:] = jnp.where(keep, lax.bitcast_convert_type(word & jnp.uint32(0xFFFF0000), F32),
                                    0.0).astype(BF16)
        y = _swiglu(h_ref[...], wg_ref.at[0], wu_ref.at[0], wd_ref.at[0], tf)
        o_ref[...] = _bf16_bits(y[:, :half]) | (_bf16_bits(y[:, half:]) << 16)

    @pl.when(valid <= 0)
    def _():
        o_ref[...] = jnp.zeros_like(o_ref)


def _ffn_experts(hp, wg, wu, wd, eid, valid):
    rows, half = hp.shape
    d = 2 * half
    tm = MOE_ROWS
    expert_spec = lambda w: pl.BlockSpec((1,) + w.shape[1:], lambda i, e, n: (e[i], 0, 0))
    grid_spec = pltpu.PrefetchScalarGridSpec(
        num_scalar_prefetch=2,
        grid=(rows // tm,),
        in_specs=[pl.BlockSpec((tm, half), lambda i, e, n: (i, 0)),
                  expert_spec(wg), expert_spec(wu), expert_spec(wd)],
        out_specs=pl.BlockSpec((tm, half), lambda i, e, n: (i, 0)),
        scratch_shapes=[pltpu.VMEM((tm, d), BF16)],
    )
    return pl.pallas_call(
        functools.partial(_ffn_expert_kernel, tf=FF_TILE),
        out_shape=jax.ShapeDtypeStruct((rows, half), jnp.uint32),
        grid_spec=grid_spec,
        compiler_params=_params("arbitrary"),
        name="ffn_experts",
    )(eid, valid, hp, wg, wu, wd)


def _xattn_kernel(xq_ref, kv_ref, qg_ref, kg_ref, o_ref):
    scale = HD ** -0.5
    for h in range(XA_W // HD):
        cols = slice(h * HD, (h + 1) * HD)
        q = _rms_rows(xq_ref[0, :, cols].astype(F32), qg_ref[...]).astype(BF16)
        k = _rms_rows(kv_ref[0, :, cols].astype(F32), kg_ref[...]).astype(BF16)
        v = kv_ref[0, :, XA_W + h * HD:XA_W + (h + 1) * HD]
        s = lax.dot_general(q, k, (((1,), (1,)), ((), ())), preferred_element_type=F32) * scale
        p = jnp.exp(s - jnp.max(s, axis=-1, keepdims=True))
        z = jnp.sum(p, axis=-1, keepdims=True)
        o_ref[0, :, cols] = jnp.dot(p.astype(BF16), v, preferred_element_type=F32) / z


def _xattn(z3, xq_block, kv3, q_gain, k_gain):
    b, s, _ = z3.shape
    m = kv3.shape[1]
    ts = ROW_TILE
    return pl.pallas_call(
        _xattn_kernel,
        out_shape=jax.ShapeDtypeStruct((b, s, XA_W), F32),
        grid=(b, s // ts),
        in_specs=[pl.BlockSpec((1, ts, XA_W), lambda i, j: (i, j, xq_block)),
                  pl.BlockSpec((1, m, 2 * XA_W), lambda i, j: (i, 0, 0)),
                  pl.BlockSpec((1, HD), lambda i, j: (0, 0)),
                  pl.BlockSpec((1, HD), lambda i, j: (0, 0))],
        out_specs=pl.BlockSpec((1, ts, XA_W), lambda i, j: (i, j, 0)),
        compiler_params=_params("parallel", "parallel"),
        name="cross_attention",
    )(z3, kv3, q_gain.reshape(1, HD), k_gain.reshape(1, HD))


def _dft_kernel(ce_ref, se_ref, co_ref, so_ref, cot_ref, sot_ref, base_ref, *, n_fft):
    rows, cols = ce_ref.shape
    i = pl.program_id(0)
    theta = 2.0 * math.pi / n_fft

    def phases(r, c):
        return r * (2 * c), r * (2 * c + 1), c * (2 * r + 1)

    @pl.when(i == 0)
    def _():
        r = lax.broadcasted_iota(jnp.int32, (rows, cols), 0)
        c = lax.broadcasted_iota(jnp.int32, (rows, cols), 1)
        for f, ph in enumerate(phases(r, c)):
            ang = (ph & (n_fft - 1)).astype(F32) * theta
            base_ref[2 * f] = jnp.cos(ang)
            base_ref[2 * f + 1] = jnp.sin(ang)

    c = lax.broadcasted_iota(jnp.int32, (8, cols), 1)
    r0 = i * rows
    shifts = (r0 * (2 * c), r0 * (2 * c + 1), c * (2 * r0))
    for f, (c_ref, s_ref) in enumerate(((ce_ref, se_ref), (co_ref, so_ref), (cot_ref, sot_ref))):
        ang = (shifts[f] & (n_fft - 1)).astype(F32) * theta
        ca, sa = jnp.cos(ang)[0:1], jnp.sin(ang)[0:1]
        cb, sb = base_ref[2 * f], base_ref[2 * f + 1]
        c_ref[...] = (cb * ca - sb * sa).astype(BF16)
        s_ref[...] = (sb * ca + cb * sa).astype(BF16)


def _dft_tables(length):
    half, rows = length // 2, 128
    shape = jax.ShapeDtypeStruct((half, half), BF16)
    spec = pl.BlockSpec((rows, half), lambda i: (i, 0))
    return pl.pallas_call(
        functools.partial(_dft_kernel, n_fft=2 * length),
        out_shape=(shape,) * 6,
        grid=(half // rows,),
        out_specs=(spec,) * 6,
        scratch_shapes=[pltpu.VMEM((6, rows, half), F32)],
        compiler_params=_params("arbitrary"),
        name="dft_tables",
    )()


LANES = 128


def _split_parity(tmp_ref, x):
    half = x.shape[0] // 2
    for j in range(tmp_ref.shape[0]):
        tmp_ref[j] = x[:, j * LANES:(j + 1) * LANES]
    pick = lambda start: jnp.concatenate(
        [tmp_ref[j, pl.ds(start, half, stride=2), :] for j in range(tmp_ref.shape[0])], axis=1)
    return pick(0), pick(1)


def _merge_parity(tmp_ref, even, odd):
    half = even.shape[0]
    for j in range(tmp_ref.shape[0]):
        tmp_ref[j, pl.ds(0, half, stride=2), :] = even[:, j * LANES:(j + 1) * LANES]
        tmp_ref[j, pl.ds(1, half, stride=2), :] = odd[:, j * LANES:(j + 1) * LANES]
    return jnp.concatenate([tmp_ref[j] for j in range(tmp_ref.shape[0])], axis=1)


def _filter_time_kernel(feats_ref, w1_ref, b1_ref, w2_ref, b2_ref, fr_ref, w3_ref, t_ref, delta_ref,
                        g_ref, hq_ref, tmp_ref):
    hp = lax.Precision.HIGHEST
    length = feats_ref.shape[0]
    half = length // 2
    fr = fr_ref[...]
    h = jnp.sin(fr * (jnp.dot(feats_ref[...], w1_ref[...], preferred_element_type=F32, precision=hp) + b1_ref[...]))
    h = jnp.sin(fr * (jnp.dot(h, w2_ref[...], preferred_element_type=F32, precision=hp) + b2_ref[...]))
    mod = jnp.exp(-t_ref[...] * delta_ref[...]) + MOD_SHIFT
    row = lax.broadcasted_iota(jnp.int32, (length, 1), 0)
    alt = jnp.where(lax.broadcasted_iota(jnp.int32, (half, 1), 0) % 2 == 0, 1.0, -1.0).astype(F32)
    for o in range(2):
        fwd = jnp.dot(h, w3_ref[2 * o], preferred_element_type=F32, precision=hp) * mod
        bwd = jnp.dot(h, w3_ref[2 * o + 1], preferred_element_type=F32, precision=hp) * mod
        bwd = jnp.where(row == 0, 0.0, bwd)
        norm = (jnp.sum(jnp.abs(fwd), axis=0, keepdims=True)
                + jnp.sum(jnp.abs(bwd), axis=0, keepdims=True) + 1e-6)
        fwd = fwd / norm
        bwd = bwd / norm
        for part, g in enumerate((fwd + bwd, fwd - bwd)):
            even, odd = _split_parity(tmp_ref, g)
            g_ref[o, 2 * part] = even.astype(BF16)
            g_ref[o, 2 * part + 1] = odd.astype(BF16)
            mid = jnp.sum((even if part == 0 else odd) * alt, axis=0, keepdims=True) * (1.0 / length)
            hq_ref[o, part] = mid if part == 0 else -mid


def _filter_freq_kernel(ce_ref, se_ref, co_ref, so_ref, g_ref, h_ref):
    half = ce_ref.shape[0]
    row = lax.broadcasted_iota(jnp.int32, (half, 1), 0)
    scale = jnp.where(row == 0, 1.0, 2.0).astype(F32) * (1.0 / (4 * half))
    a = jnp.dot(ce_ref[...], g_ref[0, 0], preferred_element_type=F32)
    b = jnp.dot(co_ref[...], g_ref[0, 1], preferred_element_type=F32)
    c = jnp.dot(se_ref[...], g_ref[0, 2], preferred_element_type=F32)
    d = jnp.dot(so_ref[...], g_ref[0, 3], preferred_element_type=F32)
    h_ref[0, 0] = (a + b) * scale
    h_ref[0, 1] = -(c + d) * scale
    h_ref[0, 2] = (a - b) * scale
    h_ref[0, 3] = (c - d) * scale


def _hyena_filters(length, tables, w1, b1, w2, b2, w3, freq):
    t = jnp.linspace(0.0, 1.0, length, dtype=F32)[:, None]
    f = jnp.linspace(1e-4, FILT_BANDS - 1, FILT_BANDS, dtype=F32)[None]
    ang = (2.0 * math.pi / length) * jnp.arange(length, dtype=F32)[:, None] * f
    feats = jnp.concatenate([t, jnp.cos(ang), -jnp.sin(ang)], axis=-1)
    deltas = jnp.abs(jnp.linspace(math.log(DECAY_TARGET) / SLOW_DECAY_PCT,
                                  math.log(DECAY_TARGET) / FAST_DECAY_PCT, HY_C, dtype=F32))[None]
    hid = w1.shape[1]
    w3r = w3.reshape(hid, 4, HY_C).transpose(1, 0, 2)
    tc = HY_TC
    full = lambda shape: pl.BlockSpec(shape, lambda c: (0,) * len(shape))
    half = length // 2
    g, hq = pl.pallas_call(
        _filter_time_kernel,
        out_shape=(jax.ShapeDtypeStruct((2, 4, half, HY_C), BF16),
                   jax.ShapeDtypeStruct((2, 2, 1, HY_C), F32)),
        grid=(HY_C // tc,),
        in_specs=[full(feats.shape), full(w1.shape), full((1, hid)), full(w2.shape), full((1, hid)),
                  full((1, hid)), pl.BlockSpec((4, hid, tc), lambda c: (0, 0, c)), full((length, 1)),
                  pl.BlockSpec((1, tc), lambda c: (0, c))],
        out_specs=(pl.BlockSpec((2, 4, half, tc), lambda c: (0, 0, 0, c)),
                   pl.BlockSpec((2, 2, 1, tc), lambda c: (0, 0, 0, c))),
        scratch_shapes=[pltpu.VMEM((tc // LANES, length, LANES), F32)],
        compiler_params=_params("parallel"),
        name="filter_time",
    )(feats, w1, b1.reshape(1, hid), w2, b2.reshape(1, hid), freq.reshape(1, hid), w3r, t, deltas)
    ce, se, co, so = tables[:4]
    table_spec = _resident((half, half), lambda o, c: (0, 0))
    spectra = pl.pallas_call(
        _filter_freq_kernel,
        out_shape=jax.ShapeDtypeStruct((2, 4, half, HY_C), F32),
        grid=(2, HY_C // tc),
        in_specs=[table_spec] * 4 + [pl.BlockSpec((1, 4, half, tc), lambda o, c: (o, 0, 0, c))],
        out_specs=pl.BlockSpec((1, 4, half, tc), lambda o, c: (o, 0, 0, c)),
        compiler_params=_params("parallel", "parallel"),
        name="filter_freq",
    )(ce, se, co, so, g)
    return spectra, hq


def _hyena_kernel(z0_ref, z1_ref, z2_ref, cw_ref, cb_ref, ce_ref, se_ref, co_ref, so_ref, cot_ref, sot_ref,
                  h_ref, hq_ref, skip_ref, o_ref, tmp_ref):
    length = z0_ref.shape[1]
    half = length // 2
    row = lax.broadcasted_iota(jnp.int32, (length, 1), 0)
    alt = jnp.where(lax.broadcasted_iota(jnp.int32, (half, 1), 0) % 2 == 0, 1.0, -1.0).astype(F32)

    def short_conv(z_ref, c):
        u = z_ref[0].astype(F32)
        prev = jnp.where(row == 0, 0.0, pltpu.roll(u, 1, axis=0))
        nxt = jnp.where(row == length - 1, 0.0, pltpu.roll(u, length - 1, axis=0))
        return cb_ref[c] + (prev * cw_ref[0, c] + u * cw_ref[1, c] + nxt * cw_ref[2, c])

    def long_conv(u, o):
        ue, uo = _split_parity(tmp_ref, u)
        ue_bf, uo_bf = ue.astype(BF16), uo.astype(BF16)
        a4 = jnp.sum(ue * alt, axis=0, keepdims=True)
        b4 = jnp.sum(uo * alt, axis=0, keepdims=True)
        hr4, hi4 = hq_ref[o, 0], hq_ref[o, 1]
        y_even = alt * (a4 * hr4 + b4 * hi4)
        y_odd = -(alt * (a4 * hi4 - b4 * hr4))
        for kt in range(half // HY_KT):
            ks = slice(kt * HY_KT, (kt + 1) * HY_KT)
            ae = jnp.dot(ce_ref[ks, :], ue_bf, preferred_element_type=F32)
            ao = jnp.dot(co_ref[ks, :], uo_bf, preferred_element_type=F32)
            be = jnp.dot(se_ref[ks, :], ue_bf, preferred_element_type=F32)
            bo = jnp.dot(so_ref[ks, :], uo_bf, preferred_element_type=F32)
            a_f, a_g, b_f, b_g = ae + ao, ae - ao, be + bo, bo - be
            hr_f, hi_f, hr_g, hi_g = (h_ref[o, j, ks, :] for j in range(4))
            re_f, im_f = a_f * hr_f + b_f * hi_f, a_f * hi_f - b_f * hr_f
            re_g, im_g = a_g * hr_g + b_g * hi_g, a_g * hi_g - b_g * hr_g
            y_even += (jnp.dot(ce_ref[:, ks], (re_f + re_g).astype(BF16), preferred_element_type=F32)
                       - jnp.dot(se_ref[:, ks], (im_f - im_g).astype(BF16), preferred_element_type=F32))
            y_odd += (jnp.dot(cot_ref[:, ks], (re_f - re_g).astype(BF16), preferred_element_type=F32)
                      - jnp.dot(sot_ref[:, ks], (im_f + im_g).astype(BF16), preferred_element_type=F32))
        return _merge_parity(tmp_ref, y_even, y_odd) + u * skip_ref[o]

    z = short_conv(z1_ref, 1) * long_conv(short_conv(z0_ref, 0), 0)
    o_ref[0] = short_conv(z2_ref, 2) * long_conv(z, 1)


def _hyena(z3, conv_w, conv_b, tables, spectra, hq, skip):
    b, s, _ = z3.shape
    half = s // 2
    tc = HY_TC
    nct = HY_C // tc
    cw = conv_w.reshape(3, 3, 1, HY_C)
    cb = conv_b.reshape(3, 1, HY_C)
    zspec = lambda chunk: pl.BlockSpec((1, s, tc), lambda c, i: (i, 0, chunk * nct + c))
    return pl.pallas_call(
        _hyena_kernel,
        out_shape=jax.ShapeDtypeStruct((b, s, HY_C), F32),
        grid=(nct, b),
        in_specs=[zspec(0), zspec(1), zspec(2),
                  pl.BlockSpec((3, 3, 1, tc), lambda c, i: (0, 0, 0, c)),
                  pl.BlockSpec((3, 1, tc), lambda c, i: (0, 0, c))]
                 + [_resident((half, half), lambda c, i: (0, 0))] * 6
                 + [_resident((2, 4, half, tc), lambda c, i: (0, 0, 0, c)),
                    pl.BlockSpec((2, 2, 1, tc), lambda c, i: (0, 0, 0, c)),
                    pl.BlockSpec((2, 1, tc), lambda c, i: (0, 0, c))],
        out_specs=pl.BlockSpec((1, s, tc), lambda c, i: (i, 0, c)),
        scratch_shapes=[pltpu.VMEM((tc // LANES, s, LANES), F32)],
        compiler_params=_params("parallel", "parallel"),
        cost_estimate=pl.CostEstimate(flops=2 * 4 * 2 * b * s * half * HY_C, transcendentals=0,
                                      bytes_accessed=2 * 3 * b * s * HY_C + 4 * b * s * HY_C + 12 * half * half),
        name="hyena",
    )(z3, z3, z3, cw, cb, *tables, spectra, hq, skip.reshape(2, 1, HY_C))


def _rel_bucket(rel):
    half = NUM_BUCKETS // 2
    exact = half // 2
    n = np.abs(rel)
    large = exact + (np.log(np.maximum(n, 1) / exact) / np.log(REL_MAX_DIST / exact) * (half - exact)).astype(np.int32)
    large = np.minimum(large, half - 1)
    return (np.where(rel > 0, half, 0) + np.where(n < exact, n, large)).astype(np.int32)


def _att_tiles(length):
    kw = min(ATT_KW, length)
    tiles = []
    for qs in range(0, length, ATT_TQ):
        ks = min(max(qs - (kw - ATT_TQ) // 2, 0), length - kw)
        tiles.append((qs, ks, {0: 0, -64: 1, -128: 2}[ks - qs]))
    return kw, tiles


def _att_bias(rel_bias, group, seq):
    window, dil = AT_GROUPS[group]
    band = window // (2 * dil)
    kw, tiles = _att_tiles(seq // dil)
    offsets = sorted({ks - qs for qs, ks, _ in tiles}, reverse=True)
    table = rel_bias[:, group * AT_HEADS:(group + 1) * AT_HEADS].astype(F32)
    span = kw + ATT_TQ
    out = []
    for off in offsets:
        k = np.arange(span)
        rel = off + np.where(k < kw, k, k - span)
        diag = jnp.where(np.abs(rel) <= band, table[_rel_bucket(rel * dil)].T, NEG_INF)
        flat = jnp.tile(diag, (1, ATT_TQ))[:, :ATT_TQ * (span - 1)]
        out.append(flat.reshape(AT_HEADS, ATT_TQ, span - 1)[:, :, :kw])
    return jnp.stack(out)


def _dilated_kernel(q1, q2, q3, k1, k2, k3, v1, v2, v3, qg_ref, kg_ref, b1, b2, b3, o_ref,
                    qn_ref, kn_ref, vn_ref, og_ref, lg_ref):
    seq = o_ref.shape[1]
    scale = HD ** -0.5
    contract_last = (((1,), (1,)), ((), ()))
    for g, (q_ref, k_ref, v_ref, bias_ref) in enumerate(((q1, k1, v1, b1), (q2, k2, v2, b2), (q3, k3, v3, b3))):
        dil = AT_GROUPS[g][1]
        kw, tiles = _att_tiles(seq // dil)
        qn_ref[...] = _rms_rows(q_ref[0].astype(F32), qg_ref[...])
        kn_ref[...] = _rms_rows(k_ref[0].astype(F32), kg_ref[...])
        vn_ref[...] = v_ref[0].astype(F32)
        for r in range(dil):
            for qs, ks, var in tiles:
                rows = lambda start, size: (pl.ds(r + start * dil, size, stride=dil) if dil > 1
                                            else pl.ds(start, size))
                qt = qn_ref[rows(qs, ATT_TQ), :].astype(BF16)
                kt = kn_ref[rows(ks, kw), :].astype(BF16)
                vt = vn_ref[rows(ks, kw), :].astype(BF16)
                s = lax.dot_general(qt, kt, contract_last, preferred_element_type=F32) * scale + bias_ref[var, 0]
                m = jnp.max(s, axis=-1, keepdims=True)
                p = jnp.exp(s - m)
                z = jnp.sum(p, axis=-1, keepdims=True)
                og_ref[g, rows(qs, ATT_TQ), :] = jnp.dot(p.astype(BF16), vt, preferred_element_type=F32) / z
                lg_ref[g, rows(qs, ATT_TQ), :] = jnp.broadcast_to(m + jnp.log(z), (ATT_TQ, HD))
    l0, l1, l2 = lg_ref[0], lg_ref[1], lg_ref[2]
    mx = jnp.maximum(jnp.maximum(l0, l1), l2)
    e0, e1, e2 = jnp.exp(l0 - mx), jnp.exp(l1 - mx), jnp.exp(l2 - mx)
    o_ref[0] = (e0 * og_ref[0] + e1 * og_ref[1] + e2 * og_ref[2]) / (e0 + e1 + e2)


def _dilated_attention(z3, q_gain, k_gain, rel_bias):
    b, s, _ = z3.shape
    ng = len(AT_GROUPS)
    col = lambda part, g: pl.BlockSpec((1, s, HD), lambda i, h: (i, 0, part * ng * AT_HEADS + g * AT_HEADS + h))
    biases = [_att_bias(rel_bias, g, s) for g in range(ng)]
    bias_spec = lambda a: pl.BlockSpec((a.shape[0], 1) + a.shape[2:], lambda i, h: (0, h, 0, 0))
    gain_spec = pl.BlockSpec((1, HD), lambda i, h: (0, 0))
    return pl.pallas_call(
        _dilated_kernel,
        out_shape=jax.ShapeDtypeStruct((b, s, AT_HEADS * HD), F32),
        grid=(b, AT_HEADS),
        in_specs=[col(p, g) for p in range(3) for g in range(ng)] + [gain_spec, gain_spec]
                 + [bias_spec(a) for a in biases],
        out_specs=pl.BlockSpec((1, s, HD), lambda i, h: (i, 0, h)),
        scratch_shapes=[pltpu.VMEM((s, HD), F32), pltpu.VMEM((s, HD), F32), pltpu.VMEM((s, HD), F32),
                        pltpu.VMEM((ng, s, HD), F32), pltpu.VMEM((ng, s, HD), F32)],
        compiler_params=_params("parallel", "parallel"),
        name="dilated_attention",
    )(*([z3] * 9), q_gain.reshape(1, HD), k_gain.reshape(1, HD), *biases)


ROUTER_LANES = 128


def _router_kernel(x_ref, g_ref, w_ref, hp_ref, idx_ref, gate_ref, count_ref, carry_ref):
    tm = x_ref.shape[0]
    half = hp_ref.shape[1]

    @pl.when(pl.program_id(0) == 0)
    def _():
        carry_ref[...] = jnp.zeros_like(carry_ref)

    h = _rms_rows(x_ref[...], g_ref[...])
    hp_ref[...] = _bf16_bits(h[:, :half]) | (_bf16_bits(h[:, half:]) << 16)
    h_hi = h.astype(BF16)
    h_lo = (h - h_hi.astype(F32)).astype(BF16)
    logits = (jnp.dot(h_hi, w_ref[0], preferred_element_type=F32)
              + (jnp.dot(h_hi, w_ref[1], preferred_element_type=F32)
                 + jnp.dot(h_lo, w_ref[0], preferred_element_type=F32)))
    lane =lax.broadcasted_iota(jnp.int32, logits.shape, 1).astype(F32)
    logits = jnp.where(lane < N_EXPERTS, logits, -jnp.inf)
    m1 = jnp.max(logits, axis=-1, keepdims=True)
    i1 = jnp.min(jnp.where(logits == m1, lane, float(ROUTER_LANES)), axis=-1, keepdims=True)
    rest = jnp.where(lane == i1, -jnp.inf, logits)
    m2 = jnp.max(rest, axis=-1, keepdims=True)
    i2 = jnp.min(jnp.where(rest == m2, lane, float(ROUTER_LANES)), axis=-1, keepdims=True)
    e2 = jnp.exp(m2 - m1)
    den = 1.0 + e2
    gate_ref[...] = jnp.where(lane == 0, 1.0 / den, e2 / den)
    chosen = jnp.where((lane == i1) | (lane == i2), 1.0, 0.0)
    earlier = lax.broadcasted_iota(jnp.int32, (tm, tm), 0) > lax.broadcasted_iota(jnp.int32, (tm, tm), 1)
    carry = carry_ref[...]
    before = jnp.dot(jnp.where(earlier, 1.0, 0.0).astype(BF16), chosen.astype(BF16),
                     preferred_element_type=F32) + carry
    r1 = jnp.sum(jnp.where(lane == i1, before, 0.0), axis=-1, keepdims=True)
    r2 = jnp.sum(jnp.where(lane == i2, before, 0.0), axis=-1, keepdims=True)
    idx_ref[...] = jnp.where(lane == 0, i1, jnp.where(lane == 1, i2, jnp.where(lane == 2, r1, r2))).astype(jnp.int32)
    carry = carry + jnp.sum(chosen, axis=0, keepdims=True)
    carry_ref[...] = carry
    count_ref[...] = carry.astype(jnp.int32)


def _router(x2, gain, router):
    rows, d = x2.shape
    tm = MOE_CHUNK
    w = jnp.zeros((d, ROUTER_LANES), F32).at[:, :N_EXPERTS].set(router.astype(F32))
    w_hi = w.astype(BF16)
    w = jnp.stack([w_hi, (w - w_hi.astype(F32)).astype(BF16)])
    row_spec = lambda width: pl.BlockSpec((tm, width), lambda i: (i, 0))
    return pl.pallas_call(
        _router_kernel,
        out_shape=(jax.ShapeDtypeStruct((rows, d // 2), jnp.uint32),
                   jax.ShapeDtypeStruct((rows, ROUTER_LANES), jnp.int32),
                   jax.ShapeDtypeStruct((rows, ROUTER_LANES), F32),
                   jax.ShapeDtypeStruct((1, ROUTER_LANES), jnp.int32)),
        grid=(rows // tm,),
        in_specs=[row_spec(d),
                  pl.BlockSpec((1, d), lambda i: (0, 0)),
                  pl.BlockSpec((2, d, ROUTER_LANES), lambda i: (0, 0, 0))],
        out_specs=(row_spec(d // 2), row_spec(ROUTER_LANES), row_spec(ROUTER_LANES),
                   pl.BlockSpec((1, ROUTER_LANES), lambda i: (0, 0))),
        scratch_shapes=[pltpu.VMEM((1, ROUTER_LANES), F32)],
        compiler_params=_params("arbitrary"),
        name="router",
    )(x2, gain.reshape(1, d), w)


def _sc_workers():
    info = plsc.get_sparse_core_info()
    mesh = plsc.VectorSubcoreMesh(core_axis_name="core", subcore_axis_name="subcore")
    return mesh, info.num_cores, info.num_subcores


def _pack_weight_rows(w, col_tile):
    r, c = w.shape
    mesh, nc, ns = _sc_workers()
    rs, ncol = SC_PACK_ROWS, c // col_tile
    per_w = (r // rs) * ncol // (nc * ns)
    assert per_w * nc * ns * rs * col_tile == r * c and per_w % 2 == 0 and col_tile % SC_LANES == 0
    params = pltpu.CompilerParams()
    if "needs_layout_passes" in pltpu.CompilerParams.__dataclass_fields__:
        params = dataclasses.replace(params, needs_layout_passes=False)

    @functools.partial(
        pl.kernel, mesh=mesh, out_type=jax.ShapeDtypeStruct((r // 2, c), jnp.uint32), compiler_params=params,
        cost_estimate=pl.CostEstimate(flops=r * c, transcendentals=0, bytes_accessed=6 * r * c),
        scratch_types=[pltpu.VMEM((2, rs, col_tile), F32), pltpu.VMEM((2, rs // 2, col_tile), jnp.uint32),
                       pltpu.SemaphoreType.DMA((2,)), pltpu.SemaphoreType.DMA((2,))])
    def pack(w_hbm, o_hbm, in_v, out_v, rsem, wsem):
        wid = lax.axis_index("subcore") * nc + lax.axis_index("core")

        @pl.loop(0, per_w // 2)
        def _(it):
            reads, writes = [], []
            for b in range(2):
                tile = wid * per_w + 2 * it + b
                r0 = pl.multiple_of((tile // ncol) * rs, rs)
                c0 = pl.multiple_of((tile % ncol) * col_tile, col_tile)
                reads.append(pltpu.make_async_copy(w_hbm.at[pl.ds(r0, rs), pl.ds(c0, col_tile)], in_v.at[b],
                                                   rsem.at[b]))
                writes.append(pltpu.make_async_copy(
                    out_v.at[b], o_hbm.at[pl.ds(pl.multiple_of(r0 // 2, rs // 2), rs // 2), pl.ds(c0, col_tile)],
                    wsem.at[b]))
            reads[0].start()
            reads[1].start()
            for b in range(2):
                reads[b].wait()
                for pair in range(rs // 2):
                    @plsc.parallel_loop(0, col_tile, step=SC_LANES, unroll=8)
                    def _(j):
                        packed = plsc.pack(in_v[b, 2 * pair, pl.ds(j, SC_LANES)],
                                           in_v[b, 2 * pair + 1, pl.ds(j, SC_LANES)],
                                           format=plsc.PackFormat.INTERLEAVED)
                        out_v[b, pair, pl.ds(j, SC_LANES)] = plsc.bitcast(packed, jnp.uint32)
                writes[b].start()
            writes[0].wait()
            writes[1].wait()

    return pack(w)


def _sc_token_rows(t, nc, ns):
    per_w = t // (nc * ns)
    assert per_w * nc * ns == t and per_w % (2 * SC_ROWS) == 0
    return per_w, per_w // SC_ROWS


def _scatter_rows(table, dests, p_rows, after=()):
    t, w = table.shape
    nk = len(dests)
    mesh, nc, ns = _sc_workers()
    per_w, nit = _sc_token_rows(t, nc, ns)
    ch = SC_ROWS

    @functools.partial(
        pl.kernel, mesh=mesh, out_type=jax.ShapeDtypeStruct((p_rows, w), table.dtype),
        scratch_types=[pltpu.VMEM((nk, nit, ch), jnp.int32), pltpu.VMEM((2, ch, w), table.dtype),
                       pltpu.SemaphoreType.DMA((2,)), pltpu.SemaphoreType.DMA((2, nk))])
    def scatter(table_hbm, *refs):
        dest_hbm, out_hbm = refs[:nk], refs[nk + len(after)]
        idx_v, rows_v, rsem, wsem = refs[nk + len(after) + 1:]
        wid = lax.axis_index("subcore") * nc + lax.axis_index("core")
        for k in range(nk):
            pltpu.sync_copy(dest_hbm[k].at[pl.ds(wid * nit, nit)], idx_v.at[k])

        @pl.loop(0, nit // 2)
        def _(it):
            reads = [pltpu.make_async_copy(table_hbm.at[pl.ds(wid * per_w + (2 * it + b) * ch, ch)],
                                           rows_v.at[b], rsem.at[b]) for b in range(2)]
            writes = [[pltpu.make_async_copy(rows_v.at[b], out_hbm.at[idx_v.at[k].at[2 * it + b]], wsem.at[b, k])
                       for k in range(nk)] for b in range(2)]
            reads[0].start()
            reads[1].start()
            for b in range(2):
                reads[b].wait()
                for k in range(nk):
                    writes[b][k].start()
            for b in range(2):
                for k in range(nk):
                    writes[b][k].wait()

    return scatter(table, *[d.reshape(t // ch, ch) for d in dests], *after)


def _gather_rows(table, idxs):
    t = idxs[0].shape[0]
    w = table.shape[1]
    nk = len(idxs)
    mesh, nc, ns = _sc_workers()
    per_w, nit = _sc_token_rows(t, nc, ns)
    ch = SC_ROWS

    @functools.partial(
        pl.kernel, mesh=mesh, out_type=[jax.ShapeDtypeStruct((t, w), table.dtype)] * nk,
        scratch_types=[pltpu.VMEM((nk, nit, ch), jnp.int32), pltpu.VMEM((nk, 2, ch, w), table.dtype),
                       pltpu.SemaphoreType.DMA((nk, 2)), pltpu.SemaphoreType.DMA((nk, 2))])
    def gather(table_hbm, *refs):
        idx_hbm, out_hbm = refs[:nk], refs[nk:2 * nk]
        idx_v, rows_v, rsem, wsem = refs[2 * nk:]
        wid = lax.axis_index("subcore") * nc + lax.axis_index("core")
        for k in range(nk):
            pltpu.sync_copy(idx_hbm[k].at[pl.ds(wid * nit, nit)], idx_v.at[k])

        @pl.loop(0, nit // 2)
        def _(it):
            slots = [(k, b) for k in range(nk) for b in range(2)]
            reads = {(k, b): pltpu.make_async_copy(table_hbm.at[idx_v.at[k].at[2 * it + b]], rows_v.at[k, b],
                                                   rsem.at[k, b]) for k, b in slots}
            writes = {(k, b): pltpu.make_async_copy(rows_v.at[k, b],
                                                    out_hbm[k].at[pl.ds(wid * per_w + (2 * it + b) * ch, ch)],
                                                    wsem.at[k, b]) for k, b in slots}
            for s in slots:
                reads[s].start()
            for s in slots:
                reads[s].wait()
                writes[s].start()
            for s in slots:
                writes[s].wait()

    return gather(table, *[i.reshape(t // ch, ch) for i in idxs])


def _combine_kernel(x_ref, y0_ref, y1_ref, gate_ref, o_ref):
    half = x_ref.shape[1] // 2
    for part in range(2):
        cols = slice(part * half, (part + 1) * half)
        acc = x_ref[:, cols]
        for k, y_ref in enumerate((y0_ref, y1_ref)):
            word = y_ref[...]
            bits = (word << 16) if part == 0 else (word & jnp.uint32(0xFFFF0000))
            acc = acc + gate_ref[:, k:k + 1] * lax.bitcast_convert_type(bits, F32)
        o_ref[:, cols] = acc


def _combine(x2, y0, y1, gates):
    t, d = x2.shape
    tm = ROW_TILE
    return pl.pallas_call(
        _combine_kernel,
        out_shape=jax.ShapeDtypeStruct((t, d), F32),
        grid=(t // tm,),
        in_specs=[pl.BlockSpec((tm, d), lambda i: (i, 0)),
                  pl.BlockSpec((tm, d // 2), lambda i: (i, 0)),
                  pl.BlockSpec((tm, d // 2), lambda i: (i, 0)),
                  pl.BlockSpec((tm, TOP_K), lambda i: (i, 0))],
        out_specs=pl.BlockSpec((tm, d), lambda i: (i, 0)),
        compiler_params=_params("parallel"),
        name="moe_combine",
    )(x2, y0, y1, gates)


def _moe(x2, gain, router, wg, wu, wd):
    t, d = x2.shape
    hp, idx, gate, count = _router(x2, gain, router)
    experts = jnp.arange(N_EXPERTS, dtype=jnp.int32)
    counts = count[0, :N_EXPERTS]
    padded = (counts + MOE_ROWS - 1) // MOE_ROWS * MOE_ROWS
    pend = jnp.cumsum(padded)
    pstart = pend - padded
    expert, rank = idx[:, :TOP_K], idx[:, TOP_K:2 * TOP_K]
    dest = jnp.sum(jnp.where(expert[:, :, None] == experts, pstart, 0), axis=-1) + rank
    dests = [dest[:, k] for k in range(TOP_K)]
    p_rows = t * TOP_K + N_EXPERTS * MOE_ROWS
    blk_row = jnp.arange(p_rows // MOE_ROWS, dtype=jnp.int32) * MOE_ROWS
    blk_expert = jnp.minimum(jnp.sum(pend[None, :] <= blk_row[:, None], axis=1), N_EXPERTS - 1).astype(jnp.int32)
    valid = jnp.clip(counts[blk_expert] - (blk_row - pstart[blk_expert]), 0, MOE_ROWS)
    valid = jnp.where(blk_row < pend[-1], valid, 0).astype(jnp.int32)
    n_exp, _, f = wg.shape
    pack = lambda w, col_tile: _pack_weight_rows(w.reshape(-1, w.shape[2]), col_tile).reshape(n_exp, -1, w.shape[2])
    wg, wu, wd = pack(wg, f // 2), pack(wu, f // 2), pack(wd, d)
    hb = _scatter_rows(hp, dests, p_rows, after=(wg, wu, wd))
    yb = _ffn_experts(hb, wg, wu, wd, blk_expert, valid)
    y0, y1 = _gather_rows(yb, dests)
    return _combine(x2, y0, y1, gate[:, :TOP_K])


def kernel(x, mem, rel_bias, norm_mix, norm_mem, norm_ffn, w_mem_kv, xq_norm, xk_norm, w_out, hy_w_in, hy_conv_w, hy_conv_b, hy_filt_w1, hy_filt_b1, hy_filt_w2, hy_filt_b2, hy_filt_w3, hy_sin_freq, hy_skip, at_w_in, at_q_norm, at_k_norm, ffn_w_gate, ffn_w_up, ffn_w_down, moe_router, moe_w_gate, moe_w_up, moe_w_down):
    b, s, d = x.shape
    t = b * s
    m_len = mem.shape[1]
    x2 = x.reshape(t, d)
    mem2 = mem.reshape(b * m_len, d)
    bf = lambda w: w.astype(BF16)

    tables = _dft_tables(s)
    spectra, hq = _hyena_filters(s, tables, hy_filt_w1[0], hy_filt_b1[0], hy_filt_w2[0], hy_filt_b2[0],
                                 hy_filt_w3[0], hy_sin_freq[0])
    z = _norm_matmul(x2, norm_mix[0], bf(hy_w_in[0]), 1024).reshape(b, s, -1)
    kv = _norm_matmul(mem2, norm_mem[0], bf(w_mem_kv[0]), 1024).reshape(b, m_len, -1)
    self_out = _hyena(z, hy_conv_w[0], hy_conv_b[0], tables, spectra, hq, hy_skip[0])
    cross = _xattn(z, 3 * HY_C // XA_W, kv, xq_norm[0], xk_norm[0])
    x2 = _outproj(x2, self_out.reshape(t, -1), cross.reshape(t, -1), bf(w_out[0]))
    x2 = _ffn_dense(x2, norm_ffn[0], bf(ffn_w_gate[0]), bf(ffn_w_up[0]), bf(ffn_w_down[0]))

    z = _norm_matmul(x2, norm_mix[1], bf(at_w_in[0]), 1024).reshape(b, s, -1)
    kv = _norm_matmul(mem2, norm_mem[1], bf(w_mem_kv[1]), 1024).reshape(b, m_len, -1)
    self_out = _dilated_attention(z, at_q_norm[0], at_k_norm[0], rel_bias)
    cross = _xattn(z, 3 * AT_W // XA_W, kv, xq_norm[1], xk_norm[1])
    x2 = _outproj(x2, self_out.reshape(t, -1), cross.reshape(t, -1), bf(w_out[1]))
    x2 = _moe(x2, norm_ffn[1], moe_router[0], moe_w_gate[0], moe_w_up[0], moe_w_down[0])
    return x2.reshape(b, s, d)
```

```python
import dataclasses
import functools
import math

import jax
import jax.numpy as jnp
import numpy as np
from jax import lax
from jax.experimental import pallas as pl
from jax.experimental.pallas import tpu as pltpu
from jax.experimental.pallas import tpu_sc as plsc

F32 = jnp.float32
BF16 = jnp.bfloat16

D_MODEL = 1024
EPS = 1e-6
HY_C = 512
FILT_BANDS = 16
DECAY_TARGET = 1e-2
FAST_DECAY_PCT = 0.3
SLOW_DECAY_PCT = 1.5
MOD_SHIFT = 0.05
AT_GROUPS = ((128, 1), (512, 4), (2048, 16))
AT_HEADS = 4
HD = 128
AT_W = 1536
NUM_BUCKETS = 32
REL_MAX_DIST = 1024
NEG_INF = -1e30
XA_W = 512
D_FF = 2816
N_EXPERTS = 8
TOP_K = 2

VMEM_LIMIT_BYTES = 56 * 1024 * 1024
ROW_TILE = 512
FF_TILE = 1408
MOE_ROWS = 512
MOE_CHUNK = 512
SC_ROWS = 32
SC_LANES = 16
SC_PACK_ROWS = 16
ATT_TQ = 128
ATT_KW = 256
HY_TC = 256
HY_KT = 512


def _params(*sem):
    return pltpu.CompilerParams(dimension_semantics=sem, vmem_limit_bytes=VMEM_LIMIT_BYTES)


def _rms_rows(x, gain):
    return x * lax.rsqrt(jnp.mean(x * x, axis=-1, keepdims=True) + EPS) * gain


def _bf16_bits(v):
    u = lax.bitcast_convert_type(v, jnp.uint32)
    return (u + jnp.uint32(0x7FFF) + ((u >> 16) & jnp.uint32(1))) >> 16


def _resident(shape, index_map):
    return pl.BlockSpec(shape, index_map, pipeline_mode=pl.Buffered(1))


def _norm_matmul_kernel(x_ref, g_ref, w_ref, hg_ref, o_ref, *, tn, head_norm):
    h = _rms_rows(x_ref[...], g_ref[...]).astype(BF16)
    for c in range(o_ref.shape[1] // tn):
        acc = jnp.dot(h, w_ref[:, c * tn:(c + 1) * tn], preferred_element_type=F32)
        for j in range(tn // HD):
            cols = slice(c * tn + j * HD, c * tn + (j + 1) * HD)
            seg = acc[:, j * HD:(j + 1) * HD]
            if head_norm[cols.start // HD]:
                seg = _rms_rows(seg, hg_ref[:, cols])
            o_ref[:, cols] = seg.astype(o_ref.dtype)


def _norm_matmul(x2, gain, w_bf, tn, head_gains):
    rows, d = x2.shape
    n = w_bf.shape[1]
    tm = min(ROW_TILE, rows)
    assert len(head_gains) * HD == n
    hg = jnp.concatenate([jnp.ones((HD,), F32) if g is None else g.astype(F32) for g in head_gains]).reshape(1, n)
    return pl.pallas_call(
        functools.partial(_norm_matmul_kernel, tn=tn, head_norm=tuple(g is not None for g in head_gains)),
        out_shape=jax.ShapeDtypeStruct((rows, n), BF16),
        grid=(rows // tm,),
        in_specs=[pl.BlockSpec((tm, d), lambda i: (i, 0)),
                  pl.BlockSpec((1, d), lambda i: (0, 0)),
                  _resident((d, n), lambda i: (0, 0)),
                  pl.BlockSpec((1, n), lambda i: (0, 0))],
        out_specs=pl.BlockSpec((tm, n), lambda i: (i, 0)),
        compiler_params=_params("parallel"),
        cost_estimate=pl.CostEstimate(flops=2 * rows * d * n, transcendentals=rows,
                                      bytes_accessed=4 * rows * d + 2 * d * n + 2 * rows * n),
        name="norm_matmul",
    )(x2, gain.reshape(1, d), w_bf, hg)


def _outproj_kernel(x_ref, a_ref, c_ref, wa_ref, wc_ref, o_ref):
    o_ref[...] = (x_ref[...]
                  + jnp.dot(a_ref[...].astype(BF16), wa_ref[...], preferred_element_type=F32)
                  + jnp.dot(c_ref[...].astype(BF16), wc_ref[...], preferred_element_type=F32))


def _outproj(x2, self_out, cross, w_bf):
    rows, d = x2.shape
    half = self_out.shape[1]
    tm = ROW_TILE
    return pl.pallas_call(
        _outproj_kernel,
        out_shape=jax.ShapeDtypeStruct((rows, d), F32),
        grid=(rows // tm,),
        in_specs=[pl.BlockSpec((tm, d), lambda i: (i, 0)),
                  pl.BlockSpec((tm, half), lambda i: (i, 0)),
                  pl.BlockSpec((tm, half), lambda i: (i, 0)),
                  pl.BlockSpec((half, d), lambda i: (0, 0)),
                  pl.BlockSpec((half, d), lambda i: (1, 0))],
        out_specs=pl.BlockSpec((tm, d), lambda i: (i, 0)),
        compiler_params=_params("parallel"),
        name="outproj",
    )(x2, self_out, cross, w_bf, w_bf)


def _weight_rows(ref, start, size):
    if ref.dtype == jnp.uint32:
        return lambda cols: pltpu.bitcast(ref[start // 2:(start + size) // 2, cols], BF16)
    return lambda cols: ref[start:start + size, cols]


def _weight_shape(ref):
    return (ref.shape[0] * (2 if ref.dtype == jnp.uint32 else 1), ref.shape[1])


def _swiglu(h, wg_ref, wu_ref, wd_ref, tf):
    d, f = _weight_shape(wg_ref)
    y = None
    for j in range(f // tf):
        cols = slice(j * tf, (j + 1) * tf)
        gg = jnp.dot(h, _weight_rows(wg_ref, 0, d)(cols), preferred_element_type=F32)
        uu = jnp.dot(h, _weight_rows(wu_ref, 0, d)(cols), preferred_element_type=F32)
        a = ((gg * jax.nn.sigmoid(gg)) * uu).astype(BF16)
        part = jnp.dot(a, _weight_rows(wd_ref, j * tf, tf)(slice(None)), preferred_element_type=F32)
        y = part if y is None else y + part
    return y


def _ffn_dense_kernel(x_ref, g_ref, wg_ref, wu_ref, wd_ref, o_ref, *, tf):
    x = x_ref[...]
    h = _rms_rows(x, g_ref[...]).astype(BF16)
    o_ref[...] = x + _swiglu(h, wg_ref, wu_ref, wd_ref, tf)


def _ffn_dense(x2, gain, wg, wu, wd):
    rows, d = x2.shape
    f = wg.shape[1]
    tm = ROW_TILE
    return pl.pallas_call(
        functools.partial(_ffn_dense_kernel, tf=FF_TILE),
        out_shape=jax.ShapeDtypeStruct((rows, d), F32),
        grid=(rows // tm,),
        in_specs=[pl.BlockSpec((tm, d), lambda i: (i, 0)),
                  pl.BlockSpec((1, d), lambda i: (0, 0)),
                  _resident((d, f), lambda i: (0, 0)),
                  _resident((d, f), lambda i: (0, 0)),
                  _resident((f, d), lambda i: (0, 0))],
        out_specs=pl.BlockSpec((tm, d), lambda i: (i, 0)),
        compiler_params=_params("parallel"),
        cost_estimate=pl.CostEstimate(flops=6 * rows * d * f, transcendentals=rows * f,
                                      bytes_accessed=8 * rows * d + 6 * d * f),
        name="ffn_dense",
    )(x2, gain.reshape(1, d), wg, wu, wd)


def _ffn_expert_kernel(eid_ref, valid_ref, hp_ref, wg_ref, wu_ref, wd_ref, o_ref, h_ref, *, tf):
    half = hp_ref.shape[1]
    valid = valid_ref[pl.program_id(0)]

    @pl.when(valid > 0)
    def _():
        keep = lax.broadcasted_iota(jnp.int32, (hp_ref.shape[0], 1), 0) < valid
        word = hp_ref[...]
        h_ref[:, :half] = jnp.where(keep, lax.bitcast_convert_type(word << 16, F32), 0.0).astype(BF16)
        h_ref[:, half:] = jnp.where(keep, lax.bitcast_convert_type(word & jnp.uint32(0xFFFF0000), F32),
                                    0.0).astype(BF16)
        y = _swiglu(h_ref[...], wg_ref.at[0], wu_ref.at[0], wd_ref.at[0], tf)
        o_ref[...] = _bf16_bits(y[:, :half]) | (_bf16_bits(y[:, half:]) << 16)

    @pl.when(valid <= 0)
    def _():
        o_ref[...] = jnp.zeros_like(o_ref)


def _ffn_experts(hp, wg, wu, wd, eid, valid):
    rows, half = hp.shape
    d = 2 * half
    tm = MOE_ROWS
    expert_spec = lambda w: pl.BlockSpec((1,) + w.shape[1:], lambda i, e, n: (e[i], 0, 0))
    grid_spec = pltpu.PrefetchScalarGridSpec(
        num_scalar_prefetch=2,
        grid=(rows // tm,),
        in_specs=[pl.BlockSpec((tm, half), lambda i, e, n: (i, 0)),
                  expert_spec(wg), expert_spec(wu), expert_spec(wd)],
        out_specs=pl.BlockSpec((tm, half), lambda i, e, n: (i, 0)),
        scratch_shapes=[pltpu.VMEM((tm, d), BF16)],
    )
    return pl.pallas_call(
        functools.partial(_ffn_expert_kernel, tf=FF_TILE),
        out_shape=jax.ShapeDtypeStruct((rows, half), jnp.uint32),
        grid_spec=grid_spec,
        compiler_params=_params("arbitrary"),
        name="ffn_experts",
    )(eid, valid, hp, wg, wu, wd)


def _xattn_kernel(xq_ref, kv_ref, o_ref):
    for h in range(XA_W // HD):
        cols = slice(h * HD, (h + 1) * HD)
        v = kv_ref[0, :, XA_W + h * HD:XA_W + (h + 1) * HD]
        s = lax.dot_general(xq_ref[0, :, cols], kv_ref[0, :, cols], (((1,), (1,)), ((), ())),
                            preferred_element_type=F32)
        p = jnp.exp(s - jnp.max(s, axis=-1, keepdims=True))
        z = jnp.sum(p, axis=-1, keepdims=True)
        o_ref[0, :, cols] = jnp.dot(p.astype(BF16), v, preferred_element_type=F32) / z


def _xattn(z3, xq_block, kv3):
    b, s, _ = z3.shape
    m = kv3.shape[1]
    ts = ROW_TILE
    return pl.pallas_call(
        _xattn_kernel,
        out_shape=jax.ShapeDtypeStruct((b, s, XA_W), F32),
        grid=(b, s // ts),
        in_specs=[pl.BlockSpec((1, ts, XA_W), lambda i, j: (i, j, xq_block)),
                  pl.BlockSpec((1, m, 2 * XA_W), lambda i, j: (i, 0, 0))],
        out_specs=pl.BlockSpec((1, ts, XA_W), lambda i, j: (i, j, 0)),
        compiler_params=_params("parallel", "parallel"),
        name="cross_attention",
    )(z3, kv3)


def _dft_kernel(ce_ref, se_ref, co_ref, so_ref, cot_ref, sot_ref, base_ref, *, n_fft):
    rows, cols = ce_ref.shape
    i = pl.program_id(0)
    theta = 2.0 * math.pi / n_fft

    def phases(r, c):
        return r * (2 * c), r * (2 * c + 1), c * (2 * r + 1)

    @pl.when(i == 0)
    def _():
        r = lax.broadcasted_iota(jnp.int32, (rows, cols), 0)
        c = lax.broadcasted_iota(jnp.int32, (rows, cols), 1)
        for f, ph in enumerate(phases(r, c)):
            ang = (ph & (n_fft - 1)).astype(F32) * theta
            base_ref[2 * f] = jnp.cos(ang)
            base_ref[2 * f + 1] = jnp.sin(ang)

    c = lax.broadcasted_iota(jnp.int32, (8, cols), 1)
    r0 = i * rows
    shifts = (r0 * (2 * c), r0 * (2 * c + 1), c * (2 * r0))
    for f, (c_ref, s_ref) in enumerate(((ce_ref, se_ref), (co_ref, so_ref), (cot_ref, sot_ref))):
        ang = (shifts[f] & (n_fft - 1)).astype(F32) * theta
        ca, sa = jnp.cos(ang)[0:1], jnp.sin(ang)[0:1]
        cb, sb = base_ref[2 * f], base_ref[2 * f + 1]
        c_ref[...] = (cb * ca - sb * sa).astype(BF16)
        s_ref[...] = (sb * ca + cb * sa).astype(BF16)


def _dft_tables(length):
    half, rows = length // 2, 128
    shape = jax.ShapeDtypeStruct((half, half), BF16)
    spec = pl.BlockSpec((rows, half), lambda i: (i, 0))
    return pl.pallas_call(
        functools.partial(_dft_kernel, n_fft=2 * length),
        out_shape=(shape,) * 6,
        grid=(half // rows,),
        out_specs=(spec,) * 6,
        scratch_shapes=[pltpu.VMEM((6, rows, half), F32)],
        compiler_params=_params("arbitrary"),
        name="dft_tables",
    )()


LANES = 128


def _split_parity(tmp_ref, x):
    half = x.shape[0] // 2
    for j in range(tmp_ref.shape[0]):
        tmp_ref[j] = x[:, j * LANES:(j + 1) * LANES]
    pick = lambda start: jnp.concatenate(
        [tmp_ref[j, pl.ds(start, half, stride=2), :] for j in range(tmp_ref.shape[0])], axis=1)
    return pick(0), pick(1)


def _merge_parity(tmp_ref, even, odd):
    half = even.shape[0]
    for j in range(tmp_ref.shape[0]):
        tmp_ref[j, pl.ds(0, half, stride=2), :] = even[:, j * LANES:(j + 1) * LANES]
        tmp_ref[j, pl.ds(1, half, stride=2), :] = odd[:, j * LANES:(j + 1) * LANES]
    return jnp.concatenate([tmp_ref[j] for j in range(tmp_ref.shape[0])], axis=1)


def _filter_time_kernel(feats_ref, w1_ref, b1_ref, w2_ref, b2_ref, fr_ref, w3_ref, t_ref, delta_ref,
                        g_ref, hq_ref, tmp_ref):
    hp = lax.Precision.HIGHEST
    length = feats_ref.shape[0]
    half = length // 2
    fr = fr_ref[...]
    h = jnp.sin(fr * (jnp.dot(feats_ref[...], w1_ref[...], preferred_element_type=F32, precision=hp) + b1_ref[...]))
    h = jnp.sin(fr * (jnp.dot(h, w2_ref[...], preferred_element_type=F32, precision=hp) + b2_ref[...]))
    mod = jnp.exp(-t_ref[...] * delta_ref[...]) + MOD_SHIFT
    row = lax.broadcasted_iota(jnp.int32, (length, 1), 0)
    alt = jnp.where(lax.broadcasted_iota(jnp.int32, (half, 1), 0) % 2 == 0, 1.0, -1.0).astype(F32)
    for o in range(2):
        fwd = jnp.dot(h, w3_ref[2 * o], preferred_element_type=F32, precision=hp) * mod
        bwd = jnp.dot(h, w3_ref[2 * o + 1], preferred_element_type=F32, precision=hp) * mod
        bwd = jnp.where(row == 0, 0.0, bwd)
        norm = (jnp.sum(jnp.abs(fwd), axis=0, keepdims=True)
                + jnp.sum(jnp.abs(bwd), axis=0, keepdims=True) + 1e-6)
        fwd = fwd / norm
        bwd = bwd / norm
        for part, g in enumerate((fwd + bwd, fwd - bwd)):
            even, odd = _split_parity(tmp_ref, g)
            g_ref[o, 2 * part] = even.astype(BF16)
            g_ref[o, 2 * part + 1] = odd.astype(BF16)
            mid = jnp.sum((even if part == 0 else odd) * alt, axis=0, keepdims=True) * (1.0 / length)
            hq_ref[o, part] = mid if part == 0 else -mid


def _filter_freq_kernel(ce_ref, se_ref, co_ref, so_ref, g_ref, h_ref):
    half = ce_ref.shape[0]
    row = lax.broadcasted_iota(jnp.int32, (half, 1), 0)
    scale = jnp.where(row == 0, 1.0, 2.0).astype(F32) * (1.0 / (4 * half))
    a = jnp.dot(ce_ref[...], g_ref[0, 0], preferred_element_type=F32)
    b = jnp.dot(co_ref[...], g_ref[0, 1], preferred_element_type=F32)
    c = jnp.dot(se_ref[...], g_ref[0, 2], preferred_element_type=F32)
    d = jnp.dot(so_ref[...], g_ref[0, 3], preferred_element_type=F32)
    h_ref[0, 0] = (a + b) * scale
    h_ref[0, 1] = -(c + d) * scale
    h_ref[0, 2] = (a - b) * scale
    h_ref[0, 3] = (c - d) * scale


def _hyena_filters(length, tables, w1, b1, w2, b2, w3, freq):
    t = jnp.linspace(0.0, 1.0, length, dtype=F32)[:, None]
    f = jnp.linspace(1e-4, FILT_BANDS - 1, FILT_BANDS, dtype=F32)[None]
    ang = (2.0 * math.pi / length) * jnp.arange(length, dtype=F32)[:, None] * f
    feats = jnp.concatenate([t, jnp.cos(ang), -jnp.sin(ang)], axis=-1)
    deltas = jnp.abs(jnp.linspace(math.log(DECAY_TARGET) / SLOW_DECAY_PCT,
                                  math.log(DECAY_TARGET) / FAST_DECAY_PCT, HY_C, dtype=F32))[None]
    hid = w1.shape[1]
    w3r = w3.reshape(hid, 4, HY_C).transpose(1, 0, 2)
    tc = HY_TC
    full = lambda shape: pl.BlockSpec(shape, lambda c: (0,) * len(shape))
    half = length // 2
    g, hq = pl.pallas_call(
        _filter_time_kernel,
        out_shape=(jax.ShapeDtypeStruct((2, 4, half, HY_C), BF16),
                   jax.ShapeDtypeStruct((2, 2, 1, HY_C), F32)),
        grid=(HY_C // tc,),
        in_specs=[full(feats.shape), full(w1.shape), full((1, hid)), full(w2.shape), full((1, hid)),
                  full((1, hid)), pl.BlockSpec((4, hid, tc), lambda c: (0, 0, c)), full((length, 1)),
                  pl.BlockSpec((1, tc), lambda c: (0, c))],
        out_specs=(pl.BlockSpec((2, 4, half, tc), lambda c: (0, 0, 0, c)),
                   pl.BlockSpec((2, 2, 1, tc), lambda c: (0, 0, 0, c))),
        scratch_shapes=[pltpu.VMEM((tc // LANES, length, LANES), F32)],
        compiler_params=_params("parallel"),
        name="filter_time",
    )(feats, w1, b1.reshape(1, hid), w2, b2.reshape(1, hid), freq.reshape(1, hid), w3r, t, deltas)
    ce, se, co, so = tables[:4]
    table_spec = _resident((half, half), lambda o, c: (0, 0))
    spectra = pl.pallas_call(
        _filter_freq_kernel,
        out_shape=jax.ShapeDtypeStruct((2, 4, half, HY_C), F32),
        grid=(2, HY_C // tc),
        in_specs=[table_spec] * 4 + [pl.BlockSpec((1, 4, half, tc), lambda o, c: (o, 0, 0, c))],
        out_specs=pl.BlockSpec((1, 4, half, tc), lambda o, c: (o, 0, 0, c)),
        compiler_params=_params("parallel", "parallel"),
        name="filter_freq",
    )(ce, se, co, so, g)
    return spectra, hq


def _hyena_kernel(z0_ref, z1_ref, z2_ref, cw_ref, cb_ref, ce_ref, se_ref, co_ref, so_ref, cot_ref, sot_ref,
                  h_ref, hq_ref, skip_ref, o_ref, tmp_ref):
    length = z0_ref.shape[1]
    half = length // 2
    row = lax.broadcasted_iota(jnp.int32, (length, 1), 0)
    alt = jnp.where(lax.broadcasted_iota(jnp.int32, (half, 1), 0) % 2 == 0, 1.0, -1.0).astype(F32)

    def short_conv(z_ref, c):
        u = z_ref[0].astype(F32)
        prev = jnp.where(row == 0, 0.0, pltpu.roll(u, 1, axis=0))
        nxt = jnp.where(row == length - 1, 0.0, pltpu.roll(u, length - 1, axis=0))
        return cb_ref[c] + (prev * cw_ref[0, c] + u * cw_ref[1, c] + nxt * cw_ref[2, c])

    def long_conv(u, o):
        ue, uo = _split_parity(tmp_ref, u)
        ue_bf, uo_bf = ue.astype(BF16), uo.astype(BF16)
        a4 = jnp.sum(ue * alt, axis=0, keepdims=True)
        b4 = jnp.sum(uo * alt, axis=0, keepdims=True)
        hr4, hi4 = hq_ref[o, 0], hq_ref[o, 1]
        y_even = alt * (a4 * hr4 + b4 * hi4)
        y_odd = -(alt * (a4 * hi4 - b4 * hr4))
        for kt in range(half // HY_KT):
            ks = slice(kt * HY_KT, (kt + 1) * HY_KT)
            ae = jnp.dot(ce_ref[ks, :], ue_bf, preferred_element_type=F32)
            ao = jnp.dot(co_ref[ks, :], uo_bf, preferred_element_type=F32)
            be = jnp.dot(se_ref[ks, :], ue_bf, preferred_element_type=F32)
            bo = jnp.dot(so_ref[ks, :], uo_bf, preferred_element_type=F32)
            a_f, a_g, b_f, b_g = ae + ao, ae - ao, be + bo, bo - be
            hr_f, hi_f, hr_g, hi_g = (h_ref[o, j, ks, :] for j in range(4))
            re_f, im_f = a_f * hr_f + b_f * hi_f, a_f * hi_f - b_f * hr_f
            re_g, im_g = a_g * hr_g + b_g * hi_g, a_g * hi_g - b_g * hr_g
            y_even += (jnp.dot(ce_ref[:, ks], (re_f + re_g).astype(BF16), preferred_element_type=F32)
                       - jnp.dot(se_ref[:, ks], (im_f - im_g).astype(BF16), preferred_element_type=F32))
            y_odd += (jnp.dot(cot_ref[:, ks], (re_f - re_g).astype(BF16), preferred_element_type=F32)
                      - jnp.dot(sot_ref[:, ks], (im_f + im_g).astype(BF16), preferred_element_type=F32))
        return _merge_parity(tmp_ref, y_even, y_odd) + u * skip_ref[o]

    z = short_conv(z1_ref, 1) * long_conv(short_conv(z0_ref, 0), 0)
    o_ref[0] = short_conv(z2_ref, 2) * long_conv(z, 1)


def _hyena(z3, conv_w, conv_b, tables, spectra, hq, skip):
    b, s, _ = z3.shape
    half = s // 2
    tc = HY_TC
    nct = HY_C // tc
    cw = conv_w.reshape(3, 3, 1, HY_C)
    cb = conv_b.reshape(3, 1, HY_C)
    zspec = lambda chunk: pl.BlockSpec((1, s, tc), lambda c, i: (i, 0, chunk * nct + c))
    return pl.pallas_call(
        _hyena_kernel,
        out_shape=jax.ShapeDtypeStruct((b, s, HY_C), F32),
        grid=(nct, b),
        in_specs=[zspec(0), zspec(1), zspec(2),
                  pl.BlockSpec((3, 3, 1, tc), lambda c, i: (0, 0, 0, c)),
                  pl.BlockSpec((3, 1, tc), lambda c, i: (0, 0, c))]
                 + [_resident((half, half), lambda c, i: (0, 0))] * 6
                 + [_resident((2, 4, half, tc), lambda c, i: (0, 0, 0, c)),
                    pl.BlockSpec((2, 2, 1, tc), lambda c, i: (0, 0, 0, c)),
                    pl.BlockSpec((2, 1, tc), lambda c, i: (0, 0, c))],
        out_specs=pl.BlockSpec((1, s, tc), lambda c, i: (i, 0, c)),
        scratch_shapes=[pltpu.VMEM((tc // LANES, s, LANES), F32)],
        compiler_params=_params("parallel", "parallel"),
        cost_estimate=pl.CostEstimate(flops=2 * 4 * 2 * b * s * half * HY_C, transcendentals=0,
                                      bytes_accessed=2 * 3 * b * s * HY_C + 4 * b * s * HY_C + 12 * half * half),
        name="hyena",
    )(z3, z3, z3, cw, cb, *tables, spectra, hq, skip.reshape(2, 1, HY_C))


def _rel_bucket(rel):
    half = NUM_BUCKETS // 2
    exact = half // 2
    n = np.abs(rel)
    large = exact + (np.log(np.maximum(n, 1) / exact) / np.log(REL_MAX_DIST / exact) * (half - exact)).astype(np.int32)
    large = np.minimum(large, half - 1)
    return (np.where(rel > 0, half, 0) + np.where(n < exact, n, large)).astype(np.int32)


def _att_tiles(length):
    kw = min(ATT_KW, length)
    tiles = []
    for qs in range(0, length, ATT_TQ):
        ks = min(max(qs - (kw - ATT_TQ) // 2, 0), length - kw)
        tiles.append((qs, ks, {0: 0, -64: 1, -128: 2}[ks - qs]))
    return kw, tiles


def _att_bias(rel_bias, group, seq):
    window, dil = AT_GROUPS[group]
    band = window // (2 * dil)
    kw, tiles = _att_tiles(seq // dil)
    offsets = sorted({ks - qs for qs, ks, _ in tiles}, reverse=True)
    table = rel_bias[:, group * AT_HEADS:(group + 1) * AT_HEADS].astype(F32)
    span = kw + ATT_TQ
    out = []
    for off in offsets:
        k = np.arange(span)
        rel = off + np.where(k < kw, k, k - span)
        diag = jnp.where(np.abs(rel) <= band, table[_rel_bucket(rel * dil)].T, NEG_INF)
        flat = jnp.tile(diag, (1, ATT_TQ))[:, :ATT_TQ * (span - 1)]
        out.append(flat.reshape(AT_HEADS, ATT_TQ, span - 1)[:, :, :kw])
    return jnp.stack(out)


def _dilated_kernel(q1, q2, q3, k1, k2, k3, v1, v2, v3, b1, b2, b3, o_ref,
                    qn_ref, kn_ref, vn_ref, og_ref, lg_ref):
    seq = o_ref.shape[1]
    contract_last = (((1,), (1,)), ((), ()))
    for g, (q_ref, k_ref, v_ref, bias_ref) in enumerate(((q1, k1, v1, b1), (q2, k2, v2, b2), (q3, k3, v3, b3))):
        dil = AT_GROUPS[g][1]
        kw, tiles = _att_tiles(seq // dil)
        qn_ref[...] = q_ref[0].astype(F32)
        kn_ref[...] = k_ref[0].astype(F32)
        vn_ref[...] = v_ref[0].astype(F32)
        for r in range(dil):
            for qs, ks, var in tiles:
                rows = lambda start, size: (pl.ds(r + start * dil, size, stride=dil) if dil > 1
                                            else pl.ds(start, size))
                qt = qn_ref[rows(qs, ATT_TQ), :].astype(BF16)
                kt = kn_ref[rows(ks, kw), :].astype(BF16)
                vt = vn_ref[rows(ks, kw), :].astype(BF16)
                s = lax.dot_general(qt, kt, contract_last, preferred_element_type=F32) + bias_ref[var, 0]
                m = jnp.max(s, axis=-1, keepdims=True)
                p = jnp.exp(s - m)
                z = jnp.sum(p, axis=-1, keepdims=True)
                og_ref[g, rows(qs, ATT_TQ), :] = jnp.dot(p.astype(BF16), vt, preferred_element_type=F32) / z
                lg_ref[g, rows(qs, ATT_TQ), :] = jnp.broadcast_to(m + jnp.log(z), (ATT_TQ, HD))
    l0, l1, l2 = lg_ref[0], lg_ref[1], lg_ref[2]
    mx = jnp.maximum(jnp.maximum(l0, l1), l2)
    e0, e1, e2 = jnp.exp(l0 - mx), jnp.exp(l1 - mx), jnp.exp(l2 - mx)
    o_ref[0] = (e0 * og_ref[0] + e1 * og_ref[1] + e2 * og_ref[2]) / (e0 + e1 + e2)


def _dilated_attention(z3, rel_bias):
    b, s, _ = z3.shape
    ng = len(AT_GROUPS)
    col = lambda part, g: pl.BlockSpec((1, s, HD), lambda i, h: (i, 0, part * ng * AT_HEADS + g * AT_HEADS + h))
    biases = [_att_bias(rel_bias, g, s) for g in range(ng)]
    bias_spec = lambda a: pl.BlockSpec((a.shape[0], 1) + a.shape[2:], lambda i, h: (0, h, 0, 0))
    return pl.pallas_call(
        _dilated_kernel,
        out_shape=jax.ShapeDtypeStruct((b, s, AT_HEADS * HD), F32),
        grid=(b, AT_HEADS),
        in_specs=[col(p, g) for p in range(3) for g in range(ng)] + [bias_spec(a) for a in biases],
        out_specs=pl.BlockSpec((1, s, HD), lambda i, h: (i, 0, h)),
        scratch_shapes=[pltpu.VMEM((s, HD), F32), pltpu.VMEM((s, HD), F32), pltpu.VMEM((s, HD), F32),
                        pltpu.VMEM((ng, s, HD), F32), pltpu.VMEM((ng, s, HD), F32)],
        compiler_params=_params("parallel", "parallel"),
        name="dilated_attention",
    )(*([z3] * 9), *biases)


ROUTER_LANES = 128


def _router_kernel(x_ref, g_ref, w_ref, hp_ref, idx_ref, gate_ref, count_ref, carry_ref):
    tm = x_ref.shape[0]
    half = hp_ref.shape[1]

    @pl.when(pl.program_id(0) == 0)
    def _():
        carry_ref[...] = jnp.zeros_like(carry_ref)

    h = _rms_rows(x_ref[...], g_ref[...])
    hp_ref[...] = _bf16_bits(h[:, :half]) | (_bf16_bits(h[:, half:]) << 16)
    h_hi = h.astype(BF16)
    h_lo = (h - h_hi.astype(F32)).astype(BF16)
    logits = (jnp.dot(h_hi, w_ref[0], preferred_element_type=F32)
              + (jnp.dot(h_hi, w_ref[1], preferred_element_type=F32)
                 + jnp.dot(h_lo, w_ref[0], preferred_element_type=F32)))
    lane =lax.broadcasted_iota(jnp.int32, logits.shape, 1).astype(F32)
    logits = jnp.where(lane < N_EXPERTS, logits, -jnp.inf)
    m1 = jnp.max(logits, axis=-1, keepdims=True)
    i1 = jnp.min(jnp.where(logits == m1, lane, float(ROUTER_LANES)), axis=-1, keepdims=True)
    rest = jnp.where(lane == i1, -jnp.inf, logits)
    m2 = jnp.max(rest, axis=-1, keepdims=True)
    i2 = jnp.min(jnp.where(rest == m2, lane, float(ROUTER_LANES)), axis=-1, keepdims=True)
    e2 = jnp.exp(m2 - m1)
    den = 1.0 + e2
    gate_ref[...] = jnp.where(lane == 0, 1.0 / den, e2 / den)
    chosen = jnp.where((lane == i1) | (lane == i2), 1.0, 0.0)
    earlier = lax.broadcasted_iota(jnp.int32, (tm, tm), 0) > lax.broadcasted_iota(jnp.int32, (tm, tm), 1)
    carry = carry_ref[...]
    before = jnp.dot(jnp.where(earlier, 1.0, 0.0).astype(BF16), chosen.astype(BF16),
                     preferred_element_type=F32) + carry
    r1 = jnp.sum(jnp.where(lane == i1, before, 0.0), axis=-1, keepdims=True)
    r2 = jnp.sum(jnp.where(lane == i2, before, 0.0), axis=-1, keepdims=True)
    idx_ref[...] = jnp.where(lane == 0, i1, jnp.where(lane == 1, i2, jnp.where(lane == 2, r1, r2))).astype(jnp.int32)
    carry = carry + jnp.sum(chosen, axis=0, keepdims=True)
    carry_ref[...] = carry
    count_ref[...] = carry.astype(jnp.int32)


def _router(x2, gain, router):
    rows, d = x2.shape
    tm = MOE_CHUNK
    w = jnp.zeros((d, ROUTER_LANES), F32).at[:, :N_EXPERTS].set(router.astype(F32))
    w_hi = w.astype(BF16)
    w = jnp.stack([w_hi, (w - w_hi.astype(F32)).astype(BF16)])
    row_spec = lambda width: pl.BlockSpec((tm, width), lambda i: (i, 0))
    return pl.pallas_call(
        _router_kernel,
        out_shape=(jax.ShapeDtypeStruct((rows, d // 2), jnp.uint32),
                   jax.ShapeDtypeStruct((rows, ROUTER_LANES), jnp.int32),
                   jax.ShapeDtypeStruct((rows, ROUTER_LANES), F32),
                   jax.ShapeDtypeStruct((1, ROUTER_LANES), jnp.int32)),
        grid=(rows // tm,),
        in_specs=[row_spec(d),
                  pl.BlockSpec((1, d), lambda i: (0, 0)),
                  pl.BlockSpec((2, d, ROUTER_LANES), lambda i: (0, 0, 0))],
        out_specs=(row_spec(d // 2), row_spec(ROUTER_LANES), row_spec(ROUTER_LANES),
                   pl.BlockSpec((1, ROUTER_LANES), lambda i: (0, 0))),
        scratch_shapes=[pltpu.VMEM((1, ROUTER_LANES), F32)],
        compiler_params=_params("arbitrary"),
        name="router",
    )(x2, gain.reshape(1, d), w)


def _sc_workers():
    info = plsc.get_sparse_core_info()
    mesh = plsc.VectorSubcoreMesh(core_axis_name="core", subcore_axis_name="subcore")
    return mesh, info.num_cores, info.num_subcores


def _pack_weight_rows(w, col_tile):
    r, c = w.shape
    mesh, nc, ns = _sc_workers()
    rs, ncol = SC_PACK_ROWS, c // col_tile
    per_w = (r // rs) * ncol // (nc * ns)
    assert per_w * nc * ns * rs * col_tile == r * c and per_w % 2 == 0 and col_tile % SC_LANES == 0
    params = pltpu.CompilerParams()
    if "needs_layout_passes" in pltpu.CompilerParams.__dataclass_fields__:
        params = dataclasses.replace(params, needs_layout_passes=False)

    @functools.partial(
        pl.kernel, mesh=mesh, out_type=jax.ShapeDtypeStruct((r // 2, c), jnp.uint32), compiler_params=params,
        cost_estimate=pl.CostEstimate(flops=r * c, transcendentals=0, bytes_accessed=6 * r * c),
        scratch_types=[pltpu.VMEM((2, rs, col_tile), F32), pltpu.VMEM((2, rs // 2, col_tile), jnp.uint32),
                       pltpu.SemaphoreType.DMA((2,)), pltpu.SemaphoreType.DMA((2,))])
    def pack(w_hbm, o_hbm, in_v, out_v, rsem, wsem):
        wid = lax.axis_index("subcore") * nc + lax.axis_index("core")

        @pl.loop(0, per_w // 2)
        def _(it):
            reads, writes = [], []
            for b in range(2):
                tile = wid * per_w + 2 * it + b
                r0 = pl.multiple_of((tile // ncol) * rs, rs)
                c0 = pl.multiple_of((tile % ncol) * col_tile, col_tile)
                reads.append(pltpu.make_async_copy(w_hbm.at[pl.ds(r0, rs), pl.ds(c0, col_tile)], in_v.at[b],
                                                   rsem.at[b]))
                writes.append(pltpu.make_async_copy(
                    out_v.at[b], o_hbm.at[pl.ds(pl.multiple_of(r0 // 2, rs // 2), rs // 2), pl.ds(c0, col_tile)],
                    wsem.at[b]))
            reads[0].start()
            reads[1].start()
            for b in range(2):
                reads[b].wait()
                for pair in range(rs // 2):
                    @plsc.parallel_loop(0, col_tile, step=SC_LANES, unroll=8)
                    def _(j):
                        packed = plsc.pack(in_v[b, 2 * pair, pl.ds(j, SC_LANES)],
                                           in_v[b, 2 * pair + 1, pl.ds(j, SC_LANES)],
                                           format=plsc.PackFormat.INTERLEAVED)
                        out_v[b, pair, pl.ds(j, SC_LANES)] = plsc.bitcast(packed, jnp.uint32)
                writes[b].start()
            writes[0].wait()
            writes[1].wait()

    return pack(w)


def _sc_token_rows(t, nc, ns):
    per_w = t // (nc * ns)
    assert per_w * nc * ns == t and per_w % (2 * SC_ROWS) == 0
    return per_w, per_w // SC_ROWS


def _scatter_rows(table, dests, p_rows, after=()):
    t, w = table.shape
    nk = len(dests)
    mesh, nc, ns = _sc_workers()
    per_w, nit = _sc_token_rows(t, nc, ns)
    ch = SC_ROWS

    @functools.partial(
        pl.kernel, mesh=mesh, out_type=jax.ShapeDtypeStruct((p_rows, w), table.dtype),
        scratch_types=[pltpu.VMEM((nk, nit, ch), jnp.int32), pltpu.VMEM((2, ch, w), table.dtype),
                       pltpu.SemaphoreType.DMA((2,)), pltpu.SemaphoreType.DMA((2, nk))])
    def scatter(table_hbm, *refs):
        dest_hbm, out_hbm = refs[:nk], refs[nk + len(after)]
        idx_v, rows_v, rsem, wsem = refs[nk + len(after) + 1:]
        wid = lax.axis_index("subcore") * nc + lax.axis_index("core")
        for k in range(nk):
            pltpu.sync_copy(dest_hbm[k].at[pl.ds(wid * nit, nit)], idx_v.at[k])

        @pl.loop(0, nit // 2)
        def _(it):
            reads = [pltpu.make_async_copy(table_hbm.at[pl.ds(wid * per_w + (2 * it + b) * ch, ch)],
                                           rows_v.at[b], rsem.at[b]) for b in range(2)]
            writes = [[pltpu.make_async_copy(rows_v.at[b], out_hbm.at[idx_v.at[k].at[2 * it + b]], wsem.at[b, k])
                       for k in range(nk)] for b in range(2)]
            reads[0].start()
            reads[1].start()
            for b in range(2):
                reads[b].wait()
                for k in range(nk):
                    writes[b][k].start()
            for b in range(2):
                for k in range(nk):
                    writes[b][k].wait()

    return scatter(table, *[d.reshape(t // ch, ch) for d in dests], *after)


def _gather_rows(table, idxs):
    t = idxs[0].shape[0]
    w = table.shape[1]
    nk = len(idxs)
    mesh, nc, ns = _sc_workers()
    per_w, nit = _sc_token_rows(t, nc, ns)
    ch = SC_ROWS

    @functools.partial(
        pl.kernel, mesh=mesh, out_type=[jax.ShapeDtypeStruct((t, w), table.dtype)] * nk,
        scratch_types=[pltpu.VMEM((nk, nit, ch), jnp.int32), pltpu.VMEM((nk, 2, ch, w), table.dtype),
                       pltpu.SemaphoreType.DMA((nk, 2)), pltpu.SemaphoreType.DMA((nk, 2))])
    def gather(table_hbm, *refs):
        idx_hbm, out_hbm = refs[:nk], refs[nk:2 * nk]
        idx_v, rows_v, rsem, wsem = refs[2 * nk:]
        wid = lax.axis_index("subcore") * nc + lax.axis_index("core")
        for k in range(nk):
            pltpu.sync_copy(idx_hbm[k].at[pl.ds(wid * nit, nit)], idx_v.at[k])

        @pl.loop(0, nit // 2)
        def _(it):
            slots = [(k, b) for k in range(nk) for b in range(2)]
            reads = {(k, b): pltpu.make_async_copy(table_hbm.at[idx_v.at[k].at[2 * it + b]], rows_v.at[k, b],
                                                   rsem.at[k, b]) for k, b in slots}
            writes = {(k, b): pltpu.make_async_copy(rows_v.at[k, b],
                                                    out_hbm[k].at[pl.ds(wid * per_w + (2 * it + b) * ch, ch)],
                                                    wsem.at[k, b]) for k, b in slots}
            for s in slots:
                reads[s].start()
            for s in slots:
                reads[s].wait()
                writes[s].start()
            for s in slots:
                writes[s].wait()

    return gather(table, *[i.reshape(t // ch, ch) for i in idxs])


def _combine_kernel(x_ref, y0_ref, y1_ref, gate_ref, o_ref):
    half = x_ref.shape[1] // 2
    for part in range(2):
        cols = slice(part * half, (part + 1) * half)
        acc = x_ref[:, cols]
        for k, y_ref in enumerate((y0_ref, y1_ref)):
            word = y_ref[...]
            bits = (word << 16) if part == 0 else (word & jnp.uint32(0xFFFF0000))
            acc = acc + gate_ref[:, k:k + 1] * lax.bitcast_convert_type(bits, F32)
        o_ref[:, cols] = acc


def _combine(x2, y0, y1, gates):
    t, d = x2.shape
    tm = ROW_TILE
    return pl.pallas_call(
        _combine_kernel,
        out_shape=jax.ShapeDtypeStruct((t, d), F32),
        grid=(t // tm,),
        in_specs=[pl.BlockSpec((tm, d), lambda i: (i, 0)),
                  pl.BlockSpec((tm, d // 2), lambda i: (i, 0)),
                  pl.BlockSpec((tm, d // 2), lambda i: (i, 0)),
                  pl.BlockSpec((tm, TOP_K), lambda i: (i, 0))],
        out_specs=pl.BlockSpec((tm, d), lambda i: (i, 0)),
        compiler_params=_params("parallel"),
        name="moe_combine",
    )(x2, y0, y1, gates)


def _moe(x2, gain, router, wg, wu, wd):
    t, d = x2.shape
    hp, idx, gate, count = _router(x2, gain, router)
    experts = jnp.arange(N_EXPERTS, dtype=jnp.int32)
    counts = count[0, :N_EXPERTS]
    padded = (counts + MOE_ROWS - 1) // MOE_ROWS * MOE_ROWS
    pend = jnp.cumsum(padded)
    pstart = pend - padded
    expert, rank = idx[:, :TOP_K], idx[:, TOP_K:2 * TOP_K]
    dest = jnp.sum(jnp.where(expert[:, :, None] == experts, pstart, 0), axis=-1) + rank
    dests = [dest[:, k] for k in range(TOP_K)]
    p_rows = t * TOP_K + N_EXPERTS * MOE_ROWS
    blk_row = jnp.arange(p_rows // MOE_ROWS, dtype=jnp.int32) * MOE_ROWS
    blk_expert = jnp.minimum(jnp.sum(pend[None, :] <= blk_row[:, None], axis=1), N_EXPERTS - 1).astype(jnp.int32)
    valid = jnp.clip(counts[blk_expert] - (blk_row - pstart[blk_expert]), 0, MOE_ROWS)
    valid = jnp.where(blk_row < pend[-1], valid, 0).astype(jnp.int32)
    n_exp, _, f = wg.shape
    pack = lambda w, col_tile: _pack_weight_rows(w.reshape(-1, w.shape[2]), col_tile).reshape(n_exp, -1, w.shape[2])
    wg, wu, wd = pack(wg, f // 2), pack(wu, f // 2), pack(wd, d)
    hb = _scatter_rows(hp, dests, p_rows, after=(wg, wu, wd))
    yb = _ffn_experts(hb, wg, wu, wd, blk_expert, valid)
    y0, y1 = _gather_rows(yb, dests)
    return _combine(x2, y0, y1, gate[:, :TOP_K])


def kernel(x, mem, rel_bias, norm_mix, norm_mem, norm_ffn, w_mem_kv, xq_norm, xk_norm, w_out, hy_w_in, hy_conv_w, hy_conv_b, hy_filt_w1, hy_filt_b1, hy_filt_w2, hy_filt_b2, hy_filt_w3, hy_sin_freq, hy_skip, at_w_in, at_q_norm, at_k_norm, ffn_w_gate, ffn_w_up, ffn_w_down, moe_router, moe_w_gate, moe_w_up, moe_w_down):
    b, s, d = x.shape
    t = b * s
    m_len = mem.shape[1]
    x2 = x.reshape(t, d)
    mem2 = mem.reshape(b * m_len, d)
    bf = lambda w: w.astype(BF16)
    score_scale = HD ** -0.5
    xa_heads = XA_W // HD
    at_heads = AT_W // HD
    plain = lambda width: [None] * (width // HD)
    kv_gains = lambda i: [xk_norm[i]] * xa_heads + plain(XA_W)

    tables = _dft_tables(s)
    spectra, hq = _hyena_filters(s, tables, hy_filt_w1[0], hy_filt_b1[0], hy_filt_w2[0], hy_filt_b2[0],
                                 hy_filt_w3[0], hy_sin_freq[0])
    z = _norm_matmul(x2, norm_mix[0], bf(hy_w_in[0]), 1024,
                     plain(3 * HY_C) + [xq_norm[0] * score_scale] * xa_heads).reshape(b, s, -1)
    kv = _norm_matmul(mem2, norm_mem[0], bf(w_mem_kv[0]), 1024, kv_gains(0)).reshape(b, m_len, -1)
    self_out = _hyena(z, hy_conv_w[0], hy_conv_b[0], tables, spectra, hq, hy_skip[0])
    cross = _xattn(z, 3 * HY_C // XA_W, kv)
    x2 = _outproj(x2, self_out.reshape(t, -1), cross.reshape(t, -1), bf(w_out[0]))
    x2 = _ffn_dense(x2, norm_ffn[0], bf(ffn_w_gate[0]), bf(ffn_w_up[0]), bf(ffn_w_down[0]))

    at_gains = ([at_q_norm[0] * score_scale] * at_heads + [at_k_norm[0]] * at_heads + plain(AT_W)
                + [xq_norm[1] * score_scale] * xa_heads)
    z = _norm_matmul(x2, norm_mix[1], bf(at_w_in[0]), 1024, at_gains).reshape(b, s, -1)
    kv = _norm_matmul(mem2, norm_mem[1], bf(w_mem_kv[1]), 1024, kv_gains(1)).reshape(b, m_len, -1)
    self_out = _dilated_attention(z, rel_bias)
    cross = _xattn(z, 3 * AT_W // XA_W, kv)
    x2 = _outproj(x2, self_out.reshape(t, -1), cross.reshape(t, -1), bf(w_out[1]))
    x2 = _moe(x2, norm_ffn[1], moe_router[0], moe_w_gate[0], moe_w_up[0], moe_w_down[0])
    return x2.reshape(b, s, d)
```

```python
import dataclasses
import functools
import math

import jax
import jax.numpy as jnp
import numpy as np
from jax import lax
from jax.experimental import pallas as pl
from jax.experimental.pallas import tpu as pltpu
from jax.experimental.pallas import tpu_sc as plsc

F32 = jnp.float32
BF16 = jnp.bfloat16

D_MODEL = 1024
EPS = 1e-6
HY_C = 512
FILT_BANDS = 16
DECAY_TARGET = 1e-2
FAST_DECAY_PCT = 0.3
SLOW_DECAY_PCT = 1.5
MOD_SHIFT = 0.05
AT_GROUPS = ((128, 1), (512, 4), (2048, 16))
AT_HEADS = 4
HD = 128
AT_W = 1536
NUM_BUCKETS = 32
REL_MAX_DIST = 1024
NEG_INF = -1e30
XA_W = 512
D_FF = 2816
N_EXPERTS = 8
TOP_K = 2

VMEM_LIMIT_BYTES = 56 * 1024 * 1024
ROW_TILE = 512
FF_TILE = 1408
MOE_ROWS = 512
MOE_CHUNK = 512
SC_ROWS = 32
SC_LANES = 16
SC_PACK_ROWS = 16
ATT_TQ = 128
ATT_KW = 256
HY_TC = 256
HY_KT = 512


def _params(*sem):
    return pltpu.CompilerParams(dimension_semantics=sem, vmem_limit_bytes=VMEM_LIMIT_BYTES)


def _rms_rows(x, gain):
    return x * lax.rsqrt(jnp.mean(x * x, axis=-1, keepdims=True) + EPS) * gain


def _bf16_bits(v):
    u = lax.bitcast_convert_type(v, jnp.uint32)
    return (u + jnp.uint32(0x7FFF) + ((u >> 16) & jnp.uint32(1))) >> 16


def _resident(shape, index_map):
    return pl.BlockSpec(shape, index_map, pipeline_mode=pl.Buffered(1))


def _norm_matmul_kernel(x_ref, g_ref, w_ref, hg_ref, o_ref, *, tn, head_norm):
    h = _rms_rows(x_ref[...], g_ref[...]).astype(BF16)
    for c in range(o_ref.shape[1] // tn):
        acc = jnp.dot(h, w_ref[:, c * tn:(c + 1) * tn], preferred_element_type=F32)
        for j in range(tn // HD):
            cols = slice(c * tn + j * HD, c * tn + (j + 1) * HD)
            seg = acc[:, j * HD:(j + 1) * HD]
            if head_norm[cols.start // HD]:
                seg = _rms_rows(seg, hg_ref[:, cols])
            o_ref[:, cols] = seg.astype(o_ref.dtype)


def _norm_matmul(x2, gain, w_bf, tn, head_gains):
    rows, d = x2.shape
    n = w_bf.shape[1]
    tm = min(ROW_TILE, rows)
    assert len(head_gains) * HD == n
    hg = jnp.concatenate([jnp.ones((HD,), F32) if g is None else g.astype(F32) for g in head_gains]).reshape(1, n)
    return pl.pallas_call(
        functools.partial(_norm_matmul_kernel, tn=tn, head_norm=tuple(g is not None for g in head_gains)),
        out_shape=jax.ShapeDtypeStruct((rows, n), BF16),
        grid=(rows // tm,),
        in_specs=[pl.BlockSpec((tm, d), lambda i: (i, 0)),
                  pl.BlockSpec((1, d), lambda i: (0, 0)),
                  _resident((d, n), lambda i: (0, 0)),
                  pl.BlockSpec((1, n), lambda i: (0, 0))],
        out_specs=pl.BlockSpec((tm, n), lambda i: (i, 0)),
        compiler_params=_params("parallel"),
        cost_estimate=pl.CostEstimate(flops=2 * rows * d * n, transcendentals=rows,
                                      bytes_accessed=4 * rows * d + 2 * d * n + 2 * rows * n),
        name="norm_matmul",
    )(x2, gain.reshape(1, d), w_bf, hg)


def _mix_out_kernel(x_ref, a_ref, xq_ref, kv_ref, wa_ref, wc_ref, o_ref):
    heads = []
    for h in range(XA_W // HD):
        cols = slice(h * HD, (h + 1) * HD)
        s = lax.dot_general(xq_ref[0, :, cols], kv_ref[0, :, cols], (((1,), (1,)), ((), ())),
                            preferred_element_type=F32)
        p = jnp.exp(s - jnp.max(s, axis=-1, keepdims=True))
        z = jnp.sum(p, axis=-1, keepdims=True)
        pv = jnp.dot(p.astype(BF16), kv_ref[0, :, XA_W + h * HD:XA_W + (h + 1) * HD], preferred_element_type=F32)
        heads.append((pv / z).astype(BF16))
    cross = jnp.concatenate(heads, axis=1)
    o_ref[0] = (x_ref[0]
                + jnp.dot(a_ref[0], wa_ref[...], preferred_element_type=F32)
                + jnp.dot(cross, wc_ref[...], preferred_element_type=F32))


def _mix_out(x3, self_out, z3, xq_block, kv3, w_bf):
    b, s, d = x3.shape
    half = self_out.shape[2]
    m = kv3.shape[1]
    tm = ROW_TILE
    return pl.pallas_call(
        _mix_out_kernel,
        out_shape=jax.ShapeDtypeStruct((b, s, d), F32),
        grid=(b, s // tm),
        in_specs=[pl.BlockSpec((1, tm, d), lambda i, j: (i, j, 0)),
                  pl.BlockSpec((1, tm, half), lambda i, j: (i, j, 0)),
                  pl.BlockSpec((1, tm, XA_W), lambda i, j: (i, j, xq_block)),
                  pl.BlockSpec((1, m, 2 * XA_W), lambda i, j: (i, 0, 0)),
                  pl.BlockSpec((half, d), lambda i, j: (0, 0)),
                  pl.BlockSpec((d - half, d), lambda i, j: (1, 0))],
        out_specs=pl.BlockSpec((1, tm, d), lambda i, j: (i, j, 0)),
        compiler_params=_params("parallel", "parallel"),
        name="mix_out",
    )(x3, self_out, z3, kv3, w_bf, w_bf)


def _weight_rows(ref, start, size):
    if ref.dtype == jnp.uint32:
        return lambda cols: pltpu.bitcast(ref[start // 2:(start + size) // 2, cols], BF16)
    return lambda cols: ref[start:start + size, cols]


def _weight_shape(ref):
    return (ref.shape[0] * (2 if ref.dtype == jnp.uint32 else 1), ref.shape[1])


def _swiglu(h, wg_ref, wu_ref, wd_ref, tf):
    d, f = _weight_shape(wg_ref)
    y = None
    for j in range(f // tf):
        cols = slice(j * tf, (j + 1) * tf)
        gg = jnp.dot(h, _weight_rows(wg_ref, 0, d)(cols), preferred_element_type=F32)
        uu = jnp.dot(h, _weight_rows(wu_ref, 0, d)(cols), preferred_element_type=F32)
        a = ((gg * jax.nn.sigmoid(gg)) * uu).astype(BF16)
        part = jnp.dot(a, _weight_rows(wd_ref, j * tf, tf)(slice(None)), preferred_element_type=F32)
        y = part if y is None else y + part
    return y


def _ffn_dense_kernel(x_ref, g_ref, wg_ref, wu_ref, wd_ref, o_ref, *, tf):
    x = x_ref[...]
    h = _rms_rows(x, g_ref[...]).astype(BF16)
    o_ref[...] = x + _swiglu(h, wg_ref, wu_ref, wd_ref, tf)


def _ffn_dense(x2, gain, wg, wu, wd):
    rows, d = x2.shape
    f = wg.shape[1]
    tm = ROW_TILE
    return pl.pallas_call(
        functools.partial(_ffn_dense_kernel, tf=FF_TILE),
        out_shape=jax.ShapeDtypeStruct((rows, d), F32),
        grid=(rows // tm,),
        in_specs=[pl.BlockSpec((tm, d), lambda i: (i, 0)),
                  pl.BlockSpec((1, d), lambda i: (0, 0)),
                  _resident((d, f), lambda i: (0, 0)),
                  _resident((d, f), lambda i: (0, 0)),
                  _resident((f, d), lambda i: (0, 0))],
        out_specs=pl.BlockSpec((tm, d), lambda i: (i, 0)),
        compiler_params=_params("parallel"),
        cost_estimate=pl.CostEstimate(flops=6 * rows * d * f, transcendentals=rows * f,
                                      bytes_accessed=8 * rows * d + 6 * d * f),
        name="ffn_dense",
    )(x2, gain.reshape(1, d), wg, wu, wd)


def _ffn_expert_kernel(eid_ref, valid_ref, hp_ref, wg_ref, wu_ref, wd_ref, o_ref, h_ref, *, tf):
    half = hp_ref.shape[1]
    valid = valid_ref[pl.program_id(0)]

    @pl.when(valid > 0)
    def _():
        keep = lax.broadcasted_iota(jnp.int32, (hp_ref.shape[0], 1), 0) < valid
        word = hp_ref[...]
        h_ref[:, :half] = jnp.where(keep, lax.bitcast_convert_type(word << 16, F32), 0.0).astype(BF16)
        h_ref[:, half:] = jnp.where(keep, lax.bitcast_convert_type(word & jnp.uint32(0xFFFF0000), F32),
                                    0.0).astype(BF16)
        y = _swiglu(h_ref[...], wg_ref.at[0], wu_ref.at[0], wd_ref.at[0], tf)
        o_ref[...] = _bf16_bits(y[:, :half]) | (_bf16_bits(y[:, half:]) << 16)

    @pl.when(valid <= 0)
    def _():
        o_ref[...] = jnp.zeros_like(o_ref)


def _ffn_experts(hp, wg, wu, wd, eid, valid):
    rows, half = hp.shape
    d = 2 * half
    tm = MOE_ROWS
    expert_spec = lambda w: pl.BlockSpec((1,) + w.shape[1:], lambda i, e, n: (e[i], 0, 0))
    grid_spec = pltpu.PrefetchScalarGridSpec(
        num_scalar_prefetch=2,
        grid=(rows // tm,),
        in_specs=[pl.BlockSpec((tm, half), lambda i, e, n: (i, 0)),
                  expert_spec(wg), expert_spec(wu), expert_spec(wd)],
        out_specs=pl.BlockSpec((tm, half), lambda i, e, n: (i, 0)),
        scratch_shapes=[pltpu.VMEM((tm, d), BF16)],
    )
    return pl.pallas_call(
        functools.partial(_ffn_expert_kernel, tf=FF_TILE),
        out_shape=jax.ShapeDtypeStruct((rows, half), jnp.uint32),
        grid_spec=grid_spec,
        compiler_params=_params("arbitrary"),
        name="ffn_experts",
    )(eid, valid, hp, wg, wu, wd)


def _dft_kernel(ce_ref, se_ref, co_ref, so_ref, cot_ref, sot_ref, base_ref, *, n_fft):
    rows, cols = ce_ref.shape
    i = pl.program_id(0)
    theta = 2.0 * math.pi / n_fft

    def phases(r, c):
        return r * (2 * c), r * (2 * c + 1), c * (2 * r + 1)

    @pl.when(i == 0)
    def _():
        r = lax.broadcasted_iota(jnp.int32, (rows, cols), 0)
        c = lax.broadcasted_iota(jnp.int32, (rows, cols), 1)
        for f, ph in enumerate(phases(r, c)):
            ang = (ph & (n_fft - 1)).astype(F32) * theta
            base_ref[2 * f] = jnp.cos(ang)
            base_ref[2 * f + 1] = jnp.sin(ang)

    c = lax.broadcasted_iota(jnp.int32, (8, cols), 1)
    r0 = i * rows
    shifts = (r0 * (2 * c), r0 * (2 * c + 1), c * (2 * r0))
    for f, (c_ref, s_ref) in enumerate(((ce_ref, se_ref), (co_ref, so_ref), (cot_ref, sot_ref))):
        ang = (shifts[f] & (n_fft - 1)).astype(F32) * theta
        ca, sa = jnp.cos(ang)[0:1], jnp.sin(ang)[0:1]
        cb, sb = base_ref[2 * f], base_ref[2 * f + 1]
        c_ref[...] = (cb * ca - sb * sa).astype(BF16)
        s_ref[...] = (sb * ca + cb * sa).astype(BF16)


def _dft_tables(length):
    half, rows = length // 2, 128
    shape = jax.ShapeDtypeStruct((half, half), BF16)
    spec = pl.BlockSpec((rows, half), lambda i: (i, 0))
    return pl.pallas_call(
        functools.partial(_dft_kernel, n_fft=2 * length),
        out_shape=(shape,) * 6,
        grid=(half // rows,),
        out_specs=(spec,) * 6,
        scratch_shapes=[pltpu.VMEM((6, rows, half), F32)],
        compiler_params=_params("arbitrary"),
        name="dft_tables",
    )()


LANES = 128


def _split_parity(tmp_ref, x):
    half = x.shape[0] // 2
    for j in range(tmp_ref.shape[0]):
        tmp_ref[j] = x[:, j * LANES:(j + 1) * LANES]
    pick = lambda start: jnp.concatenate(
        [tmp_ref[j, pl.ds(start, half, stride=2), :] for j in range(tmp_ref.shape[0])], axis=1)
    return pick(0), pick(1)


def _merge_parity(tmp_ref, even, odd):
    half = even.shape[0]
    for j in range(tmp_ref.shape[0]):
        tmp_ref[j, pl.ds(0, half, stride=2), :] = even[:, j * LANES:(j + 1) * LANES]
        tmp_ref[j, pl.ds(1, half, stride=2), :] = odd[:, j * LANES:(j + 1) * LANES]
    return jnp.concatenate([tmp_ref[j] for j in range(tmp_ref.shape[0])], axis=1)


def _filter_time_kernel(feats_ref, w1_ref, b1_ref, w2_ref, b2_ref, fr_ref, w3_ref, t_ref, delta_ref,
                        g_ref, hq_ref, tmp_ref):
    hp = lax.Precision.HIGHEST
    length = feats_ref.shape[0]
    half = length // 2
    fr = fr_ref[...]
    h = jnp.sin(fr * (jnp.dot(feats_ref[...], w1_ref[...], preferred_element_type=F32, precision=hp) + b1_ref[...]))
    h = jnp.sin(fr * (jnp.dot(h, w2_ref[...], preferred_element_type=F32, precision=hp) + b2_ref[...]))
    mod = jnp.exp(-t_ref[...] * delta_ref[...]) + MOD_SHIFT
    row = lax.broadcasted_iota(jnp.int32, (length, 1), 0)
    alt = jnp.where(lax.broadcasted_iota(jnp.int32, (half, 1), 0) % 2 == 0, 1.0, -1.0).astype(F32)
    for o in range(2):
        fwd = jnp.dot(h, w3_ref[2 * o], preferred_element_type=F32, precision=hp) * mod
        bwd = jnp.dot(h, w3_ref[2 * o + 1], preferred_element_type=F32, precision=hp) * mod
        bwd = jnp.where(row == 0, 0.0, bwd)
        norm = (jnp.sum(jnp.abs(fwd), axis=0, keepdims=True)
                + jnp.sum(jnp.abs(bwd), axis=0, keepdims=True) + 1e-6)
        fwd = fwd / norm
        bwd = bwd / norm
        for part, g in enumerate((fwd + bwd, fwd - bwd)):
            even, odd = _split_parity(tmp_ref, g)
            g_ref[o, 2 * part] = even.astype(BF16)
            g_ref[o, 2 * part + 1] = odd.astype(BF16)
            mid = jnp.sum((even if part == 0 else odd) * alt, axis=0, keepdims=True) * (1.0 / length)
            hq_ref[o, part] = mid if part == 0 else -mid


def _filter_freq_kernel(ce_ref, se_ref, co_ref, so_ref, g_ref, h_ref):
    half = ce_ref.shape[0]
    row = lax.broadcasted_iota(jnp.int32, (half, 1), 0)
    scale = jnp.where(row == 0, 1.0, 2.0).astype(F32) * (1.0 / (4 * half))
    a = jnp.dot(ce_ref[...], g_ref[0, 0], preferred_element_type=F32)
    b = jnp.dot(co_ref[...], g_ref[0, 1], preferred_element_type=F32)
    c = jnp.dot(se_ref[...], g_ref[0, 2], preferred_element_type=F32)
    d = jnp.dot(so_ref[...], g_ref[0, 3], preferred_element_type=F32)
    h_ref[0, 0] = (a + b) * scale
    h_ref[0, 1] = -(c + d) * scale
    h_ref[0, 2] = (a - b) * scale
    h_ref[0, 3] = (c - d) * scale


def _hyena_filters(length, tables, w1, b1, w2, b2, w3, freq):
    t = jnp.linspace(0.0, 1.0, length, dtype=F32)[:, None]
    f = jnp.linspace(1e-4, FILT_BANDS - 1, FILT_BANDS, dtype=F32)[None]
    ang = (2.0 * math.pi / length) * jnp.arange(length, dtype=F32)[:, None] * f
    feats = jnp.concatenate([t, jnp.cos(ang), -jnp.sin(ang)], axis=-1)
    deltas = jnp.abs(jnp.linspace(math.log(DECAY_TARGET) / SLOW_DECAY_PCT,
                                  math.log(DECAY_TARGET) / FAST_DECAY_PCT, HY_C, dtype=F32))[None]
    hid = w1.shape[1]
    w3r = w3.reshape(hid, 4, HY_C).transpose(1, 0, 2)
    tc = HY_TC
    full = lambda shape: pl.BlockSpec(shape, lambda c: (0,) * len(shape))
    half = length // 2
    g, hq = pl.pallas_call(
        _filter_time_kernel,
        out_shape=(jax.ShapeDtypeStruct((2, 4, half, HY_C), BF16),
                   jax.ShapeDtypeStruct((2, 2, 1, HY_C), F32)),
        grid=(HY_C // tc,),
        in_specs=[full(feats.shape), full(w1.shape), full((1, hid)), full(w2.shape), full((1, hid)),
                  full((1, hid)), pl.BlockSpec((4, hid, tc), lambda c: (0, 0, c)), full((length, 1)),
                  pl.BlockSpec((1, tc), lambda c: (0, c))],
        out_specs=(pl.BlockSpec((2, 4, half, tc), lambda c: (0, 0, 0, c)),
                   pl.BlockSpec((2, 2, 1, tc), lambda c: (0, 0, 0, c))),
        scratch_shapes=[pltpu.VMEM((tc // LANES, length, LANES), F32)],
        compiler_params=_params("parallel"),
        name="filter_time",
    )(feats, w1, b1.reshape(1, hid), w2, b2.reshape(1, hid), freq.reshape(1, hid), w3r, t, deltas)
    ce, se, co, so = tables[:4]
    table_spec = _resident((half, half), lambda o, c: (0, 0))
    spectra = pl.pallas_call(
        _filter_freq_kernel,
        out_shape=jax.ShapeDtypeStruct((2, 4, half, HY_C), F32),
        grid=(2, HY_C // tc),
        in_specs=[table_spec] * 4 + [pl.BlockSpec((1, 4, half, tc), lambda o, c: (o, 0, 0, c))],
        out_specs=pl.BlockSpec((1, 4, half, tc), lambda o, c: (o, 0, 0, c)),
        compiler_params=_params("parallel", "parallel"),
        name="filter_freq",
    )(ce, se, co, so, g)
    return spectra, hq


def _hyena_kernel(z0_ref, z1_ref, z2_ref, cw_ref, cb_ref, ce_ref, se_ref, co_ref, so_ref, cot_ref, sot_ref,
                  h_ref, hq_ref, skip_ref, o_ref, tmp_ref):
    length = z0_ref.shape[1]
    half = length // 2
    row = lax.broadcasted_iota(jnp.int32, (length, 1), 0)
    alt = jnp.where(lax.broadcasted_iota(jnp.int32, (half, 1), 0) % 2 == 0, 1.0, -1.0).astype(F32)

    def short_conv(z_ref, c):
        u = z_ref[0].astype(F32)
        prev = jnp.where(row == 0, 0.0, pltpu.roll(u, 1, axis=0))
        nxt = jnp.where(row == length - 1, 0.0, pltpu.roll(u, length - 1, axis=0))
        return cb_ref[c] + (prev * cw_ref[0, c] + u * cw_ref[1, c] + nxt * cw_ref[2, c])

    def long_conv(u, o):
        ue, uo = _split_parity(tmp_ref, u)
        ue_bf, uo_bf = ue.astype(BF16), uo.astype(BF16)
        a4 = jnp.sum(ue * alt, axis=0, keepdims=True)
        b4 = jnp.sum(uo * alt, axis=0, keepdims=True)
        hr4, hi4 = hq_ref[o, 0], hq_ref[o, 1]
        y_even = alt * (a4 * hr4 + b4 * hi4)
        y_odd = -(alt * (a4 * hi4 - b4 * hr4))
        for kt in range(half // HY_KT):
            ks = slice(kt * HY_KT, (kt + 1) * HY_KT)
            ae = jnp.dot(ce_ref[ks, :], ue_bf, preferred_element_type=F32)
            ao = jnp.dot(co_ref[ks, :], uo_bf, preferred_element_type=F32)
            be = jnp.dot(se_ref[ks, :], ue_bf, preferred_element_type=F32)
            bo = jnp.dot(so_ref[ks, :], uo_bf, preferred_element_type=F32)
            a_f, a_g, b_f, b_g = ae + ao, ae - ao, be + bo, bo - be
            hr_f, hi_f, hr_g, hi_g = (h_ref[o, j, ks, :] for j in range(4))
            re_f, im_f = a_f * hr_f + b_f * hi_f, a_f * hi_f - b_f * hr_f
            re_g, im_g = a_g * hr_g + b_g * hi_g, a_g * hi_g - b_g * hr_g
            y_even += (jnp.dot(ce_ref[:, ks], (re_f + re_g).astype(BF16), preferred_element_type=F32)
                       - jnp.dot(se_ref[:, ks], (im_f - im_g).astype(BF16), preferred_element_type=F32))
            y_odd += (jnp.dot(cot_ref[:, ks], (re_f - re_g).astype(BF16), preferred_element_type=F32)
                      - jnp.dot(sot_ref[:, ks], (im_f + im_g).astype(BF16), preferred_element_type=F32))
        return _merge_parity(tmp_ref, y_even, y_odd) + u * skip_ref[o]

    z = short_conv(z1_ref, 1) * long_conv(short_conv(z0_ref, 0), 0)
    o_ref[0] = (short_conv(z2_ref, 2) * long_conv(z, 1)).astype(o_ref.dtype)


def _hyena(z3, conv_w, conv_b, tables, spectra, hq, skip):
    b, s, _ = z3.shape
    half = s // 2
    tc = HY_TC
    nct = HY_C // tc
    cw = conv_w.reshape(3, 3, 1, HY_C)
    cb = conv_b.reshape(3, 1, HY_C)
    zspec = lambda chunk: pl.BlockSpec((1, s, tc), lambda c, i: (i, 0, chunk * nct + c))
    return pl.pallas_call(
        _hyena_kernel,
        out_shape=jax.ShapeDtypeStruct((b, s, HY_C), BF16),
        grid=(nct, b),
        in_specs=[zspec(0), zspec(1), zspec(2),
                  pl.BlockSpec((3, 3, 1, tc), lambda c, i: (0, 0, 0, c)),
                  pl.BlockSpec((3, 1, tc), lambda c, i: (0, 0, c))]
                 + [_resident((half, half), lambda c, i: (0, 0))] * 6
                 + [_resident((2, 4, half, tc), lambda c, i: (0, 0, 0, c)),
                    pl.BlockSpec((2, 2, 1, tc), lambda c, i: (0, 0, 0, c)),
                    pl.BlockSpec((2, 1, tc), lambda c, i: (0, 0, c))],
        out_specs=pl.BlockSpec((1, s, tc), lambda c, i: (i, 0, c)),
        scratch_shapes=[pltpu.VMEM((tc // LANES, s, LANES), F32)],
        compiler_params=_params("parallel", "parallel"),
        cost_estimate=pl.CostEstimate(flops=2 * 4 * 2 * b * s * half * HY_C, transcendentals=0,
                                      bytes_accessed=2 * 3 * b * s * HY_C + 4 * b * s * HY_C + 12 * half * half),
        name="hyena",
    )(z3, z3, z3, cw, cb, *tables, spectra, hq, skip.reshape(2, 1, HY_C))


def _rel_bucket(rel):
    half = NUM_BUCKETS // 2
    exact = half // 2
    n = np.abs(rel)
    large = exact + (np.log(np.maximum(n, 1) / exact) / np.log(REL_MAX_DIST / exact) * (half - exact)).astype(np.int32)
    large = np.minimum(large, half - 1)
    return (np.where(rel > 0, half, 0) + np.where(n < exact, n, large)).astype(np.int32)


def _att_tiles(length):
    kw = min(ATT_KW, length)
    tiles = []
    for qs in range(0, length, ATT_TQ):
        ks = min(max(qs - (kw - ATT_TQ) // 2, 0), length - kw)
        tiles.append((qs, ks, {0: 0, -64: 1, -128: 2}[ks - qs]))
    return kw, tiles


def _att_bias(rel_bias, group, seq):
    window, dil = AT_GROUPS[group]
    band = window // (2 * dil)
    kw, tiles = _att_tiles(seq // dil)
    offsets = sorted({ks - qs for qs, ks, _ in tiles}, reverse=True)
    table = rel_bias[:, group * AT_HEADS:(group + 1) * AT_HEADS].astype(F32)
    span = kw + ATT_TQ
    k = np.arange(span)
    rel = np.asarray(offsets)[:, None] + np.where(k < kw, k, k - span)[None, :]
    diag = jnp.where((np.abs(rel) <= band)[:, None, :],
                     jnp.swapaxes(table[_rel_bucket(rel * dil)], 1, 2), NEG_INF)
    flat = jnp.tile(diag, (1, 1, ATT_TQ))[:, :, :ATT_TQ * (span - 1)]
    return flat.reshape(len(offsets), AT_HEADS, ATT_TQ, span - 1)[:, :, :, :kw]


def _dilated_kernel(q1, q2, q3, k1, k2, k3, v1, v2, v3, b1, b2, b3, o_ref,
                    qn_ref, kn_ref, vn_ref, og_ref, lg_ref):
    seq = o_ref.shape[1]
    contract_last = (((1,), (1,)), ((), ()))
    for g, (q_ref, k_ref, v_ref, bias_ref) in enumerate(((q1, k1, v1, b1), (q2, k2, v2, b2), (q3, k3, v3, b3))):
        dil = AT_GROUPS[g][1]
        kw, tiles = _att_tiles(seq // dil)
        qn_ref[...] = q_ref[0].astype(F32)
        kn_ref[...] = k_ref[0].astype(F32)
        vn_ref[...] = v_ref[0].astype(F32)
        for r in range(dil):
            for qs, ks, var in tiles:
                rows = lambda start, size: (pl.ds(r + start * dil, size, stride=dil) if dil > 1
                                            else pl.ds(start, size))
                qt = qn_ref[rows(qs, ATT_TQ), :].astype(BF16)
                kt = kn_ref[rows(ks, kw), :].astype(BF16)
                vt = vn_ref[rows(ks, kw), :].astype(BF16)
                s = lax.dot_general(qt, kt, contract_last, preferred_element_type=F32) + bias_ref[var, 0]
                m = jnp.max(s, axis=-1, keepdims=True)
                p = jnp.exp(s - m)
                z = jnp.sum(p, axis=-1, keepdims=True)
                og_ref[g, rows(qs, ATT_TQ), :] = jnp.dot(p.astype(BF16), vt, preferred_element_type=F32) / z
                lg_ref[g, rows(qs, ATT_TQ), :] = jnp.broadcast_to(m + jnp.log(z), (ATT_TQ, HD))
    l0, l1, l2 = lg_ref[0], lg_ref[1], lg_ref[2]
    mx = jnp.maximum(jnp.maximum(l0, l1), l2)
    e0, e1, e2 = jnp.exp(l0 - mx), jnp.exp(l1 - mx), jnp.exp(l2 - mx)
    o_ref[0] = ((e0 * og_ref[0] + e1 * og_ref[1] + e2 * og_ref[2]) / (e0 + e1 + e2)).astype(o_ref.dtype)


def _dilated_attention(z3, rel_bias):
    b, s, _ = z3.shape
    ng = len(AT_GROUPS)
    col = lambda part, g: pl.BlockSpec((1, s, HD), lambda i, h: (i, 0, part * ng * AT_HEADS + g * AT_HEADS + h))
    biases = [_att_bias(rel_bias, g, s) for g in range(ng)]
    bias_spec = lambda a: pl.BlockSpec((a.shape[0], 1) + a.shape[2:], lambda i, h: (0, h, 0, 0))
    return pl.pallas_call(
        _dilated_kernel,
        out_shape=jax.ShapeDtypeStruct((b, s, AT_HEADS * HD), BF16),
        grid=(b, AT_HEADS),
        in_specs=[col(p, g) for p in range(3) for g in range(ng)] + [bias_spec(a) for a in biases],
        out_specs=pl.BlockSpec((1, s, HD), lambda i, h: (i, 0, h)),
        scratch_shapes=[pltpu.VMEM((s, HD), F32), pltpu.VMEM((s, HD), F32), pltpu.VMEM((s, HD), F32),
                        pltpu.VMEM((ng, s, HD), F32), pltpu.VMEM((ng, s, HD), F32)],
        compiler_params=_params("parallel", "parallel"),
        name="dilated_attention",
    )(*([z3] * 9), *biases)


ROUTER_LANES = 128


def _router_kernel(x_ref, g_ref, w_ref, hp_ref, idx_ref, gate_ref, count_ref, carry_ref):
    tm = x_ref.shape[0]
    half = hp_ref.shape[1]

    @pl.when(pl.program_id(0) == 0)
    def _():
        carry_ref[...] = jnp.zeros_like(carry_ref)

    h = _rms_rows(x_ref[...], g_ref[...])
    hp_ref[...] = _bf16_bits(h[:, :half]) | (_bf16_bits(h[:, half:]) << 16)
    h_hi = h.astype(BF16)
    h_lo = (h - h_hi.astype(F32)).astype(BF16)
    logits = (jnp.dot(h_hi, w_ref[0], preferred_element_type=F32)
              + (jnp.dot(h_hi, w_ref[1], preferred_element_type=F32)
                 + jnp.dot(h_lo, w_ref[0], preferred_element_type=F32)))
    lane =lax.broadcasted_iota(jnp.int32, logits.shape, 1).astype(F32)
    logits = jnp.where(lane < N_EXPERTS, logits, -jnp.inf)
    m1 = jnp.max(logits, axis=-1, keepdims=True)
    i1 = jnp.min(jnp.where(logits == m1, lane, float(ROUTER_LANES)), axis=-1, keepdims=True)
    rest = jnp.where(lane == i1, -jnp.inf, logits)
    m2 = jnp.max(rest, axis=-1, keepdims=True)
    i2 = jnp.min(jnp.where(rest == m2, lane, float(ROUTER_LANES)), axis=-1, keepdims=True)
    e2 = jnp.exp(m2 - m1)
    den = 1.0 + e2
    gate_ref[...] = jnp.where(lane == 0, 1.0 / den, e2 / den)
    chosen = jnp.where((lane == i1) | (lane == i2), 1.0, 0.0)
    earlier = lax.broadcasted_iota(jnp.int32, (tm, tm), 0) > lax.broadcasted_iota(jnp.int32, (tm, tm), 1)
    carry = carry_ref[...]
    before = jnp.dot(jnp.where(earlier, 1.0, 0.0).astype(BF16), chosen.astype(BF16),
                     preferred_element_type=F32) + carry
    r1 = jnp.sum(jnp.where(lane == i1, before, 0.0), axis=-1, keepdims=True)
    r2 = jnp.sum(jnp.where(lane == i2, before, 0.0), axis=-1, keepdims=True)
    idx_ref[...] = jnp.where(lane == 0, i1, jnp.where(lane == 1, i2, jnp.where(lane == 2, r1, r2))).astype(jnp.int32)
    carry = carry + jnp.sum(chosen, axis=0, keepdims=True)
    carry_ref[...] = carry
    count_ref[...] = carry.astype(jnp.int32)


def _router(x2, gain, router):
    rows, d = x2.shape
    tm = MOE_CHUNK
    w = jnp.zeros((d, ROUTER_LANES), F32).at[:, :N_EXPERTS].set(router.astype(F32))
    w_hi = w.astype(BF16)
    w = jnp.stack([w_hi, (w - w_hi.astype(F32)).astype(BF16)])
    row_spec = lambda width: pl.BlockSpec((tm, width), lambda i: (i, 0))
    return pl.pallas_call(
        _router_kernel,
        out_shape=(jax.ShapeDtypeStruct((rows, d // 2), jnp.uint32),
                   jax.ShapeDtypeStruct((rows, ROUTER_LANES), jnp.int32),
                   jax.ShapeDtypeStruct((rows, ROUTER_LANES), F32),
                   jax.ShapeDtypeStruct((1, ROUTER_LANES), jnp.int32)),
        grid=(rows // tm,),
        in_specs=[row_spec(d),
                  pl.BlockSpec((1, d), lambda i: (0, 0)),
                  pl.BlockSpec((2, d, ROUTER_LANES), lambda i: (0, 0, 0))],
        out_specs=(row_spec(d // 2), row_spec(ROUTER_LANES), row_spec(ROUTER_LANES),
                   pl.BlockSpec((1, ROUTER_LANES), lambda i: (0, 0))),
        scratch_shapes=[pltpu.VMEM((1, ROUTER_LANES), F32)],
        compiler_params=_params("arbitrary"),
        name="router",
    )(x2, gain.reshape(1, d), w)


def _sc_workers():
    info = plsc.get_sparse_core_info()
    mesh = plsc.VectorSubcoreMesh(core_axis_name="core", subcore_axis_name="subcore")
    return mesh, info.num_cores, info.num_subcores


def _pack_weight_rows(w, col_tile):
    r, c = w.shape
    mesh, nc, ns = _sc_workers()
    rs, ncol = SC_PACK_ROWS, c // col_tile
    per_w = (r // rs) * ncol // (nc * ns)
    assert per_w * nc * ns * rs * col_tile == r * c and per_w % 2 == 0 and col_tile % SC_LANES == 0
    params = pltpu.CompilerParams()
    if "needs_layout_passes" in pltpu.CompilerParams.__dataclass_fields__:
        params = dataclasses.replace(params, needs_layout_passes=False)

    @functools.partial(
        pl.kernel, mesh=mesh, out_type=jax.ShapeDtypeStruct((r // 2, c), jnp.uint32), compiler_params=params,
        cost_estimate=pl.CostEstimate(flops=r * c, transcendentals=0, bytes_accessed=6 * r * c),
        scratch_types=[pltpu.VMEM((2, rs, col_tile), F32), pltpu.VMEM((2, rs // 2, col_tile), jnp.uint32),
                       pltpu.SemaphoreType.DMA((2,)), pltpu.SemaphoreType.DMA((2,))])
    def pack(w_hbm, o_hbm, in_v, out_v, rsem, wsem):
        wid = lax.axis_index("subcore") * nc + lax.axis_index("core")

        @pl.loop(0, per_w // 2)
        def _(it):
            reads, writes = [], []
            for b in range(2):
                tile = wid * per_w + 2 * it + b
                r0 = pl.multiple_of((tile // ncol) * rs, rs)
                c0 = pl.multiple_of((tile % ncol) * col_tile, col_tile)
                reads.append(pltpu.make_async_copy(w_hbm.at[pl.ds(r0, rs), pl.ds(c0, col_tile)], in_v.at[b],
                                                   rsem.at[b]))
                writes.append(pltpu.make_async_copy(
                    out_v.at[b], o_hbm.at[pl.ds(pl.multiple_of(r0 // 2, rs // 2), rs // 2), pl.ds(c0, col_tile)],
                    wsem.at[b]))
            reads[0].start()
            reads[1].start()
            for b in range(2):
                reads[b].wait()
                for pair in range(rs // 2):
                    @plsc.parallel_loop(0, col_tile, step=SC_LANES, unroll=8)
                    def _(j):
                        packed = plsc.pack(in_v[b, 2 * pair, pl.ds(j, SC_LANES)],
                                           in_v[b, 2 * pair + 1, pl.ds(j, SC_LANES)],
                                           format=plsc.PackFormat.INTERLEAVED)
                        out_v[b, pair, pl.ds(j, SC_LANES)] = plsc.bitcast(packed, jnp.uint32)
                writes[b].start()
            writes[0].wait()
            writes[1].wait()

    return pack(w)


def _sc_token_rows(t, nc, ns):
    per_w = t // (nc * ns)
    assert per_w * nc * ns == t and per_w % (2 * SC_ROWS) == 0
    return per_w, per_w // SC_ROWS


def _scatter_rows(table, dests, p_rows, after=()):
    t, w = table.shape
    nk = len(dests)
    mesh, nc, ns = _sc_workers()
    per_w, nit = _sc_token_rows(t, nc, ns)
    ch = SC_ROWS

    @functools.partial(
        pl.kernel, mesh=mesh, out_type=jax.ShapeDtypeStruct((p_rows, w), table.dtype),
        scratch_types=[pltpu.VMEM((nk, nit, ch), jnp.int32), pltpu.VMEM((2, ch, w), table.dtype),
                       pltpu.SemaphoreType.DMA((2,)), pltpu.SemaphoreType.DMA((2, nk))])
    def scatter(table_hbm, *refs):
        dest_hbm, out_hbm = refs[:nk], refs[nk + len(after)]
        idx_v, rows_v, rsem, wsem = refs[nk + len(after) + 1:]
        wid = lax.axis_index("subcore") * nc + lax.axis_index("core")
        for k in range(nk):
            pltpu.sync_copy(dest_hbm[k].at[pl.ds(wid * nit, nit)], idx_v.at[k])

        @pl.loop(0, nit // 2)
        def _(it):
            reads = [pltpu.make_async_copy(table_hbm.at[pl.ds(wid * per_w + (2 * it + b) * ch, ch)],
                                           rows_v.at[b], rsem.at[b]) for b in range(2)]
            writes = [[pltpu.make_async_copy(rows_v.at[b], out_hbm.at[idx_v.at[k].at[2 * it + b]], wsem.at[b, k])
                       for k in range(nk)] for b in range(2)]
            reads[0].start()
            reads[1].start()
            for b in range(2):
                reads[b].wait()
                for k in range(nk):
                    writes[b][k].start()
            for b in range(2):
                for k in range(nk):
                    writes[b][k].wait()

    return scatter(table, *[d.reshape(t // ch, ch) for d in dests], *after)


def _gather_rows(table, idxs):
    t = idxs[0].shape[0]
    w = table.shape[1]
    nk = len(idxs)
    mesh, nc, ns = _sc_workers()
    per_w, nit = _sc_token_rows(t, nc, ns)
    ch = SC_ROWS

    @functools.partial(
        pl.kernel, mesh=mesh, out_type=[jax.ShapeDtypeStruct((t, w), table.dtype)] * nk,
        scratch_types=[pltpu.VMEM((nk, nit, ch), jnp.int32), pltpu.VMEM((nk, 2, ch, w), table.dtype),
                       pltpu.SemaphoreType.DMA((nk, 2)), pltpu.SemaphoreType.DMA((nk, 2))])
    def gather(table_hbm, *refs):
        idx_hbm, out_hbm = refs[:nk], refs[nk:2 * nk]
        idx_v, rows_v, rsem, wsem = refs[2 * nk:]
        wid = lax.axis_index("subcore") * nc + lax.axis_index("core")
        for k in range(nk):
            pltpu.sync_copy(idx_hbm[k].at[pl.ds(wid * nit, nit)], idx_v.at[k])

        @pl.loop(0, nit // 2)
        def _(it):
            slots = [(k, b) for k in range(nk) for b in range(2)]
            reads = {(k, b): pltpu.make_async_copy(table_hbm.at[idx_v.at[k].at[2 * it + b]], rows_v.at[k, b],
                                                   rsem.at[k, b]) for k, b in slots}
            writes = {(k, b): pltpu.make_async_copy(rows_v.at[k, b],
                                                    out_hbm[k].at[pl.ds(wid * per_w + (2 * it + b) * ch, ch)],
                                                    wsem.at[k, b]) for k, b in slots}
            for s in slots:
                reads[s].start()
            for s in slots:
                reads[s].wait()
                writes[s].start()
            for s in slots:
                writes[s].wait()

    return gather(table, *[i.reshape(t // ch, ch) for i in idxs])


def _combine_kernel(x_ref, y0_ref, y1_ref, gate_ref, o_ref):
    half = x_ref.shape[1] // 2
    for part in range(2):
        cols = slice(part * half, (part + 1) * half)
        acc = x_ref[:, cols]
        for k, y_ref in enumerate((y0_ref, y1_ref)):
            word = y_ref[...]
            bits = (word << 16) if part == 0 else (word & jnp.uint32(0xFFFF0000))
            acc = acc + gate_ref[:, k:k + 1] * lax.bitcast_convert_type(bits, F32)
        o_ref[:, cols] = acc


def _combine(x2, y0, y1, gates):
    t, d = x2.shape
    tm = ROW_TILE
    return pl.pallas_call(
        _combine_kernel,
        out_shape=jax.ShapeDtypeStruct((t, d), F32),
        grid=(t // tm,),
        in_specs=[pl.BlockSpec((tm, d), lambda i: (i, 0)),
                  pl.BlockSpec((tm, d // 2), lambda i: (i, 0)),
                  pl.BlockSpec((tm, d // 2), lambda i: (i, 0)),
                  pl.BlockSpec((tm, TOP_K), lambda i: (i, 0))],
        out_specs=pl.BlockSpec((tm, d), lambda i: (i, 0)),
        compiler_params=_params("parallel"),
        name="moe_combine",
    )(x2, y0, y1, gates)


def _moe(x2, gain, router, wg, wu, wd):
    t, d = x2.shape
    hp, idx, gate, count = _router(x2, gain, router)
    experts = jnp.arange(N_EXPERTS, dtype=jnp.int32)
    counts = count[0, :N_EXPERTS]
    padded = (counts + MOE_ROWS - 1) // MOE_ROWS * MOE_ROWS
    pend = jnp.cumsum(padded)
    pstart = pend - padded
    expert, rank = idx[:, :TOP_K], idx[:, TOP_K:2 * TOP_K]
    dest = jnp.sum(jnp.where(expert[:, :, None] == experts, pstart, 0), axis=-1) + rank
    dests = [dest[:, k] for k in range(TOP_K)]
    p_rows = t * TOP_K + N_EXPERTS * MOE_ROWS
    blk_row = jnp.arange(p_rows // MOE_ROWS, dtype=jnp.int32) * MOE_ROWS
    blk_expert = jnp.minimum(jnp.sum(pend[None, :] <= blk_row[:, None], axis=1), N_EXPERTS - 1).astype(jnp.int32)
    valid = jnp.clip(counts[blk_expert] - (blk_row - pstart[blk_expert]), 0, MOE_ROWS)
    valid = jnp.where(blk_row < pend[-1], valid, 0).astype(jnp.int32)
    n_exp, _, f = wg.shape
    pack = lambda w, col_tile: _pack_weight_rows(w.reshape(-1, w.shape[2]), col_tile).reshape(n_exp, -1, w.shape[2])
    wg, wu, wd = pack(wg, f // 2), pack(wu, f // 2), pack(wd, d)
    hb = _scatter_rows(hp, dests, p_rows, after=(wg, wu, wd))
    yb = _ffn_experts(hb, wg, wu, wd, blk_expert, valid)
    y0, y1 = _gather_rows(yb, dests)
    return _combine(x2, y0, y1, gate[:, :TOP_K])


def kernel(x, mem, rel_bias, norm_mix, norm_mem, norm_ffn, w_mem_kv, xq_norm, xk_norm, w_out, hy_w_in, hy_conv_w, hy_conv_b, hy_filt_w1, hy_filt_b1, hy_filt_w2, hy_filt_b2, hy_filt_w3, hy_sin_freq, hy_skip, at_w_in, at_q_norm, at_k_norm, ffn_w_gate, ffn_w_up, ffn_w_down, moe_router, moe_w_gate, moe_w_up, moe_w_down):
    b, s, d = x.shape
    t = b * s
    m_len = mem.shape[1]
    x2 = x.reshape(t, d)
    mem2 = mem.reshape(b * m_len, d)
    bf = lambda w: w.astype(BF16)
    score_scale = HD ** -0.5
    xa_heads = XA_W // HD
    at_heads = AT_W // HD
    plain = lambda width: [None] * (width // HD)
    kv_gains = lambda i: [xk_norm[i]] * xa_heads + plain(XA_W)

    tables = _dft_tables(s)
    spectra, hq = _hyena_filters(s, tables, hy_filt_w1[0], hy_filt_b1[0], hy_filt_w2[0], hy_filt_b2[0],
                                 hy_filt_w3[0], hy_sin_freq[0])
    z = _norm_matmul(x2, norm_mix[0], bf(hy_w_in[0]), 1024,
                     plain(3 * HY_C) + [xq_norm[0] * score_scale] * xa_heads).reshape(b, s, -1)
    kv = _norm_matmul(mem2, norm_mem[0], bf(w_mem_kv[0]), 1024, kv_gains(0)).reshape(b, m_len, -1)
    self_out = _hyena(z, hy_conv_w[0], hy_conv_b[0], tables, spectra, hq, hy_skip[0])
    x2 = _mix_out(x2.reshape(b, s, d), self_out, z, 3 * HY_C // XA_W, kv, bf(w_out[0])).reshape(t, d)
    x2 = _ffn_dense(x2, norm_ffn[0], bf(ffn_w_gate[0]), bf(ffn_w_up[0]), bf(ffn_w_down[0]))

    at_gains = ([at_q_norm[0] * score_scale] * at_heads + [at_k_norm[0]] * at_heads + plain(AT_W)
                + [xq_norm[1] * score_scale] * xa_heads)
    z = _norm_matmul(x2, norm_mix[1], bf(at_w_in[0]), 1024, at_gains).reshape(b, s, -1)
    kv = _norm_matmul(mem2, norm_mem[1], bf(w_mem_kv[1]), 1024, kv_gains(1)).reshape(b, m_len, -1)
    self_out = _dilated_attention(z, rel_bias)
    x2 = _mix_out(x2.reshape(b, s, d), self_out, z, 3 * AT_W // XA_W, kv, bf(w_out[1])).reshape(t, d)
    x2 = _moe(x2, norm_ffn[1], moe_router[0], moe_w_gate[0], moe_w_up[0], moe_w_down[0])
    return x2.reshape(b, s, d)
```

```python
import dataclasses
import functools
import math

import jax
import jax.numpy as jnp
import numpy as np
from jax import lax
from jax.experimental import pallas as pl
from jax.experimental.pallas import tpu as pltpu
from jax.experimental.pallas import tpu_sc as plsc

F32 = jnp.float32
BF16 = jnp.bfloat16

D_MODEL = 1024
EPS = 1e-6
HY_C = 512
FILT_BANDS = 16
DECAY_TARGET = 1e-2
FAST_DECAY_PCT = 0.3
SLOW_DECAY_PCT = 1.5
MOD_SHIFT = 0.05
AT_GROUPS = ((128, 1), (512, 4), (2048, 16))
AT_HEADS = 4
HD = 128
AT_W = 1536
NUM_BUCKETS = 32
REL_MAX_DIST = 1024
NEG_INF = -1e30
XA_W = 512
D_FF = 2816
N_EXPERTS = 8
TOP_K = 2

VMEM_LIMIT_BYTES = 56 * 1024 * 1024
ROW_TILE = 512
FF_TILE = 1408
MOE_ROWS = 512
MOE_CHUNK = 512
SC_ROWS = 32
SC_LANES = 16
SC_PACK_ROWS = 16
ATT_TQ = 128
ATT_KW = 256
HY_TC = 256
HY_KT = 512


def _params(*sem):
    return pltpu.CompilerParams(dimension_semantics=sem, vmem_limit_bytes=VMEM_LIMIT_BYTES)


def _rms_rows(x, gain):
    return x * lax.rsqrt(jnp.mean(x * x, axis=-1, keepdims=True) + EPS) * gain


def _bf16_bits(v):
    u = lax.bitcast_convert_type(v, jnp.uint32)
    return (u + jnp.uint32(0x7FFF) + ((u >> 16) & jnp.uint32(1))) >> 16


def _resident(shape, index_map):
    return pl.BlockSpec(shape, index_map, pipeline_mode=pl.Buffered(1))


def _norm_matmul_kernel(x_ref, g_ref, w_ref, hg_ref, o_ref, *, tn, head_norm):
    h = _rms_rows(x_ref[...], g_ref[...]).astype(BF16)
    for c in range(o_ref.shape[1] // tn):
        acc = jnp.dot(h, w_ref[:, c * tn:(c + 1) * tn], preferred_element_type=F32)
        for j in range(tn // HD):
            cols = slice(c * tn + j * HD, c * tn + (j + 1) * HD)
            seg = acc[:, j * HD:(j + 1) * HD]
            if head_norm[cols.start // HD]:
                seg = _rms_rows(seg, hg_ref[:, cols])
            o_ref[:, cols] = seg.astype(o_ref.dtype)


def _norm_matmul(x2, gain, w_bf, tn, head_gains):
    rows, d = x2.shape
    n = w_bf.shape[1]
    tm = min(ROW_TILE, rows)
    assert len(head_gains) * HD == n
    hg = jnp.concatenate([jnp.ones((HD,), F32) if g is None else g.astype(F32) for g in head_gains]).reshape(1, n)
    return pl.pallas_call(
        functools.partial(_norm_matmul_kernel, tn=tn, head_norm=tuple(g is not None for g in head_gains)),
        out_shape=jax.ShapeDtypeStruct((rows, n), BF16),
        grid=(rows // tm,),
        in_specs=[pl.BlockSpec((tm, d), lambda i: (i, 0)),
                  pl.BlockSpec((1, d), lambda i: (0, 0)),
                  _resident((d, n), lambda i: (0, 0)),
                  pl.BlockSpec((1, n), lambda i: (0, 0))],
        out_specs=pl.BlockSpec((tm, n), lambda i: (i, 0)),
        compiler_params=_params("parallel"),
        cost_estimate=pl.CostEstimate(flops=2 * rows * d * n, transcendentals=rows,
                                      bytes_accessed=4 * rows * d + 2 * d * n + 2 * rows * n),
        name="norm_matmul",
    )(x2, gain.reshape(1, d), w_bf, hg)


def _mix_out_kernel(x_ref, a_ref, xq_ref, kv_ref, wa_ref, wc_ref, o_ref):
    heads = []
    for h in range(XA_W // HD):
        cols = slice(h * HD, (h + 1) * HD)
        s = lax.dot_general(xq_ref[0, :, cols], kv_ref[0, :, cols], (((1,), (1,)), ((), ())),
                            preferred_element_type=F32)
        p = jnp.exp(s - jnp.max(s, axis=-1, keepdims=True))
        z = jnp.sum(p, axis=-1, keepdims=True)
        pv = jnp.dot(p.astype(BF16), kv_ref[0, :, XA_W + h * HD:XA_W + (h + 1) * HD], preferred_element_type=F32)
        heads.append((pv / z).astype(BF16))
    cross = jnp.concatenate(heads, axis=1)
    o_ref[0] = (x_ref[0]
                + jnp.dot(a_ref[0], wa_ref[...], preferred_element_type=F32)
                + jnp.dot(cross, wc_ref[...], preferred_element_type=F32))


def _mix_out(x3, self_out, z3, xq_block, kv3, w_bf):
    b, s, d = x3.shape
    half = self_out.shape[2]
    m = kv3.shape[1]
    tm = ROW_TILE
    return pl.pallas_call(
        _mix_out_kernel,
        out_shape=jax.ShapeDtypeStruct((b, s, d), F32),
        grid=(b, s // tm),
        in_specs=[pl.BlockSpec((1, tm, d), lambda i, j: (i, j, 0)),
                  pl.BlockSpec((1, tm, half), lambda i, j: (i, j, 0)),
                  pl.BlockSpec((1, tm, XA_W), lambda i, j: (i, j, xq_block)),
                  pl.BlockSpec((1, m, 2 * XA_W), lambda i, j: (i, 0, 0)),
                  pl.BlockSpec((half, d), lambda i, j: (0, 0)),
                  pl.BlockSpec((d - half, d), lambda i, j: (1, 0))],
        out_specs=pl.BlockSpec((1, tm, d), lambda i, j: (i, j, 0)),
        compiler_params=_params("parallel", "parallel"),
        name="mix_out",
    )(x3, self_out, z3, kv3, w_bf, w_bf)


def _weight_rows(ref, start, size):
    if ref.dtype == jnp.uint32:
        return lambda cols: pltpu.bitcast(ref[start // 2:(start + size) // 2, cols], BF16)
    return lambda cols: ref[start:start + size, cols]


def _weight_shape(ref):
    return (ref.shape[0] * (2 if ref.dtype == jnp.uint32 else 1), ref.shape[1])


def _swiglu(h, wg_ref, wu_ref, wd_ref, tf):
    d, f = _weight_shape(wg_ref)
    y = None
    for j in range(f // tf):
        cols = slice(j * tf, (j + 1) * tf)
        gg = jnp.dot(h, _weight_rows(wg_ref, 0, d)(cols), preferred_element_type=F32)
        uu = jnp.dot(h, _weight_rows(wu_ref, 0, d)(cols), preferred_element_type=F32)
        a = ((gg * jax.nn.sigmoid(gg)) * uu).astype(BF16)
        part = jnp.dot(a, _weight_rows(wd_ref, j * tf, tf)(slice(None)), preferred_element_type=F32)
        y = part if y is None else y + part
    return y


def _ffn_dense_kernel(x_ref, g_ref, wg_ref, wu_ref, wd_ref, o_ref, *, tf):
    x = x_ref[...]
    h = _rms_rows(x, g_ref[...]).astype(BF16)
    o_ref[...] = x + _swiglu(h, wg_ref, wu_ref, wd_ref, tf)


def _ffn_dense(x2, gain, wg, wu, wd):
    rows, d = x2.shape
    f = wg.shape[1]
    tm = ROW_TILE
    return pl.pallas_call(
        functools.partial(_ffn_dense_kernel, tf=FF_TILE),
        out_shape=jax.ShapeDtypeStruct((rows, d), F32),
        grid=(rows // tm,),
        in_specs=[pl.BlockSpec((tm, d), lambda i: (i, 0)),
                  pl.BlockSpec((1, d), lambda i: (0, 0)),
                  _resident((d, f), lambda i: (0, 0)),
                  _resident((d, f), lambda i: (0, 0)),
                  _resident((f, d), lambda i: (0, 0))],
        out_specs=pl.BlockSpec((tm, d), lambda i: (i, 0)),
        compiler_params=_params("parallel"),
        cost_estimate=pl.CostEstimate(flops=6 * rows * d * f, transcendentals=rows * f,
                                      bytes_accessed=8 * rows * d + 6 * d * f),
        name="ffn_dense",
    )(x2, gain.reshape(1, d), wg, wu, wd)


def _ffn_expert_kernel(eid_ref, valid_ref, hp_ref, wg_ref, wu_ref, wd_ref, o_ref, h_ref, *, tf):
    half = hp_ref.shape[1]
    valid = valid_ref[pl.program_id(0)]

    @pl.when(valid > 0)
    def _():
        keep = lax.broadcasted_iota(jnp.int32, (hp_ref.shape[0], 1), 0) < valid
        word = hp_ref[...]
        h_ref[:, :half] = jnp.where(keep, lax.bitcast_convert_type(word << 16, F32), 0.0).astype(BF16)
        h_ref[:, half:] = jnp.where(keep, lax.bitcast_convert_type(word & jnp.uint32(0xFFFF0000), F32),
                                    0.0).astype(BF16)
        y = _swiglu(h_ref[...], wg_ref.at[0], wu_ref.at[0], wd_ref.at[0], tf)
        o_ref[...] = _bf16_bits(y[:, :half]) | (_bf16_bits(y[:, half:]) << 16)

    @pl.when(valid <= 0)
    def _():
        o_ref[...] = jnp.zeros_like(o_ref)


def _ffn_experts(hp, wg, wu, wd, eid, valid):
    rows, half = hp.shape
    d = 2 * half
    tm = MOE_ROWS
    expert_spec = lambda w: pl.BlockSpec((1,) + w.shape[1:], lambda i, e, n: (e[i], 0, 0))
    grid_spec = pltpu.PrefetchScalarGridSpec(
        num_scalar_prefetch=2,
        grid=(rows // tm,),
        in_specs=[pl.BlockSpec((tm, half), lambda i, e, n: (i, 0)),
                  expert_spec(wg), expert_spec(wu), expert_spec(wd)],
        out_specs=pl.BlockSpec((tm, half), lambda i, e, n: (i, 0)),
        scratch_shapes=[pltpu.VMEM((tm, d), BF16)],
    )
    return pl.pallas_call(
        functools.partial(_ffn_expert_kernel, tf=FF_TILE),
        out_shape=jax.ShapeDtypeStruct((rows, half), jnp.uint32),
        grid_spec=grid_spec,
        compiler_params=_params("arbitrary"),
        name="ffn_experts",
    )(eid, valid, hp, wg, wu, wd)


def _dft_kernel(ce_ref, se_ref, co_ref, so_ref, cot_ref, sot_ref, base_ref, *, n_fft):
    rows, cols = ce_ref.shape
    i = pl.program_id(0)
    theta = 2.0 * math.pi / n_fft

    def phases(r, c):
        return r * (2 * c), r * (2 * c + 1), c * (2 * r + 1)

    @pl.when(i == 0)
    def _():
        r = lax.broadcasted_iota(jnp.int32, (rows, cols), 0)
        c = lax.broadcasted_iota(jnp.int32, (rows, cols), 1)
        for f, ph in enumerate(phases(r, c)):
            ang = (ph & (n_fft - 1)).astype(F32) * theta
            base_ref[2 * f] = jnp.cos(ang)
            base_ref[2 * f + 1] = jnp.sin(ang)

    c = lax.broadcasted_iota(jnp.int32, (8, cols), 1)
    r0 = i * rows
    shifts = (r0 * (2 * c), r0 * (2 * c + 1), c * (2 * r0))
    for f, (c_ref, s_ref) in enumerate(((ce_ref, se_ref), (co_ref, so_ref), (cot_ref, sot_ref))):
        ang = (shifts[f] & (n_fft - 1)).astype(F32) * theta
        ca, sa = jnp.cos(ang)[0:1], jnp.sin(ang)[0:1]
        cb, sb = base_ref[2 * f], base_ref[2 * f + 1]
        c_ref[...] = (cb * ca - sb * sa).astype(BF16)
        s_ref[...] = (sb * ca + cb * sa).astype(BF16)


def _dft_tables(length):
    half, rows = length // 2, 128
    shape = jax.ShapeDtypeStruct((half, half), BF16)
    spec = pl.BlockSpec((rows, half), lambda i: (i, 0))
    return pl.pallas_call(
        functools.partial(_dft_kernel, n_fft=2 * length),
        out_shape=(shape,) * 6,
        grid=(half // rows,),
        out_specs=(spec,) * 6,
        scratch_shapes=[pltpu.VMEM((6, rows, half), F32)],
        compiler_params=_params("arbitrary"),
        name="dft_tables",
    )()


LANES = 128


def _split_parity(tmp_ref, x):
    half = x.shape[0] // 2
    for j in range(tmp_ref.shape[0]):
        tmp_ref[j] = x[:, j * LANES:(j + 1) * LANES]
    pick = lambda start: jnp.concatenate(
        [tmp_ref[j, pl.ds(start, half, stride=2), :] for j in range(tmp_ref.shape[0])], axis=1)
    return pick(0), pick(1)


def _merge_parity(tmp_ref, even, odd):
    half = even.shape[0]
    for j in range(tmp_ref.shape[0]):
        tmp_ref[j, pl.ds(0, half, stride=2), :] = even[:, j * LANES:(j + 1) * LANES]
        tmp_ref[j, pl.ds(1, half, stride=2), :] = odd[:, j * LANES:(j + 1) * LANES]
    return jnp.concatenate([tmp_ref[j] for j in range(tmp_ref.shape[0])], axis=1)


def _filter_time_kernel(feats_ref, w1_ref, b1_ref, w2_ref, b2_ref, fr_ref, w3_ref, t_ref, delta_ref,
                        g_ref, hq_ref, tmp_ref):
    hp = lax.Precision.HIGHEST
    length = feats_ref.shape[0]
    half = length // 2
    fr = fr_ref[...]
    h = jnp.sin(fr * (jnp.dot(feats_ref[...], w1_ref[...], preferred_element_type=F32, precision=hp) + b1_ref[...]))
    h = jnp.sin(fr * (jnp.dot(h, w2_ref[...], preferred_element_type=F32, precision=hp) + b2_ref[...]))
    mod = jnp.exp(-t_ref[...] * delta_ref[...]) + MOD_SHIFT
    row = lax.broadcasted_iota(jnp.int32, (length, 1), 0)
    alt = jnp.where(lax.broadcasted_iota(jnp.int32, (half, 1), 0) % 2 == 0, 1.0, -1.0).astype(F32)
    for o in range(2):
        fwd = jnp.dot(h, w3_ref[2 * o], preferred_element_type=F32, precision=hp) * mod
        bwd = jnp.dot(h, w3_ref[2 * o + 1], preferred_element_type=F32, precision=hp) * mod
        bwd = jnp.where(row == 0, 0.0, bwd)
        norm = (jnp.sum(jnp.abs(fwd), axis=0, keepdims=True)
                + jnp.sum(jnp.abs(bwd), axis=0, keepdims=True) + 1e-6)
        fwd = fwd / norm
        bwd = bwd / norm
        for part, g in enumerate((fwd + bwd, fwd - bwd)):
            even, odd = _split_parity(tmp_ref, g)
            g_ref[o, 2 * part] = even.astype(BF16)
            g_ref[o, 2 * part + 1] = odd.astype(BF16)
            mid = jnp.sum((even if part == 0 else odd) * alt, axis=0, keepdims=True) * (1.0 / length)
            hq_ref[o, part] = mid if part == 0 else -mid


def _filter_freq_kernel(ce_ref, se_ref, co_ref, so_ref, g_ref, h_ref):
    half = ce_ref.shape[0]
    row = lax.broadcasted_iota(jnp.int32, (half, 1), 0)
    scale = jnp.where(row == 0, 1.0, 2.0).astype(F32) * (1.0 / (4 * half))
    a = jnp.dot(ce_ref[...], g_ref[0, 0], preferred_element_type=F32)
    b = jnp.dot(co_ref[...], g_ref[0, 1], preferred_element_type=F32)
    c = jnp.dot(se_ref[...], g_ref[0, 2], preferred_element_type=F32)
    d = jnp.dot(so_ref[...], g_ref[0, 3], preferred_element_type=F32)
    h_ref[0, 0] = (a + b) * scale
    h_ref[0, 1] = -(c + d) * scale
    h_ref[0, 2] = (a - b) * scale
    h_ref[0, 3] = (c - d) * scale


def _hyena_filters(length, tables, w1, b1, w2, b2, w3, freq):
    t = jnp.linspace(0.0, 1.0, length, dtype=F32)[:, None]
    f = jnp.linspace(1e-4, FILT_BANDS - 1, FILT_BANDS, dtype=F32)[None]
    ang = (2.0 * math.pi / length) * jnp.arange(length, dtype=F32)[:, None] * f
    feats = jnp.concatenate([t, jnp.cos(ang), -jnp.sin(ang)], axis=-1)
    deltas = jnp.abs(jnp.linspace(math.log(DECAY_TARGET) / SLOW_DECAY_PCT,
                                  math.log(DECAY_TARGET) / FAST_DECAY_PCT, HY_C, dtype=F32))[None]
    hid = w1.shape[1]
    w3r = w3.reshape(hid, 4, HY_C).transpose(1, 0, 2)
    tc = HY_TC
    full = lambda shape: pl.BlockSpec(shape, lambda c: (0,) * len(shape))
    half = length // 2
    g, hq = pl.pallas_call(
        _filter_time_kernel,
        out_shape=(jax.ShapeDtypeStruct((2, 4, half, HY_C), BF16),
                   jax.ShapeDtypeStruct((2, 2, 1, HY_C), F32)),
        grid=(HY_C // tc,),
        in_specs=[full(feats.shape), full(w1.shape), full((1, hid)), full(w2.shape), full((1, hid)),
                  full((1, hid)), pl.BlockSpec((4, hid, tc), lambda c: (0, 0, c)), full((length, 1)),
                  pl.BlockSpec((1, tc), lambda c: (0, c))],
        out_specs=(pl.BlockSpec((2, 4, half, tc), lambda c: (0, 0, 0, c)),
                   pl.BlockSpec((2, 2, 1, tc), lambda c: (0, 0, 0, c))),
        scratch_shapes=[pltpu.VMEM((tc // LANES, length, LANES), F32)],
        compiler_params=_params("parallel"),
        name="filter_time",
    )(feats, w1, b1.reshape(1, hid), w2, b2.reshape(1, hid), freq.reshape(1, hid), w3r, t, deltas)
    ce, se, co, so = tables[:4]
    table_spec = _resident((half, half), lambda o, c: (0, 0))
    spectra = pl.pallas_call(
        _filter_freq_kernel,
        out_shape=jax.ShapeDtypeStruct((2, 4, half, HY_C), F32),
        grid=(2, HY_C // tc),
        in_specs=[table_spec] * 4 + [pl.BlockSpec((1, 4, half, tc), lambda o, c: (o, 0, 0, c))],
        out_specs=pl.BlockSpec((1, 4, half, tc), lambda o, c: (o, 0, 0, c)),
        compiler_params=_params("parallel", "parallel"),
        name="filter_freq",
    )(ce, se, co, so, g)
    return spectra, hq


def _hyena_kernel(z0_ref, z1_ref, z2_ref, cw_ref, cb_ref, ce_ref, se_ref, co_ref, so_ref, cot_ref, sot_ref,
                  h_ref, hq_ref, skip_ref, o_ref, tmp_ref):
    length = z0_ref.shape[1]
    half = length // 2
    row = lax.broadcasted_iota(jnp.int32, (length, 1), 0)
    alt = jnp.where(lax.broadcasted_iota(jnp.int32, (half, 1), 0) % 2 == 0, 1.0, -1.0).astype(F32)

    def short_conv(z_ref, c):
        u = z_ref[0].astype(F32)
        prev = jnp.where(row == 0, 0.0, pltpu.roll(u, 1, axis=0))
        nxt = jnp.where(row == length - 1, 0.0, pltpu.roll(u, length - 1, axis=0))
        return cb_ref[c] + (prev * cw_ref[0, c] + u * cw_ref[1, c] + nxt * cw_ref[2, c])

    def long_conv(u, o):
        ue, uo = _split_parity(tmp_ref, u)
        ue_bf, uo_bf = ue.astype(BF16), uo.astype(BF16)
        a4 = jnp.sum(ue * alt, axis=0, keepdims=True)
        b4 = jnp.sum(uo * alt, axis=0, keepdims=True)
        hr4, hi4 = hq_ref[o, 0], hq_ref[o, 1]
        y_even = alt * (a4 * hr4 + b4 * hi4)
        y_odd = -(alt * (a4 * hi4 - b4 * hr4))
        for kt in range(half // HY_KT):
            ks = slice(kt * HY_KT, (kt + 1) * HY_KT)
            ae = jnp.dot(ce_ref[ks, :], ue_bf, preferred_element_type=F32)
            ao = jnp.dot(co_ref[ks, :], uo_bf, preferred_element_type=F32)
            be = jnp.dot(se_ref[ks, :], ue_bf, preferred_element_type=F32)
            bo = jnp.dot(so_ref[ks, :], uo_bf, preferred_element_type=F32)
            a_f, a_g, b_f, b_g = ae + ao, ae - ao, be + bo, bo - be
            hr_f, hi_f, hr_g, hi_g = (h_ref[o, j, ks, :] for j in range(4))
            re_f, im_f = a_f * hr_f + b_f * hi_f, a_f * hi_f - b_f * hr_f
            re_g, im_g = a_g * hr_g + b_g * hi_g, a_g * hi_g - b_g * hr_g
            y_even += (jnp.dot(ce_ref[:, ks], (re_f + re_g).astype(BF16), preferred_element_type=F32)
                       - jnp.dot(se_ref[:, ks], (im_f - im_g).astype(BF16), preferred_element_type=F32))
            y_odd += (jnp.dot(cot_ref[:, ks], (re_f - re_g).astype(BF16), preferred_element_type=F32)
                      - jnp.dot(sot_ref[:, ks], (im_f + im_g).astype(BF16), preferred_element_type=F32))
        return _merge_parity(tmp_ref, y_even, y_odd) + u * skip_ref[o]

    z = short_conv(z1_ref, 1) * long_conv(short_conv(z0_ref, 0), 0)
    o_ref[0] = (short_conv(z2_ref, 2) * long_conv(z, 1)).astype(o_ref.dtype)


def _hyena(z3, conv_w, conv_b, tables, spectra, hq, skip):
    b, s, _ = z3.shape
    half = s // 2
    tc = HY_TC
    nct = HY_C // tc
    cw = conv_w.reshape(3, 3, 1, HY_C)
    cb = conv_b.reshape(3, 1, HY_C)
    zspec = lambda chunk: pl.BlockSpec((1, s, tc), lambda c, i: (i, 0, chunk * nct + c))
    return pl.pallas_call(
        _hyena_kernel,
        out_shape=jax.ShapeDtypeStruct((b, s, HY_C), BF16),
        grid=(nct, b),
        in_specs=[zspec(0), zspec(1), zspec(2),
                  pl.BlockSpec((3, 3, 1, tc), lambda c, i: (0, 0, 0, c)),
                  pl.BlockSpec((3, 1, tc), lambda c, i: (0, 0, c))]
                 + [_resident((half, half), lambda c, i: (0, 0))] * 6
                 + [_resident((2, 4, half, tc), lambda c, i: (0, 0, 0, c)),
                    pl.BlockSpec((2, 2, 1, tc), lambda c, i: (0, 0, 0, c)),
                    pl.BlockSpec((2, 1, tc), lambda c, i: (0, 0, c))],
        out_specs=pl.BlockSpec((1, s, tc), lambda c, i: (i, 0, c)),
        scratch_shapes=[pltpu.VMEM((tc // LANES, s, LANES), F32)],
        compiler_params=_params("parallel", "parallel"),
        cost_estimate=pl.CostEstimate(flops=2 * 4 * 2 * b * s * half * HY_C, transcendentals=0,
                                      bytes_accessed=2 * 3 * b * s * HY_C + 4 * b * s * HY_C + 12 * half * half),
        name="hyena",
    )(z3, z3, z3, cw, cb, *tables, spectra, hq, skip.reshape(2, 1, HY_C))


def _rel_bucket(rel):
    half = NUM_BUCKETS // 2
    exact = half // 2
    n = np.abs(rel)
    large = exact + (np.log(np.maximum(n, 1) / exact) / np.log(REL_MAX_DIST / exact) * (half - exact)).astype(np.int32)
    large = np.minimum(large, half - 1)
    return (np.where(rel > 0, half, 0) + np.where(n < exact, n, large)).astype(np.int32)


def _att_tiles(length):
    kw = min(ATT_KW, length)
    tiles = []
    for qs in range(0, length, ATT_TQ):
        ks = min(max(qs - (kw - ATT_TQ) // 2, 0), length - kw)
        tiles.append((qs, ks, {0: 0, -64: 1, -128: 2}[ks - qs]))
    return kw, tiles


def _att_bias(rel_bias, group, seq):
    window, dil = AT_GROUPS[group]
    band = window // (2 * dil)
    kw, tiles = _att_tiles(seq // dil)
    offsets = sorted({ks - qs for qs, ks, _ in tiles}, reverse=True)
    table = rel_bias[:, group * AT_HEADS:(group + 1) * AT_HEADS].astype(F32)
    span = kw + ATT_TQ
    k = np.arange(span)
    rel = np.asarray(offsets)[:, None] + np.where(k < kw, k, k - span)[None, :]
    diag = jnp.where((np.abs(rel) <= band)[:, None, :],
                     jnp.swapaxes(table[_rel_bucket(rel * dil)], 1, 2), NEG_INF)
    return diag[:, :, None, :]


def _bias_tile(diag_ref, var, kw):
    rows = jnp.broadcast_to(diag_ref[var, 0], (ATT_TQ, diag_ref.shape[3]))
    return pltpu.roll(rows, 0, 1, stride=1, stride_axis=0)[:, :kw]


def _dilated_kernel(q1, q2, q3, k1, k2, k3, v1, v2, v3, b1, b2, b3, o_ref,
                    qn_ref, kn_ref, vn_ref, og_ref, lg_ref):
    seq = o_ref.shape[1]
    contract_last = (((1,), (1,)), ((), ()))
    for g, (q_ref, k_ref, v_ref, bias_ref) in enumerate(((q1, k1, v1, b1), (q2, k2, v2, b2), (q3, k3, v3, b3))):
        dil = AT_GROUPS[g][1]
        kw, tiles = _att_tiles(seq // dil)
        qn_ref[...] = q_ref[0].astype(F32)
        kn_ref[...] = k_ref[0].astype(F32)
        vn_ref[...] = v_ref[0].astype(F32)
        bias = {var: _bias_tile(bias_ref, var, kw) for var in sorted({var for _, _, var in tiles})}
        for r in range(dil):
            for qs, ks, var in tiles:
                rows = lambda start, size: (pl.ds(r + start * dil, size, stride=dil) if dil > 1
                                            else pl.ds(start, size))
                qt = qn_ref[rows(qs, ATT_TQ), :].astype(BF16)
                kt = kn_ref[rows(ks, kw), :].astype(BF16)
                vt = vn_ref[rows(ks, kw), :].astype(BF16)
                s = lax.dot_general(qt, kt, contract_last, preferred_element_type=F32) + bias[var]
                m = jnp.max(s, axis=-1, keepdims=True)
                p = jnp.exp(s - m)
                z = jnp.sum(p, axis=-1, keepdims=True)
                og_ref[g, rows(qs, ATT_TQ), :] = jnp.dot(p.astype(BF16), vt, preferred_element_type=F32) / z
                lg_ref[g, rows(qs, ATT_TQ), :] = jnp.broadcast_to(m + jnp.log(z), (ATT_TQ, HD))
    l0, l1, l2 = lg_ref[0], lg_ref[1], lg_ref[2]
    mx = jnp.maximum(jnp.maximum(l0, l1), l2)
    e0, e1, e2 = jnp.exp(l0 - mx), jnp.exp(l1 - mx), jnp.exp(l2 - mx)
    o_ref[0] = ((e0 * og_ref[0] + e1 * og_ref[1] + e2 * og_ref[2]) / (e0 + e1 + e2)).astype(o_ref.dtype)


def _dilated_attention(z3, rel_bias):
    b, s, _ = z3.shape
    ng = len(AT_GROUPS)
    col = lambda part, g: pl.BlockSpec((1, s, HD), lambda i, h: (i, 0, part * ng * AT_HEADS + g * AT_HEADS + h))
    biases = [_att_bias(rel_bias, g, s) for g in range(ng)]
    bias_spec = lambda a: pl.BlockSpec((a.shape[0], 1) + a.shape[2:], lambda i, h: (0, h, 0, 0))
    return pl.pallas_call(
        _dilated_kernel,
        out_shape=jax.ShapeDtypeStruct((b, s, AT_HEADS * HD), BF16),
        grid=(b, AT_HEADS),
        in_specs=[col(p, g) for p in range(3) for g in range(ng)] + [bias_spec(a) for a in biases],
        out_specs=pl.BlockSpec((1, s, HD), lambda i, h: (i, 0, h)),
        scratch_shapes=[pltpu.VMEM((s, HD), F32), pltpu.VMEM((s, HD), F32), pltpu.VMEM((s, HD), F32),
                        pltpu.VMEM((ng, s, HD), F32), pltpu.VMEM((ng, s, HD), F32)],
        compiler_params=_params("parallel", "parallel"),
        name="dilated_attention",
    )(*([z3] * 9), *biases)


ROUTER_LANES = 128


def _router_kernel(x_ref, g_ref, w_ref, hp_ref, idx_ref, gate_ref, count_ref, carry_ref):
    tm = x_ref.shape[0]
    half = hp_ref.shape[1]

    @pl.when(pl.program_id(0) == 0)
    def _():
        carry_ref[...] = jnp.zeros_like(carry_ref)

    h = _rms_rows(x_ref[...], g_ref[...])
    hp_ref[...] = _bf16_bits(h[:, :half]) | (_bf16_bits(h[:, half:]) << 16)
    h_hi = h.astype(BF16)
    h_lo = (h - h_hi.astype(F32)).astype(BF16)
    logits = (jnp.dot(h_hi, w_ref[0], preferred_element_type=F32)
              + (jnp.dot(h_hi, w_ref[1], preferred_element_type=F32)
                 + jnp.dot(h_lo, w_ref[0], preferred_element_type=F32)))
    lane =lax.broadcasted_iota(jnp.int32, logits.shape, 1).astype(F32)
    logits = jnp.where(lane < N_EXPERTS, logits, -jnp.inf)
    m1 = jnp.max(logits, axis=-1, keepdims=True)
    i1 = jnp.min(jnp.where(logits == m1, lane, float(ROUTER_LANES)), axis=-1, keepdims=True)
    rest = jnp.where(lane == i1, -jnp.inf, logits)
    m2 = jnp.max(rest, axis=-1, keepdims=True)
    i2 = jnp.min(jnp.where(rest == m2, lane, float(ROUTER_LANES)), axis=-1, keepdims=True)
    e2 = jnp.exp(m2 - m1)
    den = 1.0 + e2
    gate_ref[...] = jnp.where(lane == 0, 1.0 / den, e2 / den)
    chosen = jnp.where((lane == i1) | (lane == i2), 1.0, 0.0)
    earlier = lax.broadcasted_iota(jnp.int32, (tm, tm), 0) > lax.broadcasted_iota(jnp.int32, (tm, tm), 1)
    carry = carry_ref[...]
    before = jnp.dot(jnp.where(earlier, 1.0, 0.0).astype(BF16), chosen.astype(BF16),
                     preferred_element_type=F32) + carry
    r1 = jnp.sum(jnp.where(lane == i1, before, 0.0), axis=-1, keepdims=True)
    r2 = jnp.sum(jnp.where(lane == i2, before, 0.0), axis=-1, keepdims=True)
    idx_ref[...] = jnp.where(lane == 0, i1, jnp.where(lane == 1, i2, jnp.where(lane == 2, r1, r2))).astype(jnp.int32)
    carry = carry + jnp.sum(chosen, axis=0, keepdims=True)
    carry_ref[...] = carry
    count_ref[...] = carry.astype(jnp.int32)


def _router(x2, gain, router):
    rows, d = x2.shape
    tm = MOE_CHUNK
    w = jnp.zeros((d, ROUTER_LANES), F32).at[:, :N_EXPERTS].set(router.astype(F32))
    w_hi = w.astype(BF16)
    w = jnp.stack([w_hi, (w - w_hi.astype(F32)).astype(BF16)])
    row_spec = lambda width: pl.BlockSpec((tm, width), lambda i: (i, 0))
    return pl.pallas_call(
        _router_kernel,
        out_shape=(jax.ShapeDtypeStruct((rows, d // 2), jnp.uint32),
                   jax.ShapeDtypeStruct((rows, ROUTER_LANES), jnp.int32),
                   jax.ShapeDtypeStruct((rows, ROUTER_LANES), F32),
                   jax.ShapeDtypeStruct((1, ROUTER_LANES), jnp.int32)),
        grid=(rows // tm,),
        in_specs=[row_spec(d),
                  pl.BlockSpec((1, d), lambda i: (0, 0)),
                  pl.BlockSpec((2, d, ROUTER_LANES), lambda i: (0, 0, 0))],
        out_specs=(row_spec(d // 2), row_spec(ROUTER_LANES), row_spec(ROUTER_LANES),
                   pl.BlockSpec((1, ROUTER_LANES), lambda i: (0, 0))),
        scratch_shapes=[pltpu.VMEM((1, ROUTER_LANES), F32)],
        compiler_params=_params("arbitrary"),
        name="router",
    )(x2, gain.reshape(1, d), w)


def _sc_workers():
    info = plsc.get_sparse_core_info()
    mesh = plsc.VectorSubcoreMesh(core_axis_name="core", subcore_axis_name="subcore")
    return mesh, info.num_cores, info.num_subcores


def _pack_weight_rows(w, col_tile):
    r, c = w.shape
    mesh, nc, ns = _sc_workers()
    rs, ncol = SC_PACK_ROWS, c // col_tile
    per_w = (r // rs) * ncol // (nc * ns)
    assert per_w * nc * ns * rs * col_tile == r * c and per_w % 2 == 0 and col_tile % SC_LANES == 0
    params = pltpu.CompilerParams()
    if "needs_layout_passes" in pltpu.CompilerParams.__dataclass_fields__:
        params = dataclasses.replace(params, needs_layout_passes=False)

    @functools.partial(
        pl.kernel, mesh=mesh, out_type=jax.ShapeDtypeStruct((r // 2, c), jnp.uint32), compiler_params=params,
        cost_estimate=pl.CostEstimate(flops=r * c, transcendentals=0, bytes_accessed=6 * r * c),
        scratch_types=[pltpu.VMEM((2, rs, col_tile), F32), pltpu.VMEM((2, rs // 2, col_tile), jnp.uint32),
                       pltpu.SemaphoreType.DMA((2,)), pltpu.SemaphoreType.DMA((2,))])
    def pack(w_hbm, o_hbm, in_v, out_v, rsem, wsem):
        wid = lax.axis_index("subcore") * nc + lax.axis_index("core")

        @pl.loop(0, per_w // 2)
        def _(it):
            reads, writes = [], []
            for b in range(2):
                tile = wid * per_w + 2 * it + b
                r0 = pl.multiple_of((tile // ncol) * rs, rs)
                c0 = pl.multiple_of((tile % ncol) * col_tile, col_tile)
                reads.append(pltpu.make_async_copy(w_hbm.at[pl.ds(r0, rs), pl.ds(c0, col_tile)], in_v.at[b],
                                                   rsem.at[b]))
                writes.append(pltpu.make_async_copy(
                    out_v.at[b], o_hbm.at[pl.ds(pl.multiple_of(r0 // 2, rs // 2), rs // 2), pl.ds(c0, col_tile)],
                    wsem.at[b]))
            reads[0].start()
            reads[1].start()
            for b in range(2):
                reads[b].wait()
                for pair in range(rs // 2):
                    @plsc.parallel_loop(0, col_tile, step=SC_LANES, unroll=8)
                    def _(j):
                        packed = plsc.pack(in_v[b, 2 * pair, pl.ds(j, SC_LANES)],
                                           in_v[b, 2 * pair + 1, pl.ds(j, SC_LANES)],
                                           format=plsc.PackFormat.INTERLEAVED)
                        out_v[b, pair, pl.ds(j, SC_LANES)] = plsc.bitcast(packed, jnp.uint32)
                writes[b].start()
            writes[0].wait()
            writes[1].wait()

    return pack(w)


def _sc_token_rows(t, nc, ns):
    per_w = t // (nc * ns)
    assert per_w * nc * ns == t and per_w % (2 * SC_ROWS) == 0
    return per_w, per_w // SC_ROWS


def _scatter_rows(table, dests, p_rows, after=()):
    t, w = table.shape
    nk = len(dests)
    mesh, nc, ns = _sc_workers()
    per_w, nit = _sc_token_rows(t, nc, ns)
    ch = SC_ROWS

    @functools.partial(
        pl.kernel, mesh=mesh, out_type=jax.ShapeDtypeStruct((p_rows, w), table.dtype),
        scratch_types=[pltpu.VMEM((nk, nit, ch), jnp.int32), pltpu.VMEM((2, ch, w), table.dtype),
                       pltpu.SemaphoreType.DMA((2,)), pltpu.SemaphoreType.DMA((2, nk))])
    def scatter(table_hbm, *refs):
        dest_hbm, out_hbm = refs[:nk], refs[nk + len(after)]
        idx_v, rows_v, rsem, wsem = refs[nk + len(after) + 1:]
        wid = lax.axis_index("subcore") * nc + lax.axis_index("core")
        for k in range(nk):
            pltpu.sync_copy(dest_hbm[k].at[pl.ds(wid * nit, nit)], idx_v.at[k])

        @pl.loop(0, nit // 2)
        def _(it):
            reads = [pltpu.make_async_copy(table_hbm.at[pl.ds(wid * per_w + (2 * it + b) * ch, ch)],
                                           rows_v.at[b], rsem.at[b]) for b in range(2)]
            writes = [[pltpu.make_async_copy(rows_v.at[b], out_hbm.at[idx_v.at[k].at[2 * it + b]], wsem.at[b, k])
                       for k in range(nk)] for b in range(2)]
            reads[0].start()
            reads[1].start()
            for b in range(2):
                reads[b].wait()
                for k in range(nk):
                    writes[b][k].start()
            for b in range(2):
                for k in range(nk):
                    writes[b][k].wait()

    return scatter(table, *[d.reshape(t // ch, ch) for d in dests], *after)


def _gather_rows(table, idxs):
    t = idxs[0].shape[0]
    w = table.shape[1]
    nk = len(idxs)
    mesh, nc, ns = _sc_workers()
    per_w, nit = _sc_token_rows(t, nc, ns)
    ch = SC_ROWS

    @functools.partial(
        pl.kernel, mesh=mesh, out_type=[jax.ShapeDtypeStruct((t, w), table.dtype)] * nk,
        scratch_types=[pltpu.VMEM((nk, nit, ch), jnp.int32), pltpu.VMEM((nk, 2, ch, w), table.dtype),
                       pltpu.SemaphoreType.DMA((nk, 2)), pltpu.SemaphoreType.DMA((nk, 2))])
    def gather(table_hbm, *refs):
        idx_hbm, out_hbm = refs[:nk], refs[nk:2 * nk]
        idx_v, rows_v, rsem, wsem = refs[2 * nk:]
        wid = lax.axis_index("subcore") * nc + lax.axis_index("core")
        for k in range(nk):
            pltpu.sync_copy(idx_hbm[k].at[pl.ds(wid * nit, nit)], idx_v.at[k])

        @pl.loop(0, nit // 2)
        def _(it):
            slots = [(k, b) for k in range(nk) for b in range(2)]
            reads = {(k, b): pltpu.make_async_copy(table_hbm.at[idx_v.at[k].at[2 * it + b]], rows_v.at[k, b],
                                                   rsem.at[k, b]) for k, b in slots}
            writes = {(k, b): pltpu.make_async_copy(rows_v.at[k, b],
                                                    out_hbm[k].at[pl.ds(wid * per_w + (2 * it + b) * ch, ch)],
                                                    wsem.at[k, b]) for k, b in slots}
            for s in slots:
                reads[s].start()
            for s in slots:
                reads[s].wait()
                writes[s].start()
            for s in slots:
                writes[s].wait()

    return gather(table, *[i.reshape(t // ch, ch) for i in idxs])


def _combine_kernel(x_ref, y0_ref, y1_ref, gate_ref, o_ref):
    half = x_ref.shape[1] // 2
    for part in range(2):
        cols = slice(part * half, (part + 1) * half)
        acc = x_ref[:, cols]
        for k, y_ref in enumerate((y0_ref, y1_ref)):
            word = y_ref[...]
            bits = (word << 16) if part == 0 else (word & jnp.uint32(0xFFFF0000))
            acc = acc + gate_ref[:, k:k + 1] * lax.bitcast_convert_type(bits, F32)
        o_ref[:, cols] = acc


def _combine(x2, y0, y1, gates):
    t, d = x2.shape
    tm = ROW_TILE
    return pl.pallas_call(
        _combine_kernel,
        out_shape=jax.ShapeDtypeStruct((t, d), F32),
        grid=(t // tm,),
        in_specs=[pl.BlockSpec((tm, d), lambda i: (i, 0)),
                  pl.BlockSpec((tm, d // 2), lambda i: (i, 0)),
                  pl.BlockSpec((tm, d // 2), lambda i: (i, 0)),
                  pl.BlockSpec((tm, TOP_K), lambda i: (i, 0))],
        out_specs=pl.BlockSpec((tm, d), lambda i: (i, 0)),
        compiler_params=_params("parallel"),
        name="moe_combine",
    )(x2, y0, y1, gates)


def _moe(x2, gain, router, wg, wu, wd):
    t, d = x2.shape
    hp, idx, gate, count = _router(x2, gain, router)
    experts = jnp.arange(N_EXPERTS, dtype=jnp.int32)
    counts = count[0, :N_EXPERTS]
    padded = (counts + MOE_ROWS - 1) // MOE_ROWS * MOE_ROWS
    pend = jnp.cumsum(padded)
    pstart = pend - padded
    expert, rank = idx[:, :TOP_K], idx[:, TOP_K:2 * TOP_K]
    dest = jnp.sum(jnp.where(expert[:, :, None] == experts, pstart, 0), axis=-1) + rank
    dests = [dest[:, k] for k in range(TOP_K)]
    p_rows = t * TOP_K + N_EXPERTS * MOE_ROWS
    blk_row = jnp.arange(p_rows // MOE_ROWS, dtype=jnp.int32) * MOE_ROWS
    blk_expert = jnp.minimum(jnp.sum(pend[None, :] <= blk_row[:, None], axis=1), N_EXPERTS - 1).astype(jnp.int32)
    valid = jnp.clip(counts[blk_expert] - (blk_row - pstart[blk_expert]), 0, MOE_ROWS)
    valid = jnp.where(blk_row < pend[-1], valid, 0).astype(jnp.int32)
    n_exp, _, f = wg.shape
    pack = lambda w, col_tile: _pack_weight_rows(w.reshape(-1, w.shape[2]), col_tile).reshape(n_exp, -1, w.shape[2])
    wg, wu, wd = pack(wg, f // 2), pack(wu, f // 2), pack(wd, d)
    hb = _scatter_rows(hp, dests, p_rows, after=(wg, wu, wd))
    yb = _ffn_experts(hb, wg, wu, wd, blk_expert, valid)
    y0, y1 = _gather_rows(yb, dests)
    return _combine(x2, y0, y1, gate[:, :TOP_K])


def kernel(x, mem, rel_bias, norm_mix, norm_mem, norm_ffn, w_mem_kv, xq_norm, xk_norm, w_out, hy_w_in, hy_conv_w, hy_conv_b, hy_filt_w1, hy_filt_b1, hy_filt_w2, hy_filt_b2, hy_filt_w3, hy_sin_freq, hy_skip, at_w_in, at_q_norm, at_k_norm, ffn_w_gate, ffn_w_up, ffn_w_down, moe_router, moe_w_gate, moe_w_up, moe_w_down):
    b, s, d = x.shape
    t = b * s
    m_len = mem.shape[1]
    x2 = x.reshape(t, d)
    mem2 = mem.reshape(b * m_len, d)
    bf = lambda w: w.astype(BF16)
    score_scale = HD ** -0.5
    xa_heads = XA_W // HD
    at_heads = AT_W // HD
    plain = lambda width: [None] * (width // HD)
    kv_gains = lambda i: [xk_norm[i]] * xa_heads + plain(XA_W)

    tables = _dft_tables(s)
    spectra, hq = _hyena_filters(s, tables, hy_filt_w1[0], hy_filt_b1[0], hy_filt_w2[0], hy_filt_b2[0],
                                 hy_filt_w3[0], hy_sin_freq[0])
    z = _norm_matmul(x2, norm_mix[0], bf(hy_w_in[0]), 1024,
                     plain(3 * HY_C) + [xq_norm[0] * score_scale] * xa_heads).reshape(b, s, -1)
    kv = _norm_matmul(mem2, norm_mem[0], bf(w_mem_kv[0]), 1024, kv_gains(0)).reshape(b, m_len, -1)
    self_out = _hyena(z, hy_conv_w[0], hy_conv_b[0], tables, spectra, hq, hy_skip[0])
    x2 = _mix_out(x2.reshape(b, s, d), self_out, z, 3 * HY_C // XA_W, kv, bf(w_out[0])).reshape(t, d)
    x2 = _ffn_dense(x2, norm_ffn[0], bf(ffn_w_gate[0]), bf(ffn_w_up[0]), bf(ffn_w_down[0]))

    at_gains = ([at_q_norm[0] * score_scale] * at_heads + [at_k_norm[0]] * at_heads + plain(AT_W)
                + [xq_norm[1] * score_scale] * xa_heads)
    z = _norm_matmul(x2, norm_mix[1], bf(at_w_in[0]), 1024, at_gains).reshape(b, s, -1)
    kv = _norm_matmul(mem2, norm_mem[1], bf(w_mem_kv[1]), 1024, kv_gains(1)).reshape(b, m_len, -1)
    self_out = _dilated_attention(z, rel_bias)
    x2 = _mix_out(x2.reshape(b, s, d), self_out, z, 3 * AT_W // XA_W, kv, bf(w_out[1])).reshape(t, d)
    x2 = _moe(x2, norm_ffn[1], moe_router[0], moe_w_gate[0], moe_w_up[0], moe_w_down[0])
    return x2.reshape(b, s, d)
```

```python
import dataclasses
import functools
import math

import jax
import jax.numpy as jnp
import numpy as np
from jax import lax
from jax.experimental import pallas as pl
from jax.experimental.pallas import tpu as pltpu
from jax.experimental.pallas import tpu_sc as plsc

F32 = jnp.float32
BF16 = jnp.bfloat16

D_MODEL = 1024
EPS = 1e-6
HY_C = 512
FILT_BANDS = 16
DECAY_TARGET = 1e-2
FAST_DECAY_PCT = 0.3
SLOW_DECAY_PCT = 1.5
MOD_SHIFT = 0.05
AT_GROUPS = ((128, 1), (512, 4), (2048, 16))
AT_HEADS = 4
HD = 128
AT_W = 1536
NUM_BUCKETS = 32
REL_MAX_DIST = 1024
NEG_INF = -1e30
XA_W = 512
D_FF = 2816
N_EXPERTS = 8
TOP_K = 2

VMEM_LIMIT_BYTES = 56 * 1024 * 1024
ROW_TILE = 512
FF_TILE = 1408
MOE_ROWS = 512
MOE_CHUNK = 512
SC_ROWS = 32
SC_LANES = 16
SC_PACK_ROWS = 16
ATT_TQ = 128
ATT_KW = 256
HY_TC = 256


def _params(*sem):
    return pltpu.CompilerParams(dimension_semantics=sem, vmem_limit_bytes=VMEM_LIMIT_BYTES)


def _rms_rows(x, gain):
    return x * lax.rsqrt(jnp.mean(x * x, axis=-1, keepdims=True) + EPS) * gain


def _bf16_bits(v):
    u = lax.bitcast_convert_type(v, jnp.uint32)
    return (u + jnp.uint32(0x7FFF) + ((u >> 16) & jnp.uint32(1))) >> 16


def _resident(shape, index_map):
    return pl.BlockSpec(shape, index_map, pipeline_mode=pl.Buffered(1))


def _norm_matmul_kernel(x_ref, g_ref, w_ref, hg_ref, o_ref, *, tn, head_norm):
    h = _rms_rows(x_ref[...], g_ref[...]).astype(BF16)
    for c in range(o_ref.shape[1] // tn):
        acc = jnp.dot(h, w_ref[:, c * tn:(c + 1) * tn], preferred_element_type=F32)
        for j in range(tn // HD):
            cols = slice(c * tn + j * HD, c * tn + (j + 1) * HD)
            seg = acc[:, j * HD:(j + 1) * HD]
            if head_norm[cols.start // HD]:
                seg = _rms_rows(seg, hg_ref[:, cols])
            o_ref[:, cols] = seg.astype(o_ref.dtype)


def _norm_matmul(x2, gain, w_bf, tn, head_gains):
    rows, d = x2.shape
    n = w_bf.shape[1]
    tm = min(ROW_TILE, rows)
    assert len(head_gains) * HD == n
    hg = jnp.concatenate([jnp.ones((HD,), F32) if g is None else g.astype(F32) for g in head_gains]).reshape(1, n)
    return pl.pallas_call(
        functools.partial(_norm_matmul_kernel, tn=tn, head_norm=tuple(g is not None for g in head_gains)),
        out_shape=jax.ShapeDtypeStruct((rows, n), BF16),
        grid=(rows // tm,),
        in_specs=[pl.BlockSpec((tm, d), lambda i: (i, 0)),
                  pl.BlockSpec((1, d), lambda i: (0, 0)),
                  _resident((d, n), lambda i: (0, 0)),
                  pl.BlockSpec((1, n), lambda i: (0, 0))],
        out_specs=pl.BlockSpec((tm, n), lambda i: (i, 0)),
        compiler_params=_params("parallel"),
        cost_estimate=pl.CostEstimate(flops=2 * rows * d * n, transcendentals=rows,
                                      bytes_accessed=4 * rows * d + 2 * d * n + 2 * rows * n),
        name="norm_matmul",
    )(x2, gain.reshape(1, d), w_bf, hg)


def _mix_out_kernel(x_ref, a_ref, xq_ref, kv_ref, wa_ref, wc_ref, o_ref):
    heads = []
    for h in range(XA_W // HD):
        cols = slice(h * HD, (h + 1) * HD)
        s = lax.dot_general(xq_ref[0, :, cols], kv_ref[0, :, cols], (((1,), (1,)), ((), ())),
                            preferred_element_type=F32)
        p = jnp.exp(s - jnp.max(s, axis=-1, keepdims=True))
        z = jnp.sum(p, axis=-1, keepdims=True)
        pv = jnp.dot(p.astype(BF16), kv_ref[0, :, XA_W + h * HD:XA_W + (h + 1) * HD], preferred_element_type=F32)
        heads.append((pv / z).astype(BF16))
    cross = jnp.concatenate(heads, axis=1)
    o_ref[0] = (x_ref[0]
                + jnp.dot(a_ref[0], wa_ref[...], preferred_element_type=F32)
                + jnp.dot(cross, wc_ref[...], preferred_element_type=F32))


def _mix_out(x3, self_out, z3, xq_block, kv3, w_bf):
    b, s, d = x3.shape
    half = self_out.shape[2]
    m = kv3.shape[1]
    tm = ROW_TILE
    return pl.pallas_call(
        _mix_out_kernel,
        out_shape=jax.ShapeDtypeStruct((b, s, d), F32),
        grid=(b, s // tm),
        in_specs=[pl.BlockSpec((1, tm, d), lambda i, j: (i, j, 0)),
                  pl.BlockSpec((1, tm, half), lambda i, j: (i, j, 0)),
                  pl.BlockSpec((1, tm, XA_W), lambda i, j: (i, j, xq_block)),
                  pl.BlockSpec((1, m, 2 * XA_W), lambda i, j: (i, 0, 0)),
                  pl.BlockSpec((half, d), lambda i, j: (0, 0)),
                  pl.BlockSpec((d - half, d), lambda i, j: (1, 0))],
        out_specs=pl.BlockSpec((1, tm, d), lambda i, j: (i, j, 0)),
        compiler_params=_params("parallel", "parallel"),
        name="mix_out",
    )(x3, self_out, z3, kv3, w_bf, w_bf)


def _weight_rows(ref, start, size):
    if ref.dtype == jnp.uint32:
        return lambda cols: pltpu.bitcast(ref[start // 2:(start + size) // 2, cols], BF16)
    return lambda cols: ref[start:start + size, cols]


def _weight_shape(ref):
    return (ref.shape[0] * (2 if ref.dtype == jnp.uint32 else 1), ref.shape[1])


def _swiglu(h, wg_ref, wu_ref, wd_ref, tf):
    d, f = _weight_shape(wg_ref)
    y = None
    for j in range(f // tf):
        cols = slice(j * tf, (j + 1) * tf)
        gg = jnp.dot(h, _weight_rows(wg_ref, 0, d)(cols), preferred_element_type=F32)
        uu = jnp.dot(h, _weight_rows(wu_ref, 0, d)(cols), preferred_element_type=F32)
        a = ((gg * jax.nn.sigmoid(gg)) * uu).astype(BF16)
        part = jnp.dot(a, _weight_rows(wd_ref, j * tf, tf)(slice(None)), preferred_element_type=F32)
        y = part if y is None else y + part
    return y


def _ffn_dense_kernel(x_ref, g_ref, wg_ref, wu_ref, wd_ref, o_ref, *, tf):
    x = x_ref[...]
    h = _rms_rows(x, g_ref[...]).astype(BF16)
    o_ref[...] = x + _swiglu(h, wg_ref, wu_ref, wd_ref, tf)


def _ffn_dense(x2, gain, wg, wu, wd):
    rows, d = x2.shape
    f = wg.shape[1]
    tm = ROW_TILE
    return pl.pallas_call(
        functools.partial(_ffn_dense_kernel, tf=FF_TILE),
        out_shape=jax.ShapeDtypeStruct((rows, d), F32),
        grid=(rows // tm,),
        in_specs=[pl.BlockSpec((tm, d), lambda i: (i, 0)),
                  pl.BlockSpec((1, d), lambda i: (0, 0)),
                  _resident((d, f), lambda i: (0, 0)),
                  _resident((d, f), lambda i: (0, 0)),
                  _resident((f, d), lambda i: (0, 0))],
        out_specs=pl.BlockSpec((tm, d), lambda i: (i, 0)),
        compiler_params=_params("parallel"),
        cost_estimate=pl.CostEstimate(flops=6 * rows * d * f, transcendentals=rows * f,
                                      bytes_accessed=8 * rows * d + 6 * d * f),
        name="ffn_dense",
    )(x2, gain.reshape(1, d), wg, wu, wd)


def _ffn_expert_kernel(eid_ref, valid_ref, hp_ref, wg_ref, wu_ref, wd_ref, o_ref, h_ref, *, tf):
    half = hp_ref.shape[1]
    valid = valid_ref[pl.program_id(0)]

    @pl.when(valid > 0)
    def _():
        keep = lax.broadcasted_iota(jnp.int32, (hp_ref.shape[0], 1), 0) < valid
        word = hp_ref[...]
        h_ref[:, :half] = jnp.where(keep, lax.bitcast_convert_type(word << 16, F32), 0.0).astype(BF16)
        h_ref[:, half:] = jnp.where(keep, lax.bitcast_convert_type(word & jnp.uint32(0xFFFF0000), F32),
                                    0.0).astype(BF16)
        y = _swiglu(h_ref[...], wg_ref.at[0], wu_ref.at[0], wd_ref.at[0], tf)
        o_ref[...] = _bf16_bits(y[:, :half]) | (_bf16_bits(y[:, half:]) << 16)

    @pl.when(valid <= 0)
    def _():
        o_ref[...] = jnp.zeros_like(o_ref)


def _ffn_experts(hp, wg, wu, wd, eid, valid):
    rows, half = hp.shape
    d = 2 * half
    tm = MOE_ROWS
    expert_spec = lambda w: pl.BlockSpec((1,) + w.shape[1:], lambda i, e, n: (e[i], 0, 0))
    grid_spec = pltpu.PrefetchScalarGridSpec(
        num_scalar_prefetch=2,
        grid=(rows // tm,),
        in_specs=[pl.BlockSpec((tm, half), lambda i, e, n: (i, 0)),
                  expert_spec(wg), expert_spec(wu), expert_spec(wd)],
        out_specs=pl.BlockSpec((tm, half), lambda i, e, n: (i, 0)),
        scratch_shapes=[pltpu.VMEM((tm, d), BF16)],
    )
    return pl.pallas_call(
        functools.partial(_ffn_expert_kernel, tf=FF_TILE),
        out_shape=jax.ShapeDtypeStruct((rows, half), jnp.uint32),
        grid_spec=grid_spec,
        compiler_params=_params("arbitrary"),
        name="ffn_experts",
    )(eid, valid, hp, wg, wu, wd)


def _dft_kernel(fa_ref, fb_ref, ie_ref, io_ref, base_ref, *, n_fft):
    rows, cols = base_ref.shape[1:]
    i = pl.program_id(0)
    theta = 2.0 * math.pi / n_fft

    def phases(r, c):
        return r * (2 * c), r * (2 * c + 1), c * (2 * r + 1)

    @pl.when(i == 0)
    def _():
        r = lax.broadcasted_iota(jnp.int32, (rows, cols), 0)
        c = lax.broadcasted_iota(jnp.int32, (rows, cols), 1)
        for f, ph in enumerate(phases(r, c)):
            ang = (ph & (n_fft - 1)).astype(F32) * theta
            base_ref[2 * f] = jnp.cos(ang)
            base_ref[2 * f + 1] = jnp.sin(ang)

    c = lax.broadcasted_iota(jnp.int32, (8, cols), 1)
    r0 = i * rows
    shifts = (r0 * (2 * c), r0 * (2 * c + 1), c * (2 * r0))
    tables = []
    for f in range(3):
        ang = (shifts[f] & (n_fft - 1)).astype(F32) * theta
        ca, sa = jnp.cos(ang)[0:1], jnp.sin(ang)[0:1]
        cb, sb = base_ref[2 * f], base_ref[2 * f + 1]
        tables.append(((cb * ca - sb * sa).astype(BF16), (sb * ca + cb * sa).astype(BF16)))
    (ce, se), (co, so), (cot, sot) = tables
    left, right = slice(0, cols), slice(cols, 2 * cols)
    fa_ref[:, left], fa_ref[:, right] = ce, co
    fb_ref[:, left], fb_ref[:, right] = se, so
    ie_ref[:, left], ie_ref[:, right] = ce, -se
    io_ref[:, left], io_ref[:, right] = cot, -sot


def _dft_tables(length):
    half, rows = length // 2, 128
    shape = jax.ShapeDtypeStruct((half, length), BF16)
    spec = pl.BlockSpec((rows, length), lambda i: (i, 0))
    return pl.pallas_call(
        functools.partial(_dft_kernel, n_fft=2 * length),
        out_shape=(shape,) * 4,
        grid=(half // rows,),
        out_specs=(spec,) * 4,
        scratch_shapes=[pltpu.VMEM((6, rows, half), F32)],
        compiler_params=_params("arbitrary"),
        name="dft_tables",
    )()


LANES = 128


def _split_parity(tmp_ref, x):
    half = x.shape[0] // 2
    for j in range(tmp_ref.shape[0]):
        tmp_ref[j] = x[:, j * LANES:(j + 1) * LANES]
    pick = lambda start: jnp.concatenate(
        [tmp_ref[j, pl.ds(start, half, stride=2), :] for j in range(tmp_ref.shape[0])], axis=1)
    return pick(0), pick(1)


def _merge_parity(tmp_ref, even, odd):
    half = even.shape[0]
    for j in range(tmp_ref.shape[0]):
        tmp_ref[j, pl.ds(0, half, stride=2), :] = even[:, j * LANES:(j + 1) * LANES]
        tmp_ref[j, pl.ds(1, half, stride=2), :] = odd[:, j * LANES:(j + 1) * LANES]
    return jnp.concatenate([tmp_ref[j] for j in range(tmp_ref.shape[0])], axis=1)


def _filter_time_kernel(feats_ref, w1_ref, b1_ref, w2_ref, b2_ref, fr_ref, w3_ref, t_ref, delta_ref,
                        g_ref, hq_ref, tmp_ref):
    hp = lax.Precision.HIGHEST
    length = feats_ref.shape[0]
    half = length // 2
    fr = fr_ref[...]
    h = jnp.sin(fr * (jnp.dot(feats_ref[...], w1_ref[...], preferred_element_type=F32, precision=hp) + b1_ref[...]))
    h = jnp.sin(fr * (jnp.dot(h, w2_ref[...], preferred_element_type=F32, precision=hp) + b2_ref[...]))
    mod = jnp.exp(-t_ref[...] * delta_ref[...]) + MOD_SHIFT
    row = lax.broadcasted_iota(jnp.int32, (length, 1), 0)
    alt = jnp.where(lax.broadcasted_iota(jnp.int32, (half, 1), 0) % 2 == 0, 1.0, -1.0).astype(F32)
    for o in range(2):
        fwd = jnp.dot(h, w3_ref[2 * o], preferred_element_type=F32, precision=hp) * mod
        bwd = jnp.dot(h, w3_ref[2 * o + 1], preferred_element_type=F32, precision=hp) * mod
        bwd = jnp.where(row == 0, 0.0, bwd)
        norm = (jnp.sum(jnp.abs(fwd), axis=0, keepdims=True)
                + jnp.sum(jnp.abs(bwd), axis=0, keepdims=True) + 1e-6)
        fwd = fwd / norm
        bwd = bwd / norm
        for part, g in enumerate((fwd + bwd, fwd - bwd)):
            even, odd = _split_parity(tmp_ref, g)
            g_ref[o, 2 * part] = even.astype(BF16)
            g_ref[o, 2 * part + 1] = odd.astype(BF16)
            mid = jnp.sum((even if part == 0 else odd) * alt, axis=0, keepdims=True) * (1.0 / length)
            hq_ref[o, part] = mid if part == 0 else -mid


def _parity_operand(op_ref, even, odd):
    half, tc = even.shape
    op_ref[:half, :tc] = even
    op_ref[:half, tc:] = even
    op_ref[half:, :tc] = odd
    op_ref[half:, tc:] = -odd
    return op_ref[...]


def _filter_freq_kernel(fa_ref, fb_ref, g_ref, h_ref, op_ref):
    half = fa_ref.shape[0]
    tc = g_ref.shape[3]
    row = lax.broadcasted_iota(jnp.int32, (half, 1), 0)
    scale = jnp.where(row == 0, 1.0, 2.0).astype(F32) * (1.0 / (4 * half))
    a = jnp.dot(fa_ref[...], _parity_operand(op_ref, g_ref[0, 0], g_ref[0, 1]), preferred_element_type=F32)
    b = jnp.dot(fb_ref[...], _parity_operand(op_ref, g_ref[0, 2], g_ref[0, 3]), preferred_element_type=F32)
    h_ref[0, 0] = a[:, :tc] * scale
    h_ref[0, 1] = -b[:, :tc] * scale
    h_ref[0, 2] = a[:, tc:] * scale
    h_ref[0, 3] = b[:, tc:] * scale


def _hyena_filters(length, tables, w1, b1, w2, b2, w3, freq):
    t = jnp.linspace(0.0, 1.0, length, dtype=F32)[:, None]
    f = jnp.linspace(1e-4, FILT_BANDS - 1, FILT_BANDS, dtype=F32)[None]
    ang = (2.0 * math.pi / length) * jnp.arange(length, dtype=F32)[:, None] * f
    feats = jnp.concatenate([t, jnp.cos(ang), -jnp.sin(ang)], axis=-1)
    deltas = jnp.abs(jnp.linspace(math.log(DECAY_TARGET) / SLOW_DECAY_PCT,
                                  math.log(DECAY_TARGET) / FAST_DECAY_PCT, HY_C, dtype=F32))[None]
    hid = w1.shape[1]
    w3r = w3.reshape(hid, 4, HY_C).transpose(1, 0, 2)
    tc = HY_TC
    full = lambda shape: pl.BlockSpec(shape, lambda c: (0,) * len(shape))
    half = length // 2
    g, hq = pl.pallas_call(
        _filter_time_kernel,
        out_shape=(jax.ShapeDtypeStruct((2, 4, half, HY_C), BF16),
                   jax.ShapeDtypeStruct((2, 2, 1, HY_C), F32)),
        grid=(HY_C // tc,),
        in_specs=[full(feats.shape), full(w1.shape), full((1, hid)), full(w2.shape), full((1, hid)),
                  full((1, hid)), pl.BlockSpec((4, hid, tc), lambda c: (0, 0, c)), full((length, 1)),
                  pl.BlockSpec((1, tc), lambda c: (0, c))],
        out_specs=(pl.BlockSpec((2, 4, half, tc), lambda c: (0, 0, 0, c)),
                   pl.BlockSpec((2, 2, 1, tc), lambda c: (0, 0, 0, c))),
        scratch_shapes=[pltpu.VMEM((tc // LANES, length, LANES), F32)],
        compiler_params=_params("parallel"),
        name="filter_time",
    )(feats, w1, b1.reshape(1, hid), w2, b2.reshape(1, hid), freq.reshape(1, hid), w3r, t, deltas)
    table_spec = _resident((half, length), lambda o, c: (0, 0))
    spectra = pl.pallas_call(
        _filter_freq_kernel,
        out_shape=jax.ShapeDtypeStruct((2, 4, half, HY_C), F32),
        grid=(2, HY_C // tc),
        in_specs=[table_spec] * 2 + [pl.BlockSpec((1, 4, half, tc), lambda o, c: (o, 0, 0, c))],
        out_specs=pl.BlockSpec((1, 4, half, tc), lambda o, c: (o, 0, 0, c)),
        scratch_shapes=[pltpu.VMEM((length, 2 * tc), BF16)],
        compiler_params=_params("parallel", "parallel"),
        name="filter_freq",
    )(tables[0], tables[1], g)
    return spectra, hq


def _hyena_kernel(z0_ref, z1_ref, z2_ref, cw_ref, cb_ref, fa_ref, fb_ref, ie_ref, io_ref,
                  h_ref, hq_ref, skip_ref, o_ref, tmp_ref, op_ref):
    length = z0_ref.shape[1]
    half = length // 2
    row = lax.broadcasted_iota(jnp.int32, (length, 1), 0)
    alt = jnp.where(lax.broadcasted_iota(jnp.int32, (half, 1), 0) % 2 == 0, 1.0, -1.0).astype(F32)

    def short_conv(z_ref, c):
        u = z_ref[0].astype(F32)
        prev = jnp.where(row == 0, 0.0, pltpu.roll(u, 1, axis=0))
        nxt = jnp.where(row == length - 1, 0.0, pltpu.roll(u, length - 1, axis=0))
        return cb_ref[c] + (prev * cw_ref[0, c] + u * cw_ref[1, c] + nxt * cw_ref[2, c])

    def long_conv(u, o):
        tc = u.shape[1]
        ue, uo = _split_parity(tmp_ref, u)
        operand = _parity_operand(op_ref, ue.astype(BF16), uo.astype(BF16))
        a = jnp.dot(fa_ref[...], operand, preferred_element_type=F32)
        b = jnp.dot(fb_ref[...], operand, preferred_element_type=F32)
        a_f, a_g, b_f, nb_g = a[:, :tc], a[:, tc:], b[:, :tc], b[:, tc:]
        hr_f, hi_f, hr_g, hi_g = (h_ref[o, j] for j in range(4))
        re_f, im_f = a_f * hr_f + b_f * hi_f, a_f * hi_f - b_f * hr_f
        re_g, im_g = a_g * hr_g - nb_g * hi_g, a_g * hi_g + nb_g * hr_g
        a4 = jnp.sum(ue * alt, axis=0, keepdims=True)
        b4 = jnp.sum(uo * alt, axis=0, keepdims=True)
        hr4, hi4 = hq_ref[o, 0], hq_ref[o, 1]
        stack = lambda re, im: jnp.concatenate([re.astype(BF16), im.astype(BF16)], axis=0)
        y_even = (jnp.dot(ie_ref[...], stack(re_f + re_g, im_f - im_g), preferred_element_type=F32)
                  + alt * (a4 * hr4 + b4 * hi4))
        y_odd = (jnp.dot(io_ref[...], stack(re_f - re_g, im_f + im_g), preferred_element_type=F32)
                 - alt * (a4 * hi4 - b4 * hr4))
        return _merge_parity(tmp_ref, y_even, y_odd) + u * skip_ref[o]

    z = short_conv(z1_ref, 1) * long_conv(short_conv(z0_ref, 0), 0)
    o_ref[0] = (short_conv(z2_ref, 2) * long_conv(z, 1)).astype(o_ref.dtype)


def _hyena(z3, conv_w, conv_b, tables, spectra, hq, skip):
    b, s, _ = z3.shape
    half = s // 2
    tc = HY_TC
    nct = HY_C // tc
    cw = conv_w.reshape(3, 3, 1, HY_C)
    cb = conv_b.reshape(3, 1, HY_C)
    zspec = lambda chunk: pl.BlockSpec((1, s, tc), lambda c, i: (i, 0, chunk * nct + c))
    return pl.pallas_call(
        _hyena_kernel,
        out_shape=jax.ShapeDtypeStruct((b, s, HY_C), BF16),
        grid=(nct, b),
        in_specs=[zspec(0), zspec(1), zspec(2),
                  pl.BlockSpec((3, 3, 1, tc), lambda c, i: (0, 0, 0, c)),
                  pl.BlockSpec((3, 1, tc), lambda c, i: (0, 0, c))]
                 + [_resident((half, s), lambda c, i: (0, 0))] * 4
                 + [_resident((2, 4, half, tc), lambda c, i: (0, 0, 0, c)),
                    pl.BlockSpec((2, 2, 1, tc), lambda c, i: (0, 0, 0, c)),
                    pl.BlockSpec((2, 1, tc), lambda c, i: (0, 0, c))],
        out_specs=pl.BlockSpec((1, s, tc), lambda c, i: (i, 0, c)),
        scratch_shapes=[pltpu.VMEM((tc // LANES, s, LANES), F32), pltpu.VMEM((s, 2 * tc), BF16)],
        compiler_params=_params("parallel", "parallel"),
        cost_estimate=pl.CostEstimate(flops=2 * 4 * 2 * b * s * half * HY_C, transcendentals=0,
                                      bytes_accessed=2 * 3 * b * s * HY_C + 2 * b * s * HY_C + 8 * half * s),
        name="hyena",
    )(z3, z3, z3, cw, cb, *tables, spectra, hq, skip.reshape(2, 1, HY_C))


def _rel_bucket(rel):
    half = NUM_BUCKETS // 2
    exact = half // 2
    n = np.abs(rel)
    large = exact + (np.log(np.maximum(n, 1) / exact) / np.log(REL_MAX_DIST / exact) * (half - exact)).astype(np.int32)
    large = np.minimum(large, half - 1)
    return (np.where(rel > 0, half, 0) + np.where(n < exact, n, large)).astype(np.int32)


def _att_tiles(length):
    kw = min(ATT_KW, length)
    tiles = []
    for qs in range(0, length, ATT_TQ):
        ks = min(max(qs - (kw - ATT_TQ) // 2, 0), length - kw)
        tiles.append((qs, ks, {0: 0, -64: 1, -128: 2}[ks - qs]))
    return kw, tiles


def _att_bias(rel_bias, group, seq):
    window, dil = AT_GROUPS[group]
    band = window // (2 * dil)
    kw, tiles = _att_tiles(seq // dil)
    offsets = sorted({ks - qs for qs, ks, _ in tiles}, reverse=True)
    table = rel_bias[:, group * AT_HEADS:(group + 1) * AT_HEADS].astype(F32)
    span = kw + ATT_TQ
    k = np.arange(span)
    rel = np.asarray(offsets)[:, None] + np.where(k < kw, k, k - span)[None, :]
    diag = jnp.where((np.abs(rel) <= band)[:, None, :],
                     jnp.swapaxes(table[_rel_bucket(rel * dil)], 1, 2), NEG_INF)
    return diag[:, :, None, :]


def _bias_tile(diag_ref, var, kw):
    rows = jnp.broadcast_to(diag_ref[var, 0], (ATT_TQ, diag_ref.shape[3]))
    return pltpu.roll(rows, 0, 1, stride=1, stride_axis=0)[:, :kw]


def _dilated_kernel(q1, q2, q3, k1, k2, k3, v1, v2, v3, b1, b2, b3, o_ref,
                    qn_ref, kn_ref, vn_ref, og_ref, lg_ref):
    seq = o_ref.shape[1]
    contract_last = (((1,), (1,)), ((), ()))
    for g, (q_ref, k_ref, v_ref, bias_ref) in enumerate(((q1, k1, v1, b1), (q2, k2, v2, b2), (q3, k3, v3, b3))):
        dil = AT_GROUPS[g][1]
        kw, tiles = _att_tiles(seq // dil)
        qn_ref[...] = q_ref[0].astype(F32)
        kn_ref[...] = k_ref[0].astype(F32)
        vn_ref[...] = v_ref[0].astype(F32)
        bias = {var: _bias_tile(bias_ref, var, kw) for var in sorted({var for _, _, var in tiles})}
        for r in range(dil):
            for qs, ks, var in tiles:
                rows = lambda start, size: (pl.ds(r + start * dil, size, stride=dil) if dil > 1
                                            else pl.ds(start, size))
                qt = qn_ref[rows(qs, ATT_TQ), :].astype(BF16)
                kt = kn_ref[rows(ks, kw), :].astype(BF16)
                vt = vn_ref[rows(ks, kw), :].astype(BF16)
                s = lax.dot_general(qt, kt, contract_last, preferred_element_type=F32) + bias[var]
                m = jnp.max(s, axis=-1, keepdims=True)
                p = jnp.exp(s - m)
                z = jnp.sum(p, axis=-1, keepdims=True)
                og_ref[g, rows(qs, ATT_TQ), :] = jnp.dot(p.astype(BF16), vt, preferred_element_type=F32) / z
                lg_ref[g, rows(qs, ATT_TQ), :] = jnp.broadcast_to(m + jnp.log(z), (ATT_TQ, HD))
    l0, l1, l2 = lg_ref[0], lg_ref[1], lg_ref[2]
    mx = jnp.maximum(jnp.maximum(l0, l1), l2)
    e0, e1, e2 = jnp.exp(l0 - mx), jnp.exp(l1 - mx), jnp.exp(l2 - mx)
    o_ref[0] = ((e0 * og_ref[0] + e1 * og_ref[1] + e2 * og_ref[2]) / (e0 + e1 + e2)).astype(o_ref.dtype)


def _dilated_attention(z3, rel_bias):
    b, s, _ = z3.shape
    ng = len(AT_GROUPS)
    col = lambda part, g: pl.BlockSpec((1, s, HD), lambda i, h: (i, 0, part * ng * AT_HEADS + g * AT_HEADS + h))
    biases = [_att_bias(rel_bias, g, s) for g in range(ng)]
    bias_spec = lambda a: pl.BlockSpec((a.shape[0], 1) + a.shape[2:], lambda i, h: (0, h, 0, 0))
    return pl.pallas_call(
        _dilated_kernel,
        out_shape=jax.ShapeDtypeStruct((b, s, AT_HEADS * HD), BF16),
        grid=(b, AT_HEADS),
        in_specs=[col(p, g) for p in range(3) for g in range(ng)] + [bias_spec(a) for a in biases],
        out_specs=pl.BlockSpec((1, s, HD), lambda i, h: (i, 0, h)),
        scratch_shapes=[pltpu.VMEM((s, HD), F32), pltpu.VMEM((s, HD), F32), pltpu.VMEM((s, HD), F32),
                        pltpu.VMEM((ng, s, HD), F32), pltpu.VMEM((ng, s, HD), F32)],
        compiler_params=_params("parallel", "parallel"),
        name="dilated_attention",
    )(*([z3] * 9), *biases)


ROUTER_LANES = 128


def _router_kernel(x_ref, g_ref, w_ref, hp_ref, idx_ref, gate_ref, count_ref, carry_ref):
    tm = x_ref.shape[0]
    half = hp_ref.shape[1]

    @pl.when(pl.program_id(0) == 0)
    def _():
        carry_ref[...] = jnp.zeros_like(carry_ref)

    h = _rms_rows(x_ref[...], g_ref[...])
    hp_ref[...] = _bf16_bits(h[:, :half]) | (_bf16_bits(h[:, half:]) << 16)
    h_hi = h.astype(BF16)
    h_lo = (h - h_hi.astype(F32)).astype(BF16)
    logits = (jnp.dot(h_hi, w_ref[0], preferred_element_type=F32)
              + (jnp.dot(h_hi, w_ref[1], preferred_element_type=F32)
                 + jnp.dot(h_lo, w_ref[0], preferred_element_type=F32)))
    lane =lax.broadcasted_iota(jnp.int32, logits.shape, 1).astype(F32)
    logits = jnp.where(lane < N_EXPERTS, logits, -jnp.inf)
    m1 = jnp.max(logits, axis=-1, keepdims=True)
    i1 = jnp.min(jnp.where(logits == m1, lane, float(ROUTER_LANES)), axis=-1, keepdims=True)
    rest = jnp.where(lane == i1, -jnp.inf, logits)
    m2 = jnp.max(rest, axis=-1, keepdims=True)
    i2 = jnp.min(jnp.where(rest == m2, lane, float(ROUTER_LANES)), axis=-1, keepdims=True)
    e2 = jnp.exp(m2 - m1)
    den = 1.0 + e2
    gate_ref[...] = jnp.where(lane == 0, 1.0 / den, e2 / den)
    chosen = jnp.where((lane == i1) | (lane == i2), 1.0, 0.0)
    earlier = lax.broadcasted_iota(jnp.int32, (tm, tm), 0) > lax.broadcasted_iota(jnp.int32, (tm, tm), 1)
    carry = carry_ref[...]
    before = jnp.dot(jnp.where(earlier, 1.0, 0.0).astype(BF16), chosen.astype(BF16),
                     preferred_element_type=F32) + carry
    r1 = jnp.sum(jnp.where(lane == i1, before, 0.0), axis=-1, keepdims=True)
    r2 = jnp.sum(jnp.where(lane == i2, before, 0.0), axis=-1, keepdims=True)
    idx_ref[...] = jnp.where(lane == 0, i1, jnp.where(lane == 1, i2, jnp.where(lane == 2, r1, r2))).astype(jnp.int32)
    carry = carry + jnp.sum(chosen, axis=0, keepdims=True)
    carry_ref[...] = carry
    count_ref[...] = carry.astype(jnp.int32)


def _router(x2, gain, router):
    rows, d = x2.shape
    tm = MOE_CHUNK
    w = jnp.zeros((d, ROUTER_LANES), F32).at[:, :N_EXPERTS].set(router.astype(F32))
    w_hi = w.astype(BF16)
    w = jnp.stack([w_hi, (w - w_hi.astype(F32)).astype(BF16)])
    row_spec = lambda width: pl.BlockSpec((tm, width), lambda i: (i, 0))
    return pl.pallas_call(
        _router_kernel,
        out_shape=(jax.ShapeDtypeStruct((rows, d // 2), jnp.uint32),
                   jax.ShapeDtypeStruct((rows, ROUTER_LANES), jnp.int32),
                   jax.ShapeDtypeStruct((rows, ROUTER_LANES), F32),
                   jax.ShapeDtypeStruct((1, ROUTER_LANES), jnp.int32)),
        grid=(rows // tm,),
        in_specs=[row_spec(d),
                  pl.BlockSpec((1, d), lambda i: (0, 0)),
                  pl.BlockSpec((2, d, ROUTER_LANES), lambda i: (0, 0, 0))],
        out_specs=(row_spec(d // 2), row_spec(ROUTER_LANES), row_spec(ROUTER_LANES),
                   pl.BlockSpec((1, ROUTER_LANES), lambda i: (0, 0))),
        scratch_shapes=[pltpu.VMEM((1, ROUTER_LANES), F32)],
        compiler_params=_params("arbitrary"),
        name="router",
    )(x2, gain.reshape(1, d), w)


def _sc_workers():
    info = plsc.get_sparse_core_info()
    mesh = plsc.VectorSubcoreMesh(core_axis_name="core", subcore_axis_name="subcore")
    return mesh, info.num_cores, info.num_subcores


def _pack_weight_rows(w, col_tile):
    r, c = w.shape
    mesh, nc, ns = _sc_workers()
    rs, ncol = SC_PACK_ROWS, c // col_tile
    per_w = (r // rs) * ncol // (nc * ns)
    assert per_w * nc * ns * rs * col_tile == r * c and per_w % 2 == 0 and col_tile % SC_LANES == 0
    params = pltpu.CompilerParams()
    if "needs_layout_passes" in pltpu.CompilerParams.__dataclass_fields__:
        params = dataclasses.replace(params, needs_layout_passes=False)

    @functools.partial(
        pl.kernel, mesh=mesh, out_type=jax.ShapeDtypeStruct((r // 2, c), jnp.uint32), compiler_params=params,
        cost_estimate=pl.CostEstimate(flops=r * c, transcendentals=0, bytes_accessed=6 * r * c),
        scratch_types=[pltpu.VMEM((2, rs, col_tile), F32), pltpu.VMEM((2, rs // 2, col_tile), jnp.uint32),
                       pltpu.SemaphoreType.DMA((2,)), pltpu.SemaphoreType.DMA((2,))])
    def pack(w_hbm, o_hbm, in_v, out_v, rsem, wsem):
        wid = lax.axis_index("subcore") * nc + lax.axis_index("core")

        @pl.loop(0, per_w // 2)
        def _(it):
            reads, writes = [], []
            for b in range(2):
                tile = wid * per_w + 2 * it + b
                r0 = pl.multiple_of((tile // ncol) * rs, rs)
                c0 = pl.multiple_of((tile % ncol) * col_tile, col_tile)
                reads.append(pltpu.make_async_copy(w_hbm.at[pl.ds(r0, rs), pl.ds(c0, col_tile)], in_v.at[b],
                                                   rsem.at[b]))
                writes.append(pltpu.make_async_copy(
                    out_v.at[b], o_hbm.at[pl.ds(pl.multiple_of(r0 // 2, rs // 2), rs // 2), pl.ds(c0, col_tile)],
                    wsem.at[b]))
            reads[0].start()
            reads[1].start()
            for b in range(2):
                reads[b].wait()
                for pair in range(rs // 2):
                    @plsc.parallel_loop(0, col_tile, step=SC_LANES, unroll=8)
                    def _(j):
                        packed = plsc.pack(in_v[b, 2 * pair, pl.ds(j, SC_LANES)],
                                           in_v[b, 2 * pair + 1, pl.ds(j, SC_LANES)],
                                           format=plsc.PackFormat.INTERLEAVED)
                        out_v[b, pair, pl.ds(j, SC_LANES)] = plsc.bitcast(packed, jnp.uint32)
                writes[b].start()
            writes[0].wait()
            writes[1].wait()

    return pack(w)


def _sc_token_rows(t, nc, ns):
    per_w = t // (nc * ns)
    assert per_w * nc * ns == t and per_w % (2 * SC_ROWS) == 0
    return per_w, per_w // SC_ROWS


def _scatter_rows(table, dests, p_rows, after=()):
    t, w = table.shape
    nk = len(dests)
    mesh, nc, ns = _sc_workers()
    per_w, nit = _sc_token_rows(t, nc, ns)
    ch = SC_ROWS

    @functools.partial(
        pl.kernel, mesh=mesh, out_type=jax.ShapeDtypeStruct((p_rows, w), table.dtype),
        scratch_types=[pltpu.VMEM((nk, nit, ch), jnp.int32), pltpu.VMEM((2, ch, w), table.dtype),
                       pltpu.SemaphoreType.DMA((2,)), pltpu.SemaphoreType.DMA((2, nk))])
    def scatter(table_hbm, *refs):
        dest_hbm, out_hbm = refs[:nk], refs[nk + len(after)]
        idx_v, rows_v, rsem, wsem = refs[nk + len(after) + 1:]
        wid = lax.axis_index("subcore") * nc + lax.axis_index("core")
        for k in range(nk):
            pltpu.sync_copy(dest_hbm[k].at[pl.ds(wid * nit, nit)], idx_v.at[k])

        @pl.loop(0, nit // 2)
        def _(it):
            reads = [pltpu.make_async_copy(table_hbm.at[pl.ds(wid * per_w + (2 * it + b) * ch, ch)],
                                           rows_v.at[b], rsem.at[b]) for b in range(2)]
            writes = [[pltpu.make_async_copy(rows_v.at[b], out_hbm.at[idx_v.at[k].at[2 * it + b]], wsem.at[b, k])
                       for k in range(nk)] for b in range(2)]
            reads[0].start()
            reads[1].start()
            for b in range(2):
                reads[b].wait()
                for k in range(nk):
                    writes[b][k].start()
            for b in range(2):
                for k in range(nk):
                    writes[b][k].wait()

    return scatter(table, *[d.reshape(t // ch, ch) for d in dests], *after)


def _gather_rows(table, idxs):
    t = idxs[0].shape[0]
    w = table.shape[1]
    nk = len(idxs)
    mesh, nc, ns = _sc_workers()
    per_w, nit = _sc_token_rows(t, nc, ns)
    ch = SC_ROWS

    @functools.partial(
        pl.kernel, mesh=mesh, out_type=[jax.ShapeDtypeStruct((t, w), table.dtype)] * nk,
        scratch_types=[pltpu.VMEM((nk, nit, ch), jnp.int32), pltpu.VMEM((nk, 2, ch, w), table.dtype),
                       pltpu.SemaphoreType.DMA((nk, 2)), pltpu.SemaphoreType.DMA((nk, 2))])
    def gather(table_hbm, *refs):
        idx_hbm, out_hbm = refs[:nk], refs[nk:2 * nk]
        idx_v, rows_v, rsem, wsem = refs[2 * nk:]
        wid = lax.axis_index("subcore") * nc + lax.axis_index("core")
        for k in range(nk):
            pltpu.sync_copy(idx_hbm[k].at[pl.ds(wid * nit, nit)], idx_v.at[k])

        @pl.loop(0, nit // 2)
        def _(it):
            slots = [(k, b) for k in range(nk) for b in range(2)]
            reads = {(k, b): pltpu.make_async_copy(table_hbm.at[idx_v.at[k].at[2 * it + b]], rows_v.at[k, b],
                                                   rsem.at[k, b]) for k, b in slots}
            writes = {(k, b): pltpu.make_async_copy(rows_v.at[k, b],
                                                    out_hbm[k].at[pl.ds(wid * per_w + (2 * it + b) * ch, ch)],
                                                    wsem.at[k, b]) for k, b in slots}
            for s in slots:
                reads[s].start()
            for s in slots:
                reads[s].wait()
                writes[s].start()
            for s in slots:
                writes[s].wait()

    return gather(table, *[i.reshape(t // ch, ch) for i in idxs])


def _combine_kernel(x_ref, y0_ref, y1_ref, gate_ref, o_ref):
    half = x_ref.shape[1] // 2
    for part in range(2):
        cols = slice(part * half, (part + 1) * half)
        acc = x_ref[:, cols]
        for k, y_ref in enumerate((y0_ref, y1_ref)):
            word = y_ref[...]
            bits = (word << 16) if part == 0 else (word & jnp.uint32(0xFFFF0000))
            acc = acc + gate_ref[:, k:k + 1] * lax.bitcast_convert_type(bits, F32)
        o_ref[:, cols] = acc


def _combine(x2, y0, y1, gates):
    t, d = x2.shape
    tm = ROW_TILE
    return pl.pallas_call(
        _combine_kernel,
        out_shape=jax.ShapeDtypeStruct((t, d), F32),
        grid=(t // tm,),
        in_specs=[pl.BlockSpec((tm, d), lambda i: (i, 0)),
                  pl.BlockSpec((tm, d // 2), lambda i: (i, 0)),
                  pl.BlockSpec((tm, d // 2), lambda i: (i, 0)),
                  pl.BlockSpec((tm, TOP_K), lambda i: (i, 0))],
        out_specs=pl.BlockSpec((tm, d), lambda i: (i, 0)),
        compiler_params=_params("parallel"),
        name="moe_combine",
    )(x2, y0, y1, gates)


def _moe(x2, gain, router, wg, wu, wd):
    t, d = x2.shape
    hp, idx, gate, count = _router(x2, gain, router)
    experts = jnp.arange(N_EXPERTS, dtype=jnp.int32)
    counts = count[0, :N_EXPERTS]
    padded = (counts + MOE_ROWS - 1) // MOE_ROWS * MOE_ROWS
    pend = jnp.cumsum(padded)
    pstart = pend - padded
    expert, rank = idx[:, :TOP_K], idx[:, TOP_K:2 * TOP_K]
    dest = jnp.sum(jnp.where(expert[:, :, None] == experts, pstart, 0), axis=-1) + rank
    dests = [dest[:, k] for k in range(TOP_K)]
    p_rows = t * TOP_K + N_EXPERTS * MOE_ROWS
    blk_row = jnp.arange(p_rows // MOE_ROWS, dtype=jnp.int32) * MOE_ROWS
    blk_expert = jnp.minimum(jnp.sum(pend[None, :] <= blk_row[:, None], axis=1), N_EXPERTS - 1).astype(jnp.int32)
    valid = jnp.clip(counts[blk_expert] - (blk_row - pstart[blk_expert]), 0, MOE_ROWS)
    valid = jnp.where(blk_row < pend[-1], valid, 0).astype(jnp.int32)
    n_exp, _, f = wg.shape
    pack = lambda w, col_tile: _pack_weight_rows(w.reshape(-1, w.shape[2]), col_tile).reshape(n_exp, -1, w.shape[2])
    wg, wu, wd = pack(wg, f // 2), pack(wu, f // 2), pack(wd, d)
    hb = _scatter_rows(hp, dests, p_rows, after=(wg, wu, wd))
    yb = _ffn_experts(hb, wg, wu, wd, blk_expert, valid)
    y0, y1 = _gather_rows(yb, dests)
    return _combine(x2, y0, y1, gate[:, :TOP_K])


def kernel(x, mem, rel_bias, norm_mix, norm_mem, norm_ffn, w_mem_kv, xq_norm, xk_norm, w_out, hy_w_in, hy_conv_w, hy_conv_b, hy_filt_w1, hy_filt_b1, hy_filt_w2, hy_filt_b2, hy_filt_w3, hy_sin_freq, hy_skip, at_w_in, at_q_norm, at_k_norm, ffn_w_gate, ffn_w_up, ffn_w_down, moe_router, moe_w_gate, moe_w_up, moe_w_down):
    b, s, d = x.shape
    t = b * s
    m_len = mem.shape[1]
    x2 = x.reshape(t, d)
    mem2 = mem.reshape(b * m_len, d)
    bf = lambda w: w.astype(BF16)
    score_scale = HD ** -0.5
    xa_heads = XA_W // HD
    at_heads = AT_W // HD
    plain = lambda width: [None] * (width // HD)
    kv_gains = lambda i: [xk_norm[i]] * xa_heads + plain(XA_W)

    tables = _dft_tables(s)
    spectra, hq = _hyena_filters(s, tables, hy_filt_w1[0], hy_filt_b1[0], hy_filt_w2[0], hy_filt_b2[0],
                                 hy_filt_w3[0], hy_sin_freq[0])
    z = _norm_matmul(x2, norm_mix[0], bf(hy_w_in[0]), 1024,
                     plain(3 * HY_C) + [xq_norm[0] * score_scale] * xa_heads).reshape(b, s, -1)
    kv = _norm_matmul(mem2, norm_mem[0], bf(w_mem_kv[0]), 1024, kv_gains(0)).reshape(b, m_len, -1)
    self_out = _hyena(z, hy_conv_w[0], hy_conv_b[0], tables, spectra, hq, hy_skip[0])
    x2 = _mix_out(x2.reshape(b, s, d), self_out, z, 3 * HY_C // XA_W, kv, bf(w_out[0])).reshape(t, d)
    x2 = _ffn_dense(x2, norm_ffn[0], bf(ffn_w_gate[0]), bf(ffn_w_up[0]), bf(ffn_w_down[0]))

    at_gains = ([at_q_norm[0] * score_scale] * at_heads + [at_k_norm[0]] * at_heads + plain(AT_W)
                + [xq_norm[1] * score_scale] * xa_heads)
    z = _norm_matmul(x2, norm_mix[1], bf(at_w_in[0]), 1024, at_gains).reshape(b, s, -1)
    kv = _norm_matmul(mem2, norm_mem[1], bf(w_mem_kv[1]), 1024, kv_gains(1)).reshape(b, m_len, -1)
    self_out = _dilated_attention(z, rel_bias)
    x2 = _mix_out(x2.reshape(b, s, d), self_out, z, 3 * AT_W // XA_W, kv, bf(w_out[1])).reshape(t, d)
    x2 = _moe(x2, norm_ffn[1], moe_router[0], moe_w_gate[0], moe_w_up[0], moe_w_down[0])
    return x2.reshape(b, s, d)
```

```python
import dataclasses
import functools
import math

import jax
import jax.numpy as jnp
import numpy as np
from jax import lax
from jax.experimental import pallas as pl
from jax.experimental.pallas import tpu as pltpu
from jax.experimental.pallas import tpu_sc as plsc

F32 = jnp.float32
BF16 = jnp.bfloat16

D_MODEL = 1024
EPS = 1e-6
HY_C = 512
FILT_BANDS = 16
DECAY_TARGET = 1e-2
FAST_DECAY_PCT = 0.3
SLOW_DECAY_PCT = 1.5
MOD_SHIFT = 0.05
AT_GROUPS = ((128, 1), (512, 4), (2048, 16))
AT_HEADS = 4
HD = 128
AT_W = 1536
NUM_BUCKETS = 32
REL_MAX_DIST = 1024
NEG_INF = -1e30
XA_W = 512
D_FF = 2816
N_EXPERTS = 8
TOP_K = 2

VMEM_LIMIT_BYTES = 56 * 1024 * 1024
ROW_TILE = 512
FF_TILE = 1408
MOE_ROWS = 512
MOE_CHUNK = 512
MOE_TAIL_PARTS = 2
SC_ROWS = 32
SC_LANES = 16
SC_PACK_ROWS = 16
ATT_TQ = 128
ATT_KW = 256
HY_TC = 256
HY_KT = 512


def _params(*sem):
    return pltpu.CompilerParams(dimension_semantics=sem, vmem_limit_bytes=VMEM_LIMIT_BYTES)


def _rms_rows(x, gain):
    return x * lax.rsqrt(jnp.mean(x * x, axis=-1, keepdims=True) + EPS) * gain


def _bf16_bits(v):
    u = lax.bitcast_convert_type(v, jnp.uint32)
    return (u + jnp.uint32(0x7FFF) + ((u >> 16) & jnp.uint32(1))) >> 16


def _resident(shape, index_map):
    return pl.BlockSpec(shape, index_map, pipeline_mode=pl.Buffered(1))


def _norm_matmul_kernel(x_ref, g_ref, w_ref, hg_ref, o_ref, *, tn, head_norm):
    h = _rms_rows(x_ref[...], g_ref[...]).astype(BF16)
    for c in range(o_ref.shape[1] // tn):
        acc = jnp.dot(h, w_ref[:, c * tn:(c + 1) * tn], preferred_element_type=F32)
        for j in range(tn // HD):
            cols = slice(c * tn + j * HD, c * tn + (j + 1) * HD)
            seg = acc[:, j * HD:(j + 1) * HD]
            if head_norm[cols.start // HD]:
                seg = _rms_rows(seg, hg_ref[:, cols])
            o_ref[:, cols] = seg.astype(o_ref.dtype)


def _norm_matmul(x2, gain, w_bf, tn, head_gains):
    rows, d = x2.shape
    n = w_bf.shape[1]
    tm = min(ROW_TILE, rows)
    assert len(head_gains) * HD == n
    hg = jnp.concatenate([jnp.ones((HD,), F32) if g is None else g.astype(F32) for g in head_gains]).reshape(1, n)
    return pl.pallas_call(
        functools.partial(_norm_matmul_kernel, tn=tn, head_norm=tuple(g is not None for g in head_gains)),
        out_shape=jax.ShapeDtypeStruct((rows, n), BF16),
        grid=(rows // tm,),
        in_specs=[pl.BlockSpec((tm, d), lambda i: (i, 0)),
                  pl.BlockSpec((1, d), lambda i: (0, 0)),
                  _resident((d, n), lambda i: (0, 0)),
                  pl.BlockSpec((1, n), lambda i: (0, 0))],
        out_specs=pl.BlockSpec((tm, n), lambda i: (i, 0)),
        compiler_params=_params("parallel"),
        cost_estimate=pl.CostEstimate(flops=2 * rows * d * n, transcendentals=rows,
                                      bytes_accessed=4 * rows * d + 2 * d * n + 2 * rows * n),
        name="norm_matmul",
    )(x2, gain.reshape(1, d), w_bf, hg)


def _mix_out_kernel(x_ref, a_ref, xq_ref, kv_ref, wa_ref, wc_ref, o_ref):
    heads = []
    for h in range(XA_W // HD):
        cols = slice(h * HD, (h + 1) * HD)
        s = lax.dot_general(xq_ref[0, :, cols], kv_ref[0, :, cols], (((1,), (1,)), ((), ())),
                            preferred_element_type=F32)
        p = jnp.exp(s - jnp.max(s, axis=-1, keepdims=True))
        z = jnp.sum(p, axis=-1, keepdims=True)
        pv = jnp.dot(p.astype(BF16), kv_ref[0, :, XA_W + h * HD:XA_W + (h + 1) * HD], preferred_element_type=F32)
        heads.append((pv / z).astype(BF16))
    cross = jnp.concatenate(heads, axis=1)
    o_ref[0] = (x_ref[0]
                + jnp.dot(a_ref[0], wa_ref[...], preferred_element_type=F32)
                + jnp.dot(cross, wc_ref[...], preferred_element_type=F32))


def _mix_out(x3, self_out, z3, xq_block, kv3, w_bf):
    b, s, d = x3.shape
    half = self_out.shape[2]
    m = kv3.shape[1]
    tm = ROW_TILE
    return pl.pallas_call(
        _mix_out_kernel,
        out_shape=jax.ShapeDtypeStruct((b, s, d), F32),
        grid=(b, s // tm),
        in_specs=[pl.BlockSpec((1, tm, d), lambda i, j: (i, j, 0)),
                  pl.BlockSpec((1, tm, half), lambda i, j: (i, j, 0)),
                  pl.BlockSpec((1, tm, XA_W), lambda i, j: (i, j, xq_block)),
                  pl.BlockSpec((1, m, 2 * XA_W), lambda i, j: (i, 0, 0)),
                  pl.BlockSpec((half, d), lambda i, j: (0, 0)),
                  pl.BlockSpec((d - half, d), lambda i, j: (1, 0))],
        out_specs=pl.BlockSpec((1, tm, d), lambda i, j: (i, j, 0)),
        compiler_params=_params("parallel", "parallel"),
        name="mix_out",
    )(x3, self_out, z3, kv3, w_bf, w_bf)


def _weight_rows(ref, start, size):
    if ref.dtype == jnp.uint32:
        return lambda cols: pltpu.bitcast(ref[start // 2:(start + size) // 2, cols], BF16)
    return lambda cols: ref[start:start + size, cols]


def _weight_shape(ref):
    return (ref.shape[0] * (2 if ref.dtype == jnp.uint32 else 1), ref.shape[1])


def _swiglu(h, wg_ref, wu_ref, wd_ref, tf):
    d, f = _weight_shape(wg_ref)
    y = None
    for j in range(f // tf):
        cols = slice(j * tf, (j + 1) * tf)
        gg = jnp.dot(h, _weight_rows(wg_ref, 0, d)(cols), preferred_element_type=F32)
        uu = jnp.dot(h, _weight_rows(wu_ref, 0, d)(cols), preferred_element_type=F32)
        a = ((gg * jax.nn.sigmoid(gg)) * uu).astype(BF16)
        part = jnp.dot(a, _weight_rows(wd_ref, j * tf, tf)(slice(None)), preferred_element_type=F32)
        y = part if y is None else y + part
    return y


def _ffn_dense_kernel(x_ref, g_ref, wg_ref, wu_ref, wd_ref, o_ref, *, tf):
    x = x_ref[...]
    h = _rms_rows(x, g_ref[...]).astype(BF16)
    o_ref[...] = x + _swiglu(h, wg_ref, wu_ref, wd_ref, tf)


def _ffn_dense(x2, gain, wg, wu, wd):
    rows, d = x2.shape
    f = wg.shape[1]
    tm = ROW_TILE
    return pl.pallas_call(
        functools.partial(_ffn_dense_kernel, tf=FF_TILE),
        out_shape=jax.ShapeDtypeStruct((rows, d), F32),
        grid=(rows // tm,),
        in_specs=[pl.BlockSpec((tm, d), lambda i: (i, 0)),
                  pl.BlockSpec((1, d), lambda i: (0, 0)),
                  _resident((d, f), lambda i: (0, 0)),
                  _resident((d, f), lambda i: (0, 0)),
                  _resident((f, d), lambda i: (0, 0))],
        out_specs=pl.BlockSpec((tm, d), lambda i: (i, 0)),
        compiler_params=_params("parallel"),
        cost_estimate=pl.CostEstimate(flops=6 * rows * d * f, transcendentals=rows * f,
                                      bytes_accessed=8 * rows * d + 6 * d * f),
        name="ffn_dense",
    )(x2, gain.reshape(1, d), wg, wu, wd)


def _ffn_expert_kernel(eid_ref, valid_ref, hp_ref, wg_ref, wu_ref, wd_ref, o_ref, h_ref, *, tf):
    half = hp_ref.shape[1]
    valid = valid_ref[pl.program_id(0)]

    @pl.when(valid > 0)
    def _():
        keep = lax.broadcasted_iota(jnp.int32, (hp_ref.shape[0], 1), 0) < valid
        word = hp_ref[...]
        h_ref[:, :half] = jnp.where(keep, lax.bitcast_convert_type(word << 16, F32), 0.0).astype(BF16)
        h_ref[:, half:] = jnp.where(keep, lax.bitcast_convert_type(word & jnp.uint32(0xFFFF0000), F32),
                                    0.0).astype(BF16)
        y = _swiglu(h_ref[...], wg_ref.at[0], wu_ref.at[0], wd_ref.at[0], tf)
        o_ref[...] = _bf16_bits(y[:, :half]) | (_bf16_bits(y[:, half:]) << 16)

    @pl.when(valid <= 0)
    def _():
        o_ref[...] = jnp.zeros_like(o_ref)


def _ffn_experts(hp, wg, wu, wd, eid, valid):
    rows, half = hp.shape
    d = 2 * half
    tm = MOE_ROWS
    expert_spec = lambda w: pl.BlockSpec((1,) + w.shape[1:], lambda i, e, n: (e[i], 0, 0))
    grid_spec = pltpu.PrefetchScalarGridSpec(
        num_scalar_prefetch=2,
        grid=(rows // tm,),
        in_specs=[pl.BlockSpec((tm, half), lambda i, e, n: (i, 0)),
                  expert_spec(wg), expert_spec(wu), expert_spec(wd)],
        out_specs=pl.BlockSpec((tm, half), lambda i, e, n: (i, 0)),
        scratch_shapes=[pltpu.VMEM((tm, d), BF16)],
    )
    return pl.pallas_call(
        functools.partial(_ffn_expert_kernel, tf=FF_TILE),
        out_shape=jax.ShapeDtypeStruct((rows, half), jnp.uint32),
        grid_spec=grid_spec,
        compiler_params=_params("arbitrary"),
        name="ffn_experts",
    )(eid, valid, hp, wg, wu, wd)


def _dft_kernel(ce_ref, se_ref, co_ref, so_ref, cot_ref, sot_ref, base_ref, *, n_fft):
    rows, cols = ce_ref.shape
    i = pl.program_id(0)
    theta = 2.0 * math.pi / n_fft

    def phases(r, c):
        return r * (2 * c), r * (2 * c + 1), c * (2 * r + 1)

    @pl.when(i == 0)
    def _():
        r = lax.broadcasted_iota(jnp.int32, (rows, cols), 0)
        c = lax.broadcasted_iota(jnp.int32, (rows, cols), 1)
        for f, ph in enumerate(phases(r, c)):
            ang = (ph & (n_fft - 1)).astype(F32) * theta
            base_ref[2 * f] = jnp.cos(ang)
            base_ref[2 * f + 1] = jnp.sin(ang)

    c = lax.broadcasted_iota(jnp.int32, (8, cols), 1)
    r0 = i * rows
    shifts = (r0 * (2 * c), r0 * (2 * c + 1), c * (2 * r0))
    for f, (c_ref, s_ref) in enumerate(((ce_ref, se_ref), (co_ref, so_ref), (cot_ref, sot_ref))):
        ang = (shifts[f] & (n_fft - 1)).astype(F32) * theta
        ca, sa = jnp.cos(ang)[0:1], jnp.sin(ang)[0:1]
        cb, sb = base_ref[2 * f], base_ref[2 * f + 1]
        c_ref[...] = (cb * ca - sb * sa).astype(BF16)
        s_ref[...] = (sb * ca + cb * sa).astype(BF16)


def _dft_tables(length):
    half, rows = length // 2, 128
    shape = jax.ShapeDtypeStruct((half, half), BF16)
    spec = pl.BlockSpec((rows, half), lambda i: (i, 0))
    return pl.pallas_call(
        functools.partial(_dft_kernel, n_fft=2 * length),
        out_shape=(shape,) * 6,
        grid=(half // rows,),
        out_specs=(spec,) * 6,
        scratch_shapes=[pltpu.VMEM((6, rows, half), F32)],
        compiler_params=_params("arbitrary"),
        name="dft_tables",
    )()


LANES = 128


def _split_parity(tmp_ref, x):
    half = x.shape[0] // 2
    for j in range(tmp_ref.shape[0]):
        tmp_ref[j] = x[:, j * LANES:(j + 1) * LANES]
    pick = lambda start: jnp.concatenate(
        [tmp_ref[j, pl.ds(start, half, stride=2), :] for j in range(tmp_ref.shape[0])], axis=1)
    return pick(0), pick(1)


def _merge_parity(tmp_ref, even, odd):
    half = even.shape[0]
    for j in range(tmp_ref.shape[0]):
        tmp_ref[j, pl.ds(0, half, stride=2), :] = even[:, j * LANES:(j + 1) * LANES]
        tmp_ref[j, pl.ds(1, half, stride=2), :] = odd[:, j * LANES:(j + 1) * LANES]
    return jnp.concatenate([tmp_ref[j] for j in range(tmp_ref.shape[0])], axis=1)


def _filter_time_kernel(feats_ref, w1_ref, b1_ref, w2_ref, b2_ref, fr_ref, w3_ref, t_ref, delta_ref,
                        g_ref, hq_ref, tmp_ref):
    hp = lax.Precision.HIGHEST
    length = feats_ref.shape[0]
    half = length // 2
    fr = fr_ref[...]
    h = jnp.sin(fr * (jnp.dot(feats_ref[...], w1_ref[...], preferred_element_type=F32, precision=hp) + b1_ref[...]))
    h = jnp.sin(fr * (jnp.dot(h, w2_ref[...], preferred_element_type=F32, precision=hp) + b2_ref[...]))
    mod = jnp.exp(-t_ref[...] * delta_ref[...]) + MOD_SHIFT
    row = lax.broadcasted_iota(jnp.int32, (length, 1), 0)
    alt = jnp.where(lax.broadcasted_iota(jnp.int32, (half, 1), 0) % 2 == 0, 1.0, -1.0).astype(F32)
    for o in range(2):
        fwd = jnp.dot(h, w3_ref[2 * o], preferred_element_type=F32, precision=hp) * mod
        bwd = jnp.dot(h, w3_ref[2 * o + 1], preferred_element_type=F32, precision=hp) * mod
        bwd = jnp.where(row == 0, 0.0, bwd)
        norm = (jnp.sum(jnp.abs(fwd), axis=0, keepdims=True)
                + jnp.sum(jnp.abs(bwd), axis=0, keepdims=True) + 1e-6)
        fwd = fwd / norm
        bwd = bwd / norm
        for part, g in enumerate((fwd + bwd, fwd - bwd)):
            even, odd = _split_parity(tmp_ref, g)
            g_ref[o, 2 * part] = even.astype(BF16)
            g_ref[o, 2 * part + 1] = odd.astype(BF16)
            mid = jnp.sum((even if part == 0 else odd) * alt, axis=0, keepdims=True) * (1.0 / length)
            hq_ref[o, part] = mid if part == 0 else -mid


def _filter_freq_kernel(ce_ref, se_ref, co_ref, so_ref, g_ref, h_ref):
    half = ce_ref.shape[0]
    row = lax.broadcasted_iota(jnp.int32, (half, 1), 0)
    scale = jnp.where(row == 0, 1.0, 2.0).astype(F32) * (1.0 / (4 * half))
    a = jnp.dot(ce_ref[...], g_ref[0, 0], preferred_element_type=F32)
    b = jnp.dot(co_ref[...], g_ref[0, 1], preferred_element_type=F32)
    c = jnp.dot(se_ref[...], g_ref[0, 2], preferred_element_type=F32)
    d = jnp.dot(so_ref[...], g_ref[0, 3], preferred_element_type=F32)
    h_ref[0, 0] = (a + b) * scale
    h_ref[0, 1] = -(c + d) * scale
    h_ref[0, 2] = (a - b) * scale
    h_ref[0, 3] = (c - d) * scale


def _hyena_filters(length, tables, w1, b1, w2, b2, w3, freq):
    t = jnp.linspace(0.0, 1.0, length, dtype=F32)[:, None]
    f = jnp.linspace(1e-4, FILT_BANDS - 1, FILT_BANDS, dtype=F32)[None]
    ang = (2.0 * math.pi / length) * jnp.arange(length, dtype=F32)[:, None] * f
    feats = jnp.concatenate([t, jnp.cos(ang), -jnp.sin(ang)], axis=-1)
    deltas = jnp.abs(jnp.linspace(math.log(DECAY_TARGET) / SLOW_DECAY_PCT,
                                  math.log(DECAY_TARGET) / FAST_DECAY_PCT, HY_C, dtype=F32))[None]
    hid = w1.shape[1]
    w3r = w3.reshape(hid, 4, HY_C).transpose(1, 0, 2)
    tc = HY_TC
    full = lambda shape: pl.BlockSpec(shape, lambda c: (0,) * len(shape))
    half = length // 2
    g, hq = pl.pallas_call(
        _filter_time_kernel,
        out_shape=(jax.ShapeDtypeStruct((2, 4, half, HY_C), BF16),
                   jax.ShapeDtypeStruct((2, 2, 1, HY_C), F32)),
        grid=(HY_C // tc,),
        in_specs=[full(feats.shape), full(w1.shape), full((1, hid)), full(w2.shape), full((1, hid)),
                  full((1, hid)), pl.BlockSpec((4, hid, tc), lambda c: (0, 0, c)), full((length, 1)),
                  pl.BlockSpec((1, tc), lambda c: (0, c))],
        out_specs=(pl.BlockSpec((2, 4, half, tc), lambda c: (0, 0, 0, c)),
                   pl.BlockSpec((2, 2, 1, tc), lambda c: (0, 0, 0, c))),
        scratch_shapes=[pltpu.VMEM((tc // LANES, length, LANES), F32)],
        compiler_params=_params("parallel"),
        name="filter_time",
    )(feats, w1, b1.reshape(1, hid), w2, b2.reshape(1, hid), freq.reshape(1, hid), w3r, t, deltas)
    ce, se, co, so = tables[:4]
    table_spec = _resident((half, half), lambda o, c: (0, 0))
    spectra = pl.pallas_call(
        _filter_freq_kernel,
        out_shape=jax.ShapeDtypeStruct((2, 4, half, HY_C), F32),
        grid=(2, HY_C // tc),
        in_specs=[table_spec] * 4 + [pl.BlockSpec((1, 4, half, tc), lambda o, c: (o, 0, 0, c))],
        out_specs=pl.BlockSpec((1, 4, half, tc), lambda o, c: (o, 0, 0, c)),
        compiler_params=_params("parallel", "parallel"),
        name="filter_freq",
    )(ce, se, co, so, g)
    return spectra, hq


def _hyena_kernel(z0_ref, z1_ref, z2_ref, cw_ref, cb_ref, ce_ref, se_ref, co_ref, so_ref, cot_ref, sot_ref,
                  h_ref, hq_ref, skip_ref, o_ref, tmp_ref):
    length = z0_ref.shape[1]
    half = length // 2
    row = lax.broadcasted_iota(jnp.int32, (length, 1), 0)
    alt = jnp.where(lax.broadcasted_iota(jnp.int32, (half, 1), 0) % 2 == 0, 1.0, -1.0).astype(F32)

    def short_conv(z_ref, c):
        u = z_ref[0].astype(F32)
        prev = jnp.where(row == 0, 0.0, pltpu.roll(u, 1, axis=0))
        nxt = jnp.where(row == length - 1, 0.0, pltpu.roll(u, length - 1, axis=0))
        return cb_ref[c] + (prev * cw_ref[0, c] + u * cw_ref[1, c] + nxt * cw_ref[2, c])

    def long_conv(u, o):
        ue, uo = _split_parity(tmp_ref, u)
        ue_bf, uo_bf = ue.astype(BF16), uo.astype(BF16)
        a4 = jnp.sum(ue * alt, axis=0, keepdims=True)
        b4 = jnp.sum(uo * alt, axis=0, keepdims=True)
        hr4, hi4 = hq_ref[o, 0], hq_ref[o, 1]
        y_even = alt * (a4 * hr4 + b4 * hi4)
        y_odd = -(alt * (a4 * hi4 - b4 * hr4))
        for kt in range(half // HY_KT):
            ks = slice(kt * HY_KT, (kt + 1) * HY_KT)
            ae = jnp.dot(ce_ref[ks, :], ue_bf, preferred_element_type=F32)
            ao = jnp.dot(co_ref[ks, :], uo_bf, preferred_element_type=F32)
            be = jnp.dot(se_ref[ks, :], ue_bf, preferred_element_type=F32)
            bo = jnp.dot(so_ref[ks, :], uo_bf, preferred_element_type=F32)
            a_f, a_g, b_f, b_g = ae + ao, ae - ao, be + bo, bo - be
            hr_f, hi_f, hr_g, hi_g = (h_ref[o, j, ks, :] for j in range(4))
            re_f, im_f = a_f * hr_f + b_f * hi_f, a_f * hi_f - b_f * hr_f
            re_g, im_g = a_g * hr_g + b_g * hi_g, a_g * hi_g - b_g * hr_g
            y_even += (jnp.dot(ce_ref[:, ks], (re_f + re_g).astype(BF16), preferred_element_type=F32)
                       - jnp.dot(se_ref[:, ks], (im_f - im_g).astype(BF16), preferred_element_type=F32))
            y_odd += (jnp.dot(cot_ref[:, ks], (re_f - re_g).astype(BF16), preferred_element_type=F32)
                      - jnp.dot(sot_ref[:, ks], (im_f + im_g).astype(BF16), preferred_element_type=F32))
        return _merge_parity(tmp_ref, y_even, y_odd) + u * skip_ref[o]

    z = short_conv(z1_ref, 1) * long_conv(short_conv(z0_ref, 0), 0)
    o_ref[0] = (short_conv(z2_ref, 2) * long_conv(z, 1)).astype(o_ref.dtype)


def _hyena(z3, conv_w, conv_b, tables, spectra, hq, skip):
    b, s, _ = z3.shape
    half = s // 2
    tc = HY_TC
    nct = HY_C // tc
    cw = conv_w.reshape(3, 3, 1, HY_C)
    cb = conv_b.reshape(3, 1, HY_C)
    zspec = lambda chunk: pl.BlockSpec((1, s, tc), lambda c, i: (i, 0, chunk * nct + c))
    return pl.pallas_call(
        _hyena_kernel,
        out_shape=jax.ShapeDtypeStruct((b, s, HY_C), BF16),
        grid=(nct, b),
        in_specs=[zspec(0), zspec(1), zspec(2),
                  pl.BlockSpec((3, 3, 1, tc), lambda c, i: (0, 0, 0, c)),
                  pl.BlockSpec((3, 1, tc), lambda c, i: (0, 0, c))]
                 + [_resident((half, half), lambda c, i: (0, 0))] * 6
                 + [_resident((2, 4, half, tc), lambda c, i: (0, 0, 0, c)),
                    pl.BlockSpec((2, 2, 1, tc), lambda c, i: (0, 0, 0, c)),
                    pl.BlockSpec((2, 1, tc), lambda c, i: (0, 0, c))],
        out_specs=pl.BlockSpec((1, s, tc), lambda c, i: (i, 0, c)),
        scratch_shapes=[pltpu.VMEM((tc // LANES, s, LANES), F32)],
        compiler_params=_params("parallel", "parallel"),
        cost_estimate=pl.CostEstimate(flops=2 * 4 * 2 * b * s * half * HY_C, transcendentals=0,
                                      bytes_accessed=2 * 3 * b * s * HY_C + 4 * b * s * HY_C + 12 * half * half),
        name="hyena",
    )(z3, z3, z3, cw, cb, *tables, spectra, hq, skip.reshape(2, 1, HY_C))


def _rel_bucket(rel):
    half = NUM_BUCKETS // 2
    exact = half // 2
    n = np.abs(rel)
    large = exact + (np.log(np.maximum(n, 1) / exact) / np.log(REL_MAX_DIST / exact) * (half - exact)).astype(np.int32)
    large = np.minimum(large, half - 1)
    return (np.where(rel > 0, half, 0) + np.where(n < exact, n, large)).astype(np.int32)


def _att_tiles(length):
    kw = min(ATT_KW, length)
    tiles = []
    for qs in range(0, length, ATT_TQ):
        ks = min(max(qs - (kw - ATT_TQ) // 2, 0), length - kw)
        tiles.append((qs, ks, {0: 0, -64: 1, -128: 2}[ks - qs]))
    return kw, tiles


def _att_bias(rel_bias, group, seq):
    window, dil = AT_GROUPS[group]
    band = window // (2 * dil)
    kw, tiles = _att_tiles(seq // dil)
    offsets = sorted({ks - qs for qs, ks, _ in tiles}, reverse=True)
    table = rel_bias[:, group * AT_HEADS:(group + 1) * AT_HEADS].astype(F32)
    span = kw + ATT_TQ
    k = np.arange(span)
    rel = np.asarray(offsets)[:, None] + np.where(k < kw, k, k - span)[None, :]
    diag = jnp.where((np.abs(rel) <= band)[:, None, :],
                     jnp.swapaxes(table[_rel_bucket(rel * dil)], 1, 2), NEG_INF)
    return diag[:, :, None, :]


def _bias_tile(diag_ref, var, kw):
    rows = jnp.broadcast_to(diag_ref[var, 0], (ATT_TQ, diag_ref.shape[3]))
    return pltpu.roll(rows, 0, 1, stride=1, stride_axis=0)[:, :kw]


def _dilated_kernel(q1, q2, q3, k1, k2, k3, v1, v2, v3, b1, b2, b3, o_ref,
                    qn_ref, kn_ref, vn_ref, og_ref, lg_ref):
    seq = o_ref.shape[1]
    contract_last = (((1,), (1,)), ((), ()))
    for g, (q_ref, k_ref, v_ref, bias_ref) in enumerate(((q1, k1, v1, b1), (q2, k2, v2, b2), (q3, k3, v3, b3))):
        dil = AT_GROUPS[g][1]
        kw, tiles = _att_tiles(seq // dil)
        if dil > 1:
            qn_ref[...] = q_ref[0].astype(F32)
            kn_ref[...] = k_ref[0].astype(F32)
            vn_ref[...] = v_ref[0].astype(F32)
        bias = {var: _bias_tile(bias_ref, var, kw) for var in sorted({var for _, _, var in tiles})}
        for r in range(dil):
            for qs, ks, var in tiles:
                rows = lambda start, size: (pl.ds(r + start * dil, size, stride=dil) if dil > 1
                                            else pl.ds(start, size))
                if dil > 1:
                    qt = qn_ref[rows(qs, ATT_TQ), :].astype(BF16)
                    kt = kn_ref[rows(ks, kw), :].astype(BF16)
                    vt = vn_ref[rows(ks, kw), :].astype(BF16)
                else:
                    qt, kt, vt = q_ref[0, rows(qs, ATT_TQ), :], k_ref[0, rows(ks, kw), :], v_ref[0, rows(ks, kw), :]
                s = lax.dot_general(qt, kt, contract_last, preferred_element_type=F32) + bias[var]
                m = jnp.max(s, axis=-1, keepdims=True)
                p = jnp.exp(s - m)
                z = jnp.sum(p, axis=-1, keepdims=True)
                og_ref[g, rows(qs, ATT_TQ), :] = jnp.dot(p.astype(BF16), vt, preferred_element_type=F32) / z
                lg_ref[g, rows(qs, ATT_TQ), :] = jnp.broadcast_to(m + jnp.log(z), (ATT_TQ, HD))
    l0, l1, l2 = lg_ref[0], lg_ref[1], lg_ref[2]
    mx = jnp.maximum(jnp.maximum(l0, l1), l2)
    e0, e1, e2 = jnp.exp(l0 - mx), jnp.exp(l1 - mx), jnp.exp(l2 - mx)
    o_ref[0] = ((e0 * og_ref[0] + e1 * og_ref[1] + e2 * og_ref[2]) / (e0 + e1 + e2)).astype(o_ref.dtype)


def _dilated_attention(z3, rel_bias):
    b, s, _ = z3.shape
    ng = len(AT_GROUPS)
    col = lambda part, g: pl.BlockSpec((1, s, HD), lambda i, h: (i, 0, part * ng * AT_HEADS + g * AT_HEADS + h))
    biases = [_att_bias(rel_bias, g, s) for g in range(ng)]
    bias_spec = lambda a: pl.BlockSpec((a.shape[0], 1) + a.shape[2:], lambda i, h: (0, h, 0, 0))
    return pl.pallas_call(
        _dilated_kernel,
        out_shape=jax.ShapeDtypeStruct((b, s, AT_HEADS * HD), BF16),
        grid=(b, AT_HEADS),
        in_specs=[col(p, g) for p in range(3) for g in range(ng)] + [bias_spec(a) for a in biases],
        out_specs=pl.BlockSpec((1, s, HD), lambda i, h: (i, 0, h)),
        scratch_shapes=[pltpu.VMEM((s, HD), F32), pltpu.VMEM((s, HD), F32), pltpu.VMEM((s, HD), F32),
                        pltpu.VMEM((ng, s, HD), F32), pltpu.VMEM((ng, s, HD), F32)],
        compiler_params=_params("parallel", "parallel"),
        name="dilated_attention",
    )(*([z3] * 9), *biases)


ROUTER_LANES = 128


def _router_kernel(x_ref, g_ref, w_ref, hp_ref, idx_ref, gate_ref, count_ref, carry_ref):
    tm = x_ref.shape[0]
    half = hp_ref.shape[1]

    @pl.when(pl.program_id(0) == 0)
    def _():
        carry_ref[...] = jnp.zeros_like(carry_ref)

    h = _rms_rows(x_ref[...], g_ref[...])
    hp_ref[...] = _bf16_bits(h[:, :half]) | (_bf16_bits(h[:, half:]) << 16)
    h_hi = h.astype(BF16)
    h_lo = (h - h_hi.astype(F32)).astype(BF16)
    logits = (jnp.dot(h_hi, w_ref[0], preferred_element_type=F32)
              + (jnp.dot(h_hi, w_ref[1], preferred_element_type=F32)
                 + jnp.dot(h_lo, w_ref[0], preferred_element_type=F32)))
    lane =lax.broadcasted_iota(jnp.int32, logits.shape, 1).astype(F32)
    logits = jnp.where(lane < N_EXPERTS, logits, -jnp.inf)
    m1 = jnp.max(logits, axis=-1, keepdims=True)
    i1 = jnp.min(jnp.where(logits == m1, lane, float(ROUTER_LANES)), axis=-1, keepdims=True)
    rest = jnp.where(lane == i1, -jnp.inf, logits)
    m2 = jnp.max(rest, axis=-1, keepdims=True)
    i2 = jnp.min(jnp.where(rest == m2, lane, float(ROUTER_LANES)), axis=-1, keepdims=True)
    e2 = jnp.exp(m2 - m1)
    den = 1.0 + e2
    gate_ref[...] = jnp.where(lane == 0, 1.0 / den, e2 / den)
    chosen = jnp.where((lane == i1) | (lane == i2), 1.0, 0.0)
    earlier = lax.broadcasted_iota(jnp.int32, (tm, tm), 0) > lax.broadcasted_iota(jnp.int32, (tm, tm), 1)
    carry = carry_ref[...]
    before = jnp.dot(jnp.where(earlier, 1.0, 0.0).astype(BF16), chosen.astype(BF16),
                     preferred_element_type=F32) + carry
    r1 = jnp.sum(jnp.where(lane == i1, before, 0.0), axis=-1, keepdims=True)
    r2 = jnp.sum(jnp.where(lane == i2, before, 0.0), axis=-1, keepdims=True)
    idx_ref[...] = jnp.where(lane == 0, i1, jnp.where(lane == 1, i2, jnp.where(lane == 2, r1, r2))).astype(jnp.int32)
    carry = carry + jnp.sum(chosen, axis=0, keepdims=True)
    carry_ref[...] = carry
    count_ref[...] = carry.astype(jnp.int32)


def _router(x2, gain, router):
    rows, d = x2.shape
    tm = MOE_CHUNK
    w = jnp.zeros((d, ROUTER_LANES), F32).at[:, :N_EXPERTS].set(router.astype(F32))
    w_hi = w.astype(BF16)
    w = jnp.stack([w_hi, (w - w_hi.astype(F32)).astype(BF16)])
    row_spec = lambda width: pl.BlockSpec((tm, width), lambda i: (i, 0))
    return pl.pallas_call(
        _router_kernel,
        out_shape=(jax.ShapeDtypeStruct((rows, d // 2), jnp.uint32),
                   jax.ShapeDtypeStruct((rows, ROUTER_LANES), jnp.int32),
                   jax.ShapeDtypeStruct((rows, ROUTER_LANES), F32),
                   jax.ShapeDtypeStruct((1, ROUTER_LANES), jnp.int32)),
        grid=(rows // tm,),
        in_specs=[row_spec(d),
                  pl.BlockSpec((1, d), lambda i: (0, 0)),
                  pl.BlockSpec((2, d, ROUTER_LANES), lambda i: (0, 0, 0))],
        out_specs=(row_spec(d // 2), row_spec(ROUTER_LANES), row_spec(ROUTER_LANES),
                   pl.BlockSpec((1, ROUTER_LANES), lambda i: (0, 0))),
        scratch_shapes=[pltpu.VMEM((1, ROUTER_LANES), F32)],
        compiler_params=_params("arbitrary"),
        name="router",
    )(x2, gain.reshape(1, d), w)


def _sc_workers():
    info = plsc.get_sparse_core_info()
    mesh = plsc.VectorSubcoreMesh(core_axis_name="core", subcore_axis_name="subcore")
    return mesh, info.num_cores, info.num_subcores


def _pack_weight_rows(w, col_tile):
    r, c = w.shape
    mesh, nc, ns = _sc_workers()
    rs, ncol = SC_PACK_ROWS, c // col_tile
    per_w = (r // rs) * ncol // (nc * ns)
    assert per_w * nc * ns * rs * col_tile == r * c and per_w % 2 == 0 and col_tile % SC_LANES == 0
    params = pltpu.CompilerParams()
    if "needs_layout_passes" in pltpu.CompilerParams.__dataclass_fields__:
        params = dataclasses.replace(params, needs_layout_passes=False)

    @functools.partial(
        pl.kernel, mesh=mesh, out_type=jax.ShapeDtypeStruct((r // 2, c), jnp.uint32), compiler_params=params,
        cost_estimate=pl.CostEstimate(flops=r * c, transcendentals=0, bytes_accessed=6 * r * c),
        scratch_types=[pltpu.VMEM((2, rs, col_tile), F32), pltpu.VMEM((2, rs // 2, col_tile), jnp.uint32),
                       pltpu.SemaphoreType.DMA((2,)), pltpu.SemaphoreType.DMA((2,))])
    def pack(w_hbm, o_hbm, in_v, out_v, rsem, wsem):
        wid = lax.axis_index("subcore") * nc + lax.axis_index("core")

        @pl.loop(0, per_w // 2)
        def _(it):
            reads, writes = [], []
            for b in range(2):
                tile = wid * per_w + 2 * it + b
                r0 = pl.multiple_of((tile // ncol) * rs, rs)
                c0 = pl.multiple_of((tile % ncol) * col_tile, col_tile)
                reads.append(pltpu.make_async_copy(w_hbm.at[pl.ds(r0, rs), pl.ds(c0, col_tile)], in_v.at[b],
                                                   rsem.at[b]))
                writes.append(pltpu.make_async_copy(
                    out_v.at[b], o_hbm.at[pl.ds(pl.multiple_of(r0 // 2, rs // 2), rs // 2), pl.ds(c0, col_tile)],
                    wsem.at[b]))
            reads[0].start()
            reads[1].start()
            for b in range(2):
                reads[b].wait()
                for pair in range(rs // 2):
                    @plsc.parallel_loop(0, col_tile, step=SC_LANES, unroll=8)
                    def _(j):
                        packed = plsc.pack(in_v[b, 2 * pair, pl.ds(j, SC_LANES)],
                                           in_v[b, 2 * pair + 1, pl.ds(j, SC_LANES)],
                                           format=plsc.PackFormat.INTERLEAVED)
                        out_v[b, pair, pl.ds(j, SC_LANES)] = plsc.bitcast(packed, jnp.uint32)
                writes[b].start()
            writes[0].wait()
            writes[1].wait()

    return pack(w)


def _sc_token_rows(t, nc, ns):
    per_w = t // (nc * ns)
    assert per_w * nc * ns == t and per_w % (2 * SC_ROWS) == 0
    return per_w, per_w // SC_ROWS


def _scatter_rows(table, dests, p_rows, after=()):
    t, w = table.shape
    nk = len(dests)
    mesh, nc, ns = _sc_workers()
    per_w, nit = _sc_token_rows(t, nc, ns)
    ch = SC_ROWS

    @functools.partial(
        pl.kernel, mesh=mesh, out_type=jax.ShapeDtypeStruct((p_rows, w), table.dtype),
        scratch_types=[pltpu.VMEM((nk, nit, ch), jnp.int32), pltpu.VMEM((2, ch, w), table.dtype),
                       pltpu.SemaphoreType.DMA((2,)), pltpu.SemaphoreType.DMA((2, nk))])
    def scatter(table_hbm, *refs):
        dest_hbm, out_hbm = refs[:nk], refs[nk + len(after)]
        idx_v, rows_v, rsem, wsem = refs[nk + len(after) + 1:]
        wid = lax.axis_index("subcore") * nc + lax.axis_index("core")
        for k in range(nk):
            pltpu.sync_copy(dest_hbm[k].at[pl.ds(wid * nit, nit)], idx_v.at[k])

        @pl.loop(0, nit // 2)
        def _(it):
            reads = [pltpu.make_async_copy(table_hbm.at[pl.ds(wid * per_w + (2 * it + b) * ch, ch)],
                                           rows_v.at[b], rsem.at[b]) for b in range(2)]
            writes = [[pltpu.make_async_copy(rows_v.at[b], out_hbm.at[idx_v.at[k].at[2 * it + b]], wsem.at[b, k])
                       for k in range(nk)] for b in range(2)]
            reads[0].start()
            reads[1].start()
            for b in range(2):
                reads[b].wait()
                for k in range(nk):
                    writes[b][k].start()
            for b in range(2):
                for k in range(nk):
                    writes[b][k].wait()

    return scatter(table, *[d.reshape(t // ch, ch) for d in dests], *after)


def _gather_rows(table, idxs):
    t = idxs[0].shape[0]
    w = table.shape[1]
    nk = len(idxs)
    mesh, nc, ns = _sc_workers()
    per_w, nit = _sc_token_rows(t, nc, ns)
    ch = SC_ROWS

    @functools.partial(
        pl.kernel, mesh=mesh, out_type=[jax.ShapeDtypeStruct((t, w), table.dtype)] * nk,
        scratch_types=[pltpu.VMEM((nk, nit, ch), jnp.int32), pltpu.VMEM((nk, 2, ch, w), table.dtype),
                       pltpu.SemaphoreType.DMA((nk, 2)), pltpu.SemaphoreType.DMA((nk, 2))])
    def gather(table_hbm, *refs):
        idx_hbm, out_hbm = refs[:nk], refs[nk:2 * nk]
        idx_v, rows_v, rsem, wsem = refs[2 * nk:]
        wid = lax.axis_index("subcore") * nc + lax.axis_index("core")
        for k in range(nk):
            pltpu.sync_copy(idx_hbm[k].at[pl.ds(wid * nit, nit)], idx_v.at[k])

        @pl.loop(0, nit // 2)
        def _(it):
            slots = [(k, b) for k in range(nk) for b in range(2)]
            reads = {(k, b): pltpu.make_async_copy(table_hbm.at[idx_v.at[k].at[2 * it + b]], rows_v.at[k, b],
                                                   rsem.at[k, b]) for k, b in slots}
            writes = {(k, b): pltpu.make_async_copy(rows_v.at[k, b],
                                                    out_hbm[k].at[pl.ds(wid * per_w + (2 * it + b) * ch, ch)],
                                                    wsem.at[k, b]) for k, b in slots}
            for s in slots:
                reads[s].start()
            for s in slots:
                reads[s].wait()
                writes[s].start()
            for s in slots:
                writes[s].wait()

    return gather(table, *[i.reshape(t // ch, ch) for i in idxs])


def _combine_kernel(x_ref, y0_ref, y1_ref, gate_ref, *rest):
    o_ref = rest[-1]
    half = x_ref.shape[1] // 2
    for part in range(2):
        cols = slice(part * half, (part + 1) * half)
        acc = x_ref[:, cols]
        for k, y_ref in enumerate((y0_ref, y1_ref)):
            word = y_ref[...]
            bits = (word << 16) if part == 0 else (word & jnp.uint32(0xFFFF0000))
            acc = acc + gate_ref[:, k:k + 1] * lax.bitcast_convert_type(bits, F32)
        o_ref[:, cols] = acc


def _combine(x2, y0, y1, gates, part, earlier=None):
    t, d = x2.shape
    tm = ROW_TILE
    steps = y0.shape[0] // tm
    here = lambda i: (part * steps + i, 0)
    in_specs = [pl.BlockSpec((tm, d), here),
                pl.BlockSpec((tm, d // 2), lambda i: (i, 0)),
                pl.BlockSpec((tm, d // 2), lambda i: (i, 0)),
                pl.BlockSpec((tm, TOP_K), here)]
    operands = [x2, y0, y1, gates]
    if earlier is not None:
        in_specs.append(pl.BlockSpec(memory_space=pl.ANY))
        operands.append(earlier)
    return pl.pallas_call(
        _combine_kernel,
        out_shape=jax.ShapeDtypeStruct((t, d), F32),
        grid=(steps,),
        in_specs=in_specs,
        out_specs=pl.BlockSpec((tm, d), here),
        input_output_aliases={} if earlier is None else {len(operands) - 1: 0},
        compiler_params=_params("parallel"),
        name="moe_combine",
    )(*operands)


def _moe(x2, gain, router, wg, wu, wd):
    t, d = x2.shape
    hp, idx, gate, count = _router(x2, gain, router)
    experts = jnp.arange(N_EXPERTS, dtype=jnp.int32)
    counts = count[0, :N_EXPERTS]
    padded = (counts + MOE_ROWS - 1) // MOE_ROWS * MOE_ROWS
    pend = jnp.cumsum(padded)
    pstart = pend - padded
    expert, rank = idx[:, :TOP_K], idx[:, TOP_K:2 * TOP_K]
    dest = jnp.sum(jnp.where(expert[:, :, None] == experts, pstart, 0), axis=-1) + rank
    dests = [dest[:, k] for k in range(TOP_K)]
    p_rows = t * TOP_K + N_EXPERTS * MOE_ROWS
    blk_row = jnp.arange(p_rows // MOE_ROWS, dtype=jnp.int32) * MOE_ROWS
    blk_expert = jnp.minimum(jnp.sum(pend[None, :] <= blk_row[:, None], axis=1), N_EXPERTS - 1).astype(jnp.int32)
    valid = jnp.clip(counts[blk_expert] - (blk_row - pstart[blk_expert]), 0, MOE_ROWS)
    valid = jnp.where(blk_row < pend[-1], valid, 0).astype(jnp.int32)
    n_exp, _, f = wg.shape
    pack = lambda w, col_tile: _pack_weight_rows(w.reshape(-1, w.shape[2]), col_tile).reshape(n_exp, -1, w.shape[2])
    wg, wu, wd = pack(wg, f // 2), pack(wu, f // 2), pack(wd, d)
    hb = _scatter_rows(hp, dests, p_rows, after=(wg, wu, wd))
    yb = _ffn_experts(hb, wg, wu, wd, blk_expert, valid)
    out = None
    for part in range(MOE_TAIL_PARTS):
        rows = slice(part * t // MOE_TAIL_PARTS, (part + 1) * t // MOE_TAIL_PARTS)
        y0, y1 = _gather_rows(yb, [dk[rows] for dk in dests])
        out = _combine(x2, y0, y1, gate[:, :TOP_K], part, out)
    return out


def kernel(x, mem, rel_bias, norm_mix, norm_mem, norm_ffn, w_mem_kv, xq_norm, xk_norm, w_out, hy_w_in, hy_conv_w, hy_conv_b, hy_filt_w1, hy_filt_b1, hy_filt_w2, hy_filt_b2, hy_filt_w3, hy_sin_freq, hy_skip, at_w_in, at_q_norm, at_k_norm, ffn_w_gate, ffn_w_up, ffn_w_down, moe_router, moe_w_gate, moe_w_up, moe_w_down):
    b, s, d = x.shape
    t = b * s
    m_len = mem.shape[1]
    x2 = x.reshape(t, d)
    mem2 = mem.reshape(b * m_len, d)
    bf = lambda w: w.astype(BF16)
    score_scale = HD ** -0.5
    xa_heads = XA_W // HD
    at_heads = AT_W // HD
    plain = lambda width: [None] * (width // HD)
    kv_gains = lambda i: [xk_norm[i]] * xa_heads + plain(XA_W)

    tables = _dft_tables(s)
    spectra, hq = _hyena_filters(s, tables, hy_filt_w1[0], hy_filt_b1[0], hy_filt_w2[0], hy_filt_b2[0],
                                 hy_filt_w3[0], hy_sin_freq[0])
    z = _norm_matmul(x2, norm_mix[0], bf(hy_w_in[0]), 1024,
                     plain(3 * HY_C) + [xq_norm[0] * score_scale] * xa_heads).reshape(b, s, -1)
    kv = _norm_matmul(mem2, norm_mem[0], bf(w_mem_kv[0]), 1024, kv_gains(0)).reshape(b, m_len, -1)
    self_out = _hyena(z, hy_conv_w[0], hy_conv_b[0], tables, spectra, hq, hy_skip[0])
    x2 = _mix_out(x2.reshape(b, s, d), self_out, z, 3 * HY_C // XA_W, kv, bf(w_out[0])).reshape(t, d)
    x2 = _ffn_dense(x2, norm_ffn[0], bf(ffn_w_gate[0]), bf(ffn_w_up[0]), bf(ffn_w_down[0]))

    at_gains = ([at_q_norm[0] * score_scale] * at_heads + [at_k_norm[0]] * at_heads + plain(AT_W)
                + [xq_norm[1] * score_scale] * xa_heads)
    z = _norm_matmul(x2, norm_mix[1], bf(at_w_in[0]), 1024, at_gains).reshape(b, s, -1)
    kv = _norm_matmul(mem2, norm_mem[1], bf(w_mem_kv[1]), 1024, kv_gains(1)).reshape(b, m_len, -1)
    self_out = _dilated_attention(z, rel_bias)
    x2 = _mix_out(x2.reshape(b, s, d), self_out, z, 3 * AT_W // XA_W, kv, bf(w_out[1])).reshape(t, d)
    x2 = _moe(x2, norm_ffn[1], moe_router[0], moe_w_gate[0], moe_w_up[0], moe_w_down[0])
    return x2.reshape(b, s, d)
```

```python
import dataclasses
import functools
import math

import jax
import jax.numpy as jnp
import numpy as np
from jax import lax
from jax.experimental import pallas as pl
from jax.experimental.pallas import tpu as pltpu
from jax.experimental.pallas import tpu_sc as plsc

F32 = jnp.float32
BF16 = jnp.bfloat16

D_MODEL = 1024
EPS = 1e-6
HY_C = 512
FILT_BANDS = 16
DECAY_TARGET = 1e-2
FAST_DECAY_PCT = 0.3
SLOW_DECAY_PCT = 1.5
MOD_SHIFT = 0.05
AT_GROUPS = ((128, 1), (512, 4), (2048, 16))
AT_HEADS = 4
HD = 128
AT_W = 1536
NUM_BUCKETS = 32
REL_MAX_DIST = 1024
NEG_INF = -1e30
XA_W = 512
D_FF = 2816
N_EXPERTS = 8
TOP_K = 2

VMEM_LIMIT_BYTES = 56 * 1024 * 1024
ROW_TILE = 512
FF_TILE = 1408
MOE_ROWS = 512
MOE_CHUNK = 512
MOE_TAIL_PARTS = 2
SC_ROWS = 32
SC_LANES = 16
SC_PACK_ROWS = 16
ATT_TQ = 128
ATT_KW = 256
ATT_UNROLL = 4
HY_TC = 256
HY_KT = 512


def _params(*sem):
    return pltpu.CompilerParams(dimension_semantics=sem, vmem_limit_bytes=VMEM_LIMIT_BYTES)


def _rms_rows(x, gain):
    return x * lax.rsqrt(jnp.mean(x * x, axis=-1, keepdims=True) + EPS) * gain


def _bf16_bits(v):
    u = lax.bitcast_convert_type(v, jnp.uint32)
    return (u + jnp.uint32(0x7FFF) + ((u >> 16) & jnp.uint32(1))) >> 16


def _resident(shape, index_map):
    return pl.BlockSpec(shape, index_map, pipeline_mode=pl.Buffered(1))


def _norm_matmul_kernel(x_ref, g_ref, w_ref, hg_ref, o_ref, *, tn, head_norm):
    h = _rms_rows(x_ref[...], g_ref[...]).astype(BF16)
    for c in range(o_ref.shape[1] // tn):
        acc = jnp.dot(h, w_ref[:, c * tn:(c + 1) * tn], preferred_element_type=F32)
        for j in range(tn // HD):
            cols = slice(c * tn + j * HD, c * tn + (j + 1) * HD)
            seg = acc[:, j * HD:(j + 1) * HD]
            if head_norm[cols.start // HD]:
                seg = _rms_rows(seg, hg_ref[:, cols])
            o_ref[:, cols] = seg.astype(o_ref.dtype)


def _norm_matmul(x2, gain, w_bf, tn, head_gains):
    rows, d = x2.shape
    n = w_bf.shape[1]
    tm = min(ROW_TILE, rows)
    assert len(head_gains) * HD == n
    hg = jnp.concatenate([jnp.ones((HD,), F32) if g is None else g.astype(F32) for g in head_gains]).reshape(1, n)
    return pl.pallas_call(
        functools.partial(_norm_matmul_kernel, tn=tn, head_norm=tuple(g is not None for g in head_gains)),
        out_shape=jax.ShapeDtypeStruct((rows, n), BF16),
        grid=(rows // tm,),
        in_specs=[pl.BlockSpec((tm, d), lambda i: (i, 0)),
                  pl.BlockSpec((1, d), lambda i: (0, 0)),
                  _resident((d, n), lambda i: (0, 0)),
                  pl.BlockSpec((1, n), lambda i: (0, 0))],
        out_specs=pl.BlockSpec((tm, n), lambda i: (i, 0)),
        compiler_params=_params("parallel"),
        cost_estimate=pl.CostEstimate(flops=2 * rows * d * n, transcendentals=rows,
                                      bytes_accessed=4 * rows * d + 2 * d * n + 2 * rows * n),
        name="norm_matmul",
    )(x2, gain.reshape(1, d), w_bf, hg)


def _mix_out_kernel(x_ref, a_ref, xq_ref, kv_ref, wa_ref, wc_ref, o_ref):
    heads = []
    for h in range(XA_W // HD):
        cols = slice(h * HD, (h + 1) * HD)
        s = lax.dot_general(xq_ref[0, :, cols], kv_ref[0, :, cols], (((1,), (1,)), ((), ())),
                            preferred_element_type=F32)
        p = jnp.exp(s - jnp.max(s, axis=-1, keepdims=True))
        z = jnp.sum(p, axis=-1, keepdims=True)
        pv = jnp.dot(p.astype(BF16), kv_ref[0, :, XA_W + h * HD:XA_W + (h + 1) * HD], preferred_element_type=F32)
        heads.append((pv / z).astype(BF16))
    cross = jnp.concatenate(heads, axis=1)
    o_ref[0] = (x_ref[0]
                + jnp.dot(a_ref[0], wa_ref[...], preferred_element_type=F32)
                + jnp.dot(cross, wc_ref[...], preferred_element_type=F32))


def _mix_out(x3, self_out, z3, xq_block, kv3, w_bf):
    b, s, d = x3.shape
    half = self_out.shape[2]
    m = kv3.shape[1]
    tm = ROW_TILE
    return pl.pallas_call(
        _mix_out_kernel,
        out_shape=jax.ShapeDtypeStruct((b, s, d), F32),
        grid=(b, s // tm),
        in_specs=[pl.BlockSpec((1, tm, d), lambda i, j: (i, j, 0)),
                  pl.BlockSpec((1, tm, half), lambda i, j: (i, j, 0)),
                  pl.BlockSpec((1, tm, XA_W), lambda i, j: (i, j, xq_block)),
                  pl.BlockSpec((1, m, 2 * XA_W), lambda i, j: (i, 0, 0)),
                  pl.BlockSpec((half, d), lambda i, j: (0, 0)),
                  pl.BlockSpec((d - half, d), lambda i, j: (1, 0))],
        out_specs=pl.BlockSpec((1, tm, d), lambda i, j: (i, j, 0)),
        compiler_params=_params("parallel", "parallel"),
        name="mix_out",
    )(x3, self_out, z3, kv3, w_bf, w_bf)


def _weight_rows(ref, start, size):
    if ref.dtype == jnp.uint32:
        return lambda cols: pltpu.bitcast(ref[start // 2:(start + size) // 2, cols], BF16)
    return lambda cols: ref[start:start + size, cols]


def _weight_shape(ref):
    return (ref.shape[0] * (2 if ref.dtype == jnp.uint32 else 1), ref.shape[1])


def _swiglu(h, wg_ref, wu_ref, wd_ref, tf):
    d, f = _weight_shape(wg_ref)
    y = None
    for j in range(f // tf):
        cols = slice(j * tf, (j + 1) * tf)
        gg = jnp.dot(h, _weight_rows(wg_ref, 0, d)(cols), preferred_element_type=F32)
        uu = jnp.dot(h, _weight_rows(wu_ref, 0, d)(cols), preferred_element_type=F32)
        a = ((gg * jax.nn.sigmoid(gg)) * uu).astype(BF16)
        part = jnp.dot(a, _weight_rows(wd_ref, j * tf, tf)(slice(None)), preferred_element_type=F32)
        y = part if y is None else y + part
    return y


def _ffn_dense_kernel(x_ref, g_ref, wg_ref, wu_ref, wd_ref, o_ref, *, tf):
    x = x_ref[...]
    h = _rms_rows(x, g_ref[...]).astype(BF16)
    o_ref[...] = x + _swiglu(h, wg_ref, wu_ref, wd_ref, tf)


def _ffn_dense(x2, gain, wg, wu, wd):
    rows, d = x2.shape
    f = wg.shape[1]
    tm = ROW_TILE
    return pl.pallas_call(
        functools.partial(_ffn_dense_kernel, tf=FF_TILE),
        out_shape=jax.ShapeDtypeStruct((rows, d), F32),
        grid=(rows // tm,),
        in_specs=[pl.BlockSpec((tm, d), lambda i: (i, 0)),
                  pl.BlockSpec((1, d), lambda i: (0, 0)),
                  _resident((d, f), lambda i: (0, 0)),
                  _resident((d, f), lambda i: (0, 0)),
                  _resident((f, d), lambda i: (0, 0))],
        out_specs=pl.BlockSpec((tm, d), lambda i: (i, 0)),
        compiler_params=_params("parallel"),
        cost_estimate=pl.CostEstimate(flops=6 * rows * d * f, transcendentals=rows * f,
                                      bytes_accessed=8 * rows * d + 6 * d * f),
        name="ffn_dense",
    )(x2, gain.reshape(1, d), wg, wu, wd)


def _ffn_expert_kernel(eid_ref, valid_ref, hp_ref, wg_ref, wu_ref, wd_ref, o_ref, h_ref, *, tf):
    half = hp_ref.shape[1]
    valid = valid_ref[pl.program_id(0)]

    @pl.when(valid > 0)
    def _():
        keep = lax.broadcasted_iota(jnp.int32, (hp_ref.shape[0], 1), 0) < valid
        word = hp_ref[...]
        h_ref[:, :half] = jnp.where(keep, lax.bitcast_convert_type(word << 16, F32), 0.0).astype(BF16)
        h_ref[:, half:] = jnp.where(keep, lax.bitcast_convert_type(word & jnp.uint32(0xFFFF0000), F32),
                                    0.0).astype(BF16)
        y = _swiglu(h_ref[...], wg_ref.at[0], wu_ref.at[0], wd_ref.at[0], tf)
        o_ref[...] = _bf16_bits(y[:, :half]) | (_bf16_bits(y[:, half:]) << 16)

    @pl.when(valid <= 0)
    def _():
        o_ref[...] = jnp.zeros_like(o_ref)


def _ffn_experts(hp, wg, wu, wd, eid, valid):
    rows, half = hp.shape
    d = 2 * half
    tm = MOE_ROWS
    expert_spec = lambda w: pl.BlockSpec((1,) + w.shape[1:], lambda i, e, n: (e[i], 0, 0))
    grid_spec = pltpu.PrefetchScalarGridSpec(
        num_scalar_prefetch=2,
        grid=(rows // tm,),
        in_specs=[pl.BlockSpec((tm, half), lambda i, e, n: (i, 0)),
                  expert_spec(wg), expert_spec(wu), expert_spec(wd)],
        out_specs=pl.BlockSpec((tm, half), lambda i, e, n: (i, 0)),
        scratch_shapes=[pltpu.VMEM((tm, d), BF16)],
    )
    return pl.pallas_call(
        functools.partial(_ffn_expert_kernel, tf=FF_TILE),
        out_shape=jax.ShapeDtypeStruct((rows, half), jnp.uint32),
        grid_spec=grid_spec,
        compiler_params=_params("arbitrary"),
        name="ffn_experts",
    )(eid, valid, hp, wg, wu, wd)


def _dft_kernel(ce_ref, se_ref, co_ref, so_ref, cot_ref, sot_ref, base_ref, *, n_fft):
    rows, cols = ce_ref.shape
    i = pl.program_id(0)
    theta = 2.0 * math.pi / n_fft

    def phases(r, c):
        return r * (2 * c), r * (2 * c + 1), c * (2 * r + 1)

    @pl.when(i == 0)
    def _():
        r = lax.broadcasted_iota(jnp.int32, (rows, cols), 0)
        c = lax.broadcasted_iota(jnp.int32, (rows, cols), 1)
        for f, ph in enumerate(phases(r, c)):
            ang = (ph & (n_fft - 1)).astype(F32) * theta
            base_ref[2 * f] = jnp.cos(ang)
            base_ref[2 * f + 1] = jnp.sin(ang)

    c = lax.broadcasted_iota(jnp.int32, (8, cols), 1)
    r0 = i * rows
    shifts = (r0 * (2 * c), r0 * (2 * c + 1), c * (2 * r0))
    for f, (c_ref, s_ref) in enumerate(((ce_ref, se_ref), (co_ref, so_ref), (cot_ref, sot_ref))):
        ang = (shifts[f] & (n_fft - 1)).astype(F32) * theta
        ca, sa = jnp.cos(ang)[0:1], jnp.sin(ang)[0:1]
        cb, sb = base_ref[2 * f], base_ref[2 * f + 1]
        c_ref[...] = (cb * ca - sb * sa).astype(BF16)
        s_ref[...] = (sb * ca + cb * sa).astype(BF16)


def _dft_tables(length):
    half, rows = length // 2, 128
    shape = jax.ShapeDtypeStruct((half, half), BF16)
    spec = pl.BlockSpec((rows, half), lambda i: (i, 0))
    return pl.pallas_call(
        functools.partial(_dft_kernel, n_fft=2 * length),
        out_shape=(shape,) * 6,
        grid=(half // rows,),
        out_specs=(spec,) * 6,
        scratch_shapes=[pltpu.VMEM((6, rows, half), F32)],
        compiler_params=_params("arbitrary"),
        name="dft_tables",
    )()


LANES = 128


def _split_parity(tmp_ref, x):
    half = x.shape[0] // 2
    for j in range(tmp_ref.shape[0]):
        tmp_ref[j] = x[:, j * LANES:(j + 1) * LANES]
    pick = lambda start: jnp.concatenate(
        [tmp_ref[j, pl.ds(start, half, stride=2), :] for j in range(tmp_ref.shape[0])], axis=1)
    return pick(0), pick(1)


def _merge_parity(tmp_ref, even, odd):
    half = even.shape[0]
    for j in range(tmp_ref.shape[0]):
        tmp_ref[j, pl.ds(0, half, stride=2), :] = even[:, j * LANES:(j + 1) * LANES]
        tmp_ref[j, pl.ds(1, half, stride=2), :] = odd[:, j * LANES:(j + 1) * LANES]
    return jnp.concatenate([tmp_ref[j] for j in range(tmp_ref.shape[0])], axis=1)


def _filter_time_kernel(feats_ref, w1_ref, b1_ref, w2_ref, b2_ref, fr_ref, w3_ref, t_ref, delta_ref,
                        g_ref, hq_ref, tmp_ref):
    hp = lax.Precision.HIGHEST
    length = feats_ref.shape[0]
    half = length // 2
    fr = fr_ref[...]
    h = jnp.sin(fr * (jnp.dot(feats_ref[...], w1_ref[...], preferred_element_type=F32, precision=hp) + b1_ref[...]))
    h = jnp.sin(fr * (jnp.dot(h, w2_ref[...], preferred_element_type=F32, precision=hp) + b2_ref[...]))
    mod = jnp.exp(-t_ref[...] * delta_ref[...]) + MOD_SHIFT
    row = lax.broadcasted_iota(jnp.int32, (length, 1), 0)
    alt = jnp.where(lax.broadcasted_iota(jnp.int32, (half, 1), 0) % 2 == 0, 1.0, -1.0).astype(F32)
    for o in range(2):
        fwd = jnp.dot(h, w3_ref[2 * o], preferred_element_type=F32, precision=hp) * mod
        bwd = jnp.dot(h, w3_ref[2 * o + 1], preferred_element_type=F32, precision=hp) * mod
        bwd = jnp.where(row == 0, 0.0, bwd)
        norm = (jnp.sum(jnp.abs(fwd), axis=0, keepdims=True)
                + jnp.sum(jnp.abs(bwd), axis=0, keepdims=True) + 1e-6)
        fwd = fwd / norm
        bwd = bwd / norm
        for part, g in enumerate((fwd + bwd, fwd - bwd)):
            even, odd = _split_parity(tmp_ref, g)
            g_ref[o, 2 * part] = even.astype(BF16)
            g_ref[o, 2 * part + 1] = odd.astype(BF16)
            mid = jnp.sum((even if part == 0 else odd) * alt, axis=0, keepdims=True) * (1.0 / length)
            hq_ref[o, part] = mid if part == 0 else -mid


def _filter_freq_kernel(ce_ref, se_ref, co_ref, so_ref, g_ref, h_ref):
    half = ce_ref.shape[0]
    row = lax.broadcasted_iota(jnp.int32, (half, 1), 0)
    scale = jnp.where(row == 0, 1.0, 2.0).astype(F32) * (1.0 / (4 * half))
    a = jnp.dot(ce_ref[...], g_ref[0, 0], preferred_element_type=F32)
    b = jnp.dot(co_ref[...], g_ref[0, 1], preferred_element_type=F32)
    c = jnp.dot(se_ref[...], g_ref[0, 2], preferred_element_type=F32)
    d = jnp.dot(so_ref[...], g_ref[0, 3], preferred_element_type=F32)
    h_ref[0, 0] = (a + b) * scale
    h_ref[0, 1] = -(c + d) * scale
    h_ref[0, 2] = (a - b) * scale
    h_ref[0, 3] = (c - d) * scale


def _hyena_filters(length, tables, w1, b1, w2, b2, w3, freq):
    t = jnp.linspace(0.0, 1.0, length, dtype=F32)[:, None]
    f = jnp.linspace(1e-4, FILT_BANDS - 1, FILT_BANDS, dtype=F32)[None]
    ang = (2.0 * math.pi / length) * jnp.arange(length, dtype=F32)[:, None] * f
    feats = jnp.concatenate([t, jnp.cos(ang), -jnp.sin(ang)], axis=-1)
    deltas = jnp.abs(jnp.linspace(math.log(DECAY_TARGET) / SLOW_DECAY_PCT,
                                  math.log(DECAY_TARGET) / FAST_DECAY_PCT, HY_C, dtype=F32))[None]
    hid = w1.shape[1]
    w3r = w3.reshape(hid, 4, HY_C).transpose(1, 0, 2)
    tc = HY_TC
    full = lambda shape: pl.BlockSpec(shape, lambda c: (0,) * len(shape))
    half = length // 2
    g, hq = pl.pallas_call(
        _filter_time_kernel,
        out_shape=(jax.ShapeDtypeStruct((2, 4, half, HY_C), BF16),
                   jax.ShapeDtypeStruct((2, 2, 1, HY_C), F32)),
        grid=(HY_C // tc,),
        in_specs=[full(feats.shape), full(w1.shape), full((1, hid)), full(w2.shape), full((1, hid)),
                  full((1, hid)), pl.BlockSpec((4, hid, tc), lambda c: (0, 0, c)), full((length, 1)),
                  pl.BlockSpec((1, tc), lambda c: (0, c))],
        out_specs=(pl.BlockSpec((2, 4, half, tc), lambda c: (0, 0, 0, c)),
                   pl.BlockSpec((2, 2, 1, tc), lambda c: (0, 0, 0, c))),
        scratch_shapes=[pltpu.VMEM((tc // LANES, length, LANES), F32)],
        compiler_params=_params("parallel"),
        name="filter_time",
    )(feats, w1, b1.reshape(1, hid), w2, b2.reshape(1, hid), freq.reshape(1, hid), w3r, t, deltas)
    ce, se, co, so = tables[:4]
    table_spec = _resident((half, half), lambda o, c: (0, 0))
    spectra = pl.pallas_call(
        _filter_freq_kernel,
        out_shape=jax.ShapeDtypeStruct((2, 4, half, HY_C), F32),
        grid=(2, HY_C // tc),
        in_specs=[table_spec] * 4 + [pl.BlockSpec((1, 4, half, tc), lambda o, c: (o, 0, 0, c))],
        out_specs=pl.BlockSpec((1, 4, half, tc), lambda o, c: (o, 0, 0, c)),
        compiler_params=_params("parallel", "parallel"),
        name="filter_freq",
    )(ce, se, co, so, g)
    return spectra, hq


def _hyena_kernel(z0_ref, z1_ref, z2_ref, cw_ref, cb_ref, ce_ref, se_ref, co_ref, so_ref, cot_ref, sot_ref,
                  h_ref, hq_ref, skip_ref, o_ref, tmp_ref):
    length = z0_ref.shape[1]
    half = length // 2
    row = lax.broadcasted_iota(jnp.int32, (length, 1), 0)
    alt = jnp.where(lax.broadcasted_iota(jnp.int32, (half, 1), 0) % 2 == 0, 1.0, -1.0).astype(F32)

    def short_conv(z_ref, c):
        u = z_ref[0].astype(F32)
        prev = jnp.where(row == 0, 0.0, pltpu.roll(u, 1, axis=0))
        nxt = jnp.where(row == length - 1, 0.0, pltpu.roll(u, length - 1, axis=0))
        return cb_ref[c] + (prev * cw_ref[0, c] + u * cw_ref[1, c] + nxt * cw_ref[2, c])

    def long_conv(u, o):
        ue, uo = _split_parity(tmp_ref, u)
        ue_bf, uo_bf = ue.astype(BF16), uo.astype(BF16)
        a4 = jnp.sum(ue * alt, axis=0, keepdims=True)
        b4 = jnp.sum(uo * alt, axis=0, keepdims=True)
        hr4, hi4 = hq_ref[o, 0], hq_ref[o, 1]
        y_even = alt * (a4 * hr4 + b4 * hi4)
        y_odd = -(alt * (a4 * hi4 - b4 * hr4))
        for kt in range(half // HY_KT):
            ks = slice(kt * HY_KT, (kt + 1) * HY_KT)
            ae = jnp.dot(ce_ref[ks, :], ue_bf, preferred_element_type=F32)
            ao = jnp.dot(co_ref[ks, :], uo_bf, preferred_element_type=F32)
            be = jnp.dot(se_ref[ks, :], ue_bf, preferred_element_type=F32)
            bo = jnp.dot(so_ref[ks, :], uo_bf, preferred_element_type=F32)
            a_f, a_g, b_f, b_g = ae + ao, ae - ao, be + bo, bo - be
            hr_f, hi_f, hr_g, hi_g = (h_ref[o, j, ks, :] for j in range(4))
            re_f, im_f = a_f * hr_f + b_f * hi_f, a_f * hi_f - b_f * hr_f
            re_g, im_g = a_g * hr_g + b_g * hi_g, a_g * hi_g - b_g * hr_g
            y_even += (jnp.dot(ce_ref[:, ks], (re_f + re_g).astype(BF16), preferred_element_type=F32)
                       - jnp.dot(se_ref[:, ks], (im_f - im_g).astype(BF16), preferred_element_type=F32))
            y_odd += (jnp.dot(cot_ref[:, ks], (re_f - re_g).astype(BF16), preferred_element_type=F32)
                      - jnp.dot(sot_ref[:, ks], (im_f + im_g).astype(BF16), preferred_element_type=F32))
        return _merge_parity(tmp_ref, y_even, y_odd) + u * skip_ref[o]

    z = short_conv(z1_ref, 1) * long_conv(short_conv(z0_ref, 0), 0)
    o_ref[0] = (short_conv(z2_ref, 2) * long_conv(z, 1)).astype(o_ref.dtype)


def _hyena(z3, conv_w, conv_b, tables, spectra, hq, skip):
    b, s, _ = z3.shape
    half = s // 2
    tc = HY_TC
    nct = HY_C // tc
    cw = conv_w.reshape(3, 3, 1, HY_C)
    cb = conv_b.reshape(3, 1, HY_C)
    zspec = lambda chunk: pl.BlockSpec((1, s, tc), lambda c, i: (i, 0, chunk * nct + c))
    return pl.pallas_call(
        _hyena_kernel,
        out_shape=jax.ShapeDtypeStruct((b, s, HY_C), BF16),
        grid=(nct, b),
        in_specs=[zspec(0), zspec(1), zspec(2),
                  pl.BlockSpec((3, 3, 1, tc), lambda c, i: (0, 0, 0, c)),
                  pl.BlockSpec((3, 1, tc), lambda c, i: (0, 0, c))]
                 + [_resident((half, half), lambda c, i: (0, 0))] * 6
                 + [_resident((2, 4, half, tc), lambda c, i: (0, 0, 0, c)),
                    pl.BlockSpec((2, 2, 1, tc), lambda c, i: (0, 0, 0, c)),
                    pl.BlockSpec((2, 1, tc), lambda c, i: (0, 0, c))],
        out_specs=pl.BlockSpec((1, s, tc), lambda c, i: (i, 0, c)),
        scratch_shapes=[pltpu.VMEM((tc // LANES, s, LANES), F32)],
        compiler_params=_params("parallel", "parallel"),
        cost_estimate=pl.CostEstimate(flops=2 * 4 * 2 * b * s * half * HY_C, transcendentals=0,
                                      bytes_accessed=2 * 3 * b * s * HY_C + 4 * b * s * HY_C + 12 * half * half),
        name="hyena",
    )(z3, z3, z3, cw, cb, *tables, spectra, hq, skip.reshape(2, 1, HY_C))


def _rel_bucket(rel):
    half = NUM_BUCKETS // 2
    exact = half // 2
    n = np.abs(rel)
    large = exact + (np.log(np.maximum(n, 1) / exact) / np.log(REL_MAX_DIST / exact) * (half - exact)).astype(np.int32)
    large = np.minimum(large, half - 1)
    return (np.where(rel > 0, half, 0) + np.where(n < exact, n, large)).astype(np.int32)


def _att_tiles(length):
    kw = min(ATT_KW, length)
    tiles = []
    for qs in range(0, length, ATT_TQ):
        ks = min(max(qs - (kw - ATT_TQ) // 2, 0), length - kw)
        tiles.append((qs, ks, {0: 0, -64: 1, -128: 2}[ks - qs]))
    return kw, tiles


def _att_bias(rel_bias, group, seq):
    window, dil = AT_GROUPS[group]
    band = window // (2 * dil)
    kw, tiles = _att_tiles(seq // dil)
    offsets = sorted({ks - qs for qs, ks, _ in tiles}, reverse=True)
    table = rel_bias[:, group * AT_HEADS:(group + 1) * AT_HEADS].astype(F32)
    span = kw + ATT_TQ
    k = np.arange(span)
    rel = np.asarray(offsets)[:, None] + np.where(k < kw, k, k - span)[None, :]
    diag = jnp.where((np.abs(rel) <= band)[:, None, :],
                     jnp.swapaxes(table[_rel_bucket(rel * dil)], 1, 2), NEG_INF)
    return diag[:, :, None, :]


def _bias_tile(diag_ref, var, kw):
    rows = jnp.broadcast_to(diag_ref[var, 0], (ATT_TQ, diag_ref.shape[3]))
    return pltpu.roll(rows, 0, 1, stride=1, stride_axis=0)[:, :kw]


def _dilated_kernel(q1, q2, q3, k1, k2, k3, v1, v2, v3, b1, b2, b3, o_ref,
                    qn_ref, kn_ref, vn_ref, og_ref, lg_ref, bias_ref):
    seq = o_ref.shape[1]
    contract_last = (((1,), (1,)), ((), ()))
    for g, (q_ref, k_ref, v_ref, diag_ref) in enumerate(((q1, k1, v1, b1), (q2, k2, v2, b2), (q3, k3, v3, b3))):
        dil = AT_GROUPS[g][1]
        length = seq // dil
        kw, tiles = _att_tiles(length)
        per_residue = len(tiles)
        if dil > 1:
            qn_ref[...] = q_ref[0].astype(F32)
            kn_ref[...] = k_ref[0].astype(F32)
            vn_ref[...] = v_ref[0].astype(F32)
        for var in sorted({var for _, _, var in tiles}):
            bias_ref[var, :, :kw] = _bias_tile(diag_ref, var, kw)

        def tile(index, g=g, dil=dil, length=length, kw=kw, per_residue=per_residue,
                 q_ref=q_ref, k_ref=k_ref, v_ref=v_ref):
            r = index // per_residue
            qs = (index % per_residue) * ATT_TQ
            ks = jnp.clip(qs - (kw - ATT_TQ) // 2, 0, length - kw)
            var = (qs - ks) // (ATT_TQ // 2)
            if dil > 1:
                rows = lambda start, size: pl.ds(r + start * dil, size, stride=dil)
                qt = qn_ref[rows(qs, ATT_TQ), :].astype(BF16)
                kt = kn_ref[rows(ks, kw), :].astype(BF16)
                vt = vn_ref[rows(ks, kw), :].astype(BF16)
            else:
                rows = lambda start, size: pl.ds(pl.multiple_of(start, ATT_TQ // 2), size)
                qt, kt, vt = q_ref[0, rows(qs, ATT_TQ), :], k_ref[0, rows(ks, kw), :], v_ref[0, rows(ks, kw), :]
            s = lax.dot_general(qt, kt, contract_last, preferred_element_type=F32) + bias_ref[var, :, :kw]
            m = jnp.max(s, axis=-1, keepdims=True)
            p = jnp.exp(s - m)
            z = jnp.sum(p, axis=-1, keepdims=True)
            og_ref[g, rows(qs, ATT_TQ), :] = jnp.dot(p.astype(BF16), vt, preferred_element_type=F32) / z
            lg_ref[g, rows(qs, ATT_TQ), :] = jnp.broadcast_to(m + jnp.log(z), (ATT_TQ, HD))

        def several(step, carry, tile=tile):
            for u in range(ATT_UNROLL):
                tile(step * ATT_UNROLL + u)
            return carry

        lax.fori_loop(0, dil * per_residue // ATT_UNROLL, several, 0)
    l0, l1, l2 = lg_ref[0], lg_ref[1], lg_ref[2]
    mx = jnp.maximum(jnp.maximum(l0, l1), l2)
    e0, e1, e2 = jnp.exp(l0 - mx), jnp.exp(l1 - mx), jnp.exp(l2 - mx)
    o_ref[0] = ((e0 * og_ref[0] + e1 * og_ref[1] + e2 * og_ref[2]) / (e0 + e1 + e2)).astype(o_ref.dtype)


def _dilated_attention(z3, rel_bias):
    b, s, _ = z3.shape
    ng = len(AT_GROUPS)
    col = lambda part, g: pl.BlockSpec((1, s, HD), lambda i, h: (i, 0, part * ng * AT_HEADS + g * AT_HEADS + h))
    biases = [_att_bias(rel_bias, g, s) for g in range(ng)]
    bias_spec = lambda a: pl.BlockSpec((a.shape[0], 1) + a.shape[2:], lambda i, h: (0, h, 0, 0))
    return pl.pallas_call(
        _dilated_kernel,
        out_shape=jax.ShapeDtypeStruct((b, s, AT_HEADS * HD), BF16),
        grid=(b, AT_HEADS),
        in_specs=[col(p, g) for p in range(3) for g in range(ng)] + [bias_spec(a) for a in biases],
        out_specs=pl.BlockSpec((1, s, HD), lambda i, h: (i, 0, h)),
        scratch_shapes=[pltpu.VMEM((s, HD), F32), pltpu.VMEM((s, HD), F32), pltpu.VMEM((s, HD), F32),
                        pltpu.VMEM((ng, s, HD), F32), pltpu.VMEM((ng, s, HD), F32),
                        pltpu.VMEM((max(a.shape[0] for a in biases), ATT_TQ, ATT_KW), F32)],
        compiler_params=_params("parallel", "parallel"),
        name="dilated_attention",
    )(*([z3] * 9), *biases)


ROUTER_LANES = 128


def _router_kernel(x_ref, g_ref, w_ref, hp_ref, idx_ref, gate_ref, count_ref, carry_ref):
    tm = x_ref.shape[0]
    half = hp_ref.shape[1]

    @pl.when(pl.program_id(0) == 0)
    def _():
        carry_ref[...] = jnp.zeros_like(carry_ref)

    h = _rms_rows(x_ref[...], g_ref[...])
    hp_ref[...] = _bf16_bits(h[:, :half]) | (_bf16_bits(h[:, half:]) << 16)
    h_hi = h.astype(BF16)
    h_lo = (h - h_hi.astype(F32)).astype(BF16)
    logits = (jnp.dot(h_hi, w_ref[0], preferred_element_type=F32)
              + (jnp.dot(h_hi, w_ref[1], preferred_element_type=F32)
                 + jnp.dot(h_lo, w_ref[0], preferred_element_type=F32)))
    lane =lax.broadcasted_iota(jnp.int32, logits.shape, 1).astype(F32)
    logits = jnp.where(lane < N_EXPERTS, logits, -jnp.inf)
    m1 = jnp.max(logits, axis=-1, keepdims=True)
    i1 = jnp.min(jnp.where(logits == m1, lane, float(ROUTER_LANES)), axis=-1, keepdims=True)
    rest = jnp.where(lane == i1, -jnp.inf, logits)
    m2 = jnp.max(rest, axis=-1, keepdims=True)
    i2 = jnp.min(jnp.where(rest == m2, lane, float(ROUTER_LANES)), axis=-1, keepdims=True)
    e2 = jnp.exp(m2 - m1)
    den = 1.0 + e2
    gate_ref[...] = jnp.where(lane == 0, 1.0 / den, e2 / den)
    chosen = jnp.where((lane == i1) | (lane == i2), 1.0, 0.0)
    earlier = lax.broadcasted_iota(jnp.int32, (tm, tm), 0) > lax.broadcasted_iota(jnp.int32, (tm, tm), 1)
    carry = carry_ref[...]
    before = jnp.dot(jnp.where(earlier, 1.0, 0.0).astype(BF16), chosen.astype(BF16),
                     preferred_element_type=F32) + carry
    r1 = jnp.sum(jnp.where(lane == i1, before, 0.0), axis=-1, keepdims=True)
    r2 = jnp.sum(jnp.where(lane == i2, before, 0.0), axis=-1, keepdims=True)
    idx_ref[...] = jnp.where(lane == 0, i1, jnp.where(lane == 1, i2, jnp.where(lane == 2, r1, r2))).astype(jnp.int32)
    carry = carry + jnp.sum(chosen, axis=0, keepdims=True)
    carry_ref[...] = carry
    count_ref[...] = carry.astype(jnp.int32)


def _router(x2, gain, router):
    rows, d = x2.shape
    tm = MOE_CHUNK
    w = jnp.zeros((d, ROUTER_LANES), F32).at[:, :N_EXPERTS].set(router.astype(F32))
    w_hi = w.astype(BF16)
    w = jnp.stack([w_hi, (w - w_hi.astype(F32)).astype(BF16)])
    row_spec = lambda width: pl.BlockSpec((tm, width), lambda i: (i, 0))
    return pl.pallas_call(
        _router_kernel,
        out_shape=(jax.ShapeDtypeStruct((rows, d // 2), jnp.uint32),
                   jax.ShapeDtypeStruct((rows, ROUTER_LANES), jnp.int32),
                   jax.ShapeDtypeStruct((rows, ROUTER_LANES), F32),
                   jax.ShapeDtypeStruct((1, ROUTER_LANES), jnp.int32)),
        grid=(rows // tm,),
        in_specs=[row_spec(d),
                  pl.BlockSpec((1, d), lambda i: (0, 0)),
                  pl.BlockSpec((2, d, ROUTER_LANES), lambda i: (0, 0, 0))],
        out_specs=(row_spec(d // 2), row_spec(ROUTER_LANES), row_spec(ROUTER_LANES),
                   pl.BlockSpec((1, ROUTER_LANES), lambda i: (0, 0))),
        scratch_shapes=[pltpu.VMEM((1, ROUTER_LANES), F32)],
        compiler_params=_params("arbitrary"),
        name="router",
    )(x2, gain.reshape(1, d), w)


def _sc_workers():
    info = plsc.get_sparse_core_info()
    mesh = plsc.VectorSubcoreMesh(core_axis_name="core", subcore_axis_name="subcore")
    return mesh, info.num_cores, info.num_subcores


def _pack_weight_rows(w, col_tile):
    r, c = w.shape
    mesh, nc, ns = _sc_workers()
    rs, ncol = SC_PACK_ROWS, c // col_tile
    per_w = (r // rs) * ncol // (nc * ns)
    assert per_w * nc * ns * rs * col_tile == r * c and per_w % 2 == 0 and col_tile % SC_LANES == 0
    params = pltpu.CompilerParams()
    if "needs_layout_passes" in pltpu.CompilerParams.__dataclass_fields__:
        params = dataclasses.replace(params, needs_layout_passes=False)

    @functools.partial(
        pl.kernel, mesh=mesh, out_type=jax.ShapeDtypeStruct((r // 2, c), jnp.uint32), compiler_params=params,
        cost_estimate=pl.CostEstimate(flops=r * c, transcendentals=0, bytes_accessed=6 * r * c),
        scratch_types=[pltpu.VMEM((2, rs, col_tile), F32), pltpu.VMEM((2, rs // 2, col_tile), jnp.uint32),
                       pltpu.SemaphoreType.DMA((2,)), pltpu.SemaphoreType.DMA((2,))])
    def pack(w_hbm, o_hbm, in_v, out_v, rsem, wsem):
        wid = lax.axis_index("subcore") * nc + lax.axis_index("core")

        @pl.loop(0, per_w // 2)
        def _(it):
            reads, writes = [], []
            for b in range(2):
                tile = wid * per_w + 2 * it + b
                r0 = pl.multiple_of((tile // ncol) * rs, rs)
                c0 = pl.multiple_of((tile % ncol) * col_tile, col_tile)
                reads.append(pltpu.make_async_copy(w_hbm.at[pl.ds(r0, rs), pl.ds(c0, col_tile)], in_v.at[b],
                                                   rsem.at[b]))
                writes.append(pltpu.make_async_copy(
                    out_v.at[b], o_hbm.at[pl.ds(pl.multiple_of(r0 // 2, rs // 2), rs // 2), pl.ds(c0, col_tile)],
                    wsem.at[b]))
            reads[0].start()
            reads[1].start()
            for b in range(2):
                reads[b].wait()
                for pair in range(rs // 2):
                    @plsc.parallel_loop(0, col_tile, step=SC_LANES, unroll=8)
                    def _(j):
                        packed = plsc.pack(in_v[b, 2 * pair, pl.ds(j, SC_LANES)],
                                           in_v[b, 2 * pair + 1, pl.ds(j, SC_LANES)],
                                           format=plsc.PackFormat.INTERLEAVED)
                        out_v[b, pair, pl.ds(j, SC_LANES)] = plsc.bitcast(packed, jnp.uint32)
                writes[b].start()
            writes[0].wait()
            writes[1].wait()

    return pack(w)


def _sc_token_rows(t, nc, ns):
    per_w = t // (nc * ns)
    assert per_w * nc * ns == t and per_w % (2 * SC_ROWS) == 0
    return per_w, per_w // SC_ROWS


def _scatter_rows(table, dests, p_rows, after=()):
    t, w = table.shape
    nk = len(dests)
    mesh, nc, ns = _sc_workers()
    per_w, nit = _sc_token_rows(t, nc, ns)
    ch = SC_ROWS

    @functools.partial(
        pl.kernel, mesh=mesh, out_type=jax.ShapeDtypeStruct((p_rows, w), table.dtype),
        scratch_types=[pltpu.VMEM((nk, nit, ch), jnp.int32), pltpu.VMEM((2, ch, w), table.dtype),
                       pltpu.SemaphoreType.DMA((2,)), pltpu.SemaphoreType.DMA((2, nk))])
    def scatter(table_hbm, *refs):
        dest_hbm, out_hbm = refs[:nk], refs[nk + len(after)]
        idx_v, rows_v, rsem, wsem = refs[nk + len(after) + 1:]
        wid = lax.axis_index("subcore") * nc + lax.axis_index("core")
        for k in range(nk):
            pltpu.sync_copy(dest_hbm[k].at[pl.ds(wid * nit, nit)], idx_v.at[k])

        @pl.loop(0, nit // 2)
        def _(it):
            reads = [pltpu.make_async_copy(table_hbm.at[pl.ds(wid * per_w + (2 * it + b) * ch, ch)],
                                           rows_v.at[b], rsem.at[b]) for b in range(2)]
            writes = [[pltpu.make_async_copy(rows_v.at[b], out_hbm.at[idx_v.at[k].at[2 * it + b]], wsem.at[b, k])
                       for k in range(nk)] for b in range(2)]
            reads[0].start()
            reads[1].start()
            for b in range(2):
                reads[b].wait()
                for k in range(nk):
                    writes[b][k].start()
            for b in range(2):
                for k in range(nk):
                    writes[b][k].wait()

    return scatter(table, *[d.reshape(t // ch, ch) for d in dests], *after)


def _gather_rows(table, idxs):
    t = idxs[0].shape[0]
    w = table.shape[1]
    nk = len(idxs)
    mesh, nc, ns = _sc_workers()
    per_w, nit = _sc_token_rows(t, nc, ns)
    ch = SC_ROWS

    @functools.partial(
        pl.kernel, mesh=mesh, out_type=[jax.ShapeDtypeStruct((t, w), table.dtype)] * nk,
        scratch_types=[pltpu.VMEM((nk, nit, ch), jnp.int32), pltpu.VMEM((nk, 2, ch, w), table.dtype),
                       pltpu.SemaphoreType.DMA((nk, 2)), pltpu.SemaphoreType.DMA((nk, 2))])
    def gather(table_hbm, *refs):
        idx_hbm, out_hbm = refs[:nk], refs[nk:2 * nk]
        idx_v, rows_v, rsem, wsem = refs[2 * nk:]
        wid = lax.axis_index("subcore") * nc + lax.axis_index("core")
        for k in range(nk):
            pltpu.sync_copy(idx_hbm[k].at[pl.ds(wid * nit, nit)], idx_v.at[k])

        @pl.loop(0, nit // 2)
        def _(it):
            slots = [(k, b) for k in range(nk) for b in range(2)]
            reads = {(k, b): pltpu.make_async_copy(table_hbm.at[idx_v.at[k].at[2 * it + b]], rows_v.at[k, b],
                                                   rsem.at[k, b]) for k, b in slots}
            writes = {(k, b): pltpu.make_async_copy(rows_v.at[k, b],
                                                    out_hbm[k].at[pl.ds(wid * per_w + (2 * it + b) * ch, ch)],
                                                    wsem.at[k, b]) for k, b in slots}
            for s in slots:
                reads[s].start()
            for s in slots:
                reads[s].wait()
                writes[s].start()
            for s in slots:
                writes[s].wait()

    return gather(table, *[i.reshape(t // ch, ch) for i in idxs])


def _combine_kernel(x_ref, y0_ref, y1_ref, gate_ref, *rest):
    o_ref = rest[-1]
    half = x_ref.shape[1] // 2
    for part in range(2):
        cols = slice(part * half, (part + 1) * half)
        acc = x_ref[:, cols]
        for k, y_ref in enumerate((y0_ref, y1_ref)):
            word = y_ref[...]
            bits = (word << 16) if part == 0 else (word & jnp.uint32(0xFFFF0000))
            acc = acc + gate_ref[:, k:k + 1] * lax.bitcast_convert_type(bits, F32)
        o_ref[:, cols] = acc


def _combine(x2, y0, y1, gates, part, earlier=None):
    t, d = x2.shape
    tm = ROW_TILE
    steps = y0.shape[0] // tm
    here = lambda i: (part * steps + i, 0)
    in_specs = [pl.BlockSpec((tm, d), here),
                pl.BlockSpec((tm, d // 2), lambda i: (i, 0)),
                pl.BlockSpec((tm, d // 2), lambda i: (i, 0)),
                pl.BlockSpec((tm, TOP_K), here)]
    operands = [x2, y0, y1, gates]
    if earlier is not None:
        in_specs.append(pl.BlockSpec(memory_space=pl.ANY))
        operands.append(earlier)
    return pl.pallas_call(
        _combine_kernel,
        out_shape=jax.ShapeDtypeStruct((t, d), F32),
        grid=(steps,),
        in_specs=in_specs,
        out_specs=pl.BlockSpec((tm, d), here),
        input_output_aliases={} if earlier is None else {len(operands) - 1: 0},
        compiler_params=_params("parallel"),
        name="moe_combine",
    )(*operands)


def _moe(x2, gain, router, wg, wu, wd):
    t, d = x2.shape
    hp, idx, gate, count = _router(x2, gain, router)
    experts = jnp.arange(N_EXPERTS, dtype=jnp.int32)
    counts = count[0, :N_EXPERTS]
    padded = (counts + MOE_ROWS - 1) // MOE_ROWS * MOE_ROWS
    pend = jnp.cumsum(padded)
    pstart = pend - padded
    expert, rank = idx[:, :TOP_K], idx[:, TOP_K:2 * TOP_K]
    dest = jnp.sum(jnp.where(expert[:, :, None] == experts, pstart, 0), axis=-1) + rank
    dests = [dest[:, k] for k in range(TOP_K)]
    p_rows = t * TOP_K + N_EXPERTS * MOE_ROWS
    blk_row = jnp.arange(p_rows // MOE_ROWS, dtype=jnp.int32) * MOE_ROWS
    blk_expert = jnp.minimum(jnp.sum(pend[None, :] <= blk_row[:, None], axis=1), N_EXPERTS - 1).astype(jnp.int32)
    valid = jnp.clip(counts[blk_expert] - (blk_row - pstart[blk_expert]), 0, MOE_ROWS)
    valid = jnp.where(blk_row < pend[-1], valid, 0).astype(jnp.int32)
    n_exp, _, f = wg.shape
    pack = lambda w, col_tile: _pack_weight_rows(w.reshape(-1, w.shape[2]), col_tile).reshape(n_exp, -1, w.shape[2])
    wg, wu, wd = pack(wg, f // 2), pack(wu, f // 2), pack(wd, d)
    hb = _scatter_rows(hp, dests, p_rows, after=(wg, wu, wd))
    yb = _ffn_experts(hb, wg, wu, wd, blk_expert, valid)
    out = None
    for part in range(MOE_TAIL_PARTS):
        rows = slice(part * t // MOE_TAIL_PARTS, (part + 1) * t // MOE_TAIL_PARTS)
        y0, y1 = _gather_rows(yb, [dk[rows] for dk in dests])
        out = _combine(x2, y0, y1, gate[:, :TOP_K], part, out)
    return out


def kernel(x, mem, rel_bias, norm_mix, norm_mem, norm_ffn, w_mem_kv, xq_norm, xk_norm, w_out, hy_w_in, hy_conv_w, hy_conv_b, hy_filt_w1, hy_filt_b1, hy_filt_w2, hy_filt_b2, hy_filt_w3, hy_sin_freq, hy_skip, at_w_in, at_q_norm, at_k_norm, ffn_w_gate, ffn_w_up, ffn_w_down, moe_router, moe_w_gate, moe_w_up, moe_w_down):
    b, s, d = x.shape
    t = b * s
    m_len = mem.shape[1]
    x2 = x.reshape(t, d)
    mem2 = mem.reshape(b * m_len, d)
    bf = lambda w: w.astype(BF16)
    score_scale = HD ** -0.5
    xa_heads = XA_W // HD
    at_heads = AT_W // HD
    plain = lambda width: [None] * (width // HD)
    kv_gains = lambda i: [xk_norm[i]] * xa_heads + plain(XA_W)

    tables = _dft_tables(s)
    spectra, hq = _hyena_filters(s, tables, hy_filt_w1[0], hy_filt_b1[0], hy_filt_w2[0], hy_filt_b2[0],
                                 hy_filt_w3[0], hy_sin_freq[0])
    z = _norm_matmul(x2, norm_mix[0], bf(hy_w_in[0]), 1024,
                     plain(3 * HY_C) + [xq_norm[0] * score_scale] * xa_heads).reshape(b, s, -1)
    kv = _norm_matmul(mem2, norm_mem[0], bf(w_mem_kv[0]), 1024, kv_gains(0)).reshape(b, m_len, -1)
    self_out = _hyena(z, hy_conv_w[0], hy_conv_b[0], tables, spectra, hq, hy_skip[0])
    x2 = _mix_out(x2.reshape(b, s, d), self_out, z, 3 * HY_C // XA_W, kv, bf(w_out[0])).reshape(t, d)
    x2 = _ffn_dense(x2, norm_ffn[0], bf(ffn_w_gate[0]), bf(ffn_w_up[0]), bf(ffn_w_down[0]))

    at_gains = ([at_q_norm[0] * score_scale] * at_heads + [at_k_norm[0]] * at_heads + plain(AT_W)
                + [xq_norm[1] * score_scale] * xa_heads)
    z = _norm_matmul(x2, norm_mix[1], bf(at_w_in[0]), 1024, at_gains).reshape(b, s, -1)
    kv = _norm_matmul(mem2, norm_mem[1], bf(w_mem_kv[1]), 1024, kv_gains(1)).reshape(b, m_len, -1)
    self_out = _dilated_attention(z, rel_bias)
    x2 = _mix_out(x2.reshape(b, s, d), self_out, z, 3 * AT_W // XA_W, kv, bf(w_out[1])).reshape(t, d)
    x2 = _moe(x2, norm_ffn[1], moe_router[0], moe_w_gate[0], moe_w_up[0], moe_w_down[0])
    return x2.reshape(b, s, d)
```

```python
import dataclasses
import functools
import math

import jax
import jax.numpy as jnp
import numpy as np
from jax import lax
from jax.experimental import pallas as pl
from jax.experimental.pallas import tpu as pltpu
from jax.experimental.pallas import tpu_sc as plsc

F32 = jnp.float32
BF16 = jnp.bfloat16

D_MODEL = 1024
EPS = 1e-6
HY_C = 512
FILT_BANDS = 16
DECAY_TARGET = 1e-2
FAST_DECAY_PCT = 0.3
SLOW_DECAY_PCT = 1.5
MOD_SHIFT = 0.05
AT_GROUPS = ((128, 1), (512, 4), (2048, 16))
AT_HEADS = 4
HD = 128
AT_W = 1536
NUM_BUCKETS = 32
REL_MAX_DIST = 1024
NEG_INF = -1e30
XA_W = 512
D_FF = 2816
N_EXPERTS = 8
TOP_K = 2

VMEM_LIMIT_BYTES = 56 * 1024 * 1024
ROW_TILE = 512
FF_TILE = 256
MOE_ROWS = 512
MOE_CHUNK = 512
MOE_TAIL_PARTS = 2
SC_ROWS = 32
SC_LANES = 16
SC_PACK_ROWS = 16
ATT_TQ = 128
ATT_KW = 256
HY_TC = 256
HY_KT = 512


def _params(*sem):
    return pltpu.CompilerParams(dimension_semantics=sem, vmem_limit_bytes=VMEM_LIMIT_BYTES)


def _rms_rows(x, gain):
    return x * lax.rsqrt(jnp.mean(x * x, axis=-1, keepdims=True) + EPS) * gain


def _bf16_bits(v):
    u = lax.bitcast_convert_type(v, jnp.uint32)
    return (u + jnp.uint32(0x7FFF) + ((u >> 16) & jnp.uint32(1))) >> 16


def _resident(shape, index_map):
    return pl.BlockSpec(shape, index_map, pipeline_mode=pl.Buffered(1))


def _norm_matmul_kernel(x_ref, g_ref, w_ref, hg_ref, o_ref, *, tn, head_norm):
    h = _rms_rows(x_ref[...], g_ref[...]).astype(BF16)
    for c in range(o_ref.shape[1] // tn):
        acc = jnp.dot(h, w_ref[:, c * tn:(c + 1) * tn], preferred_element_type=F32)
        for j in range(tn // HD):
            cols = slice(c * tn + j * HD, c * tn + (j + 1) * HD)
            seg = acc[:, j * HD:(j + 1) * HD]
            if head_norm[cols.start // HD]:
                seg = _rms_rows(seg, hg_ref[:, cols])
            o_ref[:, cols] = seg.astype(o_ref.dtype)


def _norm_matmul(x2, gain, w_bf, tn, head_gains):
    rows, d = x2.shape
    n = w_bf.shape[1]
    tm = min(ROW_TILE, rows)
    assert len(head_gains) * HD == n
    hg = jnp.concatenate([jnp.ones((HD,), F32) if g is None else g.astype(F32) for g in head_gains]).reshape(1, n)
    return pl.pallas_call(
        functools.partial(_norm_matmul_kernel, tn=tn, head_norm=tuple(g is not None for g in head_gains)),
        out_shape=jax.ShapeDtypeStruct((rows, n), BF16),
        grid=(rows // tm,),
        in_specs=[pl.BlockSpec((tm, d), lambda i: (i, 0)),
                  pl.BlockSpec((1, d), lambda i: (0, 0)),
                  _resident((d, n), lambda i: (0, 0)),
                  pl.BlockSpec((1, n), lambda i: (0, 0))],
        out_specs=pl.BlockSpec((tm, n), lambda i: (i, 0)),
        compiler_params=_params("parallel"),
        cost_estimate=pl.CostEstimate(flops=2 * rows * d * n, transcendentals=rows,
                                      bytes_accessed=4 * rows * d + 2 * d * n + 2 * rows * n),
        name="norm_matmul",
    )(x2, gain.reshape(1, d), w_bf, hg)


def _mix_out_kernel(x_ref, a_ref, xq_ref, kv_ref, wa_ref, wc_ref, o_ref):
    heads = []
    for h in range(XA_W // HD):
        cols = slice(h * HD, (h + 1) * HD)
        s = lax.dot_general(xq_ref[0, :, cols], kv_ref[0, :, cols], (((1,), (1,)), ((), ())),
                            preferred_element_type=F32)
        p = jnp.exp(s - jnp.max(s, axis=-1, keepdims=True))
        z = jnp.sum(p, axis=-1, keepdims=True)
        pv = jnp.dot(p.astype(BF16), kv_ref[0, :, XA_W + h * HD:XA_W + (h + 1) * HD], preferred_element_type=F32)
        heads.append((pv / z).astype(BF16))
    cross = jnp.concatenate(heads, axis=1)
    o_ref[0] = (x_ref[0]
                + jnp.dot(a_ref[0], wa_ref[...], preferred_element_type=F32)
                + jnp.dot(cross, wc_ref[...], preferred_element_type=F32))


def _mix_out(x3, self_out, z3, xq_block, kv3, w_bf):
    b, s, d = x3.shape
    half = self_out.shape[2]
    m = kv3.shape[1]
    tm = ROW_TILE
    return pl.pallas_call(
        _mix_out_kernel,
        out_shape=jax.ShapeDtypeStruct((b, s, d), F32),
        grid=(b, s // tm),
        in_specs=[pl.BlockSpec((1, tm, d), lambda i, j: (i, j, 0)),
                  pl.BlockSpec((1, tm, half), lambda i, j: (i, j, 0)),
                  pl.BlockSpec((1, tm, XA_W), lambda i, j: (i, j, xq_block)),
                  pl.BlockSpec((1, m, 2 * XA_W), lambda i, j: (i, 0, 0)),
                  pl.BlockSpec((half, d), lambda i, j: (0, 0)),
                  pl.BlockSpec((d - half, d), lambda i, j: (1, 0))],
        out_specs=pl.BlockSpec((1, tm, d), lambda i, j: (i, j, 0)),
        compiler_params=_params("parallel", "parallel"),
        name="mix_out",
    )(x3, self_out, z3, kv3, w_bf, w_bf)


def _weight_rows(ref, start, size):
    if ref.dtype == jnp.uint32:
        return lambda cols: pltpu.bitcast(ref[start // 2:(start + size) // 2, cols], BF16)
    return lambda cols: ref[start:start + size, cols]


def _weight_shape(ref):
    return (ref.shape[0] * (2 if ref.dtype == jnp.uint32 else 1), ref.shape[1])


def _swiglu(h, wg_ref, wu_ref, wd_ref, tf):
    d, f = _weight_shape(wg_ref)
    y = None
    for j in range(f // tf):
        cols = slice(j * tf, (j + 1) * tf)
        gg = jnp.dot(h, _weight_rows(wg_ref, 0, d)(cols), preferred_element_type=F32)
        uu = jnp.dot(h, _weight_rows(wu_ref, 0, d)(cols), preferred_element_type=F32)
        a = ((gg * jax.nn.sigmoid(gg)) * uu).astype(BF16)
        part = jnp.dot(a, _weight_rows(wd_ref, j * tf, tf)(slice(None)), preferred_element_type=F32)
        y = part if y is None else y + part
    return y


def _ffn_dense_kernel(x_ref, g_ref, wg_ref, wu_ref, wd_ref, o_ref, *, tf):
    x = x_ref[...]
    h = _rms_rows(x, g_ref[...]).astype(BF16)
    o_ref[...] = x + _swiglu(h, wg_ref, wu_ref, wd_ref, tf)


def _ffn_dense(x2, gain, wg, wu, wd):
    rows, d = x2.shape
    f = wg.shape[1]
    tm = ROW_TILE
    return pl.pallas_call(
        functools.partial(_ffn_dense_kernel, tf=FF_TILE),
        out_shape=jax.ShapeDtypeStruct((rows, d), F32),
        grid=(rows // tm,),
        in_specs=[pl.BlockSpec((tm, d), lambda i: (i, 0)),
                  pl.BlockSpec((1, d), lambda i: (0, 0)),
                  _resident((d, f), lambda i: (0, 0)),
                  _resident((d, f), lambda i: (0, 0)),
                  _resident((f, d), lambda i: (0, 0))],
        out_specs=pl.BlockSpec((tm, d), lambda i: (i, 0)),
        compiler_params=_params("parallel"),
        cost_estimate=pl.CostEstimate(flops=6 * rows * d * f, transcendentals=rows * f,
                                      bytes_accessed=8 * rows * d + 6 * d * f),
        name="ffn_dense",
    )(x2, gain.reshape(1, d), wg, wu, wd)


def _ffn_expert_kernel(eid_ref, valid_ref, hp_ref, wg_ref, wu_ref, wd_ref, o_ref, h_ref, *, tf):
    half = hp_ref.shape[1]
    valid = valid_ref[pl.program_id(0)]

    @pl.when(valid > 0)
    def _():
        keep = lax.broadcasted_iota(jnp.int32, (hp_ref.shape[0], 1), 0) < valid
        word = hp_ref[...]
        h_ref[:, :half] = jnp.where(keep, lax.bitcast_convert_type(word << 16, F32), 0.0).astype(BF16)
        h_ref[:, half:] = jnp.where(keep, lax.bitcast_convert_type(word & jnp.uint32(0xFFFF0000), F32),
                                    0.0).astype(BF16)
        y = _swiglu(h_ref[...], wg_ref.at[0], wu_ref.at[0], wd_ref.at[0], tf)
        o_ref[...] = _bf16_bits(y[:, :half]) | (_bf16_bits(y[:, half:]) << 16)

    @pl.when(valid <= 0)
    def _():
        o_ref[...] = jnp.zeros_like(o_ref)


def _ffn_experts(hp, wg, wu, wd, eid, valid):
    rows, half = hp.shape
    d = 2 * half
    tm = MOE_ROWS
    expert_spec = lambda w: pl.BlockSpec((1,) + w.shape[1:], lambda i, e, n: (e[i], 0, 0))
    grid_spec = pltpu.PrefetchScalarGridSpec(
        num_scalar_prefetch=2,
        grid=(rows // tm,),
        in_specs=[pl.BlockSpec((tm, half), lambda i, e, n: (i, 0)),
                  expert_spec(wg), expert_spec(wu), expert_spec(wd)],
        out_specs=pl.BlockSpec((tm, half), lambda i, e, n: (i, 0)),
        scratch_shapes=[pltpu.VMEM((tm, d), BF16)],
    )
    return pl.pallas_call(
        functools.partial(_ffn_expert_kernel, tf=FF_TILE),
        out_shape=jax.ShapeDtypeStruct((rows, half), jnp.uint32),
        grid_spec=grid_spec,
        compiler_params=_params("arbitrary"),
        name="ffn_experts",
    )(eid, valid, hp, wg, wu, wd)


def _dft_kernel(ce_ref, se_ref, co_ref, so_ref, cot_ref, sot_ref, base_ref, *, n_fft):
    rows, cols = ce_ref.shape
    i = pl.program_id(0)
    theta = 2.0 * math.pi / n_fft

    def phases(r, c):
        return r * (2 * c), r * (2 * c + 1), c * (2 * r + 1)

    @pl.when(i == 0)
    def _():
        r = lax.broadcasted_iota(jnp.int32, (rows, cols), 0)
        c = lax.broadcasted_iota(jnp.int32, (rows, cols), 1)
        for f, ph in enumerate(phases(r, c)):
            ang = (ph & (n_fft - 1)).astype(F32) * theta
            base_ref[2 * f] = jnp.cos(ang)
            base_ref[2 * f + 1] = jnp.sin(ang)

    c = lax.broadcasted_iota(jnp.int32, (8, cols), 1)
    r0 = i * rows
    shifts = (r0 * (2 * c), r0 * (2 * c + 1), c * (2 * r0))
    for f, (c_ref, s_ref) in enumerate(((ce_ref, se_ref), (co_ref, so_ref), (cot_ref, sot_ref))):
        ang = (shifts[f] & (n_fft - 1)).astype(F32) * theta
        ca, sa = jnp.cos(ang)[0:1], jnp.sin(ang)[0:1]
        cb, sb = base_ref[2 * f], base_ref[2 * f + 1]
        c_ref[...] = (cb * ca - sb * sa).astype(BF16)
        s_ref[...] = (sb * ca + cb * sa).astype(BF16)


def _dft_tables(length):
    half, rows = length // 2, 128
    shape = jax.ShapeDtypeStruct((half, half), BF16)
    spec = pl.BlockSpec((rows, half), lambda i: (i, 0))
    return pl.pallas_call(
        functools.partial(_dft_kernel, n_fft=2 * length),
        out_shape=(shape,) * 6,
        grid=(half // rows,),
        out_specs=(spec,) * 6,
        scratch_shapes=[pltpu.VMEM((6, rows, half), F32)],
        compiler_params=_params("arbitrary"),
        name="dft_tables",
    )()


LANES = 128


def _split_parity(tmp_ref, x):
    half = x.shape[0] // 2
    for j in range(tmp_ref.shape[0]):
        tmp_ref[j] = x[:, j * LANES:(j + 1) * LANES]
    pick = lambda start: jnp.concatenate(
        [tmp_ref[j, pl.ds(start, half, stride=2), :] for j in range(tmp_ref.shape[0])], axis=1)
    return pick(0), pick(1)


def _merge_parity(tmp_ref, even, odd):
    half = even.shape[0]
    for j in range(tmp_ref.shape[0]):
        tmp_ref[j, pl.ds(0, half, stride=2), :] = even[:, j * LANES:(j + 1) * LANES]
        tmp_ref[j, pl.ds(1, half, stride=2), :] = odd[:, j * LANES:(j + 1) * LANES]
    return jnp.concatenate([tmp_ref[j] for j in range(tmp_ref.shape[0])], axis=1)


def _filter_time_kernel(feats_ref, w1_ref, b1_ref, w2_ref, b2_ref, fr_ref, w3_ref, t_ref, delta_ref,
                        g_ref, hq_ref, tmp_ref):
    hp = lax.Precision.HIGHEST
    length = feats_ref.shape[0]
    half = length // 2
    fr = fr_ref[...]
    h = jnp.sin(fr * (jnp.dot(feats_ref[...], w1_ref[...], preferred_element_type=F32, precision=hp) + b1_ref[...]))
    h = jnp.sin(fr * (jnp.dot(h, w2_ref[...], preferred_element_type=F32, precision=hp) + b2_ref[...]))
    mod = jnp.exp(-t_ref[...] * delta_ref[...]) + MOD_SHIFT
    row = lax.broadcasted_iota(jnp.int32, (length, 1), 0)
    alt = jnp.where(lax.broadcasted_iota(jnp.int32, (half, 1), 0) % 2 == 0, 1.0, -1.0).astype(F32)
    for o in range(2):
        fwd = jnp.dot(h, w3_ref[2 * o], preferred_element_type=F32, precision=hp) * mod
        bwd = jnp.dot(h, w3_ref[2 * o + 1], preferred_element_type=F32, precision=hp) * mod
        bwd = jnp.where(row == 0, 0.0, bwd)
        norm = (jnp.sum(jnp.abs(fwd), axis=0, keepdims=True)
                + jnp.sum(jnp.abs(bwd), axis=0, keepdims=True) + 1e-6)
        fwd = fwd / norm
        bwd = bwd / norm
        for part, g in enumerate((fwd + bwd, fwd - bwd)):
            even, odd = _split_parity(tmp_ref, g)
            g_ref[o, 2 * part] = even.astype(BF16)
            g_ref[o, 2 * part + 1] = odd.astype(BF16)
            mid = jnp.sum((even if part == 0 else odd) * alt, axis=0, keepdims=True) * (1.0 / length)
            hq_ref[o, part] = mid if part == 0 else -mid


def _filter_freq_kernel(ce_ref, se_ref, co_ref, so_ref, g_ref, h_ref):
    half = ce_ref.shape[0]
    row = lax.broadcasted_iota(jnp.int32, (half, 1), 0)
    scale = jnp.where(row == 0, 1.0, 2.0).astype(F32) * (1.0 / (4 * half))
    a = jnp.dot(ce_ref[...], g_ref[0, 0], preferred_element_type=F32)
    b = jnp.dot(co_ref[...], g_ref[0, 1], preferred_element_type=F32)
    c = jnp.dot(se_ref[...], g_ref[0, 2], preferred_element_type=F32)
    d = jnp.dot(so_ref[...], g_ref[0, 3], preferred_element_type=F32)
    h_ref[0, 0] = (a + b) * scale
    h_ref[0, 1] = -(c + d) * scale
    h_ref[0, 2] = (a - b) * scale
    h_ref[0, 3] = (c - d) * scale


def _hyena_filters(length, tables, w1, b1, w2, b2, w3, freq):
    t = jnp.linspace(0.0, 1.0, length, dtype=F32)[:, None]
    f = jnp.linspace(1e-4, FILT_BANDS - 1, FILT_BANDS, dtype=F32)[None]
    ang = (2.0 * math.pi / length) * jnp.arange(length, dtype=F32)[:, None] * f
    feats = jnp.concatenate([t, jnp.cos(ang), -jnp.sin(ang)], axis=-1)
    deltas = jnp.abs(jnp.linspace(math.log(DECAY_TARGET) / SLOW_DECAY_PCT,
                                  math.log(DECAY_TARGET) / FAST_DECAY_PCT, HY_C, dtype=F32))[None]
    hid = w1.shape[1]
    w3r = w3.reshape(hid, 4, HY_C).transpose(1, 0, 2)
    tc = HY_TC
    full = lambda shape: pl.BlockSpec(shape, lambda c: (0,) * len(shape))
    half = length // 2
    g, hq = pl.pallas_call(
        _filter_time_kernel,
        out_shape=(jax.ShapeDtypeStruct((2, 4, half, HY_C), BF16),
                   jax.ShapeDtypeStruct((2, 2, 1, HY_C), F32)),
        grid=(HY_C // tc,),
        in_specs=[full(feats.shape), full(w1.shape), full((1, hid)), full(w2.shape), full((1, hid)),
                  full((1, hid)), pl.BlockSpec((4, hid, tc), lambda c: (0, 0, c)), full((length, 1)),
                  pl.BlockSpec((1, tc), lambda c: (0, c))],
        out_specs=(pl.BlockSpec((2, 4, half, tc), lambda c: (0, 0, 0, c)),
                   pl.BlockSpec((2, 2, 1, tc), lambda c: (0, 0, 0, c))),
        scratch_shapes=[pltpu.VMEM((tc // LANES, length, LANES), F32)],
        compiler_params=_params("parallel"),
        name="filter_time",
    )(feats, w1, b1.reshape(1, hid), w2, b2.reshape(1, hid), freq.reshape(1, hid), w3r, t, deltas)
    ce, se, co, so = tables[:4]
    table_spec = _resident((half, half), lambda o, c: (0, 0))
    spectra = pl.pallas_call(
        _filter_freq_kernel,
        out_shape=jax.ShapeDtypeStruct((2, 4, half, HY_C), F32),
        grid=(2, HY_C // tc),
        in_specs=[table_spec] * 4 + [pl.BlockSpec((1, 4, half, tc), lambda o, c: (o, 0, 0, c))],
        out_specs=pl.BlockSpec((1, 4, half, tc), lambda o, c: (o, 0, 0, c)),
        compiler_params=_params("parallel", "parallel"),
        name="filter_freq",
    )(ce, se, co, so, g)
    return spectra, hq


def _hyena_kernel(z0_ref, z1_ref, z2_ref, cw_ref, cb_ref, ce_ref, se_ref, co_ref, so_ref, cot_ref, sot_ref,
                  h_ref, hq_ref, skip_ref, o_ref, tmp_ref):
    length = z0_ref.shape[1]
    half = length // 2
    row = lax.broadcasted_iota(jnp.int32, (length, 1), 0)
    alt = jnp.where(lax.broadcasted_iota(jnp.int32, (half, 1), 0) % 2 == 0, 1.0, -1.0).astype(F32)

    def short_conv(z_ref, c):
        u = z_ref[0].astype(F32)
        prev = jnp.where(row == 0, 0.0, pltpu.roll(u, 1, axis=0))
        nxt = jnp.where(row == length - 1, 0.0, pltpu.roll(u, length - 1, axis=0))
        return cb_ref[c] + (prev * cw_ref[0, c] + u * cw_ref[1, c] + nxt * cw_ref[2, c])

    def long_conv(u, o):
        ue, uo = _split_parity(tmp_ref, u)
        ue_bf, uo_bf = ue.astype(BF16), uo.astype(BF16)
        a4 = jnp.sum(ue * alt, axis=0, keepdims=True)
        b4 = jnp.sum(uo * alt, axis=0, keepdims=True)
        hr4, hi4 = hq_ref[o, 0], hq_ref[o, 1]
        y_even = alt * (a4 * hr4 + b4 * hi4)
        y_odd = -(alt * (a4 * hi4 - b4 * hr4))
        for kt in range(half // HY_KT):
            ks = slice(kt * HY_KT, (kt + 1) * HY_KT)
            ae = jnp.dot(ce_ref[ks, :], ue_bf, preferred_element_type=F32)
            ao = jnp.dot(co_ref[ks, :], uo_bf, preferred_element_type=F32)
            be = jnp.dot(se_ref[ks, :], ue_bf, preferred_element_type=F32)
            bo = jnp.dot(so_ref[ks, :], uo_bf, preferred_element_type=F32)
            a_f, a_g, b_f, b_g = ae + ao, ae - ao, be + bo, bo - be
            hr_f, hi_f, hr_g, hi_g = (h_ref[o, j, ks, :] for j in range(4))
            re_f, im_f = a_f * hr_f + b_f * hi_f, a_f * hi_f - b_f * hr_f
            re_g, im_g = a_g * hr_g + b_g * hi_g, a_g * hi_g - b_g * hr_g
            y_even += (jnp.dot(ce_ref[:, ks], (re_f + re_g).astype(BF16), preferred_element_type=F32)
                       - jnp.dot(se_ref[:, ks], (im_f - im_g).astype(BF16), preferred_element_type=F32))
            y_odd += (jnp.dot(cot_ref[:, ks], (re_f - re_g).astype(BF16), preferred_element_type=F32)
                      - jnp.dot(sot_ref[:, ks], (im_f + im_g).astype(BF16), preferred_element_type=F32))
        return _merge_parity(tmp_ref, y_even, y_odd) + u * skip_ref[o]

    z = short_conv(z1_ref, 1) * long_conv(short_conv(z0_ref, 0), 0)
    o_ref[0] = (short_conv(z2_ref, 2) * long_conv(z, 1)).astype(o_ref.dtype)


def _hyena(z3, conv_w, conv_b, tables, spectra, hq, skip):
    b, s, _ = z3.shape
    half = s // 2
    tc = HY_TC
    nct = HY_C // tc
    cw = conv_w.reshape(3, 3, 1, HY_C)
    cb = conv_b.reshape(3, 1, HY_C)
    zspec = lambda chunk: pl.BlockSpec((1, s, tc), lambda c, i: (i, 0, chunk * nct + c))
    return pl.pallas_call(
        _hyena_kernel,
        out_shape=jax.ShapeDtypeStruct((b, s, HY_C), BF16),
        grid=(nct, b),
        in_specs=[zspec(0), zspec(1), zspec(2),
                  pl.BlockSpec((3, 3, 1, tc), lambda c, i: (0, 0, 0, c)),
                  pl.BlockSpec((3, 1, tc), lambda c, i: (0, 0, c))]
                 + [_resident((half, half), lambda c, i: (0, 0))] * 6
                 + [_resident((2, 4, half, tc), lambda c, i: (0, 0, 0, c)),
                    pl.BlockSpec((2, 2, 1, tc), lambda c, i: (0, 0, 0, c)),
                    pl.BlockSpec((2, 1, tc), lambda c, i: (0, 0, c))],
        out_specs=pl.BlockSpec((1, s, tc), lambda c, i: (i, 0, c)),
        scratch_shapes=[pltpu.VMEM((tc // LANES, s, LANES), F32)],
        compiler_params=_params("parallel", "parallel"),
        cost_estimate=pl.CostEstimate(flops=2 * 4 * 2 * b * s * half * HY_C, transcendentals=0,
                                      bytes_accessed=2 * 3 * b * s * HY_C + 4 * b * s * HY_C + 12 * half * half),
        name="hyena",
    )(z3, z3, z3, cw, cb, *tables, spectra, hq, skip.reshape(2, 1, HY_C))


def _rel_bucket(rel):
    half = NUM_BUCKETS // 2
    exact = half // 2
    n = np.abs(rel)
    large = exact + (np.log(np.maximum(n, 1) / exact) / np.log(REL_MAX_DIST / exact) * (half - exact)).astype(np.int32)
    large = np.minimum(large, half - 1)
    return (np.where(rel > 0, half, 0) + np.where(n < exact, n, large)).astype(np.int32)


def _att_tiles(length):
    kw = min(ATT_KW, length)
    tiles = []
    for qs in range(0, length, ATT_TQ):
        ks = min(max(qs - (kw - ATT_TQ) // 2, 0), length - kw)
        tiles.append((qs, ks, {0: 0, -64: 1, -128: 2}[ks - qs]))
    return kw, tiles


def _att_bias(rel_bias, group, seq):
    window, dil = AT_GROUPS[group]
    band = window // (2 * dil)
    kw, tiles = _att_tiles(seq // dil)
    offsets = sorted({ks - qs for qs, ks, _ in tiles}, reverse=True)
    table = rel_bias[:, group * AT_HEADS:(group + 1) * AT_HEADS].astype(F32)
    span = kw + ATT_TQ
    k = np.arange(span)
    rel = np.asarray(offsets)[:, None] + np.where(k < kw, k, k - span)[None, :]
    diag = jnp.where((np.abs(rel) <= band)[:, None, :],
                     jnp.swapaxes(table[_rel_bucket(rel * dil)], 1, 2), NEG_INF)
    return diag[:, :, None, :]


def _bias_tile(diag_ref, var, kw):
    rows = jnp.broadcast_to(diag_ref[var, 0], (ATT_TQ, diag_ref.shape[3]))
    return pltpu.roll(rows, 0, 1, stride=1, stride_axis=0)[:, :kw]


def _dilated_kernel(q1, q2, q3, k1, k2, k3, v1, v2, v3, b1, b2, b3, o_ref,
                    qn_ref, kn_ref, vn_ref, og_ref, lg_ref):
    seq = o_ref.shape[1]
    contract_last = (((1,), (1,)), ((), ()))
    for g, (q_ref, k_ref, v_ref, bias_ref) in enumerate(((q1, k1, v1, b1), (q2, k2, v2, b2), (q3, k3, v3, b3))):
        dil = AT_GROUPS[g][1]
        kw, tiles = _att_tiles(seq // dil)
        if dil > 1:
            qn_ref[...] = q_ref[0].astype(F32)
            kn_ref[...] = k_ref[0].astype(F32)
            vn_ref[...] = v_ref[0].astype(F32)
        bias = {var: _bias_tile(bias_ref, var, kw) for var in sorted({var for _, _, var in tiles})}
        for r in range(dil):
            for qs, ks, var in tiles:
                rows = lambda start, size: (pl.ds(r + start * dil, size, stride=dil) if dil > 1
                                            else pl.ds(start, size))
                if dil > 1:
                    qt = qn_ref[rows(qs, ATT_TQ), :].astype(BF16)
                    kt = kn_ref[rows(ks, kw), :].astype(BF16)
                    vt = vn_ref[rows(ks, kw), :].astype(BF16)
                else:
                    qt, kt, vt = q_ref[0, rows(qs, ATT_TQ), :], k_ref[0, rows(ks, kw), :], v_ref[0, rows(ks, kw), :]
                s = lax.dot_general(qt, kt, contract_last, preferred_element_type=F32) + bias[var]
                m = jnp.max(s, axis=-1, keepdims=True)
                p = jnp.exp(s - m)
                z = jnp.sum(p, axis=-1, keepdims=True)
                og_ref[g, rows(qs, ATT_TQ), :] = jnp.dot(p.astype(BF16), vt, preferred_element_type=F32) / z
                lg_ref[g, rows(qs, ATT_TQ), :] = jnp.broadcast_to(m + jnp.log(z), (ATT_TQ, HD))
    l0, l1, l2 = lg_ref[0], lg_ref[1], lg_ref[2]
    mx = jnp.maximum(jnp.maximum(l0, l1), l2)
    e0, e1, e2 = jnp.exp(l0 - mx), jnp.exp(l1 - mx), jnp.exp(l2 - mx)
    o_ref[0] = ((e0 * og_ref[0] + e1 * og_ref[1] + e2 * og_ref[2]) / (e0 + e1 + e2)).astype(o_ref.dtype)


def _dilated_attention(z3, rel_bias):
    b, s, _ = z3.shape
    ng = len(AT_GROUPS)
    col = lambda part, g: pl.BlockSpec((1, s, HD), lambda i, h: (i, 0, part * ng * AT_HEADS + g * AT_HEADS + h))
    biases = [_att_bias(rel_bias, g, s) for g in range(ng)]
    bias_spec = lambda a: pl.BlockSpec((a.shape[0], 1) + a.shape[2:], lambda i, h: (0, h, 0, 0))
    return pl.pallas_call(
        _dilated_kernel,
        out_shape=jax.ShapeDtypeStruct((b, s, AT_HEADS * HD), BF16),
        grid=(b, AT_HEADS),
        in_specs=[col(p, g) for p in range(3) for g in range(ng)] + [bias_spec(a) for a in biases],
        out_specs=pl.BlockSpec((1, s, HD), lambda i, h: (i, 0, h)),
        scratch_shapes=[pltpu.VMEM((s, HD), F32), pltpu.VMEM((s, HD), F32), pltpu.VMEM((s, HD), F32),
                        pltpu.VMEM((ng, s, HD), F32), pltpu.VMEM((ng, s, HD), F32)],
        compiler_params=_params("parallel", "parallel"),
        name="dilated_attention",
    )(*([z3] * 9), *biases)


ROUTER_LANES = 128


def _router_kernel(x_ref, g_ref, w_ref, hp_ref, idx_ref, gate_ref, count_ref, carry_ref):
    tm = x_ref.shape[0]
    half = hp_ref.shape[1]

    @pl.when(pl.program_id(0) == 0)
    def _():
        carry_ref[...] = jnp.zeros_like(carry_ref)

    h = _rms_rows(x_ref[...], g_ref[...])
    hp_ref[...] = _bf16_bits(h[:, :half]) | (_bf16_bits(h[:, half:]) << 16)
    h_hi = h.astype(BF16)
    h_lo = (h - h_hi.astype(F32)).astype(BF16)
    logits = (jnp.dot(h_hi, w_ref[0], preferred_element_type=F32)
              + (jnp.dot(h_hi, w_ref[1], preferred_element_type=F32)
                 + jnp.dot(h_lo, w_ref[0], preferred_element_type=F32)))
    lane =lax.broadcasted_iota(jnp.int32, logits.shape, 1).astype(F32)
    logits = jnp.where(lane < N_EXPERTS, logits, -jnp.inf)
    m1 = jnp.max(logits, axis=-1, keepdims=True)
    i1 = jnp.min(jnp.where(logits == m1, lane, float(ROUTER_LANES)), axis=-1, keepdims=True)
    rest = jnp.where(lane == i1, -jnp.inf, logits)
    m2 = jnp.max(rest, axis=-1, keepdims=True)
    i2 = jnp.min(jnp.where(rest == m2, lane, float(ROUTER_LANES)), axis=-1, keepdims=True)
    e2 = jnp.exp(m2 - m1)
    den = 1.0 + e2
    gate_ref[...] = jnp.where(lane == 0, 1.0 / den, e2 / den)
    chosen = jnp.where((lane == i1) | (lane == i2), 1.0, 0.0)
    earlier = lax.broadcasted_iota(jnp.int32, (tm, tm), 0) > lax.broadcasted_iota(jnp.int32, (tm, tm), 1)
    carry = carry_ref[...]
    before = jnp.dot(jnp.where(earlier, 1.0, 0.0).astype(BF16), chosen.astype(BF16),
                     preferred_element_type=F32) + carry
    r1 = jnp.sum(jnp.where(lane == i1, before, 0.0), axis=-1, keepdims=True)
    r2 = jnp.sum(jnp.where(lane == i2, before, 0.0), axis=-1, keepdims=True)
    idx_ref[...] = jnp.where(lane == 0, i1, jnp.where(lane == 1, i2, jnp.where(lane == 2, r1, r2))).astype(jnp.int32)
    carry = carry + jnp.sum(chosen, axis=0, keepdims=True)
    carry_ref[...] = carry
    count_ref[...] = carry.astype(jnp.int32)


def _router(x2, gain, router):
    rows, d = x2.shape
    tm = MOE_CHUNK
    w = jnp.zeros((d, ROUTER_LANES), F32).at[:, :N_EXPERTS].set(router.astype(F32))
    w_hi = w.astype(BF16)
    w = jnp.stack([w_hi, (w - w_hi.astype(F32)).astype(BF16)])
    row_spec = lambda width: pl.BlockSpec((tm, width), lambda i: (i, 0))
    return pl.pallas_call(
        _router_kernel,
        out_shape=(jax.ShapeDtypeStruct((rows, d // 2), jnp.uint32),
                   jax.ShapeDtypeStruct((rows, ROUTER_LANES), jnp.int32),
                   jax.ShapeDtypeStruct((rows, ROUTER_LANES), F32),
                   jax.ShapeDtypeStruct((1, ROUTER_LANES), jnp.int32)),
        grid=(rows // tm,),
        in_specs=[row_spec(d),
                  pl.BlockSpec((1, d), lambda i: (0, 0)),
                  pl.BlockSpec((2, d, ROUTER_LANES), lambda i: (0, 0, 0))],
        out_specs=(row_spec(d // 2), row_spec(ROUTER_LANES), row_spec(ROUTER_LANES),
                   pl.BlockSpec((1, ROUTER_LANES), lambda i: (0, 0))),
        scratch_shapes=[pltpu.VMEM((1, ROUTER_LANES), F32)],
        compiler_params=_params("arbitrary"),
        name="router",
    )(x2, gain.reshape(1, d), w)


def _sc_workers():
    info = plsc.get_sparse_core_info()
    mesh = plsc.VectorSubcoreMesh(core_axis_name="core", subcore_axis_name="subcore")
    return mesh, info.num_cores, info.num_subcores


def _pack_weight_rows(w, col_tile):
    r, c = w.shape
    mesh, nc, ns = _sc_workers()
    rs, ncol = SC_PACK_ROWS, c // col_tile
    per_w = (r // rs) * ncol // (nc * ns)
    assert per_w * nc * ns * rs * col_tile == r * c and per_w % 2 == 0 and col_tile % SC_LANES == 0
    params = pltpu.CompilerParams()
    if "needs_layout_passes" in pltpu.CompilerParams.__dataclass_fields__:
        params = dataclasses.replace(params, needs_layout_passes=False)

    @functools.partial(
        pl.kernel, mesh=mesh, out_type=jax.ShapeDtypeStruct((r // 2, c), jnp.uint32), compiler_params=params,
        cost_estimate=pl.CostEstimate(flops=r * c, transcendentals=0, bytes_accessed=6 * r * c),
        scratch_types=[pltpu.VMEM((2, rs, col_tile), F32), pltpu.VMEM((2, rs // 2, col_tile), jnp.uint32),
                       pltpu.SemaphoreType.DMA((2,)), pltpu.SemaphoreType.DMA((2,))])
    def pack(w_hbm, o_hbm, in_v, out_v, rsem, wsem):
        wid = lax.axis_index("subcore") * nc + lax.axis_index("core")

        @pl.loop(0, per_w // 2)
        def _(it):
            reads, writes = [], []
            for b in range(2):
                tile = wid * per_w + 2 * it + b
                r0 = pl.multiple_of((tile // ncol) * rs, rs)
                c0 = pl.multiple_of((tile % ncol) * col_tile, col_tile)
                reads.append(pltpu.make_async_copy(w_hbm.at[pl.ds(r0, rs), pl.ds(c0, col_tile)], in_v.at[b],
                                                   rsem.at[b]))
                writes.append(pltpu.make_async_copy(
                    out_v.at[b], o_hbm.at[pl.ds(pl.multiple_of(r0 // 2, rs // 2), rs // 2), pl.ds(c0, col_tile)],
                    wsem.at[b]))
            reads[0].start()
            reads[1].start()
            for b in range(2):
                reads[b].wait()
                for pair in range(rs // 2):
                    @plsc.parallel_loop(0, col_tile, step=SC_LANES, unroll=8)
                    def _(j):
                        packed = plsc.pack(in_v[b, 2 * pair, pl.ds(j, SC_LANES)],
                                           in_v[b, 2 * pair + 1, pl.ds(j, SC_LANES)],
                                           format=plsc.PackFormat.INTERLEAVED)
                        out_v[b, pair, pl.ds(j, SC_LANES)] = plsc.bitcast(packed, jnp.uint32)
                writes[b].start()
            writes[0].wait()
            writes[1].wait()

    return pack(w)


def _sc_token_rows(t, nc, ns):
    per_w = t // (nc * ns)
    assert per_w * nc * ns == t and per_w % (2 * SC_ROWS) == 0
    return per_w, per_w // SC_ROWS


def _scatter_rows(table, dests, p_rows, after=()):
    t, w = table.shape
    nk = len(dests)
    mesh, nc, ns = _sc_workers()
    per_w, nit = _sc_token_rows(t, nc, ns)
    ch = SC_ROWS

    @functools.partial(
        pl.kernel, mesh=mesh, out_type=jax.ShapeDtypeStruct((p_rows, w), table.dtype),
        scratch_types=[pltpu.VMEM((nk, nit, ch), jnp.int32), pltpu.VMEM((2, ch, w), table.dtype),
                       pltpu.SemaphoreType.DMA((2,)), pltpu.SemaphoreType.DMA((2, nk))])
    def scatter(table_hbm, *refs):
        dest_hbm, out_hbm = refs[:nk], refs[nk + len(after)]
        idx_v, rows_v, rsem, wsem = refs[nk + len(after) + 1:]
        wid = lax.axis_index("subcore") * nc + lax.axis_index("core")
        for k in range(nk):
            pltpu.sync_copy(dest_hbm[k].at[pl.ds(wid * nit, nit)], idx_v.at[k])

        @pl.loop(0, nit // 2)
        def _(it):
            reads = [pltpu.make_async_copy(table_hbm.at[pl.ds(wid * per_w + (2 * it + b) * ch, ch)],
                                           rows_v.at[b], rsem.at[b]) for b in range(2)]
            writes = [[pltpu.make_async_copy(rows_v.at[b], out_hbm.at[idx_v.at[k].at[2 * it + b]], wsem.at[b, k])
                       for k in range(nk)] for b in range(2)]
            reads[0].start()
            reads[1].start()
            for b in range(2):
                reads[b].wait()
                for k in range(nk):
                    writes[b][k].start()
            for b in range(2):
                for k in range(nk):
                    writes[b][k].wait()

    return scatter(table, *[d.reshape(t // ch, ch) for d in dests], *after)


def _gather_rows(table, idxs):
    t = idxs[0].shape[0]
    w = table.shape[1]
    nk = len(idxs)
    mesh, nc, ns = _sc_workers()
    per_w, nit = _sc_token_rows(t, nc, ns)
    ch = SC_ROWS

    @functools.partial(
        pl.kernel, mesh=mesh, out_type=[jax.ShapeDtypeStruct((t, w), table.dtype)] * nk,
        scratch_types=[pltpu.VMEM((nk, nit, ch), jnp.int32), pltpu.VMEM((nk, 2, ch, w), table.dtype),
                       pltpu.SemaphoreType.DMA((nk, 2)), pltpu.SemaphoreType.DMA((nk, 2))])
    def gather(table_hbm, *refs):
        idx_hbm, out_hbm = refs[:nk], refs[nk:2 * nk]
        idx_v, rows_v, rsem, wsem = refs[2 * nk:]
        wid = lax.axis_index("subcore") * nc + lax.axis_index("core")
        for k in range(nk):
            pltpu.sync_copy(idx_hbm[k].at[pl.ds(wid * nit, nit)], idx_v.at[k])

        @pl.loop(0, nit // 2)
        def _(it):
            slots = [(k, b) for k in range(nk) for b in range(2)]
            reads = {(k, b): pltpu.make_async_copy(table_hbm.at[idx_v.at[k].at[2 * it + b]], rows_v.at[k, b],
                                                   rsem.at[k, b]) for k, b in slots}
            writes = {(k, b): pltpu.make_async_copy(rows_v.at[k, b],
                                                    out_hbm[k].at[pl.ds(wid * per_w + (2 * it + b) * ch, ch)],
                                                    wsem.at[k, b]) for k, b in slots}
            for s in slots:
                reads[s].start()
            for s in slots:
                reads[s].wait()
                writes[s].start()
            for s in slots:
                writes[s].wait()

    return gather(table, *[i.reshape(t // ch, ch) for i in idxs])


def _combine_kernel(x_ref, y0_ref, y1_ref, gate_ref, *rest):
    o_ref = rest[-1]
    half = x_ref.shape[1] // 2
    for part in range(2):
        cols = slice(part * half, (part + 1) * half)
        acc = x_ref[:, cols]
        for k, y_ref in enumerate((y0_ref, y1_ref)):
            word = y_ref[...]
            bits = (word << 16) if part == 0 else (word & jnp.uint32(0xFFFF0000))
            acc = acc + gate_ref[:, k:k + 1] * lax.bitcast_convert_type(bits, F32)
        o_ref[:, cols] = acc


def _combine(x2, y0, y1, gates, part, earlier=None):
    t, d = x2.shape
    tm = ROW_TILE
    steps = y0.shape[0] // tm
    here = lambda i: (part * steps + i, 0)
    in_specs = [pl.BlockSpec((tm, d), here),
                pl.BlockSpec((tm, d // 2), lambda i: (i, 0)),
                pl.BlockSpec((tm, d // 2), lambda i: (i, 0)),
                pl.BlockSpec((tm, TOP_K), here)]
    operands = [x2, y0, y1, gates]
    if earlier is not None:
        in_specs.append(pl.BlockSpec(memory_space=pl.ANY))
        operands.append(earlier)
    return pl.pallas_call(
        _combine_kernel,
        out_shape=jax.ShapeDtypeStruct((t, d), F32),
        grid=(steps,),
        in_specs=in_specs,
        out_specs=pl.BlockSpec((tm, d), here),
        input_output_aliases={} if earlier is None else {len(operands) - 1: 0},
        compiler_params=_params("parallel"),
        name="moe_combine",
    )(*operands)


def _moe(x2, gain, router, wg, wu, wd):
    t, d = x2.shape
    hp, idx, gate, count = _router(x2, gain, router)
    experts = jnp.arange(N_EXPERTS, dtype=jnp.int32)
    counts = count[0, :N_EXPERTS]
    padded = (counts + MOE_ROWS - 1) // MOE_ROWS * MOE_ROWS
    pend = jnp.cumsum(padded)
    pstart = pend - padded
    expert, rank = idx[:, :TOP_K], idx[:, TOP_K:2 * TOP_K]
    dest = jnp.sum(jnp.where(expert[:, :, None] == experts, pstart, 0), axis=-1) + rank
    dests = [dest[:, k] for k in range(TOP_K)]
    p_rows = t * TOP_K + N_EXPERTS * MOE_ROWS
    blk_row = jnp.arange(p_rows // MOE_ROWS, dtype=jnp.int32) * MOE_ROWS
    blk_expert = jnp.minimum(jnp.sum(pend[None, :] <= blk_row[:, None], axis=1), N_EXPERTS - 1).astype(jnp.int32)
    valid = jnp.clip(counts[blk_expert] - (blk_row - pstart[blk_expert]), 0, MOE_ROWS)
    valid = jnp.where(blk_row < pend[-1], valid, 0).astype(jnp.int32)
    n_exp, _, f = wg.shape
    pack = lambda w, col_tile: _pack_weight_rows(w.reshape(-1, w.shape[2]), col_tile).reshape(n_exp, -1, w.shape[2])
    wg, wu, wd = pack(wg, f // 2), pack(wu, f // 2), pack(wd, d)
    hb = _scatter_rows(hp, dests, p_rows, after=(wg, wu, wd))
    yb = _ffn_experts(hb, wg, wu, wd, blk_expert, valid)
    out = None
    for part in range(MOE_TAIL_PARTS):
        rows = slice(part * t // MOE_TAIL_PARTS, (part + 1) * t // MOE_TAIL_PARTS)
        y0, y1 = _gather_rows(yb, [dk[rows] for dk in dests])
        out = _combine(x2, y0, y1, gate[:, :TOP_K], part, out)
    return out


def kernel(x, mem, rel_bias, norm_mix, norm_mem, norm_ffn, w_mem_kv, xq_norm, xk_norm, w_out, hy_w_in, hy_conv_w, hy_conv_b, hy_filt_w1, hy_filt_b1, hy_filt_w2, hy_filt_b2, hy_filt_w3, hy_sin_freq, hy_skip, at_w_in, at_q_norm, at_k_norm, ffn_w_gate, ffn_w_up, ffn_w_down, moe_router, moe_w_gate, moe_w_up, moe_w_down):
    b, s, d = x.shape
    t = b * s
    m_len = mem.shape[1]
    x2 = x.reshape(t, d)
    mem2 = mem.reshape(b * m_len, d)
    bf = lambda w: w.astype(BF16)
    score_scale = HD ** -0.5
    xa_heads = XA_W // HD
    at_heads = AT_W // HD
    plain = lambda width: [None] * (width // HD)
    kv_gains = lambda i: [xk_norm[i]] * xa_heads + plain(XA_W)

    tables = _dft_tables(s)
    spectra, hq = _hyena_filters(s, tables, hy_filt_w1[0], hy_filt_b1[0], hy_filt_w2[0], hy_filt_b2[0],
                                 hy_filt_w3[0], hy_sin_freq[0])
    z = _norm_matmul(x2, norm_mix[0], bf(hy_w_in[0]), 1024,
                     plain(3 * HY_C) + [xq_norm[0] * score_scale] * xa_heads).reshape(b, s, -1)
    kv = _norm_matmul(mem2, norm_mem[0], bf(w_mem_kv[0]), 1024, kv_gains(0)).reshape(b, m_len, -1)
    self_out = _hyena(z, hy_conv_w[0], hy_conv_b[0], tables, spectra, hq, hy_skip[0])
    x2 = _mix_out(x2.reshape(b, s, d), self_out, z, 3 * HY_C // XA_W, kv, bf(w_out[0])).reshape(t, d)
    x2 = _ffn_dense(x2, norm_ffn[0], bf(ffn_w_gate[0]), bf(ffn_w_up[0]), bf(ffn_w_down[0]))

    at_gains = ([at_q_norm[0] * score_scale] * at_heads + [at_k_norm[0]] * at_heads + plain(AT_W)
                + [xq_norm[1] * score_scale] * xa_heads)
    z = _norm_matmul(x2, norm_mix[1], bf(at_w_in[0]), 1024, at_gains).reshape(b, s, -1)
    kv = _norm_matmul(mem2, norm_mem[1], bf(w_mem_kv[1]), 1024, kv_gains(1)).reshape(b, m_len, -1)
    self_out = _dilated_attention(z, rel_bias)
    x2 = _mix_out(x2.reshape(b, s, d), self_out, z, 3 * AT_W // XA_W, kv, bf(w_out[1])).reshape(t, d)
    x2 = _moe(x2, norm_ffn[1], moe_router[0], moe_w_gate[0], moe_w_up[0], moe_w_down[0])
    return x2.reshape(b, s, d)
```

```python
import dataclasses
import functools
import math

import jax
import jax.numpy as jnp
import numpy as np
from jax import lax
from jax.experimental import pallas as pl
from jax.experimental.pallas import tpu as pltpu
from jax.experimental.pallas import tpu_sc as plsc

F32 = jnp.float32
BF16 = jnp.bfloat16

D_MODEL = 1024
EPS = 1e-6
HY_C = 512
FILT_BANDS = 16
DECAY_TARGET = 1e-2
FAST_DECAY_PCT = 0.3
SLOW_DECAY_PCT = 1.5
MOD_SHIFT = 0.05
AT_GROUPS = ((128, 1), (512, 4), (2048, 16))
AT_HEADS = 4
HD = 128
AT_W = 1536
NUM_BUCKETS = 32
REL_MAX_DIST = 1024
NEG_INF = -1e30
XA_W = 512
D_FF = 2816
N_EXPERTS = 8
TOP_K = 2

VMEM_LIMIT_BYTES = 56 * 1024 * 1024
ROW_TILE = 512
FF_TILE = 256
MOE_ROWS = 512
MOE_CHUNK = 512
MOE_TAIL_PARTS = 2
SC_ROWS = 32
SC_LANES = 16
SC_PACK_ROWS = 16
ATT_TQ = 128
ATT_KW = 256
HY_TC = 256
HY_KT = 512


def _params(*sem):
    return pltpu.CompilerParams(dimension_semantics=sem, vmem_limit_bytes=VMEM_LIMIT_BYTES)


def _rms_rows(x, gain):
    return x * lax.rsqrt(jnp.mean(x * x, axis=-1, keepdims=True) + EPS) * gain


def _bf16_bits(v):
    u = lax.bitcast_convert_type(v, jnp.uint32)
    return (u + jnp.uint32(0x7FFF) + ((u >> 16) & jnp.uint32(1))) >> 16


def _resident(shape, index_map):
    return pl.BlockSpec(shape, index_map, pipeline_mode=pl.Buffered(1))


def _norm_matmul_kernel(x_ref, g_ref, w_ref, hg_ref, o_ref, *, tn, head_norm):
    h = _rms_rows(x_ref[...], g_ref[...]).astype(BF16)
    for c in range(o_ref.shape[1] // tn):
        acc = jnp.dot(h, w_ref[:, c * tn:(c + 1) * tn], preferred_element_type=F32)
        for j in range(tn // HD):
            cols = slice(c * tn + j * HD, c * tn + (j + 1) * HD)
            seg = acc[:, j * HD:(j + 1) * HD]
            if head_norm[cols.start // HD]:
                seg = _rms_rows(seg, hg_ref[:, cols])
            o_ref[:, cols] = seg.astype(o_ref.dtype)


def _norm_matmul(x2, gain, w_bf, tn, head_gains):
    rows, d = x2.shape
    n = w_bf.shape[1]
    tm = min(ROW_TILE, rows)
    assert len(head_gains) * HD == n
    hg = jnp.concatenate([jnp.ones((HD,), F32) if g is None else g.astype(F32) for g in head_gains]).reshape(1, n)
    return pl.pallas_call(
        functools.partial(_norm_matmul_kernel, tn=tn, head_norm=tuple(g is not None for g in head_gains)),
        out_shape=jax.ShapeDtypeStruct((rows, n), BF16),
        grid=(rows // tm,),
        in_specs=[pl.BlockSpec((tm, d), lambda i: (i, 0)),
                  pl.BlockSpec((1, d), lambda i: (0, 0)),
                  _resident((d, n), lambda i: (0, 0)),
                  pl.BlockSpec((1, n), lambda i: (0, 0))],
        out_specs=pl.BlockSpec((tm, n), lambda i: (i, 0)),
        compiler_params=_params("parallel"),
        cost_estimate=pl.CostEstimate(flops=2 * rows * d * n, transcendentals=rows,
                                      bytes_accessed=4 * rows * d + 2 * d * n + 2 * rows * n),
        name="norm_matmul",
    )(x2, gain.reshape(1, d), w_bf, hg)


def _mix_out_kernel(x_ref, a_ref, xq_ref, kv_ref, wa_ref, wc_ref, o_ref):
    heads = []
    for h in range(XA_W // HD):
        cols = slice(h * HD, (h + 1) * HD)
        s = lax.dot_general(xq_ref[0, :, cols], kv_ref[0, :, cols], (((1,), (1,)), ((), ())),
                            preferred_element_type=F32)
        p = jnp.exp(s - jnp.max(s, axis=-1, keepdims=True))
        z = jnp.sum(p, axis=-1, keepdims=True)
        pv = jnp.dot(p.astype(BF16), kv_ref[0, :, XA_W + h * HD:XA_W + (h + 1) * HD], preferred_element_type=F32)
        heads.append((pv / z).astype(BF16))
    cross = jnp.concatenate(heads, axis=1)
    o_ref[0] = (x_ref[0]
                + jnp.dot(a_ref[0], wa_ref[...], preferred_element_type=F32)
                + jnp.dot(cross, wc_ref[...], preferred_element_type=F32))


def _mix_out(x3, self_out, z3, xq_block, kv3, w_bf):
    b, s, d = x3.shape
    half = self_out.shape[2]
    m = kv3.shape[1]
    tm = ROW_TILE
    return pl.pallas_call(
        _mix_out_kernel,
        out_shape=jax.ShapeDtypeStruct((b, s, d), F32),
        grid=(b, s // tm),
        in_specs=[pl.BlockSpec((1, tm, d), lambda i, j: (i, j, 0)),
                  pl.BlockSpec((1, tm, half), lambda i, j: (i, j, 0)),
                  pl.BlockSpec((1, tm, XA_W), lambda i, j: (i, j, xq_block)),
                  pl.BlockSpec((1, m, 2 * XA_W), lambda i, j: (i, 0, 0)),
                  pl.BlockSpec((half, d), lambda i, j: (0, 0)),
                  pl.BlockSpec((d - half, d), lambda i, j: (1, 0))],
        out_specs=pl.BlockSpec((1, tm, d), lambda i, j: (i, j, 0)),
        compiler_params=_params("parallel", "parallel"),
        name="mix_out",
    )(x3, self_out, z3, kv3, w_bf, w_bf)


def _weight_rows(ref, start, size):
    if ref.dtype == jnp.uint32:
        return lambda cols: pltpu.bitcast(ref[start // 2:(start + size) // 2, cols], BF16)
    return lambda cols: ref[start:start + size, cols]


def _weight_shape(ref):
    return (ref.shape[0] * (2 if ref.dtype == jnp.uint32 else 1), ref.shape[1])


def _swiglu(h, wg_ref, wu_ref, wd_ref, tf):
    d, f = _weight_shape(wg_ref)
    assert f % tf == 0
    y = None
    for j in range(f // tf):
        cols = slice(j * tf, (j + 1) * tf)
        gg = jnp.dot(h, _weight_rows(wg_ref, 0, d)(cols), preferred_element_type=F32)
        uu = jnp.dot(h, _weight_rows(wu_ref, 0, d)(cols), preferred_element_type=F32)
        a = ((gg * jax.nn.sigmoid(gg)) * uu).astype(BF16)
        part = jnp.dot(a, _weight_rows(wd_ref, j * tf, tf)(slice(None)), preferred_element_type=F32)
        y = part if y is None else y + part
    return y


def _ffn_dense_kernel(x_ref, g_ref, wg_ref, wu_ref, wd_ref, o_ref, *, tf):
    x = x_ref[...]
    h = _rms_rows(x, g_ref[...]).astype(BF16)
    o_ref[...] = x + _swiglu(h, wg_ref, wu_ref, wd_ref, tf)


def _ffn_dense(x2, gain, wg, wu, wd):
    rows, d = x2.shape
    f = wg.shape[1]
    tm = ROW_TILE
    return pl.pallas_call(
        functools.partial(_ffn_dense_kernel, tf=FF_TILE),
        out_shape=jax.ShapeDtypeStruct((rows, d), F32),
        grid=(rows // tm,),
        in_specs=[pl.BlockSpec((tm, d), lambda i: (i, 0)),
                  pl.BlockSpec((1, d), lambda i: (0, 0)),
                  _resident((d, f), lambda i: (0, 0)),
                  _resident((d, f), lambda i: (0, 0)),
                  _resident((f, d), lambda i: (0, 0))],
        out_specs=pl.BlockSpec((tm, d), lambda i: (i, 0)),
        compiler_params=_params("parallel"),
        cost_estimate=pl.CostEstimate(flops=6 * rows * d * f, transcendentals=rows * f,
                                      bytes_accessed=8 * rows * d + 6 * d * f),
        name="ffn_dense",
    )(x2, gain.reshape(1, d), wg, wu, wd)


def _ffn_expert_kernel(eid_ref, valid_ref, row_block_ref, hp_ref, wg_ref, wu_ref, wd_ref, o_ref, h_ref, *, tf):
    half = hp_ref.shape[1]
    valid = valid_ref[pl.program_id(0)]

    @pl.when(valid > 0)
    def _():
        keep = lax.broadcasted_iota(jnp.int32, (hp_ref.shape[0], 1), 0) < valid
        word = hp_ref[...]
        h_ref[:, :half] = jnp.where(keep, lax.bitcast_convert_type(word << 16, F32), 0.0).astype(BF16)
        h_ref[:, half:] = jnp.where(keep, lax.bitcast_convert_type(word & jnp.uint32(0xFFFF0000), F32),
                                    0.0).astype(BF16)
        y = _swiglu(h_ref[...], wg_ref.at[0], wu_ref.at[0], wd_ref.at[0], tf)
        o_ref[...] = _bf16_bits(y[:, :half]) | (_bf16_bits(y[:, half:]) << 16)

    @pl.when(valid <= 0)
    def _():
        o_ref[...] = jnp.zeros_like(o_ref)


def _ffn_experts(hp, wg, wu, wd, eid, valid, row_block):
    rows, half = hp.shape
    d = 2 * half
    tm = MOE_ROWS
    expert_spec = lambda w: pl.BlockSpec((1,) + w.shape[1:], lambda i, e, n, rb: (e[i], 0, 0))
    grid_spec = pltpu.PrefetchScalarGridSpec(
        num_scalar_prefetch=3,
        grid=(eid.shape[0],),
        in_specs=[pl.BlockSpec((tm, half), lambda i, e, n, rb: (rb[i], 0)),
                  expert_spec(wg), expert_spec(wu), expert_spec(wd)],
        out_specs=pl.BlockSpec((tm, half), lambda i, e, n, rb: (rb[i], 0)),
        scratch_shapes=[pltpu.VMEM((tm, d), BF16)],
    )
    return pl.pallas_call(
        functools.partial(_ffn_expert_kernel, tf=FF_TILE),
        out_shape=jax.ShapeDtypeStruct((rows, half), jnp.uint32),
        grid_spec=grid_spec,
        compiler_params=_params("arbitrary"),
        name="ffn_experts",
    )(eid, valid, row_block, hp, wg, wu, wd)


def _dft_kernel(ce_ref, se_ref, co_ref, so_ref, cot_ref, sot_ref, base_ref, *, n_fft):
    rows, cols = ce_ref.shape
    i = pl.program_id(0)
    theta = 2.0 * math.pi / n_fft

    def phases(r, c):
        return r * (2 * c), r * (2 * c + 1), c * (2 * r + 1)

    @pl.when(i == 0)
    def _():
        r = lax.broadcasted_iota(jnp.int32, (rows, cols), 0)
        c = lax.broadcasted_iota(jnp.int32, (rows, cols), 1)
        for f, ph in enumerate(phases(r, c)):
            ang = (ph & (n_fft - 1)).astype(F32) * theta
            base_ref[2 * f] = jnp.cos(ang)
            base_ref[2 * f + 1] = jnp.sin(ang)

    c = lax.broadcasted_iota(jnp.int32, (8, cols), 1)
    r0 = i * rows
    shifts = (r0 * (2 * c), r0 * (2 * c + 1), c * (2 * r0))
    for f, (c_ref, s_ref) in enumerate(((ce_ref, se_ref), (co_ref, so_ref), (cot_ref, sot_ref))):
        ang = (shifts[f] & (n_fft - 1)).astype(F32) * theta
        ca, sa = jnp.cos(ang)[0:1], jnp.sin(ang)[0:1]
        cb, sb = base_ref[2 * f], base_ref[2 * f + 1]
        c_ref[...] = (cb * ca - sb * sa).astype(BF16)
        s_ref[...] = (sb * ca + cb * sa).astype(BF16)


def _dft_tables(length):
    half, rows = length // 2, 128
    shape = jax.ShapeDtypeStruct((half, half), BF16)
    spec = pl.BlockSpec((rows, half), lambda i: (i, 0))
    return pl.pallas_call(
        functools.partial(_dft_kernel, n_fft=2 * length),
        out_shape=(shape,) * 6,
        grid=(half // rows,),
        out_specs=(spec,) * 6,
        scratch_shapes=[pltpu.VMEM((6, rows, half), F32)],
        compiler_params=_params("arbitrary"),
        name="dft_tables",
    )()


LANES = 128


def _split_parity(tmp_ref, x):
    half = x.shape[0] // 2
    for j in range(tmp_ref.shape[0]):
        tmp_ref[j] = x[:, j * LANES:(j + 1) * LANES]
    pick = lambda start: jnp.concatenate(
        [tmp_ref[j, pl.ds(start, half, stride=2), :] for j in range(tmp_ref.shape[0])], axis=1)
    return pick(0), pick(1)


def _merge_parity(tmp_ref, even, odd):
    half = even.shape[0]
    for j in range(tmp_ref.shape[0]):
        tmp_ref[j, pl.ds(0, half, stride=2), :] = even[:, j * LANES:(j + 1) * LANES]
        tmp_ref[j, pl.ds(1, half, stride=2), :] = odd[:, j * LANES:(j + 1) * LANES]
    return jnp.concatenate([tmp_ref[j] for j in range(tmp_ref.shape[0])], axis=1)


def _filter_time_kernel(feats_ref, w1_ref, b1_ref, w2_ref, b2_ref, fr_ref, w3_ref, t_ref, delta_ref,
                        g_ref, hq_ref, tmp_ref):
    hp = lax.Precision.HIGHEST
    length = feats_ref.shape[0]
    half = length // 2
    fr = fr_ref[...]
    h = jnp.sin(fr * (jnp.dot(feats_ref[...], w1_ref[...], preferred_element_type=F32, precision=hp) + b1_ref[...]))
    h = jnp.sin(fr * (jnp.dot(h, w2_ref[...], preferred_element_type=F32, precision=hp) + b2_ref[...]))
    mod = jnp.exp(-t_ref[...] * delta_ref[...]) + MOD_SHIFT
    row = lax.broadcasted_iota(jnp.int32, (length, 1), 0)
    alt = jnp.where(lax.broadcasted_iota(jnp.int32, (half, 1), 0) % 2 == 0, 1.0, -1.0).astype(F32)
    for o in range(2):
        fwd = jnp.dot(h, w3_ref[2 * o], preferred_element_type=F32, precision=hp) * mod
        bwd = jnp.dot(h, w3_ref[2 * o + 1], preferred_element_type=F32, precision=hp) * mod
        bwd = jnp.where(row == 0, 0.0, bwd)
        norm = (jnp.sum(jnp.abs(fwd), axis=0, keepdims=True)
                + jnp.sum(jnp.abs(bwd), axis=0, keepdims=True) + 1e-6)
        fwd = fwd / norm
        bwd = bwd / norm
        for part, g in enumerate((fwd + bwd, fwd - bwd)):
            even, odd = _split_parity(tmp_ref, g)
            g_ref[o, 2 * part] = even.astype(BF16)
            g_ref[o, 2 * part + 1] = odd.astype(BF16)
            mid = jnp.sum((even if part == 0 else odd) * alt, axis=0, keepdims=True) * (1.0 / length)
            hq_ref[o, part] = mid if part == 0 else -mid


def _filter_freq_kernel(ce_ref, se_ref, co_ref, so_ref, g_ref, h_ref):
    half = ce_ref.shape[0]
    row = lax.broadcasted_iota(jnp.int32, (half, 1), 0)
    scale = jnp.where(row == 0, 1.0, 2.0).astype(F32) * (1.0 / (4 * half))
    a = jnp.dot(ce_ref[...], g_ref[0, 0], preferred_element_type=F32)
    b = jnp.dot(co_ref[...], g_ref[0, 1], preferred_element_type=F32)
    c = jnp.dot(se_ref[...], g_ref[0, 2], preferred_element_type=F32)
    d = jnp.dot(so_ref[...], g_ref[0, 3], preferred_element_type=F32)
    h_ref[0, 0] = (a + b) * scale
    h_ref[0, 1] = -(c + d) * scale
    h_ref[0, 2] = (a - b) * scale
    h_ref[0, 3] = (c - d) * scale


def _hyena_filters(length, tables, w1, b1, w2, b2, w3, freq):
    t = jnp.linspace(0.0, 1.0, length, dtype=F32)[:, None]
    f = jnp.linspace(1e-4, FILT_BANDS - 1, FILT_BANDS, dtype=F32)[None]
    ang = (2.0 * math.pi / length) * jnp.arange(length, dtype=F32)[:, None] * f
    feats = jnp.concatenate([t, jnp.cos(ang), -jnp.sin(ang)], axis=-1)
    deltas = jnp.abs(jnp.linspace(math.log(DECAY_TARGET) / SLOW_DECAY_PCT,
                                  math.log(DECAY_TARGET) / FAST_DECAY_PCT, HY_C, dtype=F32))[None]
    hid = w1.shape[1]
    w3r = w3.reshape(hid, 4, HY_C).transpose(1, 0, 2)
    tc = HY_TC
    full = lambda shape: pl.BlockSpec(shape, lambda c: (0,) * len(shape))
    half = length // 2
    g, hq = pl.pallas_call(
        _filter_time_kernel,
        out_shape=(jax.ShapeDtypeStruct((2, 4, half, HY_C), BF16),
                   jax.ShapeDtypeStruct((2, 2, 1, HY_C), F32)),
        grid=(HY_C // tc,),
        in_specs=[full(feats.shape), full(w1.shape), full((1, hid)), full(w2.shape), full((1, hid)),
                  full((1, hid)), pl.BlockSpec((4, hid, tc), lambda c: (0, 0, c)), full((length, 1)),
                  pl.BlockSpec((1, tc), lambda c: (0, c))],
        out_specs=(pl.BlockSpec((2, 4, half, tc), lambda c: (0, 0, 0, c)),
                   pl.BlockSpec((2, 2, 1, tc), lambda c: (0, 0, 0, c))),
        scratch_shapes=[pltpu.VMEM((tc // LANES, length, LANES), F32)],
        compiler_params=_params("parallel"),
        name="filter_time",
    )(feats, w1, b1.reshape(1, hid), w2, b2.reshape(1, hid), freq.reshape(1, hid), w3r, t, deltas)
    ce, se, co, so = tables[:4]
    table_spec = _resident((half, half), lambda o, c: (0, 0))
    spectra = pl.pallas_call(
        _filter_freq_kernel,
        out_shape=jax.ShapeDtypeStruct((2, 4, half, HY_C), F32),
        grid=(2, HY_C // tc),
        in_specs=[table_spec] * 4 + [pl.BlockSpec((1, 4, half, tc), lambda o, c: (o, 0, 0, c))],
        out_specs=pl.BlockSpec((1, 4, half, tc), lambda o, c: (o, 0, 0, c)),
        compiler_params=_params("parallel", "parallel"),
        name="filter_freq",
    )(ce, se, co, so, g)
    return spectra, hq


def _hyena_kernel(z0_ref, z1_ref, z2_ref, cw_ref, cb_ref, ce_ref, se_ref, co_ref, so_ref, cot_ref, sot_ref,
                  h_ref, hq_ref, skip_ref, o_ref, tmp_ref):
    length = z0_ref.shape[1]
    half = length // 2
    row = lax.broadcasted_iota(jnp.int32, (length, 1), 0)
    alt = jnp.where(lax.broadcasted_iota(jnp.int32, (half, 1), 0) % 2 == 0, 1.0, -1.0).astype(F32)

    def short_conv(z_ref, c):
        u = z_ref[0].astype(F32)
        prev = jnp.where(row == 0, 0.0, pltpu.roll(u, 1, axis=0))
        nxt = jnp.where(row == length - 1, 0.0, pltpu.roll(u, length - 1, axis=0))
        return cb_ref[c] + (prev * cw_ref[0, c] + u * cw_ref[1, c] + nxt * cw_ref[2, c])

    def long_conv(u, o):
        ue, uo = _split_parity(tmp_ref, u)
        ue_bf, uo_bf = ue.astype(BF16), uo.astype(BF16)
        a4 = jnp.sum(ue * alt, axis=0, keepdims=True)
        b4 = jnp.sum(uo * alt, axis=0, keepdims=True)
        hr4, hi4 = hq_ref[o, 0], hq_ref[o, 1]
        y_even = alt * (a4 * hr4 + b4 * hi4)
        y_odd = -(alt * (a4 * hi4 - b4 * hr4))
        for kt in range(half // HY_KT):
            ks = slice(kt * HY_KT, (kt + 1) * HY_KT)
            ae = jnp.dot(ce_ref[ks, :], ue_bf, preferred_element_type=F32)
            ao = jnp.dot(co_ref[ks, :], uo_bf, preferred_element_type=F32)
            be = jnp.dot(se_ref[ks, :], ue_bf, preferred_element_type=F32)
            bo = jnp.dot(so_ref[ks, :], uo_bf, preferred_element_type=F32)
            a_f, a_g, b_f, b_g = ae + ao, ae - ao, be + bo, bo - be
            hr_f, hi_f, hr_g, hi_g = (h_ref[o, j, ks, :] for j in range(4))
            re_f, im_f = a_f * hr_f + b_f * hi_f, a_f * hi_f - b_f * hr_f
            re_g, im_g = a_g * hr_g + b_g * hi_g, a_g * hi_g - b_g * hr_g
            y_even += (jnp.dot(ce_ref[:, ks], (re_f + re_g).astype(BF16), preferred_element_type=F32)
                       - jnp.dot(se_ref[:, ks], (im_f - im_g).astype(BF16), preferred_element_type=F32))
            y_odd += (jnp.dot(cot_ref[:, ks], (re_f - re_g).astype(BF16), preferred_element_type=F32)
                      - jnp.dot(sot_ref[:, ks], (im_f + im_g).astype(BF16), preferred_element_type=F32))
        return _merge_parity(tmp_ref, y_even, y_odd) + u * skip_ref[o]

    z = short_conv(z1_ref, 1) * long_conv(short_conv(z0_ref, 0), 0)
    o_ref[0] = (short_conv(z2_ref, 2) * long_conv(z, 1)).astype(o_ref.dtype)


def _hyena(z3, conv_w, conv_b, tables, spectra, hq, skip):
    b, s, _ = z3.shape
    half = s // 2
    tc = HY_TC
    nct = HY_C // tc
    cw = conv_w.reshape(3, 3, 1, HY_C)
    cb = conv_b.reshape(3, 1, HY_C)
    zspec = lambda chunk: pl.BlockSpec((1, s, tc), lambda c, i: (i, 0, chunk * nct + c))
    return pl.pallas_call(
        _hyena_kernel,
        out_shape=jax.ShapeDtypeStruct((b, s, HY_C), BF16),
        grid=(nct, b),
        in_specs=[zspec(0), zspec(1), zspec(2),
                  pl.BlockSpec((3, 3, 1, tc), lambda c, i: (0, 0, 0, c)),
                  pl.BlockSpec((3, 1, tc), lambda c, i: (0, 0, c))]
                 + [_resident((half, half), lambda c, i: (0, 0))] * 6
                 + [_resident((2, 4, half, tc), lambda c, i: (0, 0, 0, c)),
                    pl.BlockSpec((2, 2, 1, tc), lambda c, i: (0, 0, 0, c)),
                    pl.BlockSpec((2, 1, tc), lambda c, i: (0, 0, c))],
        out_specs=pl.BlockSpec((1, s, tc), lambda c, i: (i, 0, c)),
        scratch_shapes=[pltpu.VMEM((tc // LANES, s, LANES), F32)],
        compiler_params=_params("parallel", "parallel"),
        cost_estimate=pl.CostEstimate(flops=2 * 4 * 2 * b * s * half * HY_C, transcendentals=0,
                                      bytes_accessed=2 * 3 * b * s * HY_C + 4 * b * s * HY_C + 12 * half * half),
        name="hyena",
    )(z3, z3, z3, cw, cb, *tables, spectra, hq, skip.reshape(2, 1, HY_C))


def _rel_bucket(rel):
    half = NUM_BUCKETS // 2
    exact = half // 2
    n = np.abs(rel)
    large = exact + (np.log(np.maximum(n, 1) / exact) / np.log(REL_MAX_DIST / exact) * (half - exact)).astype(np.int32)
    large = np.minimum(large, half - 1)
    return (np.where(rel > 0, half, 0) + np.where(n < exact, n, large)).astype(np.int32)


def _att_tiles(length):
    kw = min(ATT_KW, length)
    tiles = []
    for qs in range(0, length, ATT_TQ):
        ks = min(max(qs - (kw - ATT_TQ) // 2, 0), length - kw)
        tiles.append((qs, ks, {0: 0, -64: 1, -128: 2}[ks - qs]))
    return kw, tiles


def _att_bias(rel_bias, group, seq):
    window, dil = AT_GROUPS[group]
    band = window // (2 * dil)
    kw, tiles = _att_tiles(seq // dil)
    offsets = sorted({ks - qs for qs, ks, _ in tiles}, reverse=True)
    table = rel_bias[:, group * AT_HEADS:(group + 1) * AT_HEADS].astype(F32)
    span = kw + ATT_TQ
    k = np.arange(span)
    rel = np.asarray(offsets)[:, None] + np.where(k < kw, k, k - span)[None, :]
    diag = jnp.where((np.abs(rel) <= band)[:, None, :],
                     jnp.swapaxes(table[_rel_bucket(rel * dil)], 1, 2), NEG_INF)
    return diag[:, :, None, :]


def _bias_tile(diag_ref, var, kw):
    rows = jnp.broadcast_to(diag_ref[var, 0], (ATT_TQ, diag_ref.shape[3]))
    return pltpu.roll(rows, 0, 1, stride=1, stride_axis=0)[:, :kw]


def _dilated_kernel(q1, q2, q3, k1, k2, k3, v1, v2, v3, b1, b2, b3, o_ref,
                    qn_ref, kn_ref, vn_ref, og_ref, lg_ref):
    seq = o_ref.shape[1]
    contract_last = (((1,), (1,)), ((), ()))
    for g, (q_ref, k_ref, v_ref, bias_ref) in enumerate(((q1, k1, v1, b1), (q2, k2, v2, b2), (q3, k3, v3, b3))):
        dil = AT_GROUPS[g][1]
        kw, tiles = _att_tiles(seq // dil)
        if dil > 1:
            qn_ref[...] = q_ref[0].astype(F32)
            kn_ref[...] = k_ref[0].astype(F32)
            vn_ref[...] = v_ref[0].astype(F32)
        bias = {var: _bias_tile(bias_ref, var, kw) for var in sorted({var for _, _, var in tiles})}
        for r in range(dil):
            for qs, ks, var in tiles:
                rows = lambda start, size: (pl.ds(r + start * dil, size, stride=dil) if dil > 1
                                            else pl.ds(start, size))
                if dil > 1:
                    qt = qn_ref[rows(qs, ATT_TQ), :].astype(BF16)
                    kt = kn_ref[rows(ks, kw), :].astype(BF16)
                    vt = vn_ref[rows(ks, kw), :].astype(BF16)
                else:
                    qt, kt, vt = q_ref[0, rows(qs, ATT_TQ), :], k_ref[0, rows(ks, kw), :], v_ref[0, rows(ks, kw), :]
                s = lax.dot_general(qt, kt, contract_last, preferred_element_type=F32) + bias[var]
                m = jnp.max(s, axis=-1, keepdims=True)
                p = jnp.exp(s - m)
                z = jnp.sum(p, axis=-1, keepdims=True)
                og_ref[g, rows(qs, ATT_TQ), :] = jnp.dot(p.astype(BF16), vt, preferred_element_type=F32) / z
                lg_ref[g, rows(qs, ATT_TQ), :] = jnp.broadcast_to(m + jnp.log(z), (ATT_TQ, HD))
    l0, l1, l2 = lg_ref[0], lg_ref[1], lg_ref[2]
    mx = jnp.maximum(jnp.maximum(l0, l1), l2)
    e0, e1, e2 = jnp.exp(l0 - mx), jnp.exp(l1 - mx), jnp.exp(l2 - mx)
    o_ref[0] = ((e0 * og_ref[0] + e1 * og_ref[1] + e2 * og_ref[2]) / (e0 + e1 + e2)).astype(o_ref.dtype)


def _dilated_attention(z3, rel_bias):
    b, s, _ = z3.shape
    ng = len(AT_GROUPS)
    col = lambda part, g: pl.BlockSpec((1, s, HD), lambda i, h: (i, 0, part * ng * AT_HEADS + g * AT_HEADS + h))
    biases = [_att_bias(rel_bias, g, s) for g in range(ng)]
    bias_spec = lambda a: pl.BlockSpec((a.shape[0], 1) + a.shape[2:], lambda i, h: (0, h, 0, 0))
    return pl.pallas_call(
        _dilated_kernel,
        out_shape=jax.ShapeDtypeStruct((b, s, AT_HEADS * HD), BF16),
        grid=(b, AT_HEADS),
        in_specs=[col(p, g) for p in range(3) for g in range(ng)] + [bias_spec(a) for a in biases],
        out_specs=pl.BlockSpec((1, s, HD), lambda i, h: (i, 0, h)),
        scratch_shapes=[pltpu.VMEM((s, HD), F32), pltpu.VMEM((s, HD), F32), pltpu.VMEM((s, HD), F32),
                        pltpu.VMEM((ng, s, HD), F32), pltpu.VMEM((ng, s, HD), F32)],
        compiler_params=_params("parallel", "parallel"),
        name="dilated_attention",
    )(*([z3] * 9), *biases)


ROUTER_LANES = 128


def _router_kernel(x_ref, g_ref, w_ref, hp_ref, idx_ref, gate_ref, count_ref, carry_ref, *, capacity):
    tm = x_ref.shape[0]
    half = hp_ref.shape[1]

    @pl.when(pl.program_id(0) == 0)
    def _():
        carry_ref[...] = jnp.zeros_like(carry_ref)

    h = _rms_rows(x_ref[...], g_ref[...])
    hp_ref[...] = _bf16_bits(h[:, :half]) | (_bf16_bits(h[:, half:]) << 16)
    h_hi = h.astype(BF16)
    h_lo = (h - h_hi.astype(F32)).astype(BF16)
    logits = (jnp.dot(h_hi, w_ref[0], preferred_element_type=F32)
              + (jnp.dot(h_hi, w_ref[1], preferred_element_type=F32)
                 + jnp.dot(h_lo, w_ref[0], preferred_element_type=F32)))
    lane = lax.broadcasted_iota(jnp.int32, logits.shape, 1).astype(F32)
    logits = jnp.where(lane < N_EXPERTS, logits, -jnp.inf)
    m1 = jnp.max(logits, axis=-1, keepdims=True)
    i1 = jnp.min(jnp.where(logits == m1, lane, float(ROUTER_LANES)), axis=-1, keepdims=True)
    rest = jnp.where(lane == i1, -jnp.inf, logits)
    m2 = jnp.max(rest, axis=-1, keepdims=True)
    i2 = jnp.min(jnp.where(rest == m2, lane, float(ROUTER_LANES)), axis=-1, keepdims=True)
    e2 = jnp.exp(m2 - m1)
    den = 1.0 + e2
    gate_ref[...] = jnp.where(lane == 0, 1.0 / den, e2 / den)
    chosen = jnp.where((lane == i1) | (lane == i2), 1.0, 0.0)
    earlier = lax.broadcasted_iota(jnp.int32, (tm, tm), 0) > lax.broadcasted_iota(jnp.int32, (tm, tm), 1)
    carry = carry_ref[...]
    before = jnp.dot(jnp.where(earlier, 1.0, 0.0).astype(BF16), chosen.astype(BF16),
                     preferred_element_type=F32) + carry
    r1 = jnp.sum(jnp.where(lane == i1, before, 0.0), axis=-1, keepdims=True)
    r2 = jnp.sum(jnp.where(lane == i2, before, 0.0), axis=-1, keepdims=True)
    d1, d2 = i1 * capacity + r1, i2 * capacity + r2
    idx_ref[...] = jnp.where(lane == 0, i1, jnp.where(lane == 1, i2, jnp.where(lane == 2, d1, d2))).astype(jnp.int32)
    carry = carry + jnp.sum(chosen, axis=0, keepdims=True)
    carry_ref[...] = carry
    count_ref[...] = carry.astype(jnp.int32)


def _router(x2, gain, router):
    rows, d = x2.shape
    tm = MOE_CHUNK
    w = jnp.zeros((d, ROUTER_LANES), F32).at[:, :N_EXPERTS].set(router.astype(F32))
    w_hi = w.astype(BF16)
    w = jnp.stack([w_hi, (w - w_hi.astype(F32)).astype(BF16)])
    row_spec = lambda width: pl.BlockSpec((tm, width), lambda i: (i, 0))
    return pl.pallas_call(
        functools.partial(_router_kernel, capacity=float(rows)),
        out_shape=(jax.ShapeDtypeStruct((rows, d // 2), jnp.uint32),
                   jax.ShapeDtypeStruct((rows, ROUTER_LANES), jnp.int32),
                   jax.ShapeDtypeStruct((rows, ROUTER_LANES), F32),
                   jax.ShapeDtypeStruct((1, ROUTER_LANES), jnp.int32)),
        grid=(rows // tm,),
        in_specs=[row_spec(d),
                  pl.BlockSpec((1, d), lambda i: (0, 0)),
                  pl.BlockSpec((2, d, ROUTER_LANES), lambda i: (0, 0, 0))],
        out_specs=(row_spec(d // 2), row_spec(ROUTER_LANES), row_spec(ROUTER_LANES),
                   pl.BlockSpec((1, ROUTER_LANES), lambda i: (0, 0))),
        scratch_shapes=[pltpu.VMEM((1, ROUTER_LANES), F32)],
        compiler_params=_params("arbitrary"),
        name="router",
    )(x2, gain.reshape(1, d), w)


def _sc_workers():
    info = plsc.get_sparse_core_info()
    mesh = plsc.VectorSubcoreMesh(core_axis_name="core", subcore_axis_name="subcore")
    return mesh, info.num_cores, info.num_subcores


def _pack_weight_rows(w, col_tile):
    r, c = w.shape
    mesh, nc, ns = _sc_workers()
    rs, ncol = SC_PACK_ROWS, c // col_tile
    per_w = (r // rs) * ncol // (nc * ns)
    assert per_w * nc * ns * rs * col_tile == r * c and per_w % 2 == 0 and col_tile % SC_LANES == 0
    params = pltpu.CompilerParams()
    if "needs_layout_passes" in pltpu.CompilerParams.__dataclass_fields__:
        params = dataclasses.replace(params, needs_layout_passes=False)

    @functools.partial(
        pl.kernel, mesh=mesh, out_type=jax.ShapeDtypeStruct((r // 2, c), jnp.uint32), compiler_params=params,
        cost_estimate=pl.CostEstimate(flops=r * c, transcendentals=0, bytes_accessed=6 * r * c),
        scratch_types=[pltpu.VMEM((2, rs, col_tile), F32), pltpu.VMEM((2, rs // 2, col_tile), jnp.uint32),
                       pltpu.SemaphoreType.DMA((2,)), pltpu.SemaphoreType.DMA((2,))])
    def pack(w_hbm, o_hbm, in_v, out_v, rsem, wsem):
        wid = lax.axis_index("subcore") * nc + lax.axis_index("core")

        @pl.loop(0, per_w // 2)
        def _(it):
            reads, writes = [], []
            for b in range(2):
                tile = wid * per_w + 2 * it + b
                r0 = pl.multiple_of((tile // ncol) * rs, rs)
                c0 = pl.multiple_of((tile % ncol) * col_tile, col_tile)
                reads.append(pltpu.make_async_copy(w_hbm.at[pl.ds(r0, rs), pl.ds(c0, col_tile)], in_v.at[b],
                                                   rsem.at[b]))
                writes.append(pltpu.make_async_copy(
                    out_v.at[b], o_hbm.at[pl.ds(pl.multiple_of(r0 // 2, rs // 2), rs // 2), pl.ds(c0, col_tile)],
                    wsem.at[b]))
            reads[0].start()
            reads[1].start()
            for b in range(2):
                reads[b].wait()
                for pair in range(rs // 2):
                    @plsc.parallel_loop(0, col_tile, step=SC_LANES, unroll=8)
                    def _(j):
                        packed = plsc.pack(in_v[b, 2 * pair, pl.ds(j, SC_LANES)],
                                           in_v[b, 2 * pair + 1, pl.ds(j, SC_LANES)],
                                           format=plsc.PackFormat.INTERLEAVED)
                        out_v[b, pair, pl.ds(j, SC_LANES)] = plsc.bitcast(packed, jnp.uint32)
                writes[b].start()
            writes[0].wait()
            writes[1].wait()

    return pack(w)


def _sc_token_rows(t, nc, ns):
    per_w = t // (nc * ns)
    assert per_w * nc * ns == t and per_w % (2 * SC_ROWS) == 0
    return per_w, per_w // SC_ROWS


def _scatter_rows(table, dests, p_rows, after=()):
    t, w = table.shape
    nk = len(dests)
    mesh, nc, ns = _sc_workers()
    per_w, nit = _sc_token_rows(t, nc, ns)
    ch = SC_ROWS

    @functools.partial(
        pl.kernel, mesh=mesh, out_type=jax.ShapeDtypeStruct((p_rows, w), table.dtype),
        scratch_types=[pltpu.VMEM((nk, nit, ch), jnp.int32), pltpu.VMEM((2, ch, w), table.dtype),
                       pltpu.SemaphoreType.DMA((2,)), pltpu.SemaphoreType.DMA((2, nk))])
    def scatter(table_hbm, *refs):
        dest_hbm, out_hbm = refs[:nk], refs[nk + len(after)]
        idx_v, rows_v, rsem, wsem = refs[nk + len(after) + 1:]
        wid = lax.axis_index("subcore") * nc + lax.axis_index("core")
        for k in range(nk):
            pltpu.sync_copy(dest_hbm[k].at[pl.ds(wid * nit, nit)], idx_v.at[k])

        @pl.loop(0, nit // 2)
        def _(it):
            reads = [pltpu.make_async_copy(table_hbm.at[pl.ds(wid * per_w + (2 * it + b) * ch, ch)],
                                           rows_v.at[b], rsem.at[b]) for b in range(2)]
            writes = [[pltpu.make_async_copy(rows_v.at[b], out_hbm.at[idx_v.at[k].at[2 * it + b]], wsem.at[b, k])
                       for k in range(nk)] for b in range(2)]
            reads[0].start()
            reads[1].start()
            for b in range(2):
                reads[b].wait()
                for k in range(nk):
                    writes[b][k].start()
            for b in range(2):
                for k in range(nk):
                    writes[b][k].wait()

    return scatter(table, *[d.reshape(t // ch, ch) for d in dests], *after)


def _gather_rows(table, idxs):
    t = idxs[0].shape[0]
    w = table.shape[1]
    nk = len(idxs)
    mesh, nc, ns = _sc_workers()
    per_w, nit = _sc_token_rows(t, nc, ns)
    ch = SC_ROWS

    @functools.partial(
        pl.kernel, mesh=mesh, out_type=[jax.ShapeDtypeStruct((t, w), table.dtype)] * nk,
        scratch_types=[pltpu.VMEM((nk, nit, ch), jnp.int32), pltpu.VMEM((nk, 2, ch, w), table.dtype),
                       pltpu.SemaphoreType.DMA((nk, 2)), pltpu.SemaphoreType.DMA((nk, 2))])
    def gather(table_hbm, *refs):
        idx_hbm, out_hbm = refs[:nk], refs[nk:2 * nk]
        idx_v, rows_v, rsem, wsem = refs[2 * nk:]
        wid = lax.axis_index("subcore") * nc + lax.axis_index("core")
        for k in range(nk):
            pltpu.sync_copy(idx_hbm[k].at[pl.ds(wid * nit, nit)], idx_v.at[k])

        @pl.loop(0, nit // 2)
        def _(it):
            slots = [(k, b) for k in range(nk) for b in range(2)]
            reads = {(k, b): pltpu.make_async_copy(table_hbm.at[idx_v.at[k].at[2 * it + b]], rows_v.at[k, b],
                                                   rsem.at[k, b]) for k, b in slots}
            writes = {(k, b): pltpu.make_async_copy(rows_v.at[k, b],
                                                    out_hbm[k].at[pl.ds(wid * per_w + (2 * it + b) * ch, ch)],
                                                    wsem.at[k, b]) for k, b in slots}
            for s in slots:
                reads[s].start()
            for s in slots:
                reads[s].wait()
                writes[s].start()
            for s in slots:
                writes[s].wait()

    return gather(table, *[i.reshape(t // ch, ch) for i in idxs])


def _combine_kernel(x_ref, y0_ref, y1_ref, gate_ref, *rest):
    o_ref = rest[-1]
    half = x_ref.shape[1] // 2
    for part in range(2):
        cols = slice(part * half, (part + 1) * half)
        acc = x_ref[:, cols]
        for k, y_ref in enumerate((y0_ref, y1_ref)):
            word = y_ref[...]
            bits = (word << 16) if part == 0 else (word & jnp.uint32(0xFFFF0000))
            acc = acc + gate_ref[:, k:k + 1] * lax.bitcast_convert_type(bits, F32)
        o_ref[:, cols] = acc


def _combine(x2, y0, y1, gates, part, earlier=None):
    t, d = x2.shape
    tm = ROW_TILE
    steps = y0.shape[0] // tm
    here = lambda i: (part * steps + i, 0)
    in_specs = [pl.BlockSpec((tm, d), here),
                pl.BlockSpec((tm, d // 2), lambda i: (i, 0)),
                pl.BlockSpec((tm, d // 2), lambda i: (i, 0)),
                pl.BlockSpec((tm, TOP_K), here)]
    operands = [x2, y0, y1, gates]
    if earlier is not None:
        in_specs.append(pl.BlockSpec(memory_space=pl.ANY))
        operands.append(earlier)
    return pl.pallas_call(
        _combine_kernel,
        out_shape=jax.ShapeDtypeStruct((t, d), F32),
        grid=(steps,),
        in_specs=in_specs,
        out_specs=pl.BlockSpec((tm, d), here),
        input_output_aliases={} if earlier is None else {len(operands) - 1: 0},
        compiler_params=_params("parallel"),
        name="moe_combine",
    )(*operands)


def _moe(x2, gain, router, wg, wu, wd):
    t, d = x2.shape
    hp, idx, gate, count = _router(x2, gain, router)
    dests = [idx[:, TOP_K + k] for k in range(TOP_K)]
    blocks_per_expert = t // MOE_ROWS
    spare = N_EXPERTS * blocks_per_expert
    p_rows = (spare + 1) * MOE_ROWS
    counts = count[0, :N_EXPERTS]
    occupied = (counts + MOE_ROWS - 1) // MOE_ROWS
    ends = jnp.cumsum(occupied)
    step = jnp.arange(t * TOP_K // MOE_ROWS + N_EXPERTS, dtype=jnp.int32)
    blk_expert = jnp.minimum(jnp.sum(ends[None, :] <= step[:, None], axis=1), N_EXPERTS - 1).astype(jnp.int32)
    blk_index = step - (ends - occupied)[blk_expert]
    used = step < ends[-1]
    valid = jnp.where(used, jnp.clip(counts[blk_expert] - blk_index * MOE_ROWS, 0, MOE_ROWS), 0).astype(jnp.int32)
    row_block = jnp.where(used, blk_expert * blocks_per_expert + blk_index, spare).astype(jnp.int32)
    n_exp, _, f = wg.shape
    pack = lambda w, col_tile: _pack_weight_rows(w.reshape(-1, w.shape[2]), col_tile).reshape(n_exp, -1, w.shape[2])
    wg, wu, wd = pack(wg, f // 2), pack(wu, f // 2), pack(wd, d)
    hb = _scatter_rows(hp, dests, p_rows, after=(wg, wu, wd))
    yb = _ffn_experts(hb, wg, wu, wd, blk_expert, valid, row_block)
    out = None
    for part in range(MOE_TAIL_PARTS):
        rows = slice(part * t // MOE_TAIL_PARTS, (part + 1) * t // MOE_TAIL_PARTS)
        y0, y1 = _gather_rows(yb, [dk[rows] for dk in dests])
        out = _combine(x2, y0, y1, gate[:, :TOP_K], part, out)
    return out


def kernel(x, mem, rel_bias, norm_mix, norm_mem, norm_ffn, w_mem_kv, xq_norm, xk_norm, w_out, hy_w_in, hy_conv_w, hy_conv_b, hy_filt_w1, hy_filt_b1, hy_filt_w2, hy_filt_b2, hy_filt_w3, hy_sin_freq, hy_skip, at_w_in, at_q_norm, at_k_norm, ffn_w_gate, ffn_w_up, ffn_w_down, moe_router, moe_w_gate, moe_w_up, moe_w_down):
    b, s, d = x.shape
    t = b * s
    m_len = mem.shape[1]
    x2 = x.reshape(t, d)
    mem2 = mem.reshape(b * m_len, d)
    bf = lambda w: w.astype(BF16)
    score_scale = HD ** -0.5
    xa_heads = XA_W // HD
    at_heads = AT_W // HD
    plain = lambda width: [None] * (width // HD)
    kv_gains = lambda i: [xk_norm[i]] * xa_heads + plain(XA_W)

    tables = _dft_tables(s)
    spectra, hq = _hyena_filters(s, tables, hy_filt_w1[0], hy_filt_b1[0], hy_filt_w2[0], hy_filt_b2[0],
                                 hy_filt_w3[0], hy_sin_freq[0])
    z = _norm_matmul(x2, norm_mix[0], bf(hy_w_in[0]), 1024,
                     plain(3 * HY_C) + [xq_norm[0] * score_scale] * xa_heads).reshape(b, s, -1)
    kv = _norm_matmul(mem2, norm_mem[0], bf(w_mem_kv[0]), 1024, kv_gains(0)).reshape(b, m_len, -1)
    self_out = _hyena(z, hy_conv_w[0], hy_conv_b[0], tables, spectra, hq, hy_skip[0])
    x2 = _mix_out(x2.reshape(b, s, d), self_out, z, 3 * HY_C // XA_W, kv, bf(w_out[0])).reshape(t, d)
    x2 = _ffn_dense(x2, norm_ffn[0], bf(ffn_w_gate[0]), bf(ffn_w_up[0]), bf(ffn_w_down[0]))

    at_gains = ([at_q_norm[0] * score_scale] * at_heads + [at_k_norm[0]] * at_heads + plain(AT_W)
                + [xq_norm[1] * score_scale] * xa_heads)
    z = _norm_matmul(x2, norm_mix[1], bf(at_w_in[0]), 1024, at_gains).reshape(b, s, -1)
    kv = _norm_matmul(mem2, norm_mem[1], bf(w_mem_kv[1]), 1024, kv_gains(1)).reshape(b, m_len, -1)
    self_out = _dilated_attention(z, rel_bias)
    x2 = _mix_out(x2.reshape(b, s, d), self_out, z, 3 * AT_W // XA_W, kv, bf(w_out[1])).reshape(t, d)
    x2 = _moe(x2, norm_ffn[1], moe_router[0], moe_w_gate[0], moe_w_up[0], moe_w_down[0])
    return x2.reshape(b, s, d)
```

```python
import dataclasses
import functools
import math

import jax
import jax.numpy as jnp
import numpy as np
from jax import lax
from jax.experimental import pallas as pl
from jax.experimental.pallas import tpu as pltpu
from jax.experimental.pallas import tpu_sc as plsc

F32 = jnp.float32
BF16 = jnp.bfloat16

D_MODEL = 1024
EPS = 1e-6
HY_C = 512
FILT_BANDS = 16
DECAY_TARGET = 1e-2
FAST_DECAY_PCT = 0.3
SLOW_DECAY_PCT = 1.5
MOD_SHIFT = 0.05
AT_GROUPS = ((128, 1), (512, 4), (2048, 16))
AT_HEADS = 4
HD = 128
AT_W = 1536
NUM_BUCKETS = 32
REL_MAX_DIST = 1024
NEG_INF = -1e30
XA_W = 512
D_FF = 2816
N_EXPERTS = 8
TOP_K = 2

VMEM_LIMIT_BYTES = 56 * 1024 * 1024
ROW_TILE = 1024
PROJ_ROW_TILE = 1024
FF_TILE = 256
MOE_ROWS = 512
MOE_CHUNK = 512
MOE_TAIL_PARTS = 2
SC_ROWS = 32
SC_LANES = 16
SC_PACK_ROWS = 16
ATT_TQ = 128
ATT_KW = 256
HY_TC = 256
HY_KT = 512


def _params(*sem):
    return pltpu.CompilerParams(dimension_semantics=sem, vmem_limit_bytes=VMEM_LIMIT_BYTES)


def _rms_rows(x, gain):
    return x * lax.rsqrt(jnp.mean(x * x, axis=-1, keepdims=True) + EPS) * gain


def _bf16_bits(v):
    u = lax.bitcast_convert_type(v, jnp.uint32)
    return (u + jnp.uint32(0x7FFF) + ((u >> 16) & jnp.uint32(1))) >> 16


def _resident(shape, index_map):
    return pl.BlockSpec(shape, index_map, pipeline_mode=pl.Buffered(1))


def _norm_matmul_kernel(x_ref, g_ref, w_ref, hg_ref, o_ref, *, tn, head_norm):
    h = _rms_rows(x_ref[...], g_ref[...]).astype(BF16)
    for c in range(o_ref.shape[1] // tn):
        acc = jnp.dot(h, w_ref[:, c * tn:(c + 1) * tn], preferred_element_type=F32)
        for j in range(tn // HD):
            cols = slice(c * tn + j * HD, c * tn + (j + 1) * HD)
            seg = acc[:, j * HD:(j + 1) * HD]
            if head_norm[cols.start // HD]:
                seg = _rms_rows(seg, hg_ref[:, cols])
            o_ref[:, cols] = seg.astype(o_ref.dtype)


def _norm_matmul(x2, gain, w_bf, tn, head_gains):
    rows, d = x2.shape
    n = w_bf.shape[1]
    tm = min(PROJ_ROW_TILE, rows)
    assert len(head_gains) * HD == n
    hg = jnp.concatenate([jnp.ones((HD,), F32) if g is None else g.astype(F32) for g in head_gains]).reshape(1, n)
    return pl.pallas_call(
        functools.partial(_norm_matmul_kernel, tn=tn, head_norm=tuple(g is not None for g in head_gains)),
        out_shape=jax.ShapeDtypeStruct((rows, n), BF16),
        grid=(rows // tm,),
        in_specs=[pl.BlockSpec((tm, d), lambda i: (i, 0)),
                  pl.BlockSpec((1, d), lambda i: (0, 0)),
                  _resident((d, n), lambda i: (0, 0)),
                  pl.BlockSpec((1, n), lambda i: (0, 0))],
        out_specs=pl.BlockSpec((tm, n), lambda i: (i, 0)),
        compiler_params=_params("parallel"),
        cost_estimate=pl.CostEstimate(flops=2 * rows * d * n, transcendentals=rows,
                                      bytes_accessed=4 * rows * d + 2 * d * n + 2 * rows * n),
        name="norm_matmul",
    )(x2, gain.reshape(1, d), w_bf, hg)


def _mix_out_kernel(x_ref, a_ref, xq_ref, kv_ref, wa_ref, wc_ref, o_ref):
    heads = []
    for h in range(XA_W // HD):
        cols = slice(h * HD, (h + 1) * HD)
        s = lax.dot_general(xq_ref[0, :, cols], kv_ref[0, :, cols], (((1,), (1,)), ((), ())),
                            preferred_element_type=F32)
        p = jnp.exp(s - jnp.max(s, axis=-1, keepdims=True))
        z = jnp.sum(p, axis=-1, keepdims=True)
        pv = jnp.dot(p.astype(BF16), kv_ref[0, :, XA_W + h * HD:XA_W + (h + 1) * HD], preferred_element_type=F32)
        heads.append((pv / z).astype(BF16))
    cross = jnp.concatenate(heads, axis=1)
    o_ref[0] = (x_ref[0]
                + jnp.dot(a_ref[0], wa_ref[...], preferred_element_type=F32)
                + jnp.dot(cross, wc_ref[...], preferred_element_type=F32))


def _mix_out(x3, self_out, z3, xq_block, kv3, w_bf):
    b, s, d = x3.shape
    half = self_out.shape[2]
    m = kv3.shape[1]
    tm = ROW_TILE
    return pl.pallas_call(
        _mix_out_kernel,
        out_shape=jax.ShapeDtypeStruct((b, s, d), F32),
        grid=(b, s // tm),
        in_specs=[pl.BlockSpec((1, tm, d), lambda i, j: (i, j, 0)),
                  pl.BlockSpec((1, tm, half), lambda i, j: (i, j, 0)),
                  pl.BlockSpec((1, tm, XA_W), lambda i, j: (i, j, xq_block)),
                  pl.BlockSpec((1, m, 2 * XA_W), lambda i, j: (i, 0, 0)),
                  pl.BlockSpec((half, d), lambda i, j: (0, 0)),
                  pl.BlockSpec((d - half, d), lambda i, j: (1, 0))],
        out_specs=pl.BlockSpec((1, tm, d), lambda i, j: (i, j, 0)),
        compiler_params=_params("parallel", "parallel"),
        name="mix_out",
    )(x3, self_out, z3, kv3, w_bf, w_bf)


def _weight_rows(ref, start, size):
    if ref.dtype == jnp.uint32:
        return lambda cols: pltpu.bitcast(ref[start // 2:(start + size) // 2, cols], BF16)
    return lambda cols: ref[start:start + size, cols]


def _weight_shape(ref):
    return (ref.shape[0] * (2 if ref.dtype == jnp.uint32 else 1), ref.shape[1])


def _swiglu(h, wg_ref, wu_ref, wd_ref, tf):
    d, f = _weight_shape(wg_ref)
    assert f % tf == 0
    y = None
    for j in range(f // tf):
        cols = slice(j * tf, (j + 1) * tf)
        gg = jnp.dot(h, _weight_rows(wg_ref, 0, d)(cols), preferred_element_type=F32)
        uu = jnp.dot(h, _weight_rows(wu_ref, 0, d)(cols), preferred_element_type=F32)
        a = ((gg * jax.nn.sigmoid(gg)) * uu).astype(BF16)
        part = jnp.dot(a, _weight_rows(wd_ref, j * tf, tf)(slice(None)), preferred_element_type=F32)
        y = part if y is None else y + part
    return y


def _ffn_dense_kernel(x_ref, g_ref, wg_ref, wu_ref, wd_ref, o_ref, *, tf):
    x = x_ref[...]
    h = _rms_rows(x, g_ref[...]).astype(BF16)
    o_ref[...] = x + _swiglu(h, wg_ref, wu_ref, wd_ref, tf)


def _ffn_dense(x2, gain, wg, wu, wd):
    rows, d = x2.shape
    f = wg.shape[1]
    tm = ROW_TILE
    return pl.pallas_call(
        functools.partial(_ffn_dense_kernel, tf=FF_TILE),
        out_shape=jax.ShapeDtypeStruct((rows, d), F32),
        grid=(rows // tm,),
        in_specs=[pl.BlockSpec((tm, d), lambda i: (i, 0)),
                  pl.BlockSpec((1, d), lambda i: (0, 0)),
                  _resident((d, f), lambda i: (0, 0)),
                  _resident((d, f), lambda i: (0, 0)),
                  _resident((f, d), lambda i: (0, 0))],
        out_specs=pl.BlockSpec((tm, d), lambda i: (i, 0)),
        compiler_params=_params("parallel"),
        cost_estimate=pl.CostEstimate(flops=6 * rows * d * f, transcendentals=rows * f,
                                      bytes_accessed=8 * rows * d + 6 * d * f),
        name="ffn_dense",
    )(x2, gain.reshape(1, d), wg, wu, wd)


def _ffn_expert_kernel(eid_ref, valid_ref, hp_ref, wg_ref, wu_ref, wd_ref, o_ref, h_ref, *, tf):
    half = hp_ref.shape[1]
    valid = valid_ref[pl.program_id(0)]

    @pl.when(valid > 0)
    def _():
        keep = lax.broadcasted_iota(jnp.int32, (hp_ref.shape[0], 1), 0) < valid
        word = hp_ref[...]
        h_ref[:, :half] = jnp.where(keep, lax.bitcast_convert_type(word << 16, F32), 0.0).astype(BF16)
        h_ref[:, half:] = jnp.where(keep, lax.bitcast_convert_type(word & jnp.uint32(0xFFFF0000), F32),
                                    0.0).astype(BF16)
        y = _swiglu(h_ref[...], wg_ref.at[0], wu_ref.at[0], wd_ref.at[0], tf)
        o_ref[...] = _bf16_bits(y[:, :half]) | (_bf16_bits(y[:, half:]) << 16)

    @pl.when(valid <= 0)
    def _():
        o_ref[...] = jnp.zeros_like(o_ref)


def _ffn_experts(hp, wg, wu, wd, eid, valid):
    rows, half = hp.shape
    d = 2 * half
    tm = MOE_ROWS
    expert_spec = lambda w: pl.BlockSpec((1,) + w.shape[1:], lambda i, e, n: (e[i], 0, 0))
    grid_spec = pltpu.PrefetchScalarGridSpec(
        num_scalar_prefetch=2,
        grid=(rows // tm,),
        in_specs=[pl.BlockSpec((tm, half), lambda i, e, n: (i, 0)),
                  expert_spec(wg), expert_spec(wu), expert_spec(wd)],
        out_specs=pl.BlockSpec((tm, half), lambda i, e, n: (i, 0)),
        scratch_shapes=[pltpu.VMEM((tm, d), BF16)],
    )
    return pl.pallas_call(
        functools.partial(_ffn_expert_kernel, tf=FF_TILE),
        out_shape=jax.ShapeDtypeStruct((rows, half), jnp.uint32),
        grid_spec=grid_spec,
        compiler_params=_params("arbitrary"),
        name="ffn_experts",
    )(eid, valid, hp, wg, wu, wd)


def _dft_kernel(ce_ref, se_ref, co_ref, so_ref, cot_ref, sot_ref, base_ref, *, n_fft):
    rows, cols = ce_ref.shape
    i = pl.program_id(0)
    theta = 2.0 * math.pi / n_fft

    def phases(r, c):
        return r * (2 * c), r * (2 * c + 1), c * (2 * r + 1)

    @pl.when(i == 0)
    def _():
        r = lax.broadcasted_iota(jnp.int32, (rows, cols), 0)
        c = lax.broadcasted_iota(jnp.int32, (rows, cols), 1)
        for f, ph in enumerate(phases(r, c)):
            ang = (ph & (n_fft - 1)).astype(F32) * theta
            base_ref[2 * f] = jnp.cos(ang)
            base_ref[2 * f + 1] = jnp.sin(ang)

    c = lax.broadcasted_iota(jnp.int32, (8, cols), 1)
    r0 = i * rows
    shifts = (r0 * (2 * c), r0 * (2 * c + 1), c * (2 * r0))
    for f, (c_ref, s_ref) in enumerate(((ce_ref, se_ref), (co_ref, so_ref), (cot_ref, sot_ref))):
        ang = (shifts[f] & (n_fft - 1)).astype(F32) * theta
        ca, sa = jnp.cos(ang)[0:1], jnp.sin(ang)[0:1]
        cb, sb = base_ref[2 * f], base_ref[2 * f + 1]
        c_ref[...] = (cb * ca - sb * sa).astype(BF16)
        s_ref[...] = (sb * ca + cb * sa).astype(BF16)


def _dft_tables(length):
    half, rows = length // 2, 128
    shape = jax.ShapeDtypeStruct((half, half), BF16)
    spec = pl.BlockSpec((rows, half), lambda i: (i, 0))
    return pl.pallas_call(
        functools.partial(_dft_kernel, n_fft=2 * length),
        out_shape=(shape,) * 6,
        grid=(half // rows,),
        out_specs=(spec,) * 6,
        scratch_shapes=[pltpu.VMEM((6, rows, half), F32)],
        compiler_params=_params("arbitrary"),
        name="dft_tables",
    )()


LANES = 128


def _split_parity(tmp_ref, x):
    half = x.shape[0] // 2
    for j in range(tmp_ref.shape[0]):
        tmp_ref[j] = x[:, j * LANES:(j + 1) * LANES]
    pick = lambda start: jnp.concatenate(
        [tmp_ref[j, pl.ds(start, half, stride=2), :] for j in range(tmp_ref.shape[0])], axis=1)
    return pick(0), pick(1)


def _merge_parity(tmp_ref, even, odd):
    half = even.shape[0]
    for j in range(tmp_ref.shape[0]):
        tmp_ref[j, pl.ds(0, half, stride=2), :] = even[:, j * LANES:(j + 1) * LANES]
        tmp_ref[j, pl.ds(1, half, stride=2), :] = odd[:, j * LANES:(j + 1) * LANES]
    return jnp.concatenate([tmp_ref[j] for j in range(tmp_ref.shape[0])], axis=1)


def _filter_time_kernel(feats_ref, w1_ref, b1_ref, w2_ref, b2_ref, fr_ref, w3_ref, t_ref, delta_ref,
                        g_ref, hq_ref, tmp_ref):
    hp = lax.Precision.HIGHEST
    length = feats_ref.shape[0]
    half = length // 2
    fr = fr_ref[...]
    h = jnp.sin(fr * (jnp.dot(feats_ref[...], w1_ref[...], preferred_element_type=F32, precision=hp) + b1_ref[...]))
    h = jnp.sin(fr * (jnp.dot(h, w2_ref[...], preferred_element_type=F32, precision=hp) + b2_ref[...]))
    mod = jnp.exp(-t_ref[...] * delta_ref[...]) + MOD_SHIFT
    row = lax.broadcasted_iota(jnp.int32, (length, 1), 0)
    alt = jnp.where(lax.broadcasted_iota(jnp.int32, (half, 1), 0) % 2 == 0, 1.0, -1.0).astype(F32)
    for o in range(2):
        fwd = jnp.dot(h, w3_ref[2 * o], preferred_element_type=F32, precision=hp) * mod
        bwd = jnp.dot(h, w3_ref[2 * o + 1], preferred_element_type=F32, precision=hp) * mod
        bwd = jnp.where(row == 0, 0.0, bwd)
        norm = (jnp.sum(jnp.abs(fwd), axis=0, keepdims=True)
                + jnp.sum(jnp.abs(bwd), axis=0, keepdims=True) + 1e-6)
        fwd = fwd / norm
        bwd = bwd / norm
        for part, g in enumerate((fwd + bwd, fwd - bwd)):
            even, odd = _split_parity(tmp_ref, g)
            g_ref[o, 2 * part] = even.astype(BF16)
            g_ref[o, 2 * part + 1] = odd.astype(BF16)
            mid = jnp.sum((even if part == 0 else odd) * alt, axis=0, keepdims=True) * (1.0 / length)
            hq_ref[o, part] = mid if part == 0 else -mid


def _filter_freq_kernel(ce_ref, se_ref, co_ref, so_ref, g_ref, h_ref):
    half = ce_ref.shape[0]
    row = lax.broadcasted_iota(jnp.int32, (half, 1), 0)
    scale = jnp.where(row == 0, 1.0, 2.0).astype(F32) * (1.0 / (4 * half))
    a = jnp.dot(ce_ref[...], g_ref[0, 0], preferred_element_type=F32)
    b = jnp.dot(co_ref[...], g_ref[0, 1], preferred_element_type=F32)
    c = jnp.dot(se_ref[...], g_ref[0, 2], preferred_element_type=F32)
    d = jnp.dot(so_ref[...], g_ref[0, 3], preferred_element_type=F32)
    h_ref[0, 0] = (a + b) * scale
    h_ref[0, 1] = -(c + d) * scale
    h_ref[0, 2] = (a - b) * scale
    h_ref[0, 3] = (c - d) * scale


def _hyena_filters(length, tables, w1, b1, w2, b2, w3, freq):
    t = jnp.linspace(0.0, 1.0, length, dtype=F32)[:, None]
    f = jnp.linspace(1e-4, FILT_BANDS - 1, FILT_BANDS, dtype=F32)[None]
    ang = (2.0 * math.pi / length) * jnp.arange(length, dtype=F32)[:, None] * f
    feats = jnp.concatenate([t, jnp.cos(ang), -jnp.sin(ang)], axis=-1)
    deltas = jnp.abs(jnp.linspace(math.log(DECAY_TARGET) / SLOW_DECAY_PCT,
                                  math.log(DECAY_TARGET) / FAST_DECAY_PCT, HY_C, dtype=F32))[None]
    hid = w1.shape[1]
    w3r = w3.reshape(hid, 4, HY_C).transpose(1, 0, 2)
    tc = HY_TC
    full = lambda shape: pl.BlockSpec(shape, lambda c: (0,) * len(shape))
    half = length // 2
    g, hq = pl.pallas_call(
        _filter_time_kernel,
        out_shape=(jax.ShapeDtypeStruct((2, 4, half, HY_C), BF16),
                   jax.ShapeDtypeStruct((2, 2, 1, HY_C), F32)),
        grid=(HY_C // tc,),
        in_specs=[full(feats.shape), full(w1.shape), full((1, hid)), full(w2.shape), full((1, hid)),
                  full((1, hid)), pl.BlockSpec((4, hid, tc), lambda c: (0, 0, c)), full((length, 1)),
                  pl.BlockSpec((1, tc), lambda c: (0, c))],
        out_specs=(pl.BlockSpec((2, 4, half, tc), lambda c: (0, 0, 0, c)),
                   pl.BlockSpec((2, 2, 1, tc), lambda c: (0, 0, 0, c))),
        scratch_shapes=[pltpu.VMEM((tc // LANES, length, LANES), F32)],
        compiler_params=_params("parallel"),
        name="filter_time",
    )(feats, w1, b1.reshape(1, hid), w2, b2.reshape(1, hid), freq.reshape(1, hid), w3r, t, deltas)
    ce, se, co, so = tables[:4]
    table_spec = _resident((half, half), lambda o, c: (0, 0))
    spectra = pl.pallas_call(
        _filter_freq_kernel,
        out_shape=jax.ShapeDtypeStruct((2, 4, half, HY_C), F32),
        grid=(2, HY_C // tc),
        in_specs=[table_spec] * 4 + [pl.BlockSpec((1, 4, half, tc), lambda o, c: (o, 0, 0, c))],
        out_specs=pl.BlockSpec((1, 4, half, tc), lambda o, c: (o, 0, 0, c)),
        compiler_params=_params("parallel", "parallel"),
        name="filter_freq",
    )(ce, se, co, so, g)
    return spectra, hq


def _hyena_kernel(z0_ref, z1_ref, z2_ref, cw_ref, cb_ref, ce_ref, se_ref, co_ref, so_ref, cot_ref, sot_ref,
                  h_ref, hq_ref, skip_ref, o_ref, tmp_ref):
    length = z0_ref.shape[1]
    half = length // 2
    row = lax.broadcasted_iota(jnp.int32, (length, 1), 0)
    alt = jnp.where(lax.broadcasted_iota(jnp.int32, (half, 1), 0) % 2 == 0, 1.0, -1.0).astype(F32)

    def short_conv(z_ref, c):
        u = z_ref[0].astype(F32)
        prev = jnp.where(row == 0, 0.0, pltpu.roll(u, 1, axis=0))
        nxt = jnp.where(row == length - 1, 0.0, pltpu.roll(u, length - 1, axis=0))
        return cb_ref[c] + (prev * cw_ref[0, c] + u * cw_ref[1, c] + nxt * cw_ref[2, c])

    def long_conv(u, o):
        ue, uo = _split_parity(tmp_ref, u)
        ue_bf, uo_bf = ue.astype(BF16), uo.astype(BF16)
        a4 = jnp.sum(ue * alt, axis=0, keepdims=True)
        b4 = jnp.sum(uo * alt, axis=0, keepdims=True)
        hr4, hi4 = hq_ref[o, 0], hq_ref[o, 1]
        y_even = alt * (a4 * hr4 + b4 * hi4)
        y_odd = -(alt * (a4 * hi4 - b4 * hr4))
        for kt in range(half // HY_KT):
            ks = slice(kt * HY_KT, (kt + 1) * HY_KT)
            ae = jnp.dot(ce_ref[ks, :], ue_bf, preferred_element_type=F32)
            ao = jnp.dot(co_ref[ks, :], uo_bf, preferred_element_type=F32)
            be = jnp.dot(se_ref[ks, :], ue_bf, preferred_element_type=F32)
            bo = jnp.dot(so_ref[ks, :], uo_bf, preferred_element_type=F32)
            a_f, a_g, b_f, b_g = ae + ao, ae - ao, be + bo, bo - be
            hr_f, hi_f, hr_g, hi_g = (h_ref[o, j, ks, :] for j in range(4))
            re_f, im_f = a_f * hr_f + b_f * hi_f, a_f * hi_f - b_f * hr_f
            re_g, im_g = a_g * hr_g + b_g * hi_g, a_g * hi_g - b_g * hr_g
            y_even += (jnp.dot(ce_ref[:, ks], (re_f + re_g).astype(BF16), preferred_element_type=F32)
                       - jnp.dot(se_ref[:, ks], (im_f - im_g).astype(BF16), preferred_element_type=F32))
            y_odd += (jnp.dot(cot_ref[:, ks], (re_f - re_g).astype(BF16), preferred_element_type=F32)
                      - jnp.dot(sot_ref[:, ks], (im_f + im_g).astype(BF16), preferred_element_type=F32))
        return _merge_parity(tmp_ref, y_even, y_odd) + u * skip_ref[o]

    z = short_conv(z1_ref, 1) * long_conv(short_conv(z0_ref, 0), 0)
    o_ref[0] = (short_conv(z2_ref, 2) * long_conv(z, 1)).astype(o_ref.dtype)


def _hyena(z3, conv_w, conv_b, tables, spectra, hq, skip):
    b, s, _ = z3.shape
    half = s // 2
    tc = HY_TC
    nct = HY_C // tc
    cw = conv_w.reshape(3, 3, 1, HY_C)
    cb = conv_b.reshape(3, 1, HY_C)
    zspec = lambda chunk: pl.BlockSpec((1, s, tc), lambda c, i: (i, 0, chunk * nct + c))
    return pl.pallas_call(
        _hyena_kernel,
        out_shape=jax.ShapeDtypeStruct((b, s, HY_C), BF16),
        grid=(nct, b),
        in_specs=[zspec(0), zspec(1), zspec(2),
                  pl.BlockSpec((3, 3, 1, tc), lambda c, i: (0, 0, 0, c)),
                  pl.BlockSpec((3, 1, tc), lambda c, i: (0, 0, c))]
                 + [_resident((half, half), lambda c, i: (0, 0))] * 6
                 + [_resident((2, 4, half, tc), lambda c, i: (0, 0, 0, c)),
                    pl.BlockSpec((2, 2, 1, tc), lambda c, i: (0, 0, 0, c)),
                    pl.BlockSpec((2, 1, tc), lambda c, i: (0, 0, c))],
        out_specs=pl.BlockSpec((1, s, tc), lambda c, i: (i, 0, c)),
        scratch_shapes=[pltpu.VMEM((tc // LANES, s, LANES), F32)],
        compiler_params=_params("parallel", "parallel"),
        cost_estimate=pl.CostEstimate(flops=2 * 4 * 2 * b * s * half * HY_C, transcendentals=0,
                                      bytes_accessed=2 * 3 * b * s * HY_C + 4 * b * s * HY_C + 12 * half * half),
        name="hyena",
    )(z3, z3, z3, cw, cb, *tables, spectra, hq, skip.reshape(2, 1, HY_C))


def _rel_bucket(rel):
    half = NUM_BUCKETS // 2
    exact = half // 2
    n = np.abs(rel)
    large = exact + (np.log(np.maximum(n, 1) / exact) / np.log(REL_MAX_DIST / exact) * (half - exact)).astype(np.int32)
    large = np.minimum(large, half - 1)
    return (np.where(rel > 0, half, 0) + np.where(n < exact, n, large)).astype(np.int32)


def _att_tiles(length):
    kw = min(ATT_KW, length)
    tiles = []
    for qs in range(0, length, ATT_TQ):
        ks = min(max(qs - (kw - ATT_TQ) // 2, 0), length - kw)
        tiles.append((qs, ks, {0: 0, -64: 1, -128: 2}[ks - qs]))
    return kw, tiles


def _att_bias(rel_bias, group, seq):
    window, dil = AT_GROUPS[group]
    band = window // (2 * dil)
    kw, tiles = _att_tiles(seq // dil)
    offsets = sorted({ks - qs for qs, ks, _ in tiles}, reverse=True)
    table = rel_bias[:, group * AT_HEADS:(group + 1) * AT_HEADS].astype(F32)
    span = kw + ATT_TQ
    k = np.arange(span)
    rel = np.asarray(offsets)[:, None] + np.where(k < kw, k, k - span)[None, :]
    diag = jnp.where((np.abs(rel) <= band)[:, None, :],
                     jnp.swapaxes(table[_rel_bucket(rel * dil)], 1, 2), NEG_INF)
    return diag[:, :, None, :]


def _bias_tile(diag_ref, var, kw):
    rows = jnp.broadcast_to(diag_ref[var, 0], (ATT_TQ, diag_ref.shape[3]))
    return pltpu.roll(rows, 0, 1, stride=1, stride_axis=0)[:, :kw]


def _dilated_kernel(q1, q2, q3, k1, k2, k3, v1, v2, v3, b1, b2, b3, o_ref,
                    qn_ref, kn_ref, vn_ref, og_ref, lg_ref):
    seq = o_ref.shape[1]
    contract_last = (((1,), (1,)), ((), ()))
    for g, (q_ref, k_ref, v_ref, bias_ref) in enumerate(((q1, k1, v1, b1), (q2, k2, v2, b2), (q3, k3, v3, b3))):
        dil = AT_GROUPS[g][1]
        kw, tiles = _att_tiles(seq // dil)
        if dil > 1:
            qn_ref[...] = q_ref[0].astype(F32)
            kn_ref[...] = k_ref[0].astype(F32)
            vn_ref[...] = v_ref[0].astype(F32)
        bias = {var: _bias_tile(bias_ref, var, kw) for var in sorted({var for _, _, var in tiles})}
        for r in range(dil):
            for qs, ks, var in tiles:
                rows = lambda start, size: (pl.ds(r + start * dil, size, stride=dil) if dil > 1
                                            else pl.ds(start, size))
                if dil > 1:
                    qt = qn_ref[rows(qs, ATT_TQ), :].astype(BF16)
                    kt = kn_ref[rows(ks, kw), :].astype(BF16)
                    vt = vn_ref[rows(ks, kw), :].astype(BF16)
                else:
                    qt, kt, vt = q_ref[0, rows(qs, ATT_TQ), :], k_ref[0, rows(ks, kw), :], v_ref[0, rows(ks, kw), :]
                s = lax.dot_general(qt, kt, contract_last, preferred_element_type=F32) + bias[var]
                m = jnp.max(s, axis=-1, keepdims=True)
                p = jnp.exp(s - m)
                z = jnp.sum(p, axis=-1, keepdims=True)
                og_ref[g, rows(qs, ATT_TQ), :] = jnp.dot(p.astype(BF16), vt, preferred_element_type=F32) / z
                lg_ref[g, rows(qs, ATT_TQ), :] = jnp.broadcast_to(m + jnp.log(z), (ATT_TQ, HD))
    l0, l1, l2 = lg_ref[0], lg_ref[1], lg_ref[2]
    mx = jnp.maximum(jnp.maximum(l0, l1), l2)
    e0, e1, e2 = jnp.exp(l0 - mx), jnp.exp(l1 - mx), jnp.exp(l2 - mx)
    o_ref[0] = ((e0 * og_ref[0] + e1 * og_ref[1] + e2 * og_ref[2]) / (e0 + e1 + e2)).astype(o_ref.dtype)


def _dilated_attention(z3, rel_bias):
    b, s, _ = z3.shape
    ng = len(AT_GROUPS)
    col = lambda part, g: pl.BlockSpec((1, s, HD), lambda i, h: (i, 0, part * ng * AT_HEADS + g * AT_HEADS + h))
    biases = [_att_bias(rel_bias, g, s) for g in range(ng)]
    bias_spec = lambda a: pl.BlockSpec((a.shape[0], 1) + a.shape[2:], lambda i, h: (0, h, 0, 0))
    return pl.pallas_call(
        _dilated_kernel,
        out_shape=jax.ShapeDtypeStruct((b, s, AT_HEADS * HD), BF16),
        grid=(b, AT_HEADS),
        in_specs=[col(p, g) for p in range(3) for g in range(ng)] + [bias_spec(a) for a in biases],
        out_specs=pl.BlockSpec((1, s, HD), lambda i, h: (i, 0, h)),
        scratch_shapes=[pltpu.VMEM((s, HD), F32), pltpu.VMEM((s, HD), F32), pltpu.VMEM((s, HD), F32),
                        pltpu.VMEM((ng, s, HD), F32), pltpu.VMEM((ng, s, HD), F32)],
        compiler_params=_params("parallel", "parallel"),
        name="dilated_attention",
    )(*([z3] * 9), *biases)


ROUTER_LANES = 128


def _router_kernel(x_ref, g_ref, w_ref, hp_ref, idx_ref, gate_ref, count_ref, carry_ref):
    tm = x_ref.shape[0]
    half = hp_ref.shape[1]

    @pl.when(pl.program_id(0) == 0)
    def _():
        carry_ref[...] = jnp.zeros_like(carry_ref)

    h = _rms_rows(x_ref[...], g_ref[...])
    hp_ref[...] = _bf16_bits(h[:, :half]) | (_bf16_bits(h[:, half:]) << 16)
    h_hi = h.astype(BF16)
    h_lo = (h - h_hi.astype(F32)).astype(BF16)
    logits = (jnp.dot(h_hi, w_ref[0], preferred_element_type=F32)
              + (jnp.dot(h_hi, w_ref[1], preferred_element_type=F32)
                 + jnp.dot(h_lo, w_ref[0], preferred_element_type=F32)))
    lane = lax.broadcasted_iota(jnp.int32, logits.shape, 1).astype(F32)
    logits = jnp.where(lane < N_EXPERTS, logits, -jnp.inf)
    m1 = jnp.max(logits, axis=-1, keepdims=True)
    i1 = jnp.min(jnp.where(logits == m1, lane, float(ROUTER_LANES)), axis=-1, keepdims=True)
    rest = jnp.where(lane == i1, -jnp.inf, logits)
    m2 = jnp.max(rest, axis=-1, keepdims=True)
    i2 = jnp.min(jnp.where(rest == m2, lane, float(ROUTER_LANES)), axis=-1, keepdims=True)
    e2 = jnp.exp(m2 - m1)
    den = 1.0 + e2
    gate_ref[...] = jnp.where(lane == 0, 1.0 / den, e2 / den)
    chosen = jnp.where((lane == i1) | (lane == i2), 1.0, 0.0)
    earlier = lax.broadcasted_iota(jnp.int32, (tm, tm), 0) > lax.broadcasted_iota(jnp.int32, (tm, tm), 1)
    carry = carry_ref[...]
    before = jnp.dot(jnp.where(earlier, 1.0, 0.0).astype(BF16), chosen.astype(BF16),
                     preferred_element_type=F32) + carry
    r1 = jnp.sum(jnp.where(lane == i1, before, 0.0), axis=-1, keepdims=True)
    r2 = jnp.sum(jnp.where(lane == i2, before, 0.0), axis=-1, keepdims=True)
    idx_ref[...] = jnp.where(lane == 0, i1, jnp.where(lane == 1, i2, jnp.where(lane == 2, r1, r2))).astype(jnp.int32)
    carry = carry + jnp.sum(chosen, axis=0, keepdims=True)
    carry_ref[...] = carry
    count_ref[...] = carry.astype(jnp.int32)


def _router(x2, gain, router):
    rows, d = x2.shape
    tm = MOE_CHUNK
    w = jnp.zeros((d, ROUTER_LANES), F32).at[:, :N_EXPERTS].set(router.astype(F32))
    w_hi = w.astype(BF16)
    w = jnp.stack([w_hi, (w - w_hi.astype(F32)).astype(BF16)])
    row_spec = lambda width: pl.BlockSpec((tm, width), lambda i: (i, 0))
    return pl.pallas_call(
        _router_kernel,
        out_shape=(jax.ShapeDtypeStruct((rows, d // 2), jnp.uint32),
                   jax.ShapeDtypeStruct((rows, ROUTER_LANES), jnp.int32),
                   jax.ShapeDtypeStruct((rows, ROUTER_LANES), F32),
                   jax.ShapeDtypeStruct((1, ROUTER_LANES), jnp.int32)),
        grid=(rows // tm,),
        in_specs=[row_spec(d),
                  pl.BlockSpec((1, d), lambda i: (0, 0)),
                  pl.BlockSpec((2, d, ROUTER_LANES), lambda i: (0, 0, 0))],
        out_specs=(row_spec(d // 2), row_spec(ROUTER_LANES), row_spec(ROUTER_LANES),
                   pl.BlockSpec((1, ROUTER_LANES), lambda i: (0, 0))),
        scratch_shapes=[pltpu.VMEM((1, ROUTER_LANES), F32)],
        compiler_params=_params("arbitrary"),
        name="router",
    )(x2, gain.reshape(1, d), w)


def _sc_workers():
    info = plsc.get_sparse_core_info()
    mesh = plsc.VectorSubcoreMesh(core_axis_name="core", subcore_axis_name="subcore")
    return mesh, info.num_cores, info.num_subcores


def _pack_weight_rows(w, col_tile):
    r, c = w.shape
    mesh, nc, ns = _sc_workers()
    rs, ncol = SC_PACK_ROWS, c // col_tile
    per_w = (r // rs) * ncol // (nc * ns)
    assert per_w * nc * ns * rs * col_tile == r * c and per_w % 2 == 0 and col_tile % SC_LANES == 0
    params = pltpu.CompilerParams()
    if "needs_layout_passes" in pltpu.CompilerParams.__dataclass_fields__:
        params = dataclasses.replace(params, needs_layout_passes=False)

    @functools.partial(
        pl.kernel, mesh=mesh, out_type=jax.ShapeDtypeStruct((r // 2, c), jnp.uint32), compiler_params=params,
        cost_estimate=pl.CostEstimate(flops=r * c, transcendentals=0, bytes_accessed=6 * r * c),
        scratch_types=[pltpu.VMEM((2, rs, col_tile), F32), pltpu.VMEM((2, rs // 2, col_tile), jnp.uint32),
                       pltpu.SemaphoreType.DMA((2,)), pltpu.SemaphoreType.DMA((2,))])
    def pack(w_hbm, o_hbm, in_v, out_v, rsem, wsem):
        wid = lax.axis_index("subcore") * nc + lax.axis_index("core")

        @pl.loop(0, per_w // 2)
        def _(it):
            reads, writes = [], []
            for b in range(2):
                tile = wid * per_w + 2 * it + b
                r0 = pl.multiple_of((tile // ncol) * rs, rs)
                c0 = pl.multiple_of((tile % ncol) * col_tile, col_tile)
                reads.append(pltpu.make_async_copy(w_hbm.at[pl.ds(r0, rs), pl.ds(c0, col_tile)], in_v.at[b],
                                                   rsem.at[b]))
                writes.append(pltpu.make_async_copy(
                    out_v.at[b], o_hbm.at[pl.ds(pl.multiple_of(r0 // 2, rs // 2), rs // 2), pl.ds(c0, col_tile)],
                    wsem.at[b]))
            reads[0].start()
            reads[1].start()
            for b in range(2):
                reads[b].wait()
                for pair in range(rs // 2):
                    @plsc.parallel_loop(0, col_tile, step=SC_LANES, unroll=8)
                    def _(j):
                        packed = plsc.pack(in_v[b, 2 * pair, pl.ds(j, SC_LANES)],
                                           in_v[b, 2 * pair + 1, pl.ds(j, SC_LANES)],
                                           format=plsc.PackFormat.INTERLEAVED)
                        out_v[b, pair, pl.ds(j, SC_LANES)] = plsc.bitcast(packed, jnp.uint32)
                writes[b].start()
            writes[0].wait()
            writes[1].wait()

    return pack(w)


def _sc_token_rows(t, nc, ns):
    per_w = t // (nc * ns)
    assert per_w * nc * ns == t and per_w % (2 * SC_ROWS) == 0
    return per_w, per_w // SC_ROWS


def _scatter_rows(table, dests, p_rows, after=()):
    t, w = table.shape
    nk = len(dests)
    mesh, nc, ns = _sc_workers()
    per_w, nit = _sc_token_rows(t, nc, ns)
    ch = SC_ROWS

    @functools.partial(
        pl.kernel, mesh=mesh, out_type=jax.ShapeDtypeStruct((p_rows, w), table.dtype),
        scratch_types=[pltpu.VMEM((nk, nit, ch), jnp.int32), pltpu.VMEM((2, ch, w), table.dtype),
                       pltpu.SemaphoreType.DMA((2,)), pltpu.SemaphoreType.DMA((2, nk))])
    def scatter(table_hbm, *refs):
        dest_hbm, out_hbm = refs[:nk], refs[nk + len(after)]
        idx_v, rows_v, rsem, wsem = refs[nk + len(after) + 1:]
        wid = lax.axis_index("subcore") * nc + lax.axis_index("core")
        for k in range(nk):
            pltpu.sync_copy(dest_hbm[k].at[pl.ds(wid * nit, nit)], idx_v.at[k])

        @pl.loop(0, nit // 2)
        def _(it):
            reads = [pltpu.make_async_copy(table_hbm.at[pl.ds(wid * per_w + (2 * it + b) * ch, ch)],
                                           rows_v.at[b], rsem.at[b]) for b in range(2)]
            writes = [[pltpu.make_async_copy(rows_v.at[b], out_hbm.at[idx_v.at[k].at[2 * it + b]], wsem.at[b, k])
                       for k in range(nk)] for b in range(2)]
            reads[0].start()
            reads[1].start()
            for b in range(2):
                reads[b].wait()
                for k in range(nk):
                    writes[b][k].start()
            for b in range(2):
                for k in range(nk):
                    writes[b][k].wait()

    return scatter(table, *[d.reshape(t // ch, ch) for d in dests], *after)


def _gather_rows(table, idxs):
    t = idxs[0].shape[0]
    w = table.shape[1]
    nk = len(idxs)
    mesh, nc, ns = _sc_workers()
    per_w, nit = _sc_token_rows(t, nc, ns)
    ch = SC_ROWS

    @functools.partial(
        pl.kernel, mesh=mesh, out_type=[jax.ShapeDtypeStruct((t, w), table.dtype)] * nk,
        scratch_types=[pltpu.VMEM((nk, nit, ch), jnp.int32), pltpu.VMEM((nk, 2, ch, w), table.dtype),
                       pltpu.SemaphoreType.DMA((nk, 2)), pltpu.SemaphoreType.DMA((nk, 2))])
    def gather(table_hbm, *refs):
        idx_hbm, out_hbm = refs[:nk], refs[nk:2 * nk]
        idx_v, rows_v, rsem, wsem = refs[2 * nk:]
        wid = lax.axis_index("subcore") * nc + lax.axis_index("core")
        for k in range(nk):
            pltpu.sync_copy(idx_hbm[k].at[pl.ds(wid * nit, nit)], idx_v.at[k])

        @pl.loop(0, nit // 2)
        def _(it):
            slots = [(k, b) for k in range(nk) for b in range(2)]
            reads = {(k, b): pltpu.make_async_copy(table_hbm.at[idx_v.at[k].at[2 * it + b]], rows_v.at[k, b],
                                                   rsem.at[k, b]) for k, b in slots}
            writes = {(k, b): pltpu.make_async_copy(rows_v.at[k, b],
                                                    out_hbm[k].at[pl.ds(wid * per_w + (2 * it + b) * ch, ch)],
                                                    wsem.at[k, b]) for k, b in slots}
            for s in slots:
                reads[s].start()
            for s in slots:
                reads[s].wait()
                writes[s].start()
            for s in slots:
                writes[s].wait()

    return gather(table, *[i.reshape(t // ch, ch) for i in idxs])


def _combine_kernel(x_ref, y0_ref, y1_ref, gate_ref, *rest):
    o_ref = rest[-1]
    half = x_ref.shape[1] // 2
    for part in range(2):
        cols = slice(part * half, (part + 1) * half)
        acc = x_ref[:, cols]
        for k, y_ref in enumerate((y0_ref, y1_ref)):
            word = y_ref[...]
            bits = (word << 16) if part == 0 else (word & jnp.uint32(0xFFFF0000))
            acc = acc + gate_ref[:, k:k + 1] * lax.bitcast_convert_type(bits, F32)
        o_ref[:, cols] = acc


def _combine(x2, y0, y1, gates, part, earlier=None):
    t, d = x2.shape
    tm = ROW_TILE
    steps = y0.shape[0] // tm
    here = lambda i: (part * steps + i, 0)
    in_specs = [pl.BlockSpec((tm, d), here),
                pl.BlockSpec((tm, d // 2), lambda i: (i, 0)),
                pl.BlockSpec((tm, d // 2), lambda i: (i, 0)),
                pl.BlockSpec((tm, TOP_K), here)]
    operands = [x2, y0, y1, gates]
    if earlier is not None:
        in_specs.append(pl.BlockSpec(memory_space=pl.ANY))
        operands.append(earlier)
    return pl.pallas_call(
        _combine_kernel,
        out_shape=jax.ShapeDtypeStruct((t, d), F32),
        grid=(steps,),
        in_specs=in_specs,
        out_specs=pl.BlockSpec((tm, d), here),
        input_output_aliases={} if earlier is None else {len(operands) - 1: 0},
        compiler_params=_params("parallel"),
        name="moe_combine",
    )(*operands)


def _moe(x2, gain, router, wg, wu, wd):
    t, d = x2.shape
    hp, idx, gate, count = _router(x2, gain, router)
    experts = jnp.arange(N_EXPERTS, dtype=jnp.int32)
    counts = count[0, :N_EXPERTS]
    padded = (counts + MOE_ROWS - 1) // MOE_ROWS * MOE_ROWS
    pend = jnp.cumsum(padded)
    pstart = pend - padded
    expert, rank = idx[:, :TOP_K], idx[:, TOP_K:2 * TOP_K]
    dest = jnp.sum(jnp.where(expert[:, :, None] == experts, pstart, 0), axis=-1) + rank
    dests = [dest[:, k] for k in range(TOP_K)]
    p_rows = t * TOP_K + N_EXPERTS * MOE_ROWS
    blk_row = jnp.arange(p_rows // MOE_ROWS, dtype=jnp.int32) * MOE_ROWS
    blk_expert = jnp.minimum(jnp.sum(pend[None, :] <= blk_row[:, None], axis=1), N_EXPERTS - 1).astype(jnp.int32)
    valid = jnp.clip(counts[blk_expert] - (blk_row - pstart[blk_expert]), 0, MOE_ROWS)
    valid = jnp.where(blk_row < pend[-1], valid, 0).astype(jnp.int32)
    n_exp, _, f = wg.shape
    pack = lambda w, col_tile: _pack_weight_rows(w.reshape(-1, w.shape[2]), col_tile).reshape(n_exp, -1, w.shape[2])
    wg, wu, wd = pack(wg, f // 2), pack(wu, f // 2), pack(wd, d)
    hb = _scatter_rows(hp, dests, p_rows, after=(wg, wu, wd))
    yb = _ffn_experts(hb, wg, wu, wd, blk_expert, valid)
    out = None
    for part in range(MOE_TAIL_PARTS):
        rows = slice(part * t // MOE_TAIL_PARTS, (part + 1) * t // MOE_TAIL_PARTS)
        y0, y1 = _gather_rows(yb, [dk[rows] for dk in dests])
        out = _combine(x2, y0, y1, gate[:, :TOP_K], part, out)
    return out


def kernel(x, mem, rel_bias, norm_mix, norm_mem, norm_ffn, w_mem_kv, xq_norm, xk_norm, w_out, hy_w_in, hy_conv_w, hy_conv_b, hy_filt_w1, hy_filt_b1, hy_filt_w2, hy_filt_b2, hy_filt_w3, hy_sin_freq, hy_skip, at_w_in, at_q_norm, at_k_norm, ffn_w_gate, ffn_w_up, ffn_w_down, moe_router, moe_w_gate, moe_w_up, moe_w_down):
    b, s, d = x.shape
    t = b * s
    m_len = mem.shape[1]
    x2 = x.reshape(t, d)
    mem2 = mem.reshape(b * m_len, d)
    bf = lambda w: w.astype(BF16)
    score_scale = HD ** -0.5
    xa_heads = XA_W // HD
    at_heads = AT_W // HD
    plain = lambda width: [None] * (width // HD)
    kv_gains = lambda i: [xk_norm[i]] * xa_heads + plain(XA_W)

    tables = _dft_tables(s)
    spectra, hq = _hyena_filters(s, tables, hy_filt_w1[0], hy_filt_b1[0], hy_filt_w2[0], hy_filt_b2[0],
                                 hy_filt_w3[0], hy_sin_freq[0])
    z = _norm_matmul(x2, norm_mix[0], bf(hy_w_in[0]), 1024,
                     plain(3 * HY_C) + [xq_norm[0] * score_scale] * xa_heads).reshape(b, s, -1)
    kv = _norm_matmul(mem2, norm_mem[0], bf(w_mem_kv[0]), 1024, kv_gains(0)).reshape(b, m_len, -1)
    self_out = _hyena(z, hy_conv_w[0], hy_conv_b[0], tables, spectra, hq, hy_skip[0])
    x2 = _mix_out(x2.reshape(b, s, d), self_out, z, 3 * HY_C // XA_W, kv, bf(w_out[0])).reshape(t, d)
    x2 = _ffn_dense(x2, norm_ffn[0], bf(ffn_w_gate[0]), bf(ffn_w_up[0]), bf(ffn_w_down[0]))

    at_gains = ([at_q_norm[0] * score_scale] * at_heads + [at_k_norm[0]] * at_heads + plain(AT_W)
                + [xq_norm[1] * score_scale] * xa_heads)
    z = _norm_matmul(x2, norm_mix[1], bf(at_w_in[0]), 1024, at_gains).reshape(b, s, -1)
    kv = _norm_matmul(mem2, norm_mem[1], bf(w_mem_kv[1]), 1024, kv_gains(1)).reshape(b, m_len, -1)
    self_out = _dilated_attention(z, rel_bias)
    x2 = _mix_out(x2.reshape(b, s, d), self_out, z, 3 * AT_W // XA_W, kv, bf(w_out[1])).reshape(t, d)
    x2 = _moe(x2, norm_ffn[1], moe_router[0], moe_w_gate[0], moe_w_up[0], moe_w_down[0])
    return x2.reshape(b, s, d)
```

```python
import dataclasses
import functools
import math

import jax
import jax.numpy as jnp
import numpy as np
from jax import lax
from jax.experimental import pallas as pl
from jax.experimental.pallas import tpu as pltpu
from jax.experimental.pallas import tpu_sc as plsc

F32 = jnp.float32
BF16 = jnp.bfloat16

D_MODEL = 1024
EPS = 1e-6
HY_C = 512
FILT_BANDS = 16
DECAY_TARGET = 1e-2
FAST_DECAY_PCT = 0.3
SLOW_DECAY_PCT = 1.5
MOD_SHIFT = 0.05
AT_GROUPS = ((128, 1), (512, 4), (2048, 16))
AT_HEADS = 4
HD = 128
AT_W = 1536
NUM_BUCKETS = 32
REL_MAX_DIST = 1024
NEG_INF = -1e30
XA_W = 512
D_FF = 2816
N_EXPERTS = 8
TOP_K = 2

VMEM_LIMIT_BYTES = 56 * 1024 * 1024
ROW_TILE = 1024
PROJ_ROW_TILE = 1024
FF_TILE = 256
MOE_ROWS = 512
MOE_CHUNK = 512
MOE_TAIL_PARTS = 2
SC_ROWS = 32
SC_SCATTER_ROWS = 64
SC_LANES = 16
SC_PACK_ROWS = 16
ATT_TQ = 128
ATT_KW = 256
HY_TC = 256
HY_KT = 512


def _params(*sem):
    return pltpu.CompilerParams(dimension_semantics=sem, vmem_limit_bytes=VMEM_LIMIT_BYTES)


def _rms_rows(x, gain):
    return x * lax.rsqrt(jnp.mean(x * x, axis=-1, keepdims=True) + EPS) * gain


def _bf16_bits(v):
    u = lax.bitcast_convert_type(v, jnp.uint32)
    return (u + jnp.uint32(0x7FFF) + ((u >> 16) & jnp.uint32(1))) >> 16


def _resident(shape, index_map):
    return pl.BlockSpec(shape, index_map, pipeline_mode=pl.Buffered(1))


def _norm_matmul_kernel(x_ref, g_ref, w_ref, hg_ref, o_ref, *, tn, head_norm):
    h = _rms_rows(x_ref[...], g_ref[...]).astype(BF16)
    for c in range(o_ref.shape[1] // tn):
        acc = jnp.dot(h, w_ref[:, c * tn:(c + 1) * tn], preferred_element_type=F32)
        for j in range(tn // HD):
            cols = slice(c * tn + j * HD, c * tn + (j + 1) * HD)
            seg = acc[:, j * HD:(j + 1) * HD]
            if head_norm[cols.start // HD]:
                seg = _rms_rows(seg, hg_ref[:, cols])
            o_ref[:, cols] = seg.astype(o_ref.dtype)


def _norm_matmul(x2, gain, w_bf, tn, head_gains):
    rows, d = x2.shape
    n = w_bf.shape[1]
    tm = min(PROJ_ROW_TILE, rows)
    assert len(head_gains) * HD == n
    hg = jnp.concatenate([jnp.ones((HD,), F32) if g is None else g.astype(F32) for g in head_gains]).reshape(1, n)
    return pl.pallas_call(
        functools.partial(_norm_matmul_kernel, tn=tn, head_norm=tuple(g is not None for g in head_gains)),
        out_shape=jax.ShapeDtypeStruct((rows, n), BF16),
        grid=(rows // tm,),
        in_specs=[pl.BlockSpec((tm, d), lambda i: (i, 0)),
                  pl.BlockSpec((1, d), lambda i: (0, 0)),
                  _resident((d, n), lambda i: (0, 0)),
                  pl.BlockSpec((1, n), lambda i: (0, 0))],
        out_specs=pl.BlockSpec((tm, n), lambda i: (i, 0)),
        compiler_params=_params("parallel"),
        cost_estimate=pl.CostEstimate(flops=2 * rows * d * n, transcendentals=rows,
                                      bytes_accessed=4 * rows * d + 2 * d * n + 2 * rows * n),
        name="norm_matmul",
    )(x2, gain.reshape(1, d), w_bf, hg)


def _mix_out_kernel(x_ref, a_ref, xq_ref, kv_ref, wa_ref, wc_ref, o_ref):
    heads = []
    for h in range(XA_W // HD):
        cols = slice(h * HD, (h + 1) * HD)
        s = lax.dot_general(xq_ref[0, :, cols], kv_ref[0, :, cols], (((1,), (1,)), ((), ())),
                            preferred_element_type=F32)
        p = jnp.exp(s - jnp.max(s, axis=-1, keepdims=True))
        z = jnp.sum(p, axis=-1, keepdims=True)
        pv = jnp.dot(p.astype(BF16), kv_ref[0, :, XA_W + h * HD:XA_W + (h + 1) * HD], preferred_element_type=F32)
        heads.append((pv / z).astype(BF16))
    cross = jnp.concatenate(heads, axis=1)
    o_ref[0] = (x_ref[0]
                + jnp.dot(a_ref[0], wa_ref[...], preferred_element_type=F32)
                + jnp.dot(cross, wc_ref[...], preferred_element_type=F32))


def _mix_out(x3, self_out, z3, xq_block, kv3, w_bf):
    b, s, d = x3.shape
    half = self_out.shape[2]
    m = kv3.shape[1]
    tm = ROW_TILE
    return pl.pallas_call(
        _mix_out_kernel,
        out_shape=jax.ShapeDtypeStruct((b, s, d), F32),
        grid=(b, s // tm),
        in_specs=[pl.BlockSpec((1, tm, d), lambda i, j: (i, j, 0)),
                  pl.BlockSpec((1, tm, half), lambda i, j: (i, j, 0)),
                  pl.BlockSpec((1, tm, XA_W), lambda i, j: (i, j, xq_block)),
                  pl.BlockSpec((1, m, 2 * XA_W), lambda i, j: (i, 0, 0)),
                  pl.BlockSpec((half, d), lambda i, j: (0, 0)),
                  pl.BlockSpec((d - half, d), lambda i, j: (1, 0))],
        out_specs=pl.BlockSpec((1, tm, d), lambda i, j: (i, j, 0)),
        compiler_params=_params("parallel", "parallel"),
        name="mix_out",
    )(x3, self_out, z3, kv3, w_bf, w_bf)


def _weight_rows(ref, start, size):
    if ref.dtype == jnp.uint32:
        return lambda cols: pltpu.bitcast(ref[start // 2:(start + size) // 2, cols], BF16)
    return lambda cols: ref[start:start + size, cols]


def _weight_shape(ref):
    return (ref.shape[0] * (2 if ref.dtype == jnp.uint32 else 1), ref.shape[1])


def _swiglu(h, wg_ref, wu_ref, wd_ref, tf):
    d, f = _weight_shape(wg_ref)
    assert f % tf == 0
    y = None
    for j in range(f // tf):
        cols = slice(j * tf, (j + 1) * tf)
        gg = jnp.dot(h, _weight_rows(wg_ref, 0, d)(cols), preferred_element_type=F32)
        uu = jnp.dot(h, _weight_rows(wu_ref, 0, d)(cols), preferred_element_type=F32)
        a = ((gg * jax.nn.sigmoid(gg)) * uu).astype(BF16)
        part = jnp.dot(a, _weight_rows(wd_ref, j * tf, tf)(slice(None)), preferred_element_type=F32)
        y = part if y is None else y + part
    return y


def _ffn_dense_kernel(x_ref, g_ref, wg_ref, wu_ref, wd_ref, o_ref, *, tf):
    x = x_ref[...]
    h = _rms_rows(x, g_ref[...]).astype(BF16)
    o_ref[...] = x + _swiglu(h, wg_ref, wu_ref, wd_ref, tf)


def _ffn_dense(x2, gain, wg, wu, wd):
    rows, d = x2.shape
    f = wg.shape[1]
    tm = ROW_TILE
    return pl.pallas_call(
        functools.partial(_ffn_dense_kernel, tf=FF_TILE),
        out_shape=jax.ShapeDtypeStruct((rows, d), F32),
        grid=(rows // tm,),
        in_specs=[pl.BlockSpec((tm, d), lambda i: (i, 0)),
                  pl.BlockSpec((1, d), lambda i: (0, 0)),
                  _resident((d, f), lambda i: (0, 0)),
                  _resident((d, f), lambda i: (0, 0)),
                  _resident((f, d), lambda i: (0, 0))],
        out_specs=pl.BlockSpec((tm, d), lambda i: (i, 0)),
        compiler_params=_params("parallel"),
        cost_estimate=pl.CostEstimate(flops=6 * rows * d * f, transcendentals=rows * f,
                                      bytes_accessed=8 * rows * d + 6 * d * f),
        name="ffn_dense",
    )(x2, gain.reshape(1, d), wg, wu, wd)


def _ffn_expert_kernel(eid_ref, valid_ref, hp_ref, wg_ref, wu_ref, wd_ref, o_ref, h_ref, *, tf):
    half = hp_ref.shape[1]
    valid = valid_ref[pl.program_id(0)]

    @pl.when(valid > 0)
    def _():
        keep = lax.broadcasted_iota(jnp.int32, (hp_ref.shape[0], 1), 0) < valid
        word = hp_ref[...]
        h_ref[:, :half] = jnp.where(keep, lax.bitcast_convert_type(word << 16, F32), 0.0).astype(BF16)
        h_ref[:, half:] = jnp.where(keep, lax.bitcast_convert_type(word & jnp.uint32(0xFFFF0000), F32),
                                    0.0).astype(BF16)
        y = _swiglu(h_ref[...], wg_ref.at[0], wu_ref.at[0], wd_ref.at[0], tf)
        o_ref[...] = _bf16_bits(y[:, :half]) | (_bf16_bits(y[:, half:]) << 16)

    @pl.when(valid <= 0)
    def _():
        o_ref[...] = jnp.zeros_like(o_ref)


def _ffn_experts(hp, wg, wu, wd, eid, valid):
    rows, half = hp.shape
    d = 2 * half
    tm = MOE_ROWS
    expert_spec = lambda w: pl.BlockSpec((1,) + w.shape[1:], lambda i, e, n: (e[i], 0, 0))
    grid_spec = pltpu.PrefetchScalarGridSpec(
        num_scalar_prefetch=2,
        grid=(rows // tm,),
        in_specs=[pl.BlockSpec((tm, half), lambda i, e, n: (i, 0)),
                  expert_spec(wg), expert_spec(wu), expert_spec(wd)],
        out_specs=pl.BlockSpec((tm, half), lambda i, e, n: (i, 0)),
        scratch_shapes=[pltpu.VMEM((tm, d), BF16)],
    )
    return pl.pallas_call(
        functools.partial(_ffn_expert_kernel, tf=FF_TILE),
        out_shape=jax.ShapeDtypeStruct((rows, half), jnp.uint32),
        grid_spec=grid_spec,
        compiler_params=_params("arbitrary"),
        name="ffn_experts",
    )(eid, valid, hp, wg, wu, wd)


def _dft_kernel(ce_ref, se_ref, co_ref, so_ref, cot_ref, sot_ref, base_ref, *, n_fft):
    rows, cols = ce_ref.shape
    i = pl.program_id(0)
    theta = 2.0 * math.pi / n_fft

    def phases(r, c):
        return r * (2 * c), r * (2 * c + 1), c * (2 * r + 1)

    @pl.when(i == 0)
    def _():
        r = lax.broadcasted_iota(jnp.int32, (rows, cols), 0)
        c = lax.broadcasted_iota(jnp.int32, (rows, cols), 1)
        for f, ph in enumerate(phases(r, c)):
            ang = (ph & (n_fft - 1)).astype(F32) * theta
            base_ref[2 * f] = jnp.cos(ang)
            base_ref[2 * f + 1] = jnp.sin(ang)

    c = lax.broadcasted_iota(jnp.int32, (8, cols), 1)
    r0 = i * rows
    shifts = (r0 * (2 * c), r0 * (2 * c + 1), c * (2 * r0))
    for f, (c_ref, s_ref) in enumerate(((ce_ref, se_ref), (co_ref, so_ref), (cot_ref, sot_ref))):
        ang = (shifts[f] & (n_fft - 1)).astype(F32) * theta
        ca, sa = jnp.cos(ang)[0:1], jnp.sin(ang)[0:1]
        cb, sb = base_ref[2 * f], base_ref[2 * f + 1]
        c_ref[...] = (cb * ca - sb * sa).astype(BF16)
        s_ref[...] = (sb * ca + cb * sa).astype(BF16)


def _dft_tables(length):
    half, rows = length // 2, 128
    shape = jax.ShapeDtypeStruct((half, half), BF16)
    spec = pl.BlockSpec((rows, half), lambda i: (i, 0))
    return pl.pallas_call(
        functools.partial(_dft_kernel, n_fft=2 * length),
        out_shape=(shape,) * 6,
        grid=(half // rows,),
        out_specs=(spec,) * 6,
        scratch_shapes=[pltpu.VMEM((6, rows, half), F32)],
        compiler_params=_params("arbitrary"),
        name="dft_tables",
    )()


LANES = 128


def _split_parity(tmp_ref, x):
    half = x.shape[0] // 2
    for j in range(tmp_ref.shape[0]):
        tmp_ref[j] = x[:, j * LANES:(j + 1) * LANES]
    pick = lambda start: jnp.concatenate(
        [tmp_ref[j, pl.ds(start, half, stride=2), :] for j in range(tmp_ref.shape[0])], axis=1)
    return pick(0), pick(1)


def _merge_parity(tmp_ref, even, odd):
    half = even.shape[0]
    for j in range(tmp_ref.shape[0]):
        tmp_ref[j, pl.ds(0, half, stride=2), :] = even[:, j * LANES:(j + 1) * LANES]
        tmp_ref[j, pl.ds(1, half, stride=2), :] = odd[:, j * LANES:(j + 1) * LANES]
    return jnp.concatenate([tmp_ref[j] for j in range(tmp_ref.shape[0])], axis=1)


def _filter_time_kernel(feats_ref, w1_ref, b1_ref, w2_ref, b2_ref, fr_ref, w3_ref, t_ref, delta_ref,
                        g_ref, hq_ref, tmp_ref, h_ref):
    hp = lax.Precision.HIGHEST
    length = feats_ref.shape[0]
    half = length // 2

    @pl.when(pl.program_id(0) == 0)
    def _():
        fr = fr_ref[...]
        h1 = jnp.sin(fr * (jnp.dot(feats_ref[...], w1_ref[...], preferred_element_type=F32, precision=hp)
                           + b1_ref[...]))
        h_ref[...] = jnp.sin(fr * (jnp.dot(h1, w2_ref[...], preferred_element_type=F32, precision=hp) + b2_ref[...]))

    h = h_ref[...]
    mod =jnp.exp(-t_ref[...] * delta_ref[...]) + MOD_SHIFT
    row = lax.broadcasted_iota(jnp.int32, (length, 1), 0)
    alt = jnp.where(lax.broadcasted_iota(jnp.int32, (half, 1), 0) % 2 == 0, 1.0, -1.0).astype(F32)
    for o in range(2):
        fwd = jnp.dot(h, w3_ref[2 * o], preferred_element_type=F32, precision=hp) * mod
        bwd = jnp.dot(h, w3_ref[2 * o + 1], preferred_element_type=F32, precision=hp) * mod
        bwd = jnp.where(row == 0, 0.0, bwd)
        norm = (jnp.sum(jnp.abs(fwd), axis=0, keepdims=True)
                + jnp.sum(jnp.abs(bwd), axis=0, keepdims=True) + 1e-6)
        fwd = fwd / norm
        bwd = bwd / norm
        for part, g in enumerate((fwd + bwd, fwd - bwd)):
            even, odd = _split_parity(tmp_ref, g)
            g_ref[o, 2 * part] = even.astype(BF16)
            g_ref[o, 2 * part + 1] = odd.astype(BF16)
            mid = jnp.sum((even if part == 0 else odd) * alt, axis=0, keepdims=True) * (1.0 / length)
            hq_ref[o, part] = mid if part == 0 else -mid


def _filter_freq_kernel(ce_ref, se_ref, co_ref, so_ref, g_ref, h_ref):
    half = ce_ref.shape[0]
    row = lax.broadcasted_iota(jnp.int32, (half, 1), 0)
    scale = jnp.where(row == 0, 1.0, 2.0).astype(F32) * (1.0 / (4 * half))
    a = jnp.dot(ce_ref[...], g_ref[0, 0], preferred_element_type=F32)
    b = jnp.dot(co_ref[...], g_ref[0, 1], preferred_element_type=F32)
    c = jnp.dot(se_ref[...], g_ref[0, 2], preferred_element_type=F32)
    d = jnp.dot(so_ref[...], g_ref[0, 3], preferred_element_type=F32)
    h_ref[0, 0] = (a + b) * scale
    h_ref[0, 1] = -(c + d) * scale
    h_ref[0, 2] = (a - b) * scale
    h_ref[0, 3] = (c - d) * scale


def _hyena_filters(length, tables, w1, b1, w2, b2, w3, freq):
    t = jnp.linspace(0.0, 1.0, length, dtype=F32)[:, None]
    f = jnp.linspace(1e-4, FILT_BANDS - 1, FILT_BANDS, dtype=F32)[None]
    ang = (2.0 * math.pi / length) * jnp.arange(length, dtype=F32)[:, None] * f
    feats = jnp.concatenate([t, jnp.cos(ang), -jnp.sin(ang)], axis=-1)
    deltas = jnp.abs(jnp.linspace(math.log(DECAY_TARGET) / SLOW_DECAY_PCT,
                                  math.log(DECAY_TARGET) / FAST_DECAY_PCT, HY_C, dtype=F32))[None]
    hid = w1.shape[1]
    w3r = w3.reshape(hid, 4, HY_C).transpose(1, 0, 2)
    tc = HY_TC
    full = lambda shape: pl.BlockSpec(shape, lambda c: (0,) * len(shape))
    half = length // 2
    g, hq = pl.pallas_call(
        _filter_time_kernel,
        out_shape=(jax.ShapeDtypeStruct((2, 4, half, HY_C), BF16),
                   jax.ShapeDtypeStruct((2, 2, 1, HY_C), F32)),
        grid=(HY_C // tc,),
        in_specs=[full(feats.shape), full(w1.shape), full((1, hid)), full(w2.shape), full((1, hid)),
                  full((1, hid)), pl.BlockSpec((4, hid, tc), lambda c: (0, 0, c)), full((length, 1)),
                  pl.BlockSpec((1, tc), lambda c: (0, c))],
        out_specs=(pl.BlockSpec((2, 4, half, tc), lambda c: (0, 0, 0, c)),
                   pl.BlockSpec((2, 2, 1, tc), lambda c: (0, 0, 0, c))),
        scratch_shapes=[pltpu.VMEM((tc // LANES, length, LANES), F32), pltpu.VMEM((length, hid), F32)],
        compiler_params=_params("arbitrary"),
        name="filter_time",
    )(feats, w1, b1.reshape(1, hid), w2, b2.reshape(1, hid), freq.reshape(1, hid), w3r, t, deltas)
    ce, se, co, so = tables[:4]
    table_spec = _resident((half, half), lambda o, c: (0, 0))
    spectra = pl.pallas_call(
        _filter_freq_kernel,
        out_shape=jax.ShapeDtypeStruct((2, 4, half, HY_C), F32),
        grid=(2, HY_C // tc),
        in_specs=[table_spec] * 4 + [pl.BlockSpec((1, 4, half, tc), lambda o, c: (o, 0, 0, c))],
        out_specs=pl.BlockSpec((1, 4, half, tc), lambda o, c: (o, 0, 0, c)),
        compiler_params=_params("parallel", "parallel"),
        name="filter_freq",
    )(ce, se, co, so, g)
    return spectra, hq


def _hyena_kernel(z0_ref, z1_ref, z2_ref, cw_ref, cb_ref, ce_ref, se_ref, co_ref, so_ref, cot_ref, sot_ref,
                  h_ref, hq_ref, skip_ref, o_ref, tmp_ref):
    length = z0_ref.shape[1]
    half = length // 2
    row = lax.broadcasted_iota(jnp.int32, (length, 1), 0)
    alt = jnp.where(lax.broadcasted_iota(jnp.int32, (half, 1), 0) % 2 == 0, 1.0, -1.0).astype(F32)

    def short_conv(z_ref, c):
        u = z_ref[0].astype(F32)
        prev = jnp.where(row == 0, 0.0, pltpu.roll(u, 1, axis=0))
        nxt = jnp.where(row == length - 1, 0.0, pltpu.roll(u, length - 1, axis=0))
        return cb_ref[c] + (prev * cw_ref[0, c] + u * cw_ref[1, c] + nxt * cw_ref[2, c])

    def long_conv(u, o):
        ue, uo = _split_parity(tmp_ref, u)
        ue_bf, uo_bf = ue.astype(BF16), uo.astype(BF16)
        a4 = jnp.sum(ue * alt, axis=0, keepdims=True)
        b4 = jnp.sum(uo * alt, axis=0, keepdims=True)
        hr4, hi4 = hq_ref[o, 0], hq_ref[o, 1]
        y_even = alt * (a4 * hr4 + b4 * hi4)
        y_odd = -(alt * (a4 * hi4 - b4 * hr4))
        for kt in range(half // HY_KT):
            ks = slice(kt * HY_KT, (kt + 1) * HY_KT)
            ae = jnp.dot(ce_ref[ks, :], ue_bf, preferred_element_type=F32)
            ao = jnp.dot(co_ref[ks, :], uo_bf, preferred_element_type=F32)
            be = jnp.dot(se_ref[ks, :], ue_bf, preferred_element_type=F32)
            bo = jnp.dot(so_ref[ks, :], uo_bf, preferred_element_type=F32)
            a_f, a_g, b_f, b_g = ae + ao, ae - ao, be + bo, bo - be
            hr_f, hi_f, hr_g, hi_g = (h_ref[o, j, ks, :] for j in range(4))
            re_f, im_f = a_f * hr_f + b_f * hi_f, a_f * hi_f - b_f * hr_f
            re_g, im_g = a_g * hr_g + b_g * hi_g, a_g * hi_g - b_g * hr_g
            y_even += (jnp.dot(ce_ref[:, ks], (re_f + re_g).astype(BF16), preferred_element_type=F32)
                       - jnp.dot(se_ref[:, ks], (im_f - im_g).astype(BF16), preferred_element_type=F32))
            y_odd += (jnp.dot(cot_ref[:, ks], (re_f - re_g).astype(BF16), preferred_element_type=F32)
                      - jnp.dot(sot_ref[:, ks], (im_f + im_g).astype(BF16), preferred_element_type=F32))
        return _merge_parity(tmp_ref, y_even, y_odd) + u * skip_ref[o]

    z = short_conv(z1_ref, 1) * long_conv(short_conv(z0_ref, 0), 0)
    o_ref[0] = (short_conv(z2_ref, 2) * long_conv(z, 1)).astype(o_ref.dtype)


def _hyena(z3, conv_w, conv_b, tables, spectra, hq, skip):
    b, s, _ = z3.shape
    half = s // 2
    tc = HY_TC
    nct = HY_C // tc
    cw = conv_w.reshape(3, 3, 1, HY_C)
    cb = conv_b.reshape(3, 1, HY_C)
    zspec = lambda chunk: pl.BlockSpec((1, s, tc), lambda c, i: (i, 0, chunk * nct + c))
    return pl.pallas_call(
        _hyena_kernel,
        out_shape=jax.ShapeDtypeStruct((b, s, HY_C), BF16),
        grid=(nct, b),
        in_specs=[zspec(0), zspec(1), zspec(2),
                  pl.BlockSpec((3, 3, 1, tc), lambda c, i: (0, 0, 0, c)),
                  pl.BlockSpec((3, 1, tc), lambda c, i: (0, 0, c))]
                 + [_resident((half, half), lambda c, i: (0, 0))] * 6
                 + [_resident((2, 4, half, tc), lambda c, i: (0, 0, 0, c)),
                    pl.BlockSpec((2, 2, 1, tc), lambda c, i: (0, 0, 0, c)),
                    pl.BlockSpec((2, 1, tc), lambda c, i: (0, 0, c))],
        out_specs=pl.BlockSpec((1, s, tc), lambda c, i: (i, 0, c)),
        scratch_shapes=[pltpu.VMEM((tc // LANES, s, LANES), F32)],
        compiler_params=_params("parallel", "parallel"),
        cost_estimate=pl.CostEstimate(flops=2 * 4 * 2 * b * s * half * HY_C, transcendentals=0,
                                      bytes_accessed=2 * 3 * b * s * HY_C + 4 * b * s * HY_C + 12 * half * half),
        name="hyena",
    )(z3, z3, z3, cw, cb, *tables, spectra, hq, skip.reshape(2, 1, HY_C))


def _rel_bucket(rel):
    half = NUM_BUCKETS // 2
    exact = half // 2
    n = np.abs(rel)
    large = exact + (np.log(np.maximum(n, 1) / exact) / np.log(REL_MAX_DIST / exact) * (half - exact)).astype(np.int32)
    large = np.minimum(large, half - 1)
    return (np.where(rel > 0, half, 0) + np.where(n < exact, n, large)).astype(np.int32)


def _att_tiles(length):
    kw = min(ATT_KW, length)
    tiles = []
    for qs in range(0, length, ATT_TQ):
        ks = min(max(qs - (kw - ATT_TQ) // 2, 0), length - kw)
        tiles.append((qs, ks, {0: 0, -64: 1, -128: 2}[ks - qs]))
    return kw, tiles


def _att_bias(rel_bias, group, seq):
    window, dil = AT_GROUPS[group]
    band = window // (2 * dil)
    kw, tiles = _att_tiles(seq // dil)
    offsets = sorted({ks - qs for qs, ks, _ in tiles}, reverse=True)
    table = rel_bias[:, group * AT_HEADS:(group + 1) * AT_HEADS].astype(F32)
    span = kw + ATT_TQ
    k = np.arange(span)
    rel = np.asarray(offsets)[:, None] + np.where(k < kw, k, k - span)[None, :]
    diag = jnp.where((np.abs(rel) <= band)[:, None, :],
                     jnp.swapaxes(table[_rel_bucket(rel * dil)], 1, 2), NEG_INF)
    return diag[:, :, None, :]


def _bias_tile(diag_ref, var, kw):
    rows = jnp.broadcast_to(diag_ref[var, 0], (ATT_TQ, diag_ref.shape[3]))
    return pltpu.roll(rows, 0, 1, stride=1, stride_axis=0)[:, :kw]


def _dilated_kernel(q1, q2, q3, k1, k2, k3, v1, v2, v3, b1, b2, b3, o_ref,
                    qn_ref, kn_ref, vn_ref, og_ref, lg_ref):
    seq = o_ref.shape[1]
    contract_last = (((1,), (1,)), ((), ()))
    for g, (q_ref, k_ref, v_ref, bias_ref) in enumerate(((q1, k1, v1, b1), (q2, k2, v2, b2), (q3, k3, v3, b3))):
        dil = AT_GROUPS[g][1]
        kw, tiles = _att_tiles(seq // dil)
        if dil > 1:
            qn_ref[...] = q_ref[0].astype(F32)
            kn_ref[...] = k_ref[0].astype(F32)
            vn_ref[...] = v_ref[0].astype(F32)
        bias = {var: _bias_tile(bias_ref, var, kw) for var in sorted({var for _, _, var in tiles})}
        for r in range(dil):
            for qs, ks, var in tiles:
                rows = lambda start, size: (pl.ds(r + start * dil, size, stride=dil) if dil > 1
                                            else pl.ds(start, size))
                if dil > 1:
                    qt = qn_ref[rows(qs, ATT_TQ), :].astype(BF16)
                    kt = kn_ref[rows(ks, kw), :].astype(BF16)
                    vt = vn_ref[rows(ks, kw), :].astype(BF16)
                else:
                    qt, kt, vt = q_ref[0, rows(qs, ATT_TQ), :], k_ref[0, rows(ks, kw), :], v_ref[0, rows(ks, kw), :]
                s = lax.dot_general(qt, kt, contract_last, preferred_element_type=F32) + bias[var]
                m = jnp.max(s, axis=-1, keepdims=True)
                p = jnp.exp(s - m)
                z = jnp.sum(p, axis=-1, keepdims=True)
                og_ref[g, rows(qs, ATT_TQ), :] = jnp.dot(p.astype(BF16), vt, preferred_element_type=F32) / z
                lg_ref[g, rows(qs, ATT_TQ), :] = jnp.broadcast_to(m + jnp.log(z), (ATT_TQ, HD))
    l0, l1, l2 = lg_ref[0], lg_ref[1], lg_ref[2]
    mx = jnp.maximum(jnp.maximum(l0, l1), l2)
    e0, e1, e2 = jnp.exp(l0 - mx), jnp.exp(l1 - mx), jnp.exp(l2 - mx)
    o_ref[0] = ((e0 * og_ref[0] + e1 * og_ref[1] + e2 * og_ref[2]) / (e0 + e1 + e2)).astype(o_ref.dtype)


def _dilated_attention(z3, rel_bias):
    b, s, _ = z3.shape
    ng = len(AT_GROUPS)
    col = lambda part, g: pl.BlockSpec((1, s, HD), lambda i, h: (i, 0, part * ng * AT_HEADS + g * AT_HEADS + h))
    biases = [_att_bias(rel_bias, g, s) for g in range(ng)]
    bias_spec = lambda a: pl.BlockSpec((a.shape[0], 1) + a.shape[2:], lambda i, h: (0, h, 0, 0))
    return pl.pallas_call(
        _dilated_kernel,
        out_shape=jax.ShapeDtypeStruct((b, s, AT_HEADS * HD), BF16),
        grid=(b, AT_HEADS),
        in_specs=[col(p, g) for p in range(3) for g in range(ng)] + [bias_spec(a) for a in biases],
        out_specs=pl.BlockSpec((1, s, HD), lambda i, h: (i, 0, h)),
        scratch_shapes=[pltpu.VMEM((s, HD), F32), pltpu.VMEM((s, HD), F32), pltpu.VMEM((s, HD), F32),
                        pltpu.VMEM((ng, s, HD), F32), pltpu.VMEM((ng, s, HD), F32)],
        compiler_params=_params("parallel", "parallel"),
        name="dilated_attention",
    )(*([z3] * 9), *biases)


ROUTER_LANES = 128


def _router_kernel(x_ref, g_ref, w_ref, hp_ref, idx_ref, gate_ref, count_ref, carry_ref):
    tm = x_ref.shape[0]
    half = hp_ref.shape[1]

    @pl.when(pl.program_id(0) == 0)
    def _():
        carry_ref[...] = jnp.zeros_like(carry_ref)

    h = _rms_rows(x_ref[...], g_ref[...])
    hp_ref[...] = _bf16_bits(h[:, :half]) | (_bf16_bits(h[:, half:]) << 16)
    h_hi = h.astype(BF16)
    h_lo = (h - h_hi.astype(F32)).astype(BF16)
    logits = (jnp.dot(h_hi, w_ref[0], preferred_element_type=F32)
              + (jnp.dot(h_hi, w_ref[1], preferred_element_type=F32)
                 + jnp.dot(h_lo, w_ref[0], preferred_element_type=F32)))
    lane = lax.broadcasted_iota(jnp.int32, logits.shape, 1).astype(F32)
    logits = jnp.where(lane < N_EXPERTS, logits, -jnp.inf)
    m1 = jnp.max(logits, axis=-1, keepdims=True)
    i1 = jnp.min(jnp.where(logits == m1, lane, float(ROUTER_LANES)), axis=-1, keepdims=True)
    rest = jnp.where(lane == i1, -jnp.inf, logits)
    m2 = jnp.max(rest, axis=-1, keepdims=True)
    i2 = jnp.min(jnp.where(rest == m2, lane, float(ROUTER_LANES)), axis=-1, keepdims=True)
    e2 = jnp.exp(m2 - m1)
    den = 1.0 + e2
    gate_ref[...] = jnp.where(lane == 0, 1.0 / den, e2 / den)
    chosen = jnp.where((lane == i1) | (lane == i2), 1.0, 0.0)
    earlier = lax.broadcasted_iota(jnp.int32, (tm, tm), 0) > lax.broadcasted_iota(jnp.int32, (tm, tm), 1)
    carry = carry_ref[...]
    before = jnp.dot(jnp.where(earlier, 1.0, 0.0).astype(BF16), chosen.astype(BF16),
                     preferred_element_type=F32) + carry
    r1 = jnp.sum(jnp.where(lane == i1, before, 0.0), axis=-1, keepdims=True)
    r2 = jnp.sum(jnp.where(lane == i2, before, 0.0), axis=-1, keepdims=True)
    idx_ref[...] = jnp.where(lane == 0, i1, jnp.where(lane == 1, i2, jnp.where(lane == 2, r1, r2))).astype(jnp.int32)
    carry = carry + jnp.sum(chosen, axis=0, keepdims=True)
    carry_ref[...] = carry
    count_ref[...] = carry.astype(jnp.int32)


def _router(x2, gain, router):
    rows, d = x2.shape
    tm = MOE_CHUNK
    w = jnp.zeros((d, ROUTER_LANES), F32).at[:, :N_EXPERTS].set(router.astype(F32))
    w_hi = w.astype(BF16)
    w = jnp.stack([w_hi, (w - w_hi.astype(F32)).astype(BF16)])
    row_spec = lambda width: pl.BlockSpec((tm, width), lambda i: (i, 0))
    return pl.pallas_call(
        _router_kernel,
        out_shape=(jax.ShapeDtypeStruct((rows, d // 2), jnp.uint32),
                   jax.ShapeDtypeStruct((rows, ROUTER_LANES), jnp.int32),
                   jax.ShapeDtypeStruct((rows, ROUTER_LANES), F32),
                   jax.ShapeDtypeStruct((1, ROUTER_LANES), jnp.int32)),
        grid=(rows // tm,),
        in_specs=[row_spec(d),
                  pl.BlockSpec((1, d), lambda i: (0, 0)),
                  pl.BlockSpec((2, d, ROUTER_LANES), lambda i: (0, 0, 0))],
        out_specs=(row_spec(d // 2), row_spec(ROUTER_LANES), row_spec(ROUTER_LANES),
                   pl.BlockSpec((1, ROUTER_LANES), lambda i: (0, 0))),
        scratch_shapes=[pltpu.VMEM((1, ROUTER_LANES), F32)],
        compiler_params=_params("arbitrary"),
        name="router",
    )(x2, gain.reshape(1, d), w)


def _sc_workers():
    info = plsc.get_sparse_core_info()
    mesh = plsc.VectorSubcoreMesh(core_axis_name="core", subcore_axis_name="subcore")
    return mesh, info.num_cores, info.num_subcores


def _pack_weight_rows(w, col_tile):
    r, c = w.shape
    mesh, nc, ns = _sc_workers()
    rs, ncol = SC_PACK_ROWS, c // col_tile
    per_w = (r // rs) * ncol // (nc * ns)
    assert per_w * nc * ns * rs * col_tile == r * c and per_w % 2 == 0 and col_tile % SC_LANES == 0
    params = pltpu.CompilerParams()
    if "needs_layout_passes" in pltpu.CompilerParams.__dataclass_fields__:
        params = dataclasses.replace(params, needs_layout_passes=False)

    @functools.partial(
        pl.kernel, mesh=mesh, out_type=jax.ShapeDtypeStruct((r // 2, c), jnp.uint32), compiler_params=params,
        cost_estimate=pl.CostEstimate(flops=r * c, transcendentals=0, bytes_accessed=6 * r * c),
        scratch_types=[pltpu.VMEM((2, rs, col_tile), F32), pltpu.VMEM((2, rs // 2, col_tile), jnp.uint32),
                       pltpu.SemaphoreType.DMA((2,)), pltpu.SemaphoreType.DMA((2,))])
    def pack(w_hbm, o_hbm, in_v, out_v, rsem, wsem):
        wid = lax.axis_index("subcore") * nc + lax.axis_index("core")

        @pl.loop(0, per_w // 2)
        def _(it):
            reads, writes = [], []
            for b in range(2):
                tile = wid * per_w + 2 * it + b
                r0 = pl.multiple_of((tile // ncol) * rs, rs)
                c0 = pl.multiple_of((tile % ncol) * col_tile, col_tile)
                reads.append(pltpu.make_async_copy(w_hbm.at[pl.ds(r0, rs), pl.ds(c0, col_tile)], in_v.at[b],
                                                   rsem.at[b]))
                writes.append(pltpu.make_async_copy(
                    out_v.at[b], o_hbm.at[pl.ds(pl.multiple_of(r0 // 2, rs // 2), rs // 2), pl.ds(c0, col_tile)],
                    wsem.at[b]))
            reads[0].start()
            reads[1].start()
            for b in range(2):
                reads[b].wait()
                for pair in range(rs // 2):
                    @plsc.parallel_loop(0, col_tile, step=SC_LANES, unroll=8)
                    def _(j):
                        packed = plsc.pack(in_v[b, 2 * pair, pl.ds(j, SC_LANES)],
                                           in_v[b, 2 * pair + 1, pl.ds(j, SC_LANES)],
                                           format=plsc.PackFormat.INTERLEAVED)
                        out_v[b, pair, pl.ds(j, SC_LANES)] = plsc.bitcast(packed, jnp.uint32)
                writes[b].start()
            writes[0].wait()
            writes[1].wait()

    return pack(w)


def _sc_token_rows(t, nc, ns, ch):
    per_w = t // (nc * ns)
    assert per_w * nc * ns == t and per_w % (2 * ch) == 0
    return per_w, per_w // ch


def _scatter_rows(table, dests, p_rows, after=()):
    t, w = table.shape
    nk = len(dests)
    mesh, nc, ns = _sc_workers()
    ch = SC_SCATTER_ROWS
    per_w, nit = _sc_token_rows(t, nc, ns, ch)

    @functools.partial(
        pl.kernel, mesh=mesh, out_type=jax.ShapeDtypeStruct((p_rows, w), table.dtype),
        scratch_types=[pltpu.VMEM((nk, nit, ch), jnp.int32), pltpu.VMEM((2, ch, w), table.dtype),
                       pltpu.SemaphoreType.DMA((2,)), pltpu.SemaphoreType.DMA((2, nk))])
    def scatter(table_hbm, *refs):
        dest_hbm, out_hbm = refs[:nk], refs[nk + len(after)]
        idx_v, rows_v, rsem, wsem = refs[nk + len(after) + 1:]
        wid = lax.axis_index("subcore") * nc + lax.axis_index("core")
        for k in range(nk):
            pltpu.sync_copy(dest_hbm[k].at[pl.ds(wid * nit, nit)], idx_v.at[k])

        @pl.loop(0, nit // 2)
        def _(it):
            reads = [pltpu.make_async_copy(table_hbm.at[pl.ds(wid * per_w + (2 * it + b) * ch, ch)],
                                           rows_v.at[b], rsem.at[b]) for b in range(2)]
            writes = [[pltpu.make_async_copy(rows_v.at[b], out_hbm.at[idx_v.at[k].at[2 * it + b]], wsem.at[b, k])
                       for k in range(nk)] for b in range(2)]
            reads[0].start()
            reads[1].start()
            for b in range(2):
                reads[b].wait()
                for k in range(nk):
                    writes[b][k].start()
            for b in range(2):
                for k in range(nk):
                    writes[b][k].wait()

    return scatter(table, *[d.reshape(t // ch, ch) for d in dests], *after)


def _gather_rows(table, idxs):
    t = idxs[0].shape[0]
    w = table.shape[1]
    nk = len(idxs)
    mesh, nc, ns = _sc_workers()
    ch = SC_ROWS
    per_w, nit = _sc_token_rows(t, nc, ns, ch)

    @functools.partial(
        pl.kernel, mesh=mesh, out_type=[jax.ShapeDtypeStruct((t, w), table.dtype)] * nk,
        scratch_types=[pltpu.VMEM((nk, nit, ch), jnp.int32), pltpu.VMEM((nk, 2, ch, w), table.dtype),
                       pltpu.SemaphoreType.DMA((nk, 2)), pltpu.SemaphoreType.DMA((nk, 2))])
    def gather(table_hbm, *refs):
        idx_hbm, out_hbm = refs[:nk], refs[nk:2 * nk]
        idx_v, rows_v, rsem, wsem = refs[2 * nk:]
        wid = lax.axis_index("subcore") * nc + lax.axis_index("core")
        for k in range(nk):
            pltpu.sync_copy(idx_hbm[k].at[pl.ds(wid * nit, nit)], idx_v.at[k])

        @pl.loop(0, nit // 2)
        def _(it):
            slots = [(k, b) for k in range(nk) for b in range(2)]
            reads = {(k, b): pltpu.make_async_copy(table_hbm.at[idx_v.at[k].at[2 * it + b]], rows_v.at[k, b],
                                                   rsem.at[k, b]) for k, b in slots}
            writes = {(k, b): pltpu.make_async_copy(rows_v.at[k, b],
                                                    out_hbm[k].at[pl.ds(wid * per_w + (2 * it + b) * ch, ch)],
                                                    wsem.at[k, b]) for k, b in slots}
            for s in slots:
                reads[s].start()
            for s in slots:
                reads[s].wait()
                writes[s].start()
            for s in slots:
                writes[s].wait()

    return gather(table, *[i.reshape(t // ch, ch) for i in idxs])


def _combine_kernel(x_ref, y0_ref, y1_ref, gate_ref, *rest):
    o_ref = rest[-1]
    half = x_ref.shape[1] // 2
    for part in range(2):
        cols = slice(part * half, (part + 1) * half)
        acc = x_ref[:, cols]
        for k, y_ref in enumerate((y0_ref, y1_ref)):
            word = y_ref[...]
            bits = (word << 16) if part == 0 else (word & jnp.uint32(0xFFFF0000))
            acc = acc + gate_ref[:, k:k + 1] * lax.bitcast_convert_type(bits, F32)
        o_ref[:, cols] = acc


def _combine(x2, y0, y1, gates, part, earlier=None):
    t, d = x2.shape
    tm = ROW_TILE
    steps = y0.shape[0] // tm
    here = lambda i: (part * steps + i, 0)
    in_specs = [pl.BlockSpec((tm, d), here),
                pl.BlockSpec((tm, d // 2), lambda i: (i, 0)),
                pl.BlockSpec((tm, d // 2), lambda i: (i, 0)),
                pl.BlockSpec((tm, TOP_K), here)]
    operands = [x2, y0, y1, gates]
    if earlier is not None:
        in_specs.append(pl.BlockSpec(memory_space=pl.ANY))
        operands.append(earlier)
    return pl.pallas_call(
        _combine_kernel,
        out_shape=jax.ShapeDtypeStruct((t, d), F32),
        grid=(steps,),
        in_specs=in_specs,
        out_specs=pl.BlockSpec((tm, d), here),
        input_output_aliases={} if earlier is None else {len(operands) - 1: 0},
        compiler_params=_params("parallel"),
        name="moe_combine",
    )(*operands)


def _moe(x2, gain, router, wg, wu, wd):
    t, d = x2.shape
    hp, idx, gate, count = _router(x2, gain, router)
    experts = jnp.arange(N_EXPERTS, dtype=jnp.int32)
    counts = count[0, :N_EXPERTS]
    padded = (counts + MOE_ROWS - 1) // MOE_ROWS * MOE_ROWS
    pend = jnp.cumsum(padded)
    pstart = pend - padded
    expert, rank = idx[:, :TOP_K], idx[:, TOP_K:2 * TOP_K]
    dest = jnp.sum(jnp.where(expert[:, :, None] == experts, pstart, 0), axis=-1) + rank
    dests = [dest[:, k] for k in range(TOP_K)]
    p_rows = t * TOP_K + N_EXPERTS * MOE_ROWS
    blk_row = jnp.arange(p_rows // MOE_ROWS, dtype=jnp.int32) * MOE_ROWS
    blk_expert = jnp.minimum(jnp.sum(pend[None, :] <= blk_row[:, None], axis=1), N_EXPERTS - 1).astype(jnp.int32)
    valid = jnp.clip(counts[blk_expert] - (blk_row - pstart[blk_expert]), 0, MOE_ROWS)
    valid = jnp.where(blk_row < pend[-1], valid, 0).astype(jnp.int32)
    n_exp, _, f = wg.shape
    pack = lambda w, col_tile: _pack_weight_rows(w.reshape(-1, w.shape[2]), col_tile).reshape(n_exp, -1, w.shape[2])
    wg, wu, wd = pack(wg, f // 2), pack(wu, f // 2), pack(wd, d)
    hb = _scatter_rows(hp, dests, p_rows, after=(wg, wu, wd))
    yb = _ffn_experts(hb, wg, wu, wd, blk_expert, valid)
    out = None
    for part in range(MOE_TAIL_PARTS):
        rows = slice(part * t // MOE_TAIL_PARTS, (part + 1) * t // MOE_TAIL_PARTS)
        y0, y1 = _gather_rows(yb, [dk[rows] for dk in dests])
        out = _combine(x2, y0, y1, gate[:, :TOP_K], part, out)
    return out


def kernel(x, mem, rel_bias, norm_mix, norm_mem, norm_ffn, w_mem_kv, xq_norm, xk_norm, w_out, hy_w_in, hy_conv_w, hy_conv_b, hy_filt_w1, hy_filt_b1, hy_filt_w2, hy_filt_b2, hy_filt_w3, hy_sin_freq, hy_skip, at_w_in, at_q_norm, at_k_norm, ffn_w_gate, ffn_w_up, ffn_w_down, moe_router, moe_w_gate, moe_w_up, moe_w_down):
    b, s, d = x.shape
    t = b * s
    m_len = mem.shape[1]
    x2 = x.reshape(t, d)
    mem2 = mem.reshape(b * m_len, d)
    bf = lambda w: w.astype(BF16)
    score_scale = HD ** -0.5
    xa_heads = XA_W // HD
    at_heads = AT_W // HD
    plain = lambda width: [None] * (width // HD)
    kv_gains = lambda i: [xk_norm[i]] * xa_heads + plain(XA_W)

    tables = _dft_tables(s)
    spectra, hq = _hyena_filters(s, tables, hy_filt_w1[0], hy_filt_b1[0], hy_filt_w2[0], hy_filt_b2[0],
                                 hy_filt_w3[0], hy_sin_freq[0])
    z = _norm_matmul(x2, norm_mix[0], bf(hy_w_in[0]), 1024,
                     plain(3 * HY_C) + [xq_norm[0] * score_scale] * xa_heads).reshape(b, s, -1)
    kv = _norm_matmul(mem2, norm_mem[0], bf(w_mem_kv[0]), 1024, kv_gains(0)).reshape(b, m_len, -1)
    self_out = _hyena(z, hy_conv_w[0], hy_conv_b[0], tables, spectra, hq, hy_skip[0])
    x2 = _mix_out(x2.reshape(b, s, d), self_out, z, 3 * HY_C // XA_W, kv, bf(w_out[0])).reshape(t, d)
    x2 = _ffn_dense(x2, norm_ffn[0], bf(ffn_w_gate[0]), bf(ffn_w_up[0]), bf(ffn_w_down[0]))

    at_gains = ([at_q_norm[0] * score_scale] * at_heads + [at_k_norm[0]] * at_heads + plain(AT_W)
                + [xq_norm[1] * score_scale] * xa_heads)
    z = _norm_matmul(x2, norm_mix[1], bf(at_w_in[0]), 1024, at_gains).reshape(b, s, -1)
    kv = _norm_matmul(mem2, norm_mem[1], bf(w_mem_kv[1]), 1024, kv_gains(1)).reshape(b, m_len, -1)
    self_out = _dilated_attention(z, rel_bias)
    x2 = _mix_out(x2.reshape(b, s, d), self_out, z, 3 * AT_W // XA_W, kv, bf(w_out[1])).reshape(t, d)
    x2 = _moe(x2, norm_ffn[1], moe_router[0], moe_w_gate[0], moe_w_up[0], moe_w_down[0])
    return x2.reshape(b, s, d)
```

```python
import dataclasses
import functools
import math

import jax
import jax.numpy as jnp
import numpy as np
from jax import lax
from jax.experimental import pallas as pl
from jax.experimental.pallas import tpu as pltpu
from jax.experimental.pallas import tpu_sc as plsc

F32 = jnp.float32
BF16 = jnp.bfloat16

D_MODEL = 1024
EPS = 1e-6
HY_C = 512
FILT_BANDS = 16
DECAY_TARGET = 1e-2
FAST_DECAY_PCT = 0.3
SLOW_DECAY_PCT = 1.5
MOD_SHIFT = 0.05
AT_GROUPS = ((128, 1), (512, 4), (2048, 16))
AT_HEADS = 4
HD = 128
AT_W = 1536
NUM_BUCKETS = 32
REL_MAX_DIST = 1024
NEG_INF = -1e30
XA_W = 512
D_FF = 2816
N_EXPERTS = 8
TOP_K = 2

VMEM_LIMIT_BYTES = 56 * 1024 * 1024
ROW_TILE = 1024
PROJ_ROW_TILE = 1024
FF_TILE = 256
MOE_ROWS = 512
MOE_CHUNK = 512
MOE_TAIL_PARTS = 2
SC_ROWS = 32
SC_SCATTER_ROWS = 64
SC_LANES = 16
SC_PACK_ROWS = 16
ATT_TQ = 128
ATT_KW = 256
HY_TC = 256
HY_KT = 512


def _params(*sem):
    return pltpu.CompilerParams(dimension_semantics=sem, vmem_limit_bytes=VMEM_LIMIT_BYTES)


def _rms_rows(x, gain):
    return x * lax.rsqrt(jnp.mean(x * x, axis=-1, keepdims=True) + EPS) * gain


def _bf16_bits(v):
    u = lax.bitcast_convert_type(v, jnp.uint32)
    return (u + jnp.uint32(0x7FFF) + ((u >> 16) & jnp.uint32(1))) >> 16


def _resident(shape, index_map):
    return pl.BlockSpec(shape, index_map, pipeline_mode=pl.Buffered(1))


def _norm_matmul_kernel(x_ref, g_ref, w_ref, hg_ref, o_ref, *, tn, head_norm):
    h = _rms_rows(x_ref[...], g_ref[...]).astype(BF16)
    for c in range(o_ref.shape[1] // tn):
        acc = jnp.dot(h, w_ref[:, c * tn:(c + 1) * tn], preferred_element_type=F32)
        for j in range(tn // HD):
            cols = slice(c * tn + j * HD, c * tn + (j + 1) * HD)
            seg = acc[:, j * HD:(j + 1) * HD]
            if head_norm[cols.start // HD]:
                seg = _rms_rows(seg, hg_ref[:, cols])
            o_ref[:, cols] = seg.astype(o_ref.dtype)


def _norm_matmul(x2, gain, w_bf, tn, head_gains):
    rows, d = x2.shape
    n = w_bf.shape[1]
    tm = min(PROJ_ROW_TILE, rows)
    assert len(head_gains) * HD == n
    hg = jnp.concatenate([jnp.ones((HD,), F32) if g is None else g.astype(F32) for g in head_gains]).reshape(1, n)
    return pl.pallas_call(
        functools.partial(_norm_matmul_kernel, tn=tn, head_norm=tuple(g is not None for g in head_gains)),
        out_shape=jax.ShapeDtypeStruct((rows, n), BF16),
        grid=(rows // tm,),
        in_specs=[pl.BlockSpec((tm, d), lambda i: (i, 0)),
                  pl.BlockSpec((1, d), lambda i: (0, 0)),
                  _resident((d, n), lambda i: (0, 0)),
                  pl.BlockSpec((1, n), lambda i: (0, 0))],
        out_specs=pl.BlockSpec((tm, n), lambda i: (i, 0)),
        compiler_params=_params("parallel"),
        cost_estimate=pl.CostEstimate(flops=2 * rows * d * n, transcendentals=rows,
                                      bytes_accessed=4 * rows * d + 2 * d * n + 2 * rows * n),
        name="norm_matmul",
    )(x2, gain.reshape(1, d), w_bf, hg)


def _mix_out_kernel(x_ref, a_ref, xq_ref, kv_ref, wa_ref, wc_ref, o_ref):
    heads = []
    for h in range(XA_W // HD):
        cols = slice(h * HD, (h + 1) * HD)
        s = lax.dot_general(xq_ref[0, :, cols], kv_ref[0, :, cols], (((1,), (1,)), ((), ())),
                            preferred_element_type=F32)
        p = jnp.exp(s - jnp.max(s, axis=-1, keepdims=True))
        z = jnp.sum(p, axis=-1, keepdims=True)
        pv = jnp.dot(p.astype(BF16), kv_ref[0, :, XA_W + h * HD:XA_W + (h + 1) * HD], preferred_element_type=F32)
        heads.append((pv / z).astype(BF16))
    cross = jnp.concatenate(heads, axis=1)
    o_ref[0] = (x_ref[0]
                + jnp.dot(a_ref[0], wa_ref[...], preferred_element_type=F32)
                + jnp.dot(cross, wc_ref[...], preferred_element_type=F32))


def _mix_out(x3, self_out, z3, xq_block, kv3, w_bf):
    b, s, d = x3.shape
    half = self_out.shape[2]
    m = kv3.shape[1]
    tm = ROW_TILE
    return pl.pallas_call(
        _mix_out_kernel,
        out_shape=jax.ShapeDtypeStruct((b, s, d), F32),
        grid=(b, s // tm),
        in_specs=[pl.BlockSpec((1, tm, d), lambda i, j: (i, j, 0)),
                  pl.BlockSpec((1, tm, half), lambda i, j: (i, j, 0)),
                  pl.BlockSpec((1, tm, XA_W), lambda i, j: (i, j, xq_block)),
                  pl.BlockSpec((1, m, 2 * XA_W), lambda i, j: (i, 0, 0)),
                  pl.BlockSpec((half, d), lambda i, j: (0, 0)),
                  pl.BlockSpec((d - half, d), lambda i, j: (1, 0))],
        out_specs=pl.BlockSpec((1, tm, d), lambda i, j: (i, j, 0)),
        compiler_params=_params("parallel", "parallel"),
        name="mix_out",
    )(x3, self_out, z3, kv3, w_bf, w_bf)


def _weight_rows(ref, start, size):
    if ref.dtype == jnp.uint32:
        return lambda cols: pltpu.bitcast(ref[start // 2:(start + size) // 2, cols], BF16)
    return lambda cols: ref[start:start + size, cols]


def _weight_shape(ref):
    return (ref.shape[0] * (2 if ref.dtype == jnp.uint32 else 1), ref.shape[1])


def _swiglu(h, wg_ref, wu_ref, wd_ref, tf):
    d, f = _weight_shape(wg_ref)
    assert f % tf == 0
    y = None
    for j in range(f // tf):
        cols = slice(j * tf, (j + 1) * tf)
        gg = jnp.dot(h, _weight_rows(wg_ref, 0, d)(cols), preferred_element_type=F32)
        uu = jnp.dot(h, _weight_rows(wu_ref, 0, d)(cols), preferred_element_type=F32)
        a = ((gg * jax.nn.sigmoid(gg)) * uu).astype(BF16)
        part = jnp.dot(a, _weight_rows(wd_ref, j * tf, tf)(slice(None)), preferred_element_type=F32)
        y = part if y is None else y + part
    return y


def _ffn_dense_kernel(x_ref, g_ref, wg_ref, wu_ref, wd_ref, o_ref, *, tf):
    x = x_ref[...]
    h = _rms_rows(x, g_ref[...]).astype(BF16)
    o_ref[...] = x + _swiglu(h, wg_ref, wu_ref, wd_ref, tf)


def _ffn_dense(x2, gain, wg, wu, wd):
    rows, d = x2.shape
    f = wg.shape[1]
    tm = ROW_TILE
    return pl.pallas_call(
        functools.partial(_ffn_dense_kernel, tf=FF_TILE),
        out_shape=jax.ShapeDtypeStruct((rows, d), F32),
        grid=(rows // tm,),
        in_specs=[pl.BlockSpec((tm, d), lambda i: (i, 0)),
                  pl.BlockSpec((1, d), lambda i: (0, 0)),
                  _resident((d, f), lambda i: (0, 0)),
                  _resident((d, f), lambda i: (0, 0)),
                  _resident((f, d), lambda i: (0, 0))],
        out_specs=pl.BlockSpec((tm, d), lambda i: (i, 0)),
        compiler_params=_params("parallel"),
        cost_estimate=pl.CostEstimate(flops=6 * rows * d * f, transcendentals=rows * f,
                                      bytes_accessed=8 * rows * d + 6 * d * f),
        name="ffn_dense",
    )(x2, gain.reshape(1, d), wg, wu, wd)


def _ffn_expert_kernel(eid_ref, valid_ref, hp_ref, wg_ref, wu_ref, wd_ref, o_ref, h_ref, *, tf):
    half = hp_ref.shape[1]
    valid = valid_ref[pl.program_id(0)]

    @pl.when(valid > 0)
    def _():
        keep = lax.broadcasted_iota(jnp.int32, (hp_ref.shape[0], 1), 0) < valid
        word = hp_ref[...]
        h_ref[:, :half] = jnp.where(keep, lax.bitcast_convert_type(word << 16, F32), 0.0).astype(BF16)
        h_ref[:, half:] = jnp.where(keep, lax.bitcast_convert_type(word & jnp.uint32(0xFFFF0000), F32),
                                    0.0).astype(BF16)
        y = _swiglu(h_ref[...], wg_ref.at[0], wu_ref.at[0], wd_ref.at[0], tf)
        o_ref[...] = _bf16_bits(y[:, :half]) | (_bf16_bits(y[:, half:]) << 16)

    @pl.when(valid <= 0)
    def _():
        o_ref[...] = jnp.zeros_like(o_ref)


def _ffn_experts(hp, wg, wu, wd, eid, valid):
    rows, half = hp.shape
    d = 2 * half
    tm = MOE_ROWS
    expert_spec = lambda w: pl.BlockSpec((1,) + w.shape[1:], lambda i, e, n: (e[i], 0, 0))
    grid_spec = pltpu.PrefetchScalarGridSpec(
        num_scalar_prefetch=2,
        grid=(rows // tm,),
        in_specs=[pl.BlockSpec((tm, half), lambda i, e, n: (i, 0)),
                  expert_spec(wg), expert_spec(wu), expert_spec(wd)],
        out_specs=pl.BlockSpec((tm, half), lambda i, e, n: (i, 0)),
        scratch_shapes=[pltpu.VMEM((tm, d), BF16)],
    )
    return pl.pallas_call(
        functools.partial(_ffn_expert_kernel, tf=FF_TILE),
        out_shape=jax.ShapeDtypeStruct((rows, half), jnp.uint32),
        grid_spec=grid_spec,
        compiler_params=_params("arbitrary"),
        name="ffn_experts",
    )(eid, valid, hp, wg, wu, wd)


def _dft_kernel(ce_ref, se_ref, co_ref, so_ref, cot_ref, sot_ref, base_ref, *, n_fft):
    rows, cols = ce_ref.shape
    i = pl.program_id(0)
    theta = 2.0 * math.pi / n_fft

    def phases(r, c):
        return r * (2 * c), r * (2 * c + 1), c * (2 * r + 1)

    @pl.when(i == 0)
    def _():
        r = lax.broadcasted_iota(jnp.int32, (rows, cols), 0)
        c = lax.broadcasted_iota(jnp.int32, (rows, cols), 1)
        for f, ph in enumerate(phases(r, c)):
            ang = (ph & (n_fft - 1)).astype(F32) * theta
            base_ref[2 * f] = jnp.cos(ang)
            base_ref[2 * f + 1] = jnp.sin(ang)

    c = lax.broadcasted_iota(jnp.int32, (8, cols), 1)
    r0 = i * rows
    shifts = (r0 * (2 * c), r0 * (2 * c + 1), c * (2 * r0))
    for f, (c_ref, s_ref) in enumerate(((ce_ref, se_ref), (co_ref, so_ref), (cot_ref, sot_ref))):
        ang = (shifts[f] & (n_fft - 1)).astype(F32) * theta
        ca, sa = jnp.cos(ang)[0:1], jnp.sin(ang)[0:1]
        cb, sb = base_ref[2 * f], base_ref[2 * f + 1]
        c_ref[...] = (cb * ca - sb * sa).astype(BF16)
        s_ref[...] = (sb * ca + cb * sa).astype(BF16)


def _dft_tables(length):
    half, rows = length // 2, 128
    shape = jax.ShapeDtypeStruct((half, half), BF16)
    spec = pl.BlockSpec((rows, half), lambda i: (i, 0))
    return pl.pallas_call(
        functools.partial(_dft_kernel, n_fft=2 * length),
        out_shape=(shape,) * 6,
        grid=(half // rows,),
        out_specs=(spec,) * 6,
        scratch_shapes=[pltpu.VMEM((6, rows, half), F32)],
        compiler_params=_params("arbitrary"),
        name="dft_tables",
    )()


LANES = 128


def _split_parity(tmp_ref, x):
    half = x.shape[0] // 2
    for j in range(tmp_ref.shape[0]):
        tmp_ref[j] = x[:, j * LANES:(j + 1) * LANES]
    pick = lambda start: jnp.concatenate(
        [tmp_ref[j, pl.ds(start, half, stride=2), :] for j in range(tmp_ref.shape[0])], axis=1)
    return pick(0), pick(1)


def _merge_parity(tmp_ref, even, odd):
    half = even.shape[0]
    for j in range(tmp_ref.shape[0]):
        tmp_ref[j, pl.ds(0, half, stride=2), :] = even[:, j * LANES:(j + 1) * LANES]
        tmp_ref[j, pl.ds(1, half, stride=2), :] = odd[:, j * LANES:(j + 1) * LANES]
    return jnp.concatenate([tmp_ref[j] for j in range(tmp_ref.shape[0])], axis=1)


def _filter_time_kernel(feats_ref, w1_ref, b1_ref, w2_ref, b2_ref, fr_ref, w3_ref, t_ref, delta_ref,
                        g_ref, hq_ref, tmp_ref, h_ref):
    hp = lax.Precision.HIGHEST
    length = feats_ref.shape[0]
    half = length // 2

    @pl.when(pl.program_id(0) == 0)
    def _():
        fr = fr_ref[...]
        h1 = jnp.sin(fr * (jnp.dot(feats_ref[...], w1_ref[...], preferred_element_type=F32, precision=hp)
                           + b1_ref[...]))
        h_ref[...] = jnp.sin(fr * (jnp.dot(h1, w2_ref[...], preferred_element_type=F32, precision=hp) + b2_ref[...]))

    h = h_ref[...]
    mod =jnp.exp(-t_ref[...] * delta_ref[...]) + MOD_SHIFT
    row = lax.broadcasted_iota(jnp.int32, (length, 1), 0)
    alt = jnp.where(lax.broadcasted_iota(jnp.int32, (half, 1), 0) % 2 == 0, 1.0, -1.0).astype(F32)
    for o in range(2):
        fwd = jnp.dot(h, w3_ref[2 * o], preferred_element_type=F32, precision=hp) * mod
        bwd = jnp.dot(h, w3_ref[2 * o + 1], preferred_element_type=F32, precision=hp) * mod
        bwd = jnp.where(row == 0, 0.0, bwd)
        norm = (jnp.sum(jnp.abs(fwd), axis=0, keepdims=True)
                + jnp.sum(jnp.abs(bwd), axis=0, keepdims=True) + 1e-6)
        fwd = fwd / norm
        bwd = bwd / norm
        for part, g in enumerate((fwd + bwd, fwd - bwd)):
            even, odd = _split_parity(tmp_ref, g)
            g_ref[o, 2 * part] = even.astype(BF16)
            g_ref[o, 2 * part + 1] = odd.astype(BF16)
            mid = jnp.sum((even if part == 0 else odd) * alt, axis=0, keepdims=True) * (1.0 / length)
            hq_ref[o, part] = mid if part == 0 else -mid


def _filter_freq_kernel(ce_ref, se_ref, co_ref, so_ref, g_ref, h_ref):
    half = ce_ref.shape[0]
    row = lax.broadcasted_iota(jnp.int32, (half, 1), 0)
    scale = jnp.where(row == 0, 1.0, 2.0).astype(F32) * (1.0 / (4 * half))
    a = jnp.dot(ce_ref[...], g_ref[0, 0], preferred_element_type=F32)
    b = jnp.dot(co_ref[...], g_ref[0, 1], preferred_element_type=F32)
    c = jnp.dot(se_ref[...], g_ref[0, 2], preferred_element_type=F32)
    d = jnp.dot(so_ref[...], g_ref[0, 3], preferred_element_type=F32)
    h_ref[0, 0] = (a + b) * scale
    h_ref[0, 1] = -(c + d) * scale
    h_ref[0, 2] = (a - b) * scale
    h_ref[0, 3] = (c - d) * scale


def _hyena_filters(length, tables, w1, b1, w2, b2, w3, freq):
    t = np.linspace(0.0, 1.0, length, dtype=np.float32)[:, None]
    f = np.linspace(1e-4, FILT_BANDS - 1, FILT_BANDS, dtype=np.float32)[None]
    ang = np.float32(2.0 * math.pi / length) * np.arange(length, dtype=np.float32)[:, None] * f
    feats = np.concatenate([t, np.cos(ang), -np.sin(ang)], axis=-1).astype(np.float32)
    deltas = np.abs(np.linspace(math.log(DECAY_TARGET) / SLOW_DECAY_PCT,
                                math.log(DECAY_TARGET) / FAST_DECAY_PCT, HY_C, dtype=np.float32))[None]
    hid = w1.shape[1]
    w3r = w3.reshape(hid, 4, HY_C).transpose(1, 0, 2)
    tc = HY_TC
    full = lambda shape: pl.BlockSpec(shape, lambda c: (0,) * len(shape))
    half = length // 2
    g, hq = pl.pallas_call(
        _filter_time_kernel,
        out_shape=(jax.ShapeDtypeStruct((2, 4, half, HY_C), BF16),
                   jax.ShapeDtypeStruct((2, 2, 1, HY_C), F32)),
        grid=(HY_C // tc,),
        in_specs=[full(feats.shape), full(w1.shape), full((1, hid)), full(w2.shape), full((1, hid)),
                  full((1, hid)), pl.BlockSpec((4, hid, tc), lambda c: (0, 0, c)), full((length, 1)),
                  pl.BlockSpec((1, tc), lambda c: (0, c))],
        out_specs=(pl.BlockSpec((2, 4, half, tc), lambda c: (0, 0, 0, c)),
                   pl.BlockSpec((2, 2, 1, tc), lambda c: (0, 0, 0, c))),
        scratch_shapes=[pltpu.VMEM((tc // LANES, length, LANES), F32), pltpu.VMEM((length, hid), F32)],
        compiler_params=_params("arbitrary"),
        name="filter_time",
    )(feats, w1, b1.reshape(1, hid), w2, b2.reshape(1, hid), freq.reshape(1, hid), w3r, t, deltas)
    ce, se, co, so = tables[:4]
    table_spec = _resident((half, half), lambda o, c: (0, 0))
    spectra = pl.pallas_call(
        _filter_freq_kernel,
        out_shape=jax.ShapeDtypeStruct((2, 4, half, HY_C), F32),
        grid=(2, HY_C // tc),
        in_specs=[table_spec] * 4 + [pl.BlockSpec((1, 4, half, tc), lambda o, c: (o, 0, 0, c))],
        out_specs=pl.BlockSpec((1, 4, half, tc), lambda o, c: (o, 0, 0, c)),
        compiler_params=_params("parallel", "parallel"),
        name="filter_freq",
    )(ce, se, co, so, g)
    return spectra, hq


def _hyena_kernel(z0_ref, z1_ref, z2_ref, cw_ref, cb_ref, ce_ref, se_ref, co_ref, so_ref, cot_ref, sot_ref,
                  h_ref, hq_ref, skip_ref, o_ref, tmp_ref):
    length = z0_ref.shape[1]
    half = length // 2
    row = lax.broadcasted_iota(jnp.int32, (length, 1), 0)
    alt = jnp.where(lax.broadcasted_iota(jnp.int32, (half, 1), 0) % 2 == 0, 1.0, -1.0).astype(F32)

    def short_conv(z_ref, c):
        u = z_ref[0].astype(F32)
        prev = jnp.where(row == 0, 0.0, pltpu.roll(u, 1, axis=0))
        nxt = jnp.where(row == length - 1, 0.0, pltpu.roll(u, length - 1, axis=0))
        return cb_ref[c] + (prev * cw_ref[0, c] + u * cw_ref[1, c] + nxt * cw_ref[2, c])

    def long_conv(u, o):
        ue, uo = _split_parity(tmp_ref, u)
        ue_bf, uo_bf = ue.astype(BF16), uo.astype(BF16)
        a4 = jnp.sum(ue * alt, axis=0, keepdims=True)
        b4 = jnp.sum(uo * alt, axis=0, keepdims=True)
        hr4, hi4 = hq_ref[o, 0], hq_ref[o, 1]
        y_even = alt * (a4 * hr4 + b4 * hi4)
        y_odd = -(alt * (a4 * hi4 - b4 * hr4))
        for kt in range(half // HY_KT):
            ks = slice(kt * HY_KT, (kt + 1) * HY_KT)
            ae = jnp.dot(ce_ref[ks, :], ue_bf, preferred_element_type=F32)
            ao = jnp.dot(co_ref[ks, :], uo_bf, preferred_element_type=F32)
            be = jnp.dot(se_ref[ks, :], ue_bf, preferred_element_type=F32)
            bo = jnp.dot(so_ref[ks, :], uo_bf, preferred_element_type=F32)
            a_f, a_g, b_f, b_g = ae + ao, ae - ao, be + bo, bo - be
            hr_f, hi_f, hr_g, hi_g = (h_ref[o, j, ks, :] for j in range(4))
            re_f, im_f = a_f * hr_f + b_f * hi_f, a_f * hi_f - b_f * hr_f
            re_g, im_g = a_g * hr_g + b_g * hi_g, a_g * hi_g - b_g * hr_g
            y_even += (jnp.dot(ce_ref[:, ks], (re_f + re_g).astype(BF16), preferred_element_type=F32)
                       - jnp.dot(se_ref[:, ks], (im_f - im_g).astype(BF16), preferred_element_type=F32))
            y_odd += (jnp.dot(cot_ref[:, ks], (re_f - re_g).astype(BF16), preferred_element_type=F32)
                      - jnp.dot(sot_ref[:, ks], (im_f + im_g).astype(BF16), preferred_element_type=F32))
        return _merge_parity(tmp_ref, y_even, y_odd) + u * skip_ref[o]

    z = short_conv(z1_ref, 1) * long_conv(short_conv(z0_ref, 0), 0)
    o_ref[0] = (short_conv(z2_ref, 2) * long_conv(z, 1)).astype(o_ref.dtype)


def _hyena(z3, conv_w, conv_b, tables, spectra, hq, skip):
    b, s, _ = z3.shape
    half = s // 2
    tc = HY_TC
    nct = HY_C // tc
    cw = conv_w.reshape(3, 3, 1, HY_C)
    cb = conv_b.reshape(3, 1, HY_C)
    zspec = lambda chunk: pl.BlockSpec((1, s, tc), lambda c, i: (i, 0, chunk * nct + c))
    return pl.pallas_call(
        _hyena_kernel,
        out_shape=jax.ShapeDtypeStruct((b, s, HY_C), BF16),
        grid=(nct, b),
        in_specs=[zspec(0), zspec(1), zspec(2),
                  pl.BlockSpec((3, 3, 1, tc), lambda c, i: (0, 0, 0, c)),
                  pl.BlockSpec((3, 1, tc), lambda c, i: (0, 0, c))]
                 + [_resident((half, half), lambda c, i: (0, 0))] * 6
                 + [_resident((2, 4, half, tc), lambda c, i: (0, 0, 0, c)),
                    pl.BlockSpec((2, 2, 1, tc), lambda c, i: (0, 0, 0, c)),
                    pl.BlockSpec((2, 1, tc), lambda c, i: (0, 0, c))],
        out_specs=pl.BlockSpec((1, s, tc), lambda c, i: (i, 0, c)),
        scratch_shapes=[pltpu.VMEM((tc // LANES, s, LANES), F32)],
        compiler_params=_params("parallel", "parallel"),
        cost_estimate=pl.CostEstimate(flops=2 * 4 * 2 * b * s * half * HY_C, transcendentals=0,
                                      bytes_accessed=2 * 3 * b * s * HY_C + 4 * b * s * HY_C + 12 * half * half),
        name="hyena",
    )(z3, z3, z3, cw, cb, *tables, spectra, hq, skip.reshape(2, 1, HY_C))


def _rel_bucket(rel):
    half = NUM_BUCKETS // 2
    exact = half // 2
    n = np.abs(rel)
    large = exact + (np.log(np.maximum(n, 1) / exact) / np.log(REL_MAX_DIST / exact) * (half - exact)).astype(np.int32)
    large = np.minimum(large, half - 1)
    return (np.where(rel > 0, half, 0) + np.where(n < exact, n, large)).astype(np.int32)


def _att_tiles(length):
    kw = min(ATT_KW, length)
    tiles = []
    for qs in range(0, length, ATT_TQ):
        ks = min(max(qs - (kw - ATT_TQ) // 2, 0), length - kw)
        tiles.append((qs, ks, {0: 0, -64: 1, -128: 2}[ks - qs]))
    return kw, tiles


def _att_bias(rel_bias, group, seq):
    window, dil = AT_GROUPS[group]
    band = window // (2 * dil)
    kw, tiles = _att_tiles(seq // dil)
    offsets = sorted({ks - qs for qs, ks, _ in tiles}, reverse=True)
    table = rel_bias[:, group * AT_HEADS:(group + 1) * AT_HEADS].astype(F32)
    span = kw + ATT_TQ
    k = np.arange(span)
    rel = np.asarray(offsets)[:, None] + np.where(k < kw, k, k - span)[None, :]
    diag = jnp.where((np.abs(rel) <= band)[:, None, :],
                     jnp.swapaxes(table[_rel_bucket(rel * dil)], 1, 2), NEG_INF)
    return diag[:, :, None, :]


def _bias_tile(diag_ref, var, kw):
    rows = jnp.broadcast_to(diag_ref[var, 0], (ATT_TQ, diag_ref.shape[3]))
    return pltpu.roll(rows, 0, 1, stride=1, stride_axis=0)[:, :kw]


def _dilated_kernel(q1, q2, q3, k1, k2, k3, v1, v2, v3, b1, b2, b3, o_ref,
                    qn_ref, kn_ref, vn_ref, og_ref, lg_ref):
    seq = o_ref.shape[1]
    contract_last = (((1,), (1,)), ((), ()))
    for g, (q_ref, k_ref, v_ref, bias_ref) in enumerate(((q1, k1, v1, b1), (q2, k2, v2, b2), (q3, k3, v3, b3))):
        dil = AT_GROUPS[g][1]
        kw, tiles = _att_tiles(seq // dil)
        if dil > 1:
            qn_ref[...] = q_ref[0].astype(F32)
            kn_ref[...] = k_ref[0].astype(F32)
            vn_ref[...] = v_ref[0].astype(F32)
        bias = {var: _bias_tile(bias_ref, var, kw) for var in sorted({var for _, _, var in tiles})}
        for r in range(dil):
            for qs, ks, var in tiles:
                rows = lambda start, size: (pl.ds(r + start * dil, size, stride=dil) if dil > 1
                                            else pl.ds(start, size))
                if dil > 1:
                    qt = qn_ref[rows(qs, ATT_TQ), :].astype(BF16)
                    kt = kn_ref[rows(ks, kw), :].astype(BF16)
                    vt = vn_ref[rows(ks, kw), :].astype(BF16)
                else:
                    qt, kt, vt = q_ref[0, rows(qs, ATT_TQ), :], k_ref[0, rows(ks, kw), :], v_ref[0, rows(ks, kw), :]
                s = lax.dot_general(qt, kt, contract_last, preferred_element_type=F32) + bias[var]
                m = jnp.max(s, axis=-1, keepdims=True)
                p = jnp.exp(s - m)
                z = jnp.sum(p, axis=-1, keepdims=True)
                og_ref[g, rows(qs, ATT_TQ), :] = jnp.dot(p.astype(BF16), vt, preferred_element_type=F32) / z
                lg_ref[g, rows(qs, ATT_TQ), :] = jnp.broadcast_to(m + jnp.log(z), (ATT_TQ, HD))
    l0, l1, l2 = lg_ref[0], lg_ref[1], lg_ref[2]
    mx = jnp.maximum(jnp.maximum(l0, l1), l2)
    e0, e1, e2 = jnp.exp(l0 - mx), jnp.exp(l1 - mx), jnp.exp(l2 - mx)
    o_ref[0] = ((e0 * og_ref[0] + e1 * og_ref[1] + e2 * og_ref[2]) / (e0 + e1 + e2)).astype(o_ref.dtype)


def _dilated_attention(z3, rel_bias):
    b, s, _ = z3.shape
    ng = len(AT_GROUPS)
    col = lambda part, g: pl.BlockSpec((1, s, HD), lambda i, h: (i, 0, part * ng * AT_HEADS + g * AT_HEADS + h))
    biases = [_att_bias(rel_bias, g, s) for g in range(ng)]
    bias_spec = lambda a: pl.BlockSpec((a.shape[0], 1) + a.shape[2:], lambda i, h: (0, h, 0, 0))
    return pl.pallas_call(
        _dilated_kernel,
        out_shape=jax.ShapeDtypeStruct((b, s, AT_HEADS * HD), BF16),
        grid=(b, AT_HEADS),
        in_specs=[col(p, g) for p in range(3) for g in range(ng)] + [bias_spec(a) for a in biases],
        out_specs=pl.BlockSpec((1, s, HD), lambda i, h: (i, 0, h)),
        scratch_shapes=[pltpu.VMEM((s, HD), F32), pltpu.VMEM((s, HD), F32), pltpu.VMEM((s, HD), F32),
                        pltpu.VMEM((ng, s, HD), F32), pltpu.VMEM((ng, s, HD), F32)],
        compiler_params=_params("parallel", "parallel"),
        name="dilated_attention",
    )(*([z3] * 9), *biases)


ROUTER_LANES = 128


def _router_kernel(x_ref, g_ref, w_ref, hp_ref, idx_ref, gate_ref, count_ref, carry_ref):
    tm = x_ref.shape[0]
    half = hp_ref.shape[1]

    @pl.when(pl.program_id(0) == 0)
    def _():
        carry_ref[...] = jnp.zeros_like(carry_ref)

    h = _rms_rows(x_ref[...], g_ref[...])
    hp_ref[...] = _bf16_bits(h[:, :half]) | (_bf16_bits(h[:, half:]) << 16)
    h_hi = h.astype(BF16)
    h_lo = (h - h_hi.astype(F32)).astype(BF16)
    logits = (jnp.dot(h_hi, w_ref[0], preferred_element_type=F32)
              + (jnp.dot(h_hi, w_ref[1], preferred_element_type=F32)
                 + jnp.dot(h_lo, w_ref[0], preferred_element_type=F32)))
    lane = lax.broadcasted_iota(jnp.int32, logits.shape, 1).astype(F32)
    logits = jnp.where(lane < N_EXPERTS, logits, -jnp.inf)
    m1 = jnp.max(logits, axis=-1, keepdims=True)
    i1 = jnp.min(jnp.where(logits == m1, lane, float(ROUTER_LANES)), axis=-1, keepdims=True)
    rest = jnp.where(lane == i1, -jnp.inf, logits)
    m2 = jnp.max(rest, axis=-1, keepdims=True)
    i2 = jnp.min(jnp.where(rest == m2, lane, float(ROUTER_LANES)), axis=-1, keepdims=True)
    e2 = jnp.exp(m2 - m1)
    den = 1.0 + e2
    gate_ref[...] = jnp.where(lane == 0, 1.0 / den, e2 / den)
    chosen = jnp.where((lane == i1) | (lane == i2), 1.0, 0.0)
    earlier = lax.broadcasted_iota(jnp.int32, (tm, tm), 0) > lax.broadcasted_iota(jnp.int32, (tm, tm), 1)
    carry = carry_ref[...]
    before = jnp.dot(jnp.where(earlier, 1.0, 0.0).astype(BF16), chosen.astype(BF16),
                     preferred_element_type=F32) + carry
    r1 = jnp.sum(jnp.where(lane == i1, before, 0.0), axis=-1, keepdims=True)
    r2 = jnp.sum(jnp.where(lane == i2, before, 0.0), axis=-1, keepdims=True)
    idx_ref[...] = jnp.where(lane == 0, i1, jnp.where(lane == 1, i2, jnp.where(lane == 2, r1, r2))).astype(jnp.int32)
    carry = carry + jnp.sum(chosen, axis=0, keepdims=True)
    carry_ref[...] = carry
    count_ref[...] = carry.astype(jnp.int32)


def _router(x2, gain, router):
    rows, d = x2.shape
    tm = MOE_CHUNK
    w = jnp.zeros((d, ROUTER_LANES), F32).at[:, :N_EXPERTS].set(router.astype(F32))
    w_hi = w.astype(BF16)
    w = jnp.stack([w_hi, (w - w_hi.astype(F32)).astype(BF16)])
    row_spec = lambda width: pl.BlockSpec((tm, width), lambda i: (i, 0))
    return pl.pallas_call(
        _router_kernel,
        out_shape=(jax.ShapeDtypeStruct((rows, d // 2), jnp.uint32),
                   jax.ShapeDtypeStruct((rows, ROUTER_LANES), jnp.int32),
                   jax.ShapeDtypeStruct((rows, ROUTER_LANES), F32),
                   jax.ShapeDtypeStruct((1, ROUTER_LANES), jnp.int32)),
        grid=(rows // tm,),
        in_specs=[row_spec(d),
                  pl.BlockSpec((1, d), lambda i: (0, 0)),
                  pl.BlockSpec((2, d, ROUTER_LANES), lambda i: (0, 0, 0))],
        out_specs=(row_spec(d // 2), row_spec(ROUTER_LANES), row_spec(ROUTER_LANES),
                   pl.BlockSpec((1, ROUTER_LANES), lambda i: (0, 0))),
        scratch_shapes=[pltpu.VMEM((1, ROUTER_LANES), F32)],
        compiler_params=_params("arbitrary"),
        name="router",
    )(x2, gain.reshape(1, d), w)


def _sc_workers():
    info = plsc.get_sparse_core_info()
    mesh = plsc.VectorSubcoreMesh(core_axis_name="core", subcore_axis_name="subcore")
    return mesh, info.num_cores, info.num_subcores


def _pack_weight_rows(w, col_tile):
    r, c = w.shape
    mesh, nc, ns = _sc_workers()
    rs, ncol = SC_PACK_ROWS, c // col_tile
    per_w = (r // rs) * ncol // (nc * ns)
    assert per_w * nc * ns * rs * col_tile == r * c and per_w % 2 == 0 and col_tile % SC_LANES == 0
    params = pltpu.CompilerParams()
    if "needs_layout_passes" in pltpu.CompilerParams.__dataclass_fields__:
        params = dataclasses.replace(params, needs_layout_passes=False)

    @functools.partial(
        pl.kernel, mesh=mesh, out_type=jax.ShapeDtypeStruct((r // 2, c), jnp.uint32), compiler_params=params,
        cost_estimate=pl.CostEstimate(flops=r * c, transcendentals=0, bytes_accessed=6 * r * c),
        scratch_types=[pltpu.VMEM((2, rs, col_tile), F32), pltpu.VMEM((2, rs // 2, col_tile), jnp.uint32),
                       pltpu.SemaphoreType.DMA((2,)), pltpu.SemaphoreType.DMA((2,))])
    def pack(w_hbm, o_hbm, in_v, out_v, rsem, wsem):
        wid = lax.axis_index("subcore") * nc + lax.axis_index("core")

        @pl.loop(0, per_w // 2)
        def _(it):
            reads, writes = [], []
            for b in range(2):
                tile = wid * per_w + 2 * it + b
                r0 = pl.multiple_of((tile // ncol) * rs, rs)
                c0 = pl.multiple_of((tile % ncol) * col_tile, col_tile)
                reads.append(pltpu.make_async_copy(w_hbm.at[pl.ds(r0, rs), pl.ds(c0, col_tile)], in_v.at[b],
                                                   rsem.at[b]))
                writes.append(pltpu.make_async_copy(
                    out_v.at[b], o_hbm.at[pl.ds(pl.multiple_of(r0 // 2, rs // 2), rs // 2), pl.ds(c0, col_tile)],
                    wsem.at[b]))
            reads[0].start()
            reads[1].start()
            for b in range(2):
                reads[b].wait()
                for pair in range(rs // 2):
                    @plsc.parallel_loop(0, col_tile, step=SC_LANES, unroll=8)
                    def _(j):
                        packed = plsc.pack(in_v[b, 2 * pair, pl.ds(j, SC_LANES)],
                                           in_v[b, 2 * pair + 1, pl.ds(j, SC_LANES)],
                                           format=plsc.PackFormat.INTERLEAVED)
                        out_v[b, pair, pl.ds(j, SC_LANES)] = plsc.bitcast(packed, jnp.uint32)
                writes[b].start()
            writes[0].wait()
            writes[1].wait()

    return pack(w)


def _sc_token_rows(t, nc, ns, ch):
    per_w = t // (nc * ns)
    assert per_w * nc * ns == t and per_w % (2 * ch) == 0
    return per_w, per_w // ch


def _scatter_rows(table, dests, p_rows, after=()):
    t, w = table.shape
    nk = len(dests)
    mesh, nc, ns = _sc_workers()
    ch = SC_SCATTER_ROWS
    per_w, nit = _sc_token_rows(t, nc, ns, ch)

    @functools.partial(
        pl.kernel, mesh=mesh, out_type=jax.ShapeDtypeStruct((p_rows, w), table.dtype),
        scratch_types=[pltpu.VMEM((nk, nit, ch), jnp.int32), pltpu.VMEM((2, ch, w), table.dtype),
                       pltpu.SemaphoreType.DMA((2,)), pltpu.SemaphoreType.DMA((2, nk))])
    def scatter(table_hbm, *refs):
        dest_hbm, out_hbm = refs[:nk], refs[nk + len(after)]
        idx_v, rows_v, rsem, wsem = refs[nk + len(after) + 1:]
        wid = lax.axis_index("subcore") * nc + lax.axis_index("core")
        for k in range(nk):
            pltpu.sync_copy(dest_hbm[k].at[pl.ds(wid * nit, nit)], idx_v.at[k])

        @pl.loop(0, nit // 2)
        def _(it):
            reads = [pltpu.make_async_copy(table_hbm.at[pl.ds(wid * per_w + (2 * it + b) * ch, ch)],
                                           rows_v.at[b], rsem.at[b]) for b in range(2)]
            writes = [[pltpu.make_async_copy(rows_v.at[b], out_hbm.at[idx_v.at[k].at[2 * it + b]], wsem.at[b, k])
                       for k in range(nk)] for b in range(2)]
            reads[0].start()
            reads[1].start()
            for b in range(2):
                reads[b].wait()
                for k in range(nk):
                    writes[b][k].start()
            for b in range(2):
                for k in range(nk):
                    writes[b][k].wait()

    return scatter(table, *[d.reshape(t // ch, ch) for d in dests], *after)


def _gather_rows(table, idxs):
    t = idxs[0].shape[0]
    w = table.shape[1]
    nk = len(idxs)
    mesh, nc, ns = _sc_workers()
    ch = SC_ROWS
    per_w, nit = _sc_token_rows(t, nc, ns, ch)

    @functools.partial(
        pl.kernel, mesh=mesh, out_type=[jax.ShapeDtypeStruct((t, w), table.dtype)] * nk,
        scratch_types=[pltpu.VMEM((nk, nit, ch), jnp.int32), pltpu.VMEM((nk, 2, ch, w), table.dtype),
                       pltpu.SemaphoreType.DMA((nk, 2)), pltpu.SemaphoreType.DMA((nk, 2))])
    def gather(table_hbm, *refs):
        idx_hbm, out_hbm = refs[:nk], refs[nk:2 * nk]
        idx_v, rows_v, rsem, wsem = refs[2 * nk:]
        wid = lax.axis_index("subcore") * nc + lax.axis_index("core")
        for k in range(nk):
            pltpu.sync_copy(idx_hbm[k].at[pl.ds(wid * nit, nit)], idx_v.at[k])

        @pl.loop(0, nit // 2)
        def _(it):
            slots = [(k, b) for k in range(nk) for b in range(2)]
            reads = {(k, b): pltpu.make_async_copy(table_hbm.at[idx_v.at[k].at[2 * it + b]], rows_v.at[k, b],
                                                   rsem.at[k, b]) for k, b in slots}
            writes = {(k, b): pltpu.make_async_copy(rows_v.at[k, b],
                                                    out_hbm[k].at[pl.ds(wid * per_w + (2 * it + b) * ch, ch)],
                                                    wsem.at[k, b]) for k, b in slots}
            for s in slots:
                reads[s].start()
            for s in slots:
                reads[s].wait()
                writes[s].start()
            for s in slots:
                writes[s].wait()

    return gather(table, *[i.reshape(t // ch, ch) for i in idxs])


def _combine_kernel(x_ref, y0_ref, y1_ref, gate_ref, *rest):
    o_ref = rest[-1]
    half = x_ref.shape[1] // 2
    for part in range(2):
        cols = slice(part * half, (part + 1) * half)
        acc = x_ref[:, cols]
        for k, y_ref in enumerate((y0_ref, y1_ref)):
            word = y_ref[...]
            bits = (word << 16) if part == 0 else (word & jnp.uint32(0xFFFF0000))
            acc = acc + gate_ref[:, k:k + 1] * lax.bitcast_convert_type(bits, F32)
        o_ref[:, cols] = acc


def _combine(x2, y0, y1, gates, part, earlier=None):
    t, d = x2.shape
    tm = ROW_TILE
    steps = y0.shape[0] // tm
    here = lambda i: (part * steps + i, 0)
    in_specs = [pl.BlockSpec((tm, d), here),
                pl.BlockSpec((tm, d // 2), lambda i: (i, 0)),
                pl.BlockSpec((tm, d // 2), lambda i: (i, 0)),
                pl.BlockSpec((tm, TOP_K), here)]
    operands = [x2, y0, y1, gates]
    if earlier is not None:
        in_specs.append(pl.BlockSpec(memory_space=pl.ANY))
        operands.append(earlier)
    return pl.pallas_call(
        _combine_kernel,
        out_shape=jax.ShapeDtypeStruct((t, d), F32),
        grid=(steps,),
        in_specs=in_specs,
        out_specs=pl.BlockSpec((tm, d), here),
        input_output_aliases={} if earlier is None else {len(operands) - 1: 0},
        compiler_params=_params("parallel"),
        name="moe_combine",
    )(*operands)


def _moe(x2, gain, router, wg, wu, wd):
    t, d = x2.shape
    hp, idx, gate, count = _router(x2, gain, router)
    experts = jnp.arange(N_EXPERTS, dtype=jnp.int32)
    counts = count[0, :N_EXPERTS]
    padded = (counts + MOE_ROWS - 1) // MOE_ROWS * MOE_ROWS
    pend = jnp.cumsum(padded)
    pstart = pend - padded
    expert, rank = idx[:, :TOP_K], idx[:, TOP_K:2 * TOP_K]
    dest = jnp.sum(jnp.where(expert[:, :, None] == experts, pstart, 0), axis=-1) + rank
    dests = [dest[:, k] for k in range(TOP_K)]
    p_rows = t * TOP_K + N_EXPERTS * MOE_ROWS
    blk_row = jnp.arange(p_rows // MOE_ROWS, dtype=jnp.int32) * MOE_ROWS
    blk_expert = jnp.minimum(jnp.sum(pend[None, :] <= blk_row[:, None], axis=1), N_EXPERTS - 1).astype(jnp.int32)
    valid = jnp.clip(counts[blk_expert] - (blk_row - pstart[blk_expert]), 0, MOE_ROWS)
    valid = jnp.where(blk_row < pend[-1], valid, 0).astype(jnp.int32)
    n_exp, _, f = wg.shape
    pack = lambda w, col_tile: _pack_weight_rows(w.reshape(-1, w.shape[2]), col_tile).reshape(n_exp, -1, w.shape[2])
    wg, wu, wd = pack(wg, f // 2), pack(wu, f // 2), pack(wd, d)
    hb = _scatter_rows(hp, dests, p_rows, after=(wg, wu, wd))
    yb = _ffn_experts(hb, wg, wu, wd, blk_expert, valid)
    out = None
    for part in range(MOE_TAIL_PARTS):
        rows = slice(part * t // MOE_TAIL_PARTS, (part + 1) * t // MOE_TAIL_PARTS)
        y0, y1 = _gather_rows(yb, [dk[rows] for dk in dests])
        out = _combine(x2, y0, y1, gate[:, :TOP_K], part, out)
    return out


def kernel(x, mem, rel_bias, norm_mix, norm_mem, norm_ffn, w_mem_kv, xq_norm, xk_norm, w_out, hy_w_in, hy_conv_w, hy_conv_b, hy_filt_w1, hy_filt_b1, hy_filt_w2, hy_filt_b2, hy_filt_w3, hy_sin_freq, hy_skip, at_w_in, at_q_norm, at_k_norm, ffn_w_gate, ffn_w_up, ffn_w_down, moe_router, moe_w_gate, moe_w_up, moe_w_down):
    b, s, d = x.shape
    t = b * s
    m_len = mem.shape[1]
    x2 = x.reshape(t, d)
    mem2 = mem.reshape(b * m_len, d)
    bf = lambda w: w.astype(BF16)
    score_scale = HD ** -0.5
    xa_heads = XA_W // HD
    at_heads = AT_W // HD
    plain = lambda width: [None] * (width // HD)
    kv_gains = lambda i: [xk_norm[i]] * xa_heads + plain(XA_W)

    tables = _dft_tables(s)
    spectra, hq = _hyena_filters(s, tables, hy_filt_w1[0], hy_filt_b1[0], hy_filt_w2[0], hy_filt_b2[0],
                                 hy_filt_w3[0], hy_sin_freq[0])
    z = _norm_matmul(x2, norm_mix[0], bf(hy_w_in[0]), 1024,
                     plain(3 * HY_C) + [xq_norm[0] * score_scale] * xa_heads).reshape(b, s, -1)
    kv = _norm_matmul(mem2, norm_mem[0], bf(w_mem_kv[0]), 1024, kv_gains(0)).reshape(b, m_len, -1)
    self_out = _hyena(z, hy_conv_w[0], hy_conv_b[0], tables, spectra, hq, hy_skip[0])
    x2 = _mix_out(x2.reshape(b, s, d), self_out, z, 3 * HY_C // XA_W, kv, bf(w_out[0])).reshape(t, d)
    x2 = _ffn_dense(x2, norm_ffn[0], bf(ffn_w_gate[0]), bf(ffn_w_up[0]), bf(ffn_w_down[0]))

    at_gains = ([at_q_norm[0] * score_scale] * at_heads + [at_k_norm[0]] * at_heads + plain(AT_W)
                + [xq_norm[1] * score_scale] * xa_heads)
    z = _norm_matmul(x2, norm_mix[1], bf(at_w_in[0]), 1024, at_gains).reshape(b, s, -1)
    kv = _norm_matmul(mem2, norm_mem[1], bf(w_mem_kv[1]), 1024, kv_gains(1)).reshape(b, m_len, -1)
    self_out = _dilated_attention(z, rel_bias)
    x2 = _mix_out(x2.reshape(b, s, d), self_out, z, 3 * AT_W // XA_W, kv, bf(w_out[1])).reshape(t, d)
    x2 = _moe(x2, norm_ffn[1], moe_router[0], moe_w_gate[0], moe_w_up[0], moe_w_down[0])
    return x2.reshape(b, s, d)
```

```python
import dataclasses
import functools
import math

import jax
import jax.numpy as jnp
import numpy as np
from jax import lax
from jax.experimental import pallas as pl
from jax.experimental.pallas import tpu as pltpu
from jax.experimental.pallas import tpu_sc as plsc

F32 = jnp.float32
BF16 = jnp.bfloat16

D_MODEL = 1024
EPS = 1e-6
HY_C = 512
FILT_BANDS = 16
DECAY_TARGET = 1e-2
FAST_DECAY_PCT = 0.3
SLOW_DECAY_PCT = 1.5
MOD_SHIFT = 0.05
AT_GROUPS = ((128, 1), (512, 4), (2048, 16))
AT_HEADS = 4
HD = 128
AT_W = 1536
NUM_BUCKETS = 32
REL_MAX_DIST = 1024
NEG_INF = -1e30
XA_W = 512
D_FF = 2816
N_EXPERTS = 8
TOP_K = 2

VMEM_LIMIT_BYTES = 56 * 1024 * 1024
ROW_TILE = 1024
PROJ_ROW_TILE = 1024
FF_TILE = 256
MOE_ROWS = 512
MOE_CHUNK = 512
MOE_TAIL_PARTS = 4
SC_ROWS = 32
SC_SCATTER_ROWS = 64
SC_LANES = 16
SC_PACK_ROWS = 16
ATT_TQ = 128
ATT_KW = 256
HY_TC = 256
HY_KT = 512


def _params(*sem):
    return pltpu.CompilerParams(dimension_semantics=sem, vmem_limit_bytes=VMEM_LIMIT_BYTES)


def _rms_rows(x, gain):
    return x * lax.rsqrt(jnp.mean(x * x, axis=-1, keepdims=True) + EPS) * gain


def _bf16_bits(v):
    u = lax.bitcast_convert_type(v, jnp.uint32)
    return (u + jnp.uint32(0x7FFF) + ((u >> 16) & jnp.uint32(1))) >> 16


def _resident(shape, index_map):
    return pl.BlockSpec(shape, index_map, pipeline_mode=pl.Buffered(1))


def _norm_matmul_kernel(x_ref, g_ref, w_ref, hg_ref, o_ref, *, tn, head_norm):
    h = _rms_rows(x_ref[...], g_ref[...]).astype(BF16)
    for c in range(o_ref.shape[1] // tn):
        acc = jnp.dot(h, w_ref[:, c * tn:(c + 1) * tn], preferred_element_type=F32)
        for j in range(tn // HD):
            cols = slice(c * tn + j * HD, c * tn + (j + 1) * HD)
            seg = acc[:, j * HD:(j + 1) * HD]
            if head_norm[cols.start // HD]:
                seg = _rms_rows(seg, hg_ref[:, cols])
            o_ref[:, cols] = seg.astype(o_ref.dtype)


def _norm_matmul(x2, gain, w_bf, tn, head_gains):
    rows, d = x2.shape
    n = w_bf.shape[1]
    tm = min(PROJ_ROW_TILE, rows)
    assert len(head_gains) * HD == n
    hg = jnp.concatenate([jnp.ones((HD,), F32) if g is None else g.astype(F32) for g in head_gains]).reshape(1, n)
    return pl.pallas_call(
        functools.partial(_norm_matmul_kernel, tn=tn, head_norm=tuple(g is not None for g in head_gains)),
        out_shape=jax.ShapeDtypeStruct((rows, n), BF16),
        grid=(rows // tm,),
        in_specs=[pl.BlockSpec((tm, d), lambda i: (i, 0)),
                  pl.BlockSpec((1, d), lambda i: (0, 0)),
                  _resident((d, n), lambda i: (0, 0)),
                  pl.BlockSpec((1, n), lambda i: (0, 0))],
        out_specs=pl.BlockSpec((tm, n), lambda i: (i, 0)),
        compiler_params=_params("parallel"),
        cost_estimate=pl.CostEstimate(flops=2 * rows * d * n, transcendentals=rows,
                                      bytes_accessed=4 * rows * d + 2 * d * n + 2 * rows * n),
        name="norm_matmul",
    )(x2, gain.reshape(1, d), w_bf, hg)


def _mix_out_kernel(x_ref, a_ref, xq_ref, kv_ref, wa_ref, wc_ref, o_ref):
    heads = []
    for h in range(XA_W // HD):
        cols = slice(h * HD, (h + 1) * HD)
        s = lax.dot_general(xq_ref[0, :, cols], kv_ref[0, :, cols], (((1,), (1,)), ((), ())),
                            preferred_element_type=F32)
        p = jnp.exp(s - jnp.max(s, axis=-1, keepdims=True))
        z = jnp.sum(p, axis=-1, keepdims=True)
        pv = jnp.dot(p.astype(BF16), kv_ref[0, :, XA_W + h * HD:XA_W + (h + 1) * HD], preferred_element_type=F32)
        heads.append((pv / z).astype(BF16))
    cross = jnp.concatenate(heads, axis=1)
    o_ref[0] = (x_ref[0]
                + jnp.dot(a_ref[0], wa_ref[...], preferred_element_type=F32)
                + jnp.dot(cross, wc_ref[...], preferred_element_type=F32))


def _mix_out(x3, self_out, z3, xq_block, kv3, w_bf):
    b, s, d = x3.shape
    half = self_out.shape[2]
    m = kv3.shape[1]
    tm = ROW_TILE
    return pl.pallas_call(
        _mix_out_kernel,
        out_shape=jax.ShapeDtypeStruct((b, s, d), F32),
        grid=(b, s // tm),
        in_specs=[pl.BlockSpec((1, tm, d), lambda i, j: (i, j, 0)),
                  pl.BlockSpec((1, tm, half), lambda i, j: (i, j, 0)),
                  pl.BlockSpec((1, tm, XA_W), lambda i, j: (i, j, xq_block)),
                  pl.BlockSpec((1, m, 2 * XA_W), lambda i, j: (i, 0, 0)),
                  pl.BlockSpec((half, d), lambda i, j: (0, 0)),
                  pl.BlockSpec((d - half, d), lambda i, j: (1, 0))],
        out_specs=pl.BlockSpec((1, tm, d), lambda i, j: (i, j, 0)),
        compiler_params=_params("parallel", "parallel"),
        name="mix_out",
    )(x3, self_out, z3, kv3, w_bf, w_bf)


def _weight_rows(ref, start, size):
    if ref.dtype == jnp.uint32:
        return lambda cols: pltpu.bitcast(ref[start // 2:(start + size) // 2, cols], BF16)
    return lambda cols: ref[start:start + size, cols]


def _weight_shape(ref):
    return (ref.shape[0] * (2 if ref.dtype == jnp.uint32 else 1), ref.shape[1])


def _swiglu(h, wg_ref, wu_ref, wd_ref, tf):
    d, f = _weight_shape(wg_ref)
    assert f % tf == 0
    y = None
    for j in range(f // tf):
        cols = slice(j * tf, (j + 1) * tf)
        gg = jnp.dot(h, _weight_rows(wg_ref, 0, d)(cols), preferred_element_type=F32)
        uu = jnp.dot(h, _weight_rows(wu_ref, 0, d)(cols), preferred_element_type=F32)
        a = ((gg * jax.nn.sigmoid(gg)) * uu).astype(BF16)
        part = jnp.dot(a, _weight_rows(wd_ref, j * tf, tf)(slice(None)), preferred_element_type=F32)
        y = part if y is None else y + part
    return y


def _ffn_dense_kernel(x_ref, g_ref, wg_ref, wu_ref, wd_ref, o_ref, *, tf):
    x = x_ref[...]
    h = _rms_rows(x, g_ref[...]).astype(BF16)
    o_ref[...] = x + _swiglu(h, wg_ref, wu_ref, wd_ref, tf)


def _ffn_dense(x2, gain, wg, wu, wd):
    rows, d = x2.shape
    f = wg.shape[1]
    tm = ROW_TILE
    return pl.pallas_call(
        functools.partial(_ffn_dense_kernel, tf=FF_TILE),
        out_shape=jax.ShapeDtypeStruct((rows, d), F32),
        grid=(rows // tm,),
        in_specs=[pl.BlockSpec((tm, d), lambda i: (i, 0)),
                  pl.BlockSpec((1, d), lambda i: (0, 0)),
                  _resident((d, f), lambda i: (0, 0)),
                  _resident((d, f), lambda i: (0, 0)),
                  _resident((f, d), lambda i: (0, 0))],
        out_specs=pl.BlockSpec((tm, d), lambda i: (i, 0)),
        compiler_params=_params("parallel"),
        cost_estimate=pl.CostEstimate(flops=6 * rows * d * f, transcendentals=rows * f,
                                      bytes_accessed=8 * rows * d + 6 * d * f),
        name="ffn_dense",
    )(x2, gain.reshape(1, d), wg, wu, wd)


def _ffn_expert_kernel(eid_ref, valid_ref, hp_ref, wg_ref, wu_ref, wd_ref, o_ref, h_ref, *, tf):
    half = hp_ref.shape[1]
    valid = valid_ref[pl.program_id(0)]

    @pl.when(valid > 0)
    def _():
        keep = lax.broadcasted_iota(jnp.int32, (hp_ref.shape[0], 1), 0) < valid
        word = hp_ref[...]
        h_ref[:, :half] = jnp.where(keep, lax.bitcast_convert_type(word << 16, F32), 0.0).astype(BF16)
        h_ref[:, half:] = jnp.where(keep, lax.bitcast_convert_type(word & jnp.uint32(0xFFFF0000), F32),
                                    0.0).astype(BF16)
        y = _swiglu(h_ref[...], wg_ref.at[0], wu_ref.at[0], wd_ref.at[0], tf)
        o_ref[...] = _bf16_bits(y[:, :half]) | (_bf16_bits(y[:, half:]) << 16)

    @pl.when(valid <= 0)
    def _():
        o_ref[...] = jnp.zeros_like(o_ref)


def _ffn_experts(hp, wg, wu, wd, eid, valid):
    rows, half = hp.shape
    d = 2 * half
    tm = MOE_ROWS
    expert_spec = lambda w: pl.BlockSpec((1,) + w.shape[1:], lambda i, e, n: (e[i], 0, 0))
    grid_spec = pltpu.PrefetchScalarGridSpec(
        num_scalar_prefetch=2,
        grid=(rows // tm,),
        in_specs=[pl.BlockSpec((tm, half), lambda i, e, n: (i, 0)),
                  expert_spec(wg), expert_spec(wu), expert_spec(wd)],
        out_specs=pl.BlockSpec((tm, half), lambda i, e, n: (i, 0)),
        scratch_shapes=[pltpu.VMEM((tm, d), BF16)],
    )
    return pl.pallas_call(
        functools.partial(_ffn_expert_kernel, tf=FF_TILE),
        out_shape=jax.ShapeDtypeStruct((rows, half), jnp.uint32),
        grid_spec=grid_spec,
        compiler_params=_params("arbitrary"),
        name="ffn_experts",
    )(eid, valid, hp, wg, wu, wd)


def _dft_kernel(ce_ref, se_ref, co_ref, so_ref, cot_ref, sot_ref, base_ref, *, n_fft):
    rows, cols = ce_ref.shape
    i = pl.program_id(0)
    theta = 2.0 * math.pi / n_fft

    def phases(r, c):
        return r * (2 * c), r * (2 * c + 1), c * (2 * r + 1)

    @pl.when(i == 0)
    def _():
        r = lax.broadcasted_iota(jnp.int32, (rows, cols), 0)
        c = lax.broadcasted_iota(jnp.int32, (rows, cols), 1)
        for f, ph in enumerate(phases(r, c)):
            ang = (ph & (n_fft - 1)).astype(F32) * theta
            base_ref[2 * f] = jnp.cos(ang)
            base_ref[2 * f + 1] = jnp.sin(ang)

    c = lax.broadcasted_iota(jnp.int32, (8, cols), 1)
    r0 = i * rows
    shifts = (r0 * (2 * c), r0 * (2 * c + 1), c * (2 * r0))
    for f, (c_ref, s_ref) in enumerate(((ce_ref, se_ref), (co_ref, so_ref), (cot_ref, sot_ref))):
        ang = (shifts[f] & (n_fft - 1)).astype(F32) * theta
        ca, sa = jnp.cos(ang)[0:1], jnp.sin(ang)[0:1]
        cb, sb = base_ref[2 * f], base_ref[2 * f + 1]
        c_ref[...] = (cb * ca - sb * sa).astype(BF16)
        s_ref[...] = (sb * ca + cb * sa).astype(BF16)


def _dft_tables(length):
    half, rows = length // 2, 128
    shape = jax.ShapeDtypeStruct((half, half), BF16)
    spec = pl.BlockSpec((rows, half), lambda i: (i, 0))
    return pl.pallas_call(
        functools.partial(_dft_kernel, n_fft=2 * length),
        out_shape=(shape,) * 6,
        grid=(half // rows,),
        out_specs=(spec,) * 6,
        scratch_shapes=[pltpu.VMEM((6, rows, half), F32)],
        compiler_params=_params("arbitrary"),
        name="dft_tables",
    )()


LANES = 128


def _split_parity(tmp_ref, x):
    half = x.shape[0] // 2
    for j in range(tmp_ref.shape[0]):
        tmp_ref[j] = x[:, j * LANES:(j + 1) * LANES]
    pick = lambda start: jnp.concatenate(
        [tmp_ref[j, pl.ds(start, half, stride=2), :] for j in range(tmp_ref.shape[0])], axis=1)
    return pick(0), pick(1)


def _merge_parity(tmp_ref, even, odd):
    half = even.shape[0]
    for j in range(tmp_ref.shape[0]):
        tmp_ref[j, pl.ds(0, half, stride=2), :] = even[:, j * LANES:(j + 1) * LANES]
        tmp_ref[j, pl.ds(1, half, stride=2), :] = odd[:, j * LANES:(j + 1) * LANES]
    return jnp.concatenate([tmp_ref[j] for j in range(tmp_ref.shape[0])], axis=1)


def _filter_time_kernel(feats_ref, w1_ref, b1_ref, w2_ref, b2_ref, fr_ref, w3_ref, t_ref, delta_ref,
                        g_ref, hq_ref, tmp_ref, h_ref):
    hp = lax.Precision.HIGHEST
    length = feats_ref.shape[0]
    half = length // 2

    @pl.when(pl.program_id(0) == 0)
    def _():
        fr = fr_ref[...]
        h1 = jnp.sin(fr * (jnp.dot(feats_ref[...], w1_ref[...], preferred_element_type=F32, precision=hp)
                           + b1_ref[...]))
        h_ref[...] = jnp.sin(fr * (jnp.dot(h1, w2_ref[...], preferred_element_type=F32, precision=hp) + b2_ref[...]))

    h = h_ref[...]
    mod =jnp.exp(-t_ref[...] * delta_ref[...]) + MOD_SHIFT
    row = lax.broadcasted_iota(jnp.int32, (length, 1), 0)
    alt = jnp.where(lax.broadcasted_iota(jnp.int32, (half, 1), 0) % 2 == 0, 1.0, -1.0).astype(F32)
    for o in range(2):
        fwd = jnp.dot(h, w3_ref[2 * o], preferred_element_type=F32, precision=hp) * mod
        bwd = jnp.dot(h, w3_ref[2 * o + 1], preferred_element_type=F32, precision=hp) * mod
        bwd = jnp.where(row == 0, 0.0, bwd)
        norm = (jnp.sum(jnp.abs(fwd), axis=0, keepdims=True)
                + jnp.sum(jnp.abs(bwd), axis=0, keepdims=True) + 1e-6)
        fwd = fwd / norm
        bwd = bwd / norm
        for part, g in enumerate((fwd + bwd, fwd - bwd)):
            even, odd = _split_parity(tmp_ref, g)
            g_ref[o, 2 * part] = even.astype(BF16)
            g_ref[o, 2 * part + 1] = odd.astype(BF16)
            mid = jnp.sum((even if part == 0 else odd) * alt, axis=0, keepdims=True) * (1.0 / length)
            hq_ref[o, part] = mid if part == 0 else -mid


def _filter_freq_kernel(ce_ref, se_ref, co_ref, so_ref, g_ref, h_ref):
    half = ce_ref.shape[0]
    row = lax.broadcasted_iota(jnp.int32, (half, 1), 0)
    scale = jnp.where(row == 0, 1.0, 2.0).astype(F32) * (1.0 / (4 * half))
    a = jnp.dot(ce_ref[...], g_ref[0, 0], preferred_element_type=F32)
    b = jnp.dot(co_ref[...], g_ref[0, 1], preferred_element_type=F32)
    c = jnp.dot(se_ref[...], g_ref[0, 2], preferred_element_type=F32)
    d = jnp.dot(so_ref[...], g_ref[0, 3], preferred_element_type=F32)
    h_ref[0, 0] = (a + b) * scale
    h_ref[0, 1] = -(c + d) * scale
    h_ref[0, 2] = (a - b) * scale
    h_ref[0, 3] = (c - d) * scale


def _hyena_filters(length, tables, w1, b1, w2, b2, w3, freq):
    t = np.linspace(0.0, 1.0, length, dtype=np.float32)[:, None]
    f = np.linspace(1e-4, FILT_BANDS - 1, FILT_BANDS, dtype=np.float32)[None]
    ang = np.float32(2.0 * math.pi / length) * np.arange(length, dtype=np.float32)[:, None] * f
    feats = np.concatenate([t, np.cos(ang), -np.sin(ang)], axis=-1).astype(np.float32)
    deltas = np.abs(np.linspace(math.log(DECAY_TARGET) / SLOW_DECAY_PCT,
                                math.log(DECAY_TARGET) / FAST_DECAY_PCT, HY_C, dtype=np.float32))[None]
    hid = w1.shape[1]
    w3r = w3.reshape(hid, 4, HY_C).transpose(1, 0, 2)
    tc = HY_TC
    full = lambda shape: pl.BlockSpec(shape, lambda c: (0,) * len(shape))
    half = length // 2
    g, hq = pl.pallas_call(
        _filter_time_kernel,
        out_shape=(jax.ShapeDtypeStruct((2, 4, half, HY_C), BF16),
                   jax.ShapeDtypeStruct((2, 2, 1, HY_C), F32)),
        grid=(HY_C // tc,),
        in_specs=[full(feats.shape), full(w1.shape), full((1, hid)), full(w2.shape), full((1, hid)),
                  full((1, hid)), pl.BlockSpec((4, hid, tc), lambda c: (0, 0, c)), full((length, 1)),
                  pl.BlockSpec((1, tc), lambda c: (0, c))],
        out_specs=(pl.BlockSpec((2, 4, half, tc), lambda c: (0, 0, 0, c)),
                   pl.BlockSpec((2, 2, 1, tc), lambda c: (0, 0, 0, c))),
        scratch_shapes=[pltpu.VMEM((tc // LANES, length, LANES), F32), pltpu.VMEM((length, hid), F32)],
        compiler_params=_params("arbitrary"),
        name="filter_time",
    )(feats, w1, b1.reshape(1, hid), w2, b2.reshape(1, hid), freq.reshape(1, hid), w3r, t, deltas)
    ce, se, co, so = tables[:4]
    table_spec = _resident((half, half), lambda o, c: (0, 0))
    spectra = pl.pallas_call(
        _filter_freq_kernel,
        out_shape=jax.ShapeDtypeStruct((2, 4, half, HY_C), F32),
        grid=(2, HY_C // tc),
        in_specs=[table_spec] * 4 + [pl.BlockSpec((1, 4, half, tc), lambda o, c: (o, 0, 0, c))],
        out_specs=pl.BlockSpec((1, 4, half, tc), lambda o, c: (o, 0, 0, c)),
        compiler_params=_params("parallel", "parallel"),
        name="filter_freq",
    )(ce, se, co, so, g)
    return spectra, hq


def _hyena_kernel(z0_ref, z1_ref, z2_ref, cw_ref, cb_ref, ce_ref, se_ref, co_ref, so_ref, cot_ref, sot_ref,
                  h_ref, hq_ref, skip_ref, o_ref, tmp_ref):
    length = z0_ref.shape[1]
    half = length // 2
    row = lax.broadcasted_iota(jnp.int32, (length, 1), 0)
    alt = jnp.where(lax.broadcasted_iota(jnp.int32, (half, 1), 0) % 2 == 0, 1.0, -1.0).astype(F32)

    def short_conv(z_ref, c):
        u = z_ref[0].astype(F32)
        prev = jnp.where(row == 0, 0.0, pltpu.roll(u, 1, axis=0))
        nxt = jnp.where(row == length - 1, 0.0, pltpu.roll(u, length - 1, axis=0))
        return cb_ref[c] + (prev * cw_ref[0, c] + u * cw_ref[1, c] + nxt * cw_ref[2, c])

    def long_conv(u, o):
        ue, uo = _split_parity(tmp_ref, u)
        ue_bf, uo_bf = ue.astype(BF16), uo.astype(BF16)
        a4 = jnp.sum(ue * alt, axis=0, keepdims=True)
        b4 = jnp.sum(uo * alt, axis=0, keepdims=True)
        hr4, hi4 = hq_ref[o, 0], hq_ref[o, 1]
        y_even = alt * (a4 * hr4 + b4 * hi4)
        y_odd = -(alt * (a4 * hi4 - b4 * hr4))
        for kt in range(half // HY_KT):
            ks = slice(kt * HY_KT, (kt + 1) * HY_KT)
            ae = jnp.dot(ce_ref[ks, :], ue_bf, preferred_element_type=F32)
            ao = jnp.dot(co_ref[ks, :], uo_bf, preferred_element_type=F32)
            be = jnp.dot(se_ref[ks, :], ue_bf, preferred_element_type=F32)
            bo = jnp.dot(so_ref[ks, :], uo_bf, preferred_element_type=F32)
            a_f, a_g, b_f, b_g = ae + ao, ae - ao, be + bo, bo - be
            hr_f, hi_f, hr_g, hi_g = (h_ref[o, j, ks, :] for j in range(4))
            re_f, im_f = a_f * hr_f + b_f * hi_f, a_f * hi_f - b_f * hr_f
            re_g, im_g = a_g * hr_g + b_g * hi_g, a_g * hi_g - b_g * hr_g
            y_even += (jnp.dot(ce_ref[:, ks], (re_f + re_g).astype(BF16), preferred_element_type=F32)
                       - jnp.dot(se_ref[:, ks], (im_f - im_g).astype(BF16), preferred_element_type=F32))
            y_odd += (jnp.dot(cot_ref[:, ks], (re_f - re_g).astype(BF16), preferred_element_type=F32)
                      - jnp.dot(sot_ref[:, ks], (im_f + im_g).astype(BF16), preferred_element_type=F32))
        return _merge_parity(tmp_ref, y_even, y_odd) + u * skip_ref[o]

    z = short_conv(z1_ref, 1) * long_conv(short_conv(z0_ref, 0), 0)
    o_ref[0] = (short_conv(z2_ref, 2) * long_conv(z, 1)).astype(o_ref.dtype)


def _hyena(z3, conv_w, conv_b, tables, spectra, hq, skip):
    b, s, _ = z3.shape
    half = s // 2
    tc = HY_TC
    nct = HY_C // tc
    cw = conv_w.reshape(3, 3, 1, HY_C)
    cb = conv_b.reshape(3, 1, HY_C)
    zspec = lambda chunk: pl.BlockSpec((1, s, tc), lambda c, i: (i, 0, chunk * nct + c))
    return pl.pallas_call(
        _hyena_kernel,
        out_shape=jax.ShapeDtypeStruct((b, s, HY_C), BF16),
        grid=(nct, b),
        in_specs=[zspec(0), zspec(1), zspec(2),
                  pl.BlockSpec((3, 3, 1, tc), lambda c, i: (0, 0, 0, c)),
                  pl.BlockSpec((3, 1, tc), lambda c, i: (0, 0, c))]
                 + [_resident((half, half), lambda c, i: (0, 0))] * 6
                 + [_resident((2, 4, half, tc), lambda c, i: (0, 0, 0, c)),
                    pl.BlockSpec((2, 2, 1, tc), lambda c, i: (0, 0, 0, c)),
                    pl.BlockSpec((2, 1, tc), lambda c, i: (0, 0, c))],
        out_specs=pl.BlockSpec((1, s, tc), lambda c, i: (i, 0, c)),
        scratch_shapes=[pltpu.VMEM((tc // LANES, s, LANES), F32)],
        compiler_params=_params("parallel", "parallel"),
        cost_estimate=pl.CostEstimate(flops=2 * 4 * 2 * b * s * half * HY_C, transcendentals=0,
                                      bytes_accessed=2 * 3 * b * s * HY_C + 4 * b * s * HY_C + 12 * half * half),
        name="hyena",
    )(z3, z3, z3, cw, cb, *tables, spectra, hq, skip.reshape(2, 1, HY_C))


def _rel_bucket(rel):
    half = NUM_BUCKETS // 2
    exact = half // 2
    n = np.abs(rel)
    large = exact + (np.log(np.maximum(n, 1) / exact) / np.log(REL_MAX_DIST / exact) * (half - exact)).astype(np.int32)
    large = np.minimum(large, half - 1)
    return (np.where(rel > 0, half, 0) + np.where(n < exact, n, large)).astype(np.int32)


def _att_tiles(length):
    kw = min(ATT_KW, length)
    tiles = []
    for qs in range(0, length, ATT_TQ):
        ks = min(max(qs - (kw - ATT_TQ) // 2, 0), length - kw)
        tiles.append((qs, ks, {0: 0, -64: 1, -128: 2}[ks - qs]))
    return kw, tiles


def _att_bias(rel_bias, group, seq):
    window, dil = AT_GROUPS[group]
    band = window // (2 * dil)
    kw, tiles = _att_tiles(seq // dil)
    offsets = sorted({ks - qs for qs, ks, _ in tiles}, reverse=True)
    table = rel_bias[:, group * AT_HEADS:(group + 1) * AT_HEADS].astype(F32)
    span = kw + ATT_TQ
    k = np.arange(span)
    rel = np.asarray(offsets)[:, None] + np.where(k < kw, k, k - span)[None, :]
    diag = jnp.where((np.abs(rel) <= band)[:, None, :],
                     jnp.swapaxes(table[_rel_bucket(rel * dil)], 1, 2), NEG_INF)
    return diag[:, :, None, :]


def _bias_tile(diag_ref, var, kw):
    rows = jnp.broadcast_to(diag_ref[var, 0], (ATT_TQ, diag_ref.shape[3]))
    return pltpu.roll(rows, 0, 1, stride=1, stride_axis=0)[:, :kw]


def _dilated_kernel(q1, q2, q3, k1, k2, k3, v1, v2, v3, b1, b2, b3, o_ref,
                    qn_ref, kn_ref, vn_ref, og_ref, lg_ref):
    seq = o_ref.shape[1]
    contract_last = (((1,), (1,)), ((), ()))
    for g, (q_ref, k_ref, v_ref, bias_ref) in enumerate(((q1, k1, v1, b1), (q2, k2, v2, b2), (q3, k3, v3, b3))):
        dil = AT_GROUPS[g][1]
        kw, tiles = _att_tiles(seq // dil)
        if dil > 1:
            qn_ref[...] = q_ref[0].astype(F32)
            kn_ref[...] = k_ref[0].astype(F32)
            vn_ref[...] = v_ref[0].astype(F32)
        bias = {var: _bias_tile(bias_ref, var, kw) for var in sorted({var for _, _, var in tiles})}
        for r in range(dil):
            for qs, ks, var in tiles:
                rows = lambda start, size: (pl.ds(r + start * dil, size, stride=dil) if dil > 1
                                            else pl.ds(start, size))
                if dil > 1:
                    qt = qn_ref[rows(qs, ATT_TQ), :].astype(BF16)
                    kt = kn_ref[rows(ks, kw), :].astype(BF16)
                    vt = vn_ref[rows(ks, kw), :].astype(BF16)
                else:
                    qt, kt, vt = q_ref[0, rows(qs, ATT_TQ), :], k_ref[0, rows(ks, kw), :], v_ref[0, rows(ks, kw), :]
                s = lax.dot_general(qt, kt, contract_last, preferred_element_type=F32) + bias[var]
                m = jnp.max(s, axis=-1, keepdims=True)
                p = jnp.exp(s - m)
                z = jnp.sum(p, axis=-1, keepdims=True)
                og_ref[g, rows(qs, ATT_TQ), :] = jnp.dot(p.astype(BF16), vt, preferred_element_type=F32) / z
                lg_ref[g, rows(qs, ATT_TQ), :] = jnp.broadcast_to(m + jnp.log(z), (ATT_TQ, HD))
    l0, l1, l2 = lg_ref[0], lg_ref[1], lg_ref[2]
    mx = jnp.maximum(jnp.maximum(l0, l1), l2)
    e0, e1, e2 = jnp.exp(l0 - mx), jnp.exp(l1 - mx), jnp.exp(l2 - mx)
    o_ref[0] = ((e0 * og_ref[0] + e1 * og_ref[1] + e2 * og_ref[2]) / (e0 + e1 + e2)).astype(o_ref.dtype)


def _dilated_attention(z3, rel_bias):
    b, s, _ = z3.shape
    ng = len(AT_GROUPS)
    col = lambda part, g: pl.BlockSpec((1, s, HD), lambda i, h: (i, 0, part * ng * AT_HEADS + g * AT_HEADS + h))
    biases = [_att_bias(rel_bias, g, s) for g in range(ng)]
    bias_spec = lambda a: pl.BlockSpec((a.shape[0], 1) + a.shape[2:], lambda i, h: (0, h, 0, 0))
    return pl.pallas_call(
        _dilated_kernel,
        out_shape=jax.ShapeDtypeStruct((b, s, AT_HEADS * HD), BF16),
        grid=(b, AT_HEADS),
        in_specs=[col(p, g) for p in range(3) for g in range(ng)] + [bias_spec(a) for a in biases],
        out_specs=pl.BlockSpec((1, s, HD), lambda i, h: (i, 0, h)),
        scratch_shapes=[pltpu.VMEM((s, HD), F32), pltpu.VMEM((s, HD), F32), pltpu.VMEM((s, HD), F32),
                        pltpu.VMEM((ng, s, HD), F32), pltpu.VMEM((ng, s, HD), F32)],
        compiler_params=_params("parallel", "parallel"),
        name="dilated_attention",
    )(*([z3] * 9), *biases)


ROUTER_LANES = 128


def _router_kernel(x_ref, g_ref, w_ref, hp_ref, idx_ref, gate_ref, count_ref, carry_ref):
    tm = x_ref.shape[0]
    half = hp_ref.shape[1]

    @pl.when(pl.program_id(0) == 0)
    def _():
        carry_ref[...] = jnp.zeros_like(carry_ref)

    h = _rms_rows(x_ref[...], g_ref[...])
    hp_ref[...] = _bf16_bits(h[:, :half]) | (_bf16_bits(h[:, half:]) << 16)
    h_hi = h.astype(BF16)
    h_lo = (h - h_hi.astype(F32)).astype(BF16)
    logits = (jnp.dot(h_hi, w_ref[0], preferred_element_type=F32)
              + (jnp.dot(h_hi, w_ref[1], preferred_element_type=F32)
                 + jnp.dot(h_lo, w_ref[0], preferred_element_type=F32)))
    lane = lax.broadcasted_iota(jnp.int32, logits.shape, 1).astype(F32)
    logits = jnp.where(lane < N_EXPERTS, logits, -jnp.inf)
    m1 = jnp.max(logits, axis=-1, keepdims=True)
    i1 = jnp.min(jnp.where(logits == m1, lane, float(ROUTER_LANES)), axis=-1, keepdims=True)
    rest = jnp.where(lane == i1, -jnp.inf, logits)
    m2 = jnp.max(rest, axis=-1, keepdims=True)
    i2 = jnp.min(jnp.where(rest == m2, lane, float(ROUTER_LANES)), axis=-1, keepdims=True)
    e2 = jnp.exp(m2 - m1)
    den = 1.0 + e2
    gate_ref[...] = jnp.where(lane == 0, 1.0 / den, e2 / den)
    chosen = jnp.where((lane == i1) | (lane == i2), 1.0, 0.0)
    earlier = lax.broadcasted_iota(jnp.int32, (tm, tm), 0) > lax.broadcasted_iota(jnp.int32, (tm, tm), 1)
    carry = carry_ref[...]
    before = jnp.dot(jnp.where(earlier, 1.0, 0.0).astype(BF16), chosen.astype(BF16),
                     preferred_element_type=F32) + carry
    r1 = jnp.sum(jnp.where(lane == i1, before, 0.0), axis=-1, keepdims=True)
    r2 = jnp.sum(jnp.where(lane == i2, before, 0.0), axis=-1, keepdims=True)
    idx_ref[...] = jnp.where(lane == 0, i1, jnp.where(lane == 1, i2, jnp.where(lane == 2, r1, r2))).astype(jnp.int32)
    carry = carry + jnp.sum(chosen, axis=0, keepdims=True)
    carry_ref[...] = carry
    count_ref[...] = carry.astype(jnp.int32)


def _router(x2, gain, router):
    rows, d = x2.shape
    tm = MOE_CHUNK
    w = jnp.zeros((d, ROUTER_LANES), F32).at[:, :N_EXPERTS].set(router.astype(F32))
    w_hi = w.astype(BF16)
    w = jnp.stack([w_hi, (w - w_hi.astype(F32)).astype(BF16)])
    row_spec = lambda width: pl.BlockSpec((tm, width), lambda i: (i, 0))
    return pl.pallas_call(
        _router_kernel,
        out_shape=(jax.ShapeDtypeStruct((rows, d // 2), jnp.uint32),
                   jax.ShapeDtypeStruct((rows, ROUTER_LANES), jnp.int32),
                   jax.ShapeDtypeStruct((rows, ROUTER_LANES), F32),
                   jax.ShapeDtypeStruct((1, ROUTER_LANES), jnp.int32)),
        grid=(rows // tm,),
        in_specs=[row_spec(d),
                  pl.BlockSpec((1, d), lambda i: (0, 0)),
                  pl.BlockSpec((2, d, ROUTER_LANES), lambda i: (0, 0, 0))],
        out_specs=(row_spec(d // 2), row_spec(ROUTER_LANES), row_spec(ROUTER_LANES),
                   pl.BlockSpec((1, ROUTER_LANES), lambda i: (0, 0))),
        scratch_shapes=[pltpu.VMEM((1, ROUTER_LANES), F32)],
        compiler_params=_params("arbitrary"),
        name="router",
    )(x2, gain.reshape(1, d), w)


def _sc_workers():
    info = plsc.get_sparse_core_info()
    mesh = plsc.VectorSubcoreMesh(core_axis_name="core", subcore_axis_name="subcore")
    return mesh, info.num_cores, info.num_subcores


def _pack_weight_rows(w, col_tile):
    r, c = w.shape
    mesh, nc, ns = _sc_workers()
    rs, ncol = SC_PACK_ROWS, c // col_tile
    per_w = (r // rs) * ncol // (nc * ns)
    assert per_w * nc * ns * rs * col_tile == r * c and per_w % 2 == 0 and col_tile % SC_LANES == 0
    params = pltpu.CompilerParams()
    if "needs_layout_passes" in pltpu.CompilerParams.__dataclass_fields__:
        params = dataclasses.replace(params, needs_layout_passes=False)

    @functools.partial(
        pl.kernel, mesh=mesh, out_type=jax.ShapeDtypeStruct((r // 2, c), jnp.uint32), compiler_params=params,
        cost_estimate=pl.CostEstimate(flops=r * c, transcendentals=0, bytes_accessed=6 * r * c),
        scratch_types=[pltpu.VMEM((2, rs, col_tile), F32), pltpu.VMEM((2, rs // 2, col_tile), jnp.uint32),
                       pltpu.SemaphoreType.DMA((2,)), pltpu.SemaphoreType.DMA((2,))])
    def pack(w_hbm, o_hbm, in_v, out_v, rsem, wsem):
        wid = lax.axis_index("subcore") * nc + lax.axis_index("core")

        @pl.loop(0, per_w // 2)
        def _(it):
            reads, writes = [], []
            for b in range(2):
                tile = wid * per_w + 2 * it + b
                r0 = pl.multiple_of((tile // ncol) * rs, rs)
                c0 = pl.multiple_of((tile % ncol) * col_tile, col_tile)
                reads.append(pltpu.make_async_copy(w_hbm.at[pl.ds(r0, rs), pl.ds(c0, col_tile)], in_v.at[b],
                                                   rsem.at[b]))
                writes.append(pltpu.make_async_copy(
                    out_v.at[b], o_hbm.at[pl.ds(pl.multiple_of(r0 // 2, rs // 2), rs // 2), pl.ds(c0, col_tile)],
                    wsem.at[b]))
            reads[0].start()
            reads[1].start()
            for b in range(2):
                reads[b].wait()
                for pair in range(rs // 2):
                    @plsc.parallel_loop(0, col_tile, step=SC_LANES, unroll=8)
                    def _(j):
                        packed = plsc.pack(in_v[b, 2 * pair, pl.ds(j, SC_LANES)],
                                           in_v[b, 2 * pair + 1, pl.ds(j, SC_LANES)],
                                           format=plsc.PackFormat.INTERLEAVED)
                        out_v[b, pair, pl.ds(j, SC_LANES)] = plsc.bitcast(packed, jnp.uint32)
                writes[b].start()
            writes[0].wait()
            writes[1].wait()

    return pack(w)


def _sc_token_rows(t, nc, ns, ch):
    per_w = t // (nc * ns)
    assert per_w * nc * ns == t and per_w % (2 * ch) == 0
    return per_w, per_w // ch


def _scatter_rows(table, dests, p_rows, after=()):
    t, w = table.shape
    nk = len(dests)
    mesh, nc, ns = _sc_workers()
    ch = SC_SCATTER_ROWS
    per_w, nit = _sc_token_rows(t, nc, ns, ch)

    @functools.partial(
        pl.kernel, mesh=mesh, out_type=jax.ShapeDtypeStruct((p_rows, w), table.dtype),
        scratch_types=[pltpu.VMEM((nk, nit, ch), jnp.int32), pltpu.VMEM((2, ch, w), table.dtype),
                       pltpu.SemaphoreType.DMA((2,)), pltpu.SemaphoreType.DMA((2, nk))])
    def scatter(table_hbm, *refs):
        dest_hbm, out_hbm = refs[:nk], refs[nk + len(after)]
        idx_v, rows_v, rsem, wsem = refs[nk + len(after) + 1:]
        wid = lax.axis_index("subcore") * nc + lax.axis_index("core")
        for k in range(nk):
            pltpu.sync_copy(dest_hbm[k].at[pl.ds(wid * nit, nit)], idx_v.at[k])

        @pl.loop(0, nit // 2)
        def _(it):
            reads = [pltpu.make_async_copy(table_hbm.at[pl.ds(wid * per_w + (2 * it + b) * ch, ch)],
                                           rows_v.at[b], rsem.at[b]) for b in range(2)]
            writes = [[pltpu.make_async_copy(rows_v.at[b], out_hbm.at[idx_v.at[k].at[2 * it + b]], wsem.at[b, k])
                       for k in range(nk)] for b in range(2)]
            reads[0].start()
            reads[1].start()
            for b in range(2):
                reads[b].wait()
                for k in range(nk):
                    writes[b][k].start()
            for b in range(2):
                for k in range(nk):
                    writes[b][k].wait()

    return scatter(table, *[d.reshape(t // ch, ch) for d in dests], *after)


def _gather_rows(table, idxs):
    t = idxs[0].shape[0]
    w = table.shape[1]
    nk = len(idxs)
    mesh, nc, ns = _sc_workers()
    ch = SC_ROWS
    per_w, nit = _sc_token_rows(t, nc, ns, ch)

    @functools.partial(
        pl.kernel, mesh=mesh, out_type=[jax.ShapeDtypeStruct((t, w), table.dtype)] * nk,
        scratch_types=[pltpu.VMEM((nk, nit, ch), jnp.int32), pltpu.VMEM((nk, 2, ch, w), table.dtype),
                       pltpu.SemaphoreType.DMA((nk, 2)), pltpu.SemaphoreType.DMA((nk, 2))])
    def gather(table_hbm, *refs):
        idx_hbm, out_hbm = refs[:nk], refs[nk:2 * nk]
        idx_v, rows_v, rsem, wsem = refs[2 * nk:]
        wid = lax.axis_index("subcore") * nc + lax.axis_index("core")
        for k in range(nk):
            pltpu.sync_copy(idx_hbm[k].at[pl.ds(wid * nit, nit)], idx_v.at[k])

        @pl.loop(0, nit // 2)
        def _(it):
            slots = [(k, b) for k in range(nk) for b in range(2)]
            reads = {(k, b): pltpu.make_async_copy(table_hbm.at[idx_v.at[k].at[2 * it + b]], rows_v.at[k, b],
                                                   rsem.at[k, b]) for k, b in slots}
            writes = {(k, b): pltpu.make_async_copy(rows_v.at[k, b],
                                                    out_hbm[k].at[pl.ds(wid * per_w + (2 * it + b) * ch, ch)],
                                                    wsem.at[k, b]) for k, b in slots}
            for s in slots:
                reads[s].start()
            for s in slots:
                reads[s].wait()
                writes[s].start()
            for s in slots:
                writes[s].wait()

    return gather(table, *[i.reshape(t // ch, ch) for i in idxs])


def _combine_kernel(x_ref, y0_ref, y1_ref, gate_ref, *rest):
    o_ref = rest[-1]
    half = x_ref.shape[1] // 2
    for part in range(2):
        cols = slice(part * half, (part + 1) * half)
        acc = x_ref[:, cols]
        for k, y_ref in enumerate((y0_ref, y1_ref)):
            word = y_ref[...]
            bits = (word << 16) if part == 0 else (word & jnp.uint32(0xFFFF0000))
            acc = acc + gate_ref[:, k:k + 1] * lax.bitcast_convert_type(bits, F32)
        o_ref[:, cols] = acc


def _combine(x2, y0, y1, gates, part, earlier=None):
    t, d = x2.shape
    tm = ROW_TILE
    steps = y0.shape[0] // tm
    here = lambda i: (part * steps + i, 0)
    in_specs = [pl.BlockSpec((tm, d), here),
                pl.BlockSpec((tm, d // 2), lambda i: (i, 0)),
                pl.BlockSpec((tm, d // 2), lambda i: (i, 0)),
                pl.BlockSpec((tm, TOP_K), here)]
    operands = [x2, y0, y1, gates]
    if earlier is not None:
        in_specs.append(pl.BlockSpec(memory_space=pl.ANY))
        operands.append(earlier)
    return pl.pallas_call(
        _combine_kernel,
        out_shape=jax.ShapeDtypeStruct((t, d), F32),
        grid=(steps,),
        in_specs=in_specs,
        out_specs=pl.BlockSpec((tm, d), here),
        input_output_aliases={} if earlier is None else {len(operands) - 1: 0},
        compiler_params=_params("parallel"),
        name="moe_combine",
    )(*operands)


def _moe(x2, gain, router, wg, wu, wd):
    t, d = x2.shape
    hp, idx, gate, count = _router(x2, gain, router)
    experts = jnp.arange(N_EXPERTS, dtype=jnp.int32)
    counts = count[0, :N_EXPERTS]
    padded = (counts + MOE_ROWS - 1) // MOE_ROWS * MOE_ROWS
    pend = jnp.cumsum(padded)
    pstart = pend - padded
    expert, rank = idx[:, :TOP_K], idx[:, TOP_K:2 * TOP_K]
    dest = jnp.sum(jnp.where(expert[:, :, None] == experts, pstart, 0), axis=-1) + rank
    dests = [dest[:, k] for k in range(TOP_K)]
    p_rows = t * TOP_K + N_EXPERTS * MOE_ROWS
    blk_row = jnp.arange(p_rows // MOE_ROWS, dtype=jnp.int32) * MOE_ROWS
    blk_expert = jnp.minimum(jnp.sum(pend[None, :] <= blk_row[:, None], axis=1), N_EXPERTS - 1).astype(jnp.int32)
    valid = jnp.clip(counts[blk_expert] - (blk_row - pstart[blk_expert]), 0, MOE_ROWS)
    valid = jnp.where(blk_row < pend[-1], valid, 0).astype(jnp.int32)
    n_exp, _, f = wg.shape
    pack = lambda w, col_tile: _pack_weight_rows(w.reshape(-1, w.shape[2]), col_tile).reshape(n_exp, -1, w.shape[2])
    wg, wu, wd = pack(wg, f // 2), pack(wu, f // 2), pack(wd, d)
    hb = _scatter_rows(hp, dests, p_rows, after=(wg, wu, wd))
    yb = _ffn_experts(hb, wg, wu, wd, blk_expert, valid)
    out = None
    for part in range(MOE_TAIL_PARTS):
        rows = slice(part * t // MOE_TAIL_PARTS, (part + 1) * t // MOE_TAIL_PARTS)
        y0, y1 = _gather_rows(yb, [dk[rows] for dk in dests])
        out = _combine(x2, y0, y1, gate[:, :TOP_K], part, out)
    return out


def kernel(x, mem, rel_bias, norm_mix, norm_mem, norm_ffn, w_mem_kv, xq_norm, xk_norm, w_out, hy_w_in, hy_conv_w, hy_conv_b, hy_filt_w1, hy_filt_b1, hy_filt_w2, hy_filt_b2, hy_filt_w3, hy_sin_freq, hy_skip, at_w_in, at_q_norm, at_k_norm, ffn_w_gate, ffn_w_up, ffn_w_down, moe_router, moe_w_gate, moe_w_up, moe_w_down):
    b, s, d = x.shape
    t = b * s
    m_len = mem.shape[1]
    x2 = x.reshape(t, d)
    mem2 = mem.reshape(b * m_len, d)
    bf = lambda w: w.astype(BF16)
    score_scale = HD ** -0.5
    xa_heads = XA_W // HD
    at_heads = AT_W // HD
    plain = lambda width: [None] * (width // HD)
    kv_gains = lambda i: [xk_norm[i]] * xa_heads + plain(XA_W)

    tables = _dft_tables(s)
    spectra, hq = _hyena_filters(s, tables, hy_filt_w1[0], hy_filt_b1[0], hy_filt_w2[0], hy_filt_b2[0],
                                 hy_filt_w3[0], hy_sin_freq[0])
    z = _norm_matmul(x2, norm_mix[0], bf(hy_w_in[0]), 1024,
                     plain(3 * HY_C) + [xq_norm[0] * score_scale] * xa_heads).reshape(b, s, -1)
    kv = _norm_matmul(mem2, norm_mem[0], bf(w_mem_kv[0]), 1024, kv_gains(0)).reshape(b, m_len, -1)
    self_out = _hyena(z, hy_conv_w[0], hy_conv_b[0], tables, spectra, hq, hy_skip[0])
    x2 = _mix_out(x2.reshape(b, s, d), self_out, z, 3 * HY_C // XA_W, kv, bf(w_out[0])).reshape(t, d)
    x2 = _ffn_dense(x2, norm_ffn[0], bf(ffn_w_gate[0]), bf(ffn_w_up[0]), bf(ffn_w_down[0]))

    at_gains = ([at_q_norm[0] * score_scale] * at_heads + [at_k_norm[0]] * at_heads + plain(AT_W)
                + [xq_norm[1] * score_scale] * xa_heads)
    z = _norm_matmul(x2, norm_mix[1], bf(at_w_in[0]), 1024, at_gains).reshape(b, s, -1)
    kv = _norm_matmul(mem2, norm_mem[1], bf(w_mem_kv[1]), 1024, kv_gains(1)).reshape(b, m_len, -1)
    self_out = _dilated_attention(z, rel_bias)
    x2 = _mix_out(x2.reshape(b, s, d), self_out, z, 3 * AT_W // XA_W, kv, bf(w_out[1])).reshape(t, d)
    x2 = _moe(x2, norm_ffn[1], moe_router[0], moe_w_gate[0], moe_w_up[0], moe_w_down[0])
    return x2.reshape(b, s, d)
```

```python
import dataclasses
import functools
import math

import jax
import jax.numpy as jnp
import numpy as np
from jax import lax
from jax.experimental import pallas as pl
from jax.experimental.pallas import tpu as pltpu
from jax.experimental.pallas import tpu_sc as plsc

F32 = jnp.float32
BF16 = jnp.bfloat16

D_MODEL = 1024
EPS = 1e-6
HY_C = 512
FILT_BANDS = 16
DECAY_TARGET = 1e-2
FAST_DECAY_PCT = 0.3
SLOW_DECAY_PCT = 1.5
MOD_SHIFT = 0.05
AT_GROUPS = ((128, 1), (512, 4), (2048, 16))
AT_HEADS = 4
HD = 128
AT_W = 1536
NUM_BUCKETS = 32
REL_MAX_DIST = 1024
NEG_INF = -1e30
XA_W = 512
D_FF = 2816
N_EXPERTS = 8
TOP_K = 2

VMEM_LIMIT_BYTES = 56 * 1024 * 1024
ROW_TILE = 1024
PROJ_ROW_TILE = 1024
FF_TILE = 256
MOE_ROWS = 512
MOE_CHUNK = 512
MOE_TAIL_PARTS = 4
SC_ROWS = 32
SC_SCATTER_ROWS = 64
SC_LANES = 16
SC_PACK_ROWS = 16
ATT_TQ = 128
ATT_KW = 256
HY_TC = 256
HY_KT = 1024


def _params(*sem):
    return pltpu.CompilerParams(dimension_semantics=sem, vmem_limit_bytes=VMEM_LIMIT_BYTES)


def _rms_rows(x, gain):
    return x * lax.rsqrt(jnp.mean(x * x, axis=-1, keepdims=True) + EPS) * gain


def _bf16_bits(v):
    u = lax.bitcast_convert_type(v, jnp.uint32)
    return (u + jnp.uint32(0x7FFF) + ((u >> 16) & jnp.uint32(1))) >> 16


def _resident(shape, index_map):
    return pl.BlockSpec(shape, index_map, pipeline_mode=pl.Buffered(1))


def _norm_matmul_kernel(x_ref, g_ref, w_ref, hg_ref, o_ref, *, tn, head_norm):
    h = _rms_rows(x_ref[...], g_ref[...]).astype(BF16)
    for c in range(o_ref.shape[1] // tn):
        acc = jnp.dot(h, w_ref[:, c * tn:(c + 1) * tn], preferred_element_type=F32)
        for j in range(tn // HD):
            cols = slice(c * tn + j * HD, c * tn + (j + 1) * HD)
            seg = acc[:, j * HD:(j + 1) * HD]
            if head_norm[cols.start // HD]:
                seg = _rms_rows(seg, hg_ref[:, cols])
            o_ref[:, cols] = seg.astype(o_ref.dtype)


def _norm_matmul(x2, gain, w_bf, tn, head_gains):
    rows, d = x2.shape
    n = w_bf.shape[1]
    tm = min(PROJ_ROW_TILE, rows)
    assert len(head_gains) * HD == n
    hg = jnp.concatenate([jnp.ones((HD,), F32) if g is None else g.astype(F32) for g in head_gains]).reshape(1, n)
    return pl.pallas_call(
        functools.partial(_norm_matmul_kernel, tn=tn, head_norm=tuple(g is not None for g in head_gains)),
        out_shape=jax.ShapeDtypeStruct((rows, n), BF16),
        grid=(rows // tm,),
        in_specs=[pl.BlockSpec((tm, d), lambda i: (i, 0)),
                  pl.BlockSpec((1, d), lambda i: (0, 0)),
                  _resident((d, n), lambda i: (0, 0)),
                  pl.BlockSpec((1, n), lambda i: (0, 0))],
        out_specs=pl.BlockSpec((tm, n), lambda i: (i, 0)),
        compiler_params=_params("parallel"),
        cost_estimate=pl.CostEstimate(flops=2 * rows * d * n, transcendentals=rows,
                                      bytes_accessed=4 * rows * d + 2 * d * n + 2 * rows * n),
        name="norm_matmul",
    )(x2, gain.reshape(1, d), w_bf, hg)


def _mix_out_kernel(x_ref, a_ref, xq_ref, kv_ref, wa_ref, wc_ref, o_ref):
    heads = []
    for h in range(XA_W // HD):
        cols = slice(h * HD, (h + 1) * HD)
        s = lax.dot_general(xq_ref[0, :, cols], kv_ref[0, :, cols], (((1,), (1,)), ((), ())),
                            preferred_element_type=F32)
        p = jnp.exp(s - jnp.max(s, axis=-1, keepdims=True))
        z = jnp.sum(p, axis=-1, keepdims=True)
        pv = jnp.dot(p.astype(BF16), kv_ref[0, :, XA_W + h * HD:XA_W + (h + 1) * HD], preferred_element_type=F32)
        heads.append((pv / z).astype(BF16))
    cross = jnp.concatenate(heads, axis=1)
    o_ref[0] = (x_ref[0]
                + jnp.dot(a_ref[0], wa_ref[...], preferred_element_type=F32)
                + jnp.dot(cross, wc_ref[...], preferred_element_type=F32))


def _mix_out(x3, self_out, z3, xq_block, kv3, w_bf):
    b, s, d = x3.shape
    half = self_out.shape[2]
    m = kv3.shape[1]
    tm = ROW_TILE
    return pl.pallas_call(
        _mix_out_kernel,
        out_shape=jax.ShapeDtypeStruct((b, s, d), F32),
        grid=(b, s // tm),
        in_specs=[pl.BlockSpec((1, tm, d), lambda i, j: (i, j, 0)),
                  pl.BlockSpec((1, tm, half), lambda i, j: (i, j, 0)),
                  pl.BlockSpec((1, tm, XA_W), lambda i, j: (i, j, xq_block)),
                  pl.BlockSpec((1, m, 2 * XA_W), lambda i, j: (i, 0, 0)),
                  pl.BlockSpec((half, d), lambda i, j: (0, 0)),
                  pl.BlockSpec((d - half, d), lambda i, j: (1, 0))],
        out_specs=pl.BlockSpec((1, tm, d), lambda i, j: (i, j, 0)),
        compiler_params=_params("parallel", "parallel"),
        name="mix_out",
    )(x3, self_out, z3, kv3, w_bf, w_bf)


def _weight_rows(ref, start, size):
    if ref.dtype == jnp.uint32:
        return lambda cols: pltpu.bitcast(ref[start // 2:(start + size) // 2, cols], BF16)
    return lambda cols: ref[start:start + size, cols]


def _weight_shape(ref):
    return (ref.shape[0] * (2 if ref.dtype == jnp.uint32 else 1), ref.shape[1])


def _swiglu(h, wg_ref, wu_ref, wd_ref, tf):
    d, f = _weight_shape(wg_ref)
    assert f % tf == 0
    y = None
    for j in range(f // tf):
        cols = slice(j * tf, (j + 1) * tf)
        gg = jnp.dot(h, _weight_rows(wg_ref, 0, d)(cols), preferred_element_type=F32)
        uu = jnp.dot(h, _weight_rows(wu_ref, 0, d)(cols), preferred_element_type=F32)
        a = ((gg * jax.nn.sigmoid(gg)) * uu).astype(BF16)
        part = jnp.dot(a, _weight_rows(wd_ref, j * tf, tf)(slice(None)), preferred_element_type=F32)
        y = part if y is None else y + part
    return y


def _ffn_dense_kernel(x_ref, g_ref, wg_ref, wu_ref, wd_ref, o_ref, *, tf):
    x = x_ref[...]
    h = _rms_rows(x, g_ref[...]).astype(BF16)
    o_ref[...] = x + _swiglu(h, wg_ref, wu_ref, wd_ref, tf)


def _ffn_dense(x2, gain, wg, wu, wd):
    rows, d = x2.shape
    f = wg.shape[1]
    tm = ROW_TILE
    return pl.pallas_call(
        functools.partial(_ffn_dense_kernel, tf=FF_TILE),
        out_shape=jax.ShapeDtypeStruct((rows, d), F32),
        grid=(rows // tm,),
        in_specs=[pl.BlockSpec((tm, d), lambda i: (i, 0)),
                  pl.BlockSpec((1, d), lambda i: (0, 0)),
                  _resident((d, f), lambda i: (0, 0)),
                  _resident((d, f), lambda i: (0, 0)),
                  _resident((f, d), lambda i: (0, 0))],
        out_specs=pl.BlockSpec((tm, d), lambda i: (i, 0)),
        compiler_params=_params("parallel"),
        cost_estimate=pl.CostEstimate(flops=6 * rows * d * f, transcendentals=rows * f,
                                      bytes_accessed=8 * rows * d + 6 * d * f),
        name="ffn_dense",
    )(x2, gain.reshape(1, d), wg, wu, wd)


def _ffn_expert_kernel(eid_ref, valid_ref, hp_ref, wg_ref, wu_ref, wd_ref, o_ref, h_ref, *, tf):
    half = hp_ref.shape[1]
    valid = valid_ref[pl.program_id(0)]

    @pl.when(valid > 0)
    def _():
        keep = lax.broadcasted_iota(jnp.int32, (hp_ref.shape[0], 1), 0) < valid
        word = hp_ref[...]
        h_ref[:, :half] = jnp.where(keep, lax.bitcast_convert_type(word << 16, F32), 0.0).astype(BF16)
        h_ref[:, half:] = jnp.where(keep, lax.bitcast_convert_type(word & jnp.uint32(0xFFFF0000), F32),
                                    0.0).astype(BF16)
        y = _swiglu(h_ref[...], wg_ref.at[0], wu_ref.at[0], wd_ref.at[0], tf)
        o_ref[...] = _bf16_bits(y[:, :half]) | (_bf16_bits(y[:, half:]) << 16)

    @pl.when(valid <= 0)
    def _():
        o_ref[...] = jnp.zeros_like(o_ref)


def _ffn_experts(hp, wg, wu, wd, eid, valid):
    rows, half = hp.shape
    d = 2 * half
    tm = MOE_ROWS
    expert_spec = lambda w: pl.BlockSpec((1,) + w.shape[1:], lambda i, e, n: (e[i], 0, 0))
    grid_spec = pltpu.PrefetchScalarGridSpec(
        num_scalar_prefetch=2,
        grid=(rows // tm,),
        in_specs=[pl.BlockSpec((tm, half), lambda i, e, n: (i, 0)),
                  expert_spec(wg), expert_spec(wu), expert_spec(wd)],
        out_specs=pl.BlockSpec((tm, half), lambda i, e, n: (i, 0)),
        scratch_shapes=[pltpu.VMEM((tm, d), BF16)],
    )
    return pl.pallas_call(
        functools.partial(_ffn_expert_kernel, tf=FF_TILE),
        out_shape=jax.ShapeDtypeStruct((rows, half), jnp.uint32),
        grid_spec=grid_spec,
        compiler_params=_params("arbitrary"),
        name="ffn_experts",
    )(eid, valid, hp, wg, wu, wd)


def _dft_kernel(ce_ref, se_ref, co_ref, so_ref, cot_ref, sot_ref, base_ref, *, n_fft):
    rows, cols = ce_ref.shape
    i = pl.program_id(0)
    theta = 2.0 * math.pi / n_fft

    def phases(r, c):
        return r * (2 * c), r * (2 * c + 1), c * (2 * r + 1)

    @pl.when(i == 0)
    def _():
        r = lax.broadcasted_iota(jnp.int32, (rows, cols), 0)
        c = lax.broadcasted_iota(jnp.int32, (rows, cols), 1)
        for f, ph in enumerate(phases(r, c)):
            ang = (ph & (n_fft - 1)).astype(F32) * theta
            base_ref[2 * f] = jnp.cos(ang)
            base_ref[2 * f + 1] = jnp.sin(ang)

    c = lax.broadcasted_iota(jnp.int32, (8, cols), 1)
    r0 = i * rows
    shifts = (r0 * (2 * c), r0 * (2 * c + 1), c * (2 * r0))
    for f, (c_ref, s_ref) in enumerate(((ce_ref, se_ref), (co_ref, so_ref), (cot_ref, sot_ref))):
        ang = (shifts[f] & (n_fft - 1)).astype(F32) * theta
        ca, sa = jnp.cos(ang)[0:1], jnp.sin(ang)[0:1]
        cb, sb = base_ref[2 * f], base_ref[2 * f + 1]
        c_ref[...] = (cb * ca - sb * sa).astype(BF16)
        s_ref[...] = (sb * ca + cb * sa).astype(BF16)


def _dft_tables(length):
    half, rows = length // 2, 128
    shape = jax.ShapeDtypeStruct((half, half), BF16)
    spec = pl.BlockSpec((rows, half), lambda i: (i, 0))
    return pl.pallas_call(
        functools.partial(_dft_kernel, n_fft=2 * length),
        out_shape=(shape,) * 6,
        grid=(half // rows,),
        out_specs=(spec,) * 6,
        scratch_shapes=[pltpu.VMEM((6, rows, half), F32)],
        compiler_params=_params("arbitrary"),
        name="dft_tables",
    )()


LANES = 128


def _split_parity(tmp_ref, x):
    half = x.shape[0] // 2
    for j in range(tmp_ref.shape[0]):
        tmp_ref[j] = x[:, j * LANES:(j + 1) * LANES]
    pick = lambda start: jnp.concatenate(
        [tmp_ref[j, pl.ds(start, half, stride=2), :] for j in range(tmp_ref.shape[0])], axis=1)
    return pick(0), pick(1)


def _merge_parity(tmp_ref, even, odd):
    half = even.shape[0]
    for j in range(tmp_ref.shape[0]):
        tmp_ref[j, pl.ds(0, half, stride=2), :] = even[:, j * LANES:(j + 1) * LANES]
        tmp_ref[j, pl.ds(1, half, stride=2), :] = odd[:, j * LANES:(j + 1) * LANES]
    return jnp.concatenate([tmp_ref[j] for j in range(tmp_ref.shape[0])], axis=1)


def _filter_time_kernel(feats_ref, w1_ref, b1_ref, w2_ref, b2_ref, fr_ref, w3_ref, t_ref, delta_ref,
                        g_ref, hq_ref, tmp_ref, h_ref):
    hp = lax.Precision.HIGHEST
    length = feats_ref.shape[0]
    half = length // 2

    @pl.when(pl.program_id(0) == 0)
    def _():
        fr = fr_ref[...]
        h1 = jnp.sin(fr * (jnp.dot(feats_ref[...], w1_ref[...], preferred_element_type=F32, precision=hp)
                           + b1_ref[...]))
        h_ref[...] = jnp.sin(fr * (jnp.dot(h1, w2_ref[...], preferred_element_type=F32, precision=hp) + b2_ref[...]))

    h = h_ref[...]
    mod =jnp.exp(-t_ref[...] * delta_ref[...]) + MOD_SHIFT
    row = lax.broadcasted_iota(jnp.int32, (length, 1), 0)
    alt = jnp.where(lax.broadcasted_iota(jnp.int32, (half, 1), 0) % 2 == 0, 1.0, -1.0).astype(F32)
    for o in range(2):
        fwd = jnp.dot(h, w3_ref[2 * o], preferred_element_type=F32, precision=hp) * mod
        bwd = jnp.dot(h, w3_ref[2 * o + 1], preferred_element_type=F32, precision=hp) * mod
        bwd = jnp.where(row == 0, 0.0, bwd)
        norm = (jnp.sum(jnp.abs(fwd), axis=0, keepdims=True)
                + jnp.sum(jnp.abs(bwd), axis=0, keepdims=True) + 1e-6)
        fwd = fwd / norm
        bwd = bwd / norm
        for part, g in enumerate((fwd + bwd, fwd - bwd)):
            even, odd = _split_parity(tmp_ref, g)
            g_ref[o, 2 * part] = even.astype(BF16)
            g_ref[o, 2 * part + 1] = odd.astype(BF16)
            mid = jnp.sum((even if part == 0 else odd) * alt, axis=0, keepdims=True) * (1.0 / length)
            hq_ref[o, part] = mid if part == 0 else -mid


def _filter_freq_kernel(ce_ref, se_ref, co_ref, so_ref, g_ref, h_ref):
    half = ce_ref.shape[0]
    row = lax.broadcasted_iota(jnp.int32, (half, 1), 0)
    scale = jnp.where(row == 0, 1.0, 2.0).astype(F32) * (1.0 / (4 * half))
    a = jnp.dot(ce_ref[...], g_ref[0, 0], preferred_element_type=F32)
    b = jnp.dot(co_ref[...], g_ref[0, 1], preferred_element_type=F32)
    c = jnp.dot(se_ref[...], g_ref[0, 2], preferred_element_type=F32)
    d = jnp.dot(so_ref[...], g_ref[0, 3], preferred_element_type=F32)
    h_ref[0, 0] = (a + b) * scale
    h_ref[0, 1] = -(c + d) * scale
    h_ref[0, 2] = (a - b) * scale
    h_ref[0, 3] = (c - d) * scale


def _hyena_filters(length, tables, w1, b1, w2, b2, w3, freq):
    t = np.linspace(0.0, 1.0, length, dtype=np.float32)[:, None]
    f = np.linspace(1e-4, FILT_BANDS - 1, FILT_BANDS, dtype=np.float32)[None]
    ang = np.float32(2.0 * math.pi / length) * np.arange(length, dtype=np.float32)[:, None] * f
    feats = np.concatenate([t, np.cos(ang), -np.sin(ang)], axis=-1).astype(np.float32)
    deltas = np.abs(np.linspace(math.log(DECAY_TARGET) / SLOW_DECAY_PCT,
                                math.log(DECAY_TARGET) / FAST_DECAY_PCT, HY_C, dtype=np.float32))[None]
    hid = w1.shape[1]
    w3r = w3.reshape(hid, 4, HY_C).transpose(1, 0, 2)
    tc = HY_TC
    full = lambda shape: pl.BlockSpec(shape, lambda c: (0,) * len(shape))
    half = length // 2
    g, hq = pl.pallas_call(
        _filter_time_kernel,
        out_shape=(jax.ShapeDtypeStruct((2, 4, half, HY_C), BF16),
                   jax.ShapeDtypeStruct((2, 2, 1, HY_C), F32)),
        grid=(HY_C // tc,),
        in_specs=[full(feats.shape), full(w1.shape), full((1, hid)), full(w2.shape), full((1, hid)),
                  full((1, hid)), pl.BlockSpec((4, hid, tc), lambda c: (0, 0, c)), full((length, 1)),
                  pl.BlockSpec((1, tc), lambda c: (0, c))],
        out_specs=(pl.BlockSpec((2, 4, half, tc), lambda c: (0, 0, 0, c)),
                   pl.BlockSpec((2, 2, 1, tc), lambda c: (0, 0, 0, c))),
        scratch_shapes=[pltpu.VMEM((tc // LANES, length, LANES), F32), pltpu.VMEM((length, hid), F32)],
        compiler_params=_params("arbitrary"),
        name="filter_time",
    )(feats, w1, b1.reshape(1, hid), w2, b2.reshape(1, hid), freq.reshape(1, hid), w3r, t, deltas)
    ce, se, co, so = tables[:4]
    table_spec = _resident((half, half), lambda o, c: (0, 0))
    spectra = pl.pallas_call(
        _filter_freq_kernel,
        out_shape=jax.ShapeDtypeStruct((2, 4, half, HY_C), F32),
        grid=(2, HY_C // tc),
        in_specs=[table_spec] * 4 + [pl.BlockSpec((1, 4, half, tc), lambda o, c: (o, 0, 0, c))],
        out_specs=pl.BlockSpec((1, 4, half, tc), lambda o, c: (o, 0, 0, c)),
        compiler_params=_params("parallel", "parallel"),
        name="filter_freq",
    )(ce, se, co, so, g)
    return spectra, hq


def _hyena_kernel(z0_ref, z1_ref, z2_ref, cw_ref, cb_ref, ce_ref, se_ref, co_ref, so_ref, cot_ref, sot_ref,
                  h_ref, hq_ref, skip_ref, o_ref, tmp_ref):
    length = z0_ref.shape[1]
    half = length // 2
    row = lax.broadcasted_iota(jnp.int32, (length, 1), 0)
    alt = jnp.where(lax.broadcasted_iota(jnp.int32, (half, 1), 0) % 2 == 0, 1.0, -1.0).astype(F32)

    def short_conv(z_ref, c):
        u = z_ref[0].astype(F32)
        prev = jnp.where(row == 0, 0.0, pltpu.roll(u, 1, axis=0))
        nxt = jnp.where(row == length - 1, 0.0, pltpu.roll(u, length - 1, axis=0))
        return cb_ref[c] + (prev * cw_ref[0, c] + u * cw_ref[1, c] + nxt * cw_ref[2, c])

    def long_conv(u, o):
        ue, uo = _split_parity(tmp_ref, u)
        ue_bf, uo_bf = ue.astype(BF16), uo.astype(BF16)
        a4 = jnp.sum(ue * alt, axis=0, keepdims=True)
        b4 = jnp.sum(uo * alt, axis=0, keepdims=True)
        hr4, hi4 = hq_ref[o, 0], hq_ref[o, 1]
        y_even = alt * (a4 * hr4 + b4 * hi4)
        y_odd = -(alt * (a4 * hi4 - b4 * hr4))
        for kt in range(half // HY_KT):
            ks = slice(kt * HY_KT, (kt + 1) * HY_KT)
            ae = jnp.dot(ce_ref[ks, :], ue_bf, preferred_element_type=F32)
            ao = jnp.dot(co_ref[ks, :], uo_bf, preferred_element_type=F32)
            be = jnp.dot(se_ref[ks, :], ue_bf, preferred_element_type=F32)
            bo = jnp.dot(so_ref[ks, :], uo_bf, preferred_element_type=F32)
            a_f, a_g, b_f, b_g = ae + ao, ae - ao, be + bo, bo - be
            hr_f, hi_f, hr_g, hi_g = (h_ref[o, j, ks, :] for j in range(4))
            re_f, im_f = a_f * hr_f + b_f * hi_f, a_f * hi_f - b_f * hr_f
            re_g, im_g = a_g * hr_g + b_g * hi_g, a_g * hi_g - b_g * hr_g
            y_even += (jnp.dot(ce_ref[:, ks], (re_f + re_g).astype(BF16), preferred_element_type=F32)
                       - jnp.dot(se_ref[:, ks], (im_f - im_g).astype(BF16), preferred_element_type=F32))
            y_odd += (jnp.dot(cot_ref[:, ks], (re_f - re_g).astype(BF16), preferred_element_type=F32)
                      - jnp.dot(sot_ref[:, ks], (im_f + im_g).astype(BF16), preferred_element_type=F32))
        return _merge_parity(tmp_ref, y_even, y_odd) + u * skip_ref[o]

    z = short_conv(z1_ref, 1) * long_conv(short_conv(z0_ref, 0), 0)
    o_ref[0] = (short_conv(z2_ref, 2) * long_conv(z, 1)).astype(o_ref.dtype)


def _hyena(z3, conv_w, conv_b, tables, spectra, hq, skip):
    b, s, _ = z3.shape
    half = s // 2
    tc = HY_TC
    nct = HY_C // tc
    cw = conv_w.reshape(3, 3, 1, HY_C)
    cb = conv_b.reshape(3, 1, HY_C)
    zspec = lambda chunk: pl.BlockSpec((1, s, tc), lambda c, i: (i, 0, chunk * nct + c))
    return pl.pallas_call(
        _hyena_kernel,
        out_shape=jax.ShapeDtypeStruct((b, s, HY_C), BF16),
        grid=(nct, b),
        in_specs=[zspec(0), zspec(1), zspec(2),
                  pl.BlockSpec((3, 3, 1, tc), lambda c, i: (0, 0, 0, c)),
                  pl.BlockSpec((3, 1, tc), lambda c, i: (0, 0, c))]
                 + [_resident((half, half), lambda c, i: (0, 0))] * 6
                 + [_resident((2, 4, half, tc), lambda c, i: (0, 0, 0, c)),
                    pl.BlockSpec((2, 2, 1, tc), lambda c, i: (0, 0, 0, c)),
                    pl.BlockSpec((2, 1, tc), lambda c, i: (0, 0, c))],
        out_specs=pl.BlockSpec((1, s, tc), lambda c, i: (i, 0, c)),
        scratch_shapes=[pltpu.VMEM((tc // LANES, s, LANES), F32)],
        compiler_params=_params("parallel", "parallel"),
        cost_estimate=pl.CostEstimate(flops=2 * 4 * 2 * b * s * half * HY_C, transcendentals=0,
                                      bytes_accessed=2 * 3 * b * s * HY_C + 4 * b * s * HY_C + 12 * half * half),
        name="hyena",
    )(z3, z3, z3, cw, cb, *tables, spectra, hq, skip.reshape(2, 1, HY_C))


def _rel_bucket(rel):
    half = NUM_BUCKETS // 2
    exact = half // 2
    n = np.abs(rel)
    large = exact + (np.log(np.maximum(n, 1) / exact) / np.log(REL_MAX_DIST / exact) * (half - exact)).astype(np.int32)
    large = np.minimum(large, half - 1)
    return (np.where(rel > 0, half, 0) + np.where(n < exact, n, large)).astype(np.int32)


def _att_tiles(length):
    kw = min(ATT_KW, length)
    tiles = []
    for qs in range(0, length, ATT_TQ):
        ks = min(max(qs - (kw - ATT_TQ) // 2, 0), length - kw)
        tiles.append((qs, ks, {0: 0, -64: 1, -128: 2}[ks - qs]))
    return kw, tiles


def _att_bias(rel_bias, group, seq):
    window, dil = AT_GROUPS[group]
    band = window // (2 * dil)
    kw, tiles = _att_tiles(seq // dil)
    offsets = sorted({ks - qs for qs, ks, _ in tiles}, reverse=True)
    table = rel_bias[:, group * AT_HEADS:(group + 1) * AT_HEADS].astype(F32)
    span = kw + ATT_TQ
    k = np.arange(span)
    rel = np.asarray(offsets)[:, None] + np.where(k < kw, k, k - span)[None, :]
    diag = jnp.where((np.abs(rel) <= band)[:, None, :],
                     jnp.swapaxes(table[_rel_bucket(rel * dil)], 1, 2), NEG_INF)
    return diag[:, :, None, :]


def _bias_tile(diag_ref, var, kw):
    rows = jnp.broadcast_to(diag_ref[var, 0], (ATT_TQ, diag_ref.shape[3]))
    return pltpu.roll(rows, 0, 1, stride=1, stride_axis=0)[:, :kw]


def _dilated_kernel(q1, q2, q3, k1, k2, k3, v1, v2, v3, b1, b2, b3, o_ref,
                    qn_ref, kn_ref, vn_ref, og_ref, lg_ref):
    seq = o_ref.shape[1]
    contract_last = (((1,), (1,)), ((), ()))
    for g, (q_ref, k_ref, v_ref, bias_ref) in enumerate(((q1, k1, v1, b1), (q2, k2, v2, b2), (q3, k3, v3, b3))):
        dil = AT_GROUPS[g][1]
        kw, tiles = _att_tiles(seq // dil)
        if dil > 1:
            qn_ref[...] = q_ref[0].astype(F32)
            kn_ref[...] = k_ref[0].astype(F32)
            vn_ref[...] = v_ref[0].astype(F32)
        bias = {var: _bias_tile(bias_ref, var, kw) for var in sorted({var for _, _, var in tiles})}
        for r in range(dil):
            for qs, ks, var in tiles:
                rows = lambda start, size: (pl.ds(r + start * dil, size, stride=dil) if dil > 1
                                            else pl.ds(start, size))
                if dil > 1:
                    qt = qn_ref[rows(qs, ATT_TQ), :].astype(BF16)
                    kt = kn_ref[rows(ks, kw), :].astype(BF16)
                    vt = vn_ref[rows(ks, kw), :].astype(BF16)
                else:
                    qt, kt, vt = q_ref[0, rows(qs, ATT_TQ), :], k_ref[0, rows(ks, kw), :], v_ref[0, rows(ks, kw), :]
                s = lax.dot_general(qt, kt, contract_last, preferred_element_type=F32) + bias[var]
                m = jnp.max(s, axis=-1, keepdims=True)
                p = jnp.exp(s - m)
                z = jnp.sum(p, axis=-1, keepdims=True)
                og_ref[g, rows(qs, ATT_TQ), :] = jnp.dot(p.astype(BF16), vt, preferred_element_type=F32) / z
                lg_ref[g, rows(qs, ATT_TQ), :] = jnp.broadcast_to(m + jnp.log(z), (ATT_TQ, HD))
    l0, l1, l2 = lg_ref[0], lg_ref[1], lg_ref[2]
    mx = jnp.maximum(jnp.maximum(l0, l1), l2)
    e0, e1, e2 = jnp.exp(l0 - mx), jnp.exp(l1 - mx), jnp.exp(l2 - mx)
    o_ref[0] = ((e0 * og_ref[0] + e1 * og_ref[1] + e2 * og_ref[2]) / (e0 + e1 + e2)).astype(o_ref.dtype)


def _dilated_attention(z3, rel_bias):
    b, s, _ = z3.shape
    ng = len(AT_GROUPS)
    col = lambda part, g: pl.BlockSpec((1, s, HD), lambda i, h: (i, 0, part * ng * AT_HEADS + g * AT_HEADS + h))
    biases = [_att_bias(rel_bias, g, s) for g in range(ng)]
    bias_spec = lambda a: pl.BlockSpec((a.shape[0], 1) + a.shape[2:], lambda i, h: (0, h, 0, 0))
    return pl.pallas_call(
        _dilated_kernel,
        out_shape=jax.ShapeDtypeStruct((b, s, AT_HEADS * HD), BF16),
        grid=(b, AT_HEADS),
        in_specs=[col(p, g) for p in range(3) for g in range(ng)] + [bias_spec(a) for a in biases],
        out_specs=pl.BlockSpec((1, s, HD), lambda i, h: (i, 0, h)),
        scratch_shapes=[pltpu.VMEM((s, HD), F32), pltpu.VMEM((s, HD), F32), pltpu.VMEM((s, HD), F32),
                        pltpu.VMEM((ng, s, HD), F32), pltpu.VMEM((ng, s, HD), F32)],
        compiler_params=_params("parallel", "parallel"),
        name="dilated_attention",
    )(*([z3] * 9), *biases)


ROUTER_LANES = 128


def _router_kernel(x_ref, g_ref, w_ref, hp_ref, idx_ref, gate_ref, count_ref, carry_ref):
    tm = x_ref.shape[0]
    half = hp_ref.shape[1]

    @pl.when(pl.program_id(0) == 0)
    def _():
        carry_ref[...] = jnp.zeros_like(carry_ref)

    h = _rms_rows(x_ref[...], g_ref[...])
    hp_ref[...] = _bf16_bits(h[:, :half]) | (_bf16_bits(h[:, half:]) << 16)
    h_hi = h.astype(BF16)
    h_lo = (h - h_hi.astype(F32)).astype(BF16)
    logits = (jnp.dot(h_hi, w_ref[0], preferred_element_type=F32)
              + (jnp.dot(h_hi, w_ref[1], preferred_element_type=F32)
                 + jnp.dot(h_lo, w_ref[0], preferred_element_type=F32)))
    lane = lax.broadcasted_iota(jnp.int32, logits.shape, 1).astype(F32)
    logits = jnp.where(lane < N_EXPERTS, logits, -jnp.inf)
    m1 = jnp.max(logits, axis=-1, keepdims=True)
    i1 = jnp.min(jnp.where(logits == m1, lane, float(ROUTER_LANES)), axis=-1, keepdims=True)
    rest = jnp.where(lane == i1, -jnp.inf, logits)
    m2 = jnp.max(rest, axis=-1, keepdims=True)
    i2 = jnp.min(jnp.where(rest == m2, lane, float(ROUTER_LANES)), axis=-1, keepdims=True)
    e2 = jnp.exp(m2 - m1)
    den = 1.0 + e2
    gate_ref[...] = jnp.where(lane == 0, 1.0 / den, e2 / den)
    chosen = jnp.where((lane == i1) | (lane == i2), 1.0, 0.0)
    earlier = lax.broadcasted_iota(jnp.int32, (tm, tm), 0) > lax.broadcasted_iota(jnp.int32, (tm, tm), 1)
    carry = carry_ref[...]
    before = jnp.dot(jnp.where(earlier, 1.0, 0.0).astype(BF16), chosen.astype(BF16),
                     preferred_element_type=F32) + carry
    r1 = jnp.sum(jnp.where(lane == i1, before, 0.0), axis=-1, keepdims=True)
    r2 = jnp.sum(jnp.where(lane == i2, before, 0.0), axis=-1, keepdims=True)
    idx_ref[...] = jnp.where(lane == 0, i1, jnp.where(lane == 1, i2, jnp.where(lane == 2, r1, r2))).astype(jnp.int32)
    carry = carry + jnp.sum(chosen, axis=0, keepdims=True)
    carry_ref[...] = carry
    count_ref[...] = carry.astype(jnp.int32)


def _router(x2, gain, router):
    rows, d = x2.shape
    tm = MOE_CHUNK
    w = jnp.zeros((d, ROUTER_LANES), F32).at[:, :N_EXPERTS].set(router.astype(F32))
    w_hi = w.astype(BF16)
    w = jnp.stack([w_hi, (w - w_hi.astype(F32)).astype(BF16)])
    row_spec = lambda width: pl.BlockSpec((tm, width), lambda i: (i, 0))
    return pl.pallas_call(
        _router_kernel,
        out_shape=(jax.ShapeDtypeStruct((rows, d // 2), jnp.uint32),
                   jax.ShapeDtypeStruct((rows, ROUTER_LANES), jnp.int32),
                   jax.ShapeDtypeStruct((rows, ROUTER_LANES), F32),
                   jax.ShapeDtypeStruct((1, ROUTER_LANES), jnp.int32)),
        grid=(rows // tm,),
        in_specs=[row_spec(d),
                  pl.BlockSpec((1, d), lambda i: (0, 0)),
                  pl.BlockSpec((2, d, ROUTER_LANES), lambda i: (0, 0, 0))],
        out_specs=(row_spec(d // 2), row_spec(ROUTER_LANES), row_spec(ROUTER_LANES),
                   pl.BlockSpec((1, ROUTER_LANES), lambda i: (0, 0))),
        scratch_shapes=[pltpu.VMEM((1, ROUTER_LANES), F32)],
        compiler_params=_params("arbitrary"),
        name="router",
    )(x2, gain.reshape(1, d), w)


def _sc_workers():
    info = plsc.get_sparse_core_info()
    mesh = plsc.VectorSubcoreMesh(core_axis_name="core", subcore_axis_name="subcore")
    return mesh, info.num_cores, info.num_subcores


def _pack_weight_rows(w, col_tile):
    r, c = w.shape
    mesh, nc, ns = _sc_workers()
    rs, ncol = SC_PACK_ROWS, c // col_tile
    per_w = (r // rs) * ncol // (nc * ns)
    assert per_w * nc * ns * rs * col_tile == r * c and per_w % 2 == 0 and col_tile % SC_LANES == 0
    params = pltpu.CompilerParams()
    if "needs_layout_passes" in pltpu.CompilerParams.__dataclass_fields__:
        params = dataclasses.replace(params, needs_layout_passes=False)

    @functools.partial(
        pl.kernel, mesh=mesh, out_type=jax.ShapeDtypeStruct((r // 2, c), jnp.uint32), compiler_params=params,
        cost_estimate=pl.CostEstimate(flops=r * c, transcendentals=0, bytes_accessed=6 * r * c),
        scratch_types=[pltpu.VMEM((2, rs, col_tile), F32), pltpu.VMEM((2, rs // 2, col_tile), jnp.uint32),
                       pltpu.SemaphoreType.DMA((2,)), pltpu.SemaphoreType.DMA((2,))])
    def pack(w_hbm, o_hbm, in_v, out_v, rsem, wsem):
        wid = lax.axis_index("subcore") * nc + lax.axis_index("core")

        @pl.loop(0, per_w // 2)
        def _(it):
            reads, writes = [], []
            for b in range(2):
                tile = wid * per_w + 2 * it + b
                r0 = pl.multiple_of((tile // ncol) * rs, rs)
                c0 = pl.multiple_of((tile % ncol) * col_tile, col_tile)
                reads.append(pltpu.make_async_copy(w_hbm.at[pl.ds(r0, rs), pl.ds(c0, col_tile)], in_v.at[b],
                                                   rsem.at[b]))
                writes.append(pltpu.make_async_copy(
                    out_v.at[b], o_hbm.at[pl.ds(pl.multiple_of(r0 // 2, rs // 2), rs // 2), pl.ds(c0, col_tile)],
                    wsem.at[b]))
            reads[0].start()
            reads[1].start()
            for b in range(2):
                reads[b].wait()
                for pair in range(rs // 2):
                    @plsc.parallel_loop(0, col_tile, step=SC_LANES, unroll=8)
                    def _(j):
                        packed = plsc.pack(in_v[b, 2 * pair, pl.ds(j, SC_LANES)],
                                           in_v[b, 2 * pair + 1, pl.ds(j, SC_LANES)],
                                           format=plsc.PackFormat.INTERLEAVED)
                        out_v[b, pair, pl.ds(j, SC_LANES)] = plsc.bitcast(packed, jnp.uint32)
                writes[b].start()
            writes[0].wait()
            writes[1].wait()

    return pack(w)


def _sc_token_rows(t, nc, ns, ch):
    per_w = t // (nc * ns)
    assert per_w * nc * ns == t and per_w % (2 * ch) == 0
    return per_w, per_w // ch


def _scatter_rows(table, dests, p_rows, after=()):
    t, w = table.shape
    nk = len(dests)
    mesh, nc, ns = _sc_workers()
    ch = SC_SCATTER_ROWS
    per_w, nit = _sc_token_rows(t, nc, ns, ch)

    @functools.partial(
        pl.kernel, mesh=mesh, out_type=jax.ShapeDtypeStruct((p_rows, w), table.dtype),
        scratch_types=[pltpu.VMEM((nk, nit, ch), jnp.int32), pltpu.VMEM((2, ch, w), table.dtype),
                       pltpu.SemaphoreType.DMA((2,)), pltpu.SemaphoreType.DMA((2, nk))])
    def scatter(table_hbm, *refs):
        dest_hbm, out_hbm = refs[:nk], refs[nk + len(after)]
        idx_v, rows_v, rsem, wsem = refs[nk + len(after) + 1:]
        wid = lax.axis_index("subcore") * nc + lax.axis_index("core")
        for k in range(nk):
            pltpu.sync_copy(dest_hbm[k].at[pl.ds(wid * nit, nit)], idx_v.at[k])

        @pl.loop(0, nit // 2)
        def _(it):
            reads = [pltpu.make_async_copy(table_hbm.at[pl.ds(wid * per_w + (2 * it + b) * ch, ch)],
                                           rows_v.at[b], rsem.at[b]) for b in range(2)]
            writes = [[pltpu.make_async_copy(rows_v.at[b], out_hbm.at[idx_v.at[k].at[2 * it + b]], wsem.at[b, k])
                       for k in range(nk)] for b in range(2)]
            reads[0].start()
            reads[1].start()
            for b in range(2):
                reads[b].wait()
                for k in range(nk):
                    writes[b][k].start()
            for b in range(2):
                for k in range(nk):
                    writes[b][k].wait()

    return scatter(table, *[d.reshape(t // ch, ch) for d in dests], *after)


def _gather_rows(table, idxs):
    t = idxs[0].shape[0]
    w = table.shape[1]
    nk = len(idxs)
    mesh, nc, ns = _sc_workers()
    ch = SC_ROWS
    per_w, nit = _sc_token_rows(t, nc, ns, ch)

    @functools.partial(
        pl.kernel, mesh=mesh, out_type=[jax.ShapeDtypeStruct((t, w), table.dtype)] * nk,
        scratch_types=[pltpu.VMEM((nk, nit, ch), jnp.int32), pltpu.VMEM((nk, 2, ch, w), table.dtype),
                       pltpu.SemaphoreType.DMA((nk, 2)), pltpu.SemaphoreType.DMA((nk, 2))])
    def gather(table_hbm, *refs):
        idx_hbm, out_hbm = refs[:nk], refs[nk:2 * nk]
        idx_v, rows_v, rsem, wsem = refs[2 * nk:]
        wid = lax.axis_index("subcore") * nc + lax.axis_index("core")
        for k in range(nk):
            pltpu.sync_copy(idx_hbm[k].at[pl.ds(wid * nit, nit)], idx_v.at[k])

        @pl.loop(0, nit // 2)
        def _(it):
            slots = [(k, b) for k in range(nk) for b in range(2)]
            reads = {(k, b): pltpu.make_async_copy(table_hbm.at[idx_v.at[k].at[2 * it + b]], rows_v.at[k, b],
                                                   rsem.at[k, b]) for k, b in slots}
            writes = {(k, b): pltpu.make_async_copy(rows_v.at[k, b],
                                                    out_hbm[k].at[pl.ds(wid * per_w + (2 * it + b) * ch, ch)],
                                                    wsem.at[k, b]) for k, b in slots}
            for s in slots:
                reads[s].start()
            for s in slots:
                reads[s].wait()
                writes[s].start()
            for s in slots:
                writes[s].wait()

    return gather(table, *[i.reshape(t // ch, ch) for i in idxs])


def _combine_kernel(x_ref, y0_ref, y1_ref, gate_ref, *rest):
    o_ref = rest[-1]
    half = x_ref.shape[1] // 2
    for part in range(2):
        cols = slice(part * half, (part + 1) * half)
        acc = x_ref[:, cols]
        for k, y_ref in enumerate((y0_ref, y1_ref)):
            word = y_ref[...]
            bits = (word << 16) if part == 0 else (word & jnp.uint32(0xFFFF0000))
            acc = acc + gate_ref[:, k:k + 1] * lax.bitcast_convert_type(bits, F32)
        o_ref[:, cols] = acc


def _combine(x2, y0, y1, gates, part, earlier=None):
    t, d = x2.shape
    tm = ROW_TILE
    steps = y0.shape[0] // tm
    here = lambda i: (part * steps + i, 0)
    in_specs = [pl.BlockSpec((tm, d), here),
                pl.BlockSpec((tm, d // 2), lambda i: (i, 0)),
                pl.BlockSpec((tm, d // 2), lambda i: (i, 0)),
                pl.BlockSpec((tm, TOP_K), here)]
    operands = [x2, y0, y1, gates]
    if earlier is not None:
        in_specs.append(pl.BlockSpec(memory_space=pl.ANY))
        operands.append(earlier)
    return pl.pallas_call(
        _combine_kernel,
        out_shape=jax.ShapeDtypeStruct((t, d), F32),
        grid=(steps,),
        in_specs=in_specs,
        out_specs=pl.BlockSpec((tm, d), here),
        input_output_aliases={} if earlier is None else {len(operands) - 1: 0},
        compiler_params=_params("parallel"),
        name="moe_combine",
    )(*operands)


def _moe(x2, gain, router, wg, wu, wd):
    t, d = x2.shape
    hp, idx, gate, count = _router(x2, gain, router)
    experts = jnp.arange(N_EXPERTS, dtype=jnp.int32)
    counts = count[0, :N_EXPERTS]
    padded = (counts + MOE_ROWS - 1) // MOE_ROWS * MOE_ROWS
    pend = jnp.cumsum(padded)
    pstart = pend - padded
    expert, rank = idx[:, :TOP_K], idx[:, TOP_K:2 * TOP_K]
    dest = jnp.sum(jnp.where(expert[:, :, None] == experts, pstart, 0), axis=-1) + rank
    dests = [dest[:, k] for k in range(TOP_K)]
    p_rows = t * TOP_K + N_EXPERTS * MOE_ROWS
    blk_row = jnp.arange(p_rows // MOE_ROWS, dtype=jnp.int32) * MOE_ROWS
    blk_expert = jnp.minimum(jnp.sum(pend[None, :] <= blk_row[:, None], axis=1), N_EXPERTS - 1).astype(jnp.int32)
    valid = jnp.clip(counts[blk_expert] - (blk_row - pstart[blk_expert]), 0, MOE_ROWS)
    valid = jnp.where(blk_row < pend[-1], valid, 0).astype(jnp.int32)
    n_exp, _, f = wg.shape
    pack = lambda w, col_tile: _pack_weight_rows(w.reshape(-1, w.shape[2]), col_tile).reshape(n_exp, -1, w.shape[2])
    wg, wu, wd = pack(wg, f // 2), pack(wu, f // 2), pack(wd, d)
    hb = _scatter_rows(hp, dests, p_rows, after=(wg, wu, wd))
    yb = _ffn_experts(hb, wg, wu, wd, blk_expert, valid)
    out = None
    for part in range(MOE_TAIL_PARTS):
        rows = slice(part * t // MOE_TAIL_PARTS, (part + 1) * t // MOE_TAIL_PARTS)
        y0, y1 = _gather_rows(yb, [dk[rows] for dk in dests])
        out = _combine(x2, y0, y1, gate[:, :TOP_K], part, out)
    return out


def kernel(x, mem, rel_bias, norm_mix, norm_mem, norm_ffn, w_mem_kv, xq_norm, xk_norm, w_out, hy_w_in, hy_conv_w, hy_conv_b, hy_filt_w1, hy_filt_b1, hy_filt_w2, hy_filt_b2, hy_filt_w3, hy_sin_freq, hy_skip, at_w_in, at_q_norm, at_k_norm, ffn_w_gate, ffn_w_up, ffn_w_down, moe_router, moe_w_gate, moe_w_up, moe_w_down):
    b, s, d = x.shape
    t = b * s
    m_len = mem.shape[1]
    x2 = x.reshape(t, d)
    mem2 = mem.reshape(b * m_len, d)
    bf = lambda w: w.astype(BF16)
    score_scale = HD ** -0.5
    xa_heads = XA_W // HD
    at_heads = AT_W // HD
    plain = lambda width: [None] * (width // HD)
    kv_gains = lambda i: [xk_norm[i]] * xa_heads + plain(XA_W)

    tables = _dft_tables(s)
    spectra, hq = _hyena_filters(s, tables, hy_filt_w1[0], hy_filt_b1[0], hy_filt_w2[0], hy_filt_b2[0],
                                 hy_filt_w3[0], hy_sin_freq[0])
    z = _norm_matmul(x2, norm_mix[0], bf(hy_w_in[0]), 1024,
                     plain(3 * HY_C) + [xq_norm[0] * score_scale] * xa_heads).reshape(b, s, -1)
    kv = _norm_matmul(mem2, norm_mem[0], bf(w_mem_kv[0]), 1024, kv_gains(0)).reshape(b, m_len, -1)
    self_out = _hyena(z, hy_conv_w[0], hy_conv_b[0], tables, spectra, hq, hy_skip[0])
    x2 = _mix_out(x2.reshape(b, s, d), self_out, z, 3 * HY_C // XA_W, kv, bf(w_out[0])).reshape(t, d)
    x2 = _ffn_dense(x2, norm_ffn[0], bf(ffn_w_gate[0]), bf(ffn_w_up[0]), bf(ffn_w_down[0]))

    at_gains = ([at_q_norm[0] * score_scale] * at_heads + [at_k_norm[0]] * at_heads + plain(AT_W)
                + [xq_norm[1] * score_scale] * xa_heads)
    z = _norm_matmul(x2, norm_mix[1], bf(at_w_in[0]), 1024, at_gains).reshape(b, s, -1)
    kv = _norm_matmul(mem2, norm_mem[1], bf(w_mem_kv[1]), 1024, kv_gains(1)).reshape(b, m_len, -1)
    self_out = _dilated_attention(z, rel_bias)
    x2 = _mix_out(x2.reshape(b, s, d), self_out, z, 3 * AT_W // XA_W, kv, bf(w_out[1])).reshape(t, d)
    x2 = _moe(x2, norm_ffn[1], moe_router[0], moe_w_gate[0], moe_w_up[0], moe_w_down[0])
    return x2.reshape(b, s, d)
```

```python
import dataclasses
import functools
import math

import jax
import jax.numpy as jnp
import numpy as np
from jax import lax
from jax.experimental import pallas as pl
from jax.experimental.pallas import tpu as pltpu
from jax.experimental.pallas import tpu_sc as plsc

F32 = jnp.float32
BF16 = jnp.bfloat16

D_MODEL = 1024
EPS = 1e-6
HY_C = 512
FILT_BANDS = 16
DECAY_TARGET = 1e-2
FAST_DECAY_PCT = 0.3
SLOW_DECAY_PCT = 1.5
MOD_SHIFT = 0.05
AT_GROUPS = ((128, 1), (512, 4), (2048, 16))
AT_HEADS = 4
HD = 128
AT_W = 1536
NUM_BUCKETS = 32
REL_MAX_DIST = 1024
NEG_INF = -1e30
XA_W = 512
D_FF = 2816
N_EXPERTS = 8
TOP_K = 2

VMEM_LIMIT_BYTES = 56 * 1024 * 1024
ROW_TILE = 1024
PROJ_ROW_TILE = 1024
FF_TILE = 256
MOE_ROWS = 512
MOE_CHUNK = 512
MOE_TAIL_PARTS = 4
SC_ROWS = 32
SC_SCATTER_ROWS = 64
SC_LANES = 16
SC_PACK_ROWS = 16
ATT_TQ = 128
ATT_KW = 256
ATT_BATCH = 8
HY_TC = 256
HY_KT = 1024


def _params(*sem):
    return pltpu.CompilerParams(dimension_semantics=sem, vmem_limit_bytes=VMEM_LIMIT_BYTES)


def _rms_rows(x, gain):
    return x * lax.rsqrt(jnp.mean(x * x, axis=-1, keepdims=True) + EPS) * gain


def _bf16_bits(v):
    u = lax.bitcast_convert_type(v, jnp.uint32)
    return (u + jnp.uint32(0x7FFF) + ((u >> 16) & jnp.uint32(1))) >> 16


def _resident(shape, index_map):
    return pl.BlockSpec(shape, index_map, pipeline_mode=pl.Buffered(1))


def _norm_matmul_kernel(x_ref, g_ref, w_ref, hg_ref, o_ref, *, tn, head_norm):
    h = _rms_rows(x_ref[...], g_ref[...]).astype(BF16)
    for c in range(o_ref.shape[1] // tn):
        acc = jnp.dot(h, w_ref[:, c * tn:(c + 1) * tn], preferred_element_type=F32)
        for j in range(tn // HD):
            cols = slice(c * tn + j * HD, c * tn + (j + 1) * HD)
            seg = acc[:, j * HD:(j + 1) * HD]
            if head_norm[cols.start // HD]:
                seg = _rms_rows(seg, hg_ref[:, cols])
            o_ref[:, cols] = seg.astype(o_ref.dtype)


def _norm_matmul(x2, gain, w_bf, tn, head_gains):
    rows, d = x2.shape
    n = w_bf.shape[1]
    tm = min(PROJ_ROW_TILE, rows)
    assert len(head_gains) * HD == n
    hg = jnp.concatenate([jnp.ones((HD,), F32) if g is None else g.astype(F32) for g in head_gains]).reshape(1, n)
    return pl.pallas_call(
        functools.partial(_norm_matmul_kernel, tn=tn, head_norm=tuple(g is not None for g in head_gains)),
        out_shape=jax.ShapeDtypeStruct((rows, n), BF16),
        grid=(rows // tm,),
        in_specs=[pl.BlockSpec((tm, d), lambda i: (i, 0)),
                  pl.BlockSpec((1, d), lambda i: (0, 0)),
                  _resident((d, n), lambda i: (0, 0)),
                  pl.BlockSpec((1, n), lambda i: (0, 0))],
        out_specs=pl.BlockSpec((tm, n), lambda i: (i, 0)),
        compiler_params=_params("parallel"),
        cost_estimate=pl.CostEstimate(flops=2 * rows * d * n, transcendentals=rows,
                                      bytes_accessed=4 * rows * d + 2 * d * n + 2 * rows * n),
        name="norm_matmul",
    )(x2, gain.reshape(1, d), w_bf, hg)


def _mix_out_kernel(x_ref, a_ref, xq_ref, kv_ref, wa_ref, wc_ref, o_ref):
    heads = []
    for h in range(XA_W // HD):
        cols = slice(h * HD, (h + 1) * HD)
        s = lax.dot_general(xq_ref[0, :, cols], kv_ref[0, :, cols], (((1,), (1,)), ((), ())),
                            preferred_element_type=F32)
        p = jnp.exp(s - jnp.max(s, axis=-1, keepdims=True))
        z = jnp.sum(p, axis=-1, keepdims=True)
        pv = jnp.dot(p.astype(BF16), kv_ref[0, :, XA_W + h * HD:XA_W + (h + 1) * HD], preferred_element_type=F32)
        heads.append((pv / z).astype(BF16))
    cross = jnp.concatenate(heads, axis=1)
    o_ref[0] = (x_ref[0]
                + jnp.dot(a_ref[0], wa_ref[...], preferred_element_type=F32)
                + jnp.dot(cross, wc_ref[...], preferred_element_type=F32))


def _mix_out(x3, self_out, z3, xq_block, kv3, w_bf):
    b, s, d = x3.shape
    half = self_out.shape[2]
    m = kv3.shape[1]
    tm = ROW_TILE
    return pl.pallas_call(
        _mix_out_kernel,
        out_shape=jax.ShapeDtypeStruct((b, s, d), F32),
        grid=(b, s // tm),
        in_specs=[pl.BlockSpec((1, tm, d), lambda i, j: (i, j, 0)),
                  pl.BlockSpec((1, tm, half), lambda i, j: (i, j, 0)),
                  pl.BlockSpec((1, tm, XA_W), lambda i, j: (i, j, xq_block)),
                  pl.BlockSpec((1, m, 2 * XA_W), lambda i, j: (i, 0, 0)),
                  pl.BlockSpec((half, d), lambda i, j: (0, 0)),
                  pl.BlockSpec((d - half, d), lambda i, j: (1, 0))],
        out_specs=pl.BlockSpec((1, tm, d), lambda i, j: (i, j, 0)),
        compiler_params=_params("parallel", "parallel"),
        name="mix_out",
    )(x3, self_out, z3, kv3, w_bf, w_bf)


def _weight_rows(ref, start, size):
    if ref.dtype == jnp.uint32:
        return lambda cols: pltpu.bitcast(ref[start // 2:(start + size) // 2, cols], BF16)
    return lambda cols: ref[start:start + size, cols]


def _weight_shape(ref):
    return (ref.shape[0] * (2 if ref.dtype == jnp.uint32 else 1), ref.shape[1])


def _swiglu(h, wg_ref, wu_ref, wd_ref, tf):
    d, f = _weight_shape(wg_ref)
    assert f % tf == 0
    y = None
    for j in range(f // tf):
        cols = slice(j * tf, (j + 1) * tf)
        gg = jnp.dot(h, _weight_rows(wg_ref, 0, d)(cols), preferred_element_type=F32)
        uu = jnp.dot(h, _weight_rows(wu_ref, 0, d)(cols), preferred_element_type=F32)
        a = ((gg * jax.nn.sigmoid(gg)) * uu).astype(BF16)
        part = jnp.dot(a, _weight_rows(wd_ref, j * tf, tf)(slice(None)), preferred_element_type=F32)
        y = part if y is None else y + part
    return y


def _ffn_dense_kernel(x_ref, g_ref, wg_ref, wu_ref, wd_ref, o_ref, *, tf):
    x = x_ref[...]
    h = _rms_rows(x, g_ref[...]).astype(BF16)
    o_ref[...] = x + _swiglu(h, wg_ref, wu_ref, wd_ref, tf)


def _ffn_dense(x2, gain, wg, wu, wd):
    rows, d = x2.shape
    f = wg.shape[1]
    tm = ROW_TILE
    return pl.pallas_call(
        functools.partial(_ffn_dense_kernel, tf=FF_TILE),
        out_shape=jax.ShapeDtypeStruct((rows, d), F32),
        grid=(rows // tm,),
        in_specs=[pl.BlockSpec((tm, d), lambda i: (i, 0)),
                  pl.BlockSpec((1, d), lambda i: (0, 0)),
                  _resident((d, f), lambda i: (0, 0)),
                  _resident((d, f), lambda i: (0, 0)),
                  _resident((f, d), lambda i: (0, 0))],
        out_specs=pl.BlockSpec((tm, d), lambda i: (i, 0)),
        compiler_params=_params("parallel"),
        cost_estimate=pl.CostEstimate(flops=6 * rows * d * f, transcendentals=rows * f,
                                      bytes_accessed=8 * rows * d + 6 * d * f),
        name="ffn_dense",
    )(x2, gain.reshape(1, d), wg, wu, wd)


def _ffn_expert_kernel(eid_ref, valid_ref, hp_ref, wg_ref, wu_ref, wd_ref, o_ref, h_ref, *, tf):
    half = hp_ref.shape[1]
    valid = valid_ref[pl.program_id(0)]

    @pl.when(valid > 0)
    def _():
        keep = lax.broadcasted_iota(jnp.int32, (hp_ref.shape[0], 1), 0) < valid
        word = hp_ref[...]
        h_ref[:, :half] = jnp.where(keep, lax.bitcast_convert_type(word << 16, F32), 0.0).astype(BF16)
        h_ref[:, half:] = jnp.where(keep, lax.bitcast_convert_type(word & jnp.uint32(0xFFFF0000), F32),
                                    0.0).astype(BF16)
        y = _swiglu(h_ref[...], wg_ref.at[0], wu_ref.at[0], wd_ref.at[0], tf)
        o_ref[...] = _bf16_bits(y[:, :half]) | (_bf16_bits(y[:, half:]) << 16)

    @pl.when(valid <= 0)
    def _():
        o_ref[...] = jnp.zeros_like(o_ref)


def _ffn_experts(hp, wg, wu, wd, eid, valid):
    rows, half = hp.shape
    d = 2 * half
    tm = MOE_ROWS
    expert_spec = lambda w: pl.BlockSpec((1,) + w.shape[1:], lambda i, e, n: (e[i], 0, 0))
    grid_spec = pltpu.PrefetchScalarGridSpec(
        num_scalar_prefetch=2,
        grid=(rows // tm,),
        in_specs=[pl.BlockSpec((tm, half), lambda i, e, n: (i, 0)),
                  expert_spec(wg), expert_spec(wu), expert_spec(wd)],
        out_specs=pl.BlockSpec((tm, half), lambda i, e, n: (i, 0)),
        scratch_shapes=[pltpu.VMEM((tm, d), BF16)],
    )
    return pl.pallas_call(
        functools.partial(_ffn_expert_kernel, tf=FF_TILE),
        out_shape=jax.ShapeDtypeStruct((rows, half), jnp.uint32),
        grid_spec=grid_spec,
        compiler_params=_params("arbitrary"),
        name="ffn_experts",
    )(eid, valid, hp, wg, wu, wd)


def _dft_kernel(ce_ref, se_ref, co_ref, so_ref, cot_ref, sot_ref, base_ref, *, n_fft):
    rows, cols = ce_ref.shape
    i = pl.program_id(0)
    theta = 2.0 * math.pi / n_fft

    def phases(r, c):
        return r * (2 * c), r * (2 * c + 1), c * (2 * r + 1)

    @pl.when(i == 0)
    def _():
        r = lax.broadcasted_iota(jnp.int32, (rows, cols), 0)
        c = lax.broadcasted_iota(jnp.int32, (rows, cols), 1)
        for f, ph in enumerate(phases(r, c)):
            ang = (ph & (n_fft - 1)).astype(F32) * theta
            base_ref[2 * f] = jnp.cos(ang)
            base_ref[2 * f + 1] = jnp.sin(ang)

    c = lax.broadcasted_iota(jnp.int32, (8, cols), 1)
    r0 = i * rows
    shifts = (r0 * (2 * c), r0 * (2 * c + 1), c * (2 * r0))
    for f, (c_ref, s_ref) in enumerate(((ce_ref, se_ref), (co_ref, so_ref), (cot_ref, sot_ref))):
        ang = (shifts[f] & (n_fft - 1)).astype(F32) * theta
        ca, sa = jnp.cos(ang)[0:1], jnp.sin(ang)[0:1]
        cb, sb = base_ref[2 * f], base_ref[2 * f + 1]
        c_ref[...] = (cb * ca - sb * sa).astype(BF16)
        s_ref[...] = (sb * ca + cb * sa).astype(BF16)


def _dft_tables(length):
    half, rows = length // 2, 128
    shape = jax.ShapeDtypeStruct((half, half), BF16)
    spec = pl.BlockSpec((rows, half), lambda i: (i, 0))
    return pl.pallas_call(
        functools.partial(_dft_kernel, n_fft=2 * length),
        out_shape=(shape,) * 6,
        grid=(half // rows,),
        out_specs=(spec,) * 6,
        scratch_shapes=[pltpu.VMEM((6, rows, half), F32)],
        compiler_params=_params("arbitrary"),
        name="dft_tables",
    )()


LANES = 128


def _split_parity(tmp_ref, x):
    half = x.shape[0] // 2
    for j in range(tmp_ref.shape[0]):
        tmp_ref[j] = x[:, j * LANES:(j + 1) * LANES]
    pick = lambda start: jnp.concatenate(
        [tmp_ref[j, pl.ds(start, half, stride=2), :] for j in range(tmp_ref.shape[0])], axis=1)
    return pick(0), pick(1)


def _merge_parity(tmp_ref, even, odd):
    half = even.shape[0]
    for j in range(tmp_ref.shape[0]):
        tmp_ref[j, pl.ds(0, half, stride=2), :] = even[:, j * LANES:(j + 1) * LANES]
        tmp_ref[j, pl.ds(1, half, stride=2), :] = odd[:, j * LANES:(j + 1) * LANES]
    return jnp.concatenate([tmp_ref[j] for j in range(tmp_ref.shape[0])], axis=1)


def _filter_time_kernel(feats_ref, w1_ref, b1_ref, w2_ref, b2_ref, fr_ref, w3_ref, t_ref, delta_ref,
                        g_ref, hq_ref, tmp_ref, h_ref):
    hp = lax.Precision.HIGHEST
    length = feats_ref.shape[0]
    half = length // 2

    @pl.when(pl.program_id(0) == 0)
    def _():
        fr = fr_ref[...]
        h1 = jnp.sin(fr * (jnp.dot(feats_ref[...], w1_ref[...], preferred_element_type=F32, precision=hp)
                           + b1_ref[...]))
        h_ref[...] = jnp.sin(fr * (jnp.dot(h1, w2_ref[...], preferred_element_type=F32, precision=hp) + b2_ref[...]))

    h = h_ref[...]
    mod =jnp.exp(-t_ref[...] * delta_ref[...]) + MOD_SHIFT
    row = lax.broadcasted_iota(jnp.int32, (length, 1), 0)
    alt = jnp.where(lax.broadcasted_iota(jnp.int32, (half, 1), 0) % 2 == 0, 1.0, -1.0).astype(F32)
    for o in range(2):
        fwd = jnp.dot(h, w3_ref[2 * o], preferred_element_type=F32, precision=hp) * mod
        bwd = jnp.dot(h, w3_ref[2 * o + 1], preferred_element_type=F32, precision=hp) * mod
        bwd = jnp.where(row == 0, 0.0, bwd)
        norm = (jnp.sum(jnp.abs(fwd), axis=0, keepdims=True)
                + jnp.sum(jnp.abs(bwd), axis=0, keepdims=True) + 1e-6)
        fwd = fwd / norm
        bwd = bwd / norm
        for part, g in enumerate((fwd + bwd, fwd - bwd)):
            even, odd = _split_parity(tmp_ref, g)
            g_ref[o, 2 * part] = even.astype(BF16)
            g_ref[o, 2 * part + 1] = odd.astype(BF16)
            mid = jnp.sum((even if part == 0 else odd) * alt, axis=0, keepdims=True) * (1.0 / length)
            hq_ref[o, part] = mid if part == 0 else -mid


def _filter_freq_kernel(ce_ref, se_ref, co_ref, so_ref, g_ref, h_ref):
    half = ce_ref.shape[0]
    row = lax.broadcasted_iota(jnp.int32, (half, 1), 0)
    scale = jnp.where(row == 0, 1.0, 2.0).astype(F32) * (1.0 / (4 * half))
    a = jnp.dot(ce_ref[...], g_ref[0, 0], preferred_element_type=F32)
    b = jnp.dot(co_ref[...], g_ref[0, 1], preferred_element_type=F32)
    c = jnp.dot(se_ref[...], g_ref[0, 2], preferred_element_type=F32)
    d = jnp.dot(so_ref[...], g_ref[0, 3], preferred_element_type=F32)
    h_ref[0, 0] = (a + b) * scale
    h_ref[0, 1] = -(c + d) * scale
    h_ref[0, 2] = (a - b) * scale
    h_ref[0, 3] = (c - d) * scale


def _hyena_filters(length, tables, w1, b1, w2, b2, w3, freq):
    t = np.linspace(0.0, 1.0, length, dtype=np.float32)[:, None]
    f = np.linspace(1e-4, FILT_BANDS - 1, FILT_BANDS, dtype=np.float32)[None]
    ang = np.float32(2.0 * math.pi / length) * np.arange(length, dtype=np.float32)[:, None] * f
    feats = np.concatenate([t, np.cos(ang), -np.sin(ang)], axis=-1).astype(np.float32)
    deltas = np.abs(np.linspace(math.log(DECAY_TARGET) / SLOW_DECAY_PCT,
                                math.log(DECAY_TARGET) / FAST_DECAY_PCT, HY_C, dtype=np.float32))[None]
    hid = w1.shape[1]
    w3r = w3.reshape(hid, 4, HY_C).transpose(1, 0, 2)
    tc = HY_TC
    full = lambda shape: pl.BlockSpec(shape, lambda c: (0,) * len(shape))
    half = length // 2
    g, hq = pl.pallas_call(
        _filter_time_kernel,
        out_shape=(jax.ShapeDtypeStruct((2, 4, half, HY_C), BF16),
                   jax.ShapeDtypeStruct((2, 2, 1, HY_C), F32)),
        grid=(HY_C // tc,),
        in_specs=[full(feats.shape), full(w1.shape), full((1, hid)), full(w2.shape), full((1, hid)),
                  full((1, hid)), pl.BlockSpec((4, hid, tc), lambda c: (0, 0, c)), full((length, 1)),
                  pl.BlockSpec((1, tc), lambda c: (0, c))],
        out_specs=(pl.BlockSpec((2, 4, half, tc), lambda c: (0, 0, 0, c)),
                   pl.BlockSpec((2, 2, 1, tc), lambda c: (0, 0, 0, c))),
        scratch_shapes=[pltpu.VMEM((tc // LANES, length, LANES), F32), pltpu.VMEM((length, hid), F32)],
        compiler_params=_params("arbitrary"),
        name="filter_time",
    )(feats, w1, b1.reshape(1, hid), w2, b2.reshape(1, hid), freq.reshape(1, hid), w3r, t, deltas)
    ce, se, co, so = tables[:4]
    table_spec = _resident((half, half), lambda o, c: (0, 0))
    spectra = pl.pallas_call(
        _filter_freq_kernel,
        out_shape=jax.ShapeDtypeStruct((2, 4, half, HY_C), F32),
        grid=(2, HY_C // tc),
        in_specs=[table_spec] * 4 + [pl.BlockSpec((1, 4, half, tc), lambda o, c: (o, 0, 0, c))],
        out_specs=pl.BlockSpec((1, 4, half, tc), lambda o, c: (o, 0, 0, c)),
        compiler_params=_params("parallel", "parallel"),
        name="filter_freq",
    )(ce, se, co, so, g)
    return spectra, hq


def _hyena_kernel(z0_ref, z1_ref, z2_ref, cw_ref, cb_ref, ce_ref, se_ref, co_ref, so_ref, cot_ref, sot_ref,
                  h_ref, hq_ref, skip_ref, o_ref, tmp_ref):
    length = z0_ref.shape[1]
    half = length // 2
    row = lax.broadcasted_iota(jnp.int32, (length, 1), 0)
    alt = jnp.where(lax.broadcasted_iota(jnp.int32, (half, 1), 0) % 2 == 0, 1.0, -1.0).astype(F32)

    def short_conv(z_ref, c):
        u = z_ref[0].astype(F32)
        prev = jnp.where(row == 0, 0.0, pltpu.roll(u, 1, axis=0))
        nxt = jnp.where(row == length - 1, 0.0, pltpu.roll(u, length - 1, axis=0))
        return cb_ref[c] + (prev * cw_ref[0, c] + u * cw_ref[1, c] + nxt * cw_ref[2, c])

    def long_conv(u, o):
        ue, uo = _split_parity(tmp_ref, u)
        ue_bf, uo_bf = ue.astype(BF16), uo.astype(BF16)
        a4 = jnp.sum(ue * alt, axis=0, keepdims=True)
        b4 = jnp.sum(uo * alt, axis=0, keepdims=True)
        hr4, hi4 = hq_ref[o, 0], hq_ref[o, 1]
        y_even = alt * (a4 * hr4 + b4 * hi4)
        y_odd = -(alt * (a4 * hi4 - b4 * hr4))
        for kt in range(half // HY_KT):
            ks = slice(kt * HY_KT, (kt + 1) * HY_KT)
            ae = jnp.dot(ce_ref[ks, :], ue_bf, preferred_element_type=F32)
            ao = jnp.dot(co_ref[ks, :], uo_bf, preferred_element_type=F32)
            be = jnp.dot(se_ref[ks, :], ue_bf, preferred_element_type=F32)
            bo = jnp.dot(so_ref[ks, :], uo_bf, preferred_element_type=F32)
            a_f, a_g, b_f, b_g = ae + ao, ae - ao, be + bo, bo - be
            hr_f, hi_f, hr_g, hi_g = (h_ref[o, j, ks, :] for j in range(4))
            re_f, im_f = a_f * hr_f + b_f * hi_f, a_f * hi_f - b_f * hr_f
            re_g, im_g = a_g * hr_g + b_g * hi_g, a_g * hi_g - b_g * hr_g
            y_even += (jnp.dot(ce_ref[:, ks], (re_f + re_g).astype(BF16), preferred_element_type=F32)
                       - jnp.dot(se_ref[:, ks], (im_f - im_g).astype(BF16), preferred_element_type=F32))
            y_odd += (jnp.dot(cot_ref[:, ks], (re_f - re_g).astype(BF16), preferred_element_type=F32)
                      - jnp.dot(sot_ref[:, ks], (im_f + im_g).astype(BF16), preferred_element_type=F32))
        return _merge_parity(tmp_ref, y_even, y_odd) + u * skip_ref[o]

    z = short_conv(z1_ref, 1) * long_conv(short_conv(z0_ref, 0), 0)
    o_ref[0] = (short_conv(z2_ref, 2) * long_conv(z, 1)).astype(o_ref.dtype)


def _hyena(z3, conv_w, conv_b, tables, spectra, hq, skip):
    b, s, _ = z3.shape
    half = s // 2
    tc = HY_TC
    nct = HY_C // tc
    cw = conv_w.reshape(3, 3, 1, HY_C)
    cb = conv_b.reshape(3, 1, HY_C)
    zspec = lambda chunk: pl.BlockSpec((1, s, tc), lambda c, i: (i, 0, chunk * nct + c))
    return pl.pallas_call(
        _hyena_kernel,
        out_shape=jax.ShapeDtypeStruct((b, s, HY_C), BF16),
        grid=(nct, b),
        in_specs=[zspec(0), zspec(1), zspec(2),
                  pl.BlockSpec((3, 3, 1, tc), lambda c, i: (0, 0, 0, c)),
                  pl.BlockSpec((3, 1, tc), lambda c, i: (0, 0, c))]
                 + [_resident((half, half), lambda c, i: (0, 0))] * 6
                 + [_resident((2, 4, half, tc), lambda c, i: (0, 0, 0, c)),
                    pl.BlockSpec((2, 2, 1, tc), lambda c, i: (0, 0, 0, c)),
                    pl.BlockSpec((2, 1, tc), lambda c, i: (0, 0, c))],
        out_specs=pl.BlockSpec((1, s, tc), lambda c, i: (i, 0, c)),
        scratch_shapes=[pltpu.VMEM((tc // LANES, s, LANES), F32)],
        compiler_params=_params("parallel", "parallel"),
        cost_estimate=pl.CostEstimate(flops=2 * 4 * 2 * b * s * half * HY_C, transcendentals=0,
                                      bytes_accessed=2 * 3 * b * s * HY_C + 4 * b * s * HY_C + 12 * half * half),
        name="hyena",
    )(z3, z3, z3, cw, cb, *tables, spectra, hq, skip.reshape(2, 1, HY_C))


def _rel_bucket(rel):
    half = NUM_BUCKETS // 2
    exact = half // 2
    n = np.abs(rel)
    large = exact + (np.log(np.maximum(n, 1) / exact) / np.log(REL_MAX_DIST / exact) * (half - exact)).astype(np.int32)
    large = np.minimum(large, half - 1)
    return (np.where(rel > 0, half, 0) + np.where(n < exact, n, large)).astype(np.int32)


def _att_tiles(length):
    kw = min(ATT_KW, length)
    tiles = []
    for qs in range(0, length, ATT_TQ):
        ks = min(max(qs - (kw - ATT_TQ) // 2, 0), length - kw)
        tiles.append((qs, ks, {0: 0, -64: 1, -128: 2}[ks - qs]))
    return kw, tiles


def _att_bias(rel_bias, group, seq):
    window, dil = AT_GROUPS[group]
    band = window // (2 * dil)
    kw, tiles = _att_tiles(seq // dil)
    offsets = sorted({ks - qs for qs, ks, _ in tiles}, reverse=True)
    table = rel_bias[:, group * AT_HEADS:(group + 1) * AT_HEADS].astype(F32)
    span = kw + ATT_TQ
    k = np.arange(span)
    rel = np.asarray(offsets)[:, None] + np.where(k < kw, k, k - span)[None, :]
    diag = jnp.where((np.abs(rel) <= band)[:, None, :],
                     jnp.swapaxes(table[_rel_bucket(rel * dil)], 1, 2), NEG_INF)
    return diag[:, :, None, :]


def _bias_tile(diag_ref, var, kw):
    rows = jnp.broadcast_to(diag_ref[var, 0], (ATT_TQ, diag_ref.shape[3]))
    return pltpu.roll(rows, 0, 1, stride=1, stride_axis=0)[:, :kw]


def _dilated_kernel(q1, q2, q3, k1, k2, k3, v1, v2, v3, b1, b2, b3, o_ref,
                    qn_ref, kn_ref, vn_ref, og_ref, lg_ref):
    seq = o_ref.shape[1]
    contract_last = (((1,), (1,)), ((), ()))
    for g, (q_ref, k_ref, v_ref, bias_ref) in enumerate(((q1, k1, v1, b1), (q2, k2, v2, b2), (q3, k3, v3, b3))):
        dil = AT_GROUPS[g][1]
        kw, tiles = _att_tiles(seq // dil)
        if dil > 1:
            qn_ref[...] = q_ref[0].astype(F32)
            kn_ref[...] = k_ref[0].astype(F32)
            vn_ref[...] = v_ref[0].astype(F32)
        bias = {var: _bias_tile(bias_ref, var, kw) for var in sorted({var for _, _, var in tiles})}
        rows = lambda r, start, size: (pl.ds(r + start * dil, size, stride=dil) if dil > 1 else pl.ds(start, size))
        work = [(r, qs, ks, var) for r in range(dil) for qs, ks, var in tiles]
        for first in range(0, len(work), ATT_BATCH):
            scores, values, places = [], [], []
            for r, qs, ks, var in work[first:first + ATT_BATCH]:
                if dil > 1:
                    qt = qn_ref[rows(r, qs, ATT_TQ), :].astype(BF16)
                    kt = kn_ref[rows(r, ks, kw), :].astype(BF16)
                    vt = vn_ref[rows(r, ks, kw), :].astype(BF16)
                else:
                    qt, kt, vt = (q_ref[0, rows(r, qs, ATT_TQ), :], k_ref[0, rows(r, ks, kw), :],
                                  v_ref[0, rows(r, ks, kw), :])
                scores.append(lax.dot_general(qt, kt, contract_last, preferred_element_type=F32) + bias[var])
                values.append(vt)
                places.append(rows(r, qs, ATT_TQ))
            softmax = []
            for s in scores:
                m = jnp.max(s, axis=-1, keepdims=True)
                p = jnp.exp(s - m)
                softmax.append((p.astype(BF16), jnp.sum(p, axis=-1, keepdims=True), m))
            for (p, z, m), vt, place in zip(softmax, values, places):
                og_ref[g, place, :] = jnp.dot(p, vt, preferred_element_type=F32) / z
                lg_ref[g, place, :] = jnp.broadcast_to(m + jnp.log(z), (ATT_TQ, HD))
    l0, l1, l2 = lg_ref[0], lg_ref[1], lg_ref[2]
    mx = jnp.maximum(jnp.maximum(l0, l1), l2)
    e0, e1, e2 = jnp.exp(l0 - mx), jnp.exp(l1 - mx), jnp.exp(l2 - mx)
    o_ref[0] = ((e0 * og_ref[0] + e1 * og_ref[1] + e2 * og_ref[2]) / (e0 + e1 + e2)).astype(o_ref.dtype)


def _dilated_attention(z3, rel_bias):
    b, s, _ = z3.shape
    ng = len(AT_GROUPS)
    col = lambda part, g: pl.BlockSpec((1, s, HD), lambda i, h: (i, 0, part * ng * AT_HEADS + g * AT_HEADS + h))
    biases = [_att_bias(rel_bias, g, s) for g in range(ng)]
    bias_spec = lambda a: pl.BlockSpec((a.shape[0], 1) + a.shape[2:], lambda i, h: (0, h, 0, 0))
    return pl.pallas_call(
        _dilated_kernel,
        out_shape=jax.ShapeDtypeStruct((b, s, AT_HEADS * HD), BF16),
        grid=(b, AT_HEADS),
        in_specs=[col(p, g) for p in range(3) for g in range(ng)] + [bias_spec(a) for a in biases],
        out_specs=pl.BlockSpec((1, s, HD), lambda i, h: (i, 0, h)),
        scratch_shapes=[pltpu.VMEM((s, HD), F32), pltpu.VMEM((s, HD), F32), pltpu.VMEM((s, HD), F32),
                        pltpu.VMEM((ng, s, HD), F32), pltpu.VMEM((ng, s, HD), F32)],
        compiler_params=_params("parallel", "parallel"),
        name="dilated_attention",
    )(*([z3] * 9), *biases)


ROUTER_LANES = 128


def _router_kernel(x_ref, g_ref, w_ref, hp_ref, idx_ref, gate_ref, count_ref, carry_ref):
    tm = x_ref.shape[0]
    half = hp_ref.shape[1]

    @pl.when(pl.program_id(0) == 0)
    def _():
        carry_ref[...] = jnp.zeros_like(carry_ref)

    h = _rms_rows(x_ref[...], g_ref[...])
    hp_ref[...] = _bf16_bits(h[:, :half]) | (_bf16_bits(h[:, half:]) << 16)
    h_hi = h.astype(BF16)
    h_lo = (h - h_hi.astype(F32)).astype(BF16)
    logits = (jnp.dot(h_hi, w_ref[0], preferred_element_type=F32)
              + (jnp.dot(h_hi, w_ref[1], preferred_element_type=F32)
                 + jnp.dot(h_lo, w_ref[0], preferred_element_type=F32)))
    lane = lax.broadcasted_iota(jnp.int32, logits.shape, 1).astype(F32)
    logits = jnp.where(lane < N_EXPERTS, logits, -jnp.inf)
    m1 = jnp.max(logits, axis=-1, keepdims=True)
    i1 = jnp.min(jnp.where(logits == m1, lane, float(ROUTER_LANES)), axis=-1, keepdims=True)
    rest = jnp.where(lane == i1, -jnp.inf, logits)
    m2 = jnp.max(rest, axis=-1, keepdims=True)
    i2 = jnp.min(jnp.where(rest == m2, lane, float(ROUTER_LANES)), axis=-1, keepdims=True)
    e2 = jnp.exp(m2 - m1)
    den = 1.0 + e2
    gate_ref[...] = jnp.where(lane == 0, 1.0 / den, e2 / den)
    chosen = jnp.where((lane == i1) | (lane == i2), 1.0, 0.0)
    earlier = lax.broadcasted_iota(jnp.int32, (tm, tm), 0) > lax.broadcasted_iota(jnp.int32, (tm, tm), 1)
    carry = carry_ref[...]
    before = jnp.dot(jnp.where(earlier, 1.0, 0.0).astype(BF16), chosen.astype(BF16),
                     preferred_element_type=F32) + carry
    r1 = jnp.sum(jnp.where(lane == i1, before, 0.0), axis=-1, keepdims=True)
    r2 = jnp.sum(jnp.where(lane == i2, before, 0.0), axis=-1, keepdims=True)
    idx_ref[...] = jnp.where(lane == 0, i1, jnp.where(lane == 1, i2, jnp.where(lane == 2, r1, r2))).astype(jnp.int32)
    carry = carry + jnp.sum(chosen, axis=0, keepdims=True)
    carry_ref[...] = carry
    count_ref[...] = carry.astype(jnp.int32)


def _router(x2, gain, router):
    rows, d = x2.shape
    tm = MOE_CHUNK
    w = jnp.zeros((d, ROUTER_LANES), F32).at[:, :N_EXPERTS].set(router.astype(F32))
    w_hi = w.astype(BF16)
    w = jnp.stack([w_hi, (w - w_hi.astype(F32)).astype(BF16)])
    row_spec = lambda width: pl.BlockSpec((tm, width), lambda i: (i, 0))
    return pl.pallas_call(
        _router_kernel,
        out_shape=(jax.ShapeDtypeStruct((rows, d // 2), jnp.uint32),
                   jax.ShapeDtypeStruct((rows, ROUTER_LANES), jnp.int32),
                   jax.ShapeDtypeStruct((rows, ROUTER_LANES), F32),
                   jax.ShapeDtypeStruct((1, ROUTER_LANES), jnp.int32)),
        grid=(rows // tm,),
        in_specs=[row_spec(d),
                  pl.BlockSpec((1, d), lambda i: (0, 0)),
                  pl.BlockSpec((2, d, ROUTER_LANES), lambda i: (0, 0, 0))],
        out_specs=(row_spec(d // 2), row_spec(ROUTER_LANES), row_spec(ROUTER_LANES),
                   pl.BlockSpec((1, ROUTER_LANES), lambda i: (0, 0))),
        scratch_shapes=[pltpu.VMEM((1, ROUTER_LANES), F32)],
        compiler_params=_params("arbitrary"),
        name="router",
    )(x2, gain.reshape(1, d), w)


def _sc_workers():
    info = plsc.get_sparse_core_info()
    mesh = plsc.VectorSubcoreMesh(core_axis_name="core", subcore_axis_name="subcore")
    return mesh, info.num_cores, info.num_subcores


def _pack_weight_rows(w, col_tile):
    r, c = w.shape
    mesh, nc, ns = _sc_workers()
    rs, ncol = SC_PACK_ROWS, c // col_tile
    per_w = (r // rs) * ncol // (nc * ns)
    assert per_w * nc * ns * rs * col_tile == r * c and per_w % 2 == 0 and col_tile % SC_LANES == 0
    params = pltpu.CompilerParams()
    if "needs_layout_passes" in pltpu.CompilerParams.__dataclass_fields__:
        params = dataclasses.replace(params, needs_layout_passes=False)

    @functools.partial(
        pl.kernel, mesh=mesh, out_type=jax.ShapeDtypeStruct((r // 2, c), jnp.uint32), compiler_params=params,
        cost_estimate=pl.CostEstimate(flops=r * c, transcendentals=0, bytes_accessed=6 * r * c),
        scratch_types=[pltpu.VMEM((2, rs, col_tile), F32), pltpu.VMEM((2, rs // 2, col_tile), jnp.uint32),
                       pltpu.SemaphoreType.DMA((2,)), pltpu.SemaphoreType.DMA((2,))])
    def pack(w_hbm, o_hbm, in_v, out_v, rsem, wsem):
        wid = lax.axis_index("subcore") * nc + lax.axis_index("core")

        @pl.loop(0, per_w // 2)
        def _(it):
            reads, writes = [], []
            for b in range(2):
                tile = wid * per_w + 2 * it + b
                r0 = pl.multiple_of((tile // ncol) * rs, rs)
                c0 = pl.multiple_of((tile % ncol) * col_tile, col_tile)
                reads.append(pltpu.make_async_copy(w_hbm.at[pl.ds(r0, rs), pl.ds(c0, col_tile)], in_v.at[b],
                                                   rsem.at[b]))
                writes.append(pltpu.make_async_copy(
                    out_v.at[b], o_hbm.at[pl.ds(pl.multiple_of(r0 // 2, rs // 2), rs // 2), pl.ds(c0, col_tile)],
                    wsem.at[b]))
            reads[0].start()
            reads[1].start()
            for b in range(2):
                reads[b].wait()
                for pair in range(rs // 2):
                    @plsc.parallel_loop(0, col_tile, step=SC_LANES, unroll=8)
                    def _(j):
                        packed = plsc.pack(in_v[b, 2 * pair, pl.ds(j, SC_LANES)],
                                           in_v[b, 2 * pair + 1, pl.ds(j, SC_LANES)],
                                           format=plsc.PackFormat.INTERLEAVED)
                        out_v[b, pair, pl.ds(j, SC_LANES)] = plsc.bitcast(packed, jnp.uint32)
                writes[b].start()
            writes[0].wait()
            writes[1].wait()

    return pack(w)


def _sc_token_rows(t, nc, ns, ch):
    per_w = t // (nc * ns)
    assert per_w * nc * ns == t and per_w % (2 * ch) == 0
    return per_w, per_w // ch


def _scatter_rows(table, dests, p_rows, after=()):
    t, w = table.shape
    nk = len(dests)
    mesh, nc, ns = _sc_workers()
    ch = SC_SCATTER_ROWS
    per_w, nit = _sc_token_rows(t, nc, ns, ch)

    @functools.partial(
        pl.kernel, mesh=mesh, out_type=jax.ShapeDtypeStruct((p_rows, w), table.dtype),
        scratch_types=[pltpu.VMEM((nk, nit, ch), jnp.int32), pltpu.VMEM((2, ch, w), table.dtype),
                       pltpu.SemaphoreType.DMA((2,)), pltpu.SemaphoreType.DMA((2, nk))])
    def scatter(table_hbm, *refs):
        dest_hbm, out_hbm = refs[:nk], refs[nk + len(after)]
        idx_v, rows_v, rsem, wsem = refs[nk + len(after) + 1:]
        wid = lax.axis_index("subcore") * nc + lax.axis_index("core")
        for k in range(nk):
            pltpu.sync_copy(dest_hbm[k].at[pl.ds(wid * nit, nit)], idx_v.at[k])

        @pl.loop(0, nit // 2)
        def _(it):
            reads = [pltpu.make_async_copy(table_hbm.at[pl.ds(wid * per_w + (2 * it + b) * ch, ch)],
                                           rows_v.at[b], rsem.at[b]) for b in range(2)]
            writes = [[pltpu.make_async_copy(rows_v.at[b], out_hbm.at[idx_v.at[k].at[2 * it + b]], wsem.at[b, k])
                       for k in range(nk)] for b in range(2)]
            reads[0].start()
            reads[1].start()
            for b in range(2):
                reads[b].wait()
                for k in range(nk):
                    writes[b][k].start()
            for b in range(2):
                for k in range(nk):
                    writes[b][k].wait()

    return scatter(table, *[d.reshape(t // ch, ch) for d in dests], *after)


def _gather_rows(table, idxs):
    t = idxs[0].shape[0]
    w = table.shape[1]
    nk = len(idxs)
    mesh, nc, ns = _sc_workers()
    ch = SC_ROWS
    per_w, nit = _sc_token_rows(t, nc, ns, ch)

    @functools.partial(
        pl.kernel, mesh=mesh, out_type=[jax.ShapeDtypeStruct((t, w), table.dtype)] * nk,
        scratch_types=[pltpu.VMEM((nk, nit, ch), jnp.int32), pltpu.VMEM((nk, 2, ch, w), table.dtype),
                       pltpu.SemaphoreType.DMA((nk, 2)), pltpu.SemaphoreType.DMA((nk, 2))])
    def gather(table_hbm, *refs):
        idx_hbm, out_hbm = refs[:nk], refs[nk:2 * nk]
        idx_v, rows_v, rsem, wsem = refs[2 * nk:]
        wid = lax.axis_index("subcore") * nc + lax.axis_index("core")
        for k in range(nk):
            pltpu.sync_copy(idx_hbm[k].at[pl.ds(wid * nit, nit)], idx_v.at[k])

        @pl.loop(0, nit // 2)
        def _(it):
            slots = [(k, b) for k in range(nk) for b in range(2)]
            reads = {(k, b): pltpu.make_async_copy(table_hbm.at[idx_v.at[k].at[2 * it + b]], rows_v.at[k, b],
                                                   rsem.at[k, b]) for k, b in slots}
            writes = {(k, b): pltpu.make_async_copy(rows_v.at[k, b],
                                                    out_hbm[k].at[pl.ds(wid * per_w + (2 * it + b) * ch, ch)],
                                                    wsem.at[k, b]) for k, b in slots}
            for s in slots:
                reads[s].start()
            for s in slots:
                reads[s].wait()
                writes[s].start()
            for s in slots:
                writes[s].wait()

    return gather(table, *[i.reshape(t // ch, ch) for i in idxs])


def _combine_kernel(x_ref, y0_ref, y1_ref, gate_ref, *rest):
    o_ref = rest[-1]
    half = x_ref.shape[1] // 2
    for part in range(2):
        cols = slice(part * half, (part + 1) * half)
        acc = x_ref[:, cols]
        for k, y_ref in enumerate((y0_ref, y1_ref)):
            word = y_ref[...]
            bits = (word << 16) if part == 0 else (word & jnp.uint32(0xFFFF0000))
            acc = acc + gate_ref[:, k:k + 1] * lax.bitcast_convert_type(bits, F32)
        o_ref[:, cols] = acc


def _combine(x2, y0, y1, gates, part, earlier=None):
    t, d = x2.shape
    tm = ROW_TILE
    steps = y0.shape[0] // tm
    here = lambda i: (part * steps + i, 0)
    in_specs = [pl.BlockSpec((tm, d), here),
                pl.BlockSpec((tm, d // 2), lambda i: (i, 0)),
                pl.BlockSpec((tm, d // 2), lambda i: (i, 0)),
                pl.BlockSpec((tm, TOP_K), here)]
    operands = [x2, y0, y1, gates]
    if earlier is not None:
        in_specs.append(pl.BlockSpec(memory_space=pl.ANY))
        operands.append(earlier)
    return pl.pallas_call(
        _combine_kernel,
        out_shape=jax.ShapeDtypeStruct((t, d), F32),
        grid=(steps,),
        in_specs=in_specs,
        out_specs=pl.BlockSpec((tm, d), here),
        input_output_aliases={} if earlier is None else {len(operands) - 1: 0},
        compiler_params=_params("parallel"),
        name="moe_combine",
    )(*operands)


def _moe(x2, gain, router, wg, wu, wd):
    t, d = x2.shape
    hp, idx, gate, count = _router(x2, gain, router)
    experts = jnp.arange(N_EXPERTS, dtype=jnp.int32)
    counts = count[0, :N_EXPERTS]
    padded = (counts + MOE_ROWS - 1) // MOE_ROWS * MOE_ROWS
    pend = jnp.cumsum(padded)
    pstart = pend - padded
    expert, rank = idx[:, :TOP_K], idx[:, TOP_K:2 * TOP_K]
    dest = jnp.sum(jnp.where(expert[:, :, None] == experts, pstart, 0), axis=-1) + rank
    dests = [dest[:, k] for k in range(TOP_K)]
    p_rows = t * TOP_K + N_EXPERTS * MOE_ROWS
    blk_row = jnp.arange(p_rows // MOE_ROWS, dtype=jnp.int32) * MOE_ROWS
    blk_expert = jnp.minimum(jnp.sum(pend[None, :] <= blk_row[:, None], axis=1), N_EXPERTS - 1).astype(jnp.int32)
    valid = jnp.clip(counts[blk_expert] - (blk_row - pstart[blk_expert]), 0, MOE_ROWS)
    valid = jnp.where(blk_row < pend[-1], valid, 0).astype(jnp.int32)
    n_exp, _, f = wg.shape
    pack = lambda w, col_tile: _pack_weight_rows(w.reshape(-1, w.shape[2]), col_tile).reshape(n_exp, -1, w.shape[2])
    wg, wu, wd = pack(wg, f // 2), pack(wu, f // 2), pack(wd, d)
    hb = _scatter_rows(hp, dests, p_rows, after=(wg, wu, wd))
    yb = _ffn_experts(hb, wg, wu, wd, blk_expert, valid)
    out = None
    for part in range(MOE_TAIL_PARTS):
        rows = slice(part * t // MOE_TAIL_PARTS, (part + 1) * t // MOE_TAIL_PARTS)
        y0, y1 = _gather_rows(yb, [dk[rows] for dk in dests])
        out = _combine(x2, y0, y1, gate[:, :TOP_K], part, out)
    return out


def kernel(x, mem, rel_bias, norm_mix, norm_mem, norm_ffn, w_mem_kv, xq_norm, xk_norm, w_out, hy_w_in, hy_conv_w, hy_conv_b, hy_filt_w1, hy_filt_b1, hy_filt_w2, hy_filt_b2, hy_filt_w3, hy_sin_freq, hy_skip, at_w_in, at_q_norm, at_k_norm, ffn_w_gate, ffn_w_up, ffn_w_down, moe_router, moe_w_gate, moe_w_up, moe_w_down):
    b, s, d = x.shape
    t = b * s
    m_len = mem.shape[1]
    x2 = x.reshape(t, d)
    mem2 = mem.reshape(b * m_len, d)
    bf = lambda w: w.astype(BF16)
    score_scale = HD ** -0.5
    xa_heads = XA_W // HD
    at_heads = AT_W // HD
    plain = lambda width: [None] * (width // HD)
    kv_gains = lambda i: [xk_norm[i]] * xa_heads + plain(XA_W)

    tables = _dft_tables(s)
    spectra, hq = _hyena_filters(s, tables, hy_filt_w1[0], hy_filt_b1[0], hy_filt_w2[0], hy_filt_b2[0],
                                 hy_filt_w3[0], hy_sin_freq[0])
    z = _norm_matmul(x2, norm_mix[0], bf(hy_w_in[0]), 1024,
                     plain(3 * HY_C) + [xq_norm[0] * score_scale] * xa_heads).reshape(b, s, -1)
    kv = _norm_matmul(mem2, norm_mem[0], bf(w_mem_kv[0]), 1024, kv_gains(0)).reshape(b, m_len, -1)
    self_out = _hyena(z, hy_conv_w[0], hy_conv_b[0], tables, spectra, hq, hy_skip[0])
    x2 = _mix_out(x2.reshape(b, s, d), self_out, z, 3 * HY_C // XA_W, kv, bf(w_out[0])).reshape(t, d)
    x2 = _ffn_dense(x2, norm_ffn[0], bf(ffn_w_gate[0]), bf(ffn_w_up[0]), bf(ffn_w_down[0]))

    at_gains = ([at_q_norm[0] * score_scale] * at_heads + [at_k_norm[0]] * at_heads + plain(AT_W)
                + [xq_norm[1] * score_scale] * xa_heads)
    z = _norm_matmul(x2, norm_mix[1], bf(at_w_in[0]), 1024, at_gains).reshape(b, s, -1)
    kv = _norm_matmul(mem2, norm_mem[1], bf(w_mem_kv[1]), 1024, kv_gains(1)).reshape(b, m_len, -1)
    self_out = _dilated_attention(z, rel_bias)
    x2 = _mix_out(x2.reshape(b, s, d), self_out, z, 3 * AT_W // XA_W, kv, bf(w_out[1])).reshape(t, d)
    x2 = _moe(x2, norm_ffn[1], moe_router[0], moe_w_gate[0], moe_w_up[0], moe_w_down[0])
    return x2.reshape(b, s, d)
```

```python
import dataclasses
import functools
import math

import jax
import jax.numpy as jnp
import numpy as np
from jax import lax
from jax.experimental import pallas as pl
from jax.experimental.pallas import tpu as pltpu
from jax.experimental.pallas import tpu_sc as plsc

F32 = jnp.float32
BF16 = jnp.bfloat16

D_MODEL = 1024
EPS = 1e-6
HY_C = 512
FILT_BANDS = 16
DECAY_TARGET = 1e-2
FAST_DECAY_PCT = 0.3
SLOW_DECAY_PCT = 1.5
MOD_SHIFT = 0.05
AT_GROUPS = ((128, 1), (512, 4), (2048, 16))
AT_HEADS = 4
HD = 128
AT_W = 1536
NUM_BUCKETS = 32
REL_MAX_DIST = 1024
NEG_INF = -1e30
XA_W = 512
D_FF = 2816
N_EXPERTS = 8
TOP_K = 2

VMEM_LIMIT_BYTES = 56 * 1024 * 1024
ROW_TILE = 1024
PROJ_ROW_TILE = 1024
FF_TILE = 256
MOE_ROWS = 512
MOE_CHUNK = 512
MOE_TAIL_PARTS = 4
SC_ROWS = 32
SC_SCATTER_ROWS = 64
SC_LANES = 16
SC_PACK_ROWS = 16
ATT_TQ = 128
ATT_KW = 256
ATT_BATCH = 8
HY_TC = 256
HY_KT = 1024


def _params(*sem):
    return pltpu.CompilerParams(dimension_semantics=sem, vmem_limit_bytes=VMEM_LIMIT_BYTES)


def _rms_rows(x, gain):
    return x * lax.rsqrt(jnp.mean(x * x, axis=-1, keepdims=True) + EPS) * gain


def _bf16_bits(v):
    u = lax.bitcast_convert_type(v, jnp.uint32)
    return (u + jnp.uint32(0x7FFF) + ((u >> 16) & jnp.uint32(1))) >> 16


def _resident(shape, index_map):
    return pl.BlockSpec(shape, index_map, pipeline_mode=pl.Buffered(1))


def _norm_matmul_kernel(x_ref, g_ref, w_ref, hg_ref, o_ref, *, tn, head_norm):
    h = _rms_rows(x_ref[...], g_ref[...]).astype(BF16)
    for c in range(o_ref.shape[1] // tn):
        acc = jnp.dot(h, w_ref[:, c * tn:(c + 1) * tn], preferred_element_type=F32)
        for j in range(tn // HD):
            cols = slice(c * tn + j * HD, c * tn + (j + 1) * HD)
            seg = acc[:, j * HD:(j + 1) * HD]
            if head_norm[cols.start // HD]:
                seg = _rms_rows(seg, hg_ref[:, cols])
            o_ref[:, cols] = seg.astype(o_ref.dtype)


def _norm_matmul(x2, gain, w_bf, tn, head_gains):
    rows, d = x2.shape
    n = w_bf.shape[1]
    tm = min(PROJ_ROW_TILE, rows)
    assert len(head_gains) * HD == n
    hg = jnp.concatenate([jnp.ones((HD,), F32) if g is None else g.astype(F32) for g in head_gains]).reshape(1, n)
    return pl.pallas_call(
        functools.partial(_norm_matmul_kernel, tn=tn, head_norm=tuple(g is not None for g in head_gains)),
        out_shape=jax.ShapeDtypeStruct((rows, n), BF16),
        grid=(rows // tm,),
        in_specs=[pl.BlockSpec((tm, d), lambda i: (i, 0)),
                  pl.BlockSpec((1, d), lambda i: (0, 0)),
                  _resident((d, n), lambda i: (0, 0)),
                  pl.BlockSpec((1, n), lambda i: (0, 0))],
        out_specs=pl.BlockSpec((tm, n), lambda i: (i, 0)),
        compiler_params=_params("parallel"),
        cost_estimate=pl.CostEstimate(flops=2 * rows * d * n, transcendentals=rows,
                                      bytes_accessed=4 * rows * d + 2 * d * n + 2 * rows * n),
        name="norm_matmul",
    )(x2, gain.reshape(1, d), w_bf, hg)


def _mix_out_kernel(x_ref, a_ref, xq_ref, kv_ref, wa_ref, wc_ref, o_ref):
    heads = []
    for h in range(XA_W // HD):
        cols = slice(h * HD, (h + 1) * HD)
        s = lax.dot_general(xq_ref[0, :, cols], kv_ref[0, :, cols], (((1,), (1,)), ((), ())),
                            preferred_element_type=F32)
        p = jnp.exp(s - jnp.max(s, axis=-1, keepdims=True))
        z = jnp.sum(p, axis=-1, keepdims=True)
        pv = jnp.dot(p.astype(BF16), kv_ref[0, :, XA_W + h * HD:XA_W + (h + 1) * HD], preferred_element_type=F32)
        heads.append((pv / z).astype(BF16))
    cross = jnp.concatenate(heads, axis=1)
    o_ref[0] = (x_ref[0]
                + jnp.dot(a_ref[0], wa_ref[...], preferred_element_type=F32)
                + jnp.dot(cross, wc_ref[...], preferred_element_type=F32))


def _mix_out(x3, self_out, z3, xq_block, kv3, w_bf):
    b, s, d = x3.shape
    half = self_out.shape[2]
    m = kv3.shape[1]
    tm = ROW_TILE
    return pl.pallas_call(
        _mix_out_kernel,
        out_shape=jax.ShapeDtypeStruct((b, s, d), F32),
        grid=(b, s // tm),
        in_specs=[pl.BlockSpec((1, tm, d), lambda i, j: (i, j, 0)),
                  pl.BlockSpec((1, tm, half), lambda i, j: (i, j, 0)),
                  pl.BlockSpec((1, tm, XA_W), lambda i, j: (i, j, xq_block)),
                  pl.BlockSpec((1, m, 2 * XA_W), lambda i, j: (i, 0, 0)),
                  pl.BlockSpec((half, d), lambda i, j: (0, 0)),
                  pl.BlockSpec((d - half, d), lambda i, j: (1, 0))],
        out_specs=pl.BlockSpec((1, tm, d), lambda i, j: (i, j, 0)),
        compiler_params=_params("parallel", "parallel"),
        name="mix_out",
    )(x3, self_out, z3, kv3, w_bf, w_bf)


def _weight_rows(ref, start, size):
    if ref.dtype == jnp.uint32:
        return lambda cols: pltpu.bitcast(ref[start // 2:(start + size) // 2, cols], BF16)
    return lambda cols: ref[start:start + size, cols]


def _weight_shape(ref):
    return (ref.shape[0] * (2 if ref.dtype == jnp.uint32 else 1), ref.shape[1])


def _swiglu(h, wg_ref, wu_ref, wd_ref, tf):
    d, f = _weight_shape(wg_ref)
    assert f % tf == 0
    y = None
    for j in range(f // tf):
        cols = slice(j * tf, (j + 1) * tf)
        gg = jnp.dot(h, _weight_rows(wg_ref, 0, d)(cols), preferred_element_type=F32)
        uu = jnp.dot(h, _weight_rows(wu_ref, 0, d)(cols), preferred_element_type=F32)
        a = ((gg * jax.nn.sigmoid(gg)) * uu).astype(BF16)
        part = jnp.dot(a, _weight_rows(wd_ref, j * tf, tf)(slice(None)), preferred_element_type=F32)
        y = part if y is None else y + part
    return y


def _ffn_dense_kernel(x_ref, g_ref, wg_ref, wu_ref, wd_ref, o_ref, *, tf):
    x = x_ref[...]
    h = _rms_rows(x, g_ref[...]).astype(BF16)
    o_ref[...] = x + _swiglu(h, wg_ref, wu_ref, wd_ref, tf)


def _ffn_dense(x2, gain, wg, wu, wd):
    rows, d = x2.shape
    f = wg.shape[1]
    tm = ROW_TILE
    return pl.pallas_call(
        functools.partial(_ffn_dense_kernel, tf=FF_TILE),
        out_shape=jax.ShapeDtypeStruct((rows, d), F32),
        grid=(rows // tm,),
        in_specs=[pl.BlockSpec((tm, d), lambda i: (i, 0)),
                  pl.BlockSpec((1, d), lambda i: (0, 0)),
                  _resident((d, f), lambda i: (0, 0)),
                  _resident((d, f), lambda i: (0, 0)),
                  _resident((f, d), lambda i: (0, 0))],
        out_specs=pl.BlockSpec((tm, d), lambda i: (i, 0)),
        compiler_params=_params("parallel"),
        cost_estimate=pl.CostEstimate(flops=6 * rows * d * f, transcendentals=rows * f,
                                      bytes_accessed=8 * rows * d + 6 * d * f),
        name="ffn_dense",
    )(x2, gain.reshape(1, d), wg, wu, wd)


def _ffn_expert_kernel(eid_ref, valid_ref, hp_ref, wg_ref, wu_ref, wd_ref, o_ref, h_ref, *, tf):
    half = hp_ref.shape[1]
    valid = valid_ref[pl.program_id(0)]

    @pl.when(valid > 0)
    def _():
        keep = lax.broadcasted_iota(jnp.int32, (hp_ref.shape[0], 1), 0) < valid
        word = hp_ref[...]
        h_ref[:, :half] = jnp.where(keep, lax.bitcast_convert_type(word << 16, F32), 0.0).astype(BF16)
        h_ref[:, half:] = jnp.where(keep, lax.bitcast_convert_type(word & jnp.uint32(0xFFFF0000), F32),
                                    0.0).astype(BF16)
        y = _swiglu(h_ref[...], wg_ref.at[0], wu_ref.at[0], wd_ref.at[0], tf)
        o_ref[...] = _bf16_bits(y[:, :half]) | (_bf16_bits(y[:, half:]) << 16)

    @pl.when(valid <= 0)
    def _():
        o_ref[...] = jnp.zeros_like(o_ref)


def _ffn_experts(hp, wg, wu, wd, eid, valid):
    rows, half = hp.shape
    d = 2 * half
    tm = MOE_ROWS
    expert_spec = lambda w: pl.BlockSpec((1,) + w.shape[1:], lambda i, e, n: (e[i], 0, 0))
    grid_spec = pltpu.PrefetchScalarGridSpec(
        num_scalar_prefetch=2,
        grid=(rows // tm,),
        in_specs=[pl.BlockSpec((tm, half), lambda i, e, n: (i, 0)),
                  expert_spec(wg), expert_spec(wu), expert_spec(wd)],
        out_specs=pl.BlockSpec((tm, half), lambda i, e, n: (i, 0)),
        scratch_shapes=[pltpu.VMEM((tm, d), BF16)],
    )
    return pl.pallas_call(
        functools.partial(_ffn_expert_kernel, tf=FF_TILE),
        out_shape=jax.ShapeDtypeStruct((rows, half), jnp.uint32),
        grid_spec=grid_spec,
        compiler_params=_params("arbitrary"),
        name="ffn_experts",
    )(eid, valid, hp, wg, wu, wd)


def _dft_kernel(ce_ref, se_ref, co_ref, so_ref, cot_ref, sot_ref, base_ref, *, n_fft):
    rows, cols = ce_ref.shape
    i = pl.program_id(0)
    theta = 2.0 * math.pi / n_fft

    def phases(r, c):
        return r * (2 * c), r * (2 * c + 1), c * (2 * r + 1)

    @pl.when(i == 0)
    def _():
        r = lax.broadcasted_iota(jnp.int32, (rows, cols), 0)
        c = lax.broadcasted_iota(jnp.int32, (rows, cols), 1)
        for f, ph in enumerate(phases(r, c)):
            ang = (ph & (n_fft - 1)).astype(F32) * theta
            base_ref[2 * f] = jnp.cos(ang)
            base_ref[2 * f + 1] = jnp.sin(ang)

    c = lax.broadcasted_iota(jnp.int32, (8, cols), 1)
    r0 = i * rows
    shifts = (r0 * (2 * c), r0 * (2 * c + 1), c * (2 * r0))
    for f, (c_ref, s_ref) in enumerate(((ce_ref, se_ref), (co_ref, so_ref), (cot_ref, sot_ref))):
        ang = (shifts[f] & (n_fft - 1)).astype(F32) * theta
        ca, sa = jnp.cos(ang)[0:1], jnp.sin(ang)[0:1]
        cb, sb = base_ref[2 * f], base_ref[2 * f + 1]
        c_ref[...] = (cb * ca - sb * sa).astype(BF16)
        s_ref[...] = (sb * ca + cb * sa).astype(BF16)


def _dft_tables(length):
    half, rows = length // 2, 128
    shape = jax.ShapeDtypeStruct((half, half), BF16)
    spec = pl.BlockSpec((rows, half), lambda i: (i, 0))
    return pl.pallas_call(
        functools.partial(_dft_kernel, n_fft=2 * length),
        out_shape=(shape,) * 6,
        grid=(half // rows,),
        out_specs=(spec,) * 6,
        scratch_shapes=[pltpu.VMEM((6, rows, half), F32)],
        compiler_params=_params("arbitrary"),
        name="dft_tables",
    )()


LANES = 128


def _split_parity(tmp_ref, x):
    half = x.shape[0] // 2
    for j in range(tmp_ref.shape[0]):
        tmp_ref[j] = x[:, j * LANES:(j + 1) * LANES]
    pick = lambda start: jnp.concatenate(
        [tmp_ref[j, pl.ds(start, half, stride=2), :] for j in range(tmp_ref.shape[0])], axis=1)
    return pick(0), pick(1)


def _merge_parity(tmp_ref, even, odd):
    half = even.shape[0]
    for j in range(tmp_ref.shape[0]):
        tmp_ref[j, pl.ds(0, half, stride=2), :] = even[:, j * LANES:(j + 1) * LANES]
        tmp_ref[j, pl.ds(1, half, stride=2), :] = odd[:, j * LANES:(j + 1) * LANES]
    return jnp.concatenate([tmp_ref[j] for j in range(tmp_ref.shape[0])], axis=1)


def _filter_time_kernel(feats_ref, w1_ref, b1_ref, w2_ref, b2_ref, fr_ref, w3_ref, t_ref, delta_ref,
                        g_ref, hq_ref, tmp_ref, h_ref):
    hp = lax.Precision.HIGHEST
    length = feats_ref.shape[0]
    half = length // 2

    @pl.when(pl.program_id(0) == 0)
    def _():
        fr = fr_ref[...]
        h1 = jnp.sin(fr * (jnp.dot(feats_ref[...], w1_ref[...], preferred_element_type=F32, precision=hp)
                           + b1_ref[...]))
        h_ref[...] = jnp.sin(fr * (jnp.dot(h1, w2_ref[...], preferred_element_type=F32, precision=hp) + b2_ref[...]))

    h = h_ref[...]
    mod =jnp.exp(-t_ref[...] * delta_ref[...]) + MOD_SHIFT
    row = lax.broadcasted_iota(jnp.int32, (length, 1), 0)
    alt = jnp.where(lax.broadcasted_iota(jnp.int32, (half, 1), 0) % 2 == 0, 1.0, -1.0).astype(F32)
    for o in range(2):
        fwd = jnp.dot(h, w3_ref[2 * o], preferred_element_type=F32, precision=hp) * mod
        bwd = jnp.dot(h, w3_ref[2 * o + 1], preferred_element_type=F32, precision=hp) * mod
        bwd = jnp.where(row == 0, 0.0, bwd)
        norm = (jnp.sum(jnp.abs(fwd), axis=0, keepdims=True)
                + jnp.sum(jnp.abs(bwd), axis=0, keepdims=True) + 1e-6)
        fwd = fwd / norm
        bwd = bwd / norm
        for part, g in enumerate((fwd + bwd, fwd - bwd)):
            even, odd = _split_parity(tmp_ref, g)
            g_ref[o, 2 * part] = even.astype(BF16)
            g_ref[o, 2 * part + 1] = odd.astype(BF16)
            mid = jnp.sum((even if part == 0 else odd) * alt, axis=0, keepdims=True) * (1.0 / length)
            hq_ref[o, part] = mid if part == 0 else -mid


def _filter_freq_kernel(ce_ref, se_ref, co_ref, so_ref, g_ref, h_ref):
    half = ce_ref.shape[0]
    row = lax.broadcasted_iota(jnp.int32, (half, 1), 0)
    scale = jnp.where(row == 0, 1.0, 2.0).astype(F32) * (1.0 / (4 * half))
    a = jnp.dot(ce_ref[...], g_ref[0, 0], preferred_element_type=F32)
    b = jnp.dot(co_ref[...], g_ref[0, 1], preferred_element_type=F32)
    c = jnp.dot(se_ref[...], g_ref[0, 2], preferred_element_type=F32)
    d = jnp.dot(so_ref[...], g_ref[0, 3], preferred_element_type=F32)
    h_ref[0, 0] = (a + b) * scale
    h_ref[0, 1] = -(c + d) * scale
    h_ref[0, 2] = (a - b) * scale
    h_ref[0, 3] = (c - d) * scale


def _hyena_filters(length, tables, w1, b1, w2, b2, w3, freq):
    t = np.linspace(0.0, 1.0, length, dtype=np.float32)[:, None]
    f = np.linspace(1e-4, FILT_BANDS - 1, FILT_BANDS, dtype=np.float32)[None]
    ang = np.float32(2.0 * math.pi / length) * np.arange(length, dtype=np.float32)[:, None] * f
    feats = np.concatenate([t, np.cos(ang), -np.sin(ang)], axis=-1).astype(np.float32)
    deltas = np.abs(np.linspace(math.log(DECAY_TARGET) / SLOW_DECAY_PCT,
                                math.log(DECAY_TARGET) / FAST_DECAY_PCT, HY_C, dtype=np.float32))[None]
    hid = w1.shape[1]
    w3r = w3.reshape(hid, 4, HY_C).transpose(1, 0, 2)
    tc = HY_TC
    full = lambda shape: pl.BlockSpec(shape, lambda c: (0,) * len(shape))
    half = length // 2
    g, hq = pl.pallas_call(
        _filter_time_kernel,
        out_shape=(jax.ShapeDtypeStruct((2, 4, half, HY_C), BF16),
                   jax.ShapeDtypeStruct((2, 2, 1, HY_C), F32)),
        grid=(HY_C // tc,),
        in_specs=[full(feats.shape), full(w1.shape), full((1, hid)), full(w2.shape), full((1, hid)),
                  full((1, hid)), pl.BlockSpec((4, hid, tc), lambda c: (0, 0, c)), full((length, 1)),
                  pl.BlockSpec((1, tc), lambda c: (0, c))],
        out_specs=(pl.BlockSpec((2, 4, half, tc), lambda c: (0, 0, 0, c)),
                   pl.BlockSpec((2, 2, 1, tc), lambda c: (0, 0, 0, c))),
        scratch_shapes=[pltpu.VMEM((tc // LANES, length, LANES), F32), pltpu.VMEM((length, hid), F32)],
        compiler_params=_params("arbitrary"),
        name="filter_time",
    )(feats, w1, b1.reshape(1, hid), w2, b2.reshape(1, hid), freq.reshape(1, hid), w3r, t, deltas)
    ce, se, co, so = tables[:4]
    table_spec = _resident((half, half), lambda o, c: (0, 0))
    spectra = pl.pallas_call(
        _filter_freq_kernel,
        out_shape=jax.ShapeDtypeStruct((2, 4, half, HY_C), F32),
        grid=(2, HY_C // tc),
        in_specs=[table_spec] * 4 + [pl.BlockSpec((1, 4, half, tc), lambda o, c: (o, 0, 0, c))],
        out_specs=pl.BlockSpec((1, 4, half, tc), lambda o, c: (o, 0, 0, c)),
        compiler_params=_params("parallel", "parallel"),
        name="filter_freq",
    )(ce, se, co, so, g)
    return spectra, hq


def _hyena_kernel(z0_ref, z1_ref, z2_ref, cw_ref, cb_ref, ce_ref, se_ref, co_ref, so_ref, cot_ref, sot_ref,
                  h_ref, hq_ref, skip_ref, o_ref, tmp_ref):
    length = z0_ref.shape[1]
    half = length // 2
    row = lax.broadcasted_iota(jnp.int32, (length, 1), 0)
    alt = jnp.where(lax.broadcasted_iota(jnp.int32, (half, 1), 0) % 2 == 0, 1.0, -1.0).astype(F32)

    def short_conv(z_ref, c):
        u = z_ref[0].astype(F32)
        prev = jnp.where(row == 0, 0.0, pltpu.roll(u, 1, axis=0))
        nxt = jnp.where(row == length - 1, 0.0, pltpu.roll(u, length - 1, axis=0))
        return cb_ref[c] + (prev * cw_ref[0, c] + u * cw_ref[1, c] + nxt * cw_ref[2, c])

    def long_conv(u, o):
        ue, uo = _split_parity(tmp_ref, u)
        ue_bf, uo_bf = ue.astype(BF16), uo.astype(BF16)
        a4 = jnp.sum(ue * alt, axis=0, keepdims=True)
        b4 = jnp.sum(uo * alt, axis=0, keepdims=True)
        hr4, hi4 = hq_ref[o, 0], hq_ref[o, 1]
        y_even = alt * (a4 * hr4 + b4 * hi4)
        y_odd = -(alt * (a4 * hi4 - b4 * hr4))
        for kt in range(half // HY_KT):
            ks = slice(kt * HY_KT, (kt + 1) * HY_KT)
            ae = jnp.dot(ce_ref[ks, :], ue_bf, preferred_element_type=F32)
            ao = jnp.dot(co_ref[ks, :], uo_bf, preferred_element_type=F32)
            be = jnp.dot(se_ref[ks, :], ue_bf, preferred_element_type=F32)
            bo = jnp.dot(so_ref[ks, :], uo_bf, preferred_element_type=F32)
            a_f, a_g, b_f, b_g = ae + ao, ae - ao, be + bo, bo - be
            hr_f, hi_f, hr_g, hi_g = (h_ref[o, j, ks, :] for j in range(4))
            re_f, im_f = a_f * hr_f + b_f * hi_f, a_f * hi_f - b_f * hr_f
            re_g, im_g = a_g * hr_g + b_g * hi_g, a_g * hi_g - b_g * hr_g
            y_even += (jnp.dot(ce_ref[:, ks], (re_f + re_g).astype(BF16), preferred_element_type=F32)
                       - jnp.dot(se_ref[:, ks], (im_f - im_g).astype(BF16), preferred_element_type=F32))
            y_odd += (jnp.dot(cot_ref[:, ks], (re_f - re_g).astype(BF16), preferred_element_type=F32)
                      - jnp.dot(sot_ref[:, ks], (im_f + im_g).astype(BF16), preferred_element_type=F32))
        return _merge_parity(tmp_ref, y_even, y_odd) + u * skip_ref[o]

    z = short_conv(z1_ref, 1) * long_conv(short_conv(z0_ref, 0), 0)
    o_ref[0] = (short_conv(z2_ref, 2) * long_conv(z, 1)).astype(o_ref.dtype)


def _hyena(z3, conv_w, conv_b, tables, spectra, hq, skip):
    b, s, _ = z3.shape
    half = s // 2
    tc = HY_TC
    nct = HY_C // tc
    cw = conv_w.reshape(3, 3, 1, HY_C)
    cb = conv_b.reshape(3, 1, HY_C)
    zspec = lambda chunk: pl.BlockSpec((1, s, tc), lambda c, i: (i, 0, chunk * nct + c))
    return pl.pallas_call(
        _hyena_kernel,
        out_shape=jax.ShapeDtypeStruct((b, s, HY_C), BF16),
        grid=(nct, b),
        in_specs=[zspec(0), zspec(1), zspec(2),
                  pl.BlockSpec((3, 3, 1, tc), lambda c, i: (0, 0, 0, c)),
                  pl.BlockSpec((3, 1, tc), lambda c, i: (0, 0, c))]
                 + [_resident((half, half), lambda c, i: (0, 0))] * 6
                 + [_resident((2, 4, half, tc), lambda c, i: (0, 0, 0, c)),
                    pl.BlockSpec((2, 2, 1, tc), lambda c, i: (0, 0, 0, c)),
                    pl.BlockSpec((2, 1, tc), lambda c, i: (0, 0, c))],
        out_specs=pl.BlockSpec((1, s, tc), lambda c, i: (i, 0, c)),
        scratch_shapes=[pltpu.VMEM((tc // LANES, s, LANES), F32)],
        compiler_params=_params("parallel", "parallel"),
        cost_estimate=pl.CostEstimate(flops=2 * 4 * 2 * b * s * half * HY_C, transcendentals=0,
                                      bytes_accessed=2 * 3 * b * s * HY_C + 4 * b * s * HY_C + 12 * half * half),
        name="hyena",
    )(z3, z3, z3, cw, cb, *tables, spectra, hq, skip.reshape(2, 1, HY_C))


def _rel_bucket(rel):
    half = NUM_BUCKETS // 2
    exact = half // 2
    n = np.abs(rel)
    large = exact + (np.log(np.maximum(n, 1) / exact) / np.log(REL_MAX_DIST / exact) * (half - exact)).astype(np.int32)
    large = np.minimum(large, half - 1)
    return (np.where(rel > 0, half, 0) + np.where(n < exact, n, large)).astype(np.int32)


def _att_tiles(length):
    kw = min(ATT_KW, length)
    tiles = []
    for qs in range(0, length, ATT_TQ):
        ks = min(max(qs - (kw - ATT_TQ) // 2, 0), length - kw)
        tiles.append((qs, ks, {0: 0, -64: 1, -128: 2}[ks - qs]))
    return kw, tiles


def _att_bias(rel_bias, group, seq):
    window, dil = AT_GROUPS[group]
    band = window // (2 * dil)
    kw, tiles = _att_tiles(seq // dil)
    offsets = sorted({ks - qs for qs, ks, _ in tiles}, reverse=True)
    table = rel_bias[:, group * AT_HEADS:(group + 1) * AT_HEADS].astype(F32)
    span = kw + ATT_TQ
    k = np.arange(span)
    rel = np.asarray(offsets)[:, None] + np.where(k < kw, k, k - span)[None, :]
    diag = jnp.where((np.abs(rel) <= band)[:, None, :],
                     jnp.swapaxes(table[_rel_bucket(rel * dil)], 1, 2), NEG_INF)
    return diag[:, :, None, :]


def _bias_tile(diag_ref, var, kw):
    rows = jnp.broadcast_to(diag_ref[var, 0], (ATT_TQ, diag_ref.shape[3]))
    return pltpu.roll(rows, 0, 1, stride=1, stride_axis=0)[:, :kw]


def _dilated_kernel(q1, q2, q3, k1, k2, k3, v1, v2, v3, b1, b2, b3, o_ref,
                    qn_ref, kn_ref, vn_ref, og_ref, lg_ref):
    seq = o_ref.shape[1]
    contract_last = (((1,), (1,)), ((), ()))
    for g, (q_ref, k_ref, v_ref, bias_ref) in enumerate(((q1, k1, v1, b1), (q2, k2, v2, b2), (q3, k3, v3, b3))):
        dil = AT_GROUPS[g][1]
        kw, tiles = _att_tiles(seq // dil)
        if dil > 1:
            qn_ref[...] = q_ref[0].astype(F32)
            kn_ref[...] = k_ref[0].astype(F32)
            vn_ref[...] = v_ref[0].astype(F32)
        bias = {var: _bias_tile(bias_ref, var, kw) for var in sorted({var for _, _, var in tiles})}
        rows = lambda r, start, size: (pl.ds(r + start * dil, size, stride=dil) if dil > 1 else pl.ds(start, size))
        work = [(r, qs, ks, var) for r in range(dil) for qs, ks, var in tiles]
        for first in range(0, len(work), ATT_BATCH):
            scores, values, places = [], [], []
            for r, qs, ks, var in work[first:first + ATT_BATCH]:
                if dil > 1:
                    qt = qn_ref[rows(r, qs, ATT_TQ), :].astype(BF16)
                    kt = kn_ref[rows(r, ks, kw), :].astype(BF16)
                    vt = vn_ref[rows(r, ks, kw), :].astype(BF16)
                else:
                    qt, kt, vt = (q_ref[0, rows(r, qs, ATT_TQ), :], k_ref[0, rows(r, ks, kw), :],
                                  v_ref[0, rows(r, ks, kw), :])
                scores.append(lax.dot_general(qt, kt, contract_last, preferred_element_type=F32) + bias[var])
                values.append(vt)
                places.append(rows(r, qs, ATT_TQ))
            softmax = []
            for s in scores:
                m = jnp.max(s, axis=-1, keepdims=True)
                p = jnp.exp(s - m)
                softmax.append((p.astype(BF16), jnp.sum(p, axis=-1, keepdims=True), m))
            for (p, z, m), vt, place in zip(softmax, values, places):
                og_ref[g, place, :] = jnp.dot(p, vt, preferred_element_type=F32) / z
                lg_ref[g, place, :] = jnp.broadcast_to(m + jnp.log(z), (ATT_TQ, HD))
    l0, l1, l2 = lg_ref[0], lg_ref[1], lg_ref[2]
    mx = jnp.maximum(jnp.maximum(l0, l1), l2)
    e0, e1, e2 = jnp.exp(l0 - mx), jnp.exp(l1 - mx), jnp.exp(l2 - mx)
    o_ref[0] = ((e0 * og_ref[0] + e1 * og_ref[1] + e2 * og_ref[2]) / (e0 + e1 + e2)).astype(o_ref.dtype)


def _dilated_attention(z3, rel_bias):
    b, s, _ = z3.shape
    ng = len(AT_GROUPS)
    col = lambda part, g: pl.BlockSpec((1, s, HD), lambda i, h: (i, 0, part * ng * AT_HEADS + g * AT_HEADS + h))
    biases = [_att_bias(rel_bias, g, s) for g in range(ng)]
    bias_spec = lambda a: pl.BlockSpec((a.shape[0], 1) + a.shape[2:], lambda i, h: (0, h, 0, 0))
    return pl.pallas_call(
        _dilated_kernel,
        out_shape=jax.ShapeDtypeStruct((b, s, AT_HEADS * HD), BF16),
        grid=(b, AT_HEADS),
        in_specs=[col(p, g) for p in range(3) for g in range(ng)] + [bias_spec(a) for a in biases],
        out_specs=pl.BlockSpec((1, s, HD), lambda i, h: (i, 0, h)),
        scratch_shapes=[pltpu.VMEM((s, HD), F32), pltpu.VMEM((s, HD), F32), pltpu.VMEM((s, HD), F32),
                        pltpu.VMEM((ng, s, HD), F32), pltpu.VMEM((ng, s, HD), F32)],
        compiler_params=_params("parallel", "parallel"),
        name="dilated_attention",
    )(*([z3] * 9), *biases)


ROUTER_LANES = 128
ROUTER_OUT = 8


def _router_kernel(x_ref, g_ref, w_ref, hp_ref, idx_ref, gate_ref, count_ref, carry_ref):
    tm = x_ref.shape[0]
    half = hp_ref.shape[1]

    @pl.when(pl.program_id(0) == 0)
    def _():
        carry_ref[...] = jnp.zeros_like(carry_ref)

    h = _rms_rows(x_ref[...], g_ref[...])
    hp_ref[...] = _bf16_bits(h[:, :half]) | (_bf16_bits(h[:, half:]) << 16)
    h_hi = h.astype(BF16)
    h_lo = (h - h_hi.astype(F32)).astype(BF16)
    logits = (jnp.dot(h_hi, w_ref[0], preferred_element_type=F32)
              + (jnp.dot(h_hi, w_ref[1], preferred_element_type=F32)
                 + jnp.dot(h_lo, w_ref[0], preferred_element_type=F32)))
    lane = lax.broadcasted_iota(jnp.int32, logits.shape, 1).astype(F32)
    logits = jnp.where(lane < N_EXPERTS, logits, -jnp.inf)
    m1 = jnp.max(logits, axis=-1, keepdims=True)
    i1 = jnp.min(jnp.where(logits == m1, lane, float(ROUTER_LANES)), axis=-1, keepdims=True)
    rest = jnp.where(lane == i1, -jnp.inf, logits)
    m2 = jnp.max(rest, axis=-1, keepdims=True)
    i2 = jnp.min(jnp.where(rest == m2, lane, float(ROUTER_LANES)), axis=-1, keepdims=True)
    e2 = jnp.exp(m2 - m1)
    den = 1.0 + e2
    gate_ref[...] = jnp.where(lane == 0, 1.0 / den, e2 / den)[:, :ROUTER_OUT]
    chosen = jnp.where((lane == i1) | (lane == i2), 1.0, 0.0)
    earlier = lax.broadcasted_iota(jnp.int32, (tm, tm), 0) > lax.broadcasted_iota(jnp.int32, (tm, tm), 1)
    carry = carry_ref[...]
    before = jnp.dot(jnp.where(earlier, 1.0, 0.0).astype(BF16), chosen.astype(BF16),
                     preferred_element_type=F32) + carry
    r1 = jnp.sum(jnp.where(lane == i1, before, 0.0), axis=-1, keepdims=True)
    r2 = jnp.sum(jnp.where(lane == i2, before, 0.0), axis=-1, keepdims=True)
    idx = jnp.where(lane == 0, i1, jnp.where(lane == 1, i2, jnp.where(lane == 2, r1, r2)))
    idx_ref[...] = idx[:, :ROUTER_OUT].astype(jnp.int32)
    carry = carry + jnp.sum(chosen, axis=0, keepdims=True)
    carry_ref[...] = carry
    count_ref[...] = carry.astype(jnp.int32)


def _router(x2, gain, router):
    rows, d = x2.shape
    tm = MOE_CHUNK
    w = jnp.zeros((d, ROUTER_LANES), F32).at[:, :N_EXPERTS].set(router.astype(F32))
    w_hi = w.astype(BF16)
    w = jnp.stack([w_hi, (w - w_hi.astype(F32)).astype(BF16)])
    row_spec = lambda width: pl.BlockSpec((tm, width), lambda i: (i, 0))
    return pl.pallas_call(
        _router_kernel,
        out_shape=(jax.ShapeDtypeStruct((rows, d // 2), jnp.uint32),
                   jax.ShapeDtypeStruct((rows, ROUTER_OUT), jnp.int32),
                   jax.ShapeDtypeStruct((rows, ROUTER_OUT), F32),
                   jax.ShapeDtypeStruct((1, ROUTER_LANES), jnp.int32)),
        grid=(rows // tm,),
        in_specs=[row_spec(d),
                  pl.BlockSpec((1, d), lambda i: (0, 0)),
                  pl.BlockSpec((2, d, ROUTER_LANES), lambda i: (0, 0, 0))],
        out_specs=(row_spec(d // 2), row_spec(ROUTER_OUT), row_spec(ROUTER_OUT),
                   pl.BlockSpec((1, ROUTER_LANES), lambda i: (0, 0))),
        scratch_shapes=[pltpu.VMEM((1, ROUTER_LANES), F32)],
        compiler_params=_params("arbitrary"),
        name="router",
    )(x2, gain.reshape(1, d), w)


def _sc_workers():
    info = plsc.get_sparse_core_info()
    mesh = plsc.VectorSubcoreMesh(core_axis_name="core", subcore_axis_name="subcore")
    return mesh, info.num_cores, info.num_subcores


def _pack_weight_rows(w, col_tile):
    r, c = w.shape
    mesh, nc, ns = _sc_workers()
    rs, ncol = SC_PACK_ROWS, c // col_tile
    per_w = (r // rs) * ncol // (nc * ns)
    assert per_w * nc * ns * rs * col_tile == r * c and per_w % 2 == 0 and col_tile % SC_LANES == 0
    params = pltpu.CompilerParams()
    if "needs_layout_passes" in pltpu.CompilerParams.__dataclass_fields__:
        params = dataclasses.replace(params, needs_layout_passes=False)

    @functools.partial(
        pl.kernel, mesh=mesh, out_type=jax.ShapeDtypeStruct((r // 2, c), jnp.uint32), compiler_params=params,
        cost_estimate=pl.CostEstimate(flops=r * c, transcendentals=0, bytes_accessed=6 * r * c),
        scratch_types=[pltpu.VMEM((2, rs, col_tile), F32), pltpu.VMEM((2, rs // 2, col_tile), jnp.uint32),
                       pltpu.SemaphoreType.DMA((2,)), pltpu.SemaphoreType.DMA((2,))])
    def pack(w_hbm, o_hbm, in_v, out_v, rsem, wsem):
        wid = lax.axis_index("subcore") * nc + lax.axis_index("core")

        @pl.loop(0, per_w // 2)
        def _(it):
            reads, writes = [], []
            for b in range(2):
                tile = wid * per_w + 2 * it + b
                r0 = pl.multiple_of((tile // ncol) * rs, rs)
                c0 = pl.multiple_of((tile % ncol) * col_tile, col_tile)
                reads.append(pltpu.make_async_copy(w_hbm.at[pl.ds(r0, rs), pl.ds(c0, col_tile)], in_v.at[b],
                                                   rsem.at[b]))
                writes.append(pltpu.make_async_copy(
                    out_v.at[b], o_hbm.at[pl.ds(pl.multiple_of(r0 // 2, rs // 2), rs // 2), pl.ds(c0, col_tile)],
                    wsem.at[b]))
            reads[0].start()
            reads[1].start()
            for b in range(2):
                reads[b].wait()
                for pair in range(rs // 2):
                    @plsc.parallel_loop(0, col_tile, step=SC_LANES, unroll=8)
                    def _(j):
                        packed = plsc.pack(in_v[b, 2 * pair, pl.ds(j, SC_LANES)],
                                           in_v[b, 2 * pair + 1, pl.ds(j, SC_LANES)],
                                           format=plsc.PackFormat.INTERLEAVED)
                        out_v[b, pair, pl.ds(j, SC_LANES)] = plsc.bitcast(packed, jnp.uint32)
                writes[b].start()
            writes[0].wait()
            writes[1].wait()

    return pack(w)


def _sc_token_rows(t, nc, ns, ch):
    per_w = t // (nc * ns)
    assert per_w * nc * ns == t and per_w % (2 * ch) == 0
    return per_w, per_w // ch


def _scatter_rows(table, dests, p_rows, after=()):
    t, w = table.shape
    nk = len(dests)
    mesh, nc, ns = _sc_workers()
    ch = SC_SCATTER_ROWS
    per_w, nit = _sc_token_rows(t, nc, ns, ch)

    @functools.partial(
        pl.kernel, mesh=mesh, out_type=jax.ShapeDtypeStruct((p_rows, w), table.dtype),
        scratch_types=[pltpu.VMEM((nk, nit, ch), jnp.int32), pltpu.VMEM((2, ch, w), table.dtype),
                       pltpu.SemaphoreType.DMA((2,)), pltpu.SemaphoreType.DMA((2, nk))])
    def scatter(table_hbm, *refs):
        dest_hbm, out_hbm = refs[:nk], refs[nk + len(after)]
        idx_v, rows_v, rsem, wsem = refs[nk + len(after) + 1:]
        wid = lax.axis_index("subcore") * nc + lax.axis_index("core")
        for k in range(nk):
            pltpu.sync_copy(dest_hbm[k].at[pl.ds(wid * nit, nit)], idx_v.at[k])

        @pl.loop(0, nit // 2)
        def _(it):
            reads = [pltpu.make_async_copy(table_hbm.at[pl.ds(wid * per_w + (2 * it + b) * ch, ch)],
                                           rows_v.at[b], rsem.at[b]) for b in range(2)]
            writes = [[pltpu.make_async_copy(rows_v.at[b], out_hbm.at[idx_v.at[k].at[2 * it + b]], wsem.at[b, k])
                       for k in range(nk)] for b in range(2)]
            reads[0].start()
            reads[1].start()
            for b in range(2):
                reads[b].wait()
                for k in range(nk):
                    writes[b][k].start()
            for b in range(2):
                for k in range(nk):
                    writes[b][k].wait()

    return scatter(table, *[d.reshape(t // ch, ch) for d in dests], *after)


def _gather_rows(table, idxs):
    t = idxs[0].shape[0]
    w = table.shape[1]
    nk = len(idxs)
    mesh, nc, ns = _sc_workers()
    ch = SC_ROWS
    per_w, nit = _sc_token_rows(t, nc, ns, ch)

    @functools.partial(
        pl.kernel, mesh=mesh, out_type=[jax.ShapeDtypeStruct((t, w), table.dtype)] * nk,
        scratch_types=[pltpu.VMEM((nk, nit, ch), jnp.int32), pltpu.VMEM((nk, 2, ch, w), table.dtype),
                       pltpu.SemaphoreType.DMA((nk, 2)), pltpu.SemaphoreType.DMA((nk, 2))])
    def gather(table_hbm, *refs):
        idx_hbm, out_hbm = refs[:nk], refs[nk:2 * nk]
        idx_v, rows_v, rsem, wsem = refs[2 * nk:]
        wid = lax.axis_index("subcore") * nc + lax.axis_index("core")
        for k in range(nk):
            pltpu.sync_copy(idx_hbm[k].at[pl.ds(wid * nit, nit)], idx_v.at[k])

        @pl.loop(0, nit // 2)
        def _(it):
            slots = [(k, b) for k in range(nk) for b in range(2)]
            reads = {(k, b): pltpu.make_async_copy(table_hbm.at[idx_v.at[k].at[2 * it + b]], rows_v.at[k, b],
                                                   rsem.at[k, b]) for k, b in slots}
            writes = {(k, b): pltpu.make_async_copy(rows_v.at[k, b],
                                                    out_hbm[k].at[pl.ds(wid * per_w + (2 * it + b) * ch, ch)],
                                                    wsem.at[k, b]) for k, b in slots}
            for s in slots:
                reads[s].start()
            for s in slots:
                reads[s].wait()
                writes[s].start()
            for s in slots:
                writes[s].wait()

    return gather(table, *[i.reshape(t // ch, ch) for i in idxs])


def _combine_kernel(x_ref, y0_ref, y1_ref, gate_ref, *rest):
    o_ref = rest[-1]
    half = x_ref.shape[1] // 2
    for part in range(2):
        cols = slice(part * half, (part + 1) * half)
        acc = x_ref[:, cols]
        for k, y_ref in enumerate((y0_ref, y1_ref)):
            word = y_ref[...]
            bits = (word << 16) if part == 0 else (word & jnp.uint32(0xFFFF0000))
            acc = acc + gate_ref[:, k:k + 1] * lax.bitcast_convert_type(bits, F32)
        o_ref[:, cols] = acc


def _combine(x2, y0, y1, gates, part, earlier=None):
    t, d = x2.shape
    tm = ROW_TILE
    steps = y0.shape[0] // tm
    here = lambda i: (part * steps + i, 0)
    in_specs = [pl.BlockSpec((tm, d), here),
                pl.BlockSpec((tm, d // 2), lambda i: (i, 0)),
                pl.BlockSpec((tm, d // 2), lambda i: (i, 0)),
                pl.BlockSpec((tm, TOP_K), here)]
    operands = [x2, y0, y1, gates]
    if earlier is not None:
        in_specs.append(pl.BlockSpec(memory_space=pl.ANY))
        operands.append(earlier)
    return pl.pallas_call(
        _combine_kernel,
        out_shape=jax.ShapeDtypeStruct((t, d), F32),
        grid=(steps,),
        in_specs=in_specs,
        out_specs=pl.BlockSpec((tm, d), here),
        input_output_aliases={} if earlier is None else {len(operands) - 1: 0},
        compiler_params=_params("parallel"),
        name="moe_combine",
    )(*operands)


def _moe(x2, gain, router, wg, wu, wd):
    t, d = x2.shape
    hp, idx, gate, count = _router(x2, gain, router)
    experts = jnp.arange(N_EXPERTS, dtype=jnp.int32)
    counts = count[0, :N_EXPERTS]
    padded = (counts + MOE_ROWS - 1) // MOE_ROWS * MOE_ROWS
    pend = jnp.cumsum(padded)
    pstart = pend - padded
    expert, rank = idx[:, :TOP_K], idx[:, TOP_K:2 * TOP_K]
    dest = jnp.sum(jnp.where(expert[:, :, None] == experts, pstart, 0), axis=-1) + rank
    dests = [dest[:, k] for k in range(TOP_K)]
    p_rows = t * TOP_K + N_EXPERTS * MOE_ROWS
    blk_row = jnp.arange(p_rows // MOE_ROWS, dtype=jnp.int32) * MOE_ROWS
    blk_expert = jnp.minimum(jnp.sum(pend[None, :] <= blk_row[:, None], axis=1), N_EXPERTS - 1).astype(jnp.int32)
    valid = jnp.clip(counts[blk_expert] - (blk_row - pstart[blk_expert]), 0, MOE_ROWS)
    valid = jnp.where(blk_row < pend[-1], valid, 0).astype(jnp.int32)
    n_exp, _, f = wg.shape
    pack = lambda w, col_tile: _pack_weight_rows(w.reshape(-1, w.shape[2]), col_tile).reshape(n_exp, -1, w.shape[2])
    wg, wu, wd = pack(wg, f // 2), pack(wu, f // 2), pack(wd, d)
    hb = _scatter_rows(hp, dests, p_rows, after=(wg, wu, wd))
    yb = _ffn_experts(hb, wg, wu, wd, blk_expert, valid)
    out = None
    for part in range(MOE_TAIL_PARTS):
        rows = slice(part * t // MOE_TAIL_PARTS, (part + 1) * t // MOE_TAIL_PARTS)
        y0, y1 = _gather_rows(yb, [dk[rows] for dk in dests])
        out = _combine(x2, y0, y1, gate[:, :TOP_K], part, out)
    return out


def kernel(x, mem, rel_bias, norm_mix, norm_mem, norm_ffn, w_mem_kv, xq_norm, xk_norm, w_out, hy_w_in, hy_conv_w, hy_conv_b, hy_filt_w1, hy_filt_b1, hy_filt_w2, hy_filt_b2, hy_filt_w3, hy_sin_freq, hy_skip, at_w_in, at_q_norm, at_k_norm, ffn_w_gate, ffn_w_up, ffn_w_down, moe_router, moe_w_gate, moe_w_up, moe_w_down):
    b, s, d = x.shape
    t = b * s
    m_len = mem.shape[1]
    x2 = x.reshape(t, d)
    mem2 = mem.reshape(b * m_len, d)
    bf = lambda w: w.astype(BF16)
    score_scale = HD ** -0.5
    xa_heads = XA_W // HD
    at_heads = AT_W // HD
    plain = lambda width: [None] * (width // HD)
    kv_gains = lambda i: [xk_norm[i]] * xa_heads + plain(XA_W)

    tables = _dft_tables(s)
    spectra, hq = _hyena_filters(s, tables, hy_filt_w1[0], hy_filt_b1[0], hy_filt_w2[0], hy_filt_b2[0],
                                 hy_filt_w3[0], hy_sin_freq[0])
    z = _norm_matmul(x2, norm_mix[0], bf(hy_w_in[0]), 1024,
                     plain(3 * HY_C) + [xq_norm[0] * score_scale] * xa_heads).reshape(b, s, -1)
    kv = _norm_matmul(mem2, norm_mem[0], bf(w_mem_kv[0]), 1024, kv_gains(0)).reshape(b, m_len, -1)
    self_out = _hyena(z, hy_conv_w[0], hy_conv_b[0], tables, spectra, hq, hy_skip[0])
    x2 = _mix_out(x2.reshape(b, s, d), self_out, z, 3 * HY_C // XA_W, kv, bf(w_out[0])).reshape(t, d)
    x2 = _ffn_dense(x2, norm_ffn[0], bf(ffn_w_gate[0]), bf(ffn_w_up[0]), bf(ffn_w_down[0]))

    at_gains = ([at_q_norm[0] * score_scale] * at_heads + [at_k_norm[0]] * at_heads + plain(AT_W)
                + [xq_norm[1] * score_scale] * xa_heads)
    z = _norm_matmul(x2, norm_mix[1], bf(at_w_in[0]), 1024, at_gains).reshape(b, s, -1)
    kv = _norm_matmul(mem2, norm_mem[1], bf(w_mem_kv[1]), 1024, kv_gains(1)).reshape(b, m_len, -1)
    self_out = _dilated_attention(z, rel_bias)
    x2 = _mix_out(x2.reshape(b, s, d), self_out, z, 3 * AT_W // XA_W, kv, bf(w_out[1])).reshape(t, d)
    x2 = _moe(x2, norm_ffn[1], moe_router[0], moe_w_gate[0], moe_w_up[0], moe_w_down[0])
    return x2.reshape(b, s, d)
```

```python
import dataclasses
import functools
import math

import jax
import jax.numpy as jnp
import numpy as np
from jax import lax
from jax.experimental import pallas as pl
from jax.experimental.pallas import tpu as pltpu
from jax.experimental.pallas import tpu_sc as plsc

F32 = jnp.float32
BF16 = jnp.bfloat16

EPS = 1e-6
HY_C = 512
FILT_BANDS = 16
DECAY_TARGET = 1e-2
FAST_DECAY_PCT = 0.3
SLOW_DECAY_PCT = 1.5
MOD_SHIFT = 0.05
AT_GROUPS = ((128, 1), (512, 4), (2048, 16))
AT_HEADS = 4
HD = 128
AT_W = 1536
NUM_BUCKETS = 32
REL_MAX_DIST = 1024
NEG_INF = -1e30
XA_W = 512
N_EXPERTS = 8
TOP_K = 2

VMEM_LIMIT_BYTES = 56 * 1024 * 1024
ROW_TILE = 1024
PROJ_ROW_TILE = 1024
FF_TILE = 256
MOE_ROWS = 512
MOE_CHUNK = 512
MOE_TAIL_PARTS = 4
SC_ROWS = 32
SC_SCATTER_ROWS = 64
SC_LANES = 16
SC_PACK_ROWS = 16
ATT_TQ = 128
ATT_KW = 256
ATT_BATCH = 8
HY_TC = 256
HY_KT = 1024


def _params(*sem):
    return pltpu.CompilerParams(dimension_semantics=sem, vmem_limit_bytes=VMEM_LIMIT_BYTES)


def _rms_rows(x, gain):
    return x * lax.rsqrt(jnp.mean(x * x, axis=-1, keepdims=True) + EPS) * gain


def _bf16_bits(v):
    u = lax.bitcast_convert_type(v, jnp.uint32)
    return (u + jnp.uint32(0x7FFF) + ((u >> 16) & jnp.uint32(1))) >> 16


def _resident(shape, index_map):
    return pl.BlockSpec(shape, index_map, pipeline_mode=pl.Buffered(1))


def _norm_matmul_kernel(x_ref, g_ref, w_ref, hg_ref, o_ref, *, tn, head_norm):
    h = _rms_rows(x_ref[...], g_ref[...]).astype(BF16)
    for c in range(o_ref.shape[1] // tn):
        acc = jnp.dot(h, w_ref[:, c * tn:(c + 1) * tn], preferred_element_type=F32)
        for j in range(tn // HD):
            cols = slice(c * tn + j * HD, c * tn + (j + 1) * HD)
            seg = acc[:, j * HD:(j + 1) * HD]
            if head_norm[cols.start // HD]:
                seg = _rms_rows(seg, hg_ref[:, cols])
            o_ref[:, cols] = seg.astype(o_ref.dtype)


def _norm_matmul(x2, gain, w_bf, tn, head_gains):
    rows, d = x2.shape
    n = w_bf.shape[1]
    tm = min(PROJ_ROW_TILE, rows)
    assert len(head_gains) * HD == n
    hg = jnp.concatenate([jnp.ones((HD,), F32) if g is None else g.astype(F32) for g in head_gains]).reshape(1, n)
    return pl.pallas_call(
        functools.partial(_norm_matmul_kernel, tn=tn, head_norm=tuple(g is not None for g in head_gains)),
        out_shape=jax.ShapeDtypeStruct((rows, n), BF16),
        grid=(rows // tm,),
        in_specs=[pl.BlockSpec((tm, d), lambda i: (i, 0)),
                  pl.BlockSpec((1, d), lambda i: (0, 0)),
                  _resident((d, n), lambda i: (0, 0)),
                  pl.BlockSpec((1, n), lambda i: (0, 0))],
        out_specs=pl.BlockSpec((tm, n), lambda i: (i, 0)),
        compiler_params=_params("parallel"),
        cost_estimate=pl.CostEstimate(flops=2 * rows * d * n, transcendentals=rows,
                                      bytes_accessed=4 * rows * d + 2 * d * n + 2 * rows * n),
        name="norm_matmul",
    )(x2, gain.reshape(1, d), w_bf, hg)


def _mix_out_kernel(x_ref, a_ref, xq_ref, kv_ref, wa_ref, wc_ref, o_ref):
    heads = []
    for h in range(XA_W // HD):
        cols = slice(h * HD, (h + 1) * HD)
        s = lax.dot_general(xq_ref[0, :, cols], kv_ref[0, :, cols], (((1,), (1,)), ((), ())),
                            preferred_element_type=F32)
        p = jnp.exp(s - jnp.max(s, axis=-1, keepdims=True))
        z = jnp.sum(p, axis=-1, keepdims=True)
        pv = jnp.dot(p.astype(BF16), kv_ref[0, :, XA_W + h * HD:XA_W + (h + 1) * HD], preferred_element_type=F32)
        heads.append((pv / z).astype(BF16))
    cross = jnp.concatenate(heads, axis=1)
    o_ref[0] = (x_ref[0]
                + jnp.dot(a_ref[0], wa_ref[...], preferred_element_type=F32)
                + jnp.dot(cross, wc_ref[...], preferred_element_type=F32))


def _mix_out(x3, self_out, z3, xq_block, kv3, w_bf):
    b, s, d = x3.shape
    half = self_out.shape[2]
    m = kv3.shape[1]
    tm = ROW_TILE
    return pl.pallas_call(
        _mix_out_kernel,
        out_shape=jax.ShapeDtypeStruct((b, s, d), F32),
        grid=(b, s // tm),
        in_specs=[pl.BlockSpec((1, tm, d), lambda i, j: (i, j, 0)),
                  pl.BlockSpec((1, tm, half), lambda i, j: (i, j, 0)),
                  pl.BlockSpec((1, tm, XA_W), lambda i, j: (i, j, xq_block)),
                  pl.BlockSpec((1, m, 2 * XA_W), lambda i, j: (i, 0, 0)),
                  pl.BlockSpec((half, d), lambda i, j: (0, 0)),
                  pl.BlockSpec((d - half, d), lambda i, j: (1, 0))],
        out_specs=pl.BlockSpec((1, tm, d), lambda i, j: (i, j, 0)),
        compiler_params=_params("parallel", "parallel"),
        name="mix_out",
    )(x3, self_out, z3, kv3, w_bf, w_bf)


def _weight_rows(ref, start, size):
    if ref.dtype == jnp.uint32:
        return lambda cols: pltpu.bitcast(ref[start // 2:(start + size) // 2, cols], BF16)
    return lambda cols: ref[start:start + size, cols]


def _weight_shape(ref):
    return (ref.shape[0] * (2 if ref.dtype == jnp.uint32 else 1), ref.shape[1])


def _swiglu(h, wg_ref, wu_ref, wd_ref, tf):
    d, f = _weight_shape(wg_ref)
    assert f % tf == 0
    y = None
    for j in range(f // tf):
        cols = slice(j * tf, (j + 1) * tf)
        gg = jnp.dot(h, _weight_rows(wg_ref, 0, d)(cols), preferred_element_type=F32)
        uu = jnp.dot(h, _weight_rows(wu_ref, 0, d)(cols), preferred_element_type=F32)
        a = ((gg * jax.nn.sigmoid(gg)) * uu).astype(BF16)
        part = jnp.dot(a, _weight_rows(wd_ref, j * tf, tf)(slice(None)), preferred_element_type=F32)
        y = part if y is None else y + part
    return y


def _ffn_dense_kernel(x_ref, g_ref, wg_ref, wu_ref, wd_ref, o_ref, *, tf):
    x = x_ref[...]
    h = _rms_rows(x, g_ref[...]).astype(BF16)
    o_ref[...] = x + _swiglu(h, wg_ref, wu_ref, wd_ref, tf)


def _ffn_dense(x2, gain, wg, wu, wd):
    rows, d = x2.shape
    f = wg.shape[1]
    tm = ROW_TILE
    return pl.pallas_call(
        functools.partial(_ffn_dense_kernel, tf=FF_TILE),
        out_shape=jax.ShapeDtypeStruct((rows, d), F32),
        grid=(rows // tm,),
        in_specs=[pl.BlockSpec((tm, d), lambda i: (i, 0)),
                  pl.BlockSpec((1, d), lambda i: (0, 0)),
                  _resident((d, f), lambda i: (0, 0)),
                  _resident((d, f), lambda i: (0, 0)),
                  _resident((f, d), lambda i: (0, 0))],
        out_specs=pl.BlockSpec((tm, d), lambda i: (i, 0)),
        compiler_params=_params("parallel"),
        cost_estimate=pl.CostEstimate(flops=6 * rows * d * f, transcendentals=rows * f,
                                      bytes_accessed=8 * rows * d + 6 * d * f),
        name="ffn_dense",
    )(x2, gain.reshape(1, d), wg, wu, wd)


def _ffn_expert_kernel(eid_ref, valid_ref, hp_ref, wg_ref, wu_ref, wd_ref, o_ref, h_ref, *, tf):
    half = hp_ref.shape[1]
    valid = valid_ref[pl.program_id(0)]

    @pl.when(valid > 0)
    def _():
        keep = lax.broadcasted_iota(jnp.int32, (hp_ref.shape[0], 1), 0) < valid
        word = hp_ref[...]
        h_ref[:, :half] = jnp.where(keep, lax.bitcast_convert_type(word << 16, F32), 0.0).astype(BF16)
        h_ref[:, half:] = jnp.where(keep, lax.bitcast_convert_type(word & jnp.uint32(0xFFFF0000), F32),
                                    0.0).astype(BF16)
        y = _swiglu(h_ref[...], wg_ref.at[0], wu_ref.at[0], wd_ref.at[0], tf)
        o_ref[...] = _bf16_bits(y[:, :half]) | (_bf16_bits(y[:, half:]) << 16)

    @pl.when(valid <= 0)
    def _():
        o_ref[...] = jnp.zeros_like(o_ref)


def _ffn_experts(hp, wg, wu, wd, eid, valid):
    rows, half = hp.shape
    d = 2 * half
    tm = MOE_ROWS
    expert_spec = lambda w: pl.BlockSpec((1,) + w.shape[1:], lambda i, e, n: (e[i], 0, 0))
    grid_spec = pltpu.PrefetchScalarGridSpec(
        num_scalar_prefetch=2,
        grid=(rows // tm,),
        in_specs=[pl.BlockSpec((tm, half), lambda i, e, n: (i, 0)),
                  expert_spec(wg), expert_spec(wu), expert_spec(wd)],
        out_specs=pl.BlockSpec((tm, half), lambda i, e, n: (i, 0)),
        scratch_shapes=[pltpu.VMEM((tm, d), BF16)],
    )
    return pl.pallas_call(
        functools.partial(_ffn_expert_kernel, tf=FF_TILE),
        out_shape=jax.ShapeDtypeStruct((rows, half), jnp.uint32),
        grid_spec=grid_spec,
        compiler_params=_params("arbitrary"),
        name="ffn_experts",
    )(eid, valid, hp, wg, wu, wd)


def _dft_kernel(ce_ref, se_ref, co_ref, so_ref, cot_ref, sot_ref, base_ref, *, n_fft):
    rows, cols = ce_ref.shape
    i = pl.program_id(0)
    theta = 2.0 * math.pi / n_fft

    def phases(r, c):
        return r * (2 * c), r * (2 * c + 1), c * (2 * r + 1)

    @pl.when(i == 0)
    def _():
        r = lax.broadcasted_iota(jnp.int32, (rows, cols), 0)
        c = lax.broadcasted_iota(jnp.int32, (rows, cols), 1)
        for f, ph in enumerate(phases(r, c)):
            ang = (ph & (n_fft - 1)).astype(F32) * theta
            base_ref[2 * f] = jnp.cos(ang)
            base_ref[2 * f + 1] = jnp.sin(ang)

    c = lax.broadcasted_iota(jnp.int32, (8, cols), 1)
    r0 = i * rows
    shifts = (r0 * (2 * c), r0 * (2 * c + 1), c * (2 * r0))
    for f, (c_ref, s_ref) in enumerate(((ce_ref, se_ref), (co_ref, so_ref), (cot_ref, sot_ref))):
        ang = (shifts[f] & (n_fft - 1)).astype(F32) * theta
        ca, sa = jnp.cos(ang)[0:1], jnp.sin(ang)[0:1]
        cb, sb = base_ref[2 * f], base_ref[2 * f + 1]
        c_ref[...] = (cb * ca - sb * sa).astype(BF16)
        s_ref[...] = (sb * ca + cb * sa).astype(BF16)


def _dft_tables(length):
    half, rows = length // 2, 128
    shape = jax.ShapeDtypeStruct((half, half), BF16)
    spec = pl.BlockSpec((rows, half), lambda i: (i, 0))
    return pl.pallas_call(
        functools.partial(_dft_kernel, n_fft=2 * length),
        out_shape=(shape,) * 6,
        grid=(half // rows,),
        out_specs=(spec,) * 6,
        scratch_shapes=[pltpu.VMEM((6, rows, half), F32)],
        compiler_params=_params("arbitrary"),
        name="dft_tables",
    )()


LANES = 128


def _split_parity(tmp_ref, x):
    half = x.shape[0] // 2
    for j in range(tmp_ref.shape[0]):
        tmp_ref[j] = x[:, j * LANES:(j + 1) * LANES]
    pick = lambda start: jnp.concatenate(
        [tmp_ref[j, pl.ds(start, half, stride=2), :] for j in range(tmp_ref.shape[0])], axis=1)
    return pick(0), pick(1)


def _merge_parity(tmp_ref, even, odd):
    half = even.shape[0]
    for j in range(tmp_ref.shape[0]):
        tmp_ref[j, pl.ds(0, half, stride=2), :] = even[:, j * LANES:(j + 1) * LANES]
        tmp_ref[j, pl.ds(1, half, stride=2), :] = odd[:, j * LANES:(j + 1) * LANES]
    return jnp.concatenate([tmp_ref[j] for j in range(tmp_ref.shape[0])], axis=1)


def _filter_time_kernel(feats_ref, w1_ref, b1_ref, w2_ref, b2_ref, fr_ref, w3_ref, t_ref, delta_ref,
                        g_ref, hq_ref, tmp_ref, h_ref):
    hp = lax.Precision.HIGHEST
    length = feats_ref.shape[0]
    half = length // 2

    @pl.when(pl.program_id(0) == 0)
    def _():
        fr = fr_ref[...]
        h1 = jnp.sin(fr * (jnp.dot(feats_ref[...], w1_ref[...], preferred_element_type=F32, precision=hp)
                           + b1_ref[...]))
        h_ref[...] = jnp.sin(fr * (jnp.dot(h1, w2_ref[...], preferred_element_type=F32, precision=hp) + b2_ref[...]))

    h = h_ref[...]
    mod =jnp.exp(-t_ref[...] * delta_ref[...]) + MOD_SHIFT
    row = lax.broadcasted_iota(jnp.int32, (length, 1), 0)
    alt = jnp.where(lax.broadcasted_iota(jnp.int32, (half, 1), 0) % 2 == 0, 1.0, -1.0).astype(F32)
    for o in range(2):
        fwd = jnp.dot(h, w3_ref[2 * o], preferred_element_type=F32, precision=hp) * mod
        bwd = jnp.dot(h, w3_ref[2 * o + 1], preferred_element_type=F32, precision=hp) * mod
        bwd = jnp.where(row == 0, 0.0, bwd)
        norm = (jnp.sum(jnp.abs(fwd), axis=0, keepdims=True)
                + jnp.sum(jnp.abs(bwd), axis=0, keepdims=True) + 1e-6)
        fwd = fwd / norm
        bwd = bwd / norm
        for part, g in enumerate((fwd + bwd, fwd - bwd)):
            even, odd = _split_parity(tmp_ref, g)
            g_ref[o, 2 * part] = even.astype(BF16)
            g_ref[o, 2 * part + 1] = odd.astype(BF16)
            mid = jnp.sum((even if part == 0 else odd) * alt, axis=0, keepdims=True) * (1.0 / length)
            hq_ref[o, part] = mid if part == 0 else -mid


def _filter_freq_kernel(ce_ref, se_ref, co_ref, so_ref, g_ref, h_ref):
    half = ce_ref.shape[0]
    row = lax.broadcasted_iota(jnp.int32, (half, 1), 0)
    scale = jnp.where(row == 0, 1.0, 2.0).astype(F32) * (1.0 / (4 * half))
    a = jnp.dot(ce_ref[...], g_ref[0, 0], preferred_element_type=F32)
    b = jnp.dot(co_ref[...], g_ref[0, 1], preferred_element_type=F32)
    c = jnp.dot(se_ref[...], g_ref[0, 2], preferred_element_type=F32)
    d = jnp.dot(so_ref[...], g_ref[0, 3], preferred_element_type=F32)
    h_ref[0, 0] = (a + b) * scale
    h_ref[0, 1] = -(c + d) * scale
    h_ref[0, 2] = (a - b) * scale
    h_ref[0, 3] = (c - d) * scale


def _hyena_filters(length, tables, w1, b1, w2, b2, w3, freq):
    t = np.linspace(0.0, 1.0, length, dtype=np.float32)[:, None]
    f = np.linspace(1e-4, FILT_BANDS - 1, FILT_BANDS, dtype=np.float32)[None]
    ang = np.float32(2.0 * math.pi / length) * np.arange(length, dtype=np.float32)[:, None] * f
    feats = np.concatenate([t, np.cos(ang), -np.sin(ang)], axis=-1).astype(np.float32)
    deltas = np.abs(np.linspace(math.log(DECAY_TARGET) / SLOW_DECAY_PCT,
                                math.log(DECAY_TARGET) / FAST_DECAY_PCT, HY_C, dtype=np.float32))[None]
    hid = w1.shape[1]
    w3r = w3.reshape(hid, 4, HY_C).transpose(1, 0, 2)
    tc = HY_TC
    full = lambda shape: pl.BlockSpec(shape, lambda c: (0,) * len(shape))
    half = length // 2
    g, hq = pl.pallas_call(
        _filter_time_kernel,
        out_shape=(jax.ShapeDtypeStruct((2, 4, half, HY_C), BF16),
                   jax.ShapeDtypeStruct((2, 2, 1, HY_C), F32)),
        grid=(HY_C // tc,),
        in_specs=[full(feats.shape), full(w1.shape), full((1, hid)), full(w2.shape), full((1, hid)),
                  full((1, hid)), pl.BlockSpec((4, hid, tc), lambda c: (0, 0, c)), full((length, 1)),
                  pl.BlockSpec((1, tc), lambda c: (0, c))],
        out_specs=(pl.BlockSpec((2, 4, half, tc), lambda c: (0, 0, 0, c)),
                   pl.BlockSpec((2, 2, 1, tc), lambda c: (0, 0, 0, c))),
        scratch_shapes=[pltpu.VMEM((tc // LANES, length, LANES), F32), pltpu.VMEM((length, hid), F32)],
        compiler_params=_params("arbitrary"),
        name="filter_time",
    )(feats, w1, b1.reshape(1, hid), w2, b2.reshape(1, hid), freq.reshape(1, hid), w3r, t, deltas)
    ce, se, co, so = tables[:4]
    table_spec = _resident((half, half), lambda o, c: (0, 0))
    spectra = pl.pallas_call(
        _filter_freq_kernel,
        out_shape=jax.ShapeDtypeStruct((2, 4, half, HY_C), F32),
        grid=(2, HY_C // tc),
        in_specs=[table_spec] * 4 + [pl.BlockSpec((1, 4, half, tc), lambda o, c: (o, 0, 0, c))],
        out_specs=pl.BlockSpec((1, 4, half, tc), lambda o, c: (o, 0, 0, c)),
        compiler_params=_params("parallel", "parallel"),
        name="filter_freq",
    )(ce, se, co, so, g)
    return spectra, hq


def _hyena_kernel(z0_ref, z1_ref, z2_ref, cw_ref, cb_ref, ce_ref, se_ref, co_ref, so_ref, cot_ref, sot_ref,
                  h_ref, hq_ref, skip_ref, o_ref, tmp_ref):
    length = z0_ref.shape[1]
    half = length // 2
    row = lax.broadcasted_iota(jnp.int32, (length, 1), 0)
    alt = jnp.where(lax.broadcasted_iota(jnp.int32, (half, 1), 0) % 2 == 0, 1.0, -1.0).astype(F32)

    def short_conv(z_ref, c):
        u = z_ref[0].astype(F32)
        prev = jnp.where(row == 0, 0.0, pltpu.roll(u, 1, axis=0))
        nxt = jnp.where(row == length - 1, 0.0, pltpu.roll(u, length - 1, axis=0))
        return cb_ref[c] + (prev * cw_ref[0, c] + u * cw_ref[1, c] + nxt * cw_ref[2, c])

    def long_conv(u, o):
        ue, uo = _split_parity(tmp_ref, u)
        ue_bf, uo_bf = ue.astype(BF16), uo.astype(BF16)
        a4 = jnp.sum(ue * alt, axis=0, keepdims=True)
        b4 = jnp.sum(uo * alt, axis=0, keepdims=True)
        hr4, hi4 = hq_ref[o, 0], hq_ref[o, 1]
        y_even = alt * (a4 * hr4 + b4 * hi4)
        y_odd = -(alt * (a4 * hi4 - b4 * hr4))
        for kt in range(half // HY_KT):
            ks = slice(kt * HY_KT, (kt + 1) * HY_KT)
            ae = jnp.dot(ce_ref[ks, :], ue_bf, preferred_element_type=F32)
            ao = jnp.dot(co_ref[ks, :], uo_bf, preferred_element_type=F32)
            be = jnp.dot(se_ref[ks, :], ue_bf, preferred_element_type=F32)
            bo = jnp.dot(so_ref[ks, :], uo_bf, preferred_element_type=F32)
            a_f, a_g, b_f, b_g = ae + ao, ae - ao, be + bo, bo - be
            hr_f, hi_f, hr_g, hi_g = (h_ref[o, j, ks, :] for j in range(4))
            re_f, im_f = a_f * hr_f + b_f * hi_f, a_f * hi_f - b_f * hr_f
            re_g, im_g = a_g * hr_g + b_g * hi_g, a_g * hi_g - b_g * hr_g
            y_even += (jnp.dot(ce_ref[:, ks], (re_f + re_g).astype(BF16), preferred_element_type=F32)
                       - jnp.dot(se_ref[:, ks], (im_f - im_g).astype(BF16), preferred_element_type=F32))
            y_odd += (jnp.dot(cot_ref[:, ks], (re_f - re_g).astype(BF16), preferred_element_type=F32)
                      - jnp.dot(sot_ref[:, ks], (im_f + im_g).astype(BF16), preferred_element_type=F32))
        return _merge_parity(tmp_ref, y_even, y_odd) + u * skip_ref[o]

    z = short_conv(z1_ref, 1) * long_conv(short_conv(z0_ref, 0), 0)
    o_ref[0] = (short_conv(z2_ref, 2) * long_conv(z, 1)).astype(o_ref.dtype)


def _hyena(z3, conv_w, conv_b, tables, spectra, hq, skip):
    b, s, _ = z3.shape
    half = s // 2
    tc = HY_TC
    nct = HY_C // tc
    cw = conv_w.reshape(3, 3, 1, HY_C)
    cb = conv_b.reshape(3, 1, HY_C)
    zspec = lambda chunk: pl.BlockSpec((1, s, tc), lambda c, i: (i, 0, chunk * nct + c))
    return pl.pallas_call(
        _hyena_kernel,
        out_shape=jax.ShapeDtypeStruct((b, s, HY_C), BF16),
        grid=(nct, b),
        in_specs=[zspec(0), zspec(1), zspec(2),
                  pl.BlockSpec((3, 3, 1, tc), lambda c, i: (0, 0, 0, c)),
                  pl.BlockSpec((3, 1, tc), lambda c, i: (0, 0, c))]
                 + [_resident((half, half), lambda c, i: (0, 0))] * 6
                 + [_resident((2, 4, half, tc), lambda c, i: (0, 0, 0, c)),
                    pl.BlockSpec((2, 2, 1, tc), lambda c, i: (0, 0, 0, c)),
                    pl.BlockSpec((2, 1, tc), lambda c, i: (0, 0, c))],
        out_specs=pl.BlockSpec((1, s, tc), lambda c, i: (i, 0, c)),
        scratch_shapes=[pltpu.VMEM((tc // LANES, s, LANES), F32)],
        compiler_params=_params("parallel", "parallel"),
        cost_estimate=pl.CostEstimate(flops=2 * 4 * 2 * b * s * half * HY_C, transcendentals=0,
                                      bytes_accessed=2 * 3 * b * s * HY_C + 4 * b * s * HY_C + 12 * half * half),
        name="hyena",
    )(z3, z3, z3, cw, cb, *tables, spectra, hq, skip.reshape(2, 1, HY_C))


def _rel_bucket(rel):
    half = NUM_BUCKETS // 2
    exact = half // 2
    n = np.abs(rel)
    large = exact + (np.log(np.maximum(n, 1) / exact) / np.log(REL_MAX_DIST / exact) * (half - exact)).astype(np.int32)
    large = np.minimum(large, half - 1)
    return (np.where(rel > 0, half, 0) + np.where(n < exact, n, large)).astype(np.int32)


def _att_tiles(length):
    kw = min(ATT_KW, length)
    tiles = []
    for qs in range(0, length, ATT_TQ):
        ks = min(max(qs - (kw - ATT_TQ) // 2, 0), length - kw)
        tiles.append((qs, ks, {0: 0, -64: 1, -128: 2}[ks - qs]))
    return kw, tiles


def _att_bias(rel_bias, group, seq):
    window, dil = AT_GROUPS[group]
    band = window // (2 * dil)
    kw, tiles = _att_tiles(seq // dil)
    offsets = sorted({ks - qs for qs, ks, _ in tiles}, reverse=True)
    table = rel_bias[:, group * AT_HEADS:(group + 1) * AT_HEADS].astype(F32)
    span = kw + ATT_TQ
    k = np.arange(span)
    rel = np.asarray(offsets)[:, None] + np.where(k < kw, k, k - span)[None, :]
    diag = jnp.where((np.abs(rel) <= band)[:, None, :],
                     jnp.swapaxes(table[_rel_bucket(rel * dil)], 1, 2), NEG_INF)
    return diag[:, :, None, :]


def _bias_tile(diag_ref, var, kw):
    rows = jnp.broadcast_to(diag_ref[var, 0], (ATT_TQ, diag_ref.shape[3]))
    return pltpu.roll(rows, 0, 1, stride=1, stride_axis=0)[:, :kw]


def _dilated_kernel(q1, q2, q3, k1, k2, k3, v1, v2, v3, b1, b2, b3, o_ref,
                    qn_ref, kn_ref, vn_ref, og_ref, lg_ref):
    seq = o_ref.shape[1]
    contract_last = (((1,), (1,)), ((), ()))
    for g, (q_ref, k_ref, v_ref, bias_ref) in enumerate(((q1, k1, v1, b1), (q2, k2, v2, b2), (q3, k3, v3, b3))):
        dil = AT_GROUPS[g][1]
        kw, tiles = _att_tiles(seq // dil)
        if dil > 1:
            qn_ref[...] = q_ref[0].astype(F32)
            kn_ref[...] = k_ref[0].astype(F32)
            vn_ref[...] = v_ref[0].astype(F32)
        bias = {var: _bias_tile(bias_ref, var, kw) for var in sorted({var for _, _, var in tiles})}
        rows = lambda r, start, size: (pl.ds(r + start * dil, size, stride=dil) if dil > 1 else pl.ds(start, size))
        work = [(r, qs, ks, var) for r in range(dil) for qs, ks, var in tiles]
        for first in range(0, len(work), ATT_BATCH):
            scores, values, places = [], [], []
            for r, qs, ks, var in work[first:first + ATT_BATCH]:
                if dil > 1:
                    qt = qn_ref[rows(r, qs, ATT_TQ), :].astype(BF16)
                    kt = kn_ref[rows(r, ks, kw), :].astype(BF16)
                    vt = vn_ref[rows(r, ks, kw), :].astype(BF16)
                else:
                    qt, kt, vt = (q_ref[0, rows(r, qs, ATT_TQ), :], k_ref[0, rows(r, ks, kw), :],
                                  v_ref[0, rows(r, ks, kw), :])
                scores.append(lax.dot_general(qt, kt, contract_last, preferred_element_type=F32) + bias[var])
                values.append(vt)
                places.append(rows(r, qs, ATT_TQ))
            softmax = []
            for s in scores:
                m = jnp.max(s, axis=-1, keepdims=True)
                p = jnp.exp(s - m)
                softmax.append((p.astype(BF16), jnp.sum(p, axis=-1, keepdims=True), m))
            for (p, z, m), vt, place in zip(softmax, values, places):
                og_ref[g, place, :] = jnp.dot(p, vt, preferred_element_type=F32) / z
                lg_ref[g, place, :] = jnp.broadcast_to(m + jnp.log(z), (ATT_TQ, HD))
    l0, l1, l2 = lg_ref[0], lg_ref[1], lg_ref[2]
    mx = jnp.maximum(jnp.maximum(l0, l1), l2)
    e0, e1, e2 = jnp.exp(l0 - mx), jnp.exp(l1 - mx), jnp.exp(l2 - mx)
    o_ref[0] = ((e0 * og_ref[0] + e1 * og_ref[1] + e2 * og_ref[2]) / (e0 + e1 + e2)).astype(o_ref.dtype)


def _dilated_attention(z3, rel_bias):
    b, s, _ = z3.shape
    ng = len(AT_GROUPS)
    col = lambda part, g: pl.BlockSpec((1, s, HD), lambda i, h: (i, 0, part * ng * AT_HEADS + g * AT_HEADS + h))
    biases = [_att_bias(rel_bias, g, s) for g in range(ng)]
    bias_spec = lambda a: pl.BlockSpec((a.shape[0], 1) + a.shape[2:], lambda i, h: (0, h, 0, 0))
    return pl.pallas_call(
        _dilated_kernel,
        out_shape=jax.ShapeDtypeStruct((b, s, AT_HEADS * HD), BF16),
        grid=(b, AT_HEADS),
        in_specs=[col(p, g) for p in range(3) for g in range(ng)] + [bias_spec(a) for a in biases],
        out_specs=pl.BlockSpec((1, s, HD), lambda i, h: (i, 0, h)),
        scratch_shapes=[pltpu.VMEM((s, HD), F32), pltpu.VMEM((s, HD), F32), pltpu.VMEM((s, HD), F32),
                        pltpu.VMEM((ng, s, HD), F32), pltpu.VMEM((ng, s, HD), F32)],
        compiler_params=_params("parallel", "parallel"),
        name="dilated_attention",
    )(*([z3] * 9), *biases)


ROUTER_LANES = 128


def _router_kernel(x_ref, g_ref, w_ref, hp_ref, idx_ref, gate_ref, count_ref, carry_ref):
    tm = x_ref.shape[0]
    half = hp_ref.shape[1]

    @pl.when(pl.program_id(0) == 0)
    def _():
        carry_ref[...] = jnp.zeros_like(carry_ref)

    h = _rms_rows(x_ref[...], g_ref[...])
    hp_ref[...] = _bf16_bits(h[:, :half]) | (_bf16_bits(h[:, half:]) << 16)
    h_hi = h.astype(BF16)
    h_lo = (h - h_hi.astype(F32)).astype(BF16)
    logits = (jnp.dot(h_hi, w_ref[0], preferred_element_type=F32)
              + (jnp.dot(h_hi, w_ref[1], preferred_element_type=F32)
                 + jnp.dot(h_lo, w_ref[0], preferred_element_type=F32)))
    lane = lax.broadcasted_iota(jnp.int32, logits.shape, 1).astype(F32)
    logits = jnp.where(lane < N_EXPERTS, logits, -jnp.inf)
    m1 = jnp.max(logits, axis=-1, keepdims=True)
    i1 = jnp.min(jnp.where(logits == m1, lane, float(ROUTER_LANES)), axis=-1, keepdims=True)
    rest = jnp.where(lane == i1, -jnp.inf, logits)
    m2 = jnp.max(rest, axis=-1, keepdims=True)
    i2 = jnp.min(jnp.where(rest == m2, lane, float(ROUTER_LANES)), axis=-1, keepdims=True)
    e2 = jnp.exp(m2 - m1)
    den = 1.0 + e2
    gate_ref[...] = jnp.where(lane == 0, 1.0 / den, e2 / den)
    chosen = jnp.where((lane == i1) | (lane == i2), 1.0, 0.0)
    earlier = lax.broadcasted_iota(jnp.int32, (tm, tm), 0) > lax.broadcasted_iota(jnp.int32, (tm, tm), 1)
    carry = carry_ref[...]
    before = jnp.dot(jnp.where(earlier, 1.0, 0.0).astype(BF16), chosen.astype(BF16),
                     preferred_element_type=F32) + carry
    r1 = jnp.sum(jnp.where(lane == i1, before, 0.0), axis=-1, keepdims=True)
    r2 = jnp.sum(jnp.where(lane == i2, before, 0.0), axis=-1, keepdims=True)
    idx_ref[...] = jnp.where(lane == 0, i1, jnp.where(lane == 1, i2, jnp.where(lane == 2, r1, r2))).astype(jnp.int32)
    carry = carry + jnp.sum(chosen, axis=0, keepdims=True)
    carry_ref[...] = carry
    count_ref[...] = carry.astype(jnp.int32)


def _router(x2, gain, router):
    rows, d = x2.shape
    tm = MOE_CHUNK
    w = jnp.zeros((d, ROUTER_LANES), F32).at[:, :N_EXPERTS].set(router.astype(F32))
    w_hi = w.astype(BF16)
    w = jnp.stack([w_hi, (w - w_hi.astype(F32)).astype(BF16)])
    row_spec = lambda width: pl.BlockSpec((tm, width), lambda i: (i, 0))
    return pl.pallas_call(
        _router_kernel,
        out_shape=(jax.ShapeDtypeStruct((rows, d // 2), jnp.uint32),
                   jax.ShapeDtypeStruct((rows, ROUTER_LANES), jnp.int32),
                   jax.ShapeDtypeStruct((rows, ROUTER_LANES), F32),
                   jax.ShapeDtypeStruct((1, ROUTER_LANES), jnp.int32)),
        grid=(rows // tm,),
        in_specs=[row_spec(d),
                  pl.BlockSpec((1, d), lambda i: (0, 0)),
                  pl.BlockSpec((2, d, ROUTER_LANES), lambda i: (0, 0, 0))],
        out_specs=(row_spec(d // 2), row_spec(ROUTER_LANES), row_spec(ROUTER_LANES),
                   pl.BlockSpec((1, ROUTER_LANES), lambda i: (0, 0))),
        scratch_shapes=[pltpu.VMEM((1, ROUTER_LANES), F32)],
        compiler_params=_params("arbitrary"),
        name="router",
    )(x2, gain.reshape(1, d), w)


def _sc_workers():
    info = plsc.get_sparse_core_info()
    mesh = plsc.VectorSubcoreMesh(core_axis_name="core", subcore_axis_name="subcore")
    return mesh, info.num_cores, info.num_subcores


def _pack_weight_rows(w, col_tile):
    r, c = w.shape
    mesh, nc, ns = _sc_workers()
    rs, ncol = SC_PACK_ROWS, c // col_tile
    per_w = (r // rs) * ncol // (nc * ns)
    assert per_w * nc * ns * rs * col_tile == r * c and per_w % 2 == 0 and col_tile % SC_LANES == 0
    params = pltpu.CompilerParams()
    if "needs_layout_passes" in pltpu.CompilerParams.__dataclass_fields__:
        params = dataclasses.replace(params, needs_layout_passes=False)

    @functools.partial(
        pl.kernel, mesh=mesh, out_type=jax.ShapeDtypeStruct((r // 2, c), jnp.uint32), compiler_params=params,
        cost_estimate=pl.CostEstimate(flops=r * c, transcendentals=0, bytes_accessed=6 * r * c),
        scratch_types=[pltpu.VMEM((2, rs, col_tile), F32), pltpu.VMEM((2, rs // 2, col_tile), jnp.uint32),
                       pltpu.SemaphoreType.DMA((2,)), pltpu.SemaphoreType.DMA((2,))])
    def pack(w_hbm, o_hbm, in_v, out_v, rsem, wsem):
        wid = lax.axis_index("subcore") * nc + lax.axis_index("core")

        @pl.loop(0, per_w // 2)
        def _(it):
            reads, writes = [], []
            for b in range(2):
                tile = wid * per_w + 2 * it + b
                r0 = pl.multiple_of((tile // ncol) * rs, rs)
                c0 = pl.multiple_of((tile % ncol) * col_tile, col_tile)
                reads.append(pltpu.make_async_copy(w_hbm.at[pl.ds(r0, rs), pl.ds(c0, col_tile)], in_v.at[b],
                                                   rsem.at[b]))
                writes.append(pltpu.make_async_copy(
                    out_v.at[b], o_hbm.at[pl.ds(pl.multiple_of(r0 // 2, rs // 2), rs // 2), pl.ds(c0, col_tile)],
                    wsem.at[b]))
            reads[0].start()
            reads[1].start()
            for b in range(2):
                reads[b].wait()
                for pair in range(rs // 2):
                    @plsc.parallel_loop(0, col_tile, step=SC_LANES, unroll=8)
                    def _(j):
                        packed = plsc.pack(in_v[b, 2 * pair, pl.ds(j, SC_LANES)],
                                           in_v[b, 2 * pair + 1, pl.ds(j, SC_LANES)],
                                           format=plsc.PackFormat.INTERLEAVED)
                        out_v[b, pair, pl.ds(j, SC_LANES)] = plsc.bitcast(packed, jnp.uint32)
                writes[b].start()
            writes[0].wait()
            writes[1].wait()

    return pack(w)


def _sc_token_rows(t, nc, ns, ch):
    per_w = t // (nc * ns)
    assert per_w * nc * ns == t and per_w % (2 * ch) == 0
    return per_w, per_w // ch


def _scatter_rows(table, dests, p_rows, after=()):
    t, w = table.shape
    nk = len(dests)
    mesh, nc, ns = _sc_workers()
    ch = SC_SCATTER_ROWS
    per_w, nit = _sc_token_rows(t, nc, ns, ch)

    @functools.partial(
        pl.kernel, mesh=mesh, out_type=jax.ShapeDtypeStruct((p_rows, w), table.dtype),
        scratch_types=[pltpu.VMEM((nk, nit, ch), jnp.int32), pltpu.VMEM((2, ch, w), table.dtype),
                       pltpu.SemaphoreType.DMA((2,)), pltpu.SemaphoreType.DMA((2, nk))])
    def scatter(table_hbm, *refs):
        dest_hbm, out_hbm = refs[:nk], refs[nk + len(after)]
        idx_v, rows_v, rsem, wsem = refs[nk + len(after) + 1:]
        wid = lax.axis_index("subcore") * nc + lax.axis_index("core")
        for k in range(nk):
            pltpu.sync_copy(dest_hbm[k].at[pl.ds(wid * nit, nit)], idx_v.at[k])

        @pl.loop(0, nit // 2)
        def _(it):
            reads = [pltpu.make_async_copy(table_hbm.at[pl.ds(wid * per_w + (2 * it + b) * ch, ch)],
                                           rows_v.at[b], rsem.at[b]) for b in range(2)]
            writes = [[pltpu.make_async_copy(rows_v.at[b], out_hbm.at[idx_v.at[k].at[2 * it + b]], wsem.at[b, k])
                       for k in range(nk)] for b in range(2)]
            reads[0].start()
            reads[1].start()
            for b in range(2):
                reads[b].wait()
                for k in range(nk):
                    writes[b][k].start()
            for b in range(2):
                for k in range(nk):
                    writes[b][k].wait()

    return scatter(table, *[d.reshape(t // ch, ch) for d in dests], *after)


def _gather_rows(table, idxs):
    t = idxs[0].shape[0]
    w = table.shape[1]
    nk = len(idxs)
    mesh, nc, ns = _sc_workers()
    ch = SC_ROWS
    per_w, nit = _sc_token_rows(t, nc, ns, ch)

    @functools.partial(
        pl.kernel, mesh=mesh, out_type=[jax.ShapeDtypeStruct((t, w), table.dtype)] * nk,
        scratch_types=[pltpu.VMEM((nk, nit, ch), jnp.int32), pltpu.VMEM((nk, 2, ch, w), table.dtype),
                       pltpu.SemaphoreType.DMA((nk, 2)), pltpu.SemaphoreType.DMA((nk, 2))])
    def gather(table_hbm, *refs):
        idx_hbm, out_hbm = refs[:nk], refs[nk:2 * nk]
        idx_v, rows_v, rsem, wsem = refs[2 * nk:]
        wid = lax.axis_index("subcore") * nc + lax.axis_index("core")
        for k in range(nk):
            pltpu.sync_copy(idx_hbm[k].at[pl.ds(wid * nit, nit)], idx_v.at[k])

        @pl.loop(0, nit // 2)
        def _(it):
            slots = [(k, b) for k in range(nk) for b in range(2)]
            reads = {(k, b): pltpu.make_async_copy(table_hbm.at[idx_v.at[k].at[2 * it + b]], rows_v.at[k, b],
                                                   rsem.at[k, b]) for k, b in slots}
            writes = {(k, b): pltpu.make_async_copy(rows_v.at[k, b],
                                                    out_hbm[k].at[pl.ds(wid * per_w + (2 * it + b) * ch, ch)],
                                                    wsem.at[k, b]) for k, b in slots}
            for s in slots:
                reads[s].start()
            for s in slots:
                reads[s].wait()
                writes[s].start()
            for s in slots:
                writes[s].wait()

    return gather(table, *[i.reshape(t // ch, ch) for i in idxs])


def _combine_kernel(x_ref, y0_ref, y1_ref, gate_ref, *rest):
    o_ref = rest[-1]
    half = x_ref.shape[1] // 2
    for part in range(2):
        cols = slice(part * half, (part + 1) * half)
        acc = x_ref[:, cols]
        for k, y_ref in enumerate((y0_ref, y1_ref)):
            word = y_ref[...]
            bits = (word << 16) if part == 0 else (word & jnp.uint32(0xFFFF0000))
            acc = acc + gate_ref[:, k:k + 1] * lax.bitcast_convert_type(bits, F32)
        o_ref[:, cols] = acc


def _combine(x2, y0, y1, gates, part, earlier=None):
    t, d = x2.shape
    tm = ROW_TILE
    steps = y0.shape[0] // tm
    here = lambda i: (part * steps + i, 0)
    in_specs = [pl.BlockSpec((tm, d), here),
                pl.BlockSpec((tm, d // 2), lambda i: (i, 0)),
                pl.BlockSpec((tm, d // 2), lambda i: (i, 0)),
                pl.BlockSpec((tm, TOP_K), here)]
    operands = [x2, y0, y1, gates]
    if earlier is not None:
        in_specs.append(pl.BlockSpec(memory_space=pl.ANY))
        operands.append(earlier)
    return pl.pallas_call(
        _combine_kernel,
        out_shape=jax.ShapeDtypeStruct((t, d), F32),
        grid=(steps,),
        in_specs=in_specs,
        out_specs=pl.BlockSpec((tm, d), here),
        input_output_aliases={} if earlier is None else {len(operands) - 1: 0},
        compiler_params=_params("parallel"),
        name="moe_combine",
    )(*operands)


def _moe(x2, gain, router, wg, wu, wd):
    t, d = x2.shape
    hp, idx, gate, count = _router(x2, gain, router)
    experts = jnp.arange(N_EXPERTS, dtype=jnp.int32)
    counts = count[0, :N_EXPERTS]
    padded = (counts + MOE_ROWS - 1) // MOE_ROWS * MOE_ROWS
    pend = jnp.cumsum(padded)
    pstart = pend - padded
    expert, rank = idx[:, :TOP_K], idx[:, TOP_K:2 * TOP_K]
    dest = jnp.sum(jnp.where(expert[:, :, None] == experts, pstart, 0), axis=-1) + rank
    dests = [dest[:, k] for k in range(TOP_K)]
    p_rows = t * TOP_K + N_EXPERTS * MOE_ROWS
    blk_row = jnp.arange(p_rows // MOE_ROWS, dtype=jnp.int32) * MOE_ROWS
    blk_expert = jnp.minimum(jnp.sum(pend[None, :] <= blk_row[:, None], axis=1), N_EXPERTS - 1).astype(jnp.int32)
    valid = jnp.clip(counts[blk_expert] - (blk_row - pstart[blk_expert]), 0, MOE_ROWS)
    valid = jnp.where(blk_row < pend[-1], valid, 0).astype(jnp.int32)
    n_exp, _, f = wg.shape
    pack = lambda w, col_tile: _pack_weight_rows(w.reshape(-1, w.shape[2]), col_tile).reshape(n_exp, -1, w.shape[2])
    wg, wu, wd = pack(wg, f // 2), pack(wu, f // 2), pack(wd, d)
    hb = _scatter_rows(hp, dests, p_rows, after=(wg, wu, wd))
    yb = _ffn_experts(hb, wg, wu, wd, blk_expert, valid)
    out = None
    for part in range(MOE_TAIL_PARTS):
        rows = slice(part * t // MOE_TAIL_PARTS, (part + 1) * t // MOE_TAIL_PARTS)
        y0, y1 = _gather_rows(yb, [dk[rows] for dk in dests])
        out = _combine(x2, y0, y1, gate[:, :TOP_K], part, out)
    return out


def kernel(x, mem, rel_bias, norm_mix, norm_mem, norm_ffn, w_mem_kv, xq_norm, xk_norm, w_out, hy_w_in, hy_conv_w, hy_conv_b, hy_filt_w1, hy_filt_b1, hy_filt_w2, hy_filt_b2, hy_filt_w3, hy_sin_freq, hy_skip, at_w_in, at_q_norm, at_k_norm, ffn_w_gate, ffn_w_up, ffn_w_down, moe_router, moe_w_gate, moe_w_up, moe_w_down):
    b, s, d = x.shape
    t = b * s
    m_len = mem.shape[1]
    x2 = x.reshape(t, d)
    mem2 = mem.reshape(b * m_len, d)
    bf = lambda w: w.astype(BF16)
    score_scale = HD ** -0.5
    xa_heads = XA_W // HD
    at_heads = AT_W // HD
    plain = lambda width: [None] * (width // HD)
    kv_gains = lambda i: [xk_norm[i]] * xa_heads + plain(XA_W)

    tables = _dft_tables(s)
    spectra, hq = _hyena_filters(s, tables, hy_filt_w1[0], hy_filt_b1[0], hy_filt_w2[0], hy_filt_b2[0],
                                 hy_filt_w3[0], hy_sin_freq[0])
    z = _norm_matmul(x2, norm_mix[0], bf(hy_w_in[0]), 1024,
                     plain(3 * HY_C) + [xq_norm[0] * score_scale] * xa_heads).reshape(b, s, -1)
    kv = _norm_matmul(mem2, norm_mem[0], bf(w_mem_kv[0]), 1024, kv_gains(0)).reshape(b, m_len, -1)
    self_out = _hyena(z, hy_conv_w[0], hy_conv_b[0], tables, spectra, hq, hy_skip[0])
    x2 = _mix_out(x2.reshape(b, s, d), self_out, z, 3 * HY_C // XA_W, kv, bf(w_out[0])).reshape(t, d)
    x2 = _ffn_dense(x2, norm_ffn[0], bf(ffn_w_gate[0]), bf(ffn_w_up[0]), bf(ffn_w_down[0]))

    at_gains = ([at_q_norm[0] * score_scale] * at_heads + [at_k_norm[0]] * at_heads + plain(AT_W)
                + [xq_norm[1] * score_scale] * xa_heads)
    z = _norm_matmul(x2, norm_mix[1], bf(at_w_in[0]), 1024, at_gains).reshape(b, s, -1)
    kv = _norm_matmul(mem2, norm_mem[1], bf(w_mem_kv[1]), 1024, kv_gains(1)).reshape(b, m_len, -1)
    self_out = _dilated_attention(z, rel_bias)
    x2 = _mix_out(x2.reshape(b, s, d), self_out, z, 3 * AT_W // XA_W, kv, bf(w_out[1])).reshape(t, d)
    x2 = _moe(x2, norm_ffn[1], moe_router[0], moe_w_gate[0], moe_w_up[0], moe_w_down[0])
    return x2.reshape(b, s, d)
```

```python
import dataclasses
import functools
import math

import jax
import jax.numpy as jnp
import numpy as np
from jax import lax
from jax.experimental import pallas as pl
from jax.experimental.pallas import tpu as pltpu
from jax.experimental.pallas import tpu_sc as plsc

F32 = jnp.float32
BF16 = jnp.bfloat16

EPS = 1e-6
HY_C = 512
FILT_BANDS = 16
DECAY_TARGET = 1e-2
FAST_DECAY_PCT = 0.3
SLOW_DECAY_PCT = 1.5
MOD_SHIFT = 0.05
AT_GROUPS = ((128, 1), (512, 4), (2048, 16))
AT_HEADS = 4
HD = 128
AT_W = 1536
NUM_BUCKETS = 32
REL_MAX_DIST = 1024
NEG_INF = -1e30
XA_W = 512
N_EXPERTS = 8
TOP_K = 2

VMEM_LIMIT_BYTES = 56 * 1024 * 1024
ROW_TILE = 1024
PROJ_ROW_TILE = 1024
FF_TILE = 256
MOE_ROWS = 512
MOE_CHUNK = 512
MOE_TAIL_PARTS = 4
SC_ROWS = 32
SC_SCATTER_ROWS = 64
SC_LANES = 16
SC_PACK_ROWS = 16
ATT_TQ = 128
ATT_KW = 256
ATT_BATCH = 8
HY_TC = 256
HY_KT = 1024


def _params(*sem):
    return pltpu.CompilerParams(dimension_semantics=sem, vmem_limit_bytes=VMEM_LIMIT_BYTES)


def _rms_rows(x, gain):
    return x * lax.rsqrt(jnp.mean(x * x, axis=-1, keepdims=True) + EPS) * gain


def _bf16_bits(v):
    u = lax.bitcast_convert_type(v, jnp.uint32)
    return (u + jnp.uint32(0x7FFF) + ((u >> 16) & jnp.uint32(1))) >> 16


def _resident(shape, index_map):
    return pl.BlockSpec(shape, index_map, pipeline_mode=pl.Buffered(1))


def _norm_matmul_kernel(x_ref, g_ref, w_ref, hg_ref, o_ref, *, tn, head_norm):
    h = _rms_rows(x_ref[...], g_ref[...]).astype(BF16)
    for c in range(o_ref.shape[1] // tn):
        acc = jnp.dot(h, w_ref[:, c * tn:(c + 1) * tn], preferred_element_type=F32)
        for j in range(tn // HD):
            cols = slice(c * tn + j * HD, c * tn + (j + 1) * HD)
            seg = acc[:, j * HD:(j + 1) * HD]
            if head_norm[cols.start // HD]:
                seg = _rms_rows(seg, hg_ref[:, cols])
            o_ref[:, cols] = seg.astype(o_ref.dtype)


def _norm_matmul(x2, gain, w_bf, tn, head_gains):
    rows, d = x2.shape
    n = w_bf.shape[1]
    tm = min(PROJ_ROW_TILE, rows)
    assert len(head_gains) * HD == n
    hg = jnp.concatenate([jnp.ones((HD,), F32) if g is None else g.astype(F32) for g in head_gains]).reshape(1, n)
    return pl.pallas_call(
        functools.partial(_norm_matmul_kernel, tn=tn, head_norm=tuple(g is not None for g in head_gains)),
        out_shape=jax.ShapeDtypeStruct((rows, n), BF16),
        grid=(rows // tm,),
        in_specs=[pl.BlockSpec((tm, d), lambda i: (i, 0)),
                  pl.BlockSpec((1, d), lambda i: (0, 0)),
                  _resident((d, n), lambda i: (0, 0)),
                  pl.BlockSpec((1, n), lambda i: (0, 0))],
        out_specs=pl.BlockSpec((tm, n), lambda i: (i, 0)),
        compiler_params=_params("parallel"),
        cost_estimate=pl.CostEstimate(flops=2 * rows * d * n, transcendentals=rows,
                                      bytes_accessed=4 * rows * d + 2 * d * n + 2 * rows * n),
        name="norm_matmul",
    )(x2, gain.reshape(1, d), w_bf, hg)


def _mix_out_kernel(x_ref, a_ref, xq_ref, kv_ref, wa_ref, wc_ref, o_ref):
    heads = []
    for h in range(XA_W // HD):
        cols = slice(h * HD, (h + 1) * HD)
        s = lax.dot_general(xq_ref[0, :, cols], kv_ref[0, :, cols], (((1,), (1,)), ((), ())),
                            preferred_element_type=F32)
        p = jnp.exp(s - jnp.max(s, axis=-1, keepdims=True))
        z = jnp.sum(p, axis=-1, keepdims=True)
        pv = jnp.dot(p.astype(BF16), kv_ref[0, :, XA_W + h * HD:XA_W + (h + 1) * HD], preferred_element_type=F32)
        heads.append((pv / z).astype(BF16))
    cross = jnp.concatenate(heads, axis=1)
    o_ref[0] = (x_ref[0]
                + jnp.dot(a_ref[0], wa_ref[...], preferred_element_type=F32)
                + jnp.dot(cross, wc_ref[...], preferred_element_type=F32))


def _mix_out(x3, self_out, z3, xq_block, kv3, w_bf):
    b, s, d = x3.shape
    half = self_out.shape[2]
    m = kv3.shape[1]
    tm = ROW_TILE
    return pl.pallas_call(
        _mix_out_kernel,
        out_shape=jax.ShapeDtypeStruct((b, s, d), F32),
        grid=(b, s // tm),
        in_specs=[pl.BlockSpec((1, tm, d), lambda i, j: (i, j, 0)),
                  pl.BlockSpec((1, tm, half), lambda i, j: (i, j, 0)),
                  pl.BlockSpec((1, tm, XA_W), lambda i, j: (i, j, xq_block)),
                  pl.BlockSpec((1, m, 2 * XA_W), lambda i, j: (i, 0, 0)),
                  pl.BlockSpec((half, d), lambda i, j: (0, 0)),
                  pl.BlockSpec((d - half, d), lambda i, j: (1, 0))],
        out_specs=pl.BlockSpec((1, tm, d), lambda i, j: (i, j, 0)),
        compiler_params=_params("parallel", "parallel"),
        name="mix_out",
    )(x3, self_out, z3, kv3, w_bf, w_bf)


def _weight_rows(ref, start, size):
    if ref.dtype == jnp.uint32:
        return lambda cols: pltpu.bitcast(ref[start // 2:(start + size) // 2, cols], BF16)
    return lambda cols: ref[start:start + size, cols]


def _weight_shape(ref):
    return (ref.shape[0] * (2 if ref.dtype == jnp.uint32 else 1), ref.shape[1])


def _swiglu(h, wg_ref, wu_ref, wd_ref, tf):
    d, f = _weight_shape(wg_ref)
    assert f % tf == 0
    y = None
    for j in range(f // tf):
        cols = slice(j * tf, (j + 1) * tf)
        gg = jnp.dot(h, _weight_rows(wg_ref, 0, d)(cols), preferred_element_type=F32)
        uu = jnp.dot(h, _weight_rows(wu_ref, 0, d)(cols), preferred_element_type=F32)
        a = ((gg * jax.nn.sigmoid(gg)) * uu).astype(BF16)
        part = jnp.dot(a, _weight_rows(wd_ref, j * tf, tf)(slice(None)), preferred_element_type=F32)
        y = part if y is None else y + part
    return y


def _ffn_dense_kernel(x_ref, g_ref, wg_ref, wu_ref, wd_ref, o_ref, *, tf):
    x = x_ref[...]
    h = _rms_rows(x, g_ref[...]).astype(BF16)
    o_ref[...] = x + _swiglu(h, wg_ref, wu_ref, wd_ref, tf)


def _ffn_dense(x2, gain, wg, wu, wd):
    rows, d = x2.shape
    f = wg.shape[1]
    tm = ROW_TILE
    return pl.pallas_call(
        functools.partial(_ffn_dense_kernel, tf=FF_TILE),
        out_shape=jax.ShapeDtypeStruct((rows, d), F32),
        grid=(rows // tm,),
        in_specs=[pl.BlockSpec((tm, d), lambda i: (i, 0)),
                  pl.BlockSpec((1, d), lambda i: (0, 0)),
                  _resident((d, f), lambda i: (0, 0)),
                  _resident((d, f), lambda i: (0, 0)),
                  _resident((f, d), lambda i: (0, 0))],
        out_specs=pl.BlockSpec((tm, d), lambda i: (i, 0)),
        compiler_params=_params("parallel"),
        cost_estimate=pl.CostEstimate(flops=6 * rows * d * f, transcendentals=rows * f,
                                      bytes_accessed=8 * rows * d + 6 * d * f),
        name="ffn_dense",
    )(x2, gain.reshape(1, d), wg, wu, wd)


def _ffn_expert_kernel(eid_ref, valid_ref, hp_ref, wg_ref, wu_ref, wd_ref, o_ref, h_ref, *, tf):
    half = hp_ref.shape[1]
    valid = valid_ref[pl.program_id(0)]

    @pl.when(valid > 0)
    def _():
        keep = lax.broadcasted_iota(jnp.int32, (hp_ref.shape[0], 1), 0) < valid
        word = hp_ref[...]
        h_ref[:, :half] = jnp.where(keep, lax.bitcast_convert_type(word << 16, F32), 0.0).astype(BF16)
        h_ref[:, half:] = jnp.where(keep, lax.bitcast_convert_type(word & jnp.uint32(0xFFFF0000), F32),
                                    0.0).astype(BF16)
        y = _swiglu(h_ref[...], wg_ref.at[0], wu_ref.at[0], wd_ref.at[0], tf)
        o_ref[...] = _bf16_bits(y[:, :half]) | (_bf16_bits(y[:, half:]) << 16)

    @pl.when(valid <= 0)
    def _():
        o_ref[...] = jnp.zeros_like(o_ref)


def _ffn_experts(hp, wg, wu, wd, eid, valid):
    rows, half = hp.shape
    d = 2 * half
    tm = MOE_ROWS
    expert_spec = lambda w: pl.BlockSpec((1,) + w.shape[1:], lambda i, e, n: (e[i], 0, 0))
    grid_spec = pltpu.PrefetchScalarGridSpec(
        num_scalar_prefetch=2,
        grid=(rows // tm,),
        in_specs=[pl.BlockSpec((tm, half), lambda i, e, n: (i, 0)),
                  expert_spec(wg), expert_spec(wu), expert_spec(wd)],
        out_specs=pl.BlockSpec((tm, half), lambda i, e, n: (i, 0)),
        scratch_shapes=[pltpu.VMEM((tm, d), BF16)],
    )
    return pl.pallas_call(
        functools.partial(_ffn_expert_kernel, tf=FF_TILE),
        out_shape=jax.ShapeDtypeStruct((rows, half), jnp.uint32),
        grid_spec=grid_spec,
        compiler_params=_params("arbitrary"),
        name="ffn_experts",
    )(eid, valid, hp, wg, wu, wd)


def _dft_kernel(ce_ref, se_ref, co_ref, so_ref, cot_ref, sot_ref, base_ref, *, n_fft):
    rows, cols = ce_ref.shape
    i = pl.program_id(0)
    theta = 2.0 * math.pi / n_fft

    def phases(r, c):
        return r * (2 * c), r * (2 * c + 1), c * (2 * r + 1)

    @pl.when(i == 0)
    def _():
        r = lax.broadcasted_iota(jnp.int32, (rows, cols), 0)
        c = lax.broadcasted_iota(jnp.int32, (rows, cols), 1)
        for f, ph in enumerate(phases(r, c)):
            ang = (ph & (n_fft - 1)).astype(F32) * theta
            base_ref[2 * f] = jnp.cos(ang)
            base_ref[2 * f + 1] = jnp.sin(ang)

    c = lax.broadcasted_iota(jnp.int32, (8, cols), 1)
    r0 = i * rows
    shifts = (r0 * (2 * c), r0 * (2 * c + 1), c * (2 * r0))
    for f, (c_ref, s_ref) in enumerate(((ce_ref, se_ref), (co_ref, so_ref), (cot_ref, sot_ref))):
        ang = (shifts[f] & (n_fft - 1)).astype(F32) * theta
        ca, sa = jnp.cos(ang)[0:1], jnp.sin(ang)[0:1]
        cb, sb = base_ref[2 * f], base_ref[2 * f + 1]
        c_ref[...] = (cb * ca - sb * sa).astype(BF16)
        s_ref[...] = (sb * ca + cb * sa).astype(BF16)


def _dft_tables(length):
    half, rows = length // 2, 128
    shape = jax.ShapeDtypeStruct((half, half), BF16)
    spec = pl.BlockSpec((rows, half), lambda i: (i, 0))
    return pl.pallas_call(
        functools.partial(_dft_kernel, n_fft=2 * length),
        out_shape=(shape,) * 6,
        grid=(half // rows,),
        out_specs=(spec,) * 6,
        scratch_shapes=[pltpu.VMEM((6, rows, half), F32)],
        compiler_params=_params("arbitrary"),
        name="dft_tables",
    )()


LANES = 128


def _split_parity(tmp_ref, x):
    half = x.shape[0] // 2
    for j in range(tmp_ref.shape[0]):
        tmp_ref[j] = x[:, j * LANES:(j + 1) * LANES]
    pick = lambda start: jnp.concatenate(
        [tmp_ref[j, pl.ds(start, half, stride=2), :] for j in range(tmp_ref.shape[0])], axis=1)
    return pick(0), pick(1)


def _merge_parity(tmp_ref, even, odd):
    half = even.shape[0]
    for j in range(tmp_ref.shape[0]):
        tmp_ref[j, pl.ds(0, half, stride=2), :] = even[:, j * LANES:(j + 1) * LANES]
        tmp_ref[j, pl.ds(1, half, stride=2), :] = odd[:, j * LANES:(j + 1) * LANES]
    return jnp.concatenate([tmp_ref[j] for j in range(tmp_ref.shape[0])], axis=1)


def _filter_time_kernel(feats_ref, w1_ref, b1_ref, w2_ref, b2_ref, fr_ref, w3_ref, t_ref, delta_ref,
                        g_ref, hq_ref, tmp_ref, h_ref):
    hp = lax.Precision.HIGHEST
    length = feats_ref.shape[0]
    half = length // 2

    @pl.when(pl.program_id(0) == 0)
    def _():
        fr = fr_ref[...]
        h1 = jnp.sin(fr * (jnp.dot(feats_ref[...], w1_ref[...], preferred_element_type=F32, precision=hp)
                           + b1_ref[...]))
        h_ref[...] = jnp.sin(fr * (jnp.dot(h1, w2_ref[...], preferred_element_type=F32, precision=hp) + b2_ref[...]))

    h = h_ref[...]
    mod =jnp.exp(-t_ref[...] * delta_ref[...]) + MOD_SHIFT
    row = lax.broadcasted_iota(jnp.int32, (length, 1), 0)
    alt = jnp.where(lax.broadcasted_iota(jnp.int32, (half, 1), 0) % 2 == 0, 1.0, -1.0).astype(F32)
    h_hi = h.astype(BF16)
    h_lo = (h - h_hi.astype(F32)).astype(BF16)

    def last_layer(w):
        w_hi = w.astype(BF16)
        w_lo = (w - w_hi.astype(F32)).astype(BF16)
        return (jnp.dot(h_hi, w_hi, preferred_element_type=F32)
                + (jnp.dot(h_hi, w_lo, preferred_element_type=F32) + jnp.dot(h_lo, w_hi, preferred_element_type=F32)))

    for o in range(2):
        fwd = last_layer(w3_ref[2 * o]) * mod
        bwd = last_layer(w3_ref[2 * o + 1]) * mod
        bwd = jnp.where(row == 0, 0.0, bwd)
        norm = (jnp.sum(jnp.abs(fwd), axis=0, keepdims=True)
                + jnp.sum(jnp.abs(bwd), axis=0, keepdims=True) + 1e-6)
        fwd = fwd / norm
        bwd = bwd / norm
        for part, g in enumerate((fwd + bwd, fwd - bwd)):
            even, odd = _split_parity(tmp_ref, g)
            g_ref[o, 2 * part] = even.astype(BF16)
            g_ref[o, 2 * part + 1] = odd.astype(BF16)
            mid = jnp.sum((even if part == 0 else odd) * alt, axis=0, keepdims=True) * (1.0 / length)
            hq_ref[o, part] = mid if part == 0 else -mid


def _filter_freq_kernel(ce_ref, se_ref, co_ref, so_ref, g_ref, h_ref):
    half = ce_ref.shape[0]
    row = lax.broadcasted_iota(jnp.int32, (half, 1), 0)
    scale = jnp.where(row == 0, 1.0, 2.0).astype(F32) * (1.0 / (4 * half))
    a = jnp.dot(ce_ref[...], g_ref[0, 0], preferred_element_type=F32)
    b = jnp.dot(co_ref[...], g_ref[0, 1], preferred_element_type=F32)
    c = jnp.dot(se_ref[...], g_ref[0, 2], preferred_element_type=F32)
    d = jnp.dot(so_ref[...], g_ref[0, 3], preferred_element_type=F32)
    h_ref[0, 0] = (a + b) * scale
    h_ref[0, 1] = -(c + d) * scale
    h_ref[0, 2] = (a - b) * scale
    h_ref[0, 3] = (c - d) * scale


def _hyena_filters(length, tables, w1, b1, w2, b2, w3, freq):
    t = np.linspace(0.0, 1.0, length, dtype=np.float32)[:, None]
    f = np.linspace(1e-4, FILT_BANDS - 1, FILT_BANDS, dtype=np.float32)[None]
    ang = np.float32(2.0 * math.pi / length) * np.arange(length, dtype=np.float32)[:, None] * f
    feats = np.concatenate([t, np.cos(ang), -np.sin(ang)], axis=-1).astype(np.float32)
    deltas = np.abs(np.linspace(math.log(DECAY_TARGET) / SLOW_DECAY_PCT,
                                math.log(DECAY_TARGET) / FAST_DECAY_PCT, HY_C, dtype=np.float32))[None]
    hid = w1.shape[1]
    w3r = w3.reshape(hid, 4, HY_C).transpose(1, 0, 2)
    tc = HY_TC
    full = lambda shape: pl.BlockSpec(shape, lambda c: (0,) * len(shape))
    half = length // 2
    g, hq = pl.pallas_call(
        _filter_time_kernel,
        out_shape=(jax.ShapeDtypeStruct((2, 4, half, HY_C), BF16),
                   jax.ShapeDtypeStruct((2, 2, 1, HY_C), F32)),
        grid=(HY_C // tc,),
        in_specs=[full(feats.shape), full(w1.shape), full((1, hid)), full(w2.shape), full((1, hid)),
                  full((1, hid)), pl.BlockSpec((4, hid, tc), lambda c: (0, 0, c)), full((length, 1)),
                  pl.BlockSpec((1, tc), lambda c: (0, c))],
        out_specs=(pl.BlockSpec((2, 4, half, tc), lambda c: (0, 0, 0, c)),
                   pl.BlockSpec((2, 2, 1, tc), lambda c: (0, 0, 0, c))),
        scratch_shapes=[pltpu.VMEM((tc // LANES, length, LANES), F32), pltpu.VMEM((length, hid), F32)],
        compiler_params=_params("arbitrary"),
        name="filter_time",
    )(feats, w1, b1.reshape(1, hid), w2, b2.reshape(1, hid), freq.reshape(1, hid), w3r, t, deltas)
    ce, se, co, so = tables[:4]
    table_spec = _resident((half, half), lambda o, c: (0, 0))
    spectra = pl.pallas_call(
        _filter_freq_kernel,
        out_shape=jax.ShapeDtypeStruct((2, 4, half, HY_C), F32),
        grid=(2, HY_C // tc),
        in_specs=[table_spec] * 4 + [pl.BlockSpec((1, 4, half, tc), lambda o, c: (o, 0, 0, c))],
        out_specs=pl.BlockSpec((1, 4, half, tc), lambda o, c: (o, 0, 0, c)),
        compiler_params=_params("parallel", "parallel"),
        name="filter_freq",
    )(ce, se, co, so, g)
    return spectra, hq


def _hyena_kernel(z0_ref, z1_ref, z2_ref, cw_ref, cb_ref, ce_ref, se_ref, co_ref, so_ref, cot_ref, sot_ref,
                  h_ref, hq_ref, skip_ref, o_ref, tmp_ref):
    length = z0_ref.shape[1]
    half = length // 2
    row = lax.broadcasted_iota(jnp.int32, (length, 1), 0)
    alt = jnp.where(lax.broadcasted_iota(jnp.int32, (half, 1), 0) % 2 == 0, 1.0, -1.0).astype(F32)

    def short_conv(z_ref, c):
        u = z_ref[0].astype(F32)
        prev = jnp.where(row == 0, 0.0, pltpu.roll(u, 1, axis=0))
        nxt = jnp.where(row == length - 1, 0.0, pltpu.roll(u, length - 1, axis=0))
        return cb_ref[c] + (prev * cw_ref[0, c] + u * cw_ref[1, c] + nxt * cw_ref[2, c])

    def long_conv(u, o):
        ue, uo = _split_parity(tmp_ref, u)
        ue_bf, uo_bf = ue.astype(BF16), uo.astype(BF16)
        a4 = jnp.sum(ue * alt, axis=0, keepdims=True)
        b4 = jnp.sum(uo * alt, axis=0, keepdims=True)
        hr4, hi4 = hq_ref[o, 0], hq_ref[o, 1]
        y_even = alt * (a4 * hr4 + b4 * hi4)
        y_odd = -(alt * (a4 * hi4 - b4 * hr4))
        for kt in range(half // HY_KT):
            ks = slice(kt * HY_KT, (kt + 1) * HY_KT)
            ae = jnp.dot(ce_ref[ks, :], ue_bf, preferred_element_type=F32)
            ao = jnp.dot(co_ref[ks, :], uo_bf, preferred_element_type=F32)
            be = jnp.dot(se_ref[ks, :], ue_bf, preferred_element_type=F32)
            bo = jnp.dot(so_ref[ks, :], uo_bf, preferred_element_type=F32)
            a_f, a_g, b_f, b_g = ae + ao, ae - ao, be + bo, bo - be
            hr_f, hi_f, hr_g, hi_g = (h_ref[o, j, ks, :] for j in range(4))
            re_f, im_f = a_f * hr_f + b_f * hi_f, a_f * hi_f - b_f * hr_f
            re_g, im_g = a_g * hr_g + b_g * hi_g, a_g * hi_g - b_g * hr_g
            y_even += (jnp.dot(ce_ref[:, ks], (re_f + re_g).astype(BF16), preferred_element_type=F32)
                       - jnp.dot(se_ref[:, ks], (im_f - im_g).astype(BF16), preferred_element_type=F32))
            y_odd += (jnp.dot(cot_ref[:, ks], (re_f - re_g).astype(BF16), preferred_element_type=F32)
                      - jnp.dot(sot_ref[:, ks], (im_f + im_g).astype(BF16), preferred_element_type=F32))
        return _merge_parity(tmp_ref, y_even, y_odd) + u * skip_ref[o]

    z = short_conv(z1_ref, 1) * long_conv(short_conv(z0_ref, 0), 0)
    o_ref[0] = (short_conv(z2_ref, 2) * long_conv(z, 1)).astype(o_ref.dtype)


def _hyena(z3, conv_w, conv_b, tables, spectra, hq, skip):
    b, s, _ = z3.shape
    half = s // 2
    tc = HY_TC
    nct = HY_C // tc
    cw = conv_w.reshape(3, 3, 1, HY_C)
    cb = conv_b.reshape(3, 1, HY_C)
    zspec = lambda chunk: pl.BlockSpec((1, s, tc), lambda c, i: (i, 0, chunk * nct + c))
    return pl.pallas_call(
        _hyena_kernel,
        out_shape=jax.ShapeDtypeStruct((b, s, HY_C), BF16),
        grid=(nct, b),
        in_specs=[zspec(0), zspec(1), zspec(2),
                  pl.BlockSpec((3, 3, 1, tc), lambda c, i: (0, 0, 0, c)),
                  pl.BlockSpec((3, 1, tc), lambda c, i: (0, 0, c))]
                 + [_resident((half, half), lambda c, i: (0, 0))] * 6
                 + [_resident((2, 4, half, tc), lambda c, i: (0, 0, 0, c)),
                    pl.BlockSpec((2, 2, 1, tc), lambda c, i: (0, 0, 0, c)),
                    pl.BlockSpec((2, 1, tc), lambda c, i: (0, 0, c))],
        out_specs=pl.BlockSpec((1, s, tc), lambda c, i: (i, 0, c)),
        scratch_shapes=[pltpu.VMEM((tc // LANES, s, LANES), F32)],
        compiler_params=_params("parallel", "parallel"),
        cost_estimate=pl.CostEstimate(flops=2 * 4 * 2 * b * s * half * HY_C, transcendentals=0,
                                      bytes_accessed=2 * 3 * b * s * HY_C + 4 * b * s * HY_C + 12 * half * half),
        name="hyena",
    )(z3, z3, z3, cw, cb, *tables, spectra, hq, skip.reshape(2, 1, HY_C))


def _rel_bucket(rel):
    half = NUM_BUCKETS // 2
    exact = half // 2
    n = np.abs(rel)
    large = exact + (np.log(np.maximum(n, 1) / exact) / np.log(REL_MAX_DIST / exact) * (half - exact)).astype(np.int32)
    large = np.minimum(large, half - 1)
    return (np.where(rel > 0, half, 0) + np.where(n < exact, n, large)).astype(np.int32)


def _att_tiles(length):
    kw = min(ATT_KW, length)
    tiles = []
    for qs in range(0, length, ATT_TQ):
        ks = min(max(qs - (kw - ATT_TQ) // 2, 0), length - kw)
        tiles.append((qs, ks, {0: 0, -64: 1, -128: 2}[ks - qs]))
    return kw, tiles


def _att_bias(rel_bias, group, seq):
    window, dil = AT_GROUPS[group]
    band = window // (2 * dil)
    kw, tiles = _att_tiles(seq // dil)
    offsets = sorted({ks - qs for qs, ks, _ in tiles}, reverse=True)
    table = rel_bias[:, group * AT_HEADS:(group + 1) * AT_HEADS].astype(F32)
    span = kw + ATT_TQ
    k = np.arange(span)
    rel = np.asarray(offsets)[:, None] + np.where(k < kw, k, k - span)[None, :]
    diag = jnp.where((np.abs(rel) <= band)[:, None, :],
                     jnp.swapaxes(table[_rel_bucket(rel * dil)], 1, 2), NEG_INF)
    return diag[:, :, None, :]


def _bias_tile(diag_ref, var, kw):
    rows = jnp.broadcast_to(diag_ref[var, 0], (ATT_TQ, diag_ref.shape[3]))
    return pltpu.roll(rows, 0, 1, stride=1, stride_axis=0)[:, :kw]


def _dilated_kernel(q1, q2, q3, k1, k2, k3, v1, v2, v3, b1, b2, b3, o_ref,
                    qn_ref, kn_ref, vn_ref, og_ref, lg_ref):
    seq = o_ref.shape[1]
    contract_last = (((1,), (1,)), ((), ()))
    for g, (q_ref, k_ref, v_ref, bias_ref) in enumerate(((q1, k1, v1, b1), (q2, k2, v2, b2), (q3, k3, v3, b3))):
        dil = AT_GROUPS[g][1]
        kw, tiles = _att_tiles(seq // dil)
        if dil > 1:
            qn_ref[...] = q_ref[0].astype(F32)
            kn_ref[...] = k_ref[0].astype(F32)
            vn_ref[...] = v_ref[0].astype(F32)
        bias = {var: _bias_tile(bias_ref, var, kw) for var in sorted({var for _, _, var in tiles})}
        rows = lambda r, start, size: (pl.ds(r + start * dil, size, stride=dil) if dil > 1 else pl.ds(start, size))
        work = [(r, qs, ks, var) for r in range(dil) for qs, ks, var in tiles]
        for first in range(0, len(work), ATT_BATCH):
            scores, values, places = [], [], []
            for r, qs, ks, var in work[first:first + ATT_BATCH]:
                if dil > 1:
                    qt = qn_ref[rows(r, qs, ATT_TQ), :].astype(BF16)
                    kt = kn_ref[rows(r, ks, kw), :].astype(BF16)
                    vt = vn_ref[rows(r, ks, kw), :].astype(BF16)
                else:
                    qt, kt, vt = (q_ref[0, rows(r, qs, ATT_TQ), :], k_ref[0, rows(r, ks, kw), :],
                                  v_ref[0, rows(r, ks, kw), :])
                scores.append(lax.dot_general(qt, kt, contract_last, preferred_element_type=F32) + bias[var])
                values.append(vt)
                places.append(rows(r, qs, ATT_TQ))
            softmax = []
            for s in scores:
                m = jnp.max(s, axis=-1, keepdims=True)
                p = jnp.exp(s - m)
                softmax.append((p.astype(BF16), jnp.sum(p, axis=-1, keepdims=True), m))
            for (p, z, m), vt, place in zip(softmax, values, places):
                og_ref[g, place, :] = jnp.dot(p, vt, preferred_element_type=F32) / z
                lg_ref[g, place, :] = jnp.broadcast_to(m + jnp.log(z), (ATT_TQ, HD))
    l0, l1, l2 = lg_ref[0], lg_ref[1], lg_ref[2]
    mx = jnp.maximum(jnp.maximum(l0, l1), l2)
    e0, e1, e2 = jnp.exp(l0 - mx), jnp.exp(l1 - mx), jnp.exp(l2 - mx)
    o_ref[0] = ((e0 * og_ref[0] + e1 * og_ref[1] + e2 * og_ref[2]) / (e0 + e1 + e2)).astype(o_ref.dtype)


def _dilated_attention(z3, rel_bias):
    b, s, _ = z3.shape
    ng = len(AT_GROUPS)
    col = lambda part, g: pl.BlockSpec((1, s, HD), lambda i, h: (i, 0, part * ng * AT_HEADS + g * AT_HEADS + h))
    biases = [_att_bias(rel_bias, g, s) for g in range(ng)]
    bias_spec = lambda a: pl.BlockSpec((a.shape[0], 1) + a.shape[2:], lambda i, h: (0, h, 0, 0))
    return pl.pallas_call(
        _dilated_kernel,
        out_shape=jax.ShapeDtypeStruct((b, s, AT_HEADS * HD), BF16),
        grid=(b, AT_HEADS),
        in_specs=[col(p, g) for p in range(3) for g in range(ng)] + [bias_spec(a) for a in biases],
        out_specs=pl.BlockSpec((1, s, HD), lambda i, h: (i, 0, h)),
        scratch_shapes=[pltpu.VMEM((s, HD), F32), pltpu.VMEM((s, HD), F32), pltpu.VMEM((s, HD), F32),
                        pltpu.VMEM((ng, s, HD), F32), pltpu.VMEM((ng, s, HD), F32)],
        compiler_params=_params("parallel", "parallel"),
        name="dilated_attention",
    )(*([z3] * 9), *biases)


ROUTER_LANES = 128


def _router_kernel(x_ref, g_ref, w_ref, hp_ref, idx_ref, gate_ref, count_ref, carry_ref):
    tm = x_ref.shape[0]
    half = hp_ref.shape[1]

    @pl.when(pl.program_id(0) == 0)
    def _():
        carry_ref[...] = jnp.zeros_like(carry_ref)

    h = _rms_rows(x_ref[...], g_ref[...])
    hp_ref[...] = _bf16_bits(h[:, :half]) | (_bf16_bits(h[:, half:]) << 16)
    h_hi = h.astype(BF16)
    h_lo = (h - h_hi.astype(F32)).astype(BF16)
    logits = (jnp.dot(h_hi, w_ref[0], preferred_element_type=F32)
              + (jnp.dot(h_hi, w_ref[1], preferred_element_type=F32)
                 + jnp.dot(h_lo, w_ref[0], preferred_element_type=F32)))
    lane = lax.broadcasted_iota(jnp.int32, logits.shape, 1).astype(F32)
    logits = jnp.where(lane < N_EXPERTS, logits, -jnp.inf)
    m1 = jnp.max(logits, axis=-1, keepdims=True)
    i1 = jnp.min(jnp.where(logits == m1, lane, float(ROUTER_LANES)), axis=-1, keepdims=True)
    rest = jnp.where(lane == i1, -jnp.inf, logits)
    m2 = jnp.max(rest, axis=-1, keepdims=True)
    i2 = jnp.min(jnp.where(rest == m2, lane, float(ROUTER_LANES)), axis=-1, keepdims=True)
    e2 = jnp.exp(m2 - m1)
    den = 1.0 + e2
    gate_ref[...] = jnp.where(lane == 0, 1.0 / den, e2 / den)
    chosen = jnp.where((lane == i1) | (lane == i2), 1.0, 0.0)
    earlier = lax.broadcasted_iota(jnp.int32, (tm, tm), 0) > lax.broadcasted_iota(jnp.int32, (tm, tm), 1)
    carry = carry_ref[...]
    before = jnp.dot(jnp.where(earlier, 1.0, 0.0).astype(BF16), chosen.astype(BF16),
                     preferred_element_type=F32) + carry
    r1 = jnp.sum(jnp.where(lane == i1, before, 0.0), axis=-1, keepdims=True)
    r2 = jnp.sum(jnp.where(lane == i2, before, 0.0), axis=-1, keepdims=True)
    idx_ref[...] = jnp.where(lane == 0, i1, jnp.where(lane == 1, i2, jnp.where(lane == 2, r1, r2))).astype(jnp.int32)
    carry = carry + jnp.sum(chosen, axis=0, keepdims=True)
    carry_ref[...] = carry
    count_ref[...] = carry.astype(jnp.int32)


def _router(x2, gain, router):
    rows, d = x2.shape
    tm = MOE_CHUNK
    w = jnp.zeros((d, ROUTER_LANES), F32).at[:, :N_EXPERTS].set(router.astype(F32))
    w_hi = w.astype(BF16)
    w = jnp.stack([w_hi, (w - w_hi.astype(F32)).astype(BF16)])
    row_spec = lambda width: pl.BlockSpec((tm, width), lambda i: (i, 0))
    return pl.pallas_call(
        _router_kernel,
        out_shape=(jax.ShapeDtypeStruct((rows, d // 2), jnp.uint32),
                   jax.ShapeDtypeStruct((rows, ROUTER_LANES), jnp.int32),
                   jax.ShapeDtypeStruct((rows, ROUTER_LANES), F32),
                   jax.ShapeDtypeStruct((1, ROUTER_LANES), jnp.int32)),
        grid=(rows // tm,),
        in_specs=[row_spec(d),
                  pl.BlockSpec((1, d), lambda i: (0, 0)),
                  pl.BlockSpec((2, d, ROUTER_LANES), lambda i: (0, 0, 0))],
        out_specs=(row_spec(d // 2), row_spec(ROUTER_LANES), row_spec(ROUTER_LANES),
                   pl.BlockSpec((1, ROUTER_LANES), lambda i: (0, 0))),
        scratch_shapes=[pltpu.VMEM((1, ROUTER_LANES), F32)],
        compiler_params=_params("arbitrary"),
        name="router",
    )(x2, gain.reshape(1, d), w)


def _sc_workers():
    info = plsc.get_sparse_core_info()
    mesh = plsc.VectorSubcoreMesh(core_axis_name="core", subcore_axis_name="subcore")
    return mesh, info.num_cores, info.num_subcores


def _pack_weight_rows(w, col_tile):
    r, c = w.shape
    mesh, nc, ns = _sc_workers()
    rs, ncol = SC_PACK_ROWS, c // col_tile
    per_w = (r // rs) * ncol // (nc * ns)
    assert per_w * nc * ns * rs * col_tile == r * c and per_w % 2 == 0 and col_tile % SC_LANES == 0
    params = pltpu.CompilerParams()
    if "needs_layout_passes" in pltpu.CompilerParams.__dataclass_fields__:
        params = dataclasses.replace(params, needs_layout_passes=False)

    @functools.partial(
        pl.kernel, mesh=mesh, out_type=jax.ShapeDtypeStruct((r // 2, c), jnp.uint32), compiler_params=params,
        cost_estimate=pl.CostEstimate(flops=r * c, transcendentals=0, bytes_accessed=6 * r * c),
        scratch_types=[pltpu.VMEM((2, rs, col_tile), F32), pltpu.VMEM((2, rs // 2, col_tile), jnp.uint32),
                       pltpu.SemaphoreType.DMA((2,)), pltpu.SemaphoreType.DMA((2,))])
    def pack(w_hbm, o_hbm, in_v, out_v, rsem, wsem):
        wid = lax.axis_index("subcore") * nc + lax.axis_index("core")

        @pl.loop(0, per_w // 2)
        def _(it):
            reads, writes = [], []
            for b in range(2):
                tile = wid * per_w + 2 * it + b
                r0 = pl.multiple_of((tile // ncol) * rs, rs)
                c0 = pl.multiple_of((tile % ncol) * col_tile, col_tile)
                reads.append(pltpu.make_async_copy(w_hbm.at[pl.ds(r0, rs), pl.ds(c0, col_tile)], in_v.at[b],
                                                   rsem.at[b]))
                writes.append(pltpu.make_async_copy(
                    out_v.at[b], o_hbm.at[pl.ds(pl.multiple_of(r0 // 2, rs // 2), rs // 2), pl.ds(c0, col_tile)],
                    wsem.at[b]))
            reads[0].start()
            reads[1].start()
            for b in range(2):
                reads[b].wait()
                for pair in range(rs // 2):
                    @plsc.parallel_loop(0, col_tile, step=SC_LANES, unroll=8)
                    def _(j):
                        packed = plsc.pack(in_v[b, 2 * pair, pl.ds(j, SC_LANES)],
                                           in_v[b, 2 * pair + 1, pl.ds(j, SC_LANES)],
                                           format=plsc.PackFormat.INTERLEAVED)
                        out_v[b, pair, pl.ds(j, SC_LANES)] = plsc.bitcast(packed, jnp.uint32)
                writes[b].start()
            writes[0].wait()
            writes[1].wait()

    return pack(w)


def _sc_token_rows(t, nc, ns, ch):
    per_w = t // (nc * ns)
    assert per_w * nc * ns == t and per_w % (2 * ch) == 0
    return per_w, per_w // ch


def _scatter_rows(table, dests, p_rows, after=()):
    t, w = table.shape
    nk = len(dests)
    mesh, nc, ns = _sc_workers()
    ch = SC_SCATTER_ROWS
    per_w, nit = _sc_token_rows(t, nc, ns, ch)

    @functools.partial(
        pl.kernel, mesh=mesh, out_type=jax.ShapeDtypeStruct((p_rows, w), table.dtype),
        scratch_types=[pltpu.VMEM((nk, nit, ch), jnp.int32), pltpu.VMEM((2, ch, w), table.dtype),
                       pltpu.SemaphoreType.DMA((2,)), pltpu.SemaphoreType.DMA((2, nk))])
    def scatter(table_hbm, *refs):
        dest_hbm, out_hbm = refs[:nk], refs[nk + len(after)]
        idx_v, rows_v, rsem, wsem = refs[nk + len(after) + 1:]
        wid = lax.axis_index("subcore") * nc + lax.axis_index("core")
        for k in range(nk):
            pltpu.sync_copy(dest_hbm[k].at[pl.ds(wid * nit, nit)], idx_v.at[k])

        @pl.loop(0, nit // 2)
        def _(it):
            reads = [pltpu.make_async_copy(table_hbm.at[pl.ds(wid * per_w + (2 * it + b) * ch, ch)],
                                           rows_v.at[b], rsem.at[b]) for b in range(2)]
            writes = [[pltpu.make_async_copy(rows_v.at[b], out_hbm.at[idx_v.at[k].at[2 * it + b]], wsem.at[b, k])
                       for k in range(nk)] for b in range(2)]
            reads[0].start()
            reads[1].start()
            for b in range(2):
                reads[b].wait()
                for k in range(nk):
                    writes[b][k].start()
            for b in range(2):
                for k in range(nk):
                    writes[b][k].wait()

    return scatter(table, *[d.reshape(t // ch, ch) for d in dests], *after)


def _gather_rows(table, idxs):
    t = idxs[0].shape[0]
    w = table.shape[1]
    nk = len(idxs)
    mesh, nc, ns = _sc_workers()
    ch = SC_ROWS
    per_w, nit = _sc_token_rows(t, nc, ns, ch)

    @functools.partial(
        pl.kernel, mesh=mesh, out_type=[jax.ShapeDtypeStruct((t, w), table.dtype)] * nk,
        scratch_types=[pltpu.VMEM((nk, nit, ch), jnp.int32), pltpu.VMEM((nk, 2, ch, w), table.dtype),
                       pltpu.SemaphoreType.DMA((nk, 2)), pltpu.SemaphoreType.DMA((nk, 2))])
    def gather(table_hbm, *refs):
        idx_hbm, out_hbm = refs[:nk], refs[nk:2 * nk]
        idx_v, rows_v, rsem, wsem = refs[2 * nk:]
        wid = lax.axis_index("subcore") * nc + lax.axis_index("core")
        for k in range(nk):
            pltpu.sync_copy(idx_hbm[k].at[pl.ds(wid * nit, nit)], idx_v.at[k])

        @pl.loop(0, nit // 2)
        def _(it):
            slots = [(k, b) for k in range(nk) for b in range(2)]
            reads = {(k, b): pltpu.make_async_copy(table_hbm.at[idx_v.at[k].at[2 * it + b]], rows_v.at[k, b],
                                                   rsem.at[k, b]) for k, b in slots}
            writes = {(k, b): pltpu.make_async_copy(rows_v.at[k, b],
                                                    out_hbm[k].at[pl.ds(wid * per_w + (2 * it + b) * ch, ch)],
                                                    wsem.at[k, b]) for k, b in slots}
            for s in slots:
                reads[s].start()
            for s in slots:
                reads[s].wait()
                writes[s].start()
            for s in slots:
                writes[s].wait()

    return gather(table, *[i.reshape(t // ch, ch) for i in idxs])


def _combine_kernel(x_ref, y0_ref, y1_ref, gate_ref, *rest):
    o_ref = rest[-1]
    half = x_ref.shape[1] // 2
    for part in range(2):
        cols = slice(part * half, (part + 1) * half)
        acc = x_ref[:, cols]
        for k, y_ref in enumerate((y0_ref, y1_ref)):
            word = y_ref[...]
            bits = (word << 16) if part == 0 else (word & jnp.uint32(0xFFFF0000))
            acc = acc + gate_ref[:, k:k + 1] * lax.bitcast_convert_type(bits, F32)
        o_ref[:, cols] = acc


def _combine(x2, y0, y1, gates, part, earlier=None):
    t, d = x2.shape
    tm = ROW_TILE
    steps = y0.shape[0] // tm
    here = lambda i: (part * steps + i, 0)
    in_specs = [pl.BlockSpec((tm, d), here),
                pl.BlockSpec((tm, d // 2), lambda i: (i, 0)),
                pl.BlockSpec((tm, d // 2), lambda i: (i, 0)),
                pl.BlockSpec((tm, TOP_K), here)]
    operands = [x2, y0, y1, gates]
    if earlier is not None:
        in_specs.append(pl.BlockSpec(memory_space=pl.ANY))
        operands.append(earlier)
    return pl.pallas_call(
        _combine_kernel,
        out_shape=jax.ShapeDtypeStruct((t, d), F32),
        grid=(steps,),
        in_specs=in_specs,
        out_specs=pl.BlockSpec((tm, d), here),
        input_output_aliases={} if earlier is None else {len(operands) - 1: 0},
        compiler_params=_params("parallel"),
        name="moe_combine",
    )(*operands)


def _moe(x2, gain, router, wg, wu, wd):
    t, d = x2.shape
    hp, idx, gate, count = _router(x2, gain, router)
    experts = jnp.arange(N_EXPERTS, dtype=jnp.int32)
    counts = count[0, :N_EXPERTS]
    padded = (counts + MOE_ROWS - 1) // MOE_ROWS * MOE_ROWS
    pend = jnp.cumsum(padded)
    pstart = pend - padded
    expert, rank = idx[:, :TOP_K], idx[:, TOP_K:2 * TOP_K]
    dest = jnp.sum(jnp.where(expert[:, :, None] == experts, pstart, 0), axis=-1) + rank
    dests = [dest[:, k] for k in range(TOP_K)]
    p_rows = t * TOP_K + N_EXPERTS * MOE_ROWS
    blk_row = jnp.arange(p_rows // MOE_ROWS, dtype=jnp.int32) * MOE_ROWS
    blk_expert = jnp.minimum(jnp.sum(pend[None, :] <= blk_row[:, None], axis=1), N_EXPERTS - 1).astype(jnp.int32)
    valid = jnp.clip(counts[blk_expert] - (blk_row - pstart[blk_expert]), 0, MOE_ROWS)
    valid = jnp.where(blk_row < pend[-1], valid, 0).astype(jnp.int32)
    n_exp, _, f = wg.shape
    pack = lambda w, col_tile: _pack_weight_rows(w.reshape(-1, w.shape[2]), col_tile).reshape(n_exp, -1, w.shape[2])
    wg, wu, wd = pack(wg, f // 2), pack(wu, f // 2), pack(wd, d)
    hb = _scatter_rows(hp, dests, p_rows, after=(wg, wu, wd))
    yb = _ffn_experts(hb, wg, wu, wd, blk_expert, valid)
    out = None
    for part in range(MOE_TAIL_PARTS):
        rows = slice(part * t // MOE_TAIL_PARTS, (part + 1) * t // MOE_TAIL_PARTS)
        y0, y1 = _gather_rows(yb, [dk[rows] for dk in dests])
        out = _combine(x2, y0, y1, gate[:, :TOP_K], part, out)
    return out


def kernel(x, mem, rel_bias, norm_mix, norm_mem, norm_ffn, w_mem_kv, xq_norm, xk_norm, w_out, hy_w_in, hy_conv_w, hy_conv_b, hy_filt_w1, hy_filt_b1, hy_filt_w2, hy_filt_b2, hy_filt_w3, hy_sin_freq, hy_skip, at_w_in, at_q_norm, at_k_norm, ffn_w_gate, ffn_w_up, ffn_w_down, moe_router, moe_w_gate, moe_w_up, moe_w_down):
    b, s, d = x.shape
    t = b * s
    m_len = mem.shape[1]
    x2 = x.reshape(t, d)
    mem2 = mem.reshape(b * m_len, d)
    bf = lambda w: w.astype(BF16)
    score_scale = HD ** -0.5
    xa_heads = XA_W // HD
    at_heads = AT_W // HD
    plain = lambda width: [None] * (width // HD)
    kv_gains = lambda i: [xk_norm[i]] * xa_heads + plain(XA_W)

    tables = _dft_tables(s)
    spectra, hq = _hyena_filters(s, tables, hy_filt_w1[0], hy_filt_b1[0], hy_filt_w2[0], hy_filt_b2[0],
                                 hy_filt_w3[0], hy_sin_freq[0])
    z = _norm_matmul(x2, norm_mix[0], bf(hy_w_in[0]), 1024,
                     plain(3 * HY_C) + [xq_norm[0] * score_scale] * xa_heads).reshape(b, s, -1)
    kv = _norm_matmul(mem2, norm_mem[0], bf(w_mem_kv[0]), 1024, kv_gains(0)).reshape(b, m_len, -1)
    self_out = _hyena(z, hy_conv_w[0], hy_conv_b[0], tables, spectra, hq, hy_skip[0])
    x2 = _mix_out(x2.reshape(b, s, d), self_out, z, 3 * HY_C // XA_W, kv, bf(w_out[0])).reshape(t, d)
    x2 = _ffn_dense(x2, norm_ffn[0], bf(ffn_w_gate[0]), bf(ffn_w_up[0]), bf(ffn_w_down[0]))

    at_gains = ([at_q_norm[0] * score_scale] * at_heads + [at_k_norm[0]] * at_heads + plain(AT_W)
                + [xq_norm[1] * score_scale] * xa_heads)
    z = _norm_matmul(x2, norm_mix[1], bf(at_w_in[0]), 1024, at_gains).reshape(b, s, -1)
    kv = _norm_matmul(mem2, norm_mem[1], bf(w_mem_kv[1]), 1024, kv_gains(1)).reshape(b, m_len, -1)
    self_out = _dilated_attention(z, rel_bias)
    x2 = _mix_out(x2.reshape(b, s, d), self_out, z, 3 * AT_W // XA_W, kv, bf(w_out[1])).reshape(t, d)
    x2 = _moe(x2, norm_ffn[1], moe_router[0], moe_w_gate[0], moe_w_up[0], moe_w_down[0])
    return x2.reshape(b, s, d)
```
